```python
import jax, jax.numpy as jnp
from jax import lax
import numpy as np

D_MODEL = 2048
BATCH = 8
SEQ = 2048
DEPTH = 1

MEM_LEN = 256
HEAD_DIM = 64
RWKV_WIDTH = D_MODEL // 2
RWKV_HEADS = RWKV_WIDTH // HEAD_DIM
DECAY_LORA = max(32, int(round(1.8 * RWKV_WIDTH ** 0.5 / 32)) * 32)
AAA_LORA = max(32, int(round(1.8 * RWKV_WIDTH ** 0.5 / 32)) * 32)
GATE_LORA = max(32, int(round(0.6 * RWKV_WIDTH ** 0.8 / 32)) * 32)
GN_EPS = 64e-5
SWA_WIDTH = D_MODEL - RWKV_WIDTH
SWA_Q_HEADS = SWA_WIDTH // HEAD_DIM
SWA_KV_HEADS = max(1, SWA_Q_HEADS // 8)
SWA_GROUP = SWA_Q_HEADS // SWA_KV_HEADS
WINDOW = 128
BLOCK = 128
ROPE_THETA = 10000.0
SHIFT_COLS = 3 * RWKV_WIDTH + DECAY_LORA + AAA_LORA + GATE_LORA
SWA_COLS = SWA_WIDTH + 2 * SWA_KV_HEADS * HEAD_DIM
IN_COLS = SHIFT_COLS + SWA_COLS
XATTN_HEADS = 4
XATTN_HEAD_DIM = D_MODEL // XATTN_HEADS
D_FF = ((8 * D_MODEL // 3 + 255) // 256) * 256
RMS_EPS = 1e-6
NEG_INF = -1e30

kernel_name = 'hymba_rwkv7_swa_sink_macaron_layer'


def _rmsnorm(x, g):
    xf = x.astype(jnp.float32)
    y = xf * lax.rsqrt(jnp.mean(xf * xf, axis=-1, keepdims=True) + RMS_EPS)
    return (y * g.astype(jnp.float32)).astype(x.dtype)


def _swiglu(h, w_gate, w_up, w_down):
    return (jax.nn.silu(h @ w_gate) * (h @ w_up)) @ w_down


def _rope(t, pos):
    hd = t.shape[-1]
    inv_freq = ROPE_THETA ** (-jnp.arange(0, hd, 2, dtype=jnp.float32) / hd)
    ang = pos[:, None] * inv_freq[None, :]
    cos = jnp.cos(ang)[None, :, None, :]
    sin = jnp.sin(ang)[None, :, None, :]
    tf = t.astype(jnp.float32)
    t1, t2 = tf[..., : hd // 2], tf[..., hd // 2:]
    return jnp.concatenate([t1 * cos - t2 * sin, t2 * cos + t1 * sin], axis=-1).astype(t.dtype)


def _rwkv7_time_mix(z, w0, decay_up, a0, aaa_up, gate_up, k_k, k_a, r_k, lnx_w, lnx_b):
    out_dtype = z.dtype
    f32 = jnp.float32
    z = z.astype(f32)
    B, T, _ = z.shape
    C, H, N = RWKV_WIDTH, RWKV_HEADS, HEAD_DIM
    cuts = [C, 2 * C, 3 * C, 3 * C + DECAY_LORA, 3 * C + DECAY_LORA + AAA_LORA]
    r, k, v, wd, ad, gd = jnp.split(z, cuts, axis=-1)
    w = -jax.nn.softplus(-(w0.astype(f32) + jnp.tanh(wd) @ decay_up.astype(f32))) - 0.5
    a = jax.nn.sigmoid(a0.astype(f32) + ad @ aaa_up.astype(f32))
    g = jax.nn.sigmoid(gd) @ gate_up.astype(f32)
    kk = (k * k_k.astype(f32)).reshape(B, T, H, N)
    kk = kk / jnp.maximum(jnp.sqrt(jnp.sum(kk * kk, axis=-1, keepdims=True)), 1e-12)
    k = k * (1.0 + (a - 1.0) * k_a.astype(f32))
    heads = lambda t: t.reshape(B, T, H, N)
    r_h, k_h, v_h, a_h = heads(r), heads(k), heads(v), heads(a)
    decay = jnp.exp(-jnp.exp(heads(w)))
    tm = lambda t: jnp.swapaxes(t, 0, 1)
    seq_in = (tm(r_h), tm(decay), tm(k_h), tm(v_h), tm(-kk), tm(kk * a_h))

    def step(S, inp):
        r_t, w_t, k_t, v_t, a_t, b_t = inp
        sa = jnp.einsum('bhij,bhj->bhi', S, a_t)
        S = S * w_t[:, :, None, :] + sa[..., None] * b_t[:, :, None, :] + v_t[..., None] * k_t[:, :, None, :]
        return S, jnp.einsum('bhij,bhj->bhi', S, r_t)

    S0 = jnp.zeros((B, H, N, N), f32)
    _, y = lax.scan(step, S0, seq_in)
    y = tm(y)
    mu = jnp.mean(y, axis=-1, keepdims=True)
    var = jnp.mean(jnp.square(y - mu), axis=-1, keepdims=True)
    y = (y - mu) * lax.rsqrt(var + GN_EPS) * lnx_w.astype(f32).reshape(H, N) + lnx_b.astype(f32).reshape(H, N)
    y = y + jnp.sum(r_h * k_h * r_k.astype(f32), axis=-1, keepdims=True) * v_h
    return (y.reshape(B, T, C) * g).astype(out_dtype)


def _swa_gqa_sinks(q, k, v, sinks):
    B, T, _, hd = q.shape
    nb = T // BLOCK
    qb = q.reshape(B, nb, BLOCK, SWA_KV_HEADS, SWA_GROUP, hd)

    def band(t):
        tb = t.reshape(B, nb, BLOCK, SWA_KV_HEADS, hd)
        prev = jnp.pad(tb[:, :-1], ((0, 0), (1, 0), (0, 0), (0, 0), (0, 0)))
        return jnp.concatenate([prev, tb], axis=2)

    kb, vb = band(k), band(v)
    s = jnp.einsum('bnqhgd,bnkhd->bnhgqk', qb, kb).astype(jnp.float32) * (hd ** -0.5)
    blk = jnp.arange(nb)[:, None]
    qpos = blk * BLOCK + jnp.arange(BLOCK)[None, :]
    kpos = (blk - 1) * BLOCK + jnp.arange(2 * BLOCK)[None, :]
    diff = qpos[:, :, None] - kpos[:, None, :]
    valid = (diff >= 0) & (diff < WINDOW) & (kpos[:, None, :] >= 0)
    s = jnp.where(valid[None, :, None, None], s, NEG_INF)
    sink = jnp.broadcast_to(sinks.astype(jnp.float32).reshape(SWA_KV_HEADS, SWA_GROUP)[None, None, :, :, None, None],
                            s.shape[:-1] + (1,))
    p = jax.nn.softmax(jnp.concatenate([s, sink], axis=-1), axis=-1)[..., :-1]
    o = jnp.einsum('bnhgqk,bnkhd->bnqhgd', p.astype(v.dtype), vb)
    return o.reshape(B, T, SWA_Q_HEADS * hd)


def _memory_cross_attn(h, mem_n, w_xq, w_xkv, w_xo):
    B, T, _ = h.shape
    M = mem_n.shape[1]
    q = (h @ w_xq).reshape(B, T, XATTN_HEADS, XATTN_HEAD_DIM)
    k, v = jnp.split(mem_n @ w_xkv, 2, axis=-1)
    k = k.reshape(B, M, XATTN_HEADS, XATTN_HEAD_DIM)
    v = v.reshape(B, M, XATTN_HEADS, XATTN_HEAD_DIM)
    s = jnp.einsum('bthd,bmhd->bhtm', q, k).astype(jnp.float32) * (XATTN_HEAD_DIM ** -0.5)
    p = jax.nn.softmax(s, axis=-1).astype(v.dtype)
    o = jnp.einsum('bhtm,bmhd->bthd', p, v).reshape(B, T, D_MODEL)
    return o @ w_xo


def _fwd_setup_inputs(seed: int = 0) -> dict:
    key = jax.random.key(seed)
    ks = iter(jax.random.split(key, 40))
    f32 = jnp.float32
    L, D, C = DEPTH, D_MODEL, RWKV_WIDTH
    nrm = lambda shape, scale: jax.random.normal(next(ks), shape, f32) * scale
    uni = lambda shape, lo, hi: jax.random.uniform(next(ks), shape, f32, lo, hi)
    gain = lambda shape: 1.0 + nrm(shape, 0.02)
    return {
        'x': nrm((BATCH, SEQ, D), 1.0),
        'mem': nrm((BATCH, MEM_LEN, D), 1.0),
        'f1_norm': gain((L, D)),
        'f1_gate': nrm((L, D, D_FF), D ** -0.5),
        'f1_up': nrm((L, D, D_FF), D ** -0.5),
        'f1_down': nrm((L, D_FF, D), D_FF ** -0.5),
        'mix_norm': gain((L, D)),
        'w_in': nrm((L, D, IN_COLS), D ** -0.5),
        'b_in_attn': nrm((L, SWA_COLS), 0.02),
        'rw_mu': uni((L, SHIFT_COLS), 0.0, 1.0),
        'rw_w0': uni((L, C), -6.0, -1.0),
        'rw_decay_up': nrm((L, DECAY_LORA, C), 0.1),
        'rw_a0': nrm((L, C), 0.1),
        'rw_aaa_up': nrm((L, AAA_LORA, C), 0.5 * AAA_LORA ** -0.5),
        'rw_gate_up': nrm((L, GATE_LORA, C), GATE_LORA ** -0.5),
        'rw_k_k': 0.85 + nrm((L, C), 0.02),
        'rw_k_a': 1.0 + nrm((L, C), 0.02),
        'rw_r_k': nrm((L, RWKV_HEADS, HEAD_DIM), 0.1),
        'rw_lnx_w': gain((L, C)),
        'rw_lnx_b': nrm((L, C), 0.02),
        'attn_sinks': nrm((L, SWA_Q_HEADS), 0.5),
        'w_out': nrm((L, D, D), D ** -0.5),
        'b_out': nrm((L, D), 0.02),
        'xa_norm': gain((L, D)),
        'mem_norm': gain((L, D)),
        'w_xq': nrm((L, D, D), D ** -0.5),
        'w_xkv': nrm((L, D, 2 * D), D ** -0.5),
        'w_xo': nrm((L, D, D), D ** -0.5),
        'f2_norm': gain((L, D)),
        'f2_gate': nrm((L, D, D_FF), D ** -0.5),
        'f2_up': nrm((L, D, D_FF), D ** -0.5),
        'f2_down': nrm((L, D_FF, D), D_FF ** -0.5),
        'final_norm': gain((D,)),
    }


def _fwd_reference(x, mem, f1_norm, f1_gate, f1_up, f1_down, mix_norm, w_in, b_in_attn, rw_mu, rw_w0,
              rw_decay_up, rw_a0, rw_aaa_up, rw_gate_up, rw_k_k, rw_k_a, rw_r_k, rw_lnx_w, rw_lnx_b,
              attn_sinks, w_out, b_out, xa_norm, mem_norm, w_xq, w_xkv, w_xo, f2_norm, f2_gate, f2_up,
              f2_down, final_norm):
    B, T, _ = x.shape
    pos = jnp.arange(T, dtype=jnp.float32)
    kv_w = SWA_KV_HEADS * HEAD_DIM
    for l in range(DEPTH):
        x = x + 0.5 * _swiglu(_rmsnorm(x, f1_norm[l]), f1_gate[l], f1_up[l], f1_down[l])
        h = _rmsnorm(x, mix_norm[l])
        proj = h @ w_in[l]
        zr = proj[..., :SHIFT_COLS]
        zr_prev = jnp.pad(zr[:, :-1], ((0, 0), (1, 0), (0, 0)))
        zr = zr + (zr_prev - zr) * rw_mu[l]
        za = proj[..., SHIFT_COLS:] + b_in_attn[l]
        q = za[..., :SWA_WIDTH].reshape(B, T, SWA_Q_HEADS, HEAD_DIM)
        k = za[..., SWA_WIDTH:SWA_WIDTH + kv_w].reshape(B, T, SWA_KV_HEADS, HEAD_DIM)
        v = za[..., SWA_WIDTH + kv_w:].reshape(B, T, SWA_KV_HEADS, HEAD_DIM)
        y_rwkv = _rwkv7_time_mix(zr, rw_w0[l], rw_decay_up[l], rw_a0[l], rw_aaa_up[l], rw_gate_up[l],
                                 rw_k_k[l], rw_k_a[l], rw_r_k[l], rw_lnx_w[l], rw_lnx_b[l])
        y_swa = _swa_gqa_sinks(_rope(q, pos), _rope(k, pos), v, attn_sinks[l])
        x = x + jnp.concatenate([y_rwkv, y_swa], axis=-1) @ w_out[l] + b_out[l]
        x = x + _memory_cross_attn(_rmsnorm(x, xa_norm[l]), _rmsnorm(mem, mem_norm[l]), w_xq[l], w_xkv[l], w_xo[l])
        x = x + 0.5 * _swiglu(_rmsnorm(x, f2_norm[l]), f2_gate[l], f2_up[l], f2_down[l])
    return _rmsnorm(x, final_norm)


import jax as _jax
import jax.numpy as _jnp

TWIN_FORMAT = 'train_step'
FWD_PARAMS = ['x', 'mem', 'f1_norm', 'f1_gate', 'f1_up', 'f1_down', 'mix_norm', 'w_in', 'b_in_attn', 'rw_mu', 'rw_w0', 'rw_decay_up', 'rw_a0', 'rw_aaa_up', 'rw_gate_up', 'rw_k_k', 'rw_k_a', 'rw_r_k', 'rw_lnx_w', 'rw_lnx_b', 'attn_sinks', 'w_out', 'b_out', 'xa_norm', 'mem_norm', 'w_xq', 'w_xkv', 'w_xo', 'f2_norm', 'f2_gate', 'f2_up', 'f2_down', 'final_norm']
TWIN_WEIGHTS = ['f1_norm', 'f1_gate', 'f1_up', 'f1_down', 'mix_norm', 'w_in', 'b_in_attn', 'rw_mu', 'rw_w0', 'rw_decay_up', 'rw_a0', 'rw_aaa_up', 'rw_gate_up', 'rw_k_k', 'rw_k_a', 'rw_r_k', 'rw_lnx_w', 'rw_lnx_b', 'attn_sinks', 'w_out', 'b_out', 'xa_norm', 'mem_norm', 'w_xq', 'w_xkv', 'w_xo', 'f2_norm', 'f2_gate', 'f2_up', 'f2_down', 'final_norm']
TWIN_DIFF_INPUT = 'x'
TWIN_INPUTS = ['x', 'mem', 'f1_norm', 'f1_gate', 'f1_up', 'f1_down', 'mix_norm', 'w_in', 'b_in_attn', 'rw_mu', 'rw_w0', 'rw_decay_up', 'rw_a0', 'rw_aaa_up', 'rw_gate_up', 'rw_k_k', 'rw_k_a', 'rw_r_k', 'rw_lnx_w', 'rw_lnx_b', 'attn_sinks', 'w_out', 'b_out', 'xa_norm', 'mem_norm', 'w_xq', 'w_xkv', 'w_xo', 'f2_norm', 'f2_gate', 'f2_up', 'f2_down', 'final_norm', 'loss_target', 'm_f1_norm', 'm_f1_gate', 'm_f1_up', 'm_f1_down', 'm_mix_norm', 'm_w_in', 'm_b_in_attn', 'm_rw_mu', 'm_rw_w0', 'm_rw_decay_up', 'm_rw_a0', 'm_rw_aaa_up', 'm_rw_gate_up', 'm_rw_k_k', 'm_rw_k_a', 'm_rw_r_k', 'm_rw_lnx_w', 'm_rw_lnx_b', 'm_attn_sinks', 'm_w_out', 'm_b_out', 'm_xa_norm', 'm_mem_norm', 'm_w_xq', 'm_w_xkv', 'm_w_xo', 'm_f2_norm', 'm_f2_gate', 'm_f2_up', 'm_f2_down', 'm_final_norm', 'v_f1_norm', 'v_f1_gate', 'v_f1_up', 'v_f1_down', 'v_mix_norm', 'v_w_in', 'v_b_in_attn', 'v_rw_mu', 'v_rw_w0', 'v_rw_decay_up', 'v_rw_a0', 'v_rw_aaa_up', 'v_rw_gate_up', 'v_rw_k_k', 'v_rw_k_a', 'v_rw_r_k', 'v_rw_lnx_w', 'v_rw_lnx_b', 'v_attn_sinks', 'v_w_out', 'v_b_out', 'v_xa_norm', 'v_mem_norm', 'v_w_xq', 'v_w_xkv', 'v_w_xo', 'v_f2_norm', 'v_f2_gate', 'v_f2_up', 'v_f2_down', 'v_final_norm']
TWIN_OUTPUTS = ['loss', 'grad_x', 'grad_f1_norm', 'grad_f1_gate', 'grad_f1_up', 'grad_f1_down', 'grad_mix_norm', 'grad_w_in', 'grad_b_in_attn', 'grad_rw_mu', 'grad_rw_w0', 'grad_rw_decay_up', 'grad_rw_a0', 'grad_rw_aaa_up', 'grad_rw_gate_up', 'grad_rw_k_k', 'grad_rw_k_a', 'grad_rw_r_k', 'grad_rw_lnx_w', 'grad_rw_lnx_b', 'grad_attn_sinks', 'grad_w_out', 'grad_b_out', 'grad_xa_norm', 'grad_mem_norm', 'grad_w_xq', 'grad_w_xkv', 'grad_w_xo', 'grad_f2_norm', 'grad_f2_gate', 'grad_f2_up', 'grad_f2_down', 'grad_final_norm', 'delta_f1_norm', 'delta_f1_gate', 'delta_f1_up', 'delta_f1_down', 'delta_mix_norm', 'delta_w_in', 'delta_b_in_attn', 'delta_rw_mu', 'delta_rw_w0', 'delta_rw_decay_up', 'delta_rw_a0', 'delta_rw_aaa_up', 'delta_rw_gate_up', 'delta_rw_k_k', 'delta_rw_k_a', 'delta_rw_r_k', 'delta_rw_lnx_w', 'delta_rw_lnx_b', 'delta_attn_sinks', 'delta_w_out', 'delta_b_out', 'delta_xa_norm', 'delta_mem_norm', 'delta_w_xq', 'delta_w_xkv', 'delta_w_xo', 'delta_f2_norm', 'delta_f2_gate', 'delta_f2_up', 'delta_f2_down', 'delta_final_norm', 'new_m_f1_norm', 'new_m_f1_gate', 'new_m_f1_up', 'new_m_f1_down', 'new_m_mix_norm', 'new_m_w_in', 'new_m_b_in_attn', 'new_m_rw_mu', 'new_m_rw_w0', 'new_m_rw_decay_up', 'new_m_rw_a0', 'new_m_rw_aaa_up', 'new_m_rw_gate_up', 'new_m_rw_k_k', 'new_m_rw_k_a', 'new_m_rw_r_k', 'new_m_rw_lnx_w', 'new_m_rw_lnx_b', 'new_m_attn_sinks', 'new_m_w_out', 'new_m_b_out', 'new_m_xa_norm', 'new_m_mem_norm', 'new_m_w_xq', 'new_m_w_xkv', 'new_m_w_xo', 'new_m_f2_norm', 'new_m_f2_gate', 'new_m_f2_up', 'new_m_f2_down', 'new_m_final_norm', 'new_v_f1_norm', 'new_v_f1_gate', 'new_v_f1_up', 'new_v_f1_down', 'new_v_mix_norm', 'new_v_w_in', 'new_v_b_in_attn', 'new_v_rw_mu', 'new_v_rw_w0', 'new_v_rw_decay_up', 'new_v_rw_a0', 'new_v_rw_aaa_up', 'new_v_rw_gate_up', 'new_v_rw_k_k', 'new_v_rw_k_a', 'new_v_rw_r_k', 'new_v_rw_lnx_w', 'new_v_rw_lnx_b', 'new_v_attn_sinks', 'new_v_w_out', 'new_v_b_out', 'new_v_xa_norm', 'new_v_mem_norm', 'new_v_w_xq', 'new_v_w_xkv', 'new_v_w_xo', 'new_v_f2_norm', 'new_v_f2_gate', 'new_v_f2_up', 'new_v_f2_down', 'new_v_final_norm']
TWIN_LEAF_KINDS = {'loss': 'loss', 'grad_x': 'grad_x', 'grad_f1_norm': 'grad_w', 'grad_f1_gate': 'grad_w', 'grad_f1_up': 'grad_w', 'grad_f1_down': 'grad_w', 'grad_mix_norm': 'grad_w', 'grad_w_in': 'grad_w', 'grad_b_in_attn': 'grad_w', 'grad_rw_mu': 'grad_w', 'grad_rw_w0': 'grad_w', 'grad_rw_decay_up': 'grad_w', 'grad_rw_a0': 'grad_w', 'grad_rw_aaa_up': 'grad_w', 'grad_rw_gate_up': 'grad_w', 'grad_rw_k_k': 'grad_w', 'grad_rw_k_a': 'grad_w', 'grad_rw_r_k': 'grad_w', 'grad_rw_lnx_w': 'grad_w', 'grad_rw_lnx_b': 'grad_w', 'grad_attn_sinks': 'grad_w', 'grad_w_out': 'grad_w', 'grad_b_out': 'grad_w', 'grad_xa_norm': 'grad_w', 'grad_mem_norm': 'grad_w', 'grad_w_xq': 'grad_w', 'grad_w_xkv': 'grad_w', 'grad_w_xo': 'grad_w', 'grad_f2_norm': 'grad_w', 'grad_f2_gate': 'grad_w', 'grad_f2_up': 'grad_w', 'grad_f2_down': 'grad_w', 'grad_final_norm': 'grad_w', 'delta_f1_norm': 'delta_w', 'delta_f1_gate': 'delta_w', 'delta_f1_up': 'delta_w', 'delta_f1_down': 'delta_w', 'delta_mix_norm': 'delta_w', 'delta_w_in': 'delta_w', 'delta_b_in_attn': 'delta_w', 'delta_rw_mu': 'delta_w', 'delta_rw_w0': 'delta_w', 'delta_rw_decay_up': 'delta_w', 'delta_rw_a0': 'delta_w', 'delta_rw_aaa_up': 'delta_w', 'delta_rw_gate_up': 'delta_w', 'delta_rw_k_k': 'delta_w', 'delta_rw_k_a': 'delta_w', 'delta_rw_r_k': 'delta_w', 'delta_rw_lnx_w': 'delta_w', 'delta_rw_lnx_b': 'delta_w', 'delta_attn_sinks': 'delta_w', 'delta_w_out': 'delta_w', 'delta_b_out': 'delta_w', 'delta_xa_norm': 'delta_w', 'delta_mem_norm': 'delta_w', 'delta_w_xq': 'delta_w', 'delta_w_xkv': 'delta_w', 'delta_w_xo': 'delta_w', 'delta_f2_norm': 'delta_w', 'delta_f2_gate': 'delta_w', 'delta_f2_up': 'delta_w', 'delta_f2_down': 'delta_w', 'delta_final_norm': 'delta_w', 'new_m_f1_norm': 'new_m', 'new_m_f1_gate': 'new_m', 'new_m_f1_up': 'new_m', 'new_m_f1_down': 'new_m', 'new_m_mix_norm': 'new_m', 'new_m_w_in': 'new_m', 'new_m_b_in_attn': 'new_m', 'new_m_rw_mu': 'new_m', 'new_m_rw_w0': 'new_m', 'new_m_rw_decay_up': 'new_m', 'new_m_rw_a0': 'new_m', 'new_m_rw_aaa_up': 'new_m', 'new_m_rw_gate_up': 'new_m', 'new_m_rw_k_k': 'new_m', 'new_m_rw_k_a': 'new_m', 'new_m_rw_r_k': 'new_m', 'new_m_rw_lnx_w': 'new_m', 'new_m_rw_lnx_b': 'new_m', 'new_m_attn_sinks': 'new_m', 'new_m_w_out': 'new_m', 'new_m_b_out': 'new_m', 'new_m_xa_norm': 'new_m', 'new_m_mem_norm': 'new_m', 'new_m_w_xq': 'new_m', 'new_m_w_xkv': 'new_m', 'new_m_w_xo': 'new_m', 'new_m_f2_norm': 'new_m', 'new_m_f2_gate': 'new_m', 'new_m_f2_up': 'new_m', 'new_m_f2_down': 'new_m', 'new_m_final_norm': 'new_m', 'new_v_f1_norm': 'new_v', 'new_v_f1_gate': 'new_v', 'new_v_f1_up': 'new_v', 'new_v_f1_down': 'new_v', 'new_v_mix_norm': 'new_v', 'new_v_w_in': 'new_v', 'new_v_b_in_attn': 'new_v', 'new_v_rw_mu': 'new_v', 'new_v_rw_w0': 'new_v', 'new_v_rw_decay_up': 'new_v', 'new_v_rw_a0': 'new_v', 'new_v_rw_aaa_up': 'new_v', 'new_v_rw_gate_up': 'new_v', 'new_v_rw_k_k': 'new_v', 'new_v_rw_k_a': 'new_v', 'new_v_rw_r_k': 'new_v', 'new_v_rw_lnx_w': 'new_v', 'new_v_rw_lnx_b': 'new_v', 'new_v_attn_sinks': 'new_v', 'new_v_w_out': 'new_v', 'new_v_b_out': 'new_v', 'new_v_xa_norm': 'new_v', 'new_v_mem_norm': 'new_v', 'new_v_w_xq': 'new_v', 'new_v_w_xkv': 'new_v', 'new_v_w_xo': 'new_v', 'new_v_f2_norm': 'new_v', 'new_v_f2_gate': 'new_v', 'new_v_f2_up': 'new_v', 'new_v_f2_down': 'new_v', 'new_v_final_norm': 'new_v'}


def _forward(args):
    return _fwd_reference(*[args[k] for k in FWD_PARAMS])


def _output_shape():
    out = _jax.eval_shape(lambda: _forward(_fwd_setup_inputs(0)))
    return out.shape, out.dtype

N_MICROBATCH = 1
ADAM_LR = 0.001
ADAM_B1 = 0.9
ADAM_B2 = 0.999
ADAM_EPS = 1e-08
ADAM_WD = 0.01
ADAM_STEP = 10
PER_EXAMPLE_BATCH_AXIS = {'x': 0, 'mem': 0, 'loss_target': 0}
SHARED_INPUTS = []
_WEIGHT_DTYPES = {'f1_norm': _jnp.float32, 'f1_gate': _jnp.float32, 'f1_up': _jnp.float32, 'f1_down': _jnp.float32, 'mix_norm': _jnp.float32, 'w_in': _jnp.float32, 'b_in_attn': _jnp.float32, 'rw_mu': _jnp.float32, 'rw_w0': _jnp.float32, 'rw_decay_up': _jnp.float32, 'rw_a0': _jnp.float32, 'rw_aaa_up': _jnp.float32, 'rw_gate_up': _jnp.float32, 'rw_k_k': _jnp.float32, 'rw_k_a': _jnp.float32, 'rw_r_k': _jnp.float32, 'rw_lnx_w': _jnp.float32, 'rw_lnx_b': _jnp.float32, 'attn_sinks': _jnp.float32, 'w_out': _jnp.float32, 'b_out': _jnp.float32, 'xa_norm': _jnp.float32, 'mem_norm': _jnp.float32, 'w_xq': _jnp.float32, 'w_xkv': _jnp.float32, 'w_xo': _jnp.float32, 'f2_norm': _jnp.float32, 'f2_gate': _jnp.float32, 'f2_up': _jnp.float32, 'f2_down': _jnp.float32, 'final_norm': _jnp.float32}
MOMENT_SCALE = {'f1_norm': 3.102041e-02, 'f1_gate': 1.320933e-02, 'f1_up': 1.279748e-02, 'f1_down': 2.121588e-02, 'mix_norm': 4.316997e-02, 'w_in': 2.907174e-02, 'b_in_attn': 6.647655e-02, 'rw_mu': 5.311618e-02, 'rw_w0': 1.399248e-02, 'rw_decay_up': 1.537321e-03, 'rw_a0': 1.298560e-02, 'rw_aaa_up': 1.266374e-02, 'rw_gate_up': 3.290658e-02, 'rw_k_k': 3.575679e-02, 'rw_k_a': 3.448419e-02, 'rw_r_k': 6.855211e-02, 'rw_lnx_w': 3.136835e-02, 'rw_lnx_b': 3.199370e-02, 'attn_sinks': 1.039261e-02, 'w_out': 2.372268e-02, 'b_out': 6.068353e-02, 'xa_norm': 5.971867e-03, 'mem_norm': 8.601782e-03, 'w_xq': 5.916124e-03, 'w_xkv': 5.964428e-03, 'w_xo': 5.987472e-03, 'f2_norm': 2.293826e-02, 'f2_gate': 1.000080e-02, 'f2_up': 9.705854e-03, 'f2_down': 1.607341e-02, 'final_norm': 7.996605e+00}


def _to_microbatches(a, axis):
    t = _jnp.moveaxis(a, axis, 0)
    t = t.reshape((N_MICROBATCH, t.shape[0] // N_MICROBATCH) + t.shape[1:])
    return _jnp.moveaxis(t, 1, axis + 1)


def setup_inputs(seed: int = 0) -> dict:
    inp = _fwd_setup_inputs(seed)
    key = _jax.random.fold_in(_jax.random.key(seed), 7919)
    shape, _ = _output_shape()
    out = dict(inp)
    out["loss_target"] = _jax.random.normal(_jax.random.fold_in(key, 0), shape, _jnp.float32)
    for i, name in enumerate(TWIN_WEIGHTS):
        w = inp[name].astype(_jnp.float32)
        if MOMENT_SCALE is None:
            s = _jnp.sqrt(_jnp.mean(_jnp.square(w)) + 1e-30)
        else:
            s = MOMENT_SCALE[name]
        km, kv = _jax.random.split(_jax.random.fold_in(key, i + 1))
        out[name] = w
        out["m_" + name] = s * _jax.random.normal(km, w.shape, _jnp.float32)
        out["v_" + name] = (s * s) * _jax.random.uniform(kv, w.shape, _jnp.float32, 0.5, 1.5)
    if N_MICROBATCH > 1:
        for name, axis in PER_EXAMPLE_BATCH_AXIS.items():
            out[name] = _to_microbatches(out[name], axis)
    return {'x': out['x'], 'mem': out['mem'], 'f1_norm': out['f1_norm'], 'f1_gate': out['f1_gate'], 'f1_up': out['f1_up'], 'f1_down': out['f1_down'], 'mix_norm': out['mix_norm'], 'w_in': out['w_in'], 'b_in_attn': out['b_in_attn'], 'rw_mu': out['rw_mu'], 'rw_w0': out['rw_w0'], 'rw_decay_up': out['rw_decay_up'], 'rw_a0': out['rw_a0'], 'rw_aaa_up': out['rw_aaa_up'], 'rw_gate_up': out['rw_gate_up'], 'rw_k_k': out['rw_k_k'], 'rw_k_a': out['rw_k_a'], 'rw_r_k': out['rw_r_k'], 'rw_lnx_w': out['rw_lnx_w'], 'rw_lnx_b': out['rw_lnx_b'], 'attn_sinks': out['attn_sinks'], 'w_out': out['w_out'], 'b_out': out['b_out'], 'xa_norm': out['xa_norm'], 'mem_norm': out['mem_norm'], 'w_xq': out['w_xq'], 'w_xkv': out['w_xkv'], 'w_xo': out['w_xo'], 'f2_norm': out['f2_norm'], 'f2_gate': out['f2_gate'], 'f2_up': out['f2_up'], 'f2_down': out['f2_down'], 'final_norm': out['final_norm'], 'loss_target': out['loss_target'], 'm_f1_norm': out['m_f1_norm'], 'm_f1_gate': out['m_f1_gate'], 'm_f1_up': out['m_f1_up'], 'm_f1_down': out['m_f1_down'], 'm_mix_norm': out['m_mix_norm'], 'm_w_in': out['m_w_in'], 'm_b_in_attn': out['m_b_in_attn'], 'm_rw_mu': out['m_rw_mu'], 'm_rw_w0': out['m_rw_w0'], 'm_rw_decay_up': out['m_rw_decay_up'], 'm_rw_a0': out['m_rw_a0'], 'm_rw_aaa_up': out['m_rw_aaa_up'], 'm_rw_gate_up': out['m_rw_gate_up'], 'm_rw_k_k': out['m_rw_k_k'], 'm_rw_k_a': out['m_rw_k_a'], 'm_rw_r_k': out['m_rw_r_k'], 'm_rw_lnx_w': out['m_rw_lnx_w'], 'm_rw_lnx_b': out['m_rw_lnx_b'], 'm_attn_sinks': out['m_attn_sinks'], 'm_w_out': out['m_w_out'], 'm_b_out': out['m_b_out'], 'm_xa_norm': out['m_xa_norm'], 'm_mem_norm': out['m_mem_norm'], 'm_w_xq': out['m_w_xq'], 'm_w_xkv': out['m_w_xkv'], 'm_w_xo': out['m_w_xo'], 'm_f2_norm': out['m_f2_norm'], 'm_f2_gate': out['m_f2_gate'], 'm_f2_up': out['m_f2_up'], 'm_f2_down': out['m_f2_down'], 'm_final_norm': out['m_final_norm'], 'v_f1_norm': out['v_f1_norm'], 'v_f1_gate': out['v_f1_gate'], 'v_f1_up': out['v_f1_up'], 'v_f1_down': out['v_f1_down'], 'v_mix_norm': out['v_mix_norm'], 'v_w_in': out['v_w_in'], 'v_b_in_attn': out['v_b_in_attn'], 'v_rw_mu': out['v_rw_mu'], 'v_rw_w0': out['v_rw_w0'], 'v_rw_decay_up': out['v_rw_decay_up'], 'v_rw_a0': out['v_rw_a0'], 'v_rw_aaa_up': out['v_rw_aaa_up'], 'v_rw_gate_up': out['v_rw_gate_up'], 'v_rw_k_k': out['v_rw_k_k'], 'v_rw_k_a': out['v_rw_k_a'], 'v_rw_r_k': out['v_rw_r_k'], 'v_rw_lnx_w': out['v_rw_lnx_w'], 'v_rw_lnx_b': out['v_rw_lnx_b'], 'v_attn_sinks': out['v_attn_sinks'], 'v_w_out': out['v_w_out'], 'v_b_out': out['v_b_out'], 'v_xa_norm': out['v_xa_norm'], 'v_mem_norm': out['v_mem_norm'], 'v_w_xq': out['v_w_xq'], 'v_w_xkv': out['v_w_xkv'], 'v_w_xo': out['v_w_xo'], 'v_f2_norm': out['v_f2_norm'], 'v_f2_gate': out['v_f2_gate'], 'v_f2_up': out['v_f2_up'], 'v_f2_down': out['v_f2_down'], 'v_final_norm': out['v_final_norm']}


def _loss(weights, diff, rest, loss_target):
    with _jax.named_scope("forward"):
        args = {**rest, TWIN_DIFF_INPUT: diff, **{k: w.astype(_WEIGHT_DTYPES[k]) for k, w in weights.items()}}
        y = _forward(args)
    with _jax.named_scope("loss_head"):
        err = _jnp.square(y.astype(_jnp.float32) - loss_target)
        return 0.5 * _jnp.sum(_jnp.mean(err, axis=-1)) if err.ndim else 0.5 * err


def _adamw(w, g, m, v):
    m = ADAM_B1 * m + (1.0 - ADAM_B1) * g
    v = ADAM_B2 * v + (1.0 - ADAM_B2) * _jnp.square(g)
    m_hat = m / (1.0 - ADAM_B1 ** ADAM_STEP)
    v_hat = v / (1.0 - ADAM_B2 ** ADAM_STEP)
    delta = -ADAM_LR * (m_hat / (_jnp.sqrt(v_hat) + ADAM_EPS) + ADAM_WD * w)
    return delta, m, v


def reference(x, mem, f1_norm, f1_gate, f1_up, f1_down, mix_norm, w_in, b_in_attn, rw_mu, rw_w0, rw_decay_up, rw_a0, rw_aaa_up, rw_gate_up, rw_k_k, rw_k_a, rw_r_k, rw_lnx_w, rw_lnx_b, attn_sinks, w_out, b_out, xa_norm, mem_norm, w_xq, w_xkv, w_xo, f2_norm, f2_gate, f2_up, f2_down, final_norm, loss_target, m_f1_norm, m_f1_gate, m_f1_up, m_f1_down, m_mix_norm, m_w_in, m_b_in_attn, m_rw_mu, m_rw_w0, m_rw_decay_up, m_rw_a0, m_rw_aaa_up, m_rw_gate_up, m_rw_k_k, m_rw_k_a, m_rw_r_k, m_rw_lnx_w, m_rw_lnx_b, m_attn_sinks, m_w_out, m_b_out, m_xa_norm, m_mem_norm, m_w_xq, m_w_xkv, m_w_xo, m_f2_norm, m_f2_gate, m_f2_up, m_f2_down, m_final_norm, v_f1_norm, v_f1_gate, v_f1_up, v_f1_down, v_mix_norm, v_w_in, v_b_in_attn, v_rw_mu, v_rw_w0, v_rw_decay_up, v_rw_a0, v_rw_aaa_up, v_rw_gate_up, v_rw_k_k, v_rw_k_a, v_rw_r_k, v_rw_lnx_w, v_rw_lnx_b, v_attn_sinks, v_w_out, v_b_out, v_xa_norm, v_mem_norm, v_w_xq, v_w_xkv, v_w_xo, v_f2_norm, v_f2_gate, v_f2_up, v_f2_down, v_final_norm):
    given = dict(x=x, mem=mem, f1_norm=f1_norm, f1_gate=f1_gate, f1_up=f1_up, f1_down=f1_down, mix_norm=mix_norm, w_in=w_in, b_in_attn=b_in_attn, rw_mu=rw_mu, rw_w0=rw_w0, rw_decay_up=rw_decay_up, rw_a0=rw_a0, rw_aaa_up=rw_aaa_up, rw_gate_up=rw_gate_up, rw_k_k=rw_k_k, rw_k_a=rw_k_a, rw_r_k=rw_r_k, rw_lnx_w=rw_lnx_w, rw_lnx_b=rw_lnx_b, attn_sinks=attn_sinks, w_out=w_out, b_out=b_out, xa_norm=xa_norm, mem_norm=mem_norm, w_xq=w_xq, w_xkv=w_xkv, w_xo=w_xo, f2_norm=f2_norm, f2_gate=f2_gate, f2_up=f2_up, f2_down=f2_down, final_norm=final_norm, loss_target=loss_target, m_f1_norm=m_f1_norm, m_f1_gate=m_f1_gate, m_f1_up=m_f1_up, m_f1_down=m_f1_down, m_mix_norm=m_mix_norm, m_w_in=m_w_in, m_b_in_attn=m_b_in_attn, m_rw_mu=m_rw_mu, m_rw_w0=m_rw_w0, m_rw_decay_up=m_rw_decay_up, m_rw_a0=m_rw_a0, m_rw_aaa_up=m_rw_aaa_up, m_rw_gate_up=m_rw_gate_up, m_rw_k_k=m_rw_k_k, m_rw_k_a=m_rw_k_a, m_rw_r_k=m_rw_r_k, m_rw_lnx_w=m_rw_lnx_w, m_rw_lnx_b=m_rw_lnx_b, m_attn_sinks=m_attn_sinks, m_w_out=m_w_out, m_b_out=m_b_out, m_xa_norm=m_xa_norm, m_mem_norm=m_mem_norm, m_w_xq=m_w_xq, m_w_xkv=m_w_xkv, m_w_xo=m_w_xo, m_f2_norm=m_f2_norm, m_f2_gate=m_f2_gate, m_f2_up=m_f2_up, m_f2_down=m_f2_down, m_final_norm=m_final_norm, v_f1_norm=v_f1_norm, v_f1_gate=v_f1_gate, v_f1_up=v_f1_up, v_f1_down=v_f1_down, v_mix_norm=v_mix_norm, v_w_in=v_w_in, v_b_in_attn=v_b_in_attn, v_rw_mu=v_rw_mu, v_rw_w0=v_rw_w0, v_rw_decay_up=v_rw_decay_up, v_rw_a0=v_rw_a0, v_rw_aaa_up=v_rw_aaa_up, v_rw_gate_up=v_rw_gate_up, v_rw_k_k=v_rw_k_k, v_rw_k_a=v_rw_k_a, v_rw_r_k=v_rw_r_k, v_rw_lnx_w=v_rw_lnx_w, v_rw_lnx_b=v_rw_lnx_b, v_attn_sinks=v_attn_sinks, v_w_out=v_w_out, v_b_out=v_b_out, v_xa_norm=v_xa_norm, v_mem_norm=v_mem_norm, v_w_xq=v_w_xq, v_w_xkv=v_w_xkv, v_w_xo=v_w_xo, v_f2_norm=v_f2_norm, v_f2_gate=v_f2_gate, v_f2_up=v_f2_up, v_f2_down=v_f2_down, v_final_norm=v_final_norm)
    weights = {n: given[n] for n in TWIN_WEIGHTS}
    shared = {n: given[n] for n in SHARED_INPUTS}
    per_example = {n: given[n] for n in ['x', 'mem']}
    grad_fn = _jax.value_and_grad(_loss, argnums=(0, 1))

    def one_microbatch(ex, loss_target):
        ex = dict(ex)
        diff = ex.pop(TWIN_DIFF_INPUT)
        return grad_fn(weights, diff, {**shared, **ex}, loss_target)

    if N_MICROBATCH == 1:
        loss, (grad_w, grad_x) = one_microbatch(per_example, given["loss_target"])
    else:
        def body(carry, xs):
            loss_sum, grad_sum = carry
            l_k, (gw_k, gx_k) = one_microbatch(xs[0], xs[1])
            with _jax.named_scope("update"):
                return (loss_sum + l_k, _jax.tree.map(_jnp.add, grad_sum, gw_k)), gx_k

        init = (_jnp.zeros((), _jnp.float32), _jax.tree.map(_jnp.zeros_like, weights))
        (loss, grad_w), grad_x = _jax.lax.scan(body, init, (per_example, given["loss_target"]))
    with _jax.named_scope("update"):
        delta_w, new_m, new_v = {}, {}, {}
        for n in TWIN_WEIGHTS:
            delta_w[n], new_m[n], new_v[n] = _adamw(weights[n], grad_w[n], given["m_" + n], given["v_" + n])
    return (loss, grad_x, *[grad_w[n] for n in TWIN_WEIGHTS], *[delta_w[n] for n in TWIN_WEIGHTS],
            *[new_m[n] for n in TWIN_WEIGHTS], *[new_v[n] for n in TWIN_WEIGHTS])
```

```python
import functools

import jax
import jax.numpy as jnp
from jax import lax
from jax.experimental import pallas as pl
from jax.experimental.pallas import tpu as pltpu

F32, BF16 = jnp.float32, jnp.bfloat16
MESH = pl.DeviceIdType.MESH

HEAD = 64
RW_HEADS = 16
RW_W = 1024
SWA_W = 1024
KV_W = 128
DECAY_LORA, AAA_LORA, GATE_LORA = 64, 64, 160
LORA_W = DECAY_LORA + AAA_LORA + GATE_LORA
SHIFT_COLS = 3 * RW_W + LORA_W
XH = 4
XHD = 512
MEM_LEN = 256
WINDOW = 128
GN_EPS = 64e-5
RMS_EPS = 1e-6
NEG_INF = -1e30
ADAM_LR, ADAM_B1, ADAM_B2, ADAM_EPS, ADAM_WD, ADAM_STEP = 0.001, 0.9, 0.999, 1e-08, 0.01, 10

VMEM_LIMIT = 56 * 1024 * 1024


def _cp(sem=None, **kw):
    return pltpu.CompilerParams(dimension_semantics=sem, vmem_limit_bytes=VMEM_LIMIT, **kw)


def _pick(dim, target):
    if dim <= target:
        return dim
    best = None
    for t in range(128, target + 1, 128):
        if dim % t == 0:
            best = t
    assert best is not None, (dim, target)
    return best


_DIMS = {"nn": (((1,), (0,)), ((), ())), "nt": (((1,), (1,)), ((), ())), "tn": (((0,), (0,)), ((), ()))}


def _mm(a, b, mode, name, out_dtype=F32, alpha=1.0, res=None, bias=None, tm=1024, tn=1024, tk=512):
    if mode == "nn":
        (M, K), (K2, N) = a.shape, b.shape
    elif mode == "nt":
        (M, K), (N, K2) = a.shape, b.shape
    else:
        (K, M), (K2, N) = a.shape, b.shape
    assert K == K2, (name, a.shape, b.shape)
    tm, tn, tk = _pick(M, tm), _pick(N, tn), _pick(K, tk)
    nk = K // tk
    a_spec = pl.BlockSpec((tk, tm), lambda i, j, k: (k, i)) if mode == "tn" else pl.BlockSpec((tm, tk), lambda i, j, k: (i, k))
    b_spec = pl.BlockSpec((tn, tk), lambda i, j, k: (j, k)) if mode == "nt" else pl.BlockSpec((tk, tn), lambda i, j, k: (k, j))
    o_spec = pl.BlockSpec((tm, tn), lambda i, j, k: (i, j))
    ins, specs = [a, b], [a_spec, b_spec]
    if res is not None:
        ins.append(res)
        specs.append(o_spec)
    if bias is not None:
        ins.append(bias)
        specs.append(pl.BlockSpec((1, tn), lambda i, j, k: (0, j)))
    dims = _DIMS[mode]

    def body(*refs):
        a_ref, b_ref = refs[0], refs[1]
        o_ref, acc_ref = refs[-2], refs[-1]
        k = pl.program_id(2)

        @pl.when(k == 0)
        def _():
            acc_ref[...] = jnp.zeros_like(acc_ref)

        acc_ref[...] += lax.dot_general(a_ref[...].astype(BF16), b_ref[...].astype(BF16), dims, preferred_element_type=F32)

        @pl.when(k == nk - 1)
        def _():
            o = acc_ref[...]
            if alpha != 1.0:
                o = o * alpha
            p = 2
            if res is not None:
                o = o + refs[p][...].astype(F32)
                p += 1
            if bias is not None:
                o = o + refs[p][...]
            o_ref[...] = o.astype(out_dtype)

    return pl.pallas_call(
        body, name=name, grid=(M // tm, N // tn, nk), in_specs=specs, out_specs=o_spec,
        out_shape=jax.ShapeDtypeStruct((M, N), out_dtype), scratch_shapes=[pltpu.VMEM((tm, tn), F32)],
        compiler_params=_cp(("parallel", "parallel", "arbitrary")),
    )(*ins)


def _rows(fn, name, T, tm, tiled, full, out_tiled, out_acc, extra=(), reverse=False, scratch=()):
    n = T // tm
    idx = (lambda i: n - 1 - i) if reverse else (lambda i: i)
    in_specs = [pl.BlockSpec((tm, a.shape[1]), lambda i: (idx(i), 0)) for a in tiled]
    in_specs += [mk(idx) for _, mk in extra]
    in_specs += [pl.BlockSpec(a.shape, lambda i, nd=a.ndim: (0,) * nd) for a in full]
    out_specs = [pl.BlockSpec((tm, c), lambda i: (idx(i), 0)) for c, _ in out_tiled]
    out_specs += [pl.BlockSpec(s, lambda i, nd=len(s): (0,) * nd) for s, _ in out_acc]
    out_shape = [jax.ShapeDtypeStruct((T, c), d) for c, d in out_tiled] + [jax.ShapeDtypeStruct(s, d) for s, d in out_acc]
    n_in = len(tiled) + len(extra) + len(full)
    n_t, n_a = len(out_tiled), len(out_acc)

    def body(*refs):
        step = pl.program_id(0)
        vals = [r[...] for r in refs[:n_in]]
        outs = fn(idx(step), *vals, *refs[n_in + n_t + n_a:])
        for r, v in zip(refs[n_in:n_in + n_t], outs[:n_t]):
            r[...] = v.astype(r.dtype)
        for r, v in zip(refs[n_in + n_t:n_in + n_t + n_a], outs[n_t:]):
            @pl.when(step == 0)
            def _(r=r):
                r[...] = jnp.zeros_like(r)

            r[...] += v

    return pl.pallas_call(
        body, name=name, grid=(n,), in_specs=in_specs, out_specs=out_specs, out_shape=out_shape,
        scratch_shapes=list(scratch), compiler_params=_cp(("arbitrary",)),
    )(*tiled, *[a for a, _ in extra], *full)


def _rms(x, g):
    return x * lax.rsqrt(jnp.mean(x * x, axis=-1, keepdims=True) + RMS_EPS) * g


def _rms_fwd(x, g, name, tm=256):
    (h,) = _rows(lambda i, x, g: (_rms(x, g),), name, x.shape[0], min(tm, x.shape[0]), [x], [g], [(x.shape[1], BF16)], [])
    return h


def _rms_bwd(x, g, dh, dres, name, tm=256):
    D = x.shape[1]

    def fn(i, x, dh, dres, g):
        _, vjp = jax.vjp(_rms, x, g)
        dx, dg = vjp(dh.astype(F32))
        dx = dx + dres
        return dx, dg, jnp.sum(dx, axis=0, keepdims=True)

    return _rows(fn, name, x.shape[0], tm, [x, dh, dres], [g], [(D, F32)], [((1, D), F32), ((1, D), F32)])


def _ffn_up(h, wg, wu, name, tm=1024, tn=512, tk=512):
    (M, K), N = h.shape, wg.shape[1]
    tm, tn, tk = _pick(M, tm), _pick(N, tn), _pick(K, tk)
    nk = K // tk

    def body(h_ref, wg_ref, wu_ref, g_ref, u_ref, a_ref, accg, accu):
        k = pl.program_id(2)

        @pl.when(k == 0)
        def _():
            accg[...] = jnp.zeros_like(accg)
            accu[...] = jnp.zeros_like(accu)

        hb = h_ref[...].astype(BF16)
        accg[...] += jnp.dot(hb, wg_ref[...].astype(BF16), preferred_element_type=F32)
        accu[...] += jnp.dot(hb, wu_ref[...].astype(BF16), preferred_element_type=F32)

        @pl.when(k == nk - 1)
        def _():
            g, u = accg[...], accu[...]
            g_ref[...] = g
            u_ref[...] = u
            a_ref[...] = (g * jax.nn.sigmoid(g) * u).astype(BF16)

    o_spec = pl.BlockSpec((tm, tn), lambda i, j, k: (i, j))
    w_spec = pl.BlockSpec((tk, tn), lambda i, j, k: (k, j))
    return pl.pallas_call(
        body, name=name, grid=(M // tm, N // tn, nk),
        in_specs=[pl.BlockSpec((tm, tk), lambda i, j, k: (i, k)), w_spec, w_spec], out_specs=[o_spec] * 3,
        out_shape=[jax.ShapeDtypeStruct((M, N), F32)] * 2 + [jax.ShapeDtypeStruct((M, N), BF16)],
        scratch_shapes=[pltpu.VMEM((tm, tn), F32)] * 2, compiler_params=_cp(("parallel", "parallel", "arbitrary")),
    )(h, wg, wu)


def _act_bwd(da, g, u, name, tm=256):
    def fn(i, da, g, u):
        s = jax.nn.sigmoid(g)
        return da * u * (s * (1.0 + g * (1.0 - s))), da * (g * s)

    F = g.shape[1]
    return _rows(fn, name, g.shape[0], tm, [da, g, u], [], [(F, BF16), (F, BF16)], [])


def _ffn_fwd(x, gain, wg, wu, wd, tag):
    h = _rms_fwd(x, gain, tag + "_norm")
    G, U, A = _ffn_up(h, wg, wu, tag + "_up")
    xo = _mm(A, wd, "nn", tag + "_down", alpha=0.5, res=x)
    return xo, (h, G, U, A)


def _ffn_bwd(x, gain, wg, wu, wd, saved, dxo, tag):
    h, G, U, A = saved
    dA = _mm(dxo, wd, "nt", tag + "_dA", alpha=0.5)
    dwd = _mm(A, dxo, "tn", tag + "_dwd", out_dtype=BF16, alpha=0.5)
    dG, dU = _act_bwd(dA, G, U, tag + "_act_bwd")
    dh = _mm(dG, wg, "nt", tag + "_dh_g")
    dh = _mm(dU, wu, "nt", tag + "_dh_u", res=dh)
    dwg = _mm(h, dG, "tn", tag + "_dwg", out_dtype=BF16)
    dwu = _mm(h, dU, "tn", tag + "_dwu", out_dtype=BF16)
    dx, dgain, _ = _rms_bwd(x, gain, dh, dxo, tag + "_norm_bwd")
    return dx, dgain, dwg, dwu, dwd


def _segsum64_impl(x):
    r = lax.broadcasted_iota(jnp.int32, (128, 128), 0) // HEAD
    c = lax.broadcasted_iota(jnp.int32, (128, 128), 1) // HEAD
    ones = (r == c).astype(BF16)
    hi = x.astype(BF16)
    lo = (x - hi.astype(F32)).astype(BF16)
    outs = []
    for q in range(x.shape[1] // 128):
        sl = slice(q * 128, (q + 1) * 128)
        outs.append(jnp.dot(hi[:, sl], ones, preferred_element_type=F32) + jnp.dot(lo[:, sl], ones, preferred_element_type=F32))
    return outs[0] if len(outs) == 1 else jnp.concatenate(outs, axis=1)


@jax.custom_vjp
def _segsum64(x):
    return _segsum64_impl(x)


_segsum64.defvjp(lambda x: (_segsum64_impl(x), None), lambda _, ct: (_segsum64_impl(ct),))


def _swap32(x):
    lane = lax.broadcasted_iota(jnp.int32, (x.shape[0], 128), 1)
    outs = [jnp.take_along_axis(x[:, q * 128:(q + 1) * 128], lane ^ 32, axis=1) for q in range(x.shape[1] // 128)]
    return outs[0] if len(outs) == 1 else jnp.concatenate(outs, axis=1)


def _tree_sum(xs):
    xs = list(xs)
    while len(xs) > 1:
        nxt = [xs[i] + xs[i + 1] for i in range(0, len(xs) - 1, 2)]
        if len(xs) % 2:
            nxt.append(xs[-1])
        xs = nxt
    return xs[0]


def _softplus(x):
    return jnp.maximum(x, 0.0) + jnp.log(1.0 + jnp.exp(-jnp.abs(x)))


def _pre_core(k, da, gd, w0, a0, k_k, k_a, w_da, gate_up):
    lane = lax.broadcasted_iota(jnp.int32, da.shape, 1)
    w_da = w_da.astype(BF16)
    l1 = jnp.dot(jnp.where(lane < DECAY_LORA, jnp.tanh(da), 0.0).astype(BF16), w_da, preferred_element_type=F32)
    l2 = jnp.dot(jnp.where(lane >= DECAY_LORA, da, 0.0).astype(BF16), w_da, preferred_element_type=F32)
    wlog = -_softplus(-(w0 + l1)) - 0.5
    decay = jnp.exp(-jnp.exp(wlog))
    a = jax.nn.sigmoid(a0 + l2)
    g = jnp.dot(jax.nn.sigmoid(gd).astype(BF16), gate_up.astype(BF16), preferred_element_type=F32)
    kk = k * k_k
    kkn = kk / jnp.maximum(jnp.sqrt(_segsum64(kk * kk)), 1e-12)
    k2 = k * (1.0 + (a - 1.0) * k_a)
    return decay, k2, -kkn, kkn * a, g


def _pre_shift(i, zr, zl, zr8, zl8, mu, mul):
    live = (i > 0).astype(F32)
    dz = _shift_down(zr, zr8[7:8, :] * live) - zr
    dzl = _shift_down(zl, zl8[7:8, :] * live) - zl
    return zr + dz * mu, zl + dzl * mul, dz, dzl


def _shift_down(x, first_row):
    rolled = pltpu.roll(x, 1, 0)
    row = lax.broadcasted_iota(jnp.int32, x.shape, 0)
    return jnp.where(row == 0, first_row, rolled)


def _shift_up(x, last_row):
    rolled = pltpu.roll(x, x.shape[0] - 1, 0)
    row = lax.broadcasted_iota(jnp.int32, x.shape, 0)
    return jnp.where(row == x.shape[0] - 1, last_row, rolled)


def _prev_rows_spec(tm, cols):
    return lambda idx: pl.BlockSpec((8, cols), lambda i: (jnp.maximum(idx(i) * (tm // 8) - 1, 0), 0))


def _rwkv_pre(p_rkv, p_lora, params, tm=256):
    T = p_rkv.shape[0]

    def fn(i, zr, zl, zr8, zl8, mu, mul, *ps):
        z, z2, _, _ = _pre_shift(i, zr, zl, zr8, zl8, mu, mul)
        decay, k2, an, bn, g = _pre_core(z[:, RW_W:2 * RW_W], z2[:, :128], z2[:, 128:], *ps)
        return z[:, :RW_W], decay, k2, z[:, 2 * RW_W:], an, bn, g

    extra = [(p_rkv, _prev_rows_spec(tm, 3 * RW_W)), (p_lora, _prev_rows_spec(tm, LORA_W))]
    return _rows(fn, "rwkv_pre", T, tm, [p_rkv, p_lora], list(params), [(RW_W, F32)] * 7, [], extra=extra)


def _rwkv_pre_bwd(p_rkv, p_lora, params, cts, tm=256):
    T = p_rkv.shape[0]
    n = T // tm

    def fn(i, zr, zl, cr, cdec, ck2, cv, can, cbn, cg, cr_b, ck2_b, cv_b, zr8, zl8, mu, mul, *rest):
        ps, (car, carl) = rest[:-2], rest[-2:]
        cr, ck2, cv = cr + cr_b, ck2 + ck2_b, cv + cv_b
        z, z2, dif, difl = _pre_shift(i, zr, zl, zr8, zl8, mu, mul)
        _, vjp = jax.vjp(_pre_core, z[:, RW_W:2 * RW_W], z2[:, :128], z2[:, 128:], *ps)
        dk, dda, dgd, *dps = vjp((cdec, ck2, can, cbn, cg))
        dz = jnp.concatenate([cr, dk, cv], axis=1)
        dz2 = jnp.concatenate([dda, dgd], axis=1)
        dzp, dzlp = dz * mu, dz2 * mul

        @pl.when(i == n - 1)
        def _():
            car[...] = jnp.zeros_like(car)
            carl[...] = jnp.zeros_like(carl)

        d_rkv = dz - dzp + _shift_up(dzp, car[0:1, :])
        d_lora = dz2 - dzlp + _shift_up(dzlp, carl[0:1, :])
        car[0:1, :] = dzp[0:1, :]
        carl[0:1, :] = dzlp[0:1, :]
        return (d_rkv, d_lora, jnp.sum(dz * dif, axis=0, keepdims=True), jnp.sum(dz2 * difl, axis=0, keepdims=True), *dps)

    extra = [(p_rkv, _prev_rows_spec(tm, 3 * RW_W)), (p_lora, _prev_rows_spec(tm, LORA_W))]
    acc = [(p.shape, F32) for p in params]
    return _rows(fn, "rwkv_pre_bwd", T, tm, [p_rkv, p_lora, *cts], list(params), [(3 * RW_W, BF16), (LORA_W, BF16)], acc,
                 extra=extra, reverse=True, scratch=[pltpu.VMEM((8, 3 * RW_W), F32), pltpu.VMEM((8, LORA_W), F32)])


def _post_core(y, r, k2, v, g, lw, lb, rk):
    mu = _segsum64(y) * (1.0 / HEAD)
    yc = y - mu
    var = _segsum64(yc * yc) * (1.0 / HEAD)
    yn = yc * lax.rsqrt(var + GN_EPS) * lw + lb
    return (yn + _segsum64(r * k2 * rk) * v) * g


def _rwkv_post(y, r, k2, v, g, lw, lb, rk, tm=256):
    (o,) = _rows(lambda i, *a: (_post_core(*a),), "rwkv_post", y.shape[0], tm, [y, r, k2, v, g], [lw, lb, rk], [(RW_W, BF16)], [])
    return o


def _rwkv_post_bwd(y, r, k2, v, g, lw, lb, rk, do, tm=256):
    def fn(i, y, r, k2, v, g, do, lw, lb, rk):
        _, vjp = jax.vjp(_post_core, y, r, k2, v, g, lw, lb, rk)
        return vjp(do.astype(F32))

    return _rows(fn, "rwkv_post_bwd", y.shape[0], tm, [y, r, k2, v, g, do], [lw, lb, rk], [(RW_W, F32)] * 5, [((1, RW_W), F32)] * 3)


SCAN_L = 32


def _to_tile(x):
    T = x.shape[0]
    return x.reshape(T, RW_HEADS, 8, 8).transpose(0, 2, 1, 3).reshape(T, 8, 128)


def _from_tile(x):
    T = x.shape[0]
    return x.reshape(T, 8, RW_HEADS, 8).transpose(0, 2, 1, 3).reshape(T, RW_W)


def _to_exp(x):
    T = x.shape[0]
    xt = x.reshape(T, RW_HEADS, HEAD).transpose(0, 2, 1)
    return jnp.broadcast_to(xt[..., None], (T, HEAD, RW_HEADS, 8)).reshape(T, HEAD, 128)


def _ck_a_to_b(ck):
    n = ck.shape[0]
    return ck.reshape(n, 8, 8, 8, RW_HEADS, 8).transpose(0, 3, 5, 1, 4, 2).reshape(n, HEAD, 8, 128)


def _scan_fwd(xe, vi):
    T, L = vi.shape[0], SCAN_L
    nch = T // L

    def body(xe_ref, vi_ref, yi_ref, sa_ref, ck_ref, st_ref):
        @pl.when(pl.program_id(0) == 0)
        def _():
            st_ref[...] = jnp.zeros_like(st_ref)

        ck_ref[0] = st_ref[...]

        def step(t, carry):
            v = vi_ref[t]
            row = lambda m, j: jnp.broadcast_to(xe_ref[t, m, pl.ds(j, 1), :], (8, 128))
            S = [st_ref[j] for j in range(HEAD)]
            sa = _tree_sum([S[j] * row(0, j) for j in range(HEAD)])
            sa_ref[t] = sa
            S = [S[j] * row(1, j) + row(2, j) * sa + row(3, j) * v for j in range(HEAD)]
            for j in range(HEAD):
                st_ref[j] = S[j]
            yi_ref[t] = _tree_sum([S[j] * row(4, j) for j in range(HEAD)])
            return carry

        lax.fori_loop(0, L, step, 0)

    tile = pl.BlockSpec((L, 8, 128), lambda c: (c, 0, 0))
    return pl.pallas_call(
        body, name="rwkv_scan_fwd", grid=(nch,),
        in_specs=[pl.BlockSpec((L, 5, HEAD, 128), lambda c: (c, 0, 0, 0)), tile],
        out_specs=[tile, tile, pl.BlockSpec((1, HEAD, 8, 128), lambda c: (c, 0, 0, 0))],
        out_shape=[jax.ShapeDtypeStruct((T, 8, 128), F32)] * 2 + [jax.ShapeDtypeStruct((nch, HEAD, 8, 128), F32)],
        scratch_shapes=[pltpu.VMEM((HEAD, 8, 128), F32)], compiler_params=_cp(("arbitrary",)),
    )(xe, vi)


def _scan_bwd_a(xe, dyi):
    T, L = dyi.shape[0], SCAN_L
    nch = T // L

    def body(xe_ref, dy_ref, dsa_ref, dv_ref, g_ref):
        @pl.when(pl.program_id(0) == 0)
        def _():
            g_ref[...] = jnp.zeros_like(g_ref)

        def step(s, carry):
            t = L - 1 - s
            dy = dy_ref[t]
            row = lambda m, j: jnp.broadcast_to(xe_ref[t, m, pl.ds(j, 1), :], (8, 128))
            G = [g_ref[j] + row(4, j) * dy for j in range(HEAD)]
            dsa = _tree_sum([G[j] * row(2, j) for j in range(HEAD)])
            dsa_ref[t] = dsa
            dv_ref[t] = _tree_sum([G[j] * row(3, j) for j in range(HEAD)])
            for j in range(HEAD):
                g_ref[j] = G[j] * row(1, j) + row(0, j) * dsa
            return carry

        lax.fori_loop(0, L, step, 0)

    tile = pl.BlockSpec((L, 8, 128), lambda c: (nch - 1 - c, 0, 0))
    return pl.pallas_call(
        body, name="rwkv_scan_bwd_a", grid=(nch,),
        in_specs=[pl.BlockSpec((L, 5, HEAD, 128), lambda c: (nch - 1 - c, 0, 0, 0)), tile], out_specs=[tile, tile],
        out_shape=[jax.ShapeDtypeStruct((T, 8, 128), F32)] * 2,
        scratch_shapes=[pltpu.VMEM((HEAD, 8, 128), F32)], compiler_params=_cp(("arbitrary",)),
    )(xe, dyi)


def _scan_bwd_b(xt, ie, ckb):
    T, L = xt.shape[0], SCAN_L
    nch = T // L

    def body(xt_ref, ie_ref, ck_ref, dj_ref, hist, g_ref):
        @pl.when(pl.program_id(0) == 0)
        def _():
            g_ref[...] = jnp.zeros_like(g_ref)

        hist[0] = ck_ref[0]

        def fstep(t, carry):
            x = xt_ref[t]
            w, B, k = x[1], x[2], x[3]
            row = lambda m, i: jnp.broadcast_to(ie_ref[t, m, pl.ds(i, 1), :], (8, 128))
            for i in range(HEAD):
                hist[t + 1, i] = hist[t, i] * w + row(1, i) * B + row(0, i) * k
            return carry

        lax.fori_loop(0, L, fstep, 0)

        def bstep(s, carry):
            t = L - 1 - s
            x = xt_ref[t]
            A, w, r = x[0], x[1], x[4]
            row = lambda m, i: jnp.broadcast_to(ie_ref[t, m, pl.ds(i, 1), :], (8, 128))
            G = [g_ref[i] + row(2, i) * r for i in range(HEAD)]
            Sp = [hist[t, i] for i in range(HEAD)]
            dj_ref[t, 4] = _tree_sum([hist[t + 1, i] * row(2, i) for i in range(HEAD)])
            dj_ref[t, 1] = _tree_sum([G[i] * Sp[i] for i in range(HEAD)])
            dj_ref[t, 2] = _tree_sum([G[i] * row(1, i) for i in range(HEAD)])
            dj_ref[t, 3] = _tree_sum([G[i] * row(0, i) for i in range(HEAD)])
            dj_ref[t, 0] = _tree_sum([Sp[i] * row(3, i) for i in range(HEAD)])
            for i in range(HEAD):
                g_ref[i] = G[i] * w + row(3, i) * A
            return carry

        lax.fori_loop(0, L, bstep, 0)

    rev = lambda c: (nch - 1 - c, 0, 0, 0)
    return pl.pallas_call(
        body, name="rwkv_scan_bwd_b", grid=(nch,),
        in_specs=[pl.BlockSpec((L, 5, 8, 128), rev), pl.BlockSpec((L, 4, HEAD, 128), rev), pl.BlockSpec((1, HEAD, 8, 128), rev)],
        out_specs=pl.BlockSpec((L, 5, 8, 128), rev), out_shape=jax.ShapeDtypeStruct((T, 5, 8, 128), F32),
        scratch_shapes=[pltpu.VMEM((L + 1, HEAD, 8, 128), F32), pltpu.VMEM((HEAD, 8, 128), F32)], compiler_params=_cp(("arbitrary",)),
    )(xt, ie, ckb)


SWA_COLS = SWA_W + 2 * KV_W
BLK = 128


def _swa_core(n, k2a, k2b, vla, vra, vlb, vrb, sinks, *qps):
    iq = lax.broadcasted_iota(jnp.int32, (BLK, 2 * BLK), 0)
    ik = lax.broadcasted_iota(jnp.int32, (BLK, 2 * BLK), 1)
    diff = BLK + iq - ik
    valid = (diff >= 0) & (diff < WINDOW) & ((n > 0) | (ik >= BLK))
    lane = lax.broadcasted_iota(jnp.int32, (BLK, 128), 1)
    lane1 = lax.broadcasted_iota(jnp.int32, (1, 128), 1)
    nt = (((1,), (1,)), ((), ()))
    outs = []
    for pp in range(8):
        k2, vl, vr = (k2a, vla, vra) if pp < 4 else (k2b, vlb, vrb)
        qp = qps[pp]
        o = None
        for half, vv in ((0, vl), (1, vr)):
            qh = jnp.where((lane >= HEAD) == (half == 1), qp, 0.0).astype(BF16)
            s = lax.dot_general(qh, k2.astype(BF16), nt, preferred_element_type=F32) * (HEAD ** -0.5)
            s = jnp.where(valid, s, NEG_INF)
            sink = jnp.sum(jnp.where(lane1 == 2 * pp + half, sinks, 0.0), axis=1, keepdims=True)
            m = jnp.maximum(jnp.max(s, axis=1, keepdims=True), sink)
            p = jnp.exp(s - m)
            den = jnp.sum(p, axis=1, keepdims=True) + jnp.exp(sink - m)
            oh = jnp.dot((p / den).astype(BF16), vv.astype(BF16), preferred_element_type=F32)
            o = oh if o is None else o + oh
        outs.append(o)
    return jnp.concatenate(outs, axis=1)


def _swa_prep(pc, pp, b, cq, sq, ckc, skc, ckp, skp):
    zc, zp = pc + b, pp + b
    qr = zc[:, :SWA_W] * cq + _swap32(zc[:, :SWA_W]) * sq
    kc, kp = zc[:, SWA_W:SWA_W + KV_W], zp[:, SWA_W:SWA_W + KV_W]
    kb = jnp.concatenate([kp * ckp + _swap32(kp) * skp, kc * ckc + _swap32(kc) * skc], axis=0)
    vb = jnp.concatenate([zp[:, SWA_W + KV_W:], zc[:, SWA_W + KV_W:]], axis=0)
    lane = lax.broadcasted_iota(jnp.int32, kb.shape, 1)
    left = lane < HEAD
    kbr, vbr = pltpu.roll(kb, HEAD, 1), pltpu.roll(vb, HEAD, 1)
    return (jnp.where(left, kb, kbr), jnp.where(left, kbr, kb), jnp.where(left, vb, 0.0), jnp.where(left, 0.0, vbr),
            jnp.where(left, vbr, 0.0), jnp.where(left, 0.0, vb)), [qr[:, q * 128:(q + 1) * 128] for q in range(8)]


def _swa_specs(T, tabs_q, tabs_k):
    cur = lambda c: pl.BlockSpec((BLK, c), lambda n: (n, 0))
    prev = lambda c: pl.BlockSpec((BLK, c), lambda n: (jnp.maximum(n - 1, 0), 0))
    return cur, prev


def _swa_fwd(p_swa, b, sinks, cq, sq, ck, sk):
    T = p_swa.shape[0]
    cur, prev = _swa_specs(T, None, None)

    def body(pc, pp, b_ref, s_ref, cq_r, sq_r, ckc, skc, ckp, skp, o_ref):
        ops, qps = _swa_prep(pc[...], pp[...], b_ref[...], cq_r[...], sq_r[...], ckc[...], skc[...], ckp[...], skp[...])
        o_ref[...] = _swa_core(pl.program_id(0), *ops, s_ref[...], *qps).astype(o_ref.dtype)

    full = lambda a: pl.BlockSpec(a.shape, lambda n: (0, 0))
    return pl.pallas_call(
        body, name="swa_fwd", grid=(T // BLK,),
        in_specs=[cur(SWA_COLS), prev(SWA_COLS), full(b), full(sinks), cur(SWA_W), cur(SWA_W), cur(KV_W), cur(KV_W), prev(KV_W), prev(KV_W)],
        out_specs=cur(SWA_W), out_shape=jax.ShapeDtypeStruct((T, SWA_W), BF16), compiler_params=_cp(("arbitrary",)),
    )(p_swa, p_swa, b, sinks, cq, sq, ck, sk, ck, sk)


def _swa_bwd(p_swa, b, sinks, cq, sq, ck, sk, do):
    T = p_swa.shape[0]
    nb = T // BLK
    cur = lambda c: pl.BlockSpec((BLK, c), lambda s: (nb - 1 - s, 0))
    prev = lambda c: pl.BlockSpec((BLK, c), lambda s: (jnp.maximum(nb - 2 - s, 0), 0))

    def body(pc, pp, b_ref, s_ref, cq_r, sq_r, ckc, skc, ckp, skp, do_ref, dcur, db, dsk, carry):
        step = pl.program_id(0)
        n = nb - 1 - step

        @pl.when(step == 0)
        def _():
            carry[...] = jnp.zeros_like(carry)
            db[...] = jnp.zeros_like(db)
            dsk[...] = jnp.zeros_like(dsk)

        ops, qps = _swa_prep(pc[...], pp[...], b_ref[...], cq_r[...], sq_r[...], ckc[...], skc[...], ckp[...], skp[...])
        _, vjp = jax.vjp(functools.partial(_swa_core, n), *ops, s_ref[...], *qps)
        dk2a, dk2b, dvla, dvra, dvlb, dvrb, dsinks, *dqps = vjp(do_ref[...].astype(F32))
        dqr = jnp.concatenate(dqps, axis=1)
        lane = lax.broadcasted_iota(jnp.int32, dk2a.shape, 1)
        left = lane < HEAD
        dkb = jnp.where(left, dk2a + pltpu.roll(dk2a, HEAD, 1), dk2b + pltpu.roll(dk2b, HEAD, 1))
        dvb = jnp.where(left, dvla + pltpu.roll(dvra, HEAD, 1), pltpu.roll(dvlb, HEAD, 1) + dvrb)
        dq = dqr * cq_r[...] + _swap32(dqr * sq_r[...])
        dkp, dkc = dkb[:BLK], dkb[BLK:]
        dkp = dkp * ckp[...] + _swap32(dkp * skp[...])
        dkc = dkc * ckc[...] + _swap32(dkc * skc[...])
        dc = jnp.concatenate([dq, jnp.concatenate([dkc, dvb[BLK:]], axis=1) + carry[...]], axis=1)
        carry[...] = jnp.concatenate([dkp, dvb[:BLK]], axis=1)
        dcur[...] = dc.astype(dcur.dtype)
        db[...] += jnp.sum(dc, axis=0, keepdims=True)
        dsk[...] += dsinks

    full = lambda a: pl.BlockSpec(a.shape, lambda s: (0, 0))
    return pl.pallas_call(
        body, name="swa_bwd", grid=(nb,),
        in_specs=[cur(SWA_COLS), prev(SWA_COLS), full(b), full(sinks), cur(SWA_W), cur(SWA_W), cur(KV_W), cur(KV_W), prev(KV_W), prev(KV_W),
                  cur(SWA_W)],
        out_specs=[cur(SWA_COLS), full(b), full(sinks)],
        out_shape=[jax.ShapeDtypeStruct((T, SWA_COLS), BF16), jax.ShapeDtypeStruct(b.shape, F32), jax.ShapeDtypeStruct(sinks.shape, F32)],
        scratch_shapes=[pltpu.VMEM((BLK, 2 * KV_W), F32)], compiler_params=_cp(("arbitrary",)),
    )(p_swa, p_swa, b, sinks, cq, sq, ck, sk, ck, sk, do)


def _rope_tables(T):
    inv = 10000.0 ** (-jnp.arange(0, HEAD, 2, dtype=F32) / HEAD)
    ang = jnp.arange(T, dtype=F32)[:, None] * inv[None, :]
    c = jnp.concatenate([jnp.cos(ang), jnp.cos(ang)], axis=1)
    s = jnp.concatenate([-jnp.sin(ang), jnp.sin(ang)], axis=1)
    return jnp.tile(c, (1, 16)), jnp.tile(s, (1, 16)), jnp.tile(c, (1, 2)), jnp.tile(s, (1, 2))


def _xattn_core(*qkv):
    outs = []
    for h in range(XH):
        qh, kh, vh = qkv[h], qkv[XH + h], qkv[2 * XH + h]
        s = lax.dot_general(qh.astype(BF16), kh.astype(BF16), (((1,), (1,)), ((), ())), preferred_element_type=F32) * (XHD ** -0.5)
        p = jnp.exp(s - jnp.max(s, axis=1, keepdims=True))
        p = p / jnp.sum(p, axis=1, keepdims=True)
        outs.append(jnp.dot(p.astype(BF16), vh.astype(BF16), preferred_element_type=F32))
    return jnp.concatenate(outs, axis=1)


def _xattn_split(q, kv):
    return [q[:, h * XHD:(h + 1) * XHD] for h in range(XH)] + [kv[:, h * XHD:(h + 1) * XHD] for h in range(2 * XH)]


def _xattn_fwd(q, kv, tm=256):
    (o,) = _rows(lambda i, q, kv: (_xattn_core(*_xattn_split(q, kv)),), "xattn_fwd", q.shape[0], tm, [q], [kv], [(q.shape[1], BF16)], [])
    return o


def _xattn_bwd(q, kv, do, tm=256):
    def fn(i, q, do, kv):
        _, vjp = jax.vjp(_xattn_core, *_xattn_split(q, kv))
        d = vjp(do.astype(F32))
        return jnp.concatenate(d[:XH], axis=1), jnp.concatenate(d[XH:], axis=1)

    return _rows(fn, "xattn_bwd", q.shape[0], tm, [q, do], [kv], [(q.shape[1], BF16)], [(kv.shape, F32)])


def _loss_head(x, g, tgt, tm=256):
    D = x.shape[1]

    def fn(i, x, tgt, g):
        y, vjp = jax.vjp(_rms, x, g)
        err = y - tgt
        dx, dg = vjp(err * (1.0 / D))
        part = 0.5 / D * jnp.sum(jnp.sum(err * err, axis=1, keepdims=True), axis=0, keepdims=True)
        return dx, jnp.broadcast_to(part, (1, 128)), dg

    return _rows(fn, "loss_head", x.shape[0], tm, [x, tgt], [g], [(D, F32)], [((1, 128), F32), ((1, D), F32)])


def _local_step(x, mem, tgt, W, P):
    T = x.shape[0]
    x1, s1 = _ffn_fwd(x, P["f1_norm"], W["f1_gate"], W["f1_up"], W["f1_down"], "f1")

    h2 = _rms_fwd(x1, P["mix_norm"], "mix_norm")
    w_rkv, w_lora, w_swa = W["w_inT"][:3 * RW_W], W["w_inT"][3 * RW_W:SHIFT_COLS], W["w_inT"][SHIFT_COLS:]
    p_rkv = _mm(h2, w_rkv, "nt", "in_rkv")
    p_lora = _mm(h2, w_lora, "nt", "in_lora")
    p_swa = _mm(h2, w_swa, "nt", "in_swa")
    w_da = jnp.concatenate([P["rw_decay_up"], P["rw_aaa_up"]], axis=0)
    pre_params = (P["rw_mu"][:, :3 * RW_W], P["rw_mu"][:, 3 * RW_W:], P["rw_w0"], P["rw_a0"], P["rw_k_k"], P["rw_k_a"], w_da,
                  P["rw_gate_up"])
    r, decay, k2, v, an, bn, g = _rwkv_pre(p_rkv, p_lora, pre_params)
    scan_vecs = (an, decay, bn, k2, r)
    xe = jnp.stack([_to_exp(a) for a in scan_vecs], axis=1)
    yi, sai, ck = _scan_fwd(xe, _to_tile(v))
    y_scan = _from_tile(yi)
    y_rw = _rwkv_post(y_scan, r, k2, v, g, P["rw_lnx_w"], P["rw_lnx_b"], P["rw_r_k"])
    cq, sq, ckt, skt = _rope_tables(T)
    y_swa = _swa_fwd(p_swa, P["b_in_attn"], P["attn_sinks"], cq, sq, ckt, skt)
    ycat = jnp.concatenate([y_rw, y_swa], axis=1)
    x2 = _mm(ycat, W["w_out"], "nn", "out_proj", res=x1, bias=P["b_out"])

    hx = _rms_fwd(x2, P["xa_norm"], "xa_norm")
    mn = _rms_fwd(mem, P["mem_norm"], "mem_norm")
    q = _mm(hx, W["w_xq"], "nn", "xq", out_dtype=BF16)
    kv = _mm(mn, W["w_xkv"], "nn", "xkv", out_dtype=BF16)
    o = _xattn_fwd(q, kv)
    x3 = _mm(o, W["w_xo"], "nn", "xo", res=x2)

    x4, s2 = _ffn_fwd(x3, P["f2_norm"], W["f2_gate"], W["f2_up"], W["f2_down"], "f2")
    dx4, loss_part, d_final = _loss_head(x4, P["final_norm"], tgt)

    gw, gs = {}, {"final_norm": d_final}
    dx3, gs["f2_norm"], gw["f2_gate"], gw["f2_up"], gw["f2_down"] = _ffn_bwd(
        x3, P["f2_norm"], W["f2_gate"], W["f2_up"], W["f2_down"], s2, dx4, "f2")

    do = _mm(dx3, W["w_xo"], "nt", "xo_do", out_dtype=BF16)
    gw["w_xo"] = _mm(o, dx3, "tn", "xo_dw", out_dtype=BF16)
    dq, dkv = _xattn_bwd(q, kv, do)
    gw["w_xq"] = _mm(hx, dq, "tn", "xq_dw", out_dtype=BF16)
    dhx = _mm(dq, W["w_xq"], "nt", "xq_dh")
    gw["w_xkv"] = _mm(mn, dkv, "tn", "xkv_dw", out_dtype=BF16)
    dmn = _mm(dkv, W["w_xkv"], "nt", "xkv_dmn")
    _, gs["mem_norm"], _ = _rms_bwd(mem, P["mem_norm"], dmn, jnp.zeros_like(mem), "mem_norm_bwd")
    dx2, gs["xa_norm"], gs["b_out"] = _rms_bwd(x2, P["xa_norm"], dhx, dx3, "xa_norm_bwd")

    dycat = _mm(dx2, W["w_out"], "nt", "out_dy")
    gw["w_out"] = _mm(ycat, dx2, "tn", "out_dw", out_dtype=BF16)
    dp_swa, gs["b_in_attn"], gs["attn_sinks"] = _swa_bwd(p_swa, P["b_in_attn"], P["attn_sinks"], cq, sq, ckt, skt, dycat[:, RW_W:])
    dy_scan, dr_b, dk2_b, dv_b, dg, gs["rw_lnx_w"], gs["rw_lnx_b"], gs["rw_r_k"] = _rwkv_post_bwd(
        y_scan, r, k2, v, g, P["rw_lnx_w"], P["rw_lnx_b"], P["rw_r_k"], dycat[:, :RW_W])
    dsai, dvi = _scan_bwd_a(xe, _to_tile(dy_scan))
    xt = jnp.stack([_to_tile(a) for a in scan_vecs], axis=1)
    ie = jnp.stack([_to_exp(v), _to_exp(_from_tile(sai)), _to_exp(dy_scan), _to_exp(_from_tile(dsai))], axis=1)
    dj = _scan_bwd_b(xt, ie, _ck_a_to_b(ck))
    dan, ddecay, dbn, dk2_s, dr_s = (_from_tile(dj[:, m]) for m in range(5))
    cts = (dr_s, ddecay, dk2_s, _from_tile(dvi), dan, dbn, dg, dr_b, dk2_b, dv_b)
    dp_rkv, dp_lora, dmu, dmul, gs["rw_w0"], gs["rw_a0"], gs["rw_k_k"], gs["rw_k_a"], dw_da, gs["rw_gate_up"] = _rwkv_pre_bwd(
        p_rkv, p_lora, pre_params, cts)
    gs["rw_mu"] = jnp.concatenate([dmu, dmul], axis=1)
    gs["rw_decay_up"], gs["rw_aaa_up"] = dw_da[:DECAY_LORA], dw_da[DECAY_LORA:]
    gw["w_inT"] = jnp.concatenate([_mm(dp_rkv, h2, "tn", "in_dw_rkv"), _mm(dp_lora, h2, "tn", "in_dw_lora"),
                                   _mm(dp_swa, h2, "tn", "in_dw_swa")], axis=0)
    dh2 = _mm(dp_rkv, w_rkv, "nn", "in_dh_rkv")
    dh2 = _mm(dp_lora, w_lora, "nn", "in_dh_lora", res=dh2)
    dh2 = _mm(dp_swa, w_swa, "nn", "in_dh_swa", res=dh2)
    dx1, gs["mix_norm"], _ = _rms_bwd(x1, P["mix_norm"], dh2, dx2, "mix_norm_bwd")

    dx0, gs["f1_norm"], gw["f1_gate"], gw["f1_up"], gw["f1_down"] = _ffn_bwd(
        x, P["f1_norm"], W["f1_gate"], W["f1_up"], W["f1_down"], s1, dx1, "f1")
    return loss_part, dx0, gw, gs


_ANY = pl.BlockSpec(memory_space=pl.ANY)
_OTHER_CHIPS = ((1, 0), (0, 1), (1, 1))


def _mesh_pos():
    return lax.axis_index("x"), lax.axis_index("y"), lax.axis_index("c")


def _slot(ref, kind, s, rows, cols):
    if kind == "row":
        return ref.at[pl.ds(pl.multiple_of(s * rows, 8), rows), :]
    return ref.at[:, pl.ds(pl.multiple_of(s * cols, 128), cols)]


def _gather_weights(shards, kinds):
    n = len(shards)
    out_shape = [jax.ShapeDtypeStruct((4 * s.shape[0], s.shape[1]) if k == "row" else (s.shape[0], 4 * s.shape[1]), s.dtype)
                 for s, k in zip(shards, kinds)]

    def body(*refs):
        ins, outs = refs[:n], refs[n:2 * n]
        send, recv, loc = refs[2 * n:]
        x, y, c = _mesh_pos()
        me = 2 * x + y
        copies = []
        for i in range(n):
            rows, cols = ins[i].shape
            mine = _slot(outs[i], kinds[i], me, rows, cols)
            cp = pltpu.make_async_copy(ins[i], mine, loc.at[i])
            cp.start()
            copies.append(cp)
            for r, (dx, dy) in enumerate(_OTHER_CHIPS):
                rc = pltpu.make_async_remote_copy(ins[i], mine, send.at[3 * i + r], recv.at[3 * i + r],
                                                  device_id=((x + dx) % 2, (y + dy) % 2, c), device_id_type=MESH)
                rc.start()
                copies.append(rc)
        for cp in copies:
            cp.wait()

    return pl.pallas_call(
        body, name="gather_weights", in_specs=[_ANY] * n, out_specs=[_ANY] * n, out_shape=out_shape,
        scratch_shapes=[pltpu.SemaphoreType.DMA((3 * n,)), pltpu.SemaphoreType.DMA((3 * n,)), pltpu.SemaphoreType.DMA((n,))],
    )(*shards)


def _scatter_grads(grads, kinds):
    n = len(grads)
    shard_shape = [(g.shape[0] // 4, g.shape[1]) if k == "row" else (g.shape[0], g.shape[1] // 4) for g, k in zip(grads, kinds)]
    out_shape = [jax.ShapeDtypeStruct((4, *s), g.dtype) for s, g in zip(shard_shape, grads)]

    def body(*refs):
        ins, outs = refs[:n], refs[n:2 * n]
        send, recv, loc = refs[2 * n:]
        x, y, c = _mesh_pos()
        me = 2 * x + y
        copies = []
        for i in range(n):
            rows, cols = shard_shape[i]
            cp = pltpu.make_async_copy(_slot(ins[i], kinds[i], me, rows, cols), outs[i].at[me], loc.at[i])
            cp.start()
            copies.append(cp)
            for r, (dx, dy) in enumerate(_OTHER_CHIPS):
                tx, ty = (x + dx) % 2, (y + dy) % 2
                rc = pltpu.make_async_remote_copy(_slot(ins[i], kinds[i], 2 * tx + ty, rows, cols), outs[i].at[me],
                                                  send.at[3 * i + r], recv.at[3 * i + r], device_id=(tx, ty, c), device_id_type=MESH)
                rc.start()
                copies.append(rc)
        for cp in copies:
            cp.wait()

    return pl.pallas_call(
        body, name="scatter_grads", in_specs=[_ANY] * n, out_specs=[_ANY] * n, out_shape=out_shape,
        scratch_shapes=[pltpu.SemaphoreType.DMA((3 * n,)), pltpu.SemaphoreType.DMA((3 * n,)), pltpu.SemaphoreType.DMA((n,))],
    )(*grads)


def _swap_with_sibling(arrs):
    n = len(arrs)

    def body(*refs):
        ins, outs = refs[:n], refs[n:2 * n]
        send, recv = refs[2 * n:]
        x, y, c = _mesh_pos()
        copies = []
        for i in range(n):
            rc = pltpu.make_async_remote_copy(ins[i], outs[i], send.at[i], recv.at[i], device_id=(x, y, 1 - c), device_id_type=MESH)
            rc.start()
            copies.append(rc)
        for rc in copies:
            rc.wait()

    return pl.pallas_call(
        body, name="swap_with_sibling", in_specs=[_ANY] * n, out_specs=[_ANY] * n,
        out_shape=[jax.ShapeDtypeStruct(a.shape, a.dtype) for a in arrs],
        scratch_shapes=[pltpu.SemaphoreType.DMA((n,)), pltpu.SemaphoreType.DMA((n,))],
    )(*arrs)


def _gather_small(pack):
    def body(in_ref, out_ref, send, recv, loc):
        x, y, c = _mesh_pos()
        me = 4 * x + 2 * y + c
        cp = pltpu.make_async_copy(in_ref, out_ref.at[me], loc.at[0])
        cp.start()
        copies = [cp]
        for r in range(1, 8):
            dx, dy, dc = r // 4, (r // 2) % 2, r % 2
            rc = pltpu.make_async_remote_copy(in_ref, out_ref.at[me], send.at[r - 1], recv.at[r - 1],
                                              device_id=((x + dx) % 2, (y + dy) % 2, (c + dc) % 2), device_id_type=MESH)
            rc.start()
            copies.append(rc)
        for cp in copies:
            cp.wait()

    return pl.pallas_call(
        body, name="gather_small", in_specs=[_ANY], out_specs=_ANY, out_shape=jax.ShapeDtypeStruct((8, *pack.shape), pack.dtype),
        scratch_shapes=[pltpu.SemaphoreType.DMA((7,)), pltpu.SemaphoreType.DMA((7,)), pltpu.SemaphoreType.DMA((1,))],
    )(pack)


def _row_tile(R, dtype, target=256):
    mult = 8 * 4 // jnp.dtype(dtype).itemsize
    best = R
    for t in range(mult, min(R, target) + 1, mult):
        if R % t == 0:
            best = t
    return best


def _sum_slots(stack, name, out_dtype=F32):
    k, R, C = stack.shape
    tr = _row_tile(R, stack.dtype)

    def body(s_ref, o_ref):
        acc = s_ref[0].astype(F32)
        for j in range(1, k):
            acc = acc + s_ref[j].astype(F32)
        o_ref[...] = acc.astype(out_dtype)

    return pl.pallas_call(
        body, name=name, grid=(R // tr,), in_specs=[pl.BlockSpec((k, tr, C), lambda i: (0, i, 0))],
        out_specs=pl.BlockSpec((tr, C), lambda i: (i, 0)), out_shape=jax.ShapeDtypeStruct((R, C), out_dtype),
        compiler_params=_cp(("parallel",)),
    )(stack)


def _adamw(w, m, v, ga, gb, name):
    R, C = w.shape
    tr = _row_tile(R, F32, 128)
    gs = [ga] if gb is None else [ga, gb]

    def body(*refs):
        w_ref, m_ref, v_ref = refs[:3]
        g = refs[3][...]
        if gb is not None:
            g = g + refs[4][...]
        g_ref, d_ref, nm_ref, nv_ref = refs[-4:]
        nm = ADAM_B1 * m_ref[...] + (1.0 - ADAM_B1) * g
        nv = ADAM_B2 * v_ref[...] + (1.0 - ADAM_B2) * (g * g)
        m_hat = nm / (1.0 - ADAM_B1 ** ADAM_STEP)
        v_hat = nv / (1.0 - ADAM_B2 ** ADAM_STEP)
        g_ref[...] = g
        d_ref[...] = -ADAM_LR * (m_hat / (jnp.sqrt(v_hat) + ADAM_EPS) + ADAM_WD * w_ref[...])
        nm_ref[...] = nm
        nv_ref[...] = nv

    spec = pl.BlockSpec((tr, C), lambda i: (i, 0))
    return pl.pallas_call(
        body, name=name, grid=(R // tr,), in_specs=[spec] * (3 + len(gs)), out_specs=[spec] * 4,
        out_shape=[jax.ShapeDtypeStruct((R, C), F32)] * 4, compiler_params=_cp(("parallel",)),
    )(w, m, v, *gs)


def _pack(arrs):
    rows = []
    for a in arrs:
        flat = a.reshape(-1)
        rows.append(jnp.pad(flat, (0, -flat.shape[0] % 1024)).reshape(-1, 1024))
    p = jnp.concatenate(rows, axis=0)
    return jnp.pad(p, ((0, -p.shape[0] % 8), (0, 0)))


def _unpack(p, shapes):
    out, r = [], 0
    for s in shapes:
        n = 1
        for d in s:
            n *= d
        nr = -(-n // 1024)
        out.append(p[r:r + nr].reshape(-1)[:n].reshape(s))
        r += nr
    return out


BIG = ("f1_gate", "f1_up", "f1_down", "w_in", "w_out", "w_xq", "w_xkv", "w_xo", "f2_gate", "f2_up", "f2_down")
BIG_KIND = {"f1_gate": "col", "f1_up": "col", "f1_down": "row", "w_in": "row", "w_out": "row", "w_xq": "row", "w_xkv": "col",
            "w_xo": "row", "f2_gate": "col", "f2_up": "col", "f2_down": "row"}
LORA = ("rw_decay_up", "rw_aaa_up", "rw_gate_up")
WEIGHTS = ("f1_norm", "f1_gate", "f1_up", "f1_down", "mix_norm", "w_in", "b_in_attn", "rw_mu", "rw_w0", "rw_decay_up", "rw_a0",
           "rw_aaa_up", "rw_gate_up", "rw_k_k", "rw_k_a", "rw_r_k", "rw_lnx_w", "rw_lnx_b", "attn_sinks", "w_out", "b_out", "xa_norm",
           "mem_norm", "w_xq", "w_xkv", "w_xo", "f2_norm", "f2_gate", "f2_up", "f2_down", "final_norm")
SMALL = tuple(n for n in WEIGHTS if n not in BIG)


def kernel(x, mem, f1_norm, f1_gate, f1_up, f1_down, mix_norm, w_in, b_in_attn, rw_mu, rw_w0, rw_decay_up, rw_a0, rw_aaa_up, rw_gate_up, rw_k_k, rw_k_a, rw_r_k, rw_lnx_w, rw_lnx_b, attn_sinks, w_out, b_out, xa_norm, mem_norm, w_xq, w_xkv, w_xo, f2_norm, f2_gate, f2_up, f2_down, final_norm, loss_target, m_f1_norm, m_f1_gate, m_f1_up, m_f1_down, m_mix_norm, m_w_in, m_b_in_attn, m_rw_mu, m_rw_w0, m_rw_decay_up, m_rw_a0, m_rw_aaa_up, m_rw_gate_up, m_rw_k_k, m_rw_k_a, m_rw_r_k, m_rw_lnx_w, m_rw_lnx_b, m_attn_sinks, m_w_out, m_b_out, m_xa_norm, m_mem_norm, m_w_xq, m_w_xkv, m_w_xo, m_f2_norm, m_f2_gate, m_f2_up, m_f2_down, m_final_norm, v_f1_norm, v_f1_gate, v_f1_up, v_f1_down, v_mix_norm, v_w_in, v_b_in_attn, v_rw_mu, v_rw_w0, v_rw_decay_up, v_rw_a0, v_rw_aaa_up, v_rw_gate_up, v_rw_k_k, v_rw_k_a, v_rw_r_k, v_rw_lnx_w, v_rw_lnx_b, v_attn_sinks, v_w_out, v_b_out, v_xa_norm, v_mem_norm, v_w_xq, v_w_xkv, v_w_xo, v_f2_norm, v_f2_gate, v_f2_up, v_f2_down, v_final_norm):
    a = dict(locals())
    w = {n: a[n] for n in WEIGHTS}
    m = {n: a["m_" + n] for n in WEIGHTS}
    v = {n: a["v_" + n] for n in WEIGHTS}
    sq = lambda t: t.reshape(t.shape[-2:]) if t.ndim == 3 else t.reshape(1, -1)

    shards, kinds = [], []
    for n in BIG:
        shards.append(sq(w[n]).T if n == "w_in" else sq(w[n]).astype(BF16))
        kinds.append(BIG_KIND[n])
    for n in LORA:
        shards.append(sq(w[n]))
        kinds.append("col")
    full = _gather_weights(shards, kinds)
    W = {("w_inT" if n == "w_in" else n): f for n, f in zip(BIG, full)}
    P = {n: sq(w[n]) for n in SMALL if n not in LORA}
    P["attn_sinks"] = jnp.pad(P["attn_sinks"], ((0, 0), (0, 128 - P["attn_sinks"].shape[1])))
    P["rw_r_k"] = w["rw_r_k"].reshape(1, RW_W)
    for n, f in zip(LORA, full[len(BIG):]):
        P[n] = f

    loss_part, grad_x, gw, gs = _local_step(x[0], mem[0], loss_target[0], W, P)
    loss = lax.psum(loss_part[0, 0], ("x", "y", "c"))

    gnames = [("w_inT" if n == "w_in" else n) for n in BIG]
    stacks = _scatter_grads([gw[n] for n in gnames], [BIG_KIND[n] for n in BIG])
    partial = [_sum_slots(s, "sum_chips_" + n) for s, n in zip(stacks, BIG)]
    sibling = _swap_with_sibling(partial)

    gs["attn_sinks"] = gs["attn_sinks"][:, :16]
    gsum = _sum_slots(_gather_small(_pack([gs[n] for n in SMALL])), "sum_small")
    g_small = dict(zip(SMALL, _unpack(gsum, [gs[n].shape for n in SMALL])))
    shard = 2 * lax.axis_index("x") + lax.axis_index("y")
    for n in LORA:
        cols = w[n].shape[-1]
        g_small[n] = lax.dynamic_slice_in_dim(g_small[n], shard * cols, cols, axis=1)

    out = {}
    for n, pa, sb in zip(BIG, partial, sibling):
        if n == "w_in":
            pa, sb = pa.T, sb.T
        out[n] = _adamw(sq(w[n]), sq(m[n]), sq(v[n]), pa, sb, "adamw_" + n)
    flat = lambda d: _pack([d[n] for n in SMALL])
    res = _adamw(flat(w), flat(m), flat(v), _pack([g_small[n] for n in SMALL]), None, "adamw_small")
    shapes = [w[n].shape for n in SMALL]
    for k, p in enumerate(res):
        for n, t in zip(SMALL, _unpack(p, shapes)):
            out.setdefault(n, [None] * 4)[k] = t
    outs = [loss, grad_x.reshape(x.shape)]
    for k in range(4):
        outs += [out[n][k].reshape(w[n].shape) for n in WEIGHTS]
    return tuple(outs)
```

```python
import functools

import jax
import jax.numpy as jnp
from jax import lax
from jax.experimental import pallas as pl
from jax.experimental.pallas import tpu as pltpu

F32, BF16 = jnp.float32, jnp.bfloat16
MESH = pl.DeviceIdType.MESH

HEAD = 64
RW_HEADS = 16
RW_W = 1024
SWA_W = 1024
KV_W = 128
DECAY_LORA, AAA_LORA, GATE_LORA = 64, 64, 160
LORA_W = DECAY_LORA + AAA_LORA + GATE_LORA
SHIFT_COLS = 3 * RW_W + LORA_W
XH = 4
XHD = 512
MEM_LEN = 256
WINDOW = 128
GN_EPS = 64e-5
RMS_EPS = 1e-6
NEG_INF = -1e30
ADAM_LR, ADAM_B1, ADAM_B2, ADAM_EPS, ADAM_WD, ADAM_STEP = 0.001, 0.9, 0.999, 1e-08, 0.01, 10

VMEM_LIMIT = 56 * 1024 * 1024


def _cp(sem=None, **kw):
    return pltpu.CompilerParams(dimension_semantics=sem, vmem_limit_bytes=VMEM_LIMIT, **kw)


def _pick(dim, target):
    if dim <= target:
        return dim
    best = None
    for t in range(128, target + 1, 128):
        if dim % t == 0:
            best = t
    assert best is not None, (dim, target)
    return best


_DIMS = {"nn": (((1,), (0,)), ((), ())), "nt": (((1,), (1,)), ((), ())), "tn": (((0,), (0,)), ((), ()))}


def _mm(a, b, mode, name, out_dtype=F32, alpha=1.0, res=None, bias=None, tm=1024, tn=1024, tk=512):
    if mode == "nn":
        (M, K), (K2, N) = a.shape, b.shape
    elif mode == "nt":
        (M, K), (N, K2) = a.shape, b.shape
    else:
        (K, M), (K2, N) = a.shape, b.shape
    assert K == K2, (name, a.shape, b.shape)
    tm, tn, tk = _pick(M, tm), _pick(N, tn), _pick(K, tk)
    nk = K // tk
    a_spec = pl.BlockSpec((tk, tm), lambda i, j, k: (k, i)) if mode == "tn" else pl.BlockSpec((tm, tk), lambda i, j, k: (i, k))
    b_spec = pl.BlockSpec((tn, tk), lambda i, j, k: (j, k)) if mode == "nt" else pl.BlockSpec((tk, tn), lambda i, j, k: (k, j))
    o_spec = pl.BlockSpec((tm, tn), lambda i, j, k: (i, j))
    ins, specs = [a, b], [a_spec, b_spec]
    if res is not None:
        ins.append(res)
        specs.append(o_spec)
    if bias is not None:
        ins.append(bias)
        specs.append(pl.BlockSpec((1, tn), lambda i, j, k: (0, j)))
    dims = _DIMS[mode]

    def body(*refs):
        a_ref, b_ref = refs[0], refs[1]
        o_ref, acc_ref = refs[-2], refs[-1]
        k = pl.program_id(2)

        @pl.when(k == 0)
        def _():
            acc_ref[...] = jnp.zeros_like(acc_ref)

        acc_ref[...] += lax.dot_general(a_ref[...].astype(BF16), b_ref[...].astype(BF16), dims, preferred_element_type=F32)

        @pl.when(k == nk - 1)
        def _():
            o = acc_ref[...]
            if alpha != 1.0:
                o = o * alpha
            p = 2
            if res is not None:
                o = o + refs[p][...].astype(F32)
                p += 1
            if bias is not None:
                o = o + refs[p][...]
            o_ref[...] = o.astype(out_dtype)

    return pl.pallas_call(
        body, name=name, grid=(M // tm, N // tn, nk), in_specs=specs, out_specs=o_spec,
        out_shape=jax.ShapeDtypeStruct((M, N), out_dtype), scratch_shapes=[pltpu.VMEM((tm, tn), F32)],
        compiler_params=_cp(("parallel", "parallel", "arbitrary")),
    )(*ins)


def _rows(fn, name, T, tm, tiled, full, out_tiled, out_acc, extra=(), reverse=False, scratch=()):
    n = T // tm
    idx = (lambda i: n - 1 - i) if reverse else (lambda i: i)
    in_specs = [pl.BlockSpec((tm, a.shape[1]), lambda i: (idx(i), 0)) for a in tiled]
    in_specs += [mk(idx) for _, mk in extra]
    in_specs += [pl.BlockSpec(a.shape, lambda i, nd=a.ndim: (0,) * nd) for a in full]
    out_specs = [pl.BlockSpec((tm, c), lambda i: (idx(i), 0)) for c, _ in out_tiled]
    out_specs += [pl.BlockSpec(s, lambda i, nd=len(s): (0,) * nd) for s, _ in out_acc]
    out_shape = [jax.ShapeDtypeStruct((T, c), d) for c, d in out_tiled] + [jax.ShapeDtypeStruct(s, d) for s, d in out_acc]
    n_in = len(tiled) + len(extra) + len(full)
    n_t, n_a = len(out_tiled), len(out_acc)

    def body(*refs):
        step = pl.program_id(0)
        vals = [r[...] for r in refs[:n_in]]
        outs = fn(idx(step), *vals, *refs[n_in + n_t + n_a:])
        for r, v in zip(refs[n_in:n_in + n_t], outs[:n_t]):
            r[...] = v.astype(r.dtype)
        for r, v in zip(refs[n_in + n_t:n_in + n_t + n_a], outs[n_t:]):
            @pl.when(step == 0)
            def _(r=r):
                r[...] = jnp.zeros_like(r)

            r[...] += v

    return pl.pallas_call(
        body, name=name, grid=(n,), in_specs=in_specs, out_specs=out_specs, out_shape=out_shape,
        scratch_shapes=list(scratch), compiler_params=_cp(("arbitrary",)),
    )(*tiled, *[a for a, _ in extra], *full)


def _rms(x, g):
    return x * lax.rsqrt(jnp.mean(x * x, axis=-1, keepdims=True) + RMS_EPS) * g


def _rms_fwd(x, g, name, tm=256):
    (h,) = _rows(lambda i, x, g: (_rms(x, g),), name, x.shape[0], min(tm, x.shape[0]), [x], [g], [(x.shape[1], BF16)], [])
    return h


def _rms_bwd(x, g, dh, dres, name, tm=256):
    D = x.shape[1]

    def fn(i, x, dh, dres, g):
        _, vjp = jax.vjp(_rms, x, g)
        dx, dg = vjp(dh.astype(F32))
        dx = dx + dres
        return dx, dg, jnp.sum(dx, axis=0, keepdims=True)

    return _rows(fn, name, x.shape[0], tm, [x, dh, dres], [g], [(D, F32)], [((1, D), F32), ((1, D), F32)])


def _ffn_up(h, wg, wu, name, tm=1024, tn=512, tk=512):
    (M, K), N = h.shape, wg.shape[1]
    tm, tn, tk = _pick(M, tm), _pick(N, tn), _pick(K, tk)
    nk = K // tk

    def body(h_ref, wg_ref, wu_ref, g_ref, u_ref, a_ref, accg, accu):
        k = pl.program_id(2)

        @pl.when(k == 0)
        def _():
            accg[...] = jnp.zeros_like(accg)
            accu[...] = jnp.zeros_like(accu)

        hb = h_ref[...].astype(BF16)
        accg[...] += jnp.dot(hb, wg_ref[...].astype(BF16), preferred_element_type=F32)
        accu[...] += jnp.dot(hb, wu_ref[...].astype(BF16), preferred_element_type=F32)

        @pl.when(k == nk - 1)
        def _():
            g, u = accg[...], accu[...]
            g_ref[...] = g
            u_ref[...] = u
            a_ref[...] = (g * jax.nn.sigmoid(g) * u).astype(BF16)

    o_spec = pl.BlockSpec((tm, tn), lambda i, j, k: (i, j))
    w_spec = pl.BlockSpec((tk, tn), lambda i, j, k: (k, j))
    return pl.pallas_call(
        body, name=name, grid=(M // tm, N // tn, nk),
        in_specs=[pl.BlockSpec((tm, tk), lambda i, j, k: (i, k)), w_spec, w_spec], out_specs=[o_spec] * 3,
        out_shape=[jax.ShapeDtypeStruct((M, N), F32)] * 2 + [jax.ShapeDtypeStruct((M, N), BF16)],
        scratch_shapes=[pltpu.VMEM((tm, tn), F32)] * 2, compiler_params=_cp(("parallel", "parallel", "arbitrary")),
    )(h, wg, wu)


def _act_bwd(da, g, u, name, tm=256):
    def fn(i, da, g, u):
        s = jax.nn.sigmoid(g)
        return da * u * (s * (1.0 + g * (1.0 - s))), da * (g * s)

    F = g.shape[1]
    return _rows(fn, name, g.shape[0], tm, [da, g, u], [], [(F, BF16), (F, BF16)], [])


def _ffn_fwd(x, gain, wg, wu, wd, tag):
    h = _rms_fwd(x, gain, tag + "_norm")
    G, U, A = _ffn_up(h, wg, wu, tag + "_up")
    xo = _mm(A, wd, "nn", tag + "_down", alpha=0.5, res=x)
    return xo, (h, G, U, A)


def _ffn_bwd(x, gain, wg, wu, wd, saved, dxo, tag):
    h, G, U, A = saved
    dA = _mm(dxo, wd, "nt", tag + "_dA", alpha=0.5)
    dwd = _mm(A, dxo, "tn", tag + "_dwd", out_dtype=BF16, alpha=0.5)
    dG, dU = _act_bwd(dA, G, U, tag + "_act_bwd")
    dh = _mm(dG, wg, "nt", tag + "_dh_g")
    dh = _mm(dU, wu, "nt", tag + "_dh_u", res=dh)
    dwg = _mm(h, dG, "tn", tag + "_dwg", out_dtype=BF16)
    dwu = _mm(h, dU, "tn", tag + "_dwu", out_dtype=BF16)
    dx, dgain, _ = _rms_bwd(x, gain, dh, dxo, tag + "_norm_bwd")
    return dx, dgain, dwg, dwu, dwd


def _segsum64_impl(x):
    r = lax.broadcasted_iota(jnp.int32, (128, 128), 0) // HEAD
    c = lax.broadcasted_iota(jnp.int32, (128, 128), 1) // HEAD
    ones = (r == c).astype(BF16)
    hi = x.astype(BF16)
    lo = (x - hi.astype(F32)).astype(BF16)
    outs = []
    for q in range(x.shape[1] // 128):
        sl = slice(q * 128, (q + 1) * 128)
        outs.append(jnp.dot(hi[:, sl], ones, preferred_element_type=F32) + jnp.dot(lo[:, sl], ones, preferred_element_type=F32))
    return outs[0] if len(outs) == 1 else jnp.concatenate(outs, axis=1)


@jax.custom_vjp
def _segsum64(x):
    return _segsum64_impl(x)


_segsum64.defvjp(lambda x: (_segsum64_impl(x), None), lambda _, ct: (_segsum64_impl(ct),))


def _swap32(x):
    lane = lax.broadcasted_iota(jnp.int32, (x.shape[0], 128), 1)
    outs = [jnp.take_along_axis(x[:, q * 128:(q + 1) * 128], lane ^ 32, axis=1) for q in range(x.shape[1] // 128)]
    return outs[0] if len(outs) == 1 else jnp.concatenate(outs, axis=1)


def _tree_sum(xs):
    xs = list(xs)
    while len(xs) > 1:
        nxt = [xs[i] + xs[i + 1] for i in range(0, len(xs) - 1, 2)]
        if len(xs) % 2:
            nxt.append(xs[-1])
        xs = nxt
    return xs[0]


def _softplus(x):
    return jnp.maximum(x, 0.0) + jnp.log(1.0 + jnp.exp(-jnp.abs(x)))


def _pre_core(k, da, gd, w0, a0, k_k, k_a, w_da, gate_up):
    lane = lax.broadcasted_iota(jnp.int32, da.shape, 1)
    w_da = w_da.astype(BF16)
    l1 = jnp.dot(jnp.where(lane < DECAY_LORA, jnp.tanh(da), 0.0).astype(BF16), w_da, preferred_element_type=F32)
    l2 = jnp.dot(jnp.where(lane >= DECAY_LORA, da, 0.0).astype(BF16), w_da, preferred_element_type=F32)
    wlog = -_softplus(-(w0 + l1)) - 0.5
    decay = jnp.exp(-jnp.exp(wlog))
    a = jax.nn.sigmoid(a0 + l2)
    g = jnp.dot(jax.nn.sigmoid(gd).astype(BF16), gate_up.astype(BF16), preferred_element_type=F32)
    kk = k * k_k
    kkn = kk / jnp.maximum(jnp.sqrt(_segsum64(kk * kk)), 1e-12)
    k2 = k * (1.0 + (a - 1.0) * k_a)
    return decay, k2, -kkn, kkn * a, g


def _pre_shift(i, zr, zl, zr8, zl8, mu, mul):
    live = (i > 0).astype(F32)
    dz = _shift_down(zr, zr8[7:8, :] * live) - zr
    dzl = _shift_down(zl, zl8[7:8, :] * live) - zl
    return zr + dz * mu, zl + dzl * mul, dz, dzl


def _shift_down(x, first_row):
    rolled = pltpu.roll(x, 1, 0)
    row = lax.broadcasted_iota(jnp.int32, x.shape, 0)
    return jnp.where(row == 0, first_row, rolled)


def _shift_up(x, last_row):
    rolled = pltpu.roll(x, x.shape[0] - 1, 0)
    row = lax.broadcasted_iota(jnp.int32, x.shape, 0)
    return jnp.where(row == x.shape[0] - 1, last_row, rolled)


def _prev_rows_spec(tm, cols):
    return lambda idx: pl.BlockSpec((8, cols), lambda i: (jnp.maximum(idx(i) * (tm // 8) - 1, 0), 0))


def _rwkv_pre(p_rkv, p_lora, params, tm=256):
    T = p_rkv.shape[0]

    def fn(i, zr, zl, zr8, zl8, mu, mul, *ps):
        z, z2, _, _ = _pre_shift(i, zr, zl, zr8, zl8, mu, mul)
        decay, k2, an, bn, g = _pre_core(z[:, RW_W:2 * RW_W], z2[:, :128], z2[:, 128:], *ps)
        return z[:, :RW_W], decay, k2, z[:, 2 * RW_W:], an, bn, g

    extra = [(p_rkv, _prev_rows_spec(tm, 3 * RW_W)), (p_lora, _prev_rows_spec(tm, LORA_W))]
    return _rows(fn, "rwkv_pre", T, tm, [p_rkv, p_lora], list(params), [(RW_W, F32)] * 7, [], extra=extra)


def _rwkv_pre_bwd(p_rkv, p_lora, params, cts, tm=256):
    T = p_rkv.shape[0]
    n = T // tm

    def fn(i, zr, zl, cr, cdec, ck2, cv, can, cbn, cg, cr_b, ck2_b, cv_b, zr8, zl8, mu, mul, *rest):
        ps, (car, carl) = rest[:-2], rest[-2:]
        cr, ck2, cv = cr + cr_b, ck2 + ck2_b, cv + cv_b
        z, z2, dif, difl = _pre_shift(i, zr, zl, zr8, zl8, mu, mul)
        _, vjp = jax.vjp(_pre_core, z[:, RW_W:2 * RW_W], z2[:, :128], z2[:, 128:], *ps)
        dk, dda, dgd, *dps = vjp((cdec, ck2, can, cbn, cg))
        dz = jnp.concatenate([cr, dk, cv], axis=1)
        dz2 = jnp.concatenate([dda, dgd], axis=1)
        dzp, dzlp = dz * mu, dz2 * mul

        @pl.when(i == n - 1)
        def _():
            car[...] = jnp.zeros_like(car)
            carl[...] = jnp.zeros_like(carl)

        d_rkv = dz - dzp + _shift_up(dzp, car[0:1, :])
        d_lora = dz2 - dzlp + _shift_up(dzlp, carl[0:1, :])
        car[0:1, :] = dzp[0:1, :]
        carl[0:1, :] = dzlp[0:1, :]
        return (d_rkv, d_lora, jnp.sum(dz * dif, axis=0, keepdims=True), jnp.sum(dz2 * difl, axis=0, keepdims=True), *dps)

    extra = [(p_rkv, _prev_rows_spec(tm, 3 * RW_W)), (p_lora, _prev_rows_spec(tm, LORA_W))]
    acc = [(p.shape, F32) for p in params]
    return _rows(fn, "rwkv_pre_bwd", T, tm, [p_rkv, p_lora, *cts], list(params), [(3 * RW_W, BF16), (LORA_W, BF16)], acc,
                 extra=extra, reverse=True, scratch=[pltpu.VMEM((8, 3 * RW_W), F32), pltpu.VMEM((8, LORA_W), F32)])


def _post_core(y, r, k2, v, g, lw, lb, rk):
    mu = _segsum64(y) * (1.0 / HEAD)
    yc = y - mu
    var = _segsum64(yc * yc) * (1.0 / HEAD)
    yn = yc * lax.rsqrt(var + GN_EPS) * lw + lb
    return (yn + _segsum64(r * k2 * rk) * v) * g


def _rwkv_post(y, r, k2, v, g, lw, lb, rk, tm=256):
    (o,) = _rows(lambda i, *a: (_post_core(*a),), "rwkv_post", y.shape[0], tm, [y, r, k2, v, g], [lw, lb, rk], [(RW_W, BF16)], [])
    return o


def _rwkv_post_bwd(y, r, k2, v, g, lw, lb, rk, do, tm=256):
    def fn(i, y, r, k2, v, g, do, lw, lb, rk):
        _, vjp = jax.vjp(_post_core, y, r, k2, v, g, lw, lb, rk)
        return vjp(do.astype(F32))

    return _rows(fn, "rwkv_post_bwd", y.shape[0], tm, [y, r, k2, v, g, do], [lw, lb, rk], [(RW_W, F32)] * 5, [((1, RW_W), F32)] * 3)


SCAN_L = 32


def _to_tile(x):
    T = x.shape[0]
    return x.reshape(T, RW_HEADS, 8, 8).transpose(0, 2, 1, 3).reshape(T, 8, 128)


def _from_tile(x):
    T = x.shape[0]
    return x.reshape(T, 8, RW_HEADS, 8).transpose(0, 2, 1, 3).reshape(T, RW_W)


def _to_exp(x):
    T = x.shape[0]
    xt = x.reshape(T, RW_HEADS, HEAD).transpose(0, 2, 1)
    return jnp.broadcast_to(xt[..., None], (T, HEAD, RW_HEADS, 8)).reshape(T, HEAD, 128)


def _ck_a_to_b(ck):
    n = ck.shape[0]
    return ck.reshape(n, 8, 8, 8, RW_HEADS, 8).transpose(0, 3, 5, 1, 4, 2).reshape(n, HEAD, 8, 128)


def _scan_fwd(xes, vi):
    T, L = vi.shape[0], SCAN_L
    nch = T // L

    def body(*refs):
        xr, (vi_ref, yi_ref, sa_ref, ck_ref, st_ref) = refs[:5], refs[5:]

        @pl.when(pl.program_id(0) == 0)
        def _():
            st_ref[...] = jnp.zeros_like(st_ref)

        ck_ref[0] = st_ref[...]

        def step(t, carry):
            v = vi_ref[t]
            row = lambda m, j: jnp.broadcast_to(xr[m][t, pl.ds(j, 1), :], (8, 128))
            S = [st_ref[j] for j in range(HEAD)]
            sa = _tree_sum([S[j] * row(0, j) for j in range(HEAD)])
            sa_ref[t] = sa
            S = [S[j] * row(1, j) + row(2, j) * sa + row(3, j) * v for j in range(HEAD)]
            for j in range(HEAD):
                st_ref[j] = S[j]
            yi_ref[t] = _tree_sum([S[j] * row(4, j) for j in range(HEAD)])
            return carry

        lax.fori_loop(0, L, step, 0)

    tile = pl.BlockSpec((L, 8, 128), lambda c: (c, 0, 0))
    exp = pl.BlockSpec((L, HEAD, 128), lambda c: (c, 0, 0))
    return pl.pallas_call(
        body, name="rwkv_scan_fwd", grid=(nch,), in_specs=[exp] * 5 + [tile],
        out_specs=[tile, tile, pl.BlockSpec((1, HEAD, 8, 128), lambda c: (c, 0, 0, 0))],
        out_shape=[jax.ShapeDtypeStruct((T, 8, 128), F32)] * 2 + [jax.ShapeDtypeStruct((nch, HEAD, 8, 128), F32)],
        scratch_shapes=[pltpu.VMEM((HEAD, 8, 128), F32)], compiler_params=_cp(("arbitrary",)),
    )(*xes, vi)


def _scan_bwd_a(xes, dyi):
    T, L = dyi.shape[0], SCAN_L
    nch = T // L

    def body(*refs):
        xr, (dy_ref, dsa_ref, dv_ref, g_ref) = refs[:5], refs[5:]

        @pl.when(pl.program_id(0) == 0)
        def _():
            g_ref[...] = jnp.zeros_like(g_ref)

        def step(s, carry):
            t = L - 1 - s
            dy = dy_ref[t]
            row = lambda m, j: jnp.broadcast_to(xr[m][t, pl.ds(j, 1), :], (8, 128))
            G = [g_ref[j] + row(4, j) * dy for j in range(HEAD)]
            dsa = _tree_sum([G[j] * row(2, j) for j in range(HEAD)])
            dsa_ref[t] = dsa
            dv_ref[t] = _tree_sum([G[j] * row(3, j) for j in range(HEAD)])
            for j in range(HEAD):
                g_ref[j] = G[j] * row(1, j) + row(0, j) * dsa
            return carry

        lax.fori_loop(0, L, step, 0)

    tile = pl.BlockSpec((L, 8, 128), lambda c: (nch - 1 - c, 0, 0))
    exp = pl.BlockSpec((L, HEAD, 128), lambda c: (nch - 1 - c, 0, 0))
    return pl.pallas_call(
        body, name="rwkv_scan_bwd_a", grid=(nch,), in_specs=[exp] * 5 + [tile], out_specs=[tile, tile],
        out_shape=[jax.ShapeDtypeStruct((T, 8, 128), F32)] * 2,
        scratch_shapes=[pltpu.VMEM((HEAD, 8, 128), F32)], compiler_params=_cp(("arbitrary",)),
    )(*xes, dyi)


def _scan_bwd_b(xts, ies, ckb):
    T, L = xts[0].shape[0], SCAN_L
    nch = T // L

    def body(*refs):
        xr, er, ck_ref, dj, (hist, g_ref) = refs[:5], refs[5:9], refs[9], refs[10:15], refs[15:]

        @pl.when(pl.program_id(0) == 0)
        def _():
            g_ref[...] = jnp.zeros_like(g_ref)

        hist[0] = ck_ref[0]

        def fstep(t, carry):
            w, B, k = xr[1][t], xr[2][t], xr[3][t]
            row = lambda m, i: jnp.broadcast_to(er[m][t, pl.ds(i, 1), :], (8, 128))
            for i in range(HEAD):
                hist[t + 1, i] = hist[t, i] * w + row(1, i) * B + row(0, i) * k
            return carry

        lax.fori_loop(0, L, fstep, 0)

        def bstep(s, carry):
            t = L - 1 - s
            A, w, r = xr[0][t], xr[1][t], xr[4][t]
            row = lambda m, i: jnp.broadcast_to(er[m][t, pl.ds(i, 1), :], (8, 128))
            G = [g_ref[i] + row(2, i) * r for i in range(HEAD)]
            Sp = [hist[t, i] for i in range(HEAD)]
            dj[4][t] = _tree_sum([hist[t + 1, i] * row(2, i) for i in range(HEAD)])
            dj[1][t] = _tree_sum([G[i] * Sp[i] for i in range(HEAD)])
            dj[2][t] = _tree_sum([G[i] * row(1, i) for i in range(HEAD)])
            dj[3][t] = _tree_sum([G[i] * row(0, i) for i in range(HEAD)])
            dj[0][t] = _tree_sum([Sp[i] * row(3, i) for i in range(HEAD)])
            for i in range(HEAD):
                g_ref[i] = G[i] * w + row(3, i) * A
            return carry

        lax.fori_loop(0, L, bstep, 0)

    tile = pl.BlockSpec((L, 8, 128), lambda c: (nch - 1 - c, 0, 0))
    exp = pl.BlockSpec((L, HEAD, 128), lambda c: (nch - 1 - c, 0, 0))
    return pl.pallas_call(
        body, name="rwkv_scan_bwd_b", grid=(nch,),
        in_specs=[tile] * 5 + [exp] * 4 + [pl.BlockSpec((1, HEAD, 8, 128), lambda c: (nch - 1 - c, 0, 0, 0))],
        out_specs=[tile] * 5, out_shape=[jax.ShapeDtypeStruct((T, 8, 128), F32)] * 5,
        scratch_shapes=[pltpu.VMEM((L + 1, HEAD, 8, 128), F32), pltpu.VMEM((HEAD, 8, 128), F32)], compiler_params=_cp(("arbitrary",)),
    )(*xts, *ies, ckb)


SWA_COLS = SWA_W + 2 * KV_W
BLK = 128


def _swa_core(n, k2a, k2b, vla, vra, vlb, vrb, sinks, *qps):
    iq = lax.broadcasted_iota(jnp.int32, (BLK, 2 * BLK), 0)
    ik = lax.broadcasted_iota(jnp.int32, (BLK, 2 * BLK), 1)
    diff = BLK + iq - ik
    valid = (diff >= 0) & (diff < WINDOW) & ((n > 0) | (ik >= BLK))
    lane = lax.broadcasted_iota(jnp.int32, (BLK, 128), 1)
    lane1 = lax.broadcasted_iota(jnp.int32, (1, 128), 1)
    nt = (((1,), (1,)), ((), ()))
    outs = []
    for pp in range(8):
        k2, vl, vr = (k2a, vla, vra) if pp < 4 else (k2b, vlb, vrb)
        qp = qps[pp]
        o = None
        for half, vv in ((0, vl), (1, vr)):
            qh = jnp.where((lane >= HEAD) == (half == 1), qp, 0.0).astype(BF16)
            s = lax.dot_general(qh, k2.astype(BF16), nt, preferred_element_type=F32) * (HEAD ** -0.5)
            s = jnp.where(valid, s, NEG_INF)
            sink = jnp.sum(jnp.where(lane1 == 2 * pp + half, sinks, 0.0), axis=1, keepdims=True)
            m = jnp.maximum(jnp.max(s, axis=1, keepdims=True), sink)
            p = jnp.exp(s - m)
            den = jnp.sum(p, axis=1, keepdims=True) + jnp.exp(sink - m)
            oh = jnp.dot((p / den).astype(BF16), vv.astype(BF16), preferred_element_type=F32)
            o = oh if o is None else o + oh
        outs.append(o)
    return jnp.concatenate(outs, axis=1)


def _swa_prep(pc, pp, b, cq, sq, ckc, skc, ckp, skp):
    zc, zp = pc + b, pp + b
    qr = zc[:, :SWA_W] * cq + _swap32(zc[:, :SWA_W]) * sq
    kc, kp = zc[:, SWA_W:SWA_W + KV_W], zp[:, SWA_W:SWA_W + KV_W]
    kb = jnp.concatenate([kp * ckp + _swap32(kp) * skp, kc * ckc + _swap32(kc) * skc], axis=0)
    vb = jnp.concatenate([zp[:, SWA_W + KV_W:], zc[:, SWA_W + KV_W:]], axis=0)
    lane = lax.broadcasted_iota(jnp.int32, kb.shape, 1)
    left = lane < HEAD
    kbr, vbr = pltpu.roll(kb, HEAD, 1), pltpu.roll(vb, HEAD, 1)
    return (jnp.where(left, kb, kbr), jnp.where(left, kbr, kb), jnp.where(left, vb, 0.0), jnp.where(left, 0.0, vbr),
            jnp.where(left, vbr, 0.0), jnp.where(left, 0.0, vb)), [qr[:, q * 128:(q + 1) * 128] for q in range(8)]


def _swa_specs(T, tabs_q, tabs_k):
    cur = lambda c: pl.BlockSpec((BLK, c), lambda n: (n, 0))
    prev = lambda c: pl.BlockSpec((BLK, c), lambda n: (jnp.maximum(n - 1, 0), 0))
    return cur, prev


def _swa_fwd(p_swa, b, sinks, cq, sq, ck, sk):
    T = p_swa.shape[0]
    cur, prev = _swa_specs(T, None, None)

    def body(pc, pp, b_ref, s_ref, cq_r, sq_r, ckc, skc, ckp, skp, o_ref):
        ops, qps = _swa_prep(pc[...], pp[...], b_ref[...], cq_r[...], sq_r[...], ckc[...], skc[...], ckp[...], skp[...])
        o_ref[...] = _swa_core(pl.program_id(0), *ops, s_ref[...], *qps).astype(o_ref.dtype)

    full = lambda a: pl.BlockSpec(a.shape, lambda n: (0, 0))
    return pl.pallas_call(
        body, name="swa_fwd", grid=(T // BLK,),
        in_specs=[cur(SWA_COLS), prev(SWA_COLS), full(b), full(sinks), cur(SWA_W), cur(SWA_W), cur(KV_W), cur(KV_W), prev(KV_W), prev(KV_W)],
        out_specs=cur(SWA_W), out_shape=jax.ShapeDtypeStruct((T, SWA_W), BF16), compiler_params=_cp(("arbitrary",)),
    )(p_swa, p_swa, b, sinks, cq, sq, ck, sk, ck, sk)


def _swa_bwd(p_swa, b, sinks, cq, sq, ck, sk, do):
    T = p_swa.shape[0]
    nb = T // BLK
    cur = lambda c: pl.BlockSpec((BLK, c), lambda s: (nb - 1 - s, 0))
    prev = lambda c: pl.BlockSpec((BLK, c), lambda s: (jnp.maximum(nb - 2 - s, 0), 0))

    def body(pc, pp, b_ref, s_ref, cq_r, sq_r, ckc, skc, ckp, skp, do_ref, dcur, db, dsk, carry):
        step = pl.program_id(0)
        n = nb - 1 - step

        @pl.when(step == 0)
        def _():
            carry[...] = jnp.zeros_like(carry)
            db[...] = jnp.zeros_like(db)
            dsk[...] = jnp.zeros_like(dsk)

        ops, qps = _swa_prep(pc[...], pp[...], b_ref[...], cq_r[...], sq_r[...], ckc[...], skc[...], ckp[...], skp[...])
        _, vjp = jax.vjp(functools.partial(_swa_core, n), *ops, s_ref[...], *qps)
        dk2a, dk2b, dvla, dvra, dvlb, dvrb, dsinks, *dqps = vjp(do_ref[...].astype(F32))
        dqr = jnp.concatenate(dqps, axis=1)
        lane = lax.broadcasted_iota(jnp.int32, dk2a.shape, 1)
        left = lane < HEAD
        dkb = jnp.where(left, dk2a + pltpu.roll(dk2a, HEAD, 1), dk2b + pltpu.roll(dk2b, HEAD, 1))
        dvb = jnp.where(left, dvla + pltpu.roll(dvra, HEAD, 1), pltpu.roll(dvlb, HEAD, 1) + dvrb)
        dq = dqr * cq_r[...] + _swap32(dqr * sq_r[...])
        dkp, dkc = dkb[:BLK], dkb[BLK:]
        dkp = dkp * ckp[...] + _swap32(dkp * skp[...])
        dkc = dkc * ckc[...] + _swap32(dkc * skc[...])
        dc = jnp.concatenate([dq, jnp.concatenate([dkc, dvb[BLK:]], axis=1) + carry[...]], axis=1)
        carry[...] = jnp.concatenate([dkp, dvb[:BLK]], axis=1)
        dcur[...] = dc.astype(dcur.dtype)
        db[...] += jnp.sum(dc, axis=0, keepdims=True)
        dsk[...] += dsinks

    full = lambda a: pl.BlockSpec(a.shape, lambda s: (0, 0))
    return pl.pallas_call(
        body, name="swa_bwd", grid=(nb,),
        in_specs=[cur(SWA_COLS), prev(SWA_COLS), full(b), full(sinks), cur(SWA_W), cur(SWA_W), cur(KV_W), cur(KV_W), prev(KV_W), prev(KV_W),
                  cur(SWA_W)],
        out_specs=[cur(SWA_COLS), full(b), full(sinks)],
        out_shape=[jax.ShapeDtypeStruct((T, SWA_COLS), BF16), jax.ShapeDtypeStruct(b.shape, F32), jax.ShapeDtypeStruct(sinks.shape, F32)],
        scratch_shapes=[pltpu.VMEM((BLK, 2 * KV_W), F32)], compiler_params=_cp(("arbitrary",)),
    )(p_swa, p_swa, b, sinks, cq, sq, ck, sk, ck, sk, do)


def _rope_tables(T):
    inv = 10000.0 ** (-jnp.arange(0, HEAD, 2, dtype=F32) / HEAD)
    ang = jnp.arange(T, dtype=F32)[:, None] * inv[None, :]
    c = jnp.concatenate([jnp.cos(ang), jnp.cos(ang)], axis=1)
    s = jnp.concatenate([-jnp.sin(ang), jnp.sin(ang)], axis=1)
    return jnp.tile(c, (1, 16)), jnp.tile(s, (1, 16)), jnp.tile(c, (1, 2)), jnp.tile(s, (1, 2))


def _xattn_core(*qkv):
    outs = []
    for h in range(XH):
        qh, kh, vh = qkv[h], qkv[XH + h], qkv[2 * XH + h]
        s = lax.dot_general(qh.astype(BF16), kh.astype(BF16), (((1,), (1,)), ((), ())), preferred_element_type=F32) * (XHD ** -0.5)
        p = jnp.exp(s - jnp.max(s, axis=1, keepdims=True))
        p = p / jnp.sum(p, axis=1, keepdims=True)
        outs.append(jnp.dot(p.astype(BF16), vh.astype(BF16), preferred_element_type=F32))
    return jnp.concatenate(outs, axis=1)


def _xattn_split(q, kv):
    return [q[:, h * XHD:(h + 1) * XHD] for h in range(XH)] + [kv[:, h * XHD:(h + 1) * XHD] for h in range(2 * XH)]


def _xattn_fwd(q, kv, tm=256):
    (o,) = _rows(lambda i, q, kv: (_xattn_core(*_xattn_split(q, kv)),), "xattn_fwd", q.shape[0], tm, [q], [kv], [(q.shape[1], BF16)], [])
    return o


def _xattn_bwd(q, kv, do, tm=256):
    def fn(i, q, do, kv):
        _, vjp = jax.vjp(_xattn_core, *_xattn_split(q, kv))
        d = vjp(do.astype(F32))
        return jnp.concatenate(d[:XH], axis=1), jnp.concatenate(d[XH:], axis=1)

    return _rows(fn, "xattn_bwd", q.shape[0], tm, [q, do], [kv], [(q.shape[1], BF16)], [(kv.shape, F32)])


def _loss_head(x, g, tgt, tm=256):
    D = x.shape[1]

    def fn(i, x, tgt, g):
        y, vjp = jax.vjp(_rms, x, g)
        err = y - tgt
        dx, dg = vjp(err * (1.0 / D))
        part = 0.5 / D * jnp.sum(jnp.sum(err * err, axis=1, keepdims=True), axis=0, keepdims=True)
        return dx, jnp.broadcast_to(part, (1, 128)), dg

    return _rows(fn, "loss_head", x.shape[0], tm, [x, tgt], [g], [(D, F32)], [((1, 128), F32), ((1, D), F32)])


def _local_step(x, mem, tgt, get_w, P, put_g):
    T = x.shape[0]
    W = dict(get_w("f1", None))
    x1, s1 = _ffn_fwd(x, P["f1_norm"], W["f1_gate"], W["f1_up"], W["f1_down"], "f1")

    W.update(get_w("mix", x1))
    h2 = _rms_fwd(x1, P["mix_norm"], "mix_norm")
    w_rkv, w_lora, w_swa = W["w_inT"][:3 * RW_W], W["w_inT"][3 * RW_W:SHIFT_COLS], W["w_inT"][SHIFT_COLS:]
    p_rkv = _mm(h2, w_rkv, "nt", "in_rkv")
    p_lora = _mm(h2, w_lora, "nt", "in_lora")
    p_swa = _mm(h2, w_swa, "nt", "in_swa")
    w_da = jnp.concatenate([W["rw_decay_up"], W["rw_aaa_up"]], axis=0)
    pre_params = (P["rw_mu"][:, :3 * RW_W], P["rw_mu"][:, 3 * RW_W:], P["rw_w0"], P["rw_a0"], P["rw_k_k"], P["rw_k_a"], w_da,
                  W["rw_gate_up"])
    r, decay, k2, v, an, bn, g = _rwkv_pre(p_rkv, p_lora, pre_params)
    scan_vecs = (an, decay, bn, k2, r)
    xes = [_to_exp(a) for a in scan_vecs]
    yi, sai, ck = _scan_fwd(xes, _to_tile(v))
    y_scan = _from_tile(yi)
    y_rw = _rwkv_post(y_scan, r, k2, v, g, P["rw_lnx_w"], P["rw_lnx_b"], P["rw_r_k"])
    cq, sq, ckt, skt = _rope_tables(T)
    y_swa = _swa_fwd(p_swa, P["b_in_attn"], P["attn_sinks"], cq, sq, ckt, skt)
    ycat = jnp.concatenate([y_rw, y_swa], axis=1)
    x2 = _mm(ycat, W["w_out"], "nn", "out_proj", res=x1, bias=P["b_out"])

    W.update(get_w("xattn", x2))
    hx = _rms_fwd(x2, P["xa_norm"], "xa_norm")
    mn = _rms_fwd(mem, P["mem_norm"], "mem_norm")
    q = _mm(hx, W["w_xq"], "nn", "xq", out_dtype=BF16)
    kv = _mm(mn, W["w_xkv"], "nn", "xkv", out_dtype=BF16)
    o = _xattn_fwd(q, kv)
    x3 = _mm(o, W["w_xo"], "nn", "xo", res=x2)

    W.update(get_w("f2", x3))
    x4, s2 = _ffn_fwd(x3, P["f2_norm"], W["f2_gate"], W["f2_up"], W["f2_down"], "f2")
    dx4, loss_part, d_final = _loss_head(x4, P["final_norm"], tgt)

    gw, gs = {}, {"final_norm": d_final}
    dx3, gs["f2_norm"], gw["f2_gate"], gw["f2_up"], gw["f2_down"] = _ffn_bwd(
        x3, P["f2_norm"], W["f2_gate"], W["f2_up"], W["f2_down"], s2, dx4, "f2")
    put_g("f2", gw)

    do = _mm(dx3, W["w_xo"], "nt", "xo_do", out_dtype=BF16)
    gw["w_xo"] = _mm(o, dx3, "tn", "xo_dw", out_dtype=BF16)
    dq, dkv = _xattn_bwd(q, kv, do)
    gw["w_xq"] = _mm(hx, dq, "tn", "xq_dw", out_dtype=BF16)
    dhx = _mm(dq, W["w_xq"], "nt", "xq_dh")
    gw["w_xkv"] = _mm(mn, dkv, "tn", "xkv_dw", out_dtype=BF16)
    put_g("xattn", gw)
    dmn = _mm(dkv, W["w_xkv"], "nt", "xkv_dmn")
    _, gs["mem_norm"], _ = _rms_bwd(mem, P["mem_norm"], dmn, jnp.zeros_like(mem), "mem_norm_bwd")
    dx2, gs["xa_norm"], gs["b_out"] = _rms_bwd(x2, P["xa_norm"], dhx, dx3, "xa_norm_bwd")

    dycat = _mm(dx2, W["w_out"], "nt", "out_dy")
    gw["w_out"] = _mm(ycat, dx2, "tn", "out_dw", out_dtype=BF16)
    dp_swa, gs["b_in_attn"], gs["attn_sinks"] = _swa_bwd(p_swa, P["b_in_attn"], P["attn_sinks"], cq, sq, ckt, skt, dycat[:, RW_W:])
    dy_scan, dr_b, dk2_b, dv_b, dg, gs["rw_lnx_w"], gs["rw_lnx_b"], gs["rw_r_k"] = _rwkv_post_bwd(
        y_scan, r, k2, v, g, P["rw_lnx_w"], P["rw_lnx_b"], P["rw_r_k"], dycat[:, :RW_W])
    dsai, dvi = _scan_bwd_a(xes, _to_tile(dy_scan))
    ies = [_to_exp(v), _to_exp(_from_tile(sai)), _to_exp(dy_scan), _to_exp(_from_tile(dsai))]
    dj = _scan_bwd_b([_to_tile(a) for a in scan_vecs], ies, _ck_a_to_b(ck))
    dan, ddecay, dbn, dk2_s, dr_s = (_from_tile(d) for d in dj)
    cts = (dr_s, ddecay, dk2_s, _from_tile(dvi), dan, dbn, dg, dr_b, dk2_b, dv_b)
    dp_rkv, dp_lora, dmu, dmul, gs["rw_w0"], gs["rw_a0"], gs["rw_k_k"], gs["rw_k_a"], dw_da, gs["rw_gate_up"] = _rwkv_pre_bwd(
        p_rkv, p_lora, pre_params, cts)
    gs["rw_mu"] = jnp.concatenate([dmu, dmul], axis=1)
    gs["rw_decay_up"], gs["rw_aaa_up"] = dw_da[:DECAY_LORA], dw_da[DECAY_LORA:]
    gw["w_inT"] = jnp.concatenate([_mm(dp_rkv, h2, "tn", "in_dw_rkv"), _mm(dp_lora, h2, "tn", "in_dw_lora"),
                                   _mm(dp_swa, h2, "tn", "in_dw_swa")], axis=0)
    put_g("mix", gw)
    dh2 = _mm(dp_rkv, w_rkv, "nn", "in_dh_rkv")
    dh2 = _mm(dp_lora, w_lora, "nn", "in_dh_lora", res=dh2)
    dh2 = _mm(dp_swa, w_swa, "nn", "in_dh_swa", res=dh2)
    dx1, gs["mix_norm"], _ = _rms_bwd(x1, P["mix_norm"], dh2, dx2, "mix_norm_bwd")

    dx0, gs["f1_norm"], gw["f1_gate"], gw["f1_up"], gw["f1_down"] = _ffn_bwd(
        x, P["f1_norm"], W["f1_gate"], W["f1_up"], W["f1_down"], s1, dx1, "f1")
    put_g("f1", gw)
    return loss_part, dx0, gs


_ANY = pl.BlockSpec(memory_space=pl.ANY)
_OTHER_CHIPS = ((1, 0), (0, 1), (1, 1))


def _mesh_pos():
    return lax.axis_index("x"), lax.axis_index("y"), lax.axis_index("c")


def _slot(ref, kind, s, rows, cols):
    if kind == "row":
        return ref.at[pl.ds(pl.multiple_of(s * rows, 8), rows), :]
    return ref.at[:, pl.ds(pl.multiple_of(s * cols, 128), cols)]


_HBM = pl.BlockSpec(memory_space=pltpu.HBM)
_SEMS = pl.BlockSpec(memory_space=pltpu.SEMAPHORE)
_SPLIT = dict(compiler_params=pltpu.CompilerParams(has_side_effects=pltpu.SideEffectType.DATAFLOW_SIDE_EFFECTING))


def _in_hbm(a):
    return pltpu.with_memory_space_constraint(a, pltpu.HBM)


def _full_shape(s, kind):
    return (4 * s.shape[0], s.shape[1]) if kind == "row" else (s.shape[0], 4 * s.shape[1])


def _gather_start(shards, kinds, groups):
    n, ng = len(shards), len(groups)
    lands = [_in_hbm(lax.empty(_full_shape(s, k), s.dtype)) for s, k in zip(shards, kinds)]

    def body(*refs):
        src, land, sems, token = refs[:n], refs[n:2 * n], refs[2 * n:2 * n + 2 * ng], refs[-1]
        x, y, c = _mesh_pos()
        me = 2 * x + y
        for gi, idxs in enumerate(groups):
            for k, i in enumerate(idxs):
                mine = _slot(land[i], kinds[i], me, *src[i].shape)
                for r, (dx, dy) in enumerate(_OTHER_CHIPS):
                    pltpu.make_async_remote_copy(src[i], mine, sems[2 * gi].at[3 * k + r], sems[2 * gi + 1].at[3 * k + r],
                                                 device_id=((x + dx) % 2, (y + dy) % 2, c), device_id_type=MESH).start()
        token[...] = jnp.zeros_like(token)

    sem_shapes = [pltpu.SemaphoreType.DMA((3 * len(g),)) for g in groups for _ in range(2)]
    thru = [pltpu.HBM(a.shape, a.dtype) for a in (*shards, *lands)]
    res = pl.pallas_call(
        body, name="gather_start", in_specs=[_HBM] * (2 * n),
        out_specs=[_SEMS] * (2 * ng) + [_HBM] * (2 * n) + [pl.BlockSpec(memory_space=pltpu.VMEM)],
        out_shape=sem_shapes + thru + [jax.ShapeDtypeStruct((8, 128), F32)],
        input_output_aliases={i: 2 * ng + i for i in range(2 * n)}, **_SPLIT,
    )(*[_in_hbm(s) for s in shards], *lands)
    return res[:2 * ng], res[2 * ng:2 * ng + n], res[2 * ng + n:2 * ng + 2 * n], res[-1]


def _gather_wait(name, send, recv, shards, lands, kinds, after):
    m = len(shards)

    def body(*refs):
        src, land, send_ref, recv_ref = refs[:m], refs[m:2 * m], refs[2 * m], refs[2 * m + 1]
        got, loc = refs[2 * m + 3 + m:2 * m + 3 + 2 * m], refs[-1]
        x, y, c = _mesh_pos()
        me = 2 * x + y
        own = []
        for k in range(m):
            cp = pltpu.make_async_copy(src[k], _slot(got[k], kinds[k], me, *src[k].shape), loc.at[k])
            cp.start()
            own.append(cp)
        for k in range(m):
            mine = _slot(land[k], kinds[k], me, *src[k].shape)
            for r in range(3):
                cp = pltpu.make_async_remote_copy(src[k], mine, send_ref.at[3 * k + r], recv_ref.at[3 * k + r],
                                                  device_id=(x, y, c), device_id_type=MESH)
                cp.wait_send()
                cp.wait_recv()
        for cp in own:
            cp.wait()

    thru = [pltpu.HBM(a.shape, a.dtype) for a in (*shards, *lands)]
    res = pl.pallas_call(
        body, name=name, in_specs=[_HBM] * (2 * m) + [_SEMS, _SEMS, pl.BlockSpec(memory_space=pl.ANY)],
        out_specs=[_HBM] * (2 * m), out_shape=thru, input_output_aliases={i: i for i in range(2 * m)},
        scratch_shapes=[pltpu.SemaphoreType.DMA((m,))], **_SPLIT,
    )(*shards, *lands, send, recv, after)
    return res[m:]


def _scatter_start(name, grads, kinds):
    m = len(grads)
    shard_shape = [(g.shape[0] // 4, g.shape[1]) if k == "row" else (g.shape[0], g.shape[1] // 4) for g, k in zip(grads, kinds)]
    lands = [_in_hbm(lax.empty((4, *s), g.dtype)) for s, g in zip(shard_shape, grads)]

    def body(*refs):
        src, land, send, recv = refs[:m], refs[m:2 * m], refs[2 * m], refs[2 * m + 1]
        x, y, c = _mesh_pos()
        me = 2 * x + y
        for k in range(m):
            for r, (dx, dy) in enumerate(_OTHER_CHIPS):
                tx, ty = (x + dx) % 2, (y + dy) % 2
                pltpu.make_async_remote_copy(_slot(src[k], kinds[k], 2 * tx + ty, *shard_shape[k]), land[k].at[me],
                                             send.at[3 * k + r], recv.at[3 * k + r], device_id=(tx, ty, c), device_id_type=MESH).start()
        refs[-1][...] = jnp.zeros_like(refs[-1])

    thru = [pltpu.HBM(a.shape, a.dtype) for a in (*grads, *lands)]
    res = pl.pallas_call(
        body, name=name, in_specs=[_HBM] * (2 * m),
        out_specs=[_SEMS, _SEMS] + [_HBM] * (2 * m) + [pl.BlockSpec(memory_space=pltpu.VMEM)],
        out_shape=[pltpu.SemaphoreType.DMA((3 * m,))] * 2 + thru + [jax.ShapeDtypeStruct((8, 128), F32)],
        input_output_aliases={i: 2 + i for i in range(2 * m)}, **_SPLIT,
    )(*[_in_hbm(g) for g in grads], *lands)
    return res[0], res[1], res[2:2 + m], res[2 + m:2 + 2 * m]


def _scatter_wait(name, send, recv, grads, lands, kinds, after):
    m = len(grads)

    def body(*refs):
        src, land, send_ref, recv_ref = refs[:m], refs[m:2 * m], refs[2 * m], refs[2 * m + 1]
        got, loc = refs[2 * m + 3 + m:2 * m + 3 + 2 * m], refs[-1]
        x, y, c = _mesh_pos()
        me = 2 * x + y
        own = []
        for k in range(m):
            rows, cols = land[k].shape[1:]
            cp = pltpu.make_async_copy(_slot(src[k], kinds[k], me, rows, cols), got[k].at[me], loc.at[k])
            cp.start()
            own.append(cp)
        for k in range(m):
            rows, cols = land[k].shape[1:]
            for r in range(3):
                cp = pltpu.make_async_remote_copy(_slot(src[k], kinds[k], me, rows, cols), land[k].at[me], send_ref.at[3 * k + r],
                                                  recv_ref.at[3 * k + r], device_id=(x, y, c), device_id_type=MESH)
                cp.wait_send()
                cp.wait_recv()
        for cp in own:
            cp.wait()

    thru = [pltpu.HBM(a.shape, a.dtype) for a in (*grads, *lands)]
    res = pl.pallas_call(
        body, name=name, in_specs=[_HBM] * (2 * m) + [_SEMS, _SEMS, pl.BlockSpec(memory_space=pl.ANY)],
        out_specs=[_HBM] * (2 * m), out_shape=thru, input_output_aliases={i: i for i in range(2 * m)},
        scratch_shapes=[pltpu.SemaphoreType.DMA((m,))], **_SPLIT,
    )(*grads, *lands, send, recv, after)
    return res[m:]


def _swap_with_sibling(arrs, name):
    n = len(arrs)

    def body(*refs):
        ins, outs = refs[:n], refs[n:2 * n]
        send, recv = refs[2 * n:]
        x, y, c = _mesh_pos()
        copies = []
        for i in range(n):
            rc = pltpu.make_async_remote_copy(ins[i], outs[i], send.at[i], recv.at[i], device_id=(x, y, 1 - c), device_id_type=MESH)
            rc.start()
            copies.append(rc)
        for rc in copies:
            rc.wait()

    return pl.pallas_call(
        body, name=name, in_specs=[_ANY] * n, out_specs=[_ANY] * n,
        out_shape=[jax.ShapeDtypeStruct(a.shape, a.dtype) for a in arrs],
        scratch_shapes=[pltpu.SemaphoreType.DMA((n,)), pltpu.SemaphoreType.DMA((n,))],
    )(*arrs)


def _gather_small(pack):
    def body(in_ref, out_ref, send, recv, loc):
        x, y, c = _mesh_pos()
        me = 4 * x + 2 * y + c
        cp = pltpu.make_async_copy(in_ref, out_ref.at[me], loc.at[0])
        cp.start()
        copies = [cp]
        for r in range(1, 8):
            dx, dy, dc = r // 4, (r // 2) % 2, r % 2
            rc = pltpu.make_async_remote_copy(in_ref, out_ref.at[me], send.at[r - 1], recv.at[r - 1],
                                              device_id=((x + dx) % 2, (y + dy) % 2, (c + dc) % 2), device_id_type=MESH)
            rc.start()
            copies.append(rc)
        for cp in copies:
            cp.wait()

    return pl.pallas_call(
        body, name="gather_small", in_specs=[_ANY], out_specs=_ANY, out_shape=jax.ShapeDtypeStruct((8, *pack.shape), pack.dtype),
        scratch_shapes=[pltpu.SemaphoreType.DMA((7,)), pltpu.SemaphoreType.DMA((7,)), pltpu.SemaphoreType.DMA((1,))],
    )(pack)


def _row_tile(R, dtype, target=256):
    mult = 8 * 4 // jnp.dtype(dtype).itemsize
    best = R
    for t in range(mult, min(R, target) + 1, mult):
        if R % t == 0:
            best = t
    return best


def _sum_slots(stack, name, out_dtype=F32):
    k, R, C = stack.shape
    tr = _row_tile(R, stack.dtype)

    def body(s_ref, o_ref):
        acc = s_ref[0].astype(F32)
        for j in range(1, k):
            acc = acc + s_ref[j].astype(F32)
        o_ref[...] = acc.astype(out_dtype)

    return pl.pallas_call(
        body, name=name, grid=(R // tr,), in_specs=[pl.BlockSpec((k, tr, C), lambda i: (0, i, 0))],
        out_specs=pl.BlockSpec((tr, C), lambda i: (i, 0)), out_shape=jax.ShapeDtypeStruct((R, C), out_dtype),
        compiler_params=_cp(("parallel",)),
    )(stack)


def _adamw(w, m, v, ga, gb, name):
    R, C = w.shape
    tr = _row_tile(R, F32, 128)
    gs = [ga] if gb is None else [ga, gb]

    def body(*refs):
        w_ref, m_ref, v_ref = refs[:3]
        g = refs[3][...]
        if gb is not None:
            g = g + refs[4][...]
        g_ref, d_ref, nm_ref, nv_ref = refs[-4:]
        nm = ADAM_B1 * m_ref[...] + (1.0 - ADAM_B1) * g
        nv = ADAM_B2 * v_ref[...] + (1.0 - ADAM_B2) * (g * g)
        m_hat = nm / (1.0 - ADAM_B1 ** ADAM_STEP)
        v_hat = nv / (1.0 - ADAM_B2 ** ADAM_STEP)
        g_ref[...] = g
        d_ref[...] = -ADAM_LR * (m_hat / (jnp.sqrt(v_hat) + ADAM_EPS) + ADAM_WD * w_ref[...])
        nm_ref[...] = nm
        nv_ref[...] = nv

    spec = pl.BlockSpec((tr, C), lambda i: (i, 0))
    return pl.pallas_call(
        body, name=name, grid=(R // tr,), in_specs=[spec] * (3 + len(gs)), out_specs=[spec] * 4,
        out_shape=[jax.ShapeDtypeStruct((R, C), F32)] * 4, compiler_params=_cp(("parallel",)),
    )(w, m, v, *gs)


def _pack(arrs):
    rows = []
    for a in arrs:
        flat = a.reshape(-1)
        rows.append(jnp.pad(flat, (0, -flat.shape[0] % 1024)).reshape(-1, 1024))
    p = jnp.concatenate(rows, axis=0)
    return jnp.pad(p, ((0, -p.shape[0] % 8), (0, 0)))


def _unpack(p, shapes):
    out, r = [], 0
    for s in shapes:
        n = 1
        for d in s:
            n *= d
        nr = -(-n // 1024)
        out.append(p[r:r + nr].reshape(-1)[:n].reshape(s))
        r += nr
    return out


BIG = ("f1_gate", "f1_up", "f1_down", "w_in", "w_out", "w_xq", "w_xkv", "w_xo", "f2_gate", "f2_up", "f2_down")
BIG_KIND = {"f1_gate": "col", "f1_up": "col", "f1_down": "row", "w_in": "row", "w_out": "row", "w_xq": "row", "w_xkv": "col",
            "w_xo": "row", "f2_gate": "col", "f2_up": "col", "f2_down": "row"}
LORA = ("rw_decay_up", "rw_aaa_up", "rw_gate_up")
WEIGHTS = ("f1_norm", "f1_gate", "f1_up", "f1_down", "mix_norm", "w_in", "b_in_attn", "rw_mu", "rw_w0", "rw_decay_up", "rw_a0",
           "rw_aaa_up", "rw_gate_up", "rw_k_k", "rw_k_a", "rw_r_k", "rw_lnx_w", "rw_lnx_b", "attn_sinks", "w_out", "b_out", "xa_norm",
           "mem_norm", "w_xq", "w_xkv", "w_xo", "f2_norm", "f2_gate", "f2_up", "f2_down", "final_norm")
SMALL = tuple(n for n in WEIGHTS if n not in BIG)
GROUP_ORDER = ("f1", "mix", "xattn", "f2")
GROUPS = {"f1": ("f1_gate", "f1_up", "f1_down"), "mix": ("w_in", "w_out") + LORA, "xattn": ("w_xq", "w_xkv", "w_xo"),
          "f2": ("f2_gate", "f2_up", "f2_down")}


def kernel(x, mem, f1_norm, f1_gate, f1_up, f1_down, mix_norm, w_in, b_in_attn, rw_mu, rw_w0, rw_decay_up, rw_a0, rw_aaa_up, rw_gate_up, rw_k_k, rw_k_a, rw_r_k, rw_lnx_w, rw_lnx_b, attn_sinks, w_out, b_out, xa_norm, mem_norm, w_xq, w_xkv, w_xo, f2_norm, f2_gate, f2_up, f2_down, final_norm, loss_target, m_f1_norm, m_f1_gate, m_f1_up, m_f1_down, m_mix_norm, m_w_in, m_b_in_attn, m_rw_mu, m_rw_w0, m_rw_decay_up, m_rw_a0, m_rw_aaa_up, m_rw_gate_up, m_rw_k_k, m_rw_k_a, m_rw_r_k, m_rw_lnx_w, m_rw_lnx_b, m_attn_sinks, m_w_out, m_b_out, m_xa_norm, m_mem_norm, m_w_xq, m_w_xkv, m_w_xo, m_f2_norm, m_f2_gate, m_f2_up, m_f2_down, m_final_norm, v_f1_norm, v_f1_gate, v_f1_up, v_f1_down, v_mix_norm, v_w_in, v_b_in_attn, v_rw_mu, v_rw_w0, v_rw_decay_up, v_rw_a0, v_rw_aaa_up, v_rw_gate_up, v_rw_k_k, v_rw_k_a, v_rw_r_k, v_rw_lnx_w, v_rw_lnx_b, v_attn_sinks, v_w_out, v_b_out, v_xa_norm, v_mem_norm, v_w_xq, v_w_xkv, v_w_xo, v_f2_norm, v_f2_gate, v_f2_up, v_f2_down, v_final_norm):
    a = dict(locals())
    w = {n: a[n] for n in WEIGHTS}
    m = {n: a["m_" + n] for n in WEIGHTS}
    v = {n: a["v_" + n] for n in WEIGHTS}
    sq = lambda t: t.reshape(t.shape[-2:]) if t.ndim == 3 else t.reshape(1, -1)

    local_name = lambda n: "w_inT" if n == "w_in" else n
    kind_of = lambda n: BIG_KIND.get(n, "col")
    shards, kinds, groups = [], [], []
    for grp in GROUP_ORDER:
        groups.append(list(range(len(shards), len(shards) + len(GROUPS[grp]))))
        for n in GROUPS[grp]:
            shards.append(sq(w[n]).T if n == "w_in" else sq(w[n]) if n in LORA else sq(w[n]).astype(BF16))
            kinds.append(kind_of(n))
    sems, src_thru, land_thru, token = _gather_start(shards, kinds, groups)

    def get_w(grp, after):
        gi = GROUP_ORDER.index(grp)
        got = _gather_wait("gather_wait_" + grp, sems[2 * gi], sems[2 * gi + 1], [src_thru[i] for i in groups[gi]],
                           [land_thru[i] for i in groups[gi]], [kinds[i] for i in groups[gi]], token if after is None else after)
        return {local_name(n): f for n, f in zip(GROUPS[grp], got)}

    in_flight = {}

    def put_g(grp, gw):
        names = [n for n in GROUPS[grp] if n in BIG]
        in_flight[grp] = (names, _scatter_start("scatter_start_" + grp, [gw[local_name(n)] for n in names], [kind_of(n) for n in names]))

    P = {n: sq(w[n]) for n in SMALL if n not in LORA}
    P["attn_sinks"] = jnp.pad(P["attn_sinks"], ((0, 0), (0, 128 - P["attn_sinks"].shape[1])))
    P["rw_r_k"] = w["rw_r_k"].reshape(1, RW_W)
    loss_part, grad_x, gs = _local_step(x[0], mem[0], loss_target[0], get_w, P, put_g)
    loss = lax.psum(loss_part[0, 0], ("x", "y", "c"))

    out, after = {}, grad_x
    for grp in reversed(GROUP_ORDER):
        names, (send, recv, g_thru, l_thru) = in_flight[grp]
        stacks = _scatter_wait("scatter_wait_" + grp, send, recv, g_thru, l_thru, [kind_of(n) for n in names], after)
        partial = [_sum_slots(s, "sum_chips_" + n) for s, n in zip(stacks, names)]
        sibling = _swap_with_sibling(partial, "swap_" + grp)
        for n, pa, sb in zip(names, partial, sibling):
            if n == "w_in":
                pa, sb = pa.T, sb.T
            out[n] = _adamw(sq(w[n]), sq(m[n]), sq(v[n]), pa, sb, "adamw_" + n)
        after = out[names[-1]][1]

    gs["attn_sinks"] = gs["attn_sinks"][:, :16]
    gsum = _sum_slots(_gather_small(_pack([gs[n] for n in SMALL])), "sum_small")
    g_small = dict(zip(SMALL, _unpack(gsum, [gs[n].shape for n in SMALL])))
    shard = 2 * lax.axis_index("x") + lax.axis_index("y")
    for n in LORA:
        cols = w[n].shape[-1]
        g_small[n] = lax.dynamic_slice_in_dim(g_small[n], shard * cols, cols, axis=1)

    flat = lambda d: _pack([d[n] for n in SMALL])
    res = _adamw(flat(w), flat(m), flat(v), _pack([g_small[n] for n in SMALL]), None, "adamw_small")
    shapes = [w[n].shape for n in SMALL]
    for k, p in enumerate(res):
        for n, t in zip(SMALL, _unpack(p, shapes)):
            out.setdefault(n, [None] * 4)[k] = t
    outs = [loss, grad_x.reshape(x.shape)]
    for k in range(4):
        outs += [out[n][k].reshape(w[n].shape) for n in WEIGHTS]
    return tuple(outs)
```

```python
import functools

import jax
import jax.numpy as jnp
from jax import lax
from jax.experimental import pallas as pl
from jax.experimental.pallas import tpu as pltpu

F32, BF16 = jnp.float32, jnp.bfloat16
MESH = pl.DeviceIdType.MESH

HEAD = 64
RW_HEADS = 16
RW_W = 1024
SWA_W = 1024
KV_W = 128
DECAY_LORA, AAA_LORA, GATE_LORA = 64, 64, 160
LORA_W = DECAY_LORA + AAA_LORA + GATE_LORA
SHIFT_COLS = 3 * RW_W + LORA_W
XH = 4
XHD = 512
MEM_LEN = 256
WINDOW = 128
GN_EPS = 64e-5
RMS_EPS = 1e-6
NEG_INF = -1e30
ADAM_LR, ADAM_B1, ADAM_B2, ADAM_EPS, ADAM_WD, ADAM_STEP = 0.001, 0.9, 0.999, 1e-08, 0.01, 10

VMEM_LIMIT = 56 * 1024 * 1024


def _cp(sem=None, **kw):
    return pltpu.CompilerParams(dimension_semantics=sem, vmem_limit_bytes=VMEM_LIMIT, **kw)


def _pick(dim, target):
    if dim <= target:
        return dim
    best = None
    for t in range(128, target + 1, 128):
        if dim % t == 0:
            best = t
    assert best is not None, (dim, target)
    return best


_DIMS = {"nn": (((1,), (0,)), ((), ())), "nt": (((1,), (1,)), ((), ())), "tn": (((0,), (0,)), ((), ()))}


def _mm(a, b, mode, name, out_dtype=F32, alpha=1.0, res=None, bias=None, tm=1024, tn=1024, tk=512, after=None):
    if mode == "nn":
        (M, K), (K2, N) = a.shape, b.shape
    elif mode == "nt":
        (M, K), (N, K2) = a.shape, b.shape
    else:
        (K, M), (K2, N) = a.shape, b.shape
    assert K == K2, (name, a.shape, b.shape)
    tm, tn, tk = _pick(M, tm), _pick(N, tn), _pick(K, tk)
    nk = K // tk
    a_spec = pl.BlockSpec((tk, tm), lambda i, j, k: (k, i)) if mode == "tn" else pl.BlockSpec((tm, tk), lambda i, j, k: (i, k))
    b_spec = pl.BlockSpec((tn, tk), lambda i, j, k: (j, k)) if mode == "nt" else pl.BlockSpec((tk, tn), lambda i, j, k: (k, j))
    o_spec = pl.BlockSpec((tm, tn), lambda i, j, k: (i, j))
    ins, specs = [a, b], [a_spec, b_spec]
    if res is not None:
        ins.append(res)
        specs.append(o_spec)
    if bias is not None:
        ins.append(bias)
        specs.append(pl.BlockSpec((1, tn), lambda i, j, k: (0, j)))
    if after is not None:
        ins.append(after)
        specs.append(pl.BlockSpec(memory_space=pl.ANY))
    dims = _DIMS[mode]

    def body(*refs):
        a_ref, b_ref = refs[0], refs[1]
        o_ref, acc_ref = refs[-2], refs[-1]
        k = pl.program_id(2)

        @pl.when(k == 0)
        def _():
            acc_ref[...] = jnp.zeros_like(acc_ref)

        acc_ref[...] += lax.dot_general(a_ref[...].astype(BF16), b_ref[...].astype(BF16), dims, preferred_element_type=F32)

        @pl.when(k == nk - 1)
        def _():
            o = acc_ref[...]
            if alpha != 1.0:
                o = o * alpha
            p = 2
            if res is not None:
                o = o + refs[p][...].astype(F32)
                p += 1
            if bias is not None:
                o = o + refs[p][...]
            o_ref[...] = o.astype(out_dtype)

    return pl.pallas_call(
        body, name=name, grid=(M // tm, N // tn, nk), in_specs=specs, out_specs=o_spec,
        out_shape=jax.ShapeDtypeStruct((M, N), out_dtype), scratch_shapes=[pltpu.VMEM((tm, tn), F32)],
        compiler_params=_cp(("parallel", "parallel", "arbitrary")),
    )(*ins)


def _rows(fn, name, T, tm, tiled, full, out_tiled, out_acc, extra=(), reverse=False, scratch=()):
    n = T // tm
    idx = (lambda i: n - 1 - i) if reverse else (lambda i: i)
    in_specs = [pl.BlockSpec((tm, a.shape[1]), lambda i: (idx(i), 0)) for a in tiled]
    in_specs += [mk(idx) for _, mk in extra]
    in_specs += [pl.BlockSpec(a.shape, lambda i, nd=a.ndim: (0,) * nd) for a in full]
    out_specs = [pl.BlockSpec((tm, c), lambda i: (idx(i), 0)) for c, _ in out_tiled]
    out_specs += [pl.BlockSpec(s, lambda i, nd=len(s): (0,) * nd) for s, _ in out_acc]
    out_shape = [jax.ShapeDtypeStruct((T, c), d) for c, d in out_tiled] + [jax.ShapeDtypeStruct(s, d) for s, d in out_acc]
    n_in = len(tiled) + len(extra) + len(full)
    n_t, n_a = len(out_tiled), len(out_acc)

    def body(*refs):
        step = pl.program_id(0)
        vals = [r[...] for r in refs[:n_in]]
        outs = fn(idx(step), *vals, *refs[n_in + n_t + n_a:])
        for r, v in zip(refs[n_in:n_in + n_t], outs[:n_t]):
            r[...] = v.astype(r.dtype)
        for r, v in zip(refs[n_in + n_t:n_in + n_t + n_a], outs[n_t:]):
            @pl.when(step == 0)
            def _(r=r):
                r[...] = jnp.zeros_like(r)

            r[...] += v

    return pl.pallas_call(
        body, name=name, grid=(n,), in_specs=in_specs, out_specs=out_specs, out_shape=out_shape,
        scratch_shapes=list(scratch), compiler_params=_cp(("arbitrary",)),
    )(*tiled, *[a for a, _ in extra], *full)


def _rms(x, g):
    return x * lax.rsqrt(jnp.mean(x * x, axis=-1, keepdims=True) + RMS_EPS) * g


def _rms_fwd(x, g, name, tm=256):
    (h,) = _rows(lambda i, x, g: (_rms(x, g),), name, x.shape[0], min(tm, x.shape[0]), [x], [g], [(x.shape[1], BF16)], [])
    return h


def _rms_bwd(x, g, dh, dres, name, tm=256):
    D = x.shape[1]

    def fn(i, x, dh, dres, g):
        _, vjp = jax.vjp(_rms, x, g)
        dx, dg = vjp(dh.astype(F32))
        dx = dx + dres
        return dx, dg, jnp.sum(dx, axis=0, keepdims=True)

    return _rows(fn, name, x.shape[0], tm, [x, dh, dres], [g], [(D, F32)], [((1, D), F32), ((1, D), F32)])


def _ffn_up(h, wg, wu, name, tm=1024, tn=512, tk=512):
    (M, K), N = h.shape, wg.shape[1]
    tm, tn, tk = _pick(M, tm), _pick(N, tn), _pick(K, tk)
    nk = K // tk

    def body(h_ref, wg_ref, wu_ref, g_ref, u_ref, a_ref, accg, accu):
        k = pl.program_id(2)

        @pl.when(k == 0)
        def _():
            accg[...] = jnp.zeros_like(accg)
            accu[...] = jnp.zeros_like(accu)

        hb = h_ref[...].astype(BF16)
        accg[...] += jnp.dot(hb, wg_ref[...].astype(BF16), preferred_element_type=F32)
        accu[...] += jnp.dot(hb, wu_ref[...].astype(BF16), preferred_element_type=F32)

        @pl.when(k == nk - 1)
        def _():
            g, u = accg[...], accu[...]
            g_ref[...] = g
            u_ref[...] = u
            a_ref[...] = (g * jax.nn.sigmoid(g) * u).astype(BF16)

    o_spec = pl.BlockSpec((tm, tn), lambda i, j, k: (i, j))
    w_spec = pl.BlockSpec((tk, tn), lambda i, j, k: (k, j))
    return pl.pallas_call(
        body, name=name, grid=(M // tm, N // tn, nk),
        in_specs=[pl.BlockSpec((tm, tk), lambda i, j, k: (i, k)), w_spec, w_spec], out_specs=[o_spec] * 3,
        out_shape=[jax.ShapeDtypeStruct((M, N), F32)] * 2 + [jax.ShapeDtypeStruct((M, N), BF16)],
        scratch_shapes=[pltpu.VMEM((tm, tn), F32)] * 2, compiler_params=_cp(("parallel", "parallel", "arbitrary")),
    )(h, wg, wu)


def _act_bwd(da, g, u, name, tm=256):
    def fn(i, da, g, u):
        s = jax.nn.sigmoid(g)
        return da * u * (s * (1.0 + g * (1.0 - s))), da * (g * s)

    F = g.shape[1]
    return _rows(fn, name, g.shape[0], tm, [da, g, u], [], [(F, BF16), (F, BF16)], [])


def _ffn_fwd(x, gain, wg, wu, wd, tag):
    h = _rms_fwd(x, gain, tag + "_norm")
    G, U, A = _ffn_up(h, wg, wu, tag + "_up")
    xo = _mm(A, wd, "nn", tag + "_down", alpha=0.5, res=x)
    return xo, (h, G, U, A)


def _ffn_bwd(x, gain, wg, wu, wd, saved, dxo, tag, send):
    h, G, U, A = saved
    dA = _mm(dxo, wd, "nt", tag + "_dA", alpha=0.5)
    dwd = _mm(A, dxo, "tn", tag + "_dwd", out_dtype=BF16, alpha=0.5)
    dG, dU = _act_bwd(dA, G, U, tag + "_act_bwd")
    dwg = _mm(h, dG, "tn", tag + "_dwg", out_dtype=BF16)
    dwu = _mm(h, dU, "tn", tag + "_dwu", out_dtype=BF16)
    sent = send(dwg, dwu, dwd)
    dh = _mm(dG, wg, "nt", tag + "_dh_g", after=sent)
    dh = _mm(dU, wu, "nt", tag + "_dh_u", res=dh)
    dx, dgain, _ = _rms_bwd(x, gain, dh, dxo, tag + "_norm_bwd")
    return dx, dgain


def _segsum64_impl(x):
    r = lax.broadcasted_iota(jnp.int32, (128, 128), 0) // HEAD
    c = lax.broadcasted_iota(jnp.int32, (128, 128), 1) // HEAD
    ones = (r == c).astype(BF16)
    hi = x.astype(BF16)
    lo = (x - hi.astype(F32)).astype(BF16)
    outs = []
    for q in range(x.shape[1] // 128):
        sl = slice(q * 128, (q + 1) * 128)
        outs.append(jnp.dot(hi[:, sl], ones, preferred_element_type=F32) + jnp.dot(lo[:, sl], ones, preferred_element_type=F32))
    return outs[0] if len(outs) == 1 else jnp.concatenate(outs, axis=1)


@jax.custom_vjp
def _segsum64(x):
    return _segsum64_impl(x)


_segsum64.defvjp(lambda x: (_segsum64_impl(x), None), lambda _, ct: (_segsum64_impl(ct),))


def _swap32(x):
    lane = lax.broadcasted_iota(jnp.int32, (x.shape[0], 128), 1)
    outs = [jnp.take_along_axis(x[:, q * 128:(q + 1) * 128], lane ^ 32, axis=1) for q in range(x.shape[1] // 128)]
    return outs[0] if len(outs) == 1 else jnp.concatenate(outs, axis=1)


def _tree_sum(xs):
    xs = list(xs)
    while len(xs) > 1:
        nxt = [xs[i] + xs[i + 1] for i in range(0, len(xs) - 1, 2)]
        if len(xs) % 2:
            nxt.append(xs[-1])
        xs = nxt
    return xs[0]


def _softplus(x):
    return jnp.maximum(x, 0.0) + jnp.log(1.0 + jnp.exp(-jnp.abs(x)))


def _pre_core(k, da, gd, w0, a0, k_k, k_a, w_da, gate_up):
    lane = lax.broadcasted_iota(jnp.int32, da.shape, 1)
    w_da = w_da.astype(BF16)
    l1 = jnp.dot(jnp.where(lane < DECAY_LORA, jnp.tanh(da), 0.0).astype(BF16), w_da, preferred_element_type=F32)
    l2 = jnp.dot(jnp.where(lane >= DECAY_LORA, da, 0.0).astype(BF16), w_da, preferred_element_type=F32)
    wlog = -_softplus(-(w0 + l1)) - 0.5
    decay = jnp.exp(-jnp.exp(wlog))
    a = jax.nn.sigmoid(a0 + l2)
    g = jnp.dot(jax.nn.sigmoid(gd).astype(BF16), gate_up.astype(BF16), preferred_element_type=F32)
    kk = k * k_k
    kkn = kk / jnp.maximum(jnp.sqrt(_segsum64(kk * kk)), 1e-12)
    k2 = k * (1.0 + (a - 1.0) * k_a)
    return decay, k2, -kkn, kkn * a, g


def _pre_shift(i, zr, zl, zr8, zl8, mu, mul):
    live = (i > 0).astype(F32)
    dz = _shift_down(zr, zr8[7:8, :] * live) - zr
    dzl = _shift_down(zl, zl8[7:8, :] * live) - zl
    return zr + dz * mu, zl + dzl * mul, dz, dzl


def _shift_down(x, first_row):
    rolled = pltpu.roll(x, 1, 0)
    row = lax.broadcasted_iota(jnp.int32, x.shape, 0)
    return jnp.where(row == 0, first_row, rolled)


def _shift_up(x, last_row):
    rolled = pltpu.roll(x, x.shape[0] - 1, 0)
    row = lax.broadcasted_iota(jnp.int32, x.shape, 0)
    return jnp.where(row == x.shape[0] - 1, last_row, rolled)


def _prev_rows_spec(tm, cols):
    return lambda idx: pl.BlockSpec((8, cols), lambda i: (jnp.maximum(idx(i) * (tm // 8) - 1, 0), 0))


def _rwkv_pre(p_rkv, p_lora, params, tm=256):
    T = p_rkv.shape[0]

    def fn(i, zr, zl, zr8, zl8, mu, mul, *ps):
        z, z2, _, _ = _pre_shift(i, zr, zl, zr8, zl8, mu, mul)
        decay, k2, an, bn, g = _pre_core(z[:, RW_W:2 * RW_W], z2[:, :128], z2[:, 128:], *ps)
        return z[:, :RW_W], decay, k2, z[:, 2 * RW_W:], an, bn, g

    extra = [(p_rkv, _prev_rows_spec(tm, 3 * RW_W)), (p_lora, _prev_rows_spec(tm, LORA_W))]
    return _rows(fn, "rwkv_pre", T, tm, [p_rkv, p_lora], list(params), [(RW_W, F32)] * 7, [], extra=extra)


def _rwkv_pre_bwd(p_rkv, p_lora, params, cts, tm=256):
    T = p_rkv.shape[0]
    n = T // tm

    def fn(i, zr, zl, cr, cdec, ck2, cv, can, cbn, cg, cr_b, ck2_b, cv_b, zr8, zl8, mu, mul, *rest):
        ps, (car, carl) = rest[:-2], rest[-2:]
        cr, ck2, cv = cr + cr_b, ck2 + ck2_b, cv + cv_b
        z, z2, dif, difl = _pre_shift(i, zr, zl, zr8, zl8, mu, mul)
        _, vjp = jax.vjp(_pre_core, z[:, RW_W:2 * RW_W], z2[:, :128], z2[:, 128:], *ps)
        dk, dda, dgd, *dps = vjp((cdec, ck2, can, cbn, cg))
        dz = jnp.concatenate([cr, dk, cv], axis=1)
        dz2 = jnp.concatenate([dda, dgd], axis=1)
        dzp, dzlp = dz * mu, dz2 * mul

        @pl.when(i == n - 1)
        def _():
            car[...] = jnp.zeros_like(car)
            carl[...] = jnp.zeros_like(carl)

        d_rkv = dz - dzp + _shift_up(dzp, car[0:1, :])
        d_lora = dz2 - dzlp + _shift_up(dzlp, carl[0:1, :])
        car[0:1, :] = dzp[0:1, :]
        carl[0:1, :] = dzlp[0:1, :]
        return (d_rkv, d_lora, jnp.sum(dz * dif, axis=0, keepdims=True), jnp.sum(dz2 * difl, axis=0, keepdims=True), *dps)

    extra = [(p_rkv, _prev_rows_spec(tm, 3 * RW_W)), (p_lora, _prev_rows_spec(tm, LORA_W))]
    acc = [(p.shape, F32) for p in params]
    return _rows(fn, "rwkv_pre_bwd", T, tm, [p_rkv, p_lora, *cts], list(params), [(3 * RW_W, BF16), (LORA_W, BF16)], acc,
                 extra=extra, reverse=True, scratch=[pltpu.VMEM((8, 3 * RW_W), F32), pltpu.VMEM((8, LORA_W), F32)])


def _post_core(y, r, k2, v, g, lw, lb, rk):
    mu = _segsum64(y) * (1.0 / HEAD)
    yc = y - mu
    var = _segsum64(yc * yc) * (1.0 / HEAD)
    yn = yc * lax.rsqrt(var + GN_EPS) * lw + lb
    return (yn + _segsum64(r * k2 * rk) * v) * g


def _rwkv_post(y, r, k2, v, g, lw, lb, rk, tm=256):
    (o,) = _rows(lambda i, *a: (_post_core(*a),), "rwkv_post", y.shape[0], tm, [y, r, k2, v, g], [lw, lb, rk], [(RW_W, BF16)], [])
    return o


def _rwkv_post_bwd(y, r, k2, v, g, lw, lb, rk, do, tm=256):
    def fn(i, y, r, k2, v, g, do, lw, lb, rk):
        _, vjp = jax.vjp(_post_core, y, r, k2, v, g, lw, lb, rk)
        return vjp(do.astype(F32))

    return _rows(fn, "rwkv_post_bwd", y.shape[0], tm, [y, r, k2, v, g, do], [lw, lb, rk], [(RW_W, F32)] * 5, [((1, RW_W), F32)] * 3)


SCAN_L = 32


def _to_tile(x):
    T = x.shape[0]
    return x.reshape(T, RW_HEADS, 8, 8).transpose(0, 2, 1, 3).reshape(T, 8, 128)


def _from_tile(x):
    T = x.shape[0]
    return x.reshape(T, 8, RW_HEADS, 8).transpose(0, 2, 1, 3).reshape(T, RW_W)


def _to_exp(x):
    T = x.shape[0]
    xt = x.reshape(T, RW_HEADS, HEAD).transpose(0, 2, 1)
    return jnp.broadcast_to(xt[..., None], (T, HEAD, RW_HEADS, 8)).reshape(T, HEAD, 128)


def _ck_a_to_b(ck):
    n = ck.shape[0]
    return ck.reshape(n, 8, 8, 8, RW_HEADS, 8).transpose(0, 3, 5, 1, 4, 2).reshape(n, HEAD, 8, 128)


def _scan_fwd(xes, vi):
    T, L = vi.shape[0], SCAN_L
    nch = T // L

    def body(*refs):
        xr, (vi_ref, yi_ref, sa_ref, ck_ref, st_ref) = refs[:5], refs[5:]

        @pl.when(pl.program_id(0) == 0)
        def _():
            st_ref[...] = jnp.zeros_like(st_ref)

        ck_ref[0] = st_ref[...]

        def step(t, carry):
            v = vi_ref[t]
            row = lambda m, j: jnp.broadcast_to(xr[m][t, pl.ds(j, 1), :], (8, 128))
            S = [st_ref[j] for j in range(HEAD)]
            sa = _tree_sum([S[j] * row(0, j) for j in range(HEAD)])
            sa_ref[t] = sa
            S = [S[j] * row(1, j) + row(2, j) * sa + row(3, j) * v for j in range(HEAD)]
            for j in range(HEAD):
                st_ref[j] = S[j]
            yi_ref[t] = _tree_sum([S[j] * row(4, j) for j in range(HEAD)])
            return carry

        lax.fori_loop(0, L, step, 0)

    tile = pl.BlockSpec((L, 8, 128), lambda c: (c, 0, 0))
    exp = pl.BlockSpec((L, HEAD, 128), lambda c: (c, 0, 0))
    return pl.pallas_call(
        body, name="rwkv_scan_fwd", grid=(nch,), in_specs=[exp] * 5 + [tile],
        out_specs=[tile, tile, pl.BlockSpec((1, HEAD, 8, 128), lambda c: (c, 0, 0, 0))],
        out_shape=[jax.ShapeDtypeStruct((T, 8, 128), F32)] * 2 + [jax.ShapeDtypeStruct((nch, HEAD, 8, 128), F32)],
        scratch_shapes=[pltpu.VMEM((HEAD, 8, 128), F32)], compiler_params=_cp(("arbitrary",)),
    )(*xes, vi)


def _scan_bwd_a(xes, dyi):
    T, L = dyi.shape[0], SCAN_L
    nch = T // L

    def body(*refs):
        xr, (dy_ref, dsa_ref, dv_ref, g_ref) = refs[:5], refs[5:]

        @pl.when(pl.program_id(0) == 0)
        def _():
            g_ref[...] = jnp.zeros_like(g_ref)

        def step(s, carry):
            t = L - 1 - s
            dy = dy_ref[t]
            row = lambda m, j: jnp.broadcast_to(xr[m][t, pl.ds(j, 1), :], (8, 128))
            G = [g_ref[j] + row(4, j) * dy for j in range(HEAD)]
            dsa = _tree_sum([G[j] * row(2, j) for j in range(HEAD)])
            dsa_ref[t] = dsa
            dv_ref[t] = _tree_sum([G[j] * row(3, j) for j in range(HEAD)])
            for j in range(HEAD):
                g_ref[j] = G[j] * row(1, j) + row(0, j) * dsa
            return carry

        lax.fori_loop(0, L, step, 0)

    tile = pl.BlockSpec((L, 8, 128), lambda c: (nch - 1 - c, 0, 0))
    exp = pl.BlockSpec((L, HEAD, 128), lambda c: (nch - 1 - c, 0, 0))
    return pl.pallas_call(
        body, name="rwkv_scan_bwd_a", grid=(nch,), in_specs=[exp] * 5 + [tile], out_specs=[tile, tile],
        out_shape=[jax.ShapeDtypeStruct((T, 8, 128), F32)] * 2,
        scratch_shapes=[pltpu.VMEM((HEAD, 8, 128), F32)], compiler_params=_cp(("arbitrary",)),
    )(*xes, dyi)


def _scan_bwd_b(xts, ies, ckb):
    T, L = xts[0].shape[0], SCAN_L
    nch = T // L

    def body(*refs):
        xr, er, ck_ref, dj, (hist, g_ref) = refs[:5], refs[5:9], refs[9], refs[10:15], refs[15:]

        @pl.when(pl.program_id(0) == 0)
        def _():
            g_ref[...] = jnp.zeros_like(g_ref)

        hist[0] = ck_ref[0]

        def fstep(t, carry):
            w, B, k = xr[1][t], xr[2][t], xr[3][t]
            row = lambda m, i: jnp.broadcast_to(er[m][t, pl.ds(i, 1), :], (8, 128))
            for i in range(HEAD):
                hist[t + 1, i] = hist[t, i] * w + row(1, i) * B + row(0, i) * k
            return carry

        lax.fori_loop(0, L, fstep, 0)

        def bstep(s, carry):
            t = L - 1 - s
            A, w, r = xr[0][t], xr[1][t], xr[4][t]
            row = lambda m, i: jnp.broadcast_to(er[m][t, pl.ds(i, 1), :], (8, 128))
            G = [g_ref[i] + row(2, i) * r for i in range(HEAD)]
            Sp = [hist[t, i] for i in range(HEAD)]
            dj[4][t] = _tree_sum([hist[t + 1, i] * row(2, i) for i in range(HEAD)])
            dj[1][t] = _tree_sum([G[i] * Sp[i] for i in range(HEAD)])
            dj[2][t] = _tree_sum([G[i] * row(1, i) for i in range(HEAD)])
            dj[3][t] = _tree_sum([G[i] * row(0, i) for i in range(HEAD)])
            dj[0][t] = _tree_sum([Sp[i] * row(3, i) for i in range(HEAD)])
            for i in range(HEAD):
                g_ref[i] = G[i] * w + row(3, i) * A
            return carry

        lax.fori_loop(0, L, bstep, 0)

    tile = pl.BlockSpec((L, 8, 128), lambda c: (nch - 1 - c, 0, 0))
    exp = pl.BlockSpec((L, HEAD, 128), lambda c: (nch - 1 - c, 0, 0))
    return pl.pallas_call(
        body, name="rwkv_scan_bwd_b", grid=(nch,),
        in_specs=[tile] * 5 + [exp] * 4 + [pl.BlockSpec((1, HEAD, 8, 128), lambda c: (nch - 1 - c, 0, 0, 0))],
        out_specs=[tile] * 5, out_shape=[jax.ShapeDtypeStruct((T, 8, 128), F32)] * 5,
        scratch_shapes=[pltpu.VMEM((L + 1, HEAD, 8, 128), F32), pltpu.VMEM((HEAD, 8, 128), F32)], compiler_params=_cp(("arbitrary",)),
    )(*xts, *ies, ckb)


SWA_COLS = SWA_W + 2 * KV_W
BLK = 128


def _swa_core(n, k2a, k2b, vla, vra, vlb, vrb, sinks, *qps):
    iq = lax.broadcasted_iota(jnp.int32, (BLK, 2 * BLK), 0)
    ik = lax.broadcasted_iota(jnp.int32, (BLK, 2 * BLK), 1)
    diff = BLK + iq - ik
    valid = (diff >= 0) & (diff < WINDOW) & ((n > 0) | (ik >= BLK))
    lane = lax.broadcasted_iota(jnp.int32, (BLK, 128), 1)
    lane1 = lax.broadcasted_iota(jnp.int32, (1, 128), 1)
    nt = (((1,), (1,)), ((), ()))
    outs = []
    for pp in range(8):
        k2, vl, vr = (k2a, vla, vra) if pp < 4 else (k2b, vlb, vrb)
        qp = qps[pp]
        o = None
        for half, vv in ((0, vl), (1, vr)):
            qh = jnp.where((lane >= HEAD) == (half == 1), qp, 0.0).astype(BF16)
            s = lax.dot_general(qh, k2.astype(BF16), nt, preferred_element_type=F32) * (HEAD ** -0.5)
            s = jnp.where(valid, s, NEG_INF)
            sink = jnp.sum(jnp.where(lane1 == 2 * pp + half, sinks, 0.0), axis=1, keepdims=True)
            m = jnp.maximum(jnp.max(s, axis=1, keepdims=True), sink)
            p = jnp.exp(s - m)
            den = jnp.sum(p, axis=1, keepdims=True) + jnp.exp(sink - m)
            oh = jnp.dot((p / den).astype(BF16), vv.astype(BF16), preferred_element_type=F32)
            o = oh if o is None else o + oh
        outs.append(o)
    return jnp.concatenate(outs, axis=1)


def _swa_prep(pc, pp, b, cq, sq, ckc, skc, ckp, skp):
    zc, zp = pc + b, pp + b
    qr = zc[:, :SWA_W] * cq + _swap32(zc[:, :SWA_W]) * sq
    kc, kp = zc[:, SWA_W:SWA_W + KV_W], zp[:, SWA_W:SWA_W + KV_W]
    kb = jnp.concatenate([kp * ckp + _swap32(kp) * skp, kc * ckc + _swap32(kc) * skc], axis=0)
    vb = jnp.concatenate([zp[:, SWA_W + KV_W:], zc[:, SWA_W + KV_W:]], axis=0)
    lane = lax.broadcasted_iota(jnp.int32, kb.shape, 1)
    left = lane < HEAD
    kbr, vbr = pltpu.roll(kb, HEAD, 1), pltpu.roll(vb, HEAD, 1)
    return (jnp.where(left, kb, kbr), jnp.where(left, kbr, kb), jnp.where(left, vb, 0.0), jnp.where(left, 0.0, vbr),
            jnp.where(left, vbr, 0.0), jnp.where(left, 0.0, vb)), [qr[:, q * 128:(q + 1) * 128] for q in range(8)]


def _swa_specs(T, tabs_q, tabs_k):
    cur = lambda c: pl.BlockSpec((BLK, c), lambda n: (n, 0))
    prev = lambda c: pl.BlockSpec((BLK, c), lambda n: (jnp.maximum(n - 1, 0), 0))
    return cur, prev


def _swa_fwd(p_swa, b, sinks, cq, sq, ck, sk):
    T = p_swa.shape[0]
    cur, prev = _swa_specs(T, None, None)

    def body(pc, pp, b_ref, s_ref, cq_r, sq_r, ckc, skc, ckp, skp, o_ref):
        ops, qps = _swa_prep(pc[...], pp[...], b_ref[...], cq_r[...], sq_r[...], ckc[...], skc[...], ckp[...], skp[...])
        o_ref[...] = _swa_core(pl.program_id(0), *ops, s_ref[...], *qps).astype(o_ref.dtype)

    full = lambda a: pl.BlockSpec(a.shape, lambda n: (0, 0))
    return pl.pallas_call(
        body, name="swa_fwd", grid=(T // BLK,),
        in_specs=[cur(SWA_COLS), prev(SWA_COLS), full(b), full(sinks), cur(SWA_W), cur(SWA_W), cur(KV_W), cur(KV_W), prev(KV_W), prev(KV_W)],
        out_specs=cur(SWA_W), out_shape=jax.ShapeDtypeStruct((T, SWA_W), BF16), compiler_params=_cp(("arbitrary",)),
    )(p_swa, p_swa, b, sinks, cq, sq, ck, sk, ck, sk)


def _swa_bwd(p_swa, b, sinks, cq, sq, ck, sk, do):
    T = p_swa.shape[0]
    nb = T // BLK
    cur = lambda c: pl.BlockSpec((BLK, c), lambda s: (nb - 1 - s, 0))
    prev = lambda c: pl.BlockSpec((BLK, c), lambda s: (jnp.maximum(nb - 2 - s, 0), 0))

    def body(pc, pp, b_ref, s_ref, cq_r, sq_r, ckc, skc, ckp, skp, do_ref, dcur, db, dsk, carry):
        step = pl.program_id(0)
        n = nb - 1 - step

        @pl.when(step == 0)
        def _():
            carry[...] = jnp.zeros_like(carry)
            db[...] = jnp.zeros_like(db)
            dsk[...] = jnp.zeros_like(dsk)

        ops, qps = _swa_prep(pc[...], pp[...], b_ref[...], cq_r[...], sq_r[...], ckc[...], skc[...], ckp[...], skp[...])
        _, vjp = jax.vjp(functools.partial(_swa_core, n), *ops, s_ref[...], *qps)
        dk2a, dk2b, dvla, dvra, dvlb, dvrb, dsinks, *dqps = vjp(do_ref[...].astype(F32))
        dqr = jnp.concatenate(dqps, axis=1)
        lane = lax.broadcasted_iota(jnp.int32, dk2a.shape, 1)
        left = lane < HEAD
        dkb = jnp.where(left, dk2a + pltpu.roll(dk2a, HEAD, 1), dk2b + pltpu.roll(dk2b, HEAD, 1))
        dvb = jnp.where(left, dvla + pltpu.roll(dvra, HEAD, 1), pltpu.roll(dvlb, HEAD, 1) + dvrb)
        dq = dqr * cq_r[...] + _swap32(dqr * sq_r[...])
        dkp, dkc = dkb[:BLK], dkb[BLK:]
        dkp = dkp * ckp[...] + _swap32(dkp * skp[...])
        dkc = dkc * ckc[...] + _swap32(dkc * skc[...])
        dc = jnp.concatenate([dq, jnp.concatenate([dkc, dvb[BLK:]], axis=1) + carry[...]], axis=1)
        carry[...] = jnp.concatenate([dkp, dvb[:BLK]], axis=1)
        dcur[...] = dc.astype(dcur.dtype)
        db[...] += jnp.sum(dc, axis=0, keepdims=True)
        dsk[...] += dsinks

    full = lambda a: pl.BlockSpec(a.shape, lambda s: (0, 0))
    return pl.pallas_call(
        body, name="swa_bwd", grid=(nb,),
        in_specs=[cur(SWA_COLS), prev(SWA_COLS), full(b), full(sinks), cur(SWA_W), cur(SWA_W), cur(KV_W), cur(KV_W), prev(KV_W), prev(KV_W),
                  cur(SWA_W)],
        out_specs=[cur(SWA_COLS), full(b), full(sinks)],
        out_shape=[jax.ShapeDtypeStruct((T, SWA_COLS), BF16), jax.ShapeDtypeStruct(b.shape, F32), jax.ShapeDtypeStruct(sinks.shape, F32)],
        scratch_shapes=[pltpu.VMEM((BLK, 2 * KV_W), F32)], compiler_params=_cp(("arbitrary",)),
    )(p_swa, p_swa, b, sinks, cq, sq, ck, sk, ck, sk, do)


def _rope_tables(T):
    inv = 10000.0 ** (-jnp.arange(0, HEAD, 2, dtype=F32) / HEAD)
    ang = jnp.arange(T, dtype=F32)[:, None] * inv[None, :]
    c = jnp.concatenate([jnp.cos(ang), jnp.cos(ang)], axis=1)
    s = jnp.concatenate([-jnp.sin(ang), jnp.sin(ang)], axis=1)
    return jnp.tile(c, (1, 16)), jnp.tile(s, (1, 16)), jnp.tile(c, (1, 2)), jnp.tile(s, (1, 2))


def _xattn_core(*qkv):
    outs = []
    for h in range(XH):
        qh, kh, vh = qkv[h], qkv[XH + h], qkv[2 * XH + h]
        s = lax.dot_general(qh.astype(BF16), kh.astype(BF16), (((1,), (1,)), ((), ())), preferred_element_type=F32) * (XHD ** -0.5)
        p = jnp.exp(s - jnp.max(s, axis=1, keepdims=True))
        p = p / jnp.sum(p, axis=1, keepdims=True)
        outs.append(jnp.dot(p.astype(BF16), vh.astype(BF16), preferred_element_type=F32))
    return jnp.concatenate(outs, axis=1)


def _xattn_split(q, kv):
    return [q[:, h * XHD:(h + 1) * XHD] for h in range(XH)] + [kv[:, h * XHD:(h + 1) * XHD] for h in range(2 * XH)]


def _xattn_fwd(q, kv, tm=256):
    (o,) = _rows(lambda i, q, kv: (_xattn_core(*_xattn_split(q, kv)),), "xattn_fwd", q.shape[0], tm, [q], [kv], [(q.shape[1], BF16)], [])
    return o


def _xattn_bwd(q, kv, do, tm=256):
    def fn(i, q, do, kv):
        _, vjp = jax.vjp(_xattn_core, *_xattn_split(q, kv))
        d = vjp(do.astype(F32))
        return jnp.concatenate(d[:XH], axis=1), jnp.concatenate(d[XH:], axis=1)

    return _rows(fn, "xattn_bwd", q.shape[0], tm, [q, do], [kv], [(q.shape[1], BF16)], [(kv.shape, F32)])


def _loss_head(x, g, tgt, tm=256):
    D = x.shape[1]

    def fn(i, x, tgt, g):
        y, vjp = jax.vjp(_rms, x, g)
        err = y - tgt
        dx, dg = vjp(err * (1.0 / D))
        part = 0.5 / D * jnp.sum(jnp.sum(err * err, axis=1, keepdims=True), axis=0, keepdims=True)
        return dx, jnp.broadcast_to(part, (1, 128)), dg

    return _rows(fn, "loss_head", x.shape[0], tm, [x, tgt], [g], [(D, F32)], [((1, 128), F32), ((1, D), F32)])


def _local_step(x, mem, tgt, get_w, P, put_g):
    T = x.shape[0]
    W = dict(get_w("f1", None))
    x1, s1 = _ffn_fwd(x, P["f1_norm"], W["f1_gate"], W["f1_up"], W["f1_down"], "f1")

    W.update(get_w("mix", x1))
    h2 = _rms_fwd(x1, P["mix_norm"], "mix_norm")
    w_rkv, w_lora, w_swa = W["w_inT"][:3 * RW_W], W["w_inT"][3 * RW_W:SHIFT_COLS], W["w_inT"][SHIFT_COLS:]
    p_rkv = _mm(h2, w_rkv, "nt", "in_rkv")
    p_lora = _mm(h2, w_lora, "nt", "in_lora")
    p_swa = _mm(h2, w_swa, "nt", "in_swa")
    w_da = jnp.concatenate([W["rw_decay_up"], W["rw_aaa_up"]], axis=0)
    pre_params = (P["rw_mu"][:, :3 * RW_W], P["rw_mu"][:, 3 * RW_W:], P["rw_w0"], P["rw_a0"], P["rw_k_k"], P["rw_k_a"], w_da,
                  W["rw_gate_up"])
    r, decay, k2, v, an, bn, g = _rwkv_pre(p_rkv, p_lora, pre_params)
    scan_vecs = (an, decay, bn, k2, r)
    xes = [_to_exp(a) for a in scan_vecs]
    yi, sai, ck = _scan_fwd(xes, _to_tile(v))
    y_scan = _from_tile(yi)
    y_rw = _rwkv_post(y_scan, r, k2, v, g, P["rw_lnx_w"], P["rw_lnx_b"], P["rw_r_k"])
    cq, sq, ckt, skt = _rope_tables(T)
    y_swa = _swa_fwd(p_swa, P["b_in_attn"], P["attn_sinks"], cq, sq, ckt, skt)
    ycat = jnp.concatenate([y_rw, y_swa], axis=1)
    x2 = _mm(ycat, W["w_out"], "nn", "out_proj", res=x1, bias=P["b_out"])

    W.update(get_w("xattn", x2))
    hx = _rms_fwd(x2, P["xa_norm"], "xa_norm")
    mn = _rms_fwd(mem, P["mem_norm"], "mem_norm")
    q = _mm(hx, W["w_xq"], "nn", "xq", out_dtype=BF16)
    kv = _mm(mn, W["w_xkv"], "nn", "xkv", out_dtype=BF16)
    o = _xattn_fwd(q, kv)
    x3 = _mm(o, W["w_xo"], "nn", "xo", res=x2)

    W.update(get_w("f2", x3))
    x4, s2 = _ffn_fwd(x3, P["f2_norm"], W["f2_gate"], W["f2_up"], W["f2_down"], "f2")
    dx4, loss_part, d_final = _loss_head(x4, P["final_norm"], tgt)

    gs = {"final_norm": d_final}
    dx3, gs["f2_norm"] = _ffn_bwd(x3, P["f2_norm"], W["f2_gate"], W["f2_up"], W["f2_down"], s2, dx4, "f2",
                                  lambda dwg, dwu, dwd: put_g("f2", {"f2_gate": dwg, "f2_up": dwu, "f2_down": dwd}))

    do = _mm(dx3, W["w_xo"], "nt", "xo_do", out_dtype=BF16)
    dw_xo = _mm(o, dx3, "tn", "xo_dw", out_dtype=BF16)
    dq, dkv = _xattn_bwd(q, kv, do)
    dw_xq = _mm(hx, dq, "tn", "xq_dw", out_dtype=BF16)
    dw_xkv = _mm(mn, dkv, "tn", "xkv_dw", out_dtype=BF16)
    sent = put_g("xattn", {"w_xq": dw_xq, "w_xkv": dw_xkv, "w_xo": dw_xo})
    dhx = _mm(dq, W["w_xq"], "nt", "xq_dh", after=sent)
    dmn = _mm(dkv, W["w_xkv"], "nt", "xkv_dmn")
    _, gs["mem_norm"], _ = _rms_bwd(mem, P["mem_norm"], dmn, jnp.zeros_like(mem), "mem_norm_bwd")
    dx2, gs["xa_norm"], gs["b_out"] = _rms_bwd(x2, P["xa_norm"], dhx, dx3, "xa_norm_bwd")

    dycat = _mm(dx2, W["w_out"], "nt", "out_dy")
    dw_out = _mm(ycat, dx2, "tn", "out_dw", out_dtype=BF16)
    dp_swa, gs["b_in_attn"], gs["attn_sinks"] = _swa_bwd(p_swa, P["b_in_attn"], P["attn_sinks"], cq, sq, ckt, skt, dycat[:, RW_W:])
    dy_scan, dr_b, dk2_b, dv_b, dg, gs["rw_lnx_w"], gs["rw_lnx_b"], gs["rw_r_k"] = _rwkv_post_bwd(
        y_scan, r, k2, v, g, P["rw_lnx_w"], P["rw_lnx_b"], P["rw_r_k"], dycat[:, :RW_W])
    dsai, dvi = _scan_bwd_a(xes, _to_tile(dy_scan))
    ies = [_to_exp(v), _to_exp(_from_tile(sai)), _to_exp(dy_scan), _to_exp(_from_tile(dsai))]
    dj = _scan_bwd_b([_to_tile(a) for a in scan_vecs], ies, _ck_a_to_b(ck))
    dan, ddecay, dbn, dk2_s, dr_s = (_from_tile(d) for d in dj)
    cts = (dr_s, ddecay, dk2_s, _from_tile(dvi), dan, dbn, dg, dr_b, dk2_b, dv_b)
    dp_rkv, dp_lora, dmu, dmul, gs["rw_w0"], gs["rw_a0"], gs["rw_k_k"], gs["rw_k_a"], dw_da, gs["rw_gate_up"] = _rwkv_pre_bwd(
        p_rkv, p_lora, pre_params, cts)
    gs["rw_mu"] = jnp.concatenate([dmu, dmul], axis=1)
    gs["rw_decay_up"], gs["rw_aaa_up"] = dw_da[:DECAY_LORA], dw_da[DECAY_LORA:]
    dw_inT = jnp.concatenate([_mm(dp_rkv, h2, "tn", "in_dw_rkv"), _mm(dp_lora, h2, "tn", "in_dw_lora"),
                              _mm(dp_swa, h2, "tn", "in_dw_swa")], axis=0)
    sent = put_g("mix", {"w_inT": dw_inT, "w_out": dw_out})
    dh2 = _mm(dp_rkv, w_rkv, "nn", "in_dh_rkv", after=sent)
    dh2 = _mm(dp_lora, w_lora, "nn", "in_dh_lora", res=dh2)
    dh2 = _mm(dp_swa, w_swa, "nn", "in_dh_swa", res=dh2)
    dx1, gs["mix_norm"], _ = _rms_bwd(x1, P["mix_norm"], dh2, dx2, "mix_norm_bwd")

    dx0, gs["f1_norm"] = _ffn_bwd(x, P["f1_norm"], W["f1_gate"], W["f1_up"], W["f1_down"], s1, dx1, "f1",
                                  lambda dwg, dwu, dwd: put_g("f1", {"f1_gate": dwg, "f1_up": dwu, "f1_down": dwd}))
    return loss_part, dx0, gs


_ANY = pl.BlockSpec(memory_space=pl.ANY)
_OTHER_CHIPS = ((1, 0), (0, 1), (1, 1))


def _mesh_pos():
    return lax.axis_index("x"), lax.axis_index("y"), lax.axis_index("c")


def _slot(ref, kind, s, rows, cols):
    if kind == "row":
        return ref.at[pl.ds(pl.multiple_of(s * rows, 8), rows), :]
    return ref.at[:, pl.ds(pl.multiple_of(s * cols, 128), cols)]


_HBM = pl.BlockSpec(memory_space=pltpu.HBM)
_SEMS = pl.BlockSpec(memory_space=pltpu.SEMAPHORE)
_SPLIT = dict(compiler_params=pltpu.CompilerParams(has_side_effects=pltpu.SideEffectType.DATAFLOW_SIDE_EFFECTING))


def _in_hbm(a):
    return pltpu.with_memory_space_constraint(a, pltpu.HBM)


def _full_shape(s, kind):
    return (4 * s.shape[0], s.shape[1]) if kind == "row" else (s.shape[0], 4 * s.shape[1])


def _gather_start(shards, kinds, groups):
    n, ng = len(shards), len(groups)
    lands = [_in_hbm(lax.empty(_full_shape(s, k), s.dtype)) for s, k in zip(shards, kinds)]

    def body(*refs):
        src, land, sems, token = refs[:n], refs[n:2 * n], refs[2 * n:2 * n + 3 * ng], refs[-1]
        x, y, c = _mesh_pos()
        me = 2 * x + y
        for gi, idxs in enumerate(groups):
            send, recv, own = sems[3 * gi:3 * gi + 3]
            for k, i in enumerate(idxs):
                mine = _slot(land[i], kinds[i], me, *src[i].shape)
                for r, (dx, dy) in enumerate(_OTHER_CHIPS):
                    pltpu.make_async_remote_copy(src[i], mine, send.at[3 * k + r], recv.at[3 * k + r],
                                                 device_id=((x + dx) % 2, (y + dy) % 2, c), device_id_type=MESH).start()
                pltpu.make_async_copy(src[i], mine, own.at[k]).start()
        token[...] = jnp.zeros_like(token)

    sem_shapes = [pltpu.SemaphoreType.DMA((w * len(g),)) for g in groups for w in (3, 3, 1)]
    thru = [pltpu.HBM(a.shape, a.dtype) for a in (*shards, *lands)]
    res = pl.pallas_call(
        body, name="gather_start", in_specs=[_HBM] * (2 * n),
        out_specs=[_SEMS] * (3 * ng) + [_HBM] * (2 * n) + [pl.BlockSpec(memory_space=pltpu.VMEM)],
        out_shape=sem_shapes + thru + [jax.ShapeDtypeStruct((8, 128), F32)],
        input_output_aliases={i: 3 * ng + i for i in range(2 * n)}, **_SPLIT,
    )(*[_in_hbm(s) for s in shards], *lands)
    return res[:3 * ng], res[3 * ng:3 * ng + n], res[3 * ng + n:3 * ng + 2 * n], res[-1]


def _gather_wait(name, sems, shards, lands, kinds, after):
    m = len(shards)

    def body(*refs):
        src, land, (send, recv, own) = refs[:m], refs[m:2 * m], refs[2 * m:2 * m + 3]
        x, y, c = _mesh_pos()
        me = 2 * x + y
        for k in range(m):
            mine = _slot(land[k], kinds[k], me, *src[k].shape)
            for r in range(3):
                cp = pltpu.make_async_remote_copy(src[k], mine, send.at[3 * k + r], recv.at[3 * k + r],
                                                  device_id=(x, y, c), device_id_type=MESH)
                cp.wait_send()
                cp.wait_recv()
            pltpu.make_async_copy(src[k], mine, own.at[k]).wait()

    thru = [pltpu.HBM(a.shape, a.dtype) for a in (*shards, *lands)]
    res = pl.pallas_call(
        body, name=name, in_specs=[_HBM] * (2 * m) + [_SEMS] * 3 + [pl.BlockSpec(memory_space=pl.ANY)],
        out_specs=[_HBM] * (2 * m), out_shape=thru, input_output_aliases={i: i for i in range(2 * m)}, **_SPLIT,
    )(*shards, *lands, *sems, after)
    return res[m:]


def _scatter_start(name, grads, kinds):
    m = len(grads)
    shard_shape = [(g.shape[0] // 4, g.shape[1]) if k == "row" else (g.shape[0], g.shape[1] // 4) for g, k in zip(grads, kinds)]
    lands = [_in_hbm(lax.empty((4, *s), g.dtype)) for s, g in zip(shard_shape, grads)]

    def body(*refs):
        src, land, (send, recv, own) = refs[:m], refs[m:2 * m], refs[2 * m:2 * m + 3]
        x, y, c = _mesh_pos()
        me = 2 * x + y
        for k in range(m):
            for r, (dx, dy) in enumerate(_OTHER_CHIPS):
                tx, ty = (x + dx) % 2, (y + dy) % 2
                pltpu.make_async_remote_copy(_slot(src[k], kinds[k], 2 * tx + ty, *shard_shape[k]), land[k].at[me],
                                             send.at[3 * k + r], recv.at[3 * k + r], device_id=(tx, ty, c), device_id_type=MESH).start()
            pltpu.make_async_copy(_slot(src[k], kinds[k], me, *shard_shape[k]), land[k].at[me], own.at[k]).start()
        refs[-1][...] = jnp.zeros_like(refs[-1])

    thru = [pltpu.HBM(a.shape, a.dtype) for a in (*grads, *lands)]
    res = pl.pallas_call(
        body, name=name, in_specs=[_HBM] * (2 * m),
        out_specs=[_SEMS] * 3 + [_HBM] * (2 * m) + [pl.BlockSpec(memory_space=pltpu.VMEM)],
        out_shape=[pltpu.SemaphoreType.DMA((3 * m,))] * 2 + [pltpu.SemaphoreType.DMA((m,))] + thru + [jax.ShapeDtypeStruct((8, 128), F32)],
        input_output_aliases={i: 3 + i for i in range(2 * m)}, **_SPLIT,
    )(*[_in_hbm(g) for g in grads], *lands)
    return res[:3], res[3:3 + m], res[3 + m:3 + 2 * m], res[-1]


def _scatter_wait(name, sems, grads, lands, kinds, after):
    m = len(grads)

    def body(*refs):
        src, land, (send, recv, own) = refs[:m], refs[m:2 * m], refs[2 * m:2 * m + 3]
        x, y, c = _mesh_pos()
        me = 2 * x + y
        for k in range(m):
            mine = _slot(src[k], kinds[k], me, *land[k].shape[1:])
            for r in range(3):
                cp = pltpu.make_async_remote_copy(mine, land[k].at[me], send.at[3 * k + r], recv.at[3 * k + r],
                                                  device_id=(x, y, c), device_id_type=MESH)
                cp.wait_send()
                cp.wait_recv()
            pltpu.make_async_copy(mine, land[k].at[me], own.at[k]).wait()

    thru = [pltpu.HBM(a.shape, a.dtype) for a in (*grads, *lands)]
    res = pl.pallas_call(
        body, name=name, in_specs=[_HBM] * (2 * m) + [_SEMS] * 3 + [pl.BlockSpec(memory_space=pl.ANY)],
        out_specs=[_HBM] * (2 * m), out_shape=thru, input_output_aliases={i: i for i in range(2 * m)}, **_SPLIT,
    )(*grads, *lands, *sems, after)
    return res[m:]


def _swap_with_sibling(arrs, name):
    n = len(arrs)

    def body(*refs):
        ins, outs = refs[:n], refs[n:2 * n]
        send, recv = refs[2 * n:]
        x, y, c = _mesh_pos()
        copies = []
        for i in range(n):
            rc = pltpu.make_async_remote_copy(ins[i], outs[i], send.at[i], recv.at[i], device_id=(x, y, 1 - c), device_id_type=MESH)
            rc.start()
            copies.append(rc)
        for rc in copies:
            rc.wait()

    return pl.pallas_call(
        body, name=name, in_specs=[_ANY] * n, out_specs=[_ANY] * n,
        out_shape=[jax.ShapeDtypeStruct(a.shape, a.dtype) for a in arrs],
        scratch_shapes=[pltpu.SemaphoreType.DMA((n,)), pltpu.SemaphoreType.DMA((n,))],
    )(*arrs)


def _gather_small(pack, after):
    def body(in_ref, after_ref, out_ref, send, recv, loc):
        x, y, c = _mesh_pos()
        me = 4 * x + 2 * y + c
        cp = pltpu.make_async_copy(in_ref, out_ref.at[me], loc.at[0])
        cp.start()
        copies = [cp]
        for r in range(1, 8):
            dx, dy, dc = r // 4, (r // 2) % 2, r % 2
            rc = pltpu.make_async_remote_copy(in_ref, out_ref.at[me], send.at[r - 1], recv.at[r - 1],
                                              device_id=((x + dx) % 2, (y + dy) % 2, (c + dc) % 2), device_id_type=MESH)
            rc.start()
            copies.append(rc)
        for cp in copies:
            cp.wait()

    return pl.pallas_call(
        body, name="gather_small", in_specs=[_ANY, _ANY], out_specs=_ANY, out_shape=jax.ShapeDtypeStruct((8, *pack.shape), pack.dtype),
        scratch_shapes=[pltpu.SemaphoreType.DMA((7,)), pltpu.SemaphoreType.DMA((7,)), pltpu.SemaphoreType.DMA((1,))],
    )(pack, after)


def _row_tile(R, dtype, target=256):
    mult = 8 * 4 // jnp.dtype(dtype).itemsize
    best = R
    for t in range(mult, min(R, target) + 1, mult):
        if R % t == 0:
            best = t
    return best


def _sum_slots(stack, name, out_dtype=F32):
    k, R, C = stack.shape
    tr = _row_tile(R, stack.dtype)

    def body(s_ref, o_ref):
        acc = s_ref[0].astype(F32)
        for j in range(1, k):
            acc = acc + s_ref[j].astype(F32)
        o_ref[...] = acc.astype(out_dtype)

    return pl.pallas_call(
        body, name=name, grid=(R // tr,), in_specs=[pl.BlockSpec((k, tr, C), lambda i: (0, i, 0))],
        out_specs=pl.BlockSpec((tr, C), lambda i: (i, 0)), out_shape=jax.ShapeDtypeStruct((R, C), out_dtype),
        compiler_params=_cp(("parallel",)),
    )(stack)


def _adamw(w, m, v, ga, gb, name):
    R, C = w.shape
    tr = _row_tile(R, F32, 128)
    gs = [ga] if gb is None else [ga, gb]

    def body(*refs):
        w_ref, m_ref, v_ref = refs[:3]
        g = refs[3][...]
        if gb is not None:
            g = g + refs[4][...]
        g_ref, d_ref, nm_ref, nv_ref = refs[-4:]
        nm = ADAM_B1 * m_ref[...] + (1.0 - ADAM_B1) * g
        nv = ADAM_B2 * v_ref[...] + (1.0 - ADAM_B2) * (g * g)
        m_hat = nm / (1.0 - ADAM_B1 ** ADAM_STEP)
        v_hat = nv / (1.0 - ADAM_B2 ** ADAM_STEP)
        g_ref[...] = g
        d_ref[...] = -ADAM_LR * (m_hat / (jnp.sqrt(v_hat) + ADAM_EPS) + ADAM_WD * w_ref[...])
        nm_ref[...] = nm
        nv_ref[...] = nv

    spec = pl.BlockSpec((tr, C), lambda i: (i, 0))
    return pl.pallas_call(
        body, name=name, grid=(R // tr,), in_specs=[spec] * (3 + len(gs)), out_specs=[spec] * 4,
        out_shape=[jax.ShapeDtypeStruct((R, C), F32)] * 4, compiler_params=_cp(("parallel",)),
    )(w, m, v, *gs)


def _pack(arrs):
    rows = []
    for a in arrs:
        flat = a.reshape(-1)
        rows.append(jnp.pad(flat, (0, -flat.shape[0] % 1024)).reshape(-1, 1024))
    p = jnp.concatenate(rows, axis=0)
    return jnp.pad(p, ((0, -p.shape[0] % 8), (0, 0)))


def _unpack(p, shapes):
    out, r = [], 0
    for s in shapes:
        n = 1
        for d in s:
            n *= d
        nr = -(-n // 1024)
        out.append(p[r:r + nr].reshape(-1)[:n].reshape(s))
        r += nr
    return out


BIG = ("f1_gate", "f1_up", "f1_down", "w_in", "w_out", "w_xq", "w_xkv", "w_xo", "f2_gate", "f2_up", "f2_down")
BIG_KIND = {"f1_gate": "col", "f1_up": "col", "f1_down": "row", "w_in": "row", "w_out": "row", "w_xq": "row", "w_xkv": "col",
            "w_xo": "row", "f2_gate": "col", "f2_up": "col", "f2_down": "row"}
LORA = ("rw_decay_up", "rw_aaa_up", "rw_gate_up")
WEIGHTS = ("f1_norm", "f1_gate", "f1_up", "f1_down", "mix_norm", "w_in", "b_in_attn", "rw_mu", "rw_w0", "rw_decay_up", "rw_a0",
           "rw_aaa_up", "rw_gate_up", "rw_k_k", "rw_k_a", "rw_r_k", "rw_lnx_w", "rw_lnx_b", "attn_sinks", "w_out", "b_out", "xa_norm",
           "mem_norm", "w_xq", "w_xkv", "w_xo", "f2_norm", "f2_gate", "f2_up", "f2_down", "final_norm")
SMALL = tuple(n for n in WEIGHTS if n not in BIG)
GROUP_ORDER = ("f1", "mix", "xattn", "f2")
GROUPS = {"f1": ("f1_gate", "f1_up", "f1_down"), "mix": ("w_in", "w_out") + LORA, "xattn": ("w_xq", "w_xkv", "w_xo"),
          "f2": ("f2_gate", "f2_up", "f2_down")}


def kernel(x, mem, f1_norm, f1_gate, f1_up, f1_down, mix_norm, w_in, b_in_attn, rw_mu, rw_w0, rw_decay_up, rw_a0, rw_aaa_up, rw_gate_up, rw_k_k, rw_k_a, rw_r_k, rw_lnx_w, rw_lnx_b, attn_sinks, w_out, b_out, xa_norm, mem_norm, w_xq, w_xkv, w_xo, f2_norm, f2_gate, f2_up, f2_down, final_norm, loss_target, m_f1_norm, m_f1_gate, m_f1_up, m_f1_down, m_mix_norm, m_w_in, m_b_in_attn, m_rw_mu, m_rw_w0, m_rw_decay_up, m_rw_a0, m_rw_aaa_up, m_rw_gate_up, m_rw_k_k, m_rw_k_a, m_rw_r_k, m_rw_lnx_w, m_rw_lnx_b, m_attn_sinks, m_w_out, m_b_out, m_xa_norm, m_mem_norm, m_w_xq, m_w_xkv, m_w_xo, m_f2_norm, m_f2_gate, m_f2_up, m_f2_down, m_final_norm, v_f1_norm, v_f1_gate, v_f1_up, v_f1_down, v_mix_norm, v_w_in, v_b_in_attn, v_rw_mu, v_rw_w0, v_rw_decay_up, v_rw_a0, v_rw_aaa_up, v_rw_gate_up, v_rw_k_k, v_rw_k_a, v_rw_r_k, v_rw_lnx_w, v_rw_lnx_b, v_attn_sinks, v_w_out, v_b_out, v_xa_norm, v_mem_norm, v_w_xq, v_w_xkv, v_w_xo, v_f2_norm, v_f2_gate, v_f2_up, v_f2_down, v_final_norm):
    a = dict(locals())
    w = {n: a[n] for n in WEIGHTS}
    m = {n: a["m_" + n] for n in WEIGHTS}
    v = {n: a["v_" + n] for n in WEIGHTS}
    sq = lambda t: t.reshape(t.shape[-2:]) if t.ndim == 3 else t.reshape(1, -1)

    local_name = lambda n: "w_inT" if n == "w_in" else n
    kind_of = lambda n: BIG_KIND.get(n, "col")
    shards, kinds, groups = [], [], []
    for grp in GROUP_ORDER:
        groups.append(list(range(len(shards), len(shards) + len(GROUPS[grp]))))
        for n in GROUPS[grp]:
            shards.append(sq(w[n]).T if n == "w_in" else sq(w[n]) if n in LORA else sq(w[n]).astype(BF16))
            kinds.append(kind_of(n))
    sems, src_thru, land_thru, token = _gather_start(shards, kinds, groups)

    def get_w(grp, after):
        gi = GROUP_ORDER.index(grp)
        got = _gather_wait("gather_wait_" + grp, sems[3 * gi:3 * gi + 3], [src_thru[i] for i in groups[gi]],
                           [land_thru[i] for i in groups[gi]], [kinds[i] for i in groups[gi]], token if after is None else after)
        return {local_name(n): f for n, f in zip(GROUPS[grp], got)}

    in_flight = {}

    def put_g(grp, gw):
        names = [n for n in GROUPS[grp] if n in BIG]
        *flight, sent = _scatter_start("scatter_start_" + grp, [gw[local_name(n)] for n in names], [kind_of(n) for n in names])
        in_flight[grp] = (names, flight)
        return sent

    P = {n: sq(w[n]) for n in SMALL if n not in LORA}
    P["attn_sinks"] = jnp.pad(P["attn_sinks"], ((0, 0), (0, 128 - P["attn_sinks"].shape[1])))
    P["rw_r_k"] = w["rw_r_k"].reshape(1, RW_W)
    loss_part, grad_x, gs = _local_step(x[0], mem[0], loss_target[0], get_w, P, put_g)
    loss = lax.psum(loss_part[0, 0], ("x", "y", "c"))

    out, after = {}, grad_x
    for grp in reversed(GROUP_ORDER):
        names, (g_sems, g_thru, l_thru) = in_flight[grp]
        stacks = _scatter_wait("scatter_wait_" + grp, g_sems, g_thru, l_thru, [kind_of(n) for n in names], after)
        partial = [_sum_slots(s, "sum_chips_" + n) for s, n in zip(stacks, names)]
        sibling = _swap_with_sibling(partial, "swap_" + grp)
        for n, pa, sb in zip(names, partial, sibling):
            if n == "w_in":
                pa, sb = pa.T, sb.T
            out[n] = _adamw(sq(w[n]), sq(m[n]), sq(v[n]), pa, sb, "adamw_" + n)
        after = out[names[-1]][1]

    gs["attn_sinks"] = gs["attn_sinks"][:, :16]
    gsum = _sum_slots(_gather_small(_pack([gs[n] for n in SMALL]), after), "sum_small")
    g_small = dict(zip(SMALL, _unpack(gsum, [gs[n].shape for n in SMALL])))
    shard = 2 * lax.axis_index("x") + lax.axis_index("y")
    for n in LORA:
        cols = w[n].shape[-1]
        g_small[n] = lax.dynamic_slice_in_dim(g_small[n], shard * cols, cols, axis=1)

    flat = lambda d: _pack([d[n] for n in SMALL])
    res = _adamw(flat(w), flat(m), flat(v), _pack([g_small[n] for n in SMALL]), None, "adamw_small")
    shapes = [w[n].shape for n in SMALL]
    for k, p in enumerate(res):
        for n, t in zip(SMALL, _unpack(p, shapes)):
            out.setdefault(n, [None] * 4)[k] = t
    outs = [loss, grad_x.reshape(x.shape)]
    for k in range(4):
        outs += [out[n][k].reshape(w[n].shape) for n in WEIGHTS]
    return tuple(outs)
```

```python
import functools

import jax
import jax.numpy as jnp
from jax import lax
from jax.experimental import pallas as pl
from jax.experimental.pallas import tpu as pltpu

F32, BF16 = jnp.float32, jnp.bfloat16
MESH = pl.DeviceIdType.MESH

HEAD = 64
RW_HEADS = 16
RW_W = 1024
SWA_W = 1024
KV_W = 128
DECAY_LORA, AAA_LORA, GATE_LORA = 64, 64, 160
LORA_W = DECAY_LORA + AAA_LORA + GATE_LORA
SHIFT_COLS = 3 * RW_W + LORA_W
XH = 4
XHD = 512
MEM_LEN = 256
WINDOW = 128
GN_EPS = 64e-5
RMS_EPS = 1e-6
NEG_INF = -1e30
ADAM_LR, ADAM_B1, ADAM_B2, ADAM_EPS, ADAM_WD, ADAM_STEP = 0.001, 0.9, 0.999, 1e-08, 0.01, 10

VMEM_LIMIT = 56 * 1024 * 1024


def _cp(sem=None, **kw):
    return pltpu.CompilerParams(dimension_semantics=sem, vmem_limit_bytes=VMEM_LIMIT, **kw)


def _pick(dim, target):
    if dim <= target:
        return dim
    best = None
    for t in range(128, target + 1, 128):
        if dim % t == 0:
            best = t
    assert best is not None, (dim, target)
    return best


_DIMS = {"nn": (((1,), (0,)), ((), ())), "nt": (((1,), (1,)), ((), ())), "tn": (((0,), (0,)), ((), ()))}


def _mm(a, b, mode, name, out_dtype=F32, alpha=1.0, res=None, bias=None, tm=1024, tn=1024, tk=512, after=None):
    if mode == "nn":
        (M, K), (K2, N) = a.shape, b.shape
    elif mode == "nt":
        (M, K), (N, K2) = a.shape, b.shape
    else:
        (K, M), (K2, N) = a.shape, b.shape
    assert K == K2, (name, a.shape, b.shape)
    tm, tn, tk = _pick(M, tm), _pick(N, tn), _pick(K, tk)
    nk = K // tk
    a_spec = pl.BlockSpec((tk, tm), lambda i, j, k: (k, i)) if mode == "tn" else pl.BlockSpec((tm, tk), lambda i, j, k: (i, k))
    b_spec = pl.BlockSpec((tn, tk), lambda i, j, k: (j, k)) if mode == "nt" else pl.BlockSpec((tk, tn), lambda i, j, k: (k, j))
    o_spec = pl.BlockSpec((tm, tn), lambda i, j, k: (i, j))
    ins, specs = [a, b], [a_spec, b_spec]
    if res is not None:
        ins.append(res)
        specs.append(o_spec)
    if bias is not None:
        ins.append(bias)
        specs.append(pl.BlockSpec((1, tn), lambda i, j, k: (0, j)))
    if after is not None:
        ins.append(after)
        specs.append(pl.BlockSpec(memory_space=pl.ANY))
    dims = _DIMS[mode]

    def body(*refs):
        a_ref, b_ref = refs[0], refs[1]
        o_ref, acc_ref = refs[-2], refs[-1]
        k = pl.program_id(2)

        @pl.when(k == 0)
        def _():
            acc_ref[...] = jnp.zeros_like(acc_ref)

        acc_ref[...] += lax.dot_general(a_ref[...].astype(BF16), b_ref[...].astype(BF16), dims, preferred_element_type=F32)

        @pl.when(k == nk - 1)
        def _():
            o = acc_ref[...]
            if alpha != 1.0:
                o = o * alpha
            p = 2
            if res is not None:
                o = o + refs[p][...].astype(F32)
                p += 1
            if bias is not None:
                o = o + refs[p][...]
            o_ref[...] = o.astype(out_dtype)

    return pl.pallas_call(
        body, name=name, grid=(M // tm, N // tn, nk), in_specs=specs, out_specs=o_spec,
        out_shape=jax.ShapeDtypeStruct((M, N), out_dtype), scratch_shapes=[pltpu.VMEM((tm, tn), F32)],
        compiler_params=_cp(("parallel", "parallel", "arbitrary")),
    )(*ins)


def _rows(fn, name, T, tm, tiled, full, out_tiled, out_acc, extra=(), reverse=False, scratch=()):
    n = T // tm
    idx = (lambda i: n - 1 - i) if reverse else (lambda i: i)
    in_specs = [pl.BlockSpec((tm, a.shape[1]), lambda i: (idx(i), 0)) for a in tiled]
    in_specs += [mk(idx) for _, mk in extra]
    in_specs += [pl.BlockSpec(a.shape, lambda i, nd=a.ndim: (0,) * nd) for a in full]
    out_specs = [pl.BlockSpec((tm, c), lambda i: (idx(i), 0)) for c, _ in out_tiled]
    out_specs += [pl.BlockSpec(s, lambda i, nd=len(s): (0,) * nd) for s, _ in out_acc]
    out_shape = [jax.ShapeDtypeStruct((T, c), d) for c, d in out_tiled] + [jax.ShapeDtypeStruct(s, d) for s, d in out_acc]
    n_in = len(tiled) + len(extra) + len(full)
    n_t, n_a = len(out_tiled), len(out_acc)

    def body(*refs):
        step = pl.program_id(0)
        vals = [r[...] for r in refs[:n_in]]
        outs = fn(idx(step), *vals, *refs[n_in + n_t + n_a:])
        for r, v in zip(refs[n_in:n_in + n_t], outs[:n_t]):
            r[...] = v.astype(r.dtype)
        for r, v in zip(refs[n_in + n_t:n_in + n_t + n_a], outs[n_t:]):
            @pl.when(step == 0)
            def _(r=r):
                r[...] = jnp.zeros_like(r)

            r[...] += v

    return pl.pallas_call(
        body, name=name, grid=(n,), in_specs=in_specs, out_specs=out_specs, out_shape=out_shape,
        scratch_shapes=list(scratch), compiler_params=_cp(("arbitrary",)),
    )(*tiled, *[a for a, _ in extra], *full)


def _rms(x, g):
    return x * lax.rsqrt(jnp.mean(x * x, axis=-1, keepdims=True) + RMS_EPS) * g


def _rms_fwd(x, g, name, tm=256):
    (h,) = _rows(lambda i, x, g: (_rms(x, g),), name, x.shape[0], min(tm, x.shape[0]), [x], [g], [(x.shape[1], BF16)], [])
    return h


def _rms_bwd(x, g, dh, dres, name, tm=256):
    D = x.shape[1]

    def fn(i, x, dh, dres, g):
        _, vjp = jax.vjp(_rms, x, g)
        dx, dg = vjp(dh.astype(F32))
        dx = dx + dres
        return dx, dg, jnp.sum(dx, axis=0, keepdims=True)

    return _rows(fn, name, x.shape[0], tm, [x, dh, dres], [g], [(D, F32)], [((1, D), F32), ((1, D), F32)])


def _ffn_up(h, wg, wu, name, tm=1024, tn=512, tk=512, after=None):
    (M, K), N = h.shape, wg.shape[1]
    tm, tn, tk = _pick(M, tm), _pick(N, tn), _pick(K, tk)
    nk = K // tk

    def body(*refs):
        h_ref, wg_ref, wu_ref = refs[:3]
        g_ref, u_ref, a_ref, accg, accu = refs[-5:]
        k = pl.program_id(2)

        @pl.when(k == 0)
        def _():
            accg[...] = jnp.zeros_like(accg)
            accu[...] = jnp.zeros_like(accu)

        hb = h_ref[...].astype(BF16)
        accg[...] += jnp.dot(hb, wg_ref[...].astype(BF16), preferred_element_type=F32)
        accu[...] += jnp.dot(hb, wu_ref[...].astype(BF16), preferred_element_type=F32)

        @pl.when(k == nk - 1)
        def _():
            g, u = accg[...], accu[...]
            g_ref[...] = g
            u_ref[...] = u
            a_ref[...] = (g * jax.nn.sigmoid(g) * u).astype(BF16)

    o_spec = pl.BlockSpec((tm, tn), lambda i, j, k: (i, j))
    w_spec = pl.BlockSpec((tk, tn), lambda i, j, k: (k, j))
    extra = [] if after is None else [after]
    return pl.pallas_call(
        body, name=name, grid=(M // tm, N // tn, nk),
        in_specs=[pl.BlockSpec((tm, tk), lambda i, j, k: (i, k)), w_spec, w_spec] + [pl.BlockSpec(memory_space=pl.ANY)] * len(extra),
        out_specs=[o_spec] * 3, out_shape=[jax.ShapeDtypeStruct((M, N), F32)] * 2 + [jax.ShapeDtypeStruct((M, N), BF16)],
        scratch_shapes=[pltpu.VMEM((tm, tn), F32)] * 2, compiler_params=_cp(("parallel", "parallel", "arbitrary")),
    )(h, wg, wu, *extra)


def _act_bwd(da, g, u, name, tm=256):
    def fn(i, da, g, u):
        s = jax.nn.sigmoid(g)
        return da * u * (s * (1.0 + g * (1.0 - s))), da * (g * s)

    F = g.shape[1]
    return _rows(fn, name, g.shape[0], tm, [da, g, u], [], [(F, BF16), (F, BF16)], [])


def _ffn_fwd(x, gain, wg, wu, wd, tag, after=None):
    h = _rms_fwd(x, gain, tag + "_norm")
    G, U, A = _ffn_up(h, wg, wu, tag + "_up", after=after)
    xo = _mm(A, wd, "nn", tag + "_down", alpha=0.5, res=x)
    return xo, (h, G, U, A)


def _ffn_bwd(x, gain, wg, wu, wd, saved, dxo, tag, send):
    h, G, U, A = saved
    dA = _mm(dxo, wd, "nt", tag + "_dA", alpha=0.5)
    dwd = _mm(A, dxo, "tn", tag + "_dwd", out_dtype=BF16, alpha=0.5)
    dG, dU = _act_bwd(dA, G, U, tag + "_act_bwd")
    dwg = _mm(h, dG, "tn", tag + "_dwg", out_dtype=BF16)
    dwu = _mm(h, dU, "tn", tag + "_dwu", out_dtype=BF16)
    sent = send(dwg, dwu, dwd)
    dh = _mm(dG, wg, "nt", tag + "_dh_g", after=sent)
    dh = _mm(dU, wu, "nt", tag + "_dh_u", res=dh)
    dx, dgain, _ = _rms_bwd(x, gain, dh, dxo, tag + "_norm_bwd")
    return dx, dgain


def _segsum64_impl(x):
    r = lax.broadcasted_iota(jnp.int32, (128, 128), 0) // HEAD
    c = lax.broadcasted_iota(jnp.int32, (128, 128), 1) // HEAD
    ones = (r == c).astype(BF16)
    hi = x.astype(BF16)
    lo = (x - hi.astype(F32)).astype(BF16)
    outs = []
    for q in range(x.shape[1] // 128):
        sl = slice(q * 128, (q + 1) * 128)
        outs.append(jnp.dot(hi[:, sl], ones, preferred_element_type=F32) + jnp.dot(lo[:, sl], ones, preferred_element_type=F32))
    return outs[0] if len(outs) == 1 else jnp.concatenate(outs, axis=1)


@jax.custom_vjp
def _segsum64(x):
    return _segsum64_impl(x)


_segsum64.defvjp(lambda x: (_segsum64_impl(x), None), lambda _, ct: (_segsum64_impl(ct),))


def _swap32(x):
    lane = lax.broadcasted_iota(jnp.int32, (x.shape[0], 128), 1)
    outs = [jnp.take_along_axis(x[:, q * 128:(q + 1) * 128], lane ^ 32, axis=1) for q in range(x.shape[1] // 128)]
    return outs[0] if len(outs) == 1 else jnp.concatenate(outs, axis=1)


def _tree_sum(xs):
    xs = list(xs)
    while len(xs) > 1:
        nxt = [xs[i] + xs[i + 1] for i in range(0, len(xs) - 1, 2)]
        if len(xs) % 2:
            nxt.append(xs[-1])
        xs = nxt
    return xs[0]


def _softplus(x):
    return jnp.maximum(x, 0.0) + jnp.log(1.0 + jnp.exp(-jnp.abs(x)))


def _pre_core(k, da, gd, w0, a0, k_k, k_a, w_da, gate_up):
    lane = lax.broadcasted_iota(jnp.int32, da.shape, 1)
    w_da = w_da.astype(BF16)
    l1 = jnp.dot(jnp.where(lane < DECAY_LORA, jnp.tanh(da), 0.0).astype(BF16), w_da, preferred_element_type=F32)
    l2 = jnp.dot(jnp.where(lane >= DECAY_LORA, da, 0.0).astype(BF16), w_da, preferred_element_type=F32)
    wlog = -_softplus(-(w0 + l1)) - 0.5
    decay = jnp.exp(-jnp.exp(wlog))
    a = jax.nn.sigmoid(a0 + l2)
    g = jnp.dot(jax.nn.sigmoid(gd).astype(BF16), gate_up.astype(BF16), preferred_element_type=F32)
    kk = k * k_k
    kkn = kk / jnp.maximum(jnp.sqrt(_segsum64(kk * kk)), 1e-12)
    k2 = k * (1.0 + (a - 1.0) * k_a)
    return decay, k2, -kkn, kkn * a, g


def _pre_shift(i, zr, zl, zr8, zl8, mu, mul):
    live = (i > 0).astype(F32)
    dz = _shift_down(zr, zr8[7:8, :] * live) - zr
    dzl = _shift_down(zl, zl8[7:8, :] * live) - zl
    return zr + dz * mu, zl + dzl * mul, dz, dzl


def _shift_down(x, first_row):
    rolled = pltpu.roll(x, 1, 0)
    row = lax.broadcasted_iota(jnp.int32, x.shape, 0)
    return jnp.where(row == 0, first_row, rolled)


def _shift_up(x, last_row):
    rolled = pltpu.roll(x, x.shape[0] - 1, 0)
    row = lax.broadcasted_iota(jnp.int32, x.shape, 0)
    return jnp.where(row == x.shape[0] - 1, last_row, rolled)


def _prev_rows_spec(tm, cols):
    return lambda idx: pl.BlockSpec((8, cols), lambda i: (jnp.maximum(idx(i) * (tm // 8) - 1, 0), 0))


def _rwkv_pre(p_rkv, p_lora, params, tm=256):
    T = p_rkv.shape[0]

    def fn(i, zr, zl, zr8, zl8, mu, mul, *ps):
        z, z2, _, _ = _pre_shift(i, zr, zl, zr8, zl8, mu, mul)
        decay, k2, an, bn, g = _pre_core(z[:, RW_W:2 * RW_W], z2[:, :128], z2[:, 128:], *ps)
        return z[:, :RW_W], decay, k2, z[:, 2 * RW_W:], an, bn, g

    extra = [(p_rkv, _prev_rows_spec(tm, 3 * RW_W)), (p_lora, _prev_rows_spec(tm, LORA_W))]
    return _rows(fn, "rwkv_pre", T, tm, [p_rkv, p_lora], list(params), [(RW_W, F32)] * 7, [], extra=extra)


def _rwkv_pre_bwd(p_rkv, p_lora, params, cts, tm=256):
    T = p_rkv.shape[0]
    n = T // tm

    def fn(i, zr, zl, cr, cdec, ck2, cv, can, cbn, cg, cr_b, ck2_b, cv_b, zr8, zl8, mu, mul, *rest):
        ps, (car, carl) = rest[:-2], rest[-2:]
        cr, ck2, cv = cr + cr_b, ck2 + ck2_b, cv + cv_b
        z, z2, dif, difl = _pre_shift(i, zr, zl, zr8, zl8, mu, mul)
        _, vjp = jax.vjp(_pre_core, z[:, RW_W:2 * RW_W], z2[:, :128], z2[:, 128:], *ps)
        dk, dda, dgd, *dps = vjp((cdec, ck2, can, cbn, cg))
        dz = jnp.concatenate([cr, dk, cv], axis=1)
        dz2 = jnp.concatenate([dda, dgd], axis=1)
        dzp, dzlp = dz * mu, dz2 * mul

        @pl.when(i == n - 1)
        def _():
            car[...] = jnp.zeros_like(car)
            carl[...] = jnp.zeros_like(carl)

        d_rkv = dz - dzp + _shift_up(dzp, car[0:1, :])
        d_lora = dz2 - dzlp + _shift_up(dzlp, carl[0:1, :])
        car[0:1, :] = dzp[0:1, :]
        carl[0:1, :] = dzlp[0:1, :]
        return (d_rkv, d_lora, jnp.sum(dz * dif, axis=0, keepdims=True), jnp.sum(dz2 * difl, axis=0, keepdims=True), *dps)

    extra = [(p_rkv, _prev_rows_spec(tm, 3 * RW_W)), (p_lora, _prev_rows_spec(tm, LORA_W))]
    acc = [(p.shape, F32) for p in params]
    return _rows(fn, "rwkv_pre_bwd", T, tm, [p_rkv, p_lora, *cts], list(params), [(3 * RW_W, BF16), (LORA_W, BF16)], acc,
                 extra=extra, reverse=True, scratch=[pltpu.VMEM((8, 3 * RW_W), F32), pltpu.VMEM((8, LORA_W), F32)])


def _post_core(y, r, k2, v, g, lw, lb, rk):
    mu = _segsum64(y) * (1.0 / HEAD)
    yc = y - mu
    var = _segsum64(yc * yc) * (1.0 / HEAD)
    yn = yc * lax.rsqrt(var + GN_EPS) * lw + lb
    return (yn + _segsum64(r * k2 * rk) * v) * g


def _rwkv_post(y, r, k2, v, g, lw, lb, rk, tm=256):
    (o,) = _rows(lambda i, *a: (_post_core(*a),), "rwkv_post", y.shape[0], tm, [y, r, k2, v, g], [lw, lb, rk], [(RW_W, BF16)], [])
    return o


def _rwkv_post_bwd(y, r, k2, v, g, lw, lb, rk, do, tm=256):
    def fn(i, y, r, k2, v, g, do, lw, lb, rk):
        _, vjp = jax.vjp(_post_core, y, r, k2, v, g, lw, lb, rk)
        return vjp(do.astype(F32))

    return _rows(fn, "rwkv_post_bwd", y.shape[0], tm, [y, r, k2, v, g, do], [lw, lb, rk], [(RW_W, F32)] * 5, [((1, RW_W), F32)] * 3)


SCAN_L = 32


def _to_tile(x):
    T = x.shape[0]
    return x.reshape(T, RW_HEADS, 8, 8).transpose(0, 2, 1, 3).reshape(T, 8, 128)


def _from_tile(x):
    T = x.shape[0]
    return x.reshape(T, 8, RW_HEADS, 8).transpose(0, 2, 1, 3).reshape(T, RW_W)


def _to_perm(x):
    T = x.shape[0]
    return x.reshape(T, RW_HEADS, HEAD).transpose(0, 2, 1).reshape(T, 8, 128)


def _expand_chunk(srcs, e_ref, L):
    s = lax.broadcasted_iota(jnp.int32, (8, 128), 0)
    lane = lax.broadcasted_iota(jnp.int32, (8, 128), 1)
    idx = 16 * s + lane // 8

    unroll = 4

    def step(tt, carry):
        for u in range(unroll):
            t = tt * unroll + u
            for m, r in enumerate(srcs):
                for g in range(8):
                    row = jnp.broadcast_to(r[t, pl.ds(g, 1), :], (8, 128))
                    e_ref[t, m, g * 8:(g + 1) * 8, :] = jnp.take_along_axis(row, idx, axis=1)
        return carry

    lax.fori_loop(0, L // unroll, step, 0)


def _ck_a_to_b(ck):
    n = ck.shape[0]
    return ck.reshape(n, 8, 8, 8, RW_HEADS, 8).transpose(0, 3, 5, 1, 4, 2).reshape(n, HEAD, 8, 128)


def _scan_fwd(xes, vi):
    T, L = vi.shape[0], SCAN_L
    nch = T // L

    def body(*refs):
        xr, (vi_ref, yi_ref, sa_ref, ck_ref, st_ref, e_ref) = refs[:5], refs[5:]

        @pl.when(pl.program_id(0) == 0)
        def _():
            st_ref[...] = jnp.zeros_like(st_ref)

        ck_ref[0] = st_ref[...]
        _expand_chunk(xr, e_ref, L)

        def step(t, carry):
            v = vi_ref[t]
            row = lambda m, j: jnp.broadcast_to(e_ref[t, m, pl.ds(j, 1), :], (8, 128))
            S = [st_ref[j] for j in range(HEAD)]
            sa = _tree_sum([S[j] * row(0, j) for j in range(HEAD)])
            sa_ref[t] = sa
            S = [S[j] * row(1, j) + row(2, j) * sa + row(3, j) * v for j in range(HEAD)]
            for j in range(HEAD):
                st_ref[j] = S[j]
            yi_ref[t] = _tree_sum([S[j] * row(4, j) for j in range(HEAD)])
            return carry

        lax.fori_loop(0, L, step, 0)

    tile = pl.BlockSpec((L, 8, 128), lambda c: (c, 0, 0))
    return pl.pallas_call(
        body, name="rwkv_scan_fwd", grid=(nch,), in_specs=[tile] * 6,
        out_specs=[tile, tile, pl.BlockSpec((1, HEAD, 8, 128), lambda c: (c, 0, 0, 0))],
        out_shape=[jax.ShapeDtypeStruct((T, 8, 128), F32)] * 2 + [jax.ShapeDtypeStruct((nch, HEAD, 8, 128), F32)],
        scratch_shapes=[pltpu.VMEM((HEAD, 8, 128), F32), pltpu.VMEM((L, 5, HEAD, 128), F32)], compiler_params=_cp(("arbitrary",)),
    )(*xes, vi)


def _scan_bwd_a(xes, dyi):
    T, L = dyi.shape[0], SCAN_L
    nch = T // L

    def body(*refs):
        xr, (dy_ref, dsa_ref, dv_ref, g_ref, e_ref) = refs[:5], refs[5:]

        @pl.when(pl.program_id(0) == 0)
        def _():
            g_ref[...] = jnp.zeros_like(g_ref)

        _expand_chunk(xr, e_ref, L)

        def step(s, carry):
            t = L - 1 - s
            dy = dy_ref[t]
            row = lambda m, j: jnp.broadcast_to(e_ref[t, m, pl.ds(j, 1), :], (8, 128))
            G = [g_ref[j] + row(4, j) * dy for j in range(HEAD)]
            dsa = _tree_sum([G[j] * row(2, j) for j in range(HEAD)])
            dsa_ref[t] = dsa
            dv_ref[t] = _tree_sum([G[j] * row(3, j) for j in range(HEAD)])
            for j in range(HEAD):
                g_ref[j] = G[j] * row(1, j) + row(0, j) * dsa
            return carry

        lax.fori_loop(0, L, step, 0)

    tile = pl.BlockSpec((L, 8, 128), lambda c: (nch - 1 - c, 0, 0))
    return pl.pallas_call(
        body, name="rwkv_scan_bwd_a", grid=(nch,), in_specs=[tile] * 6, out_specs=[tile, tile],
        out_shape=[jax.ShapeDtypeStruct((T, 8, 128), F32)] * 2,
        scratch_shapes=[pltpu.VMEM((HEAD, 8, 128), F32), pltpu.VMEM((L, 5, HEAD, 128), F32)], compiler_params=_cp(("arbitrary",)),
    )(*xes, dyi)


def _scan_bwd_b(xts, ies, ckb):
    T, L = xts[0].shape[0], SCAN_L
    nch = T // L

    def body(*refs):
        xr, er, ck_ref, dj, (hist, g_ref, e_ref) = refs[:5], refs[5:9], refs[9], refs[10:15], refs[15:]

        @pl.when(pl.program_id(0) == 0)
        def _():
            g_ref[...] = jnp.zeros_like(g_ref)

        hist[0] = ck_ref[0]
        _expand_chunk(er, e_ref, L)

        def fstep(t, carry):
            w, B, k = xr[1][t], xr[2][t], xr[3][t]
            row = lambda m, i: jnp.broadcast_to(e_ref[t, m, pl.ds(i, 1), :], (8, 128))
            for i in range(HEAD):
                hist[t + 1, i] = hist[t, i] * w + row(1, i) * B + row(0, i) * k
            return carry

        lax.fori_loop(0, L, fstep, 0)

        def bstep(s, carry):
            t = L - 1 - s
            A, w, r = xr[0][t], xr[1][t], xr[4][t]
            row = lambda m, i: jnp.broadcast_to(e_ref[t, m, pl.ds(i, 1), :], (8, 128))
            G = [g_ref[i] + row(2, i) * r for i in range(HEAD)]
            Sp = [hist[t, i] for i in range(HEAD)]
            dj[4][t] = _tree_sum([hist[t + 1, i] * row(2, i) for i in range(HEAD)])
            dj[1][t] = _tree_sum([G[i] * Sp[i] for i in range(HEAD)])
            dj[2][t] = _tree_sum([G[i] * row(1, i) for i in range(HEAD)])
            dj[3][t] = _tree_sum([G[i] * row(0, i) for i in range(HEAD)])
            dj[0][t] = _tree_sum([Sp[i] * row(3, i) for i in range(HEAD)])
            for i in range(HEAD):
                g_ref[i] = G[i] * w + row(3, i) * A
            return carry

        lax.fori_loop(0, L, bstep, 0)

    tile = pl.BlockSpec((L, 8, 128), lambda c: (nch - 1 - c, 0, 0))
    return pl.pallas_call(
        body, name="rwkv_scan_bwd_b", grid=(nch,),
        in_specs=[tile] * 9 + [pl.BlockSpec((1, HEAD, 8, 128), lambda c: (nch - 1 - c, 0, 0, 0))],
        out_specs=[tile] * 5, out_shape=[jax.ShapeDtypeStruct((T, 8, 128), F32)] * 5,
        scratch_shapes=[pltpu.VMEM((L + 1, HEAD, 8, 128), F32), pltpu.VMEM((HEAD, 8, 128), F32), pltpu.VMEM((L, 4, HEAD, 128), F32)],
        compiler_params=_cp(("arbitrary",)),
    )(*xts, *ies, ckb)


SWA_COLS = SWA_W + 2 * KV_W
BLK = 128


def _swa_core(n, k2a, k2b, vla, vra, vlb, vrb, sinks, *qps):
    iq = lax.broadcasted_iota(jnp.int32, (BLK, 2 * BLK), 0)
    ik = lax.broadcasted_iota(jnp.int32, (BLK, 2 * BLK), 1)
    diff = BLK + iq - ik
    valid = (diff >= 0) & (diff < WINDOW) & ((n > 0) | (ik >= BLK))
    lane = lax.broadcasted_iota(jnp.int32, (BLK, 128), 1)
    lane1 = lax.broadcasted_iota(jnp.int32, (1, 128), 1)
    nt = (((1,), (1,)), ((), ()))
    outs = []
    for pp in range(8):
        k2, vl, vr = (k2a, vla, vra) if pp < 4 else (k2b, vlb, vrb)
        qp = qps[pp]
        o = None
        for half, vv in ((0, vl), (1, vr)):
            qh = jnp.where((lane >= HEAD) == (half == 1), qp, 0.0).astype(BF16)
            s = lax.dot_general(qh, k2.astype(BF16), nt, preferred_element_type=F32) * (HEAD ** -0.5)
            s = jnp.where(valid, s, NEG_INF)
            sink = jnp.sum(jnp.where(lane1 == 2 * pp + half, sinks, 0.0), axis=1, keepdims=True)
            m = jnp.maximum(jnp.max(s, axis=1, keepdims=True), sink)
            p = jnp.exp(s - m)
            den = jnp.sum(p, axis=1, keepdims=True) + jnp.exp(sink - m)
            oh = jnp.dot((p / den).astype(BF16), vv.astype(BF16), preferred_element_type=F32)
            o = oh if o is None else o + oh
        outs.append(o)
    return jnp.concatenate(outs, axis=1)


def _swa_prep(pc, pp, b, cq, sq, ckc, skc, ckp, skp):
    zc, zp = pc + b, pp + b
    qr = zc[:, :SWA_W] * cq + _swap32(zc[:, :SWA_W]) * sq
    kc, kp = zc[:, SWA_W:SWA_W + KV_W], zp[:, SWA_W:SWA_W + KV_W]
    kb = jnp.concatenate([kp * ckp + _swap32(kp) * skp, kc * ckc + _swap32(kc) * skc], axis=0)
    vb = jnp.concatenate([zp[:, SWA_W + KV_W:], zc[:, SWA_W + KV_W:]], axis=0)
    lane = lax.broadcasted_iota(jnp.int32, kb.shape, 1)
    left = lane < HEAD
    kbr, vbr = pltpu.roll(kb, HEAD, 1), pltpu.roll(vb, HEAD, 1)
    return (jnp.where(left, kb, kbr), jnp.where(left, kbr, kb), jnp.where(left, vb, 0.0), jnp.where(left, 0.0, vbr),
            jnp.where(left, vbr, 0.0), jnp.where(left, 0.0, vb)), [qr[:, q * 128:(q + 1) * 128] for q in range(8)]


def _swa_specs(T, tabs_q, tabs_k):
    cur = lambda c: pl.BlockSpec((BLK, c), lambda n: (n, 0))
    prev = lambda c: pl.BlockSpec((BLK, c), lambda n: (jnp.maximum(n - 1, 0), 0))
    return cur, prev


def _swa_fwd(p_swa, b, sinks, cq, sq, ck, sk):
    T = p_swa.shape[0]
    cur, prev = _swa_specs(T, None, None)

    def body(pc, pp, b_ref, s_ref, cq_r, sq_r, ckc, skc, ckp, skp, o_ref):
        ops, qps = _swa_prep(pc[...], pp[...], b_ref[...], cq_r[...], sq_r[...], ckc[...], skc[...], ckp[...], skp[...])
        o_ref[...] = _swa_core(pl.program_id(0), *ops, s_ref[...], *qps).astype(o_ref.dtype)

    full = lambda a: pl.BlockSpec(a.shape, lambda n: (0, 0))
    return pl.pallas_call(
        body, name="swa_fwd", grid=(T // BLK,),
        in_specs=[cur(SWA_COLS), prev(SWA_COLS), full(b), full(sinks), cur(SWA_W), cur(SWA_W), cur(KV_W), cur(KV_W), prev(KV_W), prev(KV_W)],
        out_specs=cur(SWA_W), out_shape=jax.ShapeDtypeStruct((T, SWA_W), BF16), compiler_params=_cp(("arbitrary",)),
    )(p_swa, p_swa, b, sinks, cq, sq, ck, sk, ck, sk)


def _swa_bwd(p_swa, b, sinks, cq, sq, ck, sk, do):
    T = p_swa.shape[0]
    nb = T // BLK
    cur = lambda c: pl.BlockSpec((BLK, c), lambda s: (nb - 1 - s, 0))
    prev = lambda c: pl.BlockSpec((BLK, c), lambda s: (jnp.maximum(nb - 2 - s, 0), 0))

    def body(pc, pp, b_ref, s_ref, cq_r, sq_r, ckc, skc, ckp, skp, do_ref, dcur, db, dsk, carry):
        step = pl.program_id(0)
        n = nb - 1 - step

        @pl.when(step == 0)
        def _():
            carry[...] = jnp.zeros_like(carry)
            db[...] = jnp.zeros_like(db)
            dsk[...] = jnp.zeros_like(dsk)

        ops, qps = _swa_prep(pc[...], pp[...], b_ref[...], cq_r[...], sq_r[...], ckc[...], skc[...], ckp[...], skp[...])
        _, vjp = jax.vjp(functools.partial(_swa_core, n), *ops, s_ref[...], *qps)
        dk2a, dk2b, dvla, dvra, dvlb, dvrb, dsinks, *dqps = vjp(do_ref[...].astype(F32))
        dqr = jnp.concatenate(dqps, axis=1)
        lane = lax.broadcasted_iota(jnp.int32, dk2a.shape, 1)
        left = lane < HEAD
        dkb = jnp.where(left, dk2a + pltpu.roll(dk2a, HEAD, 1), dk2b + pltpu.roll(dk2b, HEAD, 1))
        dvb = jnp.where(left, dvla + pltpu.roll(dvra, HEAD, 1), pltpu.roll(dvlb, HEAD, 1) + dvrb)
        dq = dqr * cq_r[...] + _swap32(dqr * sq_r[...])
        dkp, dkc = dkb[:BLK], dkb[BLK:]
        dkp = dkp * ckp[...] + _swap32(dkp * skp[...])
        dkc = dkc * ckc[...] + _swap32(dkc * skc[...])
        dc = jnp.concatenate([dq, jnp.concatenate([dkc, dvb[BLK:]], axis=1) + carry[...]], axis=1)
        carry[...] = jnp.concatenate([dkp, dvb[:BLK]], axis=1)
        dcur[...] = dc.astype(dcur.dtype)
        db[...] += jnp.sum(dc, axis=0, keepdims=True)
        dsk[...] += dsinks

    full = lambda a: pl.BlockSpec(a.shape, lambda s: (0, 0))
    return pl.pallas_call(
        body, name="swa_bwd", grid=(nb,),
        in_specs=[cur(SWA_COLS), prev(SWA_COLS), full(b), full(sinks), cur(SWA_W), cur(SWA_W), cur(KV_W), cur(KV_W), prev(KV_W), prev(KV_W),
                  cur(SWA_W)],
        out_specs=[cur(SWA_COLS), full(b), full(sinks)],
        out_shape=[jax.ShapeDtypeStruct((T, SWA_COLS), BF16), jax.ShapeDtypeStruct(b.shape, F32), jax.ShapeDtypeStruct(sinks.shape, F32)],
        scratch_shapes=[pltpu.VMEM((BLK, 2 * KV_W), F32)], compiler_params=_cp(("arbitrary",)),
    )(p_swa, p_swa, b, sinks, cq, sq, ck, sk, ck, sk, do)


def _rope_tables(T):
    inv = 10000.0 ** (-jnp.arange(0, HEAD, 2, dtype=F32) / HEAD)
    ang = jnp.arange(T, dtype=F32)[:, None] * inv[None, :]
    c = jnp.concatenate([jnp.cos(ang), jnp.cos(ang)], axis=1)
    s = jnp.concatenate([-jnp.sin(ang), jnp.sin(ang)], axis=1)
    return jnp.tile(c, (1, 16)), jnp.tile(s, (1, 16)), jnp.tile(c, (1, 2)), jnp.tile(s, (1, 2))


def _xattn_core(*qkv):
    outs = []
    for h in range(XH):
        qh, kh, vh = qkv[h], qkv[XH + h], qkv[2 * XH + h]
        s = lax.dot_general(qh.astype(BF16), kh.astype(BF16), (((1,), (1,)), ((), ())), preferred_element_type=F32) * (XHD ** -0.5)
        p = jnp.exp(s - jnp.max(s, axis=1, keepdims=True))
        p = p / jnp.sum(p, axis=1, keepdims=True)
        outs.append(jnp.dot(p.astype(BF16), vh.astype(BF16), preferred_element_type=F32))
    return jnp.concatenate(outs, axis=1)


def _xattn_split(q, kv):
    return [q[:, h * XHD:(h + 1) * XHD] for h in range(XH)] + [kv[:, h * XHD:(h + 1) * XHD] for h in range(2 * XH)]


def _xattn_fwd(q, kv, tm=256):
    (o,) = _rows(lambda i, q, kv: (_xattn_core(*_xattn_split(q, kv)),), "xattn_fwd", q.shape[0], tm, [q], [kv], [(q.shape[1], BF16)], [])
    return o


def _xattn_bwd(q, kv, do, tm=256):
    def fn(i, q, do, kv):
        _, vjp = jax.vjp(_xattn_core, *_xattn_split(q, kv))
        d = vjp(do.astype(F32))
        return jnp.concatenate(d[:XH], axis=1), jnp.concatenate(d[XH:], axis=1)

    return _rows(fn, "xattn_bwd", q.shape[0], tm, [q, do], [kv], [(q.shape[1], BF16)], [(kv.shape, F32)])


def _loss_head(x, g, tgt, tm=256):
    D = x.shape[1]

    def fn(i, x, tgt, g):
        y, vjp = jax.vjp(_rms, x, g)
        err = y - tgt
        dx, dg = vjp(err * (1.0 / D))
        part = 0.5 / D * jnp.sum(jnp.sum(err * err, axis=1, keepdims=True), axis=0, keepdims=True)
        return dx, jnp.broadcast_to(part, (1, 128)), dg

    return _rows(fn, "loss_head", x.shape[0], tm, [x, tgt], [g], [(D, F32)], [((1, 128), F32), ((1, D), F32)])


def _local_step(x, mem, tgt, get_w, P, put_g):
    T = x.shape[0]
    W = dict(get_w("f1", None))
    x1, s1 = _ffn_fwd(x, P["f1_norm"], W["f1_gate"], W["f1_up"], W["f1_down"], "f1", after=W.get("_after"))

    W.update(get_w("mix", x1))
    h2 = _rms_fwd(x1, P["mix_norm"], "mix_norm")
    w_rkv, w_lora, w_swa = W["w_inT"][:3 * RW_W], W["w_inT"][3 * RW_W:SHIFT_COLS], W["w_inT"][SHIFT_COLS:]
    p_rkv = _mm(h2, w_rkv, "nt", "in_rkv")
    p_lora = _mm(h2, w_lora, "nt", "in_lora")
    p_swa = _mm(h2, w_swa, "nt", "in_swa")
    w_da = jnp.concatenate([W["rw_decay_up"], W["rw_aaa_up"]], axis=0)
    pre_params = (P["rw_mu"][:, :3 * RW_W], P["rw_mu"][:, 3 * RW_W:], P["rw_w0"], P["rw_a0"], P["rw_k_k"], P["rw_k_a"], w_da,
                  W["rw_gate_up"])
    r, decay, k2, v, an, bn, g = _rwkv_pre(p_rkv, p_lora, pre_params)
    scan_vecs = (an, decay, bn, k2, r)
    xes = [_to_perm(a) for a in scan_vecs]
    yi, sai, ck = _scan_fwd(xes, _to_tile(v))
    y_scan = _from_tile(yi)
    y_rw = _rwkv_post(y_scan, r, k2, v, g, P["rw_lnx_w"], P["rw_lnx_b"], P["rw_r_k"])
    cq, sq, ckt, skt = _rope_tables(T)
    y_swa = _swa_fwd(p_swa, P["b_in_attn"], P["attn_sinks"], cq, sq, ckt, skt)
    ycat = jnp.concatenate([y_rw, y_swa], axis=1)
    x2 = _mm(ycat, W["w_out"], "nn", "out_proj", res=x1, bias=P["b_out"])

    W.update(get_w("xattn", x2))
    hx = _rms_fwd(x2, P["xa_norm"], "xa_norm")
    mn = _rms_fwd(mem, P["mem_norm"], "mem_norm")
    q = _mm(hx, W["w_xq"], "nn", "xq", out_dtype=BF16)
    kv = _mm(mn, W["w_xkv"], "nn", "xkv", out_dtype=BF16)
    o = _xattn_fwd(q, kv)
    x3 = _mm(o, W["w_xo"], "nn", "xo", res=x2)

    W.update(get_w("f2", x3))
    x4, s2 = _ffn_fwd(x3, P["f2_norm"], W["f2_gate"], W["f2_up"], W["f2_down"], "f2")
    dx4, loss_part, d_final = _loss_head(x4, P["final_norm"], tgt)

    gs = {"final_norm": d_final}
    dx3, gs["f2_norm"] = _ffn_bwd(x3, P["f2_norm"], W["f2_gate"], W["f2_up"], W["f2_down"], s2, dx4, "f2",
                                  lambda dwg, dwu, dwd: put_g("f2", {"f2_gate": dwg, "f2_up": dwu, "f2_down": dwd}))

    do = _mm(dx3, W["w_xo"], "nt", "xo_do", out_dtype=BF16)
    dw_xo = _mm(o, dx3, "tn", "xo_dw", out_dtype=BF16)
    dq, dkv = _xattn_bwd(q, kv, do)
    dw_xq = _mm(hx, dq, "tn", "xq_dw", out_dtype=BF16)
    dw_xkv = _mm(mn, dkv, "tn", "xkv_dw", out_dtype=BF16)
    sent = put_g("xattn", {"w_xq": dw_xq, "w_xkv": dw_xkv, "w_xo": dw_xo})
    dhx = _mm(dq, W["w_xq"], "nt", "xq_dh", after=sent)
    dmn = _mm(dkv, W["w_xkv"], "nt", "xkv_dmn")
    _, gs["mem_norm"], _ = _rms_bwd(mem, P["mem_norm"], dmn, jnp.zeros_like(mem), "mem_norm_bwd")
    dx2, gs["xa_norm"], gs["b_out"] = _rms_bwd(x2, P["xa_norm"], dhx, dx3, "xa_norm_bwd")

    dycat = _mm(dx2, W["w_out"], "nt", "out_dy")
    dw_out = _mm(ycat, dx2, "tn", "out_dw", out_dtype=BF16)
    dp_swa, gs["b_in_attn"], gs["attn_sinks"] = _swa_bwd(p_swa, P["b_in_attn"], P["attn_sinks"], cq, sq, ckt, skt, dycat[:, RW_W:])
    dy_scan, dr_b, dk2_b, dv_b, dg, gs["rw_lnx_w"], gs["rw_lnx_b"], gs["rw_r_k"] = _rwkv_post_bwd(
        y_scan, r, k2, v, g, P["rw_lnx_w"], P["rw_lnx_b"], P["rw_r_k"], dycat[:, :RW_W])
    dsai, dvi = _scan_bwd_a(xes, _to_tile(dy_scan))
    ies = [_to_perm(v), _to_perm(_from_tile(sai)), _to_perm(dy_scan), _to_perm(_from_tile(dsai))]
    dj = _scan_bwd_b([_to_tile(a) for a in scan_vecs], ies, _ck_a_to_b(ck))
    dan, ddecay, dbn, dk2_s, dr_s = (_from_tile(d) for d in dj)
    cts = (dr_s, ddecay, dk2_s, _from_tile(dvi), dan, dbn, dg, dr_b, dk2_b, dv_b)
    dp_rkv, dp_lora, dmu, dmul, gs["rw_w0"], gs["rw_a0"], gs["rw_k_k"], gs["rw_k_a"], dw_da, gs["rw_gate_up"] = _rwkv_pre_bwd(
        p_rkv, p_lora, pre_params, cts)
    gs["rw_mu"] = jnp.concatenate([dmu, dmul], axis=1)
    gs["rw_decay_up"], gs["rw_aaa_up"] = dw_da[:DECAY_LORA], dw_da[DECAY_LORA:]
    dw_inT = jnp.concatenate([_mm(dp_rkv, h2, "tn", "in_dw_rkv"), _mm(dp_lora, h2, "tn", "in_dw_lora"),
                              _mm(dp_swa, h2, "tn", "in_dw_swa")], axis=0)
    sent = put_g("mix", {"w_inT": dw_inT, "w_out": dw_out})
    dh2 = _mm(dp_rkv, w_rkv, "nn", "in_dh_rkv", after=sent)
    dh2 = _mm(dp_lora, w_lora, "nn", "in_dh_lora", res=dh2)
    dh2 = _mm(dp_swa, w_swa, "nn", "in_dh_swa", res=dh2)
    dx1, gs["mix_norm"], _ = _rms_bwd(x1, P["mix_norm"], dh2, dx2, "mix_norm_bwd")

    dx0, gs["f1_norm"] = _ffn_bwd(x, P["f1_norm"], W["f1_gate"], W["f1_up"], W["f1_down"], s1, dx1, "f1",
                                  lambda dwg, dwu, dwd: put_g("f1", {"f1_gate": dwg, "f1_up": dwu, "f1_down": dwd}))
    return loss_part, dx0, gs


_ANY = pl.BlockSpec(memory_space=pl.ANY)
_OTHER_CHIPS = ((1, 0), (0, 1), (1, 1))


def _mesh_pos():
    return lax.axis_index("x"), lax.axis_index("y"), lax.axis_index("c")


def _slot(ref, kind, s, rows, cols):
    if kind == "row":
        return ref.at[pl.ds(pl.multiple_of(s * rows, 8), rows), :]
    return ref.at[:, pl.ds(pl.multiple_of(s * cols, 128), cols)]


_HBM = pl.BlockSpec(memory_space=pltpu.HBM)
_SEMS = pl.BlockSpec(memory_space=pltpu.SEMAPHORE)
_SPLIT = dict(compiler_params=pltpu.CompilerParams(has_side_effects=pltpu.SideEffectType.DATAFLOW_SIDE_EFFECTING))


def _in_hbm(a):
    return pltpu.with_memory_space_constraint(a, pltpu.HBM)


def _full_shape(s, kind):
    return (4 * s.shape[0], s.shape[1]) if kind == "row" else (s.shape[0], 4 * s.shape[1])


def _gather_start(name, shards, kinds, groups, after=None):
    n, ng = len(shards), len(groups)
    lands = [_in_hbm(lax.empty(_full_shape(s, k), s.dtype)) for s, k in zip(shards, kinds)]
    n_in = 2 * n + (after is not None)

    def body(*refs):
        src, land, sems, token = refs[:n], refs[n:2 * n], refs[n_in:n_in + 3 * ng], refs[-1]
        x, y, c = _mesh_pos()
        me = 2 * x + y
        for gi, idxs in enumerate(groups):
            send, recv, own = sems[3 * gi:3 * gi + 3]
            for k, i in enumerate(idxs):
                mine = _slot(land[i], kinds[i], me, *src[i].shape)
                for r, (dx, dy) in enumerate(_OTHER_CHIPS):
                    pltpu.make_async_remote_copy(src[i], mine, send.at[3 * k + r], recv.at[3 * k + r],
                                                 device_id=((x + dx) % 2, (y + dy) % 2, c), device_id_type=MESH).start()
                pltpu.make_async_copy(src[i], mine, own.at[k]).start()
        token[...] = jnp.zeros_like(token)

    sem_shapes = [pltpu.SemaphoreType.DMA((w * len(g),)) for g in groups for w in (3, 3, 1)]
    thru = [pltpu.HBM(a.shape, a.dtype) for a in (*shards, *lands)]
    res = pl.pallas_call(
        body, name=name, in_specs=[_HBM] * (2 * n) + [_ANY] * (after is not None),
        out_specs=[_SEMS] * (3 * ng) + [_HBM] * (2 * n) + [pl.BlockSpec(memory_space=pltpu.VMEM)],
        out_shape=sem_shapes + thru + [jax.ShapeDtypeStruct((8, 128), F32)],
        input_output_aliases={i: 3 * ng + i for i in range(2 * n)}, **_SPLIT,
    )(*[_in_hbm(s) for s in shards], *lands, *([] if after is None else [after]))
    return res[:3 * ng], res[3 * ng:3 * ng + n], res[3 * ng + n:3 * ng + 2 * n], res[-1]


def _gather_wait(name, sems, shards, lands, kinds, after):
    m = len(shards)

    def body(*refs):
        src, land, (send, recv, own) = refs[:m], refs[m:2 * m], refs[2 * m:2 * m + 3]
        x, y, c = _mesh_pos()
        me = 2 * x + y
        for k in range(m):
            mine = _slot(land[k], kinds[k], me, *src[k].shape)
            for r in range(3):
                cp = pltpu.make_async_remote_copy(src[k], mine, send.at[3 * k + r], recv.at[3 * k + r],
                                                  device_id=(x, y, c), device_id_type=MESH)
                cp.wait_send()
                cp.wait_recv()
            pltpu.make_async_copy(src[k], mine, own.at[k]).wait()

    thru = [pltpu.HBM(a.shape, a.dtype) for a in (*shards, *lands)]
    res = pl.pallas_call(
        body, name=name, in_specs=[_HBM] * (2 * m) + [_SEMS] * 3 + [pl.BlockSpec(memory_space=pl.ANY)],
        out_specs=[_HBM] * (2 * m), out_shape=thru, input_output_aliases={i: i for i in range(2 * m)}, **_SPLIT,
    )(*shards, *lands, *sems, after)
    return res[m:]


def _scatter_start(name, grads, kinds):
    m = len(grads)
    shard_shape = [(g.shape[0] // 4, g.shape[1]) if k == "row" else (g.shape[0], g.shape[1] // 4) for g, k in zip(grads, kinds)]
    lands = [_in_hbm(lax.empty((4, *s), g.dtype)) for s, g in zip(shard_shape, grads)]

    def body(*refs):
        src, land, (send, recv, own) = refs[:m], refs[m:2 * m], refs[2 * m:2 * m + 3]
        x, y, c = _mesh_pos()
        me = 2 * x + y
        for k in range(m):
            for r, (dx, dy) in enumerate(_OTHER_CHIPS):
                tx, ty = (x + dx) % 2, (y + dy) % 2
                pltpu.make_async_remote_copy(_slot(src[k], kinds[k], 2 * tx + ty, *shard_shape[k]), land[k].at[me],
                                             send.at[3 * k + r], recv.at[3 * k + r], device_id=(tx, ty, c), device_id_type=MESH).start()
            pltpu.make_async_copy(_slot(src[k], kinds[k], me, *shard_shape[k]), land[k].at[me], own.at[k]).start()
        refs[-1][...] = jnp.zeros_like(refs[-1])

    thru = [pltpu.HBM(a.shape, a.dtype) for a in (*grads, *lands)]
    res = pl.pallas_call(
        body, name=name, in_specs=[_HBM] * (2 * m),
        out_specs=[_SEMS] * 3 + [_HBM] * (2 * m) + [pl.BlockSpec(memory_space=pltpu.VMEM)],
        out_shape=[pltpu.SemaphoreType.DMA((3 * m,))] * 2 + [pltpu.SemaphoreType.DMA((m,))] + thru + [jax.ShapeDtypeStruct((8, 128), F32)],
        input_output_aliases={i: 3 + i for i in range(2 * m)}, **_SPLIT,
    )(*[_in_hbm(g) for g in grads], *lands)
    return res[:3], res[3:3 + m], res[3 + m:3 + 2 * m], res[-1]


def _scatter_wait(name, sems, grads, lands, kinds, after):
    m = len(grads)

    def body(*refs):
        src, land, (send, recv, own) = refs[:m], refs[m:2 * m], refs[2 * m:2 * m + 3]
        x, y, c = _mesh_pos()
        me = 2 * x + y
        for k in range(m):
            mine = _slot(src[k], kinds[k], me, *land[k].shape[1:])
            for r in range(3):
                cp = pltpu.make_async_remote_copy(mine, land[k].at[me], send.at[3 * k + r], recv.at[3 * k + r],
                                                  device_id=(x, y, c), device_id_type=MESH)
                cp.wait_send()
                cp.wait_recv()
            pltpu.make_async_copy(mine, land[k].at[me], own.at[k]).wait()

    thru = [pltpu.HBM(a.shape, a.dtype) for a in (*grads, *lands)]
    res = pl.pallas_call(
        body, name=name, in_specs=[_HBM] * (2 * m) + [_SEMS] * 3 + [pl.BlockSpec(memory_space=pl.ANY)],
        out_specs=[_HBM] * (2 * m), out_shape=thru, input_output_aliases={i: i for i in range(2 * m)}, **_SPLIT,
    )(*grads, *lands, *sems, after)
    return res[m:]


def _swap_with_sibling(arrs, name):
    n = len(arrs)

    def body(*refs):
        ins, outs = refs[:n], refs[n:2 * n]
        send, recv = refs[2 * n:]
        x, y, c = _mesh_pos()
        copies = []
        for i in range(n):
            rc = pltpu.make_async_remote_copy(ins[i], outs[i], send.at[i], recv.at[i], device_id=(x, y, 1 - c), device_id_type=MESH)
            rc.start()
            copies.append(rc)
        for rc in copies:
            rc.wait()

    return pl.pallas_call(
        body, name=name, in_specs=[_ANY] * n, out_specs=[_ANY] * n,
        out_shape=[jax.ShapeDtypeStruct(a.shape, a.dtype) for a in arrs],
        scratch_shapes=[pltpu.SemaphoreType.DMA((n,)), pltpu.SemaphoreType.DMA((n,))],
    )(*arrs)


def _gather_small(pack, after):
    def body(in_ref, after_ref, out_ref, send, recv, loc):
        x, y, c = _mesh_pos()
        me = 4 * x + 2 * y + c
        cp = pltpu.make_async_copy(in_ref, out_ref.at[me], loc.at[0])
        cp.start()
        copies = [cp]
        for r in range(1, 8):
            dx, dy, dc = r // 4, (r // 2) % 2, r % 2
            rc = pltpu.make_async_remote_copy(in_ref, out_ref.at[me], send.at[r - 1], recv.at[r - 1],
                                              device_id=((x + dx) % 2, (y + dy) % 2, (c + dc) % 2), device_id_type=MESH)
            rc.start()
            copies.append(rc)
        for cp in copies:
            cp.wait()

    return pl.pallas_call(
        body, name="gather_small", in_specs=[_ANY, _ANY], out_specs=_ANY, out_shape=jax.ShapeDtypeStruct((8, *pack.shape), pack.dtype),
        scratch_shapes=[pltpu.SemaphoreType.DMA((7,)), pltpu.SemaphoreType.DMA((7,)), pltpu.SemaphoreType.DMA((1,))],
    )(pack, after)


def _row_tile(R, dtype, target=256):
    mult = 8 * 4 // jnp.dtype(dtype).itemsize
    best = R
    for t in range(mult, min(R, target) + 1, mult):
        if R % t == 0:
            best = t
    return best


def _sum_slots(stack, name, out_dtype=F32):
    k, R, C = stack.shape
    tr = _row_tile(R, stack.dtype)

    def body(s_ref, o_ref):
        acc = s_ref[0].astype(F32)
        for j in range(1, k):
            acc = acc + s_ref[j].astype(F32)
        o_ref[...] = acc.astype(out_dtype)

    return pl.pallas_call(
        body, name=name, grid=(R // tr,), in_specs=[pl.BlockSpec((k, tr, C), lambda i: (0, i, 0))],
        out_specs=pl.BlockSpec((tr, C), lambda i: (i, 0)), out_shape=jax.ShapeDtypeStruct((R, C), out_dtype),
        compiler_params=_cp(("parallel",)),
    )(stack)


def _adamw(w, m, v, ga, gb, name):
    R, C = w.shape
    tr = _row_tile(R, F32, 128)
    gs = [ga] if gb is None else [ga, gb]

    def body(*refs):
        w_ref, m_ref, v_ref = refs[:3]
        g = refs[3][...]
        if gb is not None:
            g = g + refs[4][...]
        g_ref, d_ref, nm_ref, nv_ref = refs[-4:]
        nm = ADAM_B1 * m_ref[...] + (1.0 - ADAM_B1) * g
        nv = ADAM_B2 * v_ref[...] + (1.0 - ADAM_B2) * (g * g)
        m_hat = nm / (1.0 - ADAM_B1 ** ADAM_STEP)
        v_hat = nv / (1.0 - ADAM_B2 ** ADAM_STEP)
        g_ref[...] = g
        d_ref[...] = -ADAM_LR * (m_hat / (jnp.sqrt(v_hat) + ADAM_EPS) + ADAM_WD * w_ref[...])
        nm_ref[...] = nm
        nv_ref[...] = nv

    spec = pl.BlockSpec((tr, C), lambda i: (i, 0))
    return pl.pallas_call(
        body, name=name, grid=(R // tr,), in_specs=[spec] * (3 + len(gs)), out_specs=[spec] * 4,
        out_shape=[jax.ShapeDtypeStruct((R, C), F32)] * 4, compiler_params=_cp(("parallel",)),
    )(w, m, v, *gs)


def _pack(arrs):
    rows = []
    for a in arrs:
        flat = a.reshape(-1)
        rows.append(jnp.pad(flat, (0, -flat.shape[0] % 1024)).reshape(-1, 1024))
    p = jnp.concatenate(rows, axis=0)
    return jnp.pad(p, ((0, -p.shape[0] % 8), (0, 0)))


def _unpack(p, shapes):
    out, r = [], 0
    for s in shapes:
        n = 1
        for d in s:
            n *= d
        nr = -(-n // 1024)
        out.append(p[r:r + nr].reshape(-1)[:n].reshape(s))
        r += nr
    return out


BIG = ("f1_gate", "f1_up", "f1_down", "w_in", "w_out", "w_xq", "w_xkv", "w_xo", "f2_gate", "f2_up", "f2_down")
BIG_KIND = {"f1_gate": "col", "f1_up": "col", "f1_down": "row", "w_in": "row", "w_out": "row", "w_xq": "row", "w_xkv": "col",
            "w_xo": "row", "f2_gate": "col", "f2_up": "col", "f2_down": "row"}
LORA = ("rw_decay_up", "rw_aaa_up", "rw_gate_up")
WEIGHTS = ("f1_norm", "f1_gate", "f1_up", "f1_down", "mix_norm", "w_in", "b_in_attn", "rw_mu", "rw_w0", "rw_decay_up", "rw_a0",
           "rw_aaa_up", "rw_gate_up", "rw_k_k", "rw_k_a", "rw_r_k", "rw_lnx_w", "rw_lnx_b", "attn_sinks", "w_out", "b_out", "xa_norm",
           "mem_norm", "w_xq", "w_xkv", "w_xo", "f2_norm", "f2_gate", "f2_up", "f2_down", "final_norm")
SMALL = tuple(n for n in WEIGHTS if n not in BIG)
GROUP_ORDER = ("f1", "mix", "xattn", "f2")
GROUPS = {"f1": ("f1_gate", "f1_up", "f1_down"), "mix": ("w_in", "w_out") + LORA, "xattn": ("w_xq", "w_xkv", "w_xo"),
          "f2": ("f2_gate", "f2_up", "f2_down")}


def kernel(x, mem, f1_norm, f1_gate, f1_up, f1_down, mix_norm, w_in, b_in_attn, rw_mu, rw_w0, rw_decay_up, rw_a0, rw_aaa_up, rw_gate_up, rw_k_k, rw_k_a, rw_r_k, rw_lnx_w, rw_lnx_b, attn_sinks, w_out, b_out, xa_norm, mem_norm, w_xq, w_xkv, w_xo, f2_norm, f2_gate, f2_up, f2_down, final_norm, loss_target, m_f1_norm, m_f1_gate, m_f1_up, m_f1_down, m_mix_norm, m_w_in, m_b_in_attn, m_rw_mu, m_rw_w0, m_rw_decay_up, m_rw_a0, m_rw_aaa_up, m_rw_gate_up, m_rw_k_k, m_rw_k_a, m_rw_r_k, m_rw_lnx_w, m_rw_lnx_b, m_attn_sinks, m_w_out, m_b_out, m_xa_norm, m_mem_norm, m_w_xq, m_w_xkv, m_w_xo, m_f2_norm, m_f2_gate, m_f2_up, m_f2_down, m_final_norm, v_f1_norm, v_f1_gate, v_f1_up, v_f1_down, v_mix_norm, v_w_in, v_b_in_attn, v_rw_mu, v_rw_w0, v_rw_decay_up, v_rw_a0, v_rw_aaa_up, v_rw_gate_up, v_rw_k_k, v_rw_k_a, v_rw_r_k, v_rw_lnx_w, v_rw_lnx_b, v_attn_sinks, v_w_out, v_b_out, v_xa_norm, v_mem_norm, v_w_xq, v_w_xkv, v_w_xo, v_f2_norm, v_f2_gate, v_f2_up, v_f2_down, v_final_norm):
    a = dict(locals())
    w = {n: a[n] for n in WEIGHTS}
    m = {n: a["m_" + n] for n in WEIGHTS}
    v = {n: a["v_" + n] for n in WEIGHTS}
    sq = lambda t: t.reshape(t.shape[-2:]) if t.ndim == 3 else t.reshape(1, -1)

    local_name = lambda n: "w_inT" if n == "w_in" else n
    kind_of = lambda n: BIG_KIND.get(n, "col")
    payload = lambda n: sq(w[n]).T if n == "w_in" else sq(w[n]) if n in LORA else sq(w[n]).astype(BF16)
    gathers = {}

    def start_gather(name, grps, after):
        shards = [payload(n) for g in grps for n in GROUPS[g]]
        kinds = [kind_of(n) for g in grps for n in GROUPS[g]]
        groups, at = [], 0
        for g in grps:
            groups.append(list(range(at, at + len(GROUPS[g]))))
            at += len(GROUPS[g])
        sems, src_thru, land_thru, token = _gather_start(name, shards, kinds, groups, after)
        for gi, g in enumerate(grps):
            gathers[g] = (sems[3 * gi:3 * gi + 3], [src_thru[i] for i in groups[gi]], [land_thru[i] for i in groups[gi]],
                          [kinds[i] for i in groups[gi]], token)

    start_gather("gather_start_" + GROUP_ORDER[0], GROUP_ORDER[:1], None)

    def get_w(grp, after):
        g_sems, g_src, g_land, g_kinds, token = gathers[grp]
        got = _gather_wait("gather_wait_" + grp, g_sems, g_src, g_land, g_kinds, token if after is None else after)
        out = {local_name(n): f for n, f in zip(GROUPS[grp], got)}
        if grp == GROUP_ORDER[0]:
            start_gather("gather_start_rest", GROUP_ORDER[1:], got[0])
            out["_after"] = gathers[GROUP_ORDER[1]][4]
        return out

    in_flight = {}

    def put_g(grp, gw):
        names = [n for n in GROUPS[grp] if n in BIG]
        *flight, sent = _scatter_start("scatter_start_" + grp, [gw[local_name(n)] for n in names], [kind_of(n) for n in names])
        in_flight[grp] = (names, flight)
        return sent

    P = {n: sq(w[n]) for n in SMALL if n not in LORA}
    P["attn_sinks"] = jnp.pad(P["attn_sinks"], ((0, 0), (0, 128 - P["attn_sinks"].shape[1])))
    P["rw_r_k"] = w["rw_r_k"].reshape(1, RW_W)
    loss_part, grad_x, gs = _local_step(x[0], mem[0], loss_target[0], get_w, P, put_g)
    loss = lax.psum(loss_part[0, 0], ("x", "y", "c"))

    out, after = {}, grad_x
    for grp in reversed(GROUP_ORDER):
        names, (g_sems, g_thru, l_thru) = in_flight[grp]
        stacks = _scatter_wait("scatter_wait_" + grp, g_sems, g_thru, l_thru, [kind_of(n) for n in names], after)
        partial = [_sum_slots(s, "sum_chips_" + n) for s, n in zip(stacks, names)]
        sibling = _swap_with_sibling(partial, "swap_" + grp)
        for n, pa, sb in zip(names, partial, sibling):
            if n == "w_in":
                pa, sb = pa.T, sb.T
            out[n] = _adamw(sq(w[n]), sq(m[n]), sq(v[n]), pa, sb, "adamw_" + n)
        after = out[names[-1]][1]

    gs["attn_sinks"] = gs["attn_sinks"][:, :16]
    gsum = _sum_slots(_gather_small(_pack([gs[n] for n in SMALL]), after), "sum_small")
    g_small = dict(zip(SMALL, _unpack(gsum, [gs[n].shape for n in SMALL])))
    shard = 2 * lax.axis_index("x") + lax.axis_index("y")
    for n in LORA:
        cols = w[n].shape[-1]
        g_small[n] = lax.dynamic_slice_in_dim(g_small[n], shard * cols, cols, axis=1)

    flat = lambda d: _pack([d[n] for n in SMALL])
    res = _adamw(flat(w), flat(m), flat(v), _pack([g_small[n] for n in SMALL]), None, "adamw_small")
    shapes = [w[n].shape for n in SMALL]
    for k, p in enumerate(res):
        for n, t in zip(SMALL, _unpack(p, shapes)):
            out.setdefault(n, [None] * 4)[k] = t
    outs = [loss, grad_x.reshape(x.shape)]
    for k in range(4):
        outs += [out[n][k].reshape(w[n].shape) for n in WEIGHTS]
    return tuple(outs)
```

```python
import functools

import jax
import jax.numpy as jnp
from jax import lax
from jax.experimental import pallas as pl
from jax.experimental.pallas import tpu as pltpu

F32, BF16 = jnp.float32, jnp.bfloat16
MESH = pl.DeviceIdType.MESH

HEAD = 64
RW_HEADS = 16
RW_W = 1024
SWA_W = 1024
KV_W = 128
DECAY_LORA, AAA_LORA, GATE_LORA = 64, 64, 160
LORA_W = DECAY_LORA + AAA_LORA + GATE_LORA
SHIFT_COLS = 3 * RW_W + LORA_W
XH = 4
XHD = 512
MEM_LEN = 256
WINDOW = 128
GN_EPS = 64e-5
RMS_EPS = 1e-6
NEG_INF = -1e30
ADAM_LR, ADAM_B1, ADAM_B2, ADAM_EPS, ADAM_WD, ADAM_STEP = 0.001, 0.9, 0.999, 1e-08, 0.01, 10

VMEM_LIMIT = 56 * 1024 * 1024


def _cp(sem=None, **kw):
    return pltpu.CompilerParams(dimension_semantics=sem, vmem_limit_bytes=VMEM_LIMIT, **kw)


def _pick(dim, target):
    if dim <= target:
        return dim
    best = None
    for t in range(128, target + 1, 128):
        if dim % t == 0:
            best = t
    assert best is not None, (dim, target)
    return best


_DIMS = {"nn": (((1,), (0,)), ((), ())), "nt": (((1,), (1,)), ((), ())), "tn": (((0,), (0,)), ((), ()))}


def _mm(a, b, mode, name, out_dtype=F32, alpha=1.0, res=None, bias=None, tm=1024, tn=1024, tk=2048, after=None):
    if mode == "nn":
        (M, K), (K2, N) = a.shape, b.shape
    elif mode == "nt":
        (M, K), (N, K2) = a.shape, b.shape
    else:
        (K, M), (K2, N) = a.shape, b.shape
    assert K == K2, (name, a.shape, b.shape)
    tm, tn, tk = _pick(M, tm), _pick(N, tn), _pick(K, tk)
    nk = K // tk
    a_spec = pl.BlockSpec((tk, tm), lambda i, j, k: (k, i)) if mode == "tn" else pl.BlockSpec((tm, tk), lambda i, j, k: (i, k))
    b_spec = pl.BlockSpec((tn, tk), lambda i, j, k: (j, k)) if mode == "nt" else pl.BlockSpec((tk, tn), lambda i, j, k: (k, j))
    o_spec = pl.BlockSpec((tm, tn), lambda i, j, k: (i, j))
    ins, specs = [a, b], [a_spec, b_spec]
    if res is not None:
        ins.append(res)
        specs.append(o_spec)
    if bias is not None:
        ins.append(bias)
        specs.append(pl.BlockSpec((1, tn), lambda i, j, k: (0, j)))
    if after is not None:
        ins.append(after)
        specs.append(pl.BlockSpec(memory_space=pl.ANY))
    dims = _DIMS[mode]

    def body(*refs):
        a_ref, b_ref = refs[0], refs[1]
        part = lax.dot_general(a_ref[...].astype(BF16), b_ref[...].astype(BF16), dims, preferred_element_type=F32)

        def finish(o, o_ref):
            if alpha != 1.0:
                o = o * alpha
            p = 2
            if res is not None:
                o = o + refs[p][...].astype(F32)
                p += 1
            if bias is not None:
                o = o + refs[p][...]
            o_ref[...] = o.astype(out_dtype)

        if nk == 1:
            finish(part, refs[-1])
            return
        o_ref, acc_ref = refs[-2], refs[-1]
        k = pl.program_id(2)

        @pl.when(k == 0)
        def _():
            acc_ref[...] = part

        @pl.when(k > 0)
        def _():
            acc_ref[...] += part

        @pl.when(k == nk - 1)
        def _():
            finish(acc_ref[...], o_ref)

    return pl.pallas_call(
        body, name=name, grid=(M // tm, N // tn, nk), in_specs=specs, out_specs=o_spec,
        out_shape=jax.ShapeDtypeStruct((M, N), out_dtype), scratch_shapes=[pltpu.VMEM((tm, tn), F32)] * (nk > 1),
        compiler_params=_cp(("parallel", "parallel", "arbitrary")),
    )(*ins)


def _rows(fn, name, T, tm, tiled, full, out_tiled, out_acc, extra=(), reverse=False, scratch=()):
    n = T // tm
    idx = (lambda i: n - 1 - i) if reverse else (lambda i: i)
    in_specs = [pl.BlockSpec((tm, a.shape[1]), lambda i: (idx(i), 0)) for a in tiled]
    in_specs += [mk(idx) for _, mk in extra]
    in_specs += [pl.BlockSpec(a.shape, lambda i, nd=a.ndim: (0,) * nd) for a in full]
    out_specs = [pl.BlockSpec((tm, c), lambda i: (idx(i), 0)) for c, _ in out_tiled]
    out_specs += [pl.BlockSpec(s, lambda i, nd=len(s): (0,) * nd) for s, _ in out_acc]
    out_shape = [jax.ShapeDtypeStruct((T, c), d) for c, d in out_tiled] + [jax.ShapeDtypeStruct(s, d) for s, d in out_acc]
    n_in = len(tiled) + len(extra) + len(full)
    n_t, n_a = len(out_tiled), len(out_acc)

    def body(*refs):
        step = pl.program_id(0)
        vals = [r[...] for r in refs[:n_in]]
        outs = fn(idx(step), *vals, *refs[n_in + n_t + n_a:])
        for r, v in zip(refs[n_in:n_in + n_t], outs[:n_t]):
            r[...] = v.astype(r.dtype)
        for r, v in zip(refs[n_in + n_t:n_in + n_t + n_a], outs[n_t:]):
            @pl.when(step == 0)
            def _(r=r):
                r[...] = jnp.zeros_like(r)

            r[...] += v

    return pl.pallas_call(
        body, name=name, grid=(n,), in_specs=in_specs, out_specs=out_specs, out_shape=out_shape,
        scratch_shapes=list(scratch), compiler_params=_cp(("arbitrary",)),
    )(*tiled, *[a for a, _ in extra], *full)


def _rms(x, g):
    return x * lax.rsqrt(jnp.mean(x * x, axis=-1, keepdims=True) + RMS_EPS) * g


def _rms_fwd(x, g, name, tm=256):
    (h,) = _rows(lambda i, x, g: (_rms(x, g),), name, x.shape[0], min(tm, x.shape[0]), [x], [g], [(x.shape[1], BF16)], [])
    return h


def _rms_bwd(x, g, dh, dres, name, tm=256):
    D = x.shape[1]

    def fn(i, x, dh, dres, g):
        _, vjp = jax.vjp(_rms, x, g)
        dx, dg = vjp(dh.astype(F32))
        dx = dx + dres
        return dx, dg, jnp.sum(dx, axis=0, keepdims=True)

    return _rows(fn, name, x.shape[0], tm, [x, dh, dres], [g], [(D, F32)], [((1, D), F32), ((1, D), F32)])


def _ffn_up(h, wg, wu, name, tm=1024, tn=512, after=None):
    (M, K), N = h.shape, wg.shape[1]
    tm, tn = _pick(M, tm), _pick(N, tn)

    def body(*refs):
        h_ref, wg_ref, wu_ref = refs[:3]
        g_ref, u_ref, a_ref = refs[-3:]
        hb = h_ref[...].astype(BF16)
        g = jnp.dot(hb, wg_ref[...].astype(BF16), preferred_element_type=F32)
        u = jnp.dot(hb, wu_ref[...].astype(BF16), preferred_element_type=F32)
        g_ref[...] = g
        u_ref[...] = u
        a_ref[...] = (g * jax.nn.sigmoid(g) * u).astype(BF16)

    o_spec = pl.BlockSpec((tm, tn), lambda i, j: (i, j))
    w_spec = pl.BlockSpec((K, tn), lambda i, j: (0, j))
    extra = [] if after is None else [after]
    return pl.pallas_call(
        body, name=name, grid=(M // tm, N // tn),
        in_specs=[pl.BlockSpec((tm, K), lambda i, j: (i, 0)), w_spec, w_spec] + [pl.BlockSpec(memory_space=pl.ANY)] * len(extra),
        out_specs=[o_spec] * 3, out_shape=[jax.ShapeDtypeStruct((M, N), F32)] * 2 + [jax.ShapeDtypeStruct((M, N), BF16)],
        compiler_params=_cp(("parallel", "parallel")),
    )(h, wg, wu, *extra)


def _act_bwd(da, g, u, name, tm=256):
    def fn(i, da, g, u):
        s = jax.nn.sigmoid(g)
        return da * u * (s * (1.0 + g * (1.0 - s))), da * (g * s)

    F = g.shape[1]
    return _rows(fn, name, g.shape[0], tm, [da, g, u], [], [(F, BF16), (F, BF16)], [])


def _ffn_fwd(x, gain, wg, wu, wd, tag, after=None):
    h = _rms_fwd(x, gain, tag + "_norm")
    G, U, A = _ffn_up(h, wg, wu, tag + "_up", after=after)
    xo = _mm(A, wd, "nn", tag + "_down", alpha=0.5, res=x)
    return xo, (h, G, U, A)


def _ffn_bwd(x, gain, wg, wu, wd, saved, dxo, tag, send):
    h, G, U, A = saved
    dA = _mm(dxo, wd, "nt", tag + "_dA", alpha=0.5)
    dwd = _mm(A, dxo, "tn", tag + "_dwd", out_dtype=BF16, alpha=0.5)
    dG, dU = _act_bwd(dA, G, U, tag + "_act_bwd")
    dwg = _mm(h, dG, "tn", tag + "_dwg", out_dtype=BF16)
    dwu = _mm(h, dU, "tn", tag + "_dwu", out_dtype=BF16)
    sent = send(dwg, dwu, dwd)
    dh = _mm(dG, wg, "nt", tag + "_dh_g", after=sent)
    dh = _mm(dU, wu, "nt", tag + "_dh_u", res=dh)
    dx, dgain, _ = _rms_bwd(x, gain, dh, dxo, tag + "_norm_bwd")
    return dx, dgain


def _segsum64_impl(x):
    r = lax.broadcasted_iota(jnp.int32, (128, 128), 0) // HEAD
    c = lax.broadcasted_iota(jnp.int32, (128, 128), 1) // HEAD
    ones = (r == c).astype(BF16)
    hi = x.astype(BF16)
    lo = (x - hi.astype(F32)).astype(BF16)
    outs = []
    for q in range(x.shape[1] // 128):
        sl = slice(q * 128, (q + 1) * 128)
        outs.append(jnp.dot(hi[:, sl], ones, preferred_element_type=F32) + jnp.dot(lo[:, sl], ones, preferred_element_type=F32))
    return outs[0] if len(outs) == 1 else jnp.concatenate(outs, axis=1)


@jax.custom_vjp
def _segsum64(x):
    return _segsum64_impl(x)


_segsum64.defvjp(lambda x: (_segsum64_impl(x), None), lambda _, ct: (_segsum64_impl(ct),))


def _swap32(x):
    lane = lax.broadcasted_iota(jnp.int32, (x.shape[0], 128), 1)
    outs = [jnp.take_along_axis(x[:, q * 128:(q + 1) * 128], lane ^ 32, axis=1) for q in range(x.shape[1] // 128)]
    return outs[0] if len(outs) == 1 else jnp.concatenate(outs, axis=1)


def _tree_sum(xs):
    xs = list(xs)
    while len(xs) > 1:
        nxt = [xs[i] + xs[i + 1] for i in range(0, len(xs) - 1, 2)]
        if len(xs) % 2:
            nxt.append(xs[-1])
        xs = nxt
    return xs[0]


def _softplus(x):
    return jnp.maximum(x, 0.0) + jnp.log(1.0 + jnp.exp(-jnp.abs(x)))


def _pre_core(k, da, gd, w0, a0, k_k, k_a, w_da, gate_up):
    lane = lax.broadcasted_iota(jnp.int32, da.shape, 1)
    w_da = w_da.astype(BF16)
    l1 = jnp.dot(jnp.where(lane < DECAY_LORA, jnp.tanh(da), 0.0).astype(BF16), w_da, preferred_element_type=F32)
    l2 = jnp.dot(jnp.where(lane >= DECAY_LORA, da, 0.0).astype(BF16), w_da, preferred_element_type=F32)
    wlog = -_softplus(-(w0 + l1)) - 0.5
    decay = jnp.exp(-jnp.exp(wlog))
    a = jax.nn.sigmoid(a0 + l2)
    g = jnp.dot(jax.nn.sigmoid(gd).astype(BF16), gate_up.astype(BF16), preferred_element_type=F32)
    kk = k * k_k
    kkn = kk / jnp.maximum(jnp.sqrt(_segsum64(kk * kk)), 1e-12)
    k2 = k * (1.0 + (a - 1.0) * k_a)
    return decay, k2, -kkn, kkn * a, g


def _pre_shift(i, zr, zl, zr8, zl8, mu, mul):
    live = (i > 0).astype(F32)
    dz = _shift_down(zr, zr8[7:8, :] * live) - zr
    dzl = _shift_down(zl, zl8[7:8, :] * live) - zl
    return zr + dz * mu, zl + dzl * mul, dz, dzl


def _shift_down(x, first_row):
    rolled = pltpu.roll(x, 1, 0)
    row = lax.broadcasted_iota(jnp.int32, x.shape, 0)
    return jnp.where(row == 0, first_row, rolled)


def _shift_up(x, last_row):
    rolled = pltpu.roll(x, x.shape[0] - 1, 0)
    row = lax.broadcasted_iota(jnp.int32, x.shape, 0)
    return jnp.where(row == x.shape[0] - 1, last_row, rolled)


def _prev_rows_spec(tm, cols):
    return lambda idx: pl.BlockSpec((8, cols), lambda i: (jnp.maximum(idx(i) * (tm // 8) - 1, 0), 0))


def _rwkv_pre(p_rkv, p_lora, params, tm=256):
    T = p_rkv.shape[0]

    def fn(i, zr, zl, zr8, zl8, mu, mul, *ps):
        z, z2, _, _ = _pre_shift(i, zr, zl, zr8, zl8, mu, mul)
        decay, k2, an, bn, g = _pre_core(z[:, RW_W:2 * RW_W], z2[:, :128], z2[:, 128:], *ps)
        return z[:, :RW_W], decay, k2, z[:, 2 * RW_W:], an, bn, g

    extra = [(p_rkv, _prev_rows_spec(tm, 3 * RW_W)), (p_lora, _prev_rows_spec(tm, LORA_W))]
    return _rows(fn, "rwkv_pre", T, tm, [p_rkv, p_lora], list(params), [(RW_W, F32)] * 7, [], extra=extra)


def _rwkv_pre_bwd(p_rkv, p_lora, params, cts, tm=256):
    T = p_rkv.shape[0]
    n = T // tm

    def fn(i, zr, zl, cr, cdec, ck2, cv, can, cbn, cg, cr_b, ck2_b, cv_b, zr8, zl8, mu, mul, *rest):
        ps, (car, carl) = rest[:-2], rest[-2:]
        cr, ck2, cv = cr + cr_b, ck2 + ck2_b, cv + cv_b
        z, z2, dif, difl = _pre_shift(i, zr, zl, zr8, zl8, mu, mul)
        _, vjp = jax.vjp(_pre_core, z[:, RW_W:2 * RW_W], z2[:, :128], z2[:, 128:], *ps)
        dk, dda, dgd, *dps = vjp((cdec, ck2, can, cbn, cg))
        dz = jnp.concatenate([cr, dk, cv], axis=1)
        dz2 = jnp.concatenate([dda, dgd], axis=1)
        dzp, dzlp = dz * mu, dz2 * mul

        @pl.when(i == n - 1)
        def _():
            car[...] = jnp.zeros_like(car)
            carl[...] = jnp.zeros_like(carl)

        d_rkv = dz - dzp + _shift_up(dzp, car[0:1, :])
        d_lora = dz2 - dzlp + _shift_up(dzlp, carl[0:1, :])
        car[0:1, :] = dzp[0:1, :]
        carl[0:1, :] = dzlp[0:1, :]
        return (d_rkv, d_lora, jnp.sum(dz * dif, axis=0, keepdims=True), jnp.sum(dz2 * difl, axis=0, keepdims=True), *dps)

    extra = [(p_rkv, _prev_rows_spec(tm, 3 * RW_W)), (p_lora, _prev_rows_spec(tm, LORA_W))]
    acc = [(p.shape, F32) for p in params]
    return _rows(fn, "rwkv_pre_bwd", T, tm, [p_rkv, p_lora, *cts], list(params), [(3 * RW_W, BF16), (LORA_W, BF16)], acc,
                 extra=extra, reverse=True, scratch=[pltpu.VMEM((8, 3 * RW_W), F32), pltpu.VMEM((8, LORA_W), F32)])


def _post_core(y, r, k2, v, g, lw, lb, rk):
    mu = _segsum64(y) * (1.0 / HEAD)
    yc = y - mu
    var = _segsum64(yc * yc) * (1.0 / HEAD)
    yn = yc * lax.rsqrt(var + GN_EPS) * lw + lb
    return (yn + _segsum64(r * k2 * rk) * v) * g


def _rwkv_post(y, r, k2, v, g, lw, lb, rk, tm=256):
    (o,) = _rows(lambda i, *a: (_post_core(*a),), "rwkv_post", y.shape[0], tm, [y, r, k2, v, g], [lw, lb, rk], [(RW_W, BF16)], [])
    return o


def _rwkv_post_bwd(y, r, k2, v, g, lw, lb, rk, do, tm=256):
    def fn(i, y, r, k2, v, g, do, lw, lb, rk):
        _, vjp = jax.vjp(_post_core, y, r, k2, v, g, lw, lb, rk)
        return vjp(do.astype(F32))

    return _rows(fn, "rwkv_post_bwd", y.shape[0], tm, [y, r, k2, v, g, do], [lw, lb, rk], [(RW_W, F32)] * 5, [((1, RW_W), F32)] * 3)


SCAN_L = 32


def _to_tile(x):
    T = x.shape[0]
    return x.reshape(T, RW_HEADS, 8, 8).transpose(0, 2, 1, 3).reshape(T, 8, 128)


def _from_tile(x):
    T = x.shape[0]
    return x.reshape(T, 8, RW_HEADS, 8).transpose(0, 2, 1, 3).reshape(T, RW_W)


def _to_perm(x):
    T = x.shape[0]
    return x.reshape(T, RW_HEADS, HEAD).transpose(0, 2, 1).reshape(T, 8, 128)


def _expander(srcs):
    s = lax.broadcasted_iota(jnp.int32, (8, 128), 0)
    lane = lax.broadcasted_iota(jnp.int32, (8, 128), 1)
    idx = 16 * s + lane // 8

    def expand(t, e_ref):
        for m, r in enumerate(srcs):
            for g in range(8):
                row = jnp.broadcast_to(r[t, pl.ds(g, 1), :], (8, 128))
                e_ref[m, g * 8:(g + 1) * 8, :] = jnp.take_along_axis(row, idx, axis=1)

    return expand


def _ck_a_to_b(ck):
    n = ck.shape[0]
    return ck.reshape(n, 8, 8, 8, RW_HEADS, 8).transpose(0, 3, 5, 1, 4, 2).reshape(n, HEAD, 8, 128)


def _scan_fwd(xes, vi):
    T, L = vi.shape[0], SCAN_L
    nch = T // L

    def body(*refs):
        xr, (vi_ref, yi_ref, sa_ref, ck_ref, st_ref, e0, e1) = refs[:5], refs[5:]

        @pl.when(pl.program_id(0) == 0)
        def _():
            st_ref[...] = jnp.zeros_like(st_ref)

        ck_ref[0] = st_ref[...]
        expand = _expander(xr)
        expand(0, e0)

        def step(t, e_ref):
            v = vi_ref[t]
            row = lambda m, j: jnp.broadcast_to(e_ref[m, pl.ds(j, 1), :], (8, 128))
            S = [st_ref[j] for j in range(HEAD)]
            sa = _tree_sum([S[j] * row(0, j) for j in range(HEAD)])
            sa_ref[t] = sa
            S = [S[j] * row(1, j) + row(2, j) * sa + row(3, j) * v for j in range(HEAD)]
            for j in range(HEAD):
                st_ref[j] = S[j]
            yi_ref[t] = _tree_sum([S[j] * row(4, j) for j in range(HEAD)])

        def pair(p, carry):
            t = 2 * p
            expand(t + 1, e1)
            step(t, e0)
            expand(jnp.minimum(t + 2, L - 1), e0)
            step(t + 1, e1)
            return carry

        lax.fori_loop(0, L // 2, pair, 0)

    tile = pl.BlockSpec((L, 8, 128), lambda c: (c, 0, 0))
    return pl.pallas_call(
        body, name="rwkv_scan_fwd", grid=(nch,), in_specs=[tile] * 6,
        out_specs=[tile, tile, pl.BlockSpec((1, HEAD, 8, 128), lambda c: (c, 0, 0, 0))],
        out_shape=[jax.ShapeDtypeStruct((T, 8, 128), F32)] * 2 + [jax.ShapeDtypeStruct((nch, HEAD, 8, 128), F32)],
        scratch_shapes=[pltpu.VMEM((HEAD, 8, 128), F32)] + [pltpu.VMEM((5, HEAD, 128), F32)] * 2, compiler_params=_cp(("arbitrary",)),
    )(*xes, vi)


def _scan_bwd_a(xes, dyi):
    T, L = dyi.shape[0], SCAN_L
    nch = T // L

    def body(*refs):
        xr, (dy_ref, dsa_ref, dv_ref, g_ref, e0, e1) = refs[:5], refs[5:]

        @pl.when(pl.program_id(0) == 0)
        def _():
            g_ref[...] = jnp.zeros_like(g_ref)

        expand = _expander(xr)
        expand(L - 1, e0)

        def step(t, e_ref):
            dy = dy_ref[t]
            row = lambda m, j: jnp.broadcast_to(e_ref[m, pl.ds(j, 1), :], (8, 128))
            G = [g_ref[j] + row(4, j) * dy for j in range(HEAD)]
            dsa = _tree_sum([G[j] * row(2, j) for j in range(HEAD)])
            dsa_ref[t] = dsa
            dv_ref[t] = _tree_sum([G[j] * row(3, j) for j in range(HEAD)])
            for j in range(HEAD):
                g_ref[j] = G[j] * row(1, j) + row(0, j) * dsa

        def pair(p, carry):
            t = L - 1 - 2 * p
            expand(t - 1, e1)
            step(t, e0)
            expand(jnp.maximum(t - 2, 0), e0)
            step(t - 1, e1)
            return carry

        lax.fori_loop(0, L // 2, pair, 0)

    tile = pl.BlockSpec((L, 8, 128), lambda c: (nch - 1 - c, 0, 0))
    return pl.pallas_call(
        body, name="rwkv_scan_bwd_a", grid=(nch,), in_specs=[tile] * 6, out_specs=[tile, tile],
        out_shape=[jax.ShapeDtypeStruct((T, 8, 128), F32)] * 2,
        scratch_shapes=[pltpu.VMEM((HEAD, 8, 128), F32)] + [pltpu.VMEM((5, HEAD, 128), F32)] * 2, compiler_params=_cp(("arbitrary",)),
    )(*xes, dyi)


def _scan_bwd_b(xts, ies, ckb):
    T, L = xts[0].shape[0], SCAN_L
    nch = T // L

    def body(*refs):
        xr, er, ck_ref, dj, (hist, g_ref, e0, e1) = refs[:5], refs[5:9], refs[9], refs[10:15], refs[15:]

        @pl.when(pl.program_id(0) == 0)
        def _():
            g_ref[...] = jnp.zeros_like(g_ref)

        hist[0] = ck_ref[0]
        expand_vs = _expander(er[:2])
        expand = _expander(er)
        expand_vs(0, e0)

        def fstep(t, e_ref):
            w, B, k = xr[1][t], xr[2][t], xr[3][t]
            row = lambda m, i: jnp.broadcast_to(e_ref[m, pl.ds(i, 1), :], (8, 128))
            for i in range(HEAD):
                hist[t + 1, i] = hist[t, i] * w + row(1, i) * B + row(0, i) * k

        def fpair(p, carry):
            t = 2 * p
            expand_vs(t + 1, e1)
            fstep(t, e0)
            expand_vs(jnp.minimum(t + 2, L - 1), e0)
            fstep(t + 1, e1)
            return carry

        lax.fori_loop(0, L // 2, fpair, 0)
        expand(L - 1, e0)

        def bstep(t, e_ref):
            A, w, r = xr[0][t], xr[1][t], xr[4][t]
            row = lambda m, i: jnp.broadcast_to(e_ref[m, pl.ds(i, 1), :], (8, 128))
            G = [g_ref[i] + row(2, i) * r for i in range(HEAD)]
            Sp = [hist[t, i] for i in range(HEAD)]
            dj[4][t] = _tree_sum([hist[t + 1, i] * row(2, i) for i in range(HEAD)])
            dj[1][t] = _tree_sum([G[i] * Sp[i] for i in range(HEAD)])
            dj[2][t] = _tree_sum([G[i] * row(1, i) for i in range(HEAD)])
            dj[3][t] = _tree_sum([G[i] * row(0, i) for i in range(HEAD)])
            dj[0][t] = _tree_sum([Sp[i] * row(3, i) for i in range(HEAD)])
            for i in range(HEAD):
                g_ref[i] = G[i] * w + row(3, i) * A

        def bpair(p, carry):
            t = L - 1 - 2 * p
            expand(t - 1, e1)
            bstep(t, e0)
            expand(jnp.maximum(t - 2, 0), e0)
            bstep(t - 1, e1)
            return carry

        lax.fori_loop(0, L // 2, bpair, 0)

    tile = pl.BlockSpec((L, 8, 128), lambda c: (nch - 1 - c, 0, 0))
    return pl.pallas_call(
        body, name="rwkv_scan_bwd_b", grid=(nch,),
        in_specs=[tile] * 9 + [pl.BlockSpec((1, HEAD, 8, 128), lambda c: (nch - 1 - c, 0, 0, 0))],
        out_specs=[tile] * 5, out_shape=[jax.ShapeDtypeStruct((T, 8, 128), F32)] * 5,
        scratch_shapes=[pltpu.VMEM((L + 1, HEAD, 8, 128), F32), pltpu.VMEM((HEAD, 8, 128), F32)] + [pltpu.VMEM((4, HEAD, 128), F32)] * 2,
        compiler_params=_cp(("arbitrary",)),
    )(*xts, *ies, ckb)


SWA_COLS = SWA_W + 2 * KV_W
BLK = 128


def _swa_core(n, k2a, k2b, vla, vra, vlb, vrb, sinks, *qps):
    iq = lax.broadcasted_iota(jnp.int32, (BLK, 2 * BLK), 0)
    ik = lax.broadcasted_iota(jnp.int32, (BLK, 2 * BLK), 1)
    diff = BLK + iq - ik
    valid = (diff >= 0) & (diff < WINDOW) & ((n > 0) | (ik >= BLK))
    lane = lax.broadcasted_iota(jnp.int32, (BLK, 128), 1)
    lane1 = lax.broadcasted_iota(jnp.int32, (1, 128), 1)
    nt = (((1,), (1,)), ((), ()))
    outs = []
    for pp in range(8):
        k2, vl, vr = (k2a, vla, vra) if pp < 4 else (k2b, vlb, vrb)
        qp = qps[pp]
        o = None
        for half, vv in ((0, vl), (1, vr)):
            qh = jnp.where((lane >= HEAD) == (half == 1), qp, 0.0).astype(BF16)
            s = lax.dot_general(qh, k2.astype(BF16), nt, preferred_element_type=F32) * (HEAD ** -0.5)
            s = jnp.where(valid, s, NEG_INF)
            sink = jnp.sum(jnp.where(lane1 == 2 * pp + half, sinks, 0.0), axis=1, keepdims=True)
            m = jnp.maximum(jnp.max(s, axis=1, keepdims=True), sink)
            p = jnp.exp(s - m)
            den = jnp.sum(p, axis=1, keepdims=True) + jnp.exp(sink - m)
            oh = jnp.dot((p / den).astype(BF16), vv.astype(BF16), preferred_element_type=F32)
            o = oh if o is None else o + oh
        outs.append(o)
    return jnp.concatenate(outs, axis=1)


def _swa_prep(pc, pp, b, cq, sq, ckc, skc, ckp, skp):
    zc, zp = pc + b, pp + b
    qr = zc[:, :SWA_W] * cq + _swap32(zc[:, :SWA_W]) * sq
    kc, kp = zc[:, SWA_W:SWA_W + KV_W], zp[:, SWA_W:SWA_W + KV_W]
    kb = jnp.concatenate([kp * ckp + _swap32(kp) * skp, kc * ckc + _swap32(kc) * skc], axis=0)
    vb = jnp.concatenate([zp[:, SWA_W + KV_W:], zc[:, SWA_W + KV_W:]], axis=0)
    lane = lax.broadcasted_iota(jnp.int32, kb.shape, 1)
    left = lane < HEAD
    kbr, vbr = pltpu.roll(kb, HEAD, 1), pltpu.roll(vb, HEAD, 1)
    return (jnp.where(left, kb, kbr), jnp.where(left, kbr, kb), jnp.where(left, vb, 0.0), jnp.where(left, 0.0, vbr),
            jnp.where(left, vbr, 0.0), jnp.where(left, 0.0, vb)), [qr[:, q * 128:(q + 1) * 128] for q in range(8)]


def _swa_specs(T, tabs_q, tabs_k):
    cur = lambda c: pl.BlockSpec((BLK, c), lambda n: (n, 0))
    prev = lambda c: pl.BlockSpec((BLK, c), lambda n: (jnp.maximum(n - 1, 0), 0))
    return cur, prev


def _swa_fwd(p_swa, b, sinks, cq, sq, ck, sk):
    T = p_swa.shape[0]
    cur, prev = _swa_specs(T, None, None)

    def body(pc, pp, b_ref, s_ref, cq_r, sq_r, ckc, skc, ckp, skp, o_ref):
        ops, qps = _swa_prep(pc[...], pp[...], b_ref[...], cq_r[...], sq_r[...], ckc[...], skc[...], ckp[...], skp[...])
        o_ref[...] = _swa_core(pl.program_id(0), *ops, s_ref[...], *qps).astype(o_ref.dtype)

    full = lambda a: pl.BlockSpec(a.shape, lambda n: (0, 0))
    return pl.pallas_call(
        body, name="swa_fwd", grid=(T // BLK,),
        in_specs=[cur(SWA_COLS), prev(SWA_COLS), full(b), full(sinks), cur(SWA_W), cur(SWA_W), cur(KV_W), cur(KV_W), prev(KV_W), prev(KV_W)],
        out_specs=cur(SWA_W), out_shape=jax.ShapeDtypeStruct((T, SWA_W), BF16), compiler_params=_cp(("arbitrary",)),
    )(p_swa, p_swa, b, sinks, cq, sq, ck, sk, ck, sk)


def _swa_bwd(p_swa, b, sinks, cq, sq, ck, sk, do):
    T = p_swa.shape[0]
    nb = T // BLK
    cur = lambda c: pl.BlockSpec((BLK, c), lambda s: (nb - 1 - s, 0))
    prev = lambda c: pl.BlockSpec((BLK, c), lambda s: (jnp.maximum(nb - 2 - s, 0), 0))

    def body(pc, pp, b_ref, s_ref, cq_r, sq_r, ckc, skc, ckp, skp, do_ref, dcur, db, dsk, carry):
        step = pl.program_id(0)
        n = nb - 1 - step

        @pl.when(step == 0)
        def _():
            carry[...] = jnp.zeros_like(carry)
            db[...] = jnp.zeros_like(db)
            dsk[...] = jnp.zeros_like(dsk)

        ops, qps = _swa_prep(pc[...], pp[...], b_ref[...], cq_r[...], sq_r[...], ckc[...], skc[...], ckp[...], skp[...])
        _, vjp = jax.vjp(functools.partial(_swa_core, n), *ops, s_ref[...], *qps)
        dk2a, dk2b, dvla, dvra, dvlb, dvrb, dsinks, *dqps = vjp(do_ref[...].astype(F32))
        dqr = jnp.concatenate(dqps, axis=1)
        lane = lax.broadcasted_iota(jnp.int32, dk2a.shape, 1)
        left = lane < HEAD
        dkb = jnp.where(left, dk2a + pltpu.roll(dk2a, HEAD, 1), dk2b + pltpu.roll(dk2b, HEAD, 1))
        dvb = jnp.where(left, dvla + pltpu.roll(dvra, HEAD, 1), pltpu.roll(dvlb, HEAD, 1) + dvrb)
        dq = dqr * cq_r[...] + _swap32(dqr * sq_r[...])
        dkp, dkc = dkb[:BLK], dkb[BLK:]
        dkp = dkp * ckp[...] + _swap32(dkp * skp[...])
        dkc = dkc * ckc[...] + _swap32(dkc * skc[...])
        dc = jnp.concatenate([dq, jnp.concatenate([dkc, dvb[BLK:]], axis=1) + carry[...]], axis=1)
        carry[...] = jnp.concatenate([dkp, dvb[:BLK]], axis=1)
        dcur[...] = dc.astype(dcur.dtype)
        db[...] += jnp.sum(dc, axis=0, keepdims=True)
        dsk[...] += dsinks

    full = lambda a: pl.BlockSpec(a.shape, lambda s: (0, 0))
    return pl.pallas_call(
        body, name="swa_bwd", grid=(nb,),
        in_specs=[cur(SWA_COLS), prev(SWA_COLS), full(b), full(sinks), cur(SWA_W), cur(SWA_W), cur(KV_W), cur(KV_W), prev(KV_W), prev(KV_W),
                  cur(SWA_W)],
        out_specs=[cur(SWA_COLS), full(b), full(sinks)],
        out_shape=[jax.ShapeDtypeStruct((T, SWA_COLS), BF16), jax.ShapeDtypeStruct(b.shape, F32), jax.ShapeDtypeStruct(sinks.shape, F32)],
        scratch_shapes=[pltpu.VMEM((BLK, 2 * KV_W), F32)], compiler_params=_cp(("arbitrary",)),
    )(p_swa, p_swa, b, sinks, cq, sq, ck, sk, ck, sk, do)


def _rope_tables(T):
    inv = 10000.0 ** (-jnp.arange(0, HEAD, 2, dtype=F32) / HEAD)
    ang = jnp.arange(T, dtype=F32)[:, None] * inv[None, :]
    c = jnp.concatenate([jnp.cos(ang), jnp.cos(ang)], axis=1)
    s = jnp.concatenate([-jnp.sin(ang), jnp.sin(ang)], axis=1)
    return jnp.tile(c, (1, 16)), jnp.tile(s, (1, 16)), jnp.tile(c, (1, 2)), jnp.tile(s, (1, 2))


def _xattn_core(*qkv):
    outs = []
    for h in range(XH):
        qh, kh, vh = qkv[h], qkv[XH + h], qkv[2 * XH + h]
        s = lax.dot_general(qh.astype(BF16), kh.astype(BF16), (((1,), (1,)), ((), ())), preferred_element_type=F32) * (XHD ** -0.5)
        p = jnp.exp(s - jnp.max(s, axis=1, keepdims=True))
        p = p / jnp.sum(p, axis=1, keepdims=True)
        outs.append(jnp.dot(p.astype(BF16), vh.astype(BF16), preferred_element_type=F32))
    return jnp.concatenate(outs, axis=1)


def _xattn_split(q, kv):
    return [q[:, h * XHD:(h + 1) * XHD] for h in range(XH)] + [kv[:, h * XHD:(h + 1) * XHD] for h in range(2 * XH)]


def _xattn_fwd(q, kv, tm=256):
    (o,) = _rows(lambda i, q, kv: (_xattn_core(*_xattn_split(q, kv)),), "xattn_fwd", q.shape[0], tm, [q], [kv], [(q.shape[1], BF16)], [])
    return o


def _xattn_bwd(q, kv, do, tm=256):
    def fn(i, q, do, kv):
        _, vjp = jax.vjp(_xattn_core, *_xattn_split(q, kv))
        d = vjp(do.astype(F32))
        return jnp.concatenate(d[:XH], axis=1), jnp.concatenate(d[XH:], axis=1)

    return _rows(fn, "xattn_bwd", q.shape[0], tm, [q, do], [kv], [(q.shape[1], BF16)], [(kv.shape, F32)])


def _loss_head(x, g, tgt, tm=256):
    D = x.shape[1]

    def fn(i, x, tgt, g):
        y, vjp = jax.vjp(_rms, x, g)
        err = y - tgt
        dx, dg = vjp(err * (1.0 / D))
        part = 0.5 / D * jnp.sum(jnp.sum(err * err, axis=1, keepdims=True), axis=0, keepdims=True)
        return dx, jnp.broadcast_to(part, (1, 128)), dg

    return _rows(fn, "loss_head", x.shape[0], tm, [x, tgt], [g], [(D, F32)], [((1, 128), F32), ((1, D), F32)])


def _local_step(x, mem, tgt, get_w, P, put_g):
    T = x.shape[0]
    W = dict(get_w("f1", None))
    x1, s1 = _ffn_fwd(x, P["f1_norm"], W["f1_gate"], W["f1_up"], W["f1_down"], "f1", after=W.get("_after"))

    W.update(get_w("mix", x1))
    h2 = _rms_fwd(x1, P["mix_norm"], "mix_norm")
    w_rkv, w_lora, w_swa = W["w_inT"][:3 * RW_W], W["w_inT"][3 * RW_W:SHIFT_COLS], W["w_inT"][SHIFT_COLS:]
    p_rkv = _mm(h2, w_rkv, "nt", "in_rkv")
    p_lora = _mm(h2, w_lora, "nt", "in_lora")
    p_swa = _mm(h2, w_swa, "nt", "in_swa")
    w_da = jnp.concatenate([W["rw_decay_up"], W["rw_aaa_up"]], axis=0)
    pre_params = (P["rw_mu"][:, :3 * RW_W], P["rw_mu"][:, 3 * RW_W:], P["rw_w0"], P["rw_a0"], P["rw_k_k"], P["rw_k_a"], w_da,
                  W["rw_gate_up"])
    r, decay, k2, v, an, bn, g = _rwkv_pre(p_rkv, p_lora, pre_params)
    scan_vecs = (an, decay, bn, k2, r)
    xes = [_to_perm(a) for a in scan_vecs]
    yi, sai, ck = _scan_fwd(xes, _to_tile(v))
    y_scan = _from_tile(yi)
    y_rw = _rwkv_post(y_scan, r, k2, v, g, P["rw_lnx_w"], P["rw_lnx_b"], P["rw_r_k"])
    cq, sq, ckt, skt = _rope_tables(T)
    y_swa = _swa_fwd(p_swa, P["b_in_attn"], P["attn_sinks"], cq, sq, ckt, skt)
    ycat = jnp.concatenate([y_rw, y_swa], axis=1)
    x2 = _mm(ycat, W["w_out"], "nn", "out_proj", res=x1, bias=P["b_out"])

    W.update(get_w("xattn", x2))
    hx = _rms_fwd(x2, P["xa_norm"], "xa_norm")
    mn = _rms_fwd(mem, P["mem_norm"], "mem_norm")
    q = _mm(hx, W["w_xq"], "nn", "xq", out_dtype=BF16)
    kv = _mm(mn, W["w_xkv"], "nn", "xkv", out_dtype=BF16)
    o = _xattn_fwd(q, kv)
    x3 = _mm(o, W["w_xo"], "nn", "xo", res=x2)

    W.update(get_w("f2", x3))
    x4, s2 = _ffn_fwd(x3, P["f2_norm"], W["f2_gate"], W["f2_up"], W["f2_down"], "f2")
    dx4, loss_part, d_final = _loss_head(x4, P["final_norm"], tgt)

    gs = {"final_norm": d_final}
    dx3, gs["f2_norm"] = _ffn_bwd(x3, P["f2_norm"], W["f2_gate"], W["f2_up"], W["f2_down"], s2, dx4, "f2",
                                  lambda dwg, dwu, dwd: put_g("f2", {"f2_gate": dwg, "f2_up": dwu, "f2_down": dwd}))

    do = _mm(dx3, W["w_xo"], "nt", "xo_do", out_dtype=BF16)
    dw_xo = _mm(o, dx3, "tn", "xo_dw", out_dtype=BF16)
    dq, dkv = _xattn_bwd(q, kv, do)
    dw_xq = _mm(hx, dq, "tn", "xq_dw", out_dtype=BF16)
    dw_xkv = _mm(mn, dkv, "tn", "xkv_dw", out_dtype=BF16)
    sent = put_g("xattn", {"w_xq": dw_xq, "w_xkv": dw_xkv, "w_xo": dw_xo})
    dhx = _mm(dq, W["w_xq"], "nt", "xq_dh", after=sent)
    dmn = _mm(dkv, W["w_xkv"], "nt", "xkv_dmn")
    _, gs["mem_norm"], _ = _rms_bwd(mem, P["mem_norm"], dmn, jnp.zeros_like(mem), "mem_norm_bwd")
    dx2, gs["xa_norm"], gs["b_out"] = _rms_bwd(x2, P["xa_norm"], dhx, dx3, "xa_norm_bwd")

    dycat = _mm(dx2, W["w_out"], "nt", "out_dy")
    dw_out = _mm(ycat, dx2, "tn", "out_dw", out_dtype=BF16)
    dp_swa, gs["b_in_attn"], gs["attn_sinks"] = _swa_bwd(p_swa, P["b_in_attn"], P["attn_sinks"], cq, sq, ckt, skt, dycat[:, RW_W:])
    dy_scan, dr_b, dk2_b, dv_b, dg, gs["rw_lnx_w"], gs["rw_lnx_b"], gs["rw_r_k"] = _rwkv_post_bwd(
        y_scan, r, k2, v, g, P["rw_lnx_w"], P["rw_lnx_b"], P["rw_r_k"], dycat[:, :RW_W])
    dsai, dvi = _scan_bwd_a(xes, _to_tile(dy_scan))
    ies = [_to_perm(v), _to_perm(_from_tile(sai)), _to_perm(dy_scan), _to_perm(_from_tile(dsai))]
    dj = _scan_bwd_b([_to_tile(a) for a in scan_vecs], ies, _ck_a_to_b(ck))
    dan, ddecay, dbn, dk2_s, dr_s = (_from_tile(d) for d in dj)
    cts = (dr_s, ddecay, dk2_s, _from_tile(dvi), dan, dbn, dg, dr_b, dk2_b, dv_b)
    dp_rkv, dp_lora, dmu, dmul, gs["rw_w0"], gs["rw_a0"], gs["rw_k_k"], gs["rw_k_a"], dw_da, gs["rw_gate_up"] = _rwkv_pre_bwd(
        p_rkv, p_lora, pre_params, cts)
    gs["rw_mu"] = jnp.concatenate([dmu, dmul], axis=1)
    gs["rw_decay_up"], gs["rw_aaa_up"] = dw_da[:DECAY_LORA], dw_da[DECAY_LORA:]
    dw_inT = jnp.concatenate([_mm(dp_rkv, h2, "tn", "in_dw_rkv"), _mm(dp_lora, h2, "tn", "in_dw_lora"),
                              _mm(dp_swa, h2, "tn", "in_dw_swa")], axis=0)
    sent = put_g("mix", {"w_inT": dw_inT, "w_out": dw_out})
    dh2 = _mm(dp_rkv, w_rkv, "nn", "in_dh_rkv", after=sent)
    dh2 = _mm(dp_lora, w_lora, "nn", "in_dh_lora", res=dh2)
    dh2 = _mm(dp_swa, w_swa, "nn", "in_dh_swa", res=dh2)
    dx1, gs["mix_norm"], _ = _rms_bwd(x1, P["mix_norm"], dh2, dx2, "mix_norm_bwd")

    dx0, gs["f1_norm"] = _ffn_bwd(x, P["f1_norm"], W["f1_gate"], W["f1_up"], W["f1_down"], s1, dx1, "f1",
                                  lambda dwg, dwu, dwd: put_g("f1", {"f1_gate": dwg, "f1_up": dwu, "f1_down": dwd}))
    return loss_part, dx0, gs


_ANY = pl.BlockSpec(memory_space=pl.ANY)
_OTHER_CHIPS = ((1, 0), (0, 1), (1, 1))


def _mesh_pos():
    return lax.axis_index("x"), lax.axis_index("y"), lax.axis_index("c")


def _slot(ref, kind, s, rows, cols):
    if kind == "row":
        return ref.at[pl.ds(pl.multiple_of(s * rows, 8), rows), :]
    return ref.at[:, pl.ds(pl.multiple_of(s * cols, 128), cols)]


_HBM = pl.BlockSpec(memory_space=pltpu.HBM)
_SEMS = pl.BlockSpec(memory_space=pltpu.SEMAPHORE)
_SPLIT = dict(compiler_params=pltpu.CompilerParams(has_side_effects=pltpu.SideEffectType.DATAFLOW_SIDE_EFFECTING))


def _in_hbm(a):
    return pltpu.with_memory_space_constraint(a, pltpu.HBM)


def _full_shape(s, kind):
    return (4 * s.shape[0], s.shape[1]) if kind == "row" else (s.shape[0], 4 * s.shape[1])


def _gather_now(name, shards, kinds):
    n = len(shards)

    def body(*refs):
        src, out = refs[:n], refs[n:2 * n]
        ici_send, ici_recv, d2d_send, d2d_recv, loc = refs[2 * n:]
        x, y, c = _mesh_pos()
        me = 2 * x + y
        half = lambda ref, rows, h: ref.at[pl.ds(pl.multiple_of(h * (rows // 2), 8), rows // 2), :]
        own, sent = [], []
        for i in range(n):
            rows, cols = src[i].shape
            mine = _slot(out[i], kinds[i], me, rows, cols)
            cp = pltpu.make_async_copy(src[i], mine, loc.at[i])
            cp.start()
            own.append(cp)
            for r, (dx, dy) in enumerate(_OTHER_CHIPS):
                rc = pltpu.make_async_remote_copy(half(src[i], rows, c), half(mine, rows, c), ici_send.at[3 * i + r], ici_recv.at[3 * i + r],
                                                  device_id=((x + dx) % 2, (y + dy) % 2, c), device_id_type=MESH)
                rc.start()
                sent.append(rc)
        for i in range(n):
            rows, cols = src[i].shape
            for r, (dx, dy) in enumerate(_OTHER_CHIPS):
                theirs = _slot(out[i], kinds[i], 2 * ((x + dx) % 2) + (y + dy) % 2, rows, cols)
                landed = half(theirs, rows, c)
                pltpu.make_async_remote_copy(landed, landed, ici_send.at[3 * i + r], ici_recv.at[3 * i + r],
                                             device_id=(x, y, c), device_id_type=MESH).wait_recv()
                fw = pltpu.make_async_remote_copy(landed, landed, d2d_send.at[3 * i + r], d2d_recv.at[3 * i + r],
                                                  device_id=(x, y, 1 - c), device_id_type=MESH)
                fw.start()
                sent.append(fw)
        for i in range(n):
            rows, cols = src[i].shape
            for r, (dx, dy) in enumerate(_OTHER_CHIPS):
                other = half(_slot(out[i], kinds[i], 2 * ((x + dx) % 2) + (y + dy) % 2, rows, cols), rows, 1 - c)
                pltpu.make_async_remote_copy(other, other, d2d_send.at[3 * i + r], d2d_recv.at[3 * i + r],
                                             device_id=(x, y, c), device_id_type=MESH).wait_recv()
        for cp in sent:
            cp.wait_send()
        for cp in own:
            cp.wait()

    return pl.pallas_call(
        body, name=name, in_specs=[_ANY] * n, out_specs=[_ANY] * n,
        out_shape=[jax.ShapeDtypeStruct(_full_shape(s, k), s.dtype) for s, k in zip(shards, kinds)],
        scratch_shapes=[pltpu.SemaphoreType.DMA((3 * n,))] * 4 + [pltpu.SemaphoreType.DMA((n,))],
    )(*shards)


def _gather_start(name, shards, kinds, groups, after=None):
    n, ng = len(shards), len(groups)
    lands = [_in_hbm(lax.empty(_full_shape(s, k), s.dtype)) for s, k in zip(shards, kinds)]
    n_in = 2 * n + (after is not None)

    def body(*refs):
        src, land, sems, token = refs[:n], refs[n:2 * n], refs[n_in:n_in + 3 * ng], refs[-1]
        x, y, c = _mesh_pos()
        me = 2 * x + y
        for gi, idxs in enumerate(groups):
            send, recv, own = sems[3 * gi:3 * gi + 3]
            for k, i in enumerate(idxs):
                mine = _slot(land[i], kinds[i], me, *src[i].shape)
                for r, (dx, dy) in enumerate(_OTHER_CHIPS):
                    pltpu.make_async_remote_copy(src[i], mine, send.at[3 * k + r], recv.at[3 * k + r],
                                                 device_id=((x + dx) % 2, (y + dy) % 2, c), device_id_type=MESH).start()
                pltpu.make_async_copy(src[i], mine, own.at[k]).start()
        token[...] = jnp.zeros_like(token)

    sem_shapes = [pltpu.SemaphoreType.DMA((w * len(g),)) for g in groups for w in (3, 3, 1)]
    thru = [pltpu.HBM(a.shape, a.dtype) for a in (*shards, *lands)]
    res = pl.pallas_call(
        body, name=name, in_specs=[_HBM] * (2 * n) + [_ANY] * (after is not None),
        out_specs=[_SEMS] * (3 * ng) + [_HBM] * (2 * n) + [pl.BlockSpec(memory_space=pltpu.VMEM)],
        out_shape=sem_shapes + thru + [jax.ShapeDtypeStruct((8, 128), F32)],
        input_output_aliases={i: 3 * ng + i for i in range(2 * n)}, **_SPLIT,
    )(*[_in_hbm(s) for s in shards], *lands, *([] if after is None else [after]))
    return res[:3 * ng], res[3 * ng:3 * ng + n], res[3 * ng + n:3 * ng + 2 * n], res[-1]


def _gather_wait(name, sems, shards, lands, kinds, after):
    m = len(shards)

    def body(*refs):
        src, land, (send, recv, own) = refs[:m], refs[m:2 * m], refs[2 * m:2 * m + 3]
        x, y, c = _mesh_pos()
        me = 2 * x + y
        for k in range(m):
            mine = _slot(land[k], kinds[k], me, *src[k].shape)
            for r in range(3):
                cp = pltpu.make_async_remote_copy(src[k], mine, send.at[3 * k + r], recv.at[3 * k + r],
                                                  device_id=(x, y, c), device_id_type=MESH)
                cp.wait_send()
                cp.wait_recv()
            pltpu.make_async_copy(src[k], mine, own.at[k]).wait()

    thru = [pltpu.HBM(a.shape, a.dtype) for a in (*shards, *lands)]
    res = pl.pallas_call(
        body, name=name, in_specs=[_HBM] * (2 * m) + [_SEMS] * 3 + [pl.BlockSpec(memory_space=pl.ANY)],
        out_specs=[_HBM] * (2 * m), out_shape=thru, input_output_aliases={i: i for i in range(2 * m)}, **_SPLIT,
    )(*shards, *lands, *sems, after)
    return res[m:]


def _scatter_start(name, grads, kinds):
    m = len(grads)
    shard_shape = [(g.shape[0] // 4, g.shape[1]) if k == "row" else (g.shape[0], g.shape[1] // 4) for g, k in zip(grads, kinds)]
    lands = [_in_hbm(lax.empty((4, *s), g.dtype)) for s, g in zip(shard_shape, grads)]

    def body(*refs):
        src, land, (send, recv, own) = refs[:m], refs[m:2 * m], refs[2 * m:2 * m + 3]
        x, y, c = _mesh_pos()
        me = 2 * x + y
        for k in range(m):
            for r, (dx, dy) in enumerate(_OTHER_CHIPS):
                tx, ty = (x + dx) % 2, (y + dy) % 2
                pltpu.make_async_remote_copy(_slot(src[k], kinds[k], 2 * tx + ty, *shard_shape[k]), land[k].at[me],
                                             send.at[3 * k + r], recv.at[3 * k + r], device_id=(tx, ty, c), device_id_type=MESH).start()
            pltpu.make_async_copy(_slot(src[k], kinds[k], me, *shard_shape[k]), land[k].at[me], own.at[k]).start()
        refs[-1][...] = jnp.zeros_like(refs[-1])

    thru = [pltpu.HBM(a.shape, a.dtype) for a in (*grads, *lands)]
    res = pl.pallas_call(
        body, name=name, in_specs=[_HBM] * (2 * m),
        out_specs=[_SEMS] * 3 + [_HBM] * (2 * m) + [pl.BlockSpec(memory_space=pltpu.VMEM)],
        out_shape=[pltpu.SemaphoreType.DMA((3 * m,))] * 2 + [pltpu.SemaphoreType.DMA((m,))] + thru + [jax.ShapeDtypeStruct((8, 128), F32)],
        input_output_aliases={i: 3 + i for i in range(2 * m)}, **_SPLIT,
    )(*[_in_hbm(g) for g in grads], *lands)
    return res[:3], res[3:3 + m], res[3 + m:3 + 2 * m], res[-1]


def _scatter_wait(name, sems, grads, lands, kinds, after):
    m = len(grads)

    def body(*refs):
        src, land, (send, recv, own) = refs[:m], refs[m:2 * m], refs[2 * m:2 * m + 3]
        x, y, c = _mesh_pos()
        me = 2 * x + y
        for k in range(m):
            mine = _slot(src[k], kinds[k], me, *land[k].shape[1:])
            for r in range(3):
                cp = pltpu.make_async_remote_copy(mine, land[k].at[me], send.at[3 * k + r], recv.at[3 * k + r],
                                                  device_id=(x, y, c), device_id_type=MESH)
                cp.wait_send()
                cp.wait_recv()
            pltpu.make_async_copy(mine, land[k].at[me], own.at[k]).wait()

    thru = [pltpu.HBM(a.shape, a.dtype) for a in (*grads, *lands)]
    res = pl.pallas_call(
        body, name=name, in_specs=[_HBM] * (2 * m) + [_SEMS] * 3 + [pl.BlockSpec(memory_space=pl.ANY)],
        out_specs=[_HBM] * (2 * m), out_shape=thru, input_output_aliases={i: i for i in range(2 * m)}, **_SPLIT,
    )(*grads, *lands, *sems, after)
    return res[m:]


def _swap_with_sibling(arrs, name):
    n = len(arrs)

    def body(*refs):
        ins, outs = refs[:n], refs[n:2 * n]
        send, recv = refs[2 * n:]
        x, y, c = _mesh_pos()
        copies = []
        for i in range(n):
            rc = pltpu.make_async_remote_copy(ins[i], outs[i], send.at[i], recv.at[i], device_id=(x, y, 1 - c), device_id_type=MESH)
            rc.start()
            copies.append(rc)
        for rc in copies:
            rc.wait()

    return pl.pallas_call(
        body, name=name, in_specs=[_ANY] * n, out_specs=[_ANY] * n,
        out_shape=[jax.ShapeDtypeStruct(a.shape, a.dtype) for a in arrs],
        scratch_shapes=[pltpu.SemaphoreType.DMA((n,)), pltpu.SemaphoreType.DMA((n,))],
    )(*arrs)


def _gather_small(pack, after):
    def body(in_ref, after_ref, out_ref, send, recv, loc):
        x, y, c = _mesh_pos()
        me = 4 * x + 2 * y + c
        cp = pltpu.make_async_copy(in_ref, out_ref.at[me], loc.at[0])
        cp.start()
        copies = [cp]
        for r in range(1, 8):
            dx, dy, dc = r // 4, (r // 2) % 2, r % 2
            rc = pltpu.make_async_remote_copy(in_ref, out_ref.at[me], send.at[r - 1], recv.at[r - 1],
                                              device_id=((x + dx) % 2, (y + dy) % 2, (c + dc) % 2), device_id_type=MESH)
            rc.start()
            copies.append(rc)
        for cp in copies:
            cp.wait()

    return pl.pallas_call(
        body, name="gather_small", in_specs=[_ANY, _ANY], out_specs=_ANY, out_shape=jax.ShapeDtypeStruct((8, *pack.shape), pack.dtype),
        scratch_shapes=[pltpu.SemaphoreType.DMA((7,)), pltpu.SemaphoreType.DMA((7,)), pltpu.SemaphoreType.DMA((1,))],
    )(pack, after)


def _row_tile(R, dtype, target=256):
    mult = 8 * 4 // jnp.dtype(dtype).itemsize
    best = R
    for t in range(mult, min(R, target) + 1, mult):
        if R % t == 0:
            best = t
    return best


def _sum_slots(stack, name, out_dtype=F32):
    k, R, C = stack.shape
    tr = _row_tile(R, stack.dtype)

    def body(s_ref, o_ref):
        acc = s_ref[0].astype(F32)
        for j in range(1, k):
            acc = acc + s_ref[j].astype(F32)
        o_ref[...] = acc.astype(out_dtype)

    return pl.pallas_call(
        body, name=name, grid=(R // tr,), in_specs=[pl.BlockSpec((k, tr, C), lambda i: (0, i, 0))],
        out_specs=pl.BlockSpec((tr, C), lambda i: (i, 0)), out_shape=jax.ShapeDtypeStruct((R, C), out_dtype),
        compiler_params=_cp(("parallel",)),
    )(stack)


def _adamw(w, m, v, ga, gb, name, after=None):
    R, C = w.shape
    tr = _row_tile(R, F32, 128)
    gs = [ga] if gb is None else [ga, gb]
    extra = [] if after is None else [after]

    def body(*refs):
        w_ref, m_ref, v_ref = refs[:3]
        g = refs[3][...]
        if gb is not None:
            g = g + refs[4][...]
        g_ref, d_ref, nm_ref, nv_ref = refs[-4:]
        nm = ADAM_B1 * m_ref[...] + (1.0 - ADAM_B1) * g
        nv = ADAM_B2 * v_ref[...] + (1.0 - ADAM_B2) * (g * g)
        m_hat = nm / (1.0 - ADAM_B1 ** ADAM_STEP)
        v_hat = nv / (1.0 - ADAM_B2 ** ADAM_STEP)
        g_ref[...] = g
        d_ref[...] = -ADAM_LR * (m_hat / (jnp.sqrt(v_hat) + ADAM_EPS) + ADAM_WD * w_ref[...])
        nm_ref[...] = nm
        nv_ref[...] = nv

    spec = pl.BlockSpec((tr, C), lambda i: (i, 0))
    return pl.pallas_call(
        body, name=name, grid=(R // tr,), in_specs=[spec] * (3 + len(gs)) + [_ANY] * len(extra), out_specs=[spec] * 4,
        out_shape=[jax.ShapeDtypeStruct((R, C), F32)] * 4, compiler_params=_cp(("parallel",)),
    )(w, m, v, *gs, *extra)


def _pack(arrs):
    rows = []
    for a in arrs:
        flat = a.reshape(-1)
        rows.append(jnp.pad(flat, (0, -flat.shape[0] % 1024)).reshape(-1, 1024))
    p = jnp.concatenate(rows, axis=0)
    return jnp.pad(p, ((0, -p.shape[0] % 8), (0, 0)))


def _unpack(p, shapes):
    out, r = [], 0
    for s in shapes:
        n = 1
        for d in s:
            n *= d
        nr = -(-n // 1024)
        out.append(p[r:r + nr].reshape(-1)[:n].reshape(s))
        r += nr
    return out


BIG = ("f1_gate", "f1_up", "f1_down", "w_in", "w_out", "w_xq", "w_xkv", "w_xo", "f2_gate", "f2_up", "f2_down")
BIG_KIND = {"f1_gate": "col", "f1_up": "col", "f1_down": "row", "w_in": "row", "w_out": "row", "w_xq": "row", "w_xkv": "col",
            "w_xo": "row", "f2_gate": "col", "f2_up": "col", "f2_down": "row"}
LORA = ("rw_decay_up", "rw_aaa_up", "rw_gate_up")
WEIGHTS = ("f1_norm", "f1_gate", "f1_up", "f1_down", "mix_norm", "w_in", "b_in_attn", "rw_mu", "rw_w0", "rw_decay_up", "rw_a0",
           "rw_aaa_up", "rw_gate_up", "rw_k_k", "rw_k_a", "rw_r_k", "rw_lnx_w", "rw_lnx_b", "attn_sinks", "w_out", "b_out", "xa_norm",
           "mem_norm", "w_xq", "w_xkv", "w_xo", "f2_norm", "f2_gate", "f2_up", "f2_down", "final_norm")
SMALL = tuple(n for n in WEIGHTS if n not in BIG)
GROUP_ORDER = ("f1", "mix", "xattn", "f2")
GROUPS = {"f1": ("f1_gate", "f1_up", "f1_down"), "mix": ("w_in", "w_out") + LORA, "xattn": ("w_xq", "w_xkv", "w_xo"),
          "f2": ("f2_gate", "f2_up", "f2_down")}


def kernel(x, mem, f1_norm, f1_gate, f1_up, f1_down, mix_norm, w_in, b_in_attn, rw_mu, rw_w0, rw_decay_up, rw_a0, rw_aaa_up, rw_gate_up, rw_k_k, rw_k_a, rw_r_k, rw_lnx_w, rw_lnx_b, attn_sinks, w_out, b_out, xa_norm, mem_norm, w_xq, w_xkv, w_xo, f2_norm, f2_gate, f2_up, f2_down, final_norm, loss_target, m_f1_norm, m_f1_gate, m_f1_up, m_f1_down, m_mix_norm, m_w_in, m_b_in_attn, m_rw_mu, m_rw_w0, m_rw_decay_up, m_rw_a0, m_rw_aaa_up, m_rw_gate_up, m_rw_k_k, m_rw_k_a, m_rw_r_k, m_rw_lnx_w, m_rw_lnx_b, m_attn_sinks, m_w_out, m_b_out, m_xa_norm, m_mem_norm, m_w_xq, m_w_xkv, m_w_xo, m_f2_norm, m_f2_gate, m_f2_up, m_f2_down, m_final_norm, v_f1_norm, v_f1_gate, v_f1_up, v_f1_down, v_mix_norm, v_w_in, v_b_in_attn, v_rw_mu, v_rw_w0, v_rw_decay_up, v_rw_a0, v_rw_aaa_up, v_rw_gate_up, v_rw_k_k, v_rw_k_a, v_rw_r_k, v_rw_lnx_w, v_rw_lnx_b, v_attn_sinks, v_w_out, v_b_out, v_xa_norm, v_mem_norm, v_w_xq, v_w_xkv, v_w_xo, v_f2_norm, v_f2_gate, v_f2_up, v_f2_down, v_final_norm):
    a = dict(locals())
    w = {n: a[n] for n in WEIGHTS}
    m = {n: a["m_" + n] for n in WEIGHTS}
    v = {n: a["v_" + n] for n in WEIGHTS}
    sq = lambda t: t.reshape(t.shape[-2:]) if t.ndim == 3 else t.reshape(1, -1)

    local_name = lambda n: "w_inT" if n == "w_in" else n
    kind_of = lambda n: BIG_KIND.get(n, "col")
    payload = lambda n: sq(w[n]).T if n == "w_in" else sq(w[n]) if n in LORA else sq(w[n]).astype(BF16)
    gathers = {}

    def start_gather(name, grps, after):
        shards = [payload(n) for g in grps for n in GROUPS[g]]
        kinds = [kind_of(n) for g in grps for n in GROUPS[g]]
        groups, at = [], 0
        for g in grps:
            groups.append(list(range(at, at + len(GROUPS[g]))))
            at += len(GROUPS[g])
        sems, src_thru, land_thru, token = _gather_start(name, shards, kinds, groups, after)
        for gi, g in enumerate(grps):
            gathers[g] = (sems[3 * gi:3 * gi + 3], [src_thru[i] for i in groups[gi]], [land_thru[i] for i in groups[gi]],
                          [kinds[i] for i in groups[gi]], token)

    def get_w(grp, after):
        if grp == GROUP_ORDER[0]:
            got = _gather_now("gather_" + grp, [payload(n) for n in GROUPS[grp]], [kind_of(n) for n in GROUPS[grp]])
            start_gather("gather_start_rest", GROUP_ORDER[1:], got[0])
            out = {"_after": gathers[GROUP_ORDER[1]][4]}
        else:
            g_sems, g_src, g_land, g_kinds, _ = gathers[grp]
            got = _gather_wait("gather_wait_" + grp, g_sems, g_src, g_land, g_kinds, after)
            out = {}
        out.update({local_name(n): f for n, f in zip(GROUPS[grp], got)})
        return out

    in_flight = {}

    def put_g(grp, gw):
        names = [n for n in GROUPS[grp] if n in BIG]
        *flight, sent = _scatter_start("scatter_start_" + grp, [gw[local_name(n)] for n in names], [kind_of(n) for n in names])
        in_flight[grp] = (names, flight)
        return sent

    P = {n: sq(w[n]) for n in SMALL if n not in LORA}
    P["attn_sinks"] = jnp.pad(P["attn_sinks"], ((0, 0), (0, 128 - P["attn_sinks"].shape[1])))
    P["rw_r_k"] = w["rw_r_k"].reshape(1, RW_W)
    loss_part, grad_x, gs = _local_step(x[0], mem[0], loss_target[0], get_w, P, put_g)
    loss = lax.psum(loss_part[0, 0], ("x", "y", "c"))

    out, after = {}, grad_x
    for grp in reversed(GROUP_ORDER):
        names, (g_sems, g_thru, l_thru) = in_flight[grp]
        stacks = _scatter_wait("scatter_wait_" + grp, g_sems, g_thru, l_thru, [kind_of(n) for n in names], after)
        partial = [_sum_slots(s, "sum_chips_" + n) for s, n in zip(stacks, names)]
        sibling = _swap_with_sibling(partial, "swap_" + grp)
        chain = None
        for n, pa, sb in zip(names, partial, sibling):
            if n == "w_in":
                pa, sb = pa.T, sb.T
            out[n] = _adamw(sq(w[n]), sq(m[n]), sq(v[n]), pa, sb, "adamw_" + n, after=chain)
            chain = out[n][1]
        after = chain

    gs["attn_sinks"] = gs["attn_sinks"][:, :16]
    gsum = _sum_slots(_gather_small(_pack([gs[n] for n in SMALL]), after), "sum_small")
    g_small = dict(zip(SMALL, _unpack(gsum, [gs[n].shape for n in SMALL])))
    shard = 2 * lax.axis_index("x") + lax.axis_index("y")
    for n in LORA:
        cols = w[n].shape[-1]
        g_small[n] = lax.dynamic_slice_in_dim(g_small[n], shard * cols, cols, axis=1)

    flat = lambda d: _pack([d[n] for n in SMALL])
    res = _adamw(flat(w), flat(m), flat(v), _pack([g_small[n] for n in SMALL]), None, "adamw_small")
    shapes = [w[n].shape for n in SMALL]
    for k, p in enumerate(res):
        for n, t in zip(SMALL, _unpack(p, shapes)):
            out.setdefault(n, [None] * 4)[k] = t
    outs = [loss, grad_x.reshape(x.shape)]
    for k in range(4):
        outs += [out[n][k].reshape(w[n].shape) for n in WEIGHTS]
    return tuple(outs)
```

```python
import functools

import jax
import jax.numpy as jnp
from jax import lax
from jax.experimental import pallas as pl
from jax.experimental.pallas import tpu as pltpu

F32, BF16 = jnp.float32, jnp.bfloat16
MESH = pl.DeviceIdType.MESH

HEAD = 64
RW_HEADS = 16
RW_W = 1024
SWA_W = 1024
KV_W = 128
DECAY_LORA, AAA_LORA, GATE_LORA = 64, 64, 160
LORA_W = DECAY_LORA + AAA_LORA + GATE_LORA
SHIFT_COLS = 3 * RW_W + LORA_W
XH = 4
XHD = 512
MEM_LEN = 256
WINDOW = 128
GN_EPS = 64e-5
RMS_EPS = 1e-6
NEG_INF = -1e30
ADAM_LR, ADAM_B1, ADAM_B2, ADAM_EPS, ADAM_WD, ADAM_STEP = 0.001, 0.9, 0.999, 1e-08, 0.01, 10

VMEM_LIMIT = 56 * 1024 * 1024


def _cp(sem=None, **kw):
    return pltpu.CompilerParams(dimension_semantics=sem, vmem_limit_bytes=VMEM_LIMIT, **kw)


def _pick(dim, target):
    if dim <= target:
        return dim
    best = None
    for t in range(128, target + 1, 128):
        if dim % t == 0:
            best = t
    assert best is not None, (dim, target)
    return best


_DIMS = {"nn": (((1,), (0,)), ((), ())), "nt": (((1,), (1,)), ((), ())), "tn": (((0,), (0,)), ((), ()))}


def _mm(a, b, mode, name, out_dtype=F32, alpha=1.0, res=None, bias=None, tm=1024, tn=1024, tk=2048, after=None):
    if mode == "nn":
        (M, K), (K2, N) = a.shape, b.shape
    elif mode == "nt":
        (M, K), (N, K2) = a.shape, b.shape
    else:
        (K, M), (K2, N) = a.shape, b.shape
    assert K == K2, (name, a.shape, b.shape)
    tm, tn, tk = _pick(M, tm), _pick(N, tn), _pick(K, tk)
    nk = K // tk
    a_spec = pl.BlockSpec((tk, tm), lambda i, j, k: (k, i)) if mode == "tn" else pl.BlockSpec((tm, tk), lambda i, j, k: (i, k))
    b_spec = pl.BlockSpec((tn, tk), lambda i, j, k: (j, k)) if mode == "nt" else pl.BlockSpec((tk, tn), lambda i, j, k: (k, j))
    o_spec = pl.BlockSpec((tm, tn), lambda i, j, k: (i, j))
    ins, specs = [a, b], [a_spec, b_spec]
    if res is not None:
        ins.append(res)
        specs.append(o_spec)
    if bias is not None:
        ins.append(bias)
        specs.append(pl.BlockSpec((1, tn), lambda i, j, k: (0, j)))
    if after is not None:
        ins.append(after)
        specs.append(pl.BlockSpec(memory_space=pl.ANY))
    dims = _DIMS[mode]

    def body(*refs):
        a_ref, b_ref = refs[0], refs[1]
        part = lax.dot_general(a_ref[...].astype(BF16), b_ref[...].astype(BF16), dims, preferred_element_type=F32)

        def finish(o, o_ref):
            if alpha != 1.0:
                o = o * alpha
            p = 2
            if res is not None:
                o = o + refs[p][...].astype(F32)
                p += 1
            if bias is not None:
                o = o + refs[p][...]
            o_ref[...] = o.astype(out_dtype)

        if nk == 1:
            finish(part, refs[-1])
            return
        o_ref, acc_ref = refs[-2], refs[-1]
        k = pl.program_id(2)

        @pl.when(k == 0)
        def _():
            acc_ref[...] = part

        @pl.when(k > 0)
        def _():
            acc_ref[...] += part

        @pl.when(k == nk - 1)
        def _():
            finish(acc_ref[...], o_ref)

    return pl.pallas_call(
        body, name=name, grid=(M // tm, N // tn, nk), in_specs=specs, out_specs=o_spec,
        out_shape=jax.ShapeDtypeStruct((M, N), out_dtype), scratch_shapes=[pltpu.VMEM((tm, tn), F32)] * (nk > 1),
        compiler_params=_cp(("parallel", "parallel", "arbitrary")),
    )(*ins)


def _rows(fn, name, T, tm, tiled, full, out_tiled, out_acc, extra=(), reverse=False, scratch=()):
    n = T // tm
    idx = (lambda i: n - 1 - i) if reverse else (lambda i: i)
    in_specs = [pl.BlockSpec((tm, a.shape[1]), lambda i: (idx(i), 0)) for a in tiled]
    in_specs += [mk(idx) for _, mk in extra]
    in_specs += [pl.BlockSpec(a.shape, lambda i, nd=a.ndim: (0,) * nd) for a in full]
    out_specs = [pl.BlockSpec((tm, c), lambda i: (idx(i), 0)) for c, _ in out_tiled]
    out_specs += [pl.BlockSpec(s, lambda i, nd=len(s): (0,) * nd) for s, _ in out_acc]
    out_shape = [jax.ShapeDtypeStruct((T, c), d) for c, d in out_tiled] + [jax.ShapeDtypeStruct(s, d) for s, d in out_acc]
    n_in = len(tiled) + len(extra) + len(full)
    n_t, n_a = len(out_tiled), len(out_acc)

    def body(*refs):
        step = pl.program_id(0)
        vals = [r[...] for r in refs[:n_in]]
        outs = fn(idx(step), *vals, *refs[n_in + n_t + n_a:])
        for r, v in zip(refs[n_in:n_in + n_t], outs[:n_t]):
            r[...] = v.astype(r.dtype)
        for r, v in zip(refs[n_in + n_t:n_in + n_t + n_a], outs[n_t:]):
            @pl.when(step == 0)
            def _(r=r):
                r[...] = jnp.zeros_like(r)

            r[...] += v

    return pl.pallas_call(
        body, name=name, grid=(n,), in_specs=in_specs, out_specs=out_specs, out_shape=out_shape,
        scratch_shapes=list(scratch), compiler_params=_cp(("arbitrary",)),
    )(*tiled, *[a for a, _ in extra], *full)


def _rms(x, g):
    return x * lax.rsqrt(jnp.mean(x * x, axis=-1, keepdims=True) + RMS_EPS) * g


def _rms_fwd(x, g, name, tm=256):
    (h,) = _rows(lambda i, x, g: (_rms(x, g),), name, x.shape[0], min(tm, x.shape[0]), [x], [g], [(x.shape[1], BF16)], [])
    return h


def _rms_bwd(x, g, dh, dres, name, tm=256):
    D = x.shape[1]

    def fn(i, x, dh, dres, g):
        _, vjp = jax.vjp(_rms, x, g)
        dx, dg = vjp(dh.astype(F32))
        dx = dx + dres
        return dx, dg, jnp.sum(dx, axis=0, keepdims=True)

    return _rows(fn, name, x.shape[0], tm, [x, dh, dres], [g], [(D, F32)], [((1, D), F32), ((1, D), F32)])


def _ffn_up(h, wg, wu, name, tm=1024, tn=512, after=None):
    (M, K), N = h.shape, wg.shape[1]
    tm, tn = _pick(M, tm), _pick(N, tn)

    def body(*refs):
        h_ref, wg_ref, wu_ref = refs[:3]
        g_ref, u_ref, a_ref = refs[-3:]
        hb = h_ref[...].astype(BF16)
        g = jnp.dot(hb, wg_ref[...].astype(BF16), preferred_element_type=F32)
        u = jnp.dot(hb, wu_ref[...].astype(BF16), preferred_element_type=F32)
        g_ref[...] = g
        u_ref[...] = u
        a_ref[...] = (g * jax.nn.sigmoid(g) * u).astype(BF16)

    o_spec = pl.BlockSpec((tm, tn), lambda i, j: (i, j))
    w_spec = pl.BlockSpec((K, tn), lambda i, j: (0, j))
    extra = [] if after is None else [after]
    return pl.pallas_call(
        body, name=name, grid=(M // tm, N // tn),
        in_specs=[pl.BlockSpec((tm, K), lambda i, j: (i, 0)), w_spec, w_spec] + [pl.BlockSpec(memory_space=pl.ANY)] * len(extra),
        out_specs=[o_spec] * 3, out_shape=[jax.ShapeDtypeStruct((M, N), F32)] * 2 + [jax.ShapeDtypeStruct((M, N), BF16)],
        compiler_params=_cp(("parallel", "parallel")),
    )(h, wg, wu, *extra)


def _act_bwd(da, g, u, name, tm=256):
    def fn(i, da, g, u):
        s = jax.nn.sigmoid(g)
        return da * u * (s * (1.0 + g * (1.0 - s))), da * (g * s)

    F = g.shape[1]
    return _rows(fn, name, g.shape[0], tm, [da, g, u], [], [(F, BF16), (F, BF16)], [])


def _ffn_fwd(x, gain, wg, wu, wd, tag, after=None):
    h = _rms_fwd(x, gain, tag + "_norm")
    G, U, A = _ffn_up(h, wg, wu, tag + "_up", after=after)
    xo = _mm(A, wd, "nn", tag + "_down", alpha=0.5, res=x)
    return xo, (h, G, U, A)


def _ffn_bwd(x, gain, wg, wu, wd, saved, dxo, tag, send):
    h, G, U, A = saved
    dA = _mm(dxo, wd, "nt", tag + "_dA", alpha=0.5)
    dwd = _mm(A, dxo, "tn", tag + "_dwd", out_dtype=BF16, alpha=0.5)
    sent = send(tag + "_down", {tag + "_down": dwd})
    dG, dU = _act_bwd(dA, G, U, tag + "_act_bwd")
    dwu = _mm(h, dU, "tn", tag + "_dwu", out_dtype=BF16, after=sent)
    sent = send(tag + "_up", {tag + "_up": dwu})
    dwg = _mm(h, dG, "tn", tag + "_dwg", out_dtype=BF16, after=sent)
    sent = send(tag + "_gate", {tag + "_gate": dwg})
    dh = _mm(dG, wg, "nt", tag + "_dh_g", after=sent)
    dh = _mm(dU, wu, "nt", tag + "_dh_u", res=dh)
    dx, dgain, _ = _rms_bwd(x, gain, dh, dxo, tag + "_norm_bwd")
    return dx, dgain


def _segsum64_impl(x):
    r = lax.broadcasted_iota(jnp.int32, (128, 128), 0) // HEAD
    c = lax.broadcasted_iota(jnp.int32, (128, 128), 1) // HEAD
    ones = (r == c).astype(BF16)
    hi = x.astype(BF16)
    lo = (x - hi.astype(F32)).astype(BF16)
    outs = []
    for q in range(x.shape[1] // 128):
        sl = slice(q * 128, (q + 1) * 128)
        outs.append(jnp.dot(hi[:, sl], ones, preferred_element_type=F32) + jnp.dot(lo[:, sl], ones, preferred_element_type=F32))
    return outs[0] if len(outs) == 1 else jnp.concatenate(outs, axis=1)


@jax.custom_vjp
def _segsum64(x):
    return _segsum64_impl(x)


_segsum64.defvjp(lambda x: (_segsum64_impl(x), None), lambda _, ct: (_segsum64_impl(ct),))


def _swap32(x):
    lane = lax.broadcasted_iota(jnp.int32, (x.shape[0], 128), 1)
    outs = [jnp.take_along_axis(x[:, q * 128:(q + 1) * 128], lane ^ 32, axis=1) for q in range(x.shape[1] // 128)]
    return outs[0] if len(outs) == 1 else jnp.concatenate(outs, axis=1)


def _tree_sum(xs):
    xs = list(xs)
    while len(xs) > 1:
        nxt = [xs[i] + xs[i + 1] for i in range(0, len(xs) - 1, 2)]
        if len(xs) % 2:
            nxt.append(xs[-1])
        xs = nxt
    return xs[0]


def _softplus(x):
    return jnp.maximum(x, 0.0) + jnp.log(1.0 + jnp.exp(-jnp.abs(x)))


def _pre_core(k, da, gd, w0, a0, k_k, k_a, w_da, gate_up):
    lane = lax.broadcasted_iota(jnp.int32, da.shape, 1)
    w_da = w_da.astype(BF16)
    l1 = jnp.dot(jnp.where(lane < DECAY_LORA, jnp.tanh(da), 0.0).astype(BF16), w_da, preferred_element_type=F32)
    l2 = jnp.dot(jnp.where(lane >= DECAY_LORA, da, 0.0).astype(BF16), w_da, preferred_element_type=F32)
    wlog = -_softplus(-(w0 + l1)) - 0.5
    decay = jnp.exp(-jnp.exp(wlog))
    a = jax.nn.sigmoid(a0 + l2)
    g = jnp.dot(jax.nn.sigmoid(gd).astype(BF16), gate_up.astype(BF16), preferred_element_type=F32)
    kk = k * k_k
    kkn = kk / jnp.maximum(jnp.sqrt(_segsum64(kk * kk)), 1e-12)
    k2 = k * (1.0 + (a - 1.0) * k_a)
    return decay, k2, -kkn, kkn * a, g


def _pre_shift(i, zr, zl, zr8, zl8, mu, mul):
    live = (i > 0).astype(F32)
    dz = _shift_down(zr, zr8[7:8, :] * live) - zr
    dzl = _shift_down(zl, zl8[7:8, :] * live) - zl
    return zr + dz * mu, zl + dzl * mul, dz, dzl


def _shift_down(x, first_row):
    rolled = pltpu.roll(x, 1, 0)
    row = lax.broadcasted_iota(jnp.int32, x.shape, 0)
    return jnp.where(row == 0, first_row, rolled)


def _shift_up(x, last_row):
    rolled = pltpu.roll(x, x.shape[0] - 1, 0)
    row = lax.broadcasted_iota(jnp.int32, x.shape, 0)
    return jnp.where(row == x.shape[0] - 1, last_row, rolled)


def _prev_rows_spec(tm, cols):
    return lambda idx: pl.BlockSpec((8, cols), lambda i: (jnp.maximum(idx(i) * (tm // 8) - 1, 0), 0))


def _rwkv_pre(p_rkv, p_lora, params, tm=256):
    T = p_rkv.shape[0]

    def fn(i, zr, zl, zr8, zl8, mu, mul, *ps):
        z, z2, _, _ = _pre_shift(i, zr, zl, zr8, zl8, mu, mul)
        decay, k2, an, bn, g = _pre_core(z[:, RW_W:2 * RW_W], z2[:, :128], z2[:, 128:], *ps)
        return z[:, :RW_W], decay, k2, z[:, 2 * RW_W:], an, bn, g

    extra = [(p_rkv, _prev_rows_spec(tm, 3 * RW_W)), (p_lora, _prev_rows_spec(tm, LORA_W))]
    return _rows(fn, "rwkv_pre", T, tm, [p_rkv, p_lora], list(params), [(RW_W, F32)] * 7, [], extra=extra)


def _rwkv_pre_bwd(p_rkv, p_lora, params, cts, tm=256):
    T = p_rkv.shape[0]
    n = T // tm

    def fn(i, zr, zl, cr, cdec, ck2, cv, can, cbn, cg, cr_b, ck2_b, cv_b, zr8, zl8, mu, mul, *rest):
        ps, (car, carl) = rest[:-2], rest[-2:]
        cr, ck2, cv = cr + cr_b, ck2 + ck2_b, cv + cv_b
        z, z2, dif, difl = _pre_shift(i, zr, zl, zr8, zl8, mu, mul)
        _, vjp = jax.vjp(_pre_core, z[:, RW_W:2 * RW_W], z2[:, :128], z2[:, 128:], *ps)
        dk, dda, dgd, *dps = vjp((cdec, ck2, can, cbn, cg))
        dz = jnp.concatenate([cr, dk, cv], axis=1)
        dz2 = jnp.concatenate([dda, dgd], axis=1)
        dzp, dzlp = dz * mu, dz2 * mul

        @pl.when(i == n - 1)
        def _():
            car[...] = jnp.zeros_like(car)
            carl[...] = jnp.zeros_like(carl)

        d_rkv = dz - dzp + _shift_up(dzp, car[0:1, :])
        d_lora = dz2 - dzlp + _shift_up(dzlp, carl[0:1, :])
        car[0:1, :] = dzp[0:1, :]
        carl[0:1, :] = dzlp[0:1, :]
        return (d_rkv, d_lora, jnp.sum(dz * dif, axis=0, keepdims=True), jnp.sum(dz2 * difl, axis=0, keepdims=True), *dps)

    extra = [(p_rkv, _prev_rows_spec(tm, 3 * RW_W)), (p_lora, _prev_rows_spec(tm, LORA_W))]
    acc = [(p.shape, F32) for p in params]
    return _rows(fn, "rwkv_pre_bwd", T, tm, [p_rkv, p_lora, *cts], list(params), [(3 * RW_W, BF16), (LORA_W, BF16)], acc,
                 extra=extra, reverse=True, scratch=[pltpu.VMEM((8, 3 * RW_W), F32), pltpu.VMEM((8, LORA_W), F32)])


def _post_core(y, r, k2, v, g, lw, lb, rk):
    mu = _segsum64(y) * (1.0 / HEAD)
    yc = y - mu
    var = _segsum64(yc * yc) * (1.0 / HEAD)
    yn = yc * lax.rsqrt(var + GN_EPS) * lw + lb
    return (yn + _segsum64(r * k2 * rk) * v) * g


def _rwkv_post(y, r, k2, v, g, lw, lb, rk, tm=256):
    (o,) = _rows(lambda i, *a: (_post_core(*a),), "rwkv_post", y.shape[0], tm, [y, r, k2, v, g], [lw, lb, rk], [(RW_W, BF16)], [])
    return o


def _rwkv_post_bwd(y, r, k2, v, g, lw, lb, rk, do, tm=256):
    def fn(i, y, r, k2, v, g, do, lw, lb, rk):
        _, vjp = jax.vjp(_post_core, y, r, k2, v, g, lw, lb, rk)
        return vjp(do.astype(F32))

    return _rows(fn, "rwkv_post_bwd", y.shape[0], tm, [y, r, k2, v, g, do], [lw, lb, rk], [(RW_W, F32)] * 5, [((1, RW_W), F32)] * 3)


SCAN_L = 32


def _to_tile(x):
    T = x.shape[0]
    return x.reshape(T, RW_HEADS, 8, 8).transpose(0, 2, 1, 3).reshape(T, 8, 128)


def _from_tile(x):
    T = x.shape[0]
    return x.reshape(T, 8, RW_HEADS, 8).transpose(0, 2, 1, 3).reshape(T, RW_W)


def _to_perm(x):
    T = x.shape[0]
    return x.reshape(T, RW_HEADS, HEAD).transpose(0, 2, 1).reshape(T, 8, 128)


def _expander(srcs):
    s = lax.broadcasted_iota(jnp.int32, (8, 128), 0)
    lane = lax.broadcasted_iota(jnp.int32, (8, 128), 1)
    idx = 16 * s + lane // 8

    def expand(t, e_ref):
        for m, r in enumerate(srcs):
            for g in range(8):
                row = jnp.broadcast_to(r[t, pl.ds(g, 1), :], (8, 128))
                e_ref[m, g * 8:(g + 1) * 8, :] = jnp.take_along_axis(row, idx, axis=1)

    return expand


def _ck_a_to_b(ck):
    n = ck.shape[0]
    return ck.reshape(n, 8, 8, 8, RW_HEADS, 8).transpose(0, 3, 5, 1, 4, 2).reshape(n, HEAD, 8, 128)


def _scan_fwd(xes, vi):
    T, L = vi.shape[0], SCAN_L
    nch = T // L

    def body(*refs):
        xr, (vi_ref, yi_ref, sa_ref, ck_ref, st_ref, e0, e1) = refs[:5], refs[5:]

        @pl.when(pl.program_id(0) == 0)
        def _():
            st_ref[...] = jnp.zeros_like(st_ref)

        ck_ref[0] = st_ref[...]
        expand = _expander(xr)
        expand(0, e0)

        def step(t, e_ref):
            v = vi_ref[t]
            row = lambda m, j: jnp.broadcast_to(e_ref[m, pl.ds(j, 1), :], (8, 128))
            S = [st_ref[j] for j in range(HEAD)]
            sa = _tree_sum([S[j] * row(0, j) for j in range(HEAD)])
            sa_ref[t] = sa
            S = [S[j] * row(1, j) + row(2, j) * sa + row(3, j) * v for j in range(HEAD)]
            for j in range(HEAD):
                st_ref[j] = S[j]
            yi_ref[t] = _tree_sum([S[j] * row(4, j) for j in range(HEAD)])

        def pair(p, carry):
            t = 2 * p
            expand(t + 1, e1)
            step(t, e0)
            expand(jnp.minimum(t + 2, L - 1), e0)
            step(t + 1, e1)
            return carry

        lax.fori_loop(0, L // 2, pair, 0)

    tile = pl.BlockSpec((L, 8, 128), lambda c: (c, 0, 0))
    return pl.pallas_call(
        body, name="rwkv_scan_fwd", grid=(nch,), in_specs=[tile] * 6,
        out_specs=[tile, tile, pl.BlockSpec((1, HEAD, 8, 128), lambda c: (c, 0, 0, 0))],
        out_shape=[jax.ShapeDtypeStruct((T, 8, 128), F32)] * 2 + [jax.ShapeDtypeStruct((nch, HEAD, 8, 128), F32)],
        scratch_shapes=[pltpu.VMEM((HEAD, 8, 128), F32)] + [pltpu.VMEM((5, HEAD, 128), F32)] * 2, compiler_params=_cp(("arbitrary",)),
    )(*xes, vi)


def _scan_bwd_a(xes, dyi):
    T, L = dyi.shape[0], SCAN_L
    nch = T // L

    def body(*refs):
        xr, (dy_ref, dsa_ref, dv_ref, g_ref, e0, e1) = refs[:5], refs[5:]

        @pl.when(pl.program_id(0) == 0)
        def _():
            g_ref[...] = jnp.zeros_like(g_ref)

        expand = _expander(xr)
        expand(L - 1, e0)

        def step(t, e_ref):
            dy = dy_ref[t]
            row = lambda m, j: jnp.broadcast_to(e_ref[m, pl.ds(j, 1), :], (8, 128))
            G = [g_ref[j] + row(4, j) * dy for j in range(HEAD)]
            dsa = _tree_sum([G[j] * row(2, j) for j in range(HEAD)])
            dsa_ref[t] = dsa
            dv_ref[t] = _tree_sum([G[j] * row(3, j) for j in range(HEAD)])
            for j in range(HEAD):
                g_ref[j] = G[j] * row(1, j) + row(0, j) * dsa

        def pair(p, carry):
            t = L - 1 - 2 * p
            expand(t - 1, e1)
            step(t, e0)
            expand(jnp.maximum(t - 2, 0), e0)
            step(t - 1, e1)
            return carry

        lax.fori_loop(0, L // 2, pair, 0)

    tile = pl.BlockSpec((L, 8, 128), lambda c: (nch - 1 - c, 0, 0))
    return pl.pallas_call(
        body, name="rwkv_scan_bwd_a", grid=(nch,), in_specs=[tile] * 6, out_specs=[tile, tile],
        out_shape=[jax.ShapeDtypeStruct((T, 8, 128), F32)] * 2,
        scratch_shapes=[pltpu.VMEM((HEAD, 8, 128), F32)] + [pltpu.VMEM((5, HEAD, 128), F32)] * 2, compiler_params=_cp(("arbitrary",)),
    )(*xes, dyi)


def _scan_bwd_b(xts, ies, ckb):
    T, L = xts[0].shape[0], SCAN_L
    nch = T // L

    def body(*refs):
        xr, er, ck_ref, dj, (hist, g_ref, e0, e1) = refs[:5], refs[5:9], refs[9], refs[10:15], refs[15:]

        @pl.when(pl.program_id(0) == 0)
        def _():
            g_ref[...] = jnp.zeros_like(g_ref)

        hist[0] = ck_ref[0]
        expand_vs = _expander(er[:2])
        expand = _expander(er)
        expand_vs(0, e0)

        def fstep(t, e_ref):
            w, B, k = xr[1][t], xr[2][t], xr[3][t]
            row = lambda m, i: jnp.broadcast_to(e_ref[m, pl.ds(i, 1), :], (8, 128))
            for i in range(HEAD):
                hist[t + 1, i] = hist[t, i] * w + row(1, i) * B + row(0, i) * k

        def fpair(p, carry):
            t = 2 * p
            expand_vs(t + 1, e1)
            fstep(t, e0)
            expand_vs(jnp.minimum(t + 2, L - 1), e0)
            fstep(t + 1, e1)
            return carry

        lax.fori_loop(0, L // 2, fpair, 0)
        expand(L - 1, e0)

        def bstep(t, e_ref):
            A, w, r = xr[0][t], xr[1][t], xr[4][t]
            row = lambda m, i: jnp.broadcast_to(e_ref[m, pl.ds(i, 1), :], (8, 128))
            G = [g_ref[i] + row(2, i) * r for i in range(HEAD)]
            Sp = [hist[t, i] for i in range(HEAD)]
            dj[4][t] = _tree_sum([hist[t + 1, i] * row(2, i) for i in range(HEAD)])
            dj[1][t] = _tree_sum([G[i] * Sp[i] for i in range(HEAD)])
            dj[2][t] = _tree_sum([G[i] * row(1, i) for i in range(HEAD)])
            dj[3][t] = _tree_sum([G[i] * row(0, i) for i in range(HEAD)])
            dj[0][t] = _tree_sum([Sp[i] * row(3, i) for i in range(HEAD)])
            for i in range(HEAD):
                g_ref[i] = G[i] * w + row(3, i) * A

        def bpair(p, carry):
            t = L - 1 - 2 * p
            expand(t - 1, e1)
            bstep(t, e0)
            expand(jnp.maximum(t - 2, 0), e0)
            bstep(t - 1, e1)
            return carry

        lax.fori_loop(0, L // 2, bpair, 0)

    tile = pl.BlockSpec((L, 8, 128), lambda c: (nch - 1 - c, 0, 0))
    return pl.pallas_call(
        body, name="rwkv_scan_bwd_b", grid=(nch,),
        in_specs=[tile] * 9 + [pl.BlockSpec((1, HEAD, 8, 128), lambda c: (nch - 1 - c, 0, 0, 0))],
        out_specs=[tile] * 5, out_shape=[jax.ShapeDtypeStruct((T, 8, 128), F32)] * 5,
        scratch_shapes=[pltpu.VMEM((L + 1, HEAD, 8, 128), F32), pltpu.VMEM((HEAD, 8, 128), F32)] + [pltpu.VMEM((4, HEAD, 128), F32)] * 2,
        compiler_params=_cp(("arbitrary",)),
    )(*xts, *ies, ckb)


SWA_COLS = SWA_W + 2 * KV_W
BLK = 128


def _swa_core(n, k2a, k2b, vla, vra, vlb, vrb, sinks, *qps):
    iq = lax.broadcasted_iota(jnp.int32, (BLK, 2 * BLK), 0)
    ik = lax.broadcasted_iota(jnp.int32, (BLK, 2 * BLK), 1)
    diff = BLK + iq - ik
    valid = (diff >= 0) & (diff < WINDOW) & ((n > 0) | (ik >= BLK))
    lane = lax.broadcasted_iota(jnp.int32, (BLK, 128), 1)
    lane1 = lax.broadcasted_iota(jnp.int32, (1, 128), 1)
    nt = (((1,), (1,)), ((), ()))
    outs = []
    for pp in range(8):
        k2, vl, vr = (k2a, vla, vra) if pp < 4 else (k2b, vlb, vrb)
        qp = qps[pp]
        o = None
        for half, vv in ((0, vl), (1, vr)):
            qh = jnp.where((lane >= HEAD) == (half == 1), qp, 0.0).astype(BF16)
            s = lax.dot_general(qh, k2.astype(BF16), nt, preferred_element_type=F32) * (HEAD ** -0.5)
            s = jnp.where(valid, s, NEG_INF)
            sink = jnp.sum(jnp.where(lane1 == 2 * pp + half, sinks, 0.0), axis=1, keepdims=True)
            m = jnp.maximum(jnp.max(s, axis=1, keepdims=True), sink)
            p = jnp.exp(s - m)
            den = jnp.sum(p, axis=1, keepdims=True) + jnp.exp(sink - m)
            oh = jnp.dot((p / den).astype(BF16), vv.astype(BF16), preferred_element_type=F32)
            o = oh if o is None else o + oh
        outs.append(o)
    return jnp.concatenate(outs, axis=1)


def _swa_prep(pc, pp, b, cq, sq, ckc, skc, ckp, skp):
    zc, zp = pc + b, pp + b
    qr = zc[:, :SWA_W] * cq + _swap32(zc[:, :SWA_W]) * sq
    kc, kp = zc[:, SWA_W:SWA_W + KV_W], zp[:, SWA_W:SWA_W + KV_W]
    kb = jnp.concatenate([kp * ckp + _swap32(kp) * skp, kc * ckc + _swap32(kc) * skc], axis=0)
    vb = jnp.concatenate([zp[:, SWA_W + KV_W:], zc[:, SWA_W + KV_W:]], axis=0)
    lane = lax.broadcasted_iota(jnp.int32, kb.shape, 1)
    left = lane < HEAD
    kbr, vbr = pltpu.roll(kb, HEAD, 1), pltpu.roll(vb, HEAD, 1)
    return (jnp.where(left, kb, kbr), jnp.where(left, kbr, kb), jnp.where(left, vb, 0.0), jnp.where(left, 0.0, vbr),
            jnp.where(left, vbr, 0.0), jnp.where(left, 0.0, vb)), [qr[:, q * 128:(q + 1) * 128] for q in range(8)]


def _swa_specs(T, tabs_q, tabs_k):
    cur = lambda c: pl.BlockSpec((BLK, c), lambda n: (n, 0))
    prev = lambda c: pl.BlockSpec((BLK, c), lambda n: (jnp.maximum(n - 1, 0), 0))
    return cur, prev


def _swa_fwd(p_swa, b, sinks, cq, sq, ck, sk):
    T = p_swa.shape[0]
    cur, prev = _swa_specs(T, None, None)

    def body(pc, pp, b_ref, s_ref, cq_r, sq_r, ckc, skc, ckp, skp, o_ref):
        ops, qps = _swa_prep(pc[...], pp[...], b_ref[...], cq_r[...], sq_r[...], ckc[...], skc[...], ckp[...], skp[...])
        o_ref[...] = _swa_core(pl.program_id(0), *ops, s_ref[...], *qps).astype(o_ref.dtype)

    full = lambda a: pl.BlockSpec(a.shape, lambda n: (0, 0))
    return pl.pallas_call(
        body, name="swa_fwd", grid=(T // BLK,),
        in_specs=[cur(SWA_COLS), prev(SWA_COLS), full(b), full(sinks), cur(SWA_W), cur(SWA_W), cur(KV_W), cur(KV_W), prev(KV_W), prev(KV_W)],
        out_specs=cur(SWA_W), out_shape=jax.ShapeDtypeStruct((T, SWA_W), BF16), compiler_params=_cp(("arbitrary",)),
    )(p_swa, p_swa, b, sinks, cq, sq, ck, sk, ck, sk)


def _swa_bwd(p_swa, b, sinks, cq, sq, ck, sk, do):
    T = p_swa.shape[0]
    nb = T // BLK
    cur = lambda c: pl.BlockSpec((BLK, c), lambda s: (nb - 1 - s, 0))
    prev = lambda c: pl.BlockSpec((BLK, c), lambda s: (jnp.maximum(nb - 2 - s, 0), 0))

    def body(pc, pp, b_ref, s_ref, cq_r, sq_r, ckc, skc, ckp, skp, do_ref, dcur, db, dsk, carry):
        step = pl.program_id(0)
        n = nb - 1 - step

        @pl.when(step == 0)
        def _():
            carry[...] = jnp.zeros_like(carry)
            db[...] = jnp.zeros_like(db)
            dsk[...] = jnp.zeros_like(dsk)

        ops, qps = _swa_prep(pc[...], pp[...], b_ref[...], cq_r[...], sq_r[...], ckc[...], skc[...], ckp[...], skp[...])
        _, vjp = jax.vjp(functools.partial(_swa_core, n), *ops, s_ref[...], *qps)
        dk2a, dk2b, dvla, dvra, dvlb, dvrb, dsinks, *dqps = vjp(do_ref[...].astype(F32))
        dqr = jnp.concatenate(dqps, axis=1)
        lane = lax.broadcasted_iota(jnp.int32, dk2a.shape, 1)
        left = lane < HEAD
        dkb = jnp.where(left, dk2a + pltpu.roll(dk2a, HEAD, 1), dk2b + pltpu.roll(dk2b, HEAD, 1))
        dvb = jnp.where(left, dvla + pltpu.roll(dvra, HEAD, 1), pltpu.roll(dvlb, HEAD, 1) + dvrb)
        dq = dqr * cq_r[...] + _swap32(dqr * sq_r[...])
        dkp, dkc = dkb[:BLK], dkb[BLK:]
        dkp = dkp * ckp[...] + _swap32(dkp * skp[...])
        dkc = dkc * ckc[...] + _swap32(dkc * skc[...])
        dc = jnp.concatenate([dq, jnp.concatenate([dkc, dvb[BLK:]], axis=1) + carry[...]], axis=1)
        carry[...] = jnp.concatenate([dkp, dvb[:BLK]], axis=1)
        dcur[...] = dc.astype(dcur.dtype)
        db[...] += jnp.sum(dc, axis=0, keepdims=True)
        dsk[...] += dsinks

    full = lambda a: pl.BlockSpec(a.shape, lambda s: (0, 0))
    return pl.pallas_call(
        body, name="swa_bwd", grid=(nb,),
        in_specs=[cur(SWA_COLS), prev(SWA_COLS), full(b), full(sinks), cur(SWA_W), cur(SWA_W), cur(KV_W), cur(KV_W), prev(KV_W), prev(KV_W),
                  cur(SWA_W)],
        out_specs=[cur(SWA_COLS), full(b), full(sinks)],
        out_shape=[jax.ShapeDtypeStruct((T, SWA_COLS), BF16), jax.ShapeDtypeStruct(b.shape, F32), jax.ShapeDtypeStruct(sinks.shape, F32)],
        scratch_shapes=[pltpu.VMEM((BLK, 2 * KV_W), F32)], compiler_params=_cp(("arbitrary",)),
    )(p_swa, p_swa, b, sinks, cq, sq, ck, sk, ck, sk, do)


def _rope_tables(T):
    inv = 10000.0 ** (-jnp.arange(0, HEAD, 2, dtype=F32) / HEAD)
    ang = jnp.arange(T, dtype=F32)[:, None] * inv[None, :]
    c = jnp.concatenate([jnp.cos(ang), jnp.cos(ang)], axis=1)
    s = jnp.concatenate([-jnp.sin(ang), jnp.sin(ang)], axis=1)
    return jnp.tile(c, (1, 16)), jnp.tile(s, (1, 16)), jnp.tile(c, (1, 2)), jnp.tile(s, (1, 2))


def _xattn_core(*qkv):
    outs = []
    for h in range(XH):
        qh, kh, vh = qkv[h], qkv[XH + h], qkv[2 * XH + h]
        s = lax.dot_general(qh.astype(BF16), kh.astype(BF16), (((1,), (1,)), ((), ())), preferred_element_type=F32) * (XHD ** -0.5)
        p = jnp.exp(s - jnp.max(s, axis=1, keepdims=True))
        p = p / jnp.sum(p, axis=1, keepdims=True)
        outs.append(jnp.dot(p.astype(BF16), vh.astype(BF16), preferred_element_type=F32))
    return jnp.concatenate(outs, axis=1)


def _xattn_split(q, kv):
    return [q[:, h * XHD:(h + 1) * XHD] for h in range(XH)] + [kv[:, h * XHD:(h + 1) * XHD] for h in range(2 * XH)]


def _xattn_fwd(q, kv, tm=256):
    (o,) = _rows(lambda i, q, kv: (_xattn_core(*_xattn_split(q, kv)),), "xattn_fwd", q.shape[0], tm, [q], [kv], [(q.shape[1], BF16)], [])
    return o


def _xattn_bwd(q, kv, do, tm=256):
    def fn(i, q, do, kv):
        _, vjp = jax.vjp(_xattn_core, *_xattn_split(q, kv))
        d = vjp(do.astype(F32))
        return jnp.concatenate(d[:XH], axis=1), jnp.concatenate(d[XH:], axis=1)

    return _rows(fn, "xattn_bwd", q.shape[0], tm, [q, do], [kv], [(q.shape[1], BF16)], [(kv.shape, F32)])


def _loss_head(x, g, tgt, tm=256):
    D = x.shape[1]

    def fn(i, x, tgt, g):
        y, vjp = jax.vjp(_rms, x, g)
        err = y - tgt
        dx, dg = vjp(err * (1.0 / D))
        part = 0.5 / D * jnp.sum(jnp.sum(err * err, axis=1, keepdims=True), axis=0, keepdims=True)
        return dx, jnp.broadcast_to(part, (1, 128)), dg

    return _rows(fn, "loss_head", x.shape[0], tm, [x, tgt], [g], [(D, F32)], [((1, 128), F32), ((1, D), F32)])


def _local_step(x, mem, tgt, get_w, P, put_g):
    T = x.shape[0]
    W = dict(get_w("f1", None))
    x1, s1 = _ffn_fwd(x, P["f1_norm"], W["f1_gate"], W["f1_up"], W["f1_down"], "f1", after=W.get("_after"))

    W.update(get_w("mix", x1))
    h2 = _rms_fwd(x1, P["mix_norm"], "mix_norm")
    w_rkv, w_lora, w_swa = W["w_inT"][:3 * RW_W], W["w_inT"][3 * RW_W:SHIFT_COLS], W["w_inT"][SHIFT_COLS:]
    p_rkv = _mm(h2, w_rkv, "nt", "in_rkv")
    p_lora = _mm(h2, w_lora, "nt", "in_lora")
    p_swa = _mm(h2, w_swa, "nt", "in_swa")
    w_da = jnp.concatenate([W["rw_decay_up"], W["rw_aaa_up"]], axis=0)
    pre_params = (P["rw_mu"][:, :3 * RW_W], P["rw_mu"][:, 3 * RW_W:], P["rw_w0"], P["rw_a0"], P["rw_k_k"], P["rw_k_a"], w_da,
                  W["rw_gate_up"])
    r, decay, k2, v, an, bn, g = _rwkv_pre(p_rkv, p_lora, pre_params)
    scan_vecs = (an, decay, bn, k2, r)
    xes = [_to_perm(a) for a in scan_vecs]
    yi, sai, ck = _scan_fwd(xes, _to_tile(v))
    y_scan = _from_tile(yi)
    y_rw = _rwkv_post(y_scan, r, k2, v, g, P["rw_lnx_w"], P["rw_lnx_b"], P["rw_r_k"])
    cq, sq, ckt, skt = _rope_tables(T)
    y_swa = _swa_fwd(p_swa, P["b_in_attn"], P["attn_sinks"], cq, sq, ckt, skt)
    ycat = jnp.concatenate([y_rw, y_swa], axis=1)
    x2 = _mm(ycat, W["w_out"], "nn", "out_proj", res=x1, bias=P["b_out"])

    W.update(get_w("xattn", x2))
    hx = _rms_fwd(x2, P["xa_norm"], "xa_norm")
    mn = _rms_fwd(mem, P["mem_norm"], "mem_norm")
    q = _mm(hx, W["w_xq"], "nn", "xq", out_dtype=BF16)
    kv = _mm(mn, W["w_xkv"], "nn", "xkv", out_dtype=BF16)
    o = _xattn_fwd(q, kv)
    x3 = _mm(o, W["w_xo"], "nn", "xo", res=x2)

    W.update(get_w("f2", x3))
    x4, s2 = _ffn_fwd(x3, P["f2_norm"], W["f2_gate"], W["f2_up"], W["f2_down"], "f2")
    dx4, loss_part, d_final = _loss_head(x4, P["final_norm"], tgt)

    gs = {"final_norm": d_final}
    dx3, gs["f2_norm"] = _ffn_bwd(x3, P["f2_norm"], W["f2_gate"], W["f2_up"], W["f2_down"], s2, dx4, "f2", put_g)

    do = _mm(dx3, W["w_xo"], "nt", "xo_do", out_dtype=BF16)
    dw_xo = _mm(o, dx3, "tn", "xo_dw", out_dtype=BF16)
    dq, dkv = _xattn_bwd(q, kv, do)
    dw_xq = _mm(hx, dq, "tn", "xq_dw", out_dtype=BF16)
    dw_xkv = _mm(mn, dkv, "tn", "xkv_dw", out_dtype=BF16)
    sent = put_g("xattn", {"w_xq": dw_xq, "w_xkv": dw_xkv, "w_xo": dw_xo})
    dhx = _mm(dq, W["w_xq"], "nt", "xq_dh", after=sent)
    dmn = _mm(dkv, W["w_xkv"], "nt", "xkv_dmn")
    _, gs["mem_norm"], _ = _rms_bwd(mem, P["mem_norm"], dmn, jnp.zeros_like(mem), "mem_norm_bwd")
    dx2, gs["xa_norm"], gs["b_out"] = _rms_bwd(x2, P["xa_norm"], dhx, dx3, "xa_norm_bwd")

    dycat = _mm(dx2, W["w_out"], "nt", "out_dy")
    dw_out = _mm(ycat, dx2, "tn", "out_dw", out_dtype=BF16)
    dp_swa, gs["b_in_attn"], gs["attn_sinks"] = _swa_bwd(p_swa, P["b_in_attn"], P["attn_sinks"], cq, sq, ckt, skt, dycat[:, RW_W:])
    dy_scan, dr_b, dk2_b, dv_b, dg, gs["rw_lnx_w"], gs["rw_lnx_b"], gs["rw_r_k"] = _rwkv_post_bwd(
        y_scan, r, k2, v, g, P["rw_lnx_w"], P["rw_lnx_b"], P["rw_r_k"], dycat[:, :RW_W])
    dsai, dvi = _scan_bwd_a(xes, _to_tile(dy_scan))
    ies = [_to_perm(v), _to_perm(_from_tile(sai)), _to_perm(dy_scan), _to_perm(_from_tile(dsai))]
    dj = _scan_bwd_b([_to_tile(a) for a in scan_vecs], ies, _ck_a_to_b(ck))
    dan, ddecay, dbn, dk2_s, dr_s = (_from_tile(d) for d in dj)
    cts = (dr_s, ddecay, dk2_s, _from_tile(dvi), dan, dbn, dg, dr_b, dk2_b, dv_b)
    dp_rkv, dp_lora, dmu, dmul, gs["rw_w0"], gs["rw_a0"], gs["rw_k_k"], gs["rw_k_a"], dw_da, gs["rw_gate_up"] = _rwkv_pre_bwd(
        p_rkv, p_lora, pre_params, cts)
    gs["rw_mu"] = jnp.concatenate([dmu, dmul], axis=1)
    gs["rw_decay_up"], gs["rw_aaa_up"] = dw_da[:DECAY_LORA], dw_da[DECAY_LORA:]
    dw_inT = jnp.concatenate([_mm(dp_rkv, h2, "tn", "in_dw_rkv"), _mm(dp_lora, h2, "tn", "in_dw_lora"),
                              _mm(dp_swa, h2, "tn", "in_dw_swa")], axis=0)
    sent = put_g("mix", {"w_in": dw_inT, "w_out": dw_out})
    dh2 = _mm(dp_rkv, w_rkv, "nn", "in_dh_rkv", after=sent)
    dh2 = _mm(dp_lora, w_lora, "nn", "in_dh_lora", res=dh2)
    dh2 = _mm(dp_swa, w_swa, "nn", "in_dh_swa", res=dh2)
    dx1, gs["mix_norm"], _ = _rms_bwd(x1, P["mix_norm"], dh2, dx2, "mix_norm_bwd")

    dx0, gs["f1_norm"] = _ffn_bwd(x, P["f1_norm"], W["f1_gate"], W["f1_up"], W["f1_down"], s1, dx1, "f1", put_g)
    return loss_part, dx0, gs


_ANY = pl.BlockSpec(memory_space=pl.ANY)
_OTHER_CHIPS = ((1, 0), (0, 1), (1, 1))


def _mesh_pos():
    return lax.axis_index("x"), lax.axis_index("y"), lax.axis_index("c")


def _slot(ref, kind, s, rows, cols):
    if kind == "row":
        return ref.at[pl.ds(pl.multiple_of(s * rows, 8), rows), :]
    return ref.at[:, pl.ds(pl.multiple_of(s * cols, 128), cols)]


_HBM = pl.BlockSpec(memory_space=pltpu.HBM)
_SEMS = pl.BlockSpec(memory_space=pltpu.SEMAPHORE)
_SPLIT = dict(compiler_params=pltpu.CompilerParams(has_side_effects=pltpu.SideEffectType.DATAFLOW_SIDE_EFFECTING))


def _in_hbm(a):
    return pltpu.with_memory_space_constraint(a, pltpu.HBM)


def _full_shape(s, kind):
    return (4 * s.shape[0], s.shape[1]) if kind == "row" else (s.shape[0], 4 * s.shape[1])


def _half(ref, shape, h):
    rows, cols = shape
    if rows % 32 == 0:
        return ref.at[pl.ds(pl.multiple_of(h * (rows // 2), 16), rows // 2), :]
    assert cols % 256 == 0, shape
    return ref.at[:, pl.ds(pl.multiple_of(h * (cols // 2), 128), cols // 2)]


def _swap_halves(name, fulls, shard_shapes, kinds):
    n = len(fulls)

    def body(*refs):
        out, send, recv = refs[n:2 * n], refs[2 * n], refs[2 * n + 1]
        x, y, c = _mesh_pos()
        sent = []
        for i in range(n):
            for r, (dx, dy) in enumerate(_OTHER_CHIPS):
                theirs = _slot(out[i], kinds[i], 2 * ((x + dx) % 2) + (y + dy) % 2, *shard_shapes[i])
                have = _half(theirs, shard_shapes[i], c)
                rc = pltpu.make_async_remote_copy(have, have, send.at[3 * i + r], recv.at[3 * i + r], device_id=(x, y, 1 - c),
                                                  device_id_type=MESH)
                rc.start()
                sent.append(rc)
        for i in range(n):
            for r, (dx, dy) in enumerate(_OTHER_CHIPS):
                theirs = _slot(out[i], kinds[i], 2 * ((x + dx) % 2) + (y + dy) % 2, *shard_shapes[i])
                need = _half(theirs, shard_shapes[i], 1 - c)
                pltpu.make_async_remote_copy(need, need, send.at[3 * i + r], recv.at[3 * i + r], device_id=(x, y, c),
                                             device_id_type=MESH).wait_recv()
        for rc in sent:
            rc.wait_send()

    return pl.pallas_call(
        body, name=name, in_specs=[_ANY] * n, out_specs=[_ANY] * n, out_shape=[jax.ShapeDtypeStruct(f.shape, f.dtype) for f in fulls],
        input_output_aliases={i: i for i in range(n)},
        scratch_shapes=[pltpu.SemaphoreType.DMA((3 * n,)), pltpu.SemaphoreType.DMA((3 * n,))],
    )(*fulls)


def _gather_now(name, shards, kinds):
    n = len(shards)

    def body(*refs):
        src, out = refs[:n], refs[n:2 * n]
        ici_send, ici_recv, d2d_send, d2d_recv, loc = refs[2 * n:]
        x, y, c = _mesh_pos()
        me = 2 * x + y
        half = lambda ref, rows, h: ref.at[pl.ds(pl.multiple_of(h * (rows // 2), 8), rows // 2), :]
        own, sent = [], []
        for i in range(n):
            rows, cols = src[i].shape
            mine = _slot(out[i], kinds[i], me, rows, cols)
            cp = pltpu.make_async_copy(src[i], mine, loc.at[i])
            cp.start()
            own.append(cp)
            for r, (dx, dy) in enumerate(_OTHER_CHIPS):
                rc = pltpu.make_async_remote_copy(half(src[i], rows, c), half(mine, rows, c), ici_send.at[3 * i + r], ici_recv.at[3 * i + r],
                                                  device_id=((x + dx) % 2, (y + dy) % 2, c), device_id_type=MESH)
                rc.start()
                sent.append(rc)
        for i in range(n):
            rows, cols = src[i].shape
            for r, (dx, dy) in enumerate(_OTHER_CHIPS):
                theirs = _slot(out[i], kinds[i], 2 * ((x + dx) % 2) + (y + dy) % 2, rows, cols)
                landed = half(theirs, rows, c)
                pltpu.make_async_remote_copy(landed, landed, ici_send.at[3 * i + r], ici_recv.at[3 * i + r],
                                             device_id=(x, y, c), device_id_type=MESH).wait_recv()
                fw = pltpu.make_async_remote_copy(landed, landed, d2d_send.at[3 * i + r], d2d_recv.at[3 * i + r],
                                                  device_id=(x, y, 1 - c), device_id_type=MESH)
                fw.start()
                sent.append(fw)
        for i in range(n):
            rows, cols = src[i].shape
            for r, (dx, dy) in enumerate(_OTHER_CHIPS):
                other = half(_slot(out[i], kinds[i], 2 * ((x + dx) % 2) + (y + dy) % 2, rows, cols), rows, 1 - c)
                pltpu.make_async_remote_copy(other, other, d2d_send.at[3 * i + r], d2d_recv.at[3 * i + r],
                                             device_id=(x, y, c), device_id_type=MESH).wait_recv()
        for cp in sent:
            cp.wait_send()
        for cp in own:
            cp.wait()

    return pl.pallas_call(
        body, name=name, in_specs=[_ANY] * n, out_specs=[_ANY] * n,
        out_shape=[jax.ShapeDtypeStruct(_full_shape(s, k), s.dtype) for s, k in zip(shards, kinds)],
        scratch_shapes=[pltpu.SemaphoreType.DMA((3 * n,))] * 4 + [pltpu.SemaphoreType.DMA((n,))],
    )(*shards)


def _gather_start(name, shards, kinds, groups, after=None):
    n, ng = len(shards), len(groups)
    lands = [_in_hbm(lax.empty(_full_shape(s, k), s.dtype)) for s, k in zip(shards, kinds)]
    n_in = 2 * n + (after is not None)

    def body(*refs):
        src, land, sems, token = refs[:n], refs[n:2 * n], refs[n_in:n_in + 3 * ng], refs[-1]
        x, y, c = _mesh_pos()
        me = 2 * x + y
        for gi, idxs in enumerate(groups):
            send, recv, own = sems[3 * gi:3 * gi + 3]
            for k, i in enumerate(idxs):
                mine = _slot(land[i], kinds[i], me, *src[i].shape)
                for r, (dx, dy) in enumerate(_OTHER_CHIPS):
                    pltpu.make_async_remote_copy(_half(src[i], src[i].shape, c), _half(mine, src[i].shape, c), send.at[3 * k + r],
                                                 recv.at[3 * k + r], device_id=((x + dx) % 2, (y + dy) % 2, c), device_id_type=MESH).start()
                pltpu.make_async_copy(src[i], mine, own.at[k]).start()
        token[...] = jnp.zeros_like(token)

    sem_shapes = [pltpu.SemaphoreType.DMA((w * len(g),)) for g in groups for w in (3, 3, 1)]
    thru = [pltpu.HBM(a.shape, a.dtype) for a in (*shards, *lands)]
    res = pl.pallas_call(
        body, name=name, in_specs=[_HBM] * (2 * n) + [_ANY] * (after is not None),
        out_specs=[_SEMS] * (3 * ng) + [_HBM] * (2 * n) + [pl.BlockSpec(memory_space=pltpu.VMEM)],
        out_shape=sem_shapes + thru + [jax.ShapeDtypeStruct((8, 128), F32)],
        input_output_aliases={i: 3 * ng + i for i in range(2 * n)}, **_SPLIT,
    )(*[_in_hbm(s) for s in shards], *lands, *([] if after is None else [after]))
    return res[:3 * ng], res[3 * ng:3 * ng + n], res[3 * ng + n:3 * ng + 2 * n], res[-1]


def _gather_wait(name, sems, shards, lands, kinds, after):
    m = len(shards)

    def body(*refs):
        src, land, (send, recv, own) = refs[:m], refs[m:2 * m], refs[2 * m:2 * m + 3]
        x, y, c = _mesh_pos()
        me = 2 * x + y
        for k in range(m):
            mine = _slot(land[k], kinds[k], me, *src[k].shape)
            for r in range(3):
                cp = pltpu.make_async_remote_copy(_half(src[k], src[k].shape, c), _half(mine, src[k].shape, c), send.at[3 * k + r],
                                                  recv.at[3 * k + r], device_id=(x, y, c), device_id_type=MESH)
                cp.wait_send()
                cp.wait_recv()
            pltpu.make_async_copy(src[k], mine, own.at[k]).wait()

    thru = [pltpu.HBM(a.shape, a.dtype) for a in (*shards, *lands)]
    res = pl.pallas_call(
        body, name=name, in_specs=[_HBM] * (2 * m) + [_SEMS] * 3 + [pl.BlockSpec(memory_space=pl.ANY)],
        out_specs=[_HBM] * (2 * m), out_shape=thru, input_output_aliases={i: i for i in range(2 * m)}, **_SPLIT,
    )(*shards, *lands, *sems, after)
    return res[m:]


def _scatter_start(name, grads, kinds):
    m = len(grads)
    shard_shape = [(g.shape[0] // 4, g.shape[1]) if k == "row" else (g.shape[0], g.shape[1] // 4) for g, k in zip(grads, kinds)]
    lands = [_in_hbm(lax.empty((4, *s), g.dtype)) for s, g in zip(shard_shape, grads)]

    def body(*refs):
        src, land, (send, recv, own) = refs[:m], refs[m:2 * m], refs[2 * m:2 * m + 3]
        x, y, c = _mesh_pos()
        me = 2 * x + y
        for k in range(m):
            for r, (dx, dy) in enumerate(_OTHER_CHIPS):
                tx, ty = (x + dx) % 2, (y + dy) % 2
                pltpu.make_async_remote_copy(_slot(src[k], kinds[k], 2 * tx + ty, *shard_shape[k]), land[k].at[me],
                                             send.at[3 * k + r], recv.at[3 * k + r], device_id=(tx, ty, c), device_id_type=MESH).start()
            pltpu.make_async_copy(_slot(src[k], kinds[k], me, *shard_shape[k]), land[k].at[me], own.at[k]).start()
        refs[-1][...] = jnp.zeros_like(refs[-1])

    thru = [pltpu.HBM(a.shape, a.dtype) for a in (*grads, *lands)]
    res = pl.pallas_call(
        body, name=name, in_specs=[_HBM] * (2 * m),
        out_specs=[_SEMS] * 3 + [_HBM] * (2 * m) + [pl.BlockSpec(memory_space=pltpu.VMEM)],
        out_shape=[pltpu.SemaphoreType.DMA((3 * m,))] * 2 + [pltpu.SemaphoreType.DMA((m,))] + thru + [jax.ShapeDtypeStruct((8, 128), F32)],
        input_output_aliases={i: 3 + i for i in range(2 * m)}, **_SPLIT,
    )(*[_in_hbm(g) for g in grads], *lands)
    return res[:3], res[3:3 + m], res[3 + m:3 + 2 * m], res[-1]


def _scatter_wait(name, sems, grads, lands, kinds, after):
    m = len(grads)

    def body(*refs):
        src, land, (send, recv, own) = refs[:m], refs[m:2 * m], refs[2 * m:2 * m + 3]
        x, y, c = _mesh_pos()
        me = 2 * x + y
        for k in range(m):
            mine = _slot(src[k], kinds[k], me, *land[k].shape[1:])
            for r in range(3):
                cp = pltpu.make_async_remote_copy(mine, land[k].at[me], send.at[3 * k + r], recv.at[3 * k + r],
                                                  device_id=(x, y, c), device_id_type=MESH)
                cp.wait_send()
                cp.wait_recv()
            pltpu.make_async_copy(mine, land[k].at[me], own.at[k]).wait()

    thru = [pltpu.HBM(a.shape, a.dtype) for a in (*grads, *lands)]
    res = pl.pallas_call(
        body, name=name, in_specs=[_HBM] * (2 * m) + [_SEMS] * 3 + [pl.BlockSpec(memory_space=pl.ANY)],
        out_specs=[_HBM] * (2 * m), out_shape=thru, input_output_aliases={i: i for i in range(2 * m)}, **_SPLIT,
    )(*grads, *lands, *sems, after)
    return res[m:]


def _swap_with_sibling(arrs, name):
    n = len(arrs)

    def body(*refs):
        ins, outs = refs[:n], refs[n:2 * n]
        send, recv = refs[2 * n:]
        x, y, c = _mesh_pos()
        copies = []
        for i in range(n):
            rc = pltpu.make_async_remote_copy(ins[i], outs[i], send.at[i], recv.at[i], device_id=(x, y, 1 - c), device_id_type=MESH)
            rc.start()
            copies.append(rc)
        for rc in copies:
            rc.wait()

    return pl.pallas_call(
        body, name=name, in_specs=[_ANY] * n, out_specs=[_ANY] * n,
        out_shape=[jax.ShapeDtypeStruct(a.shape, a.dtype) for a in arrs],
        scratch_shapes=[pltpu.SemaphoreType.DMA((n,)), pltpu.SemaphoreType.DMA((n,))],
    )(*arrs)


def _small_start(pack, after):
    land = _in_hbm(lax.empty((8, *pack.shape), pack.dtype))

    def body(in_ref, land_ref, after_ref, send, recv, own, in_thru, land_thru, token):
        x, y, c = _mesh_pos()
        me = 4 * x + 2 * y + c
        for r in range(1, 8):
            dx, dy, dc = r // 4, (r // 2) % 2, r % 2
            pltpu.make_async_remote_copy(in_ref, land_ref.at[me], send.at[r - 1], recv.at[r - 1],
                                         device_id=((x + dx) % 2, (y + dy) % 2, (c + dc) % 2), device_id_type=MESH).start()
        pltpu.make_async_copy(in_ref, land_ref.at[me], own.at[0]).start()
        token[...] = jnp.zeros_like(token)

    res = pl.pallas_call(
        body, name="small_start", in_specs=[_HBM, _HBM, _ANY],
        out_specs=[_SEMS] * 3 + [_HBM, _HBM, pl.BlockSpec(memory_space=pltpu.VMEM)],
        out_shape=[pltpu.SemaphoreType.DMA((7,)), pltpu.SemaphoreType.DMA((7,)), pltpu.SemaphoreType.DMA((1,)),
                   pltpu.HBM(pack.shape, pack.dtype), pltpu.HBM(land.shape, land.dtype), jax.ShapeDtypeStruct((8, 128), F32)],
        input_output_aliases={0: 3, 1: 4}, **_SPLIT,
    )(_in_hbm(pack), land, after)
    return res[:3], res[3], res[4], res[5]


def _small_wait(sems, pack, land, after):
    def body(in_ref, land_ref, send, recv, own, after_ref, in_dead, got):
        x, y, c = _mesh_pos()
        me = 4 * x + 2 * y + c
        for r in range(1, 8):
            cp = pltpu.make_async_remote_copy(in_ref, land_ref.at[me], send.at[r - 1], recv.at[r - 1], device_id=(x, y, c),
                                              device_id_type=MESH)
            cp.wait_send()
            cp.wait_recv()
        pltpu.make_async_copy(in_ref, land_ref.at[me], own.at[0]).wait()

    res = pl.pallas_call(
        body, name="small_wait", in_specs=[_HBM, _HBM] + [_SEMS] * 3 + [_ANY], out_specs=[_HBM, _HBM],
        out_shape=[pltpu.HBM(pack.shape, pack.dtype), pltpu.HBM(land.shape, land.dtype)], input_output_aliases={0: 0, 1: 1}, **_SPLIT,
    )(pack, land, *sems, after)
    return res[1]


def _row_tile(R, dtype, target=256):
    mult = 8 * 4 // jnp.dtype(dtype).itemsize
    best = R
    for t in range(mult, min(R, target) + 1, mult):
        if R % t == 0:
            best = t
    return best


def _sum_slots(stack, name, out_dtype=F32):
    k, R, C = stack.shape
    tr = _row_tile(R, stack.dtype)

    def body(s_ref, o_ref):
        acc = s_ref[0].astype(F32)
        for j in range(1, k):
            acc = acc + s_ref[j].astype(F32)
        o_ref[...] = acc.astype(out_dtype)

    return pl.pallas_call(
        body, name=name, grid=(R // tr,), in_specs=[pl.BlockSpec((k, tr, C), lambda i: (0, i, 0))],
        out_specs=pl.BlockSpec((tr, C), lambda i: (i, 0)), out_shape=jax.ShapeDtypeStruct((R, C), out_dtype),
        compiler_params=_cp(("parallel",)),
    )(stack)


def _adamw(w, m, v, ga, gb, name, after=None):
    R, C = w.shape
    tr = _row_tile(R, F32, 128)
    gs = [ga] if gb is None else [ga, gb]
    extra = [] if after is None else [after]

    def body(*refs):
        w_ref, m_ref, v_ref = refs[:3]
        g = refs[3][...]
        if gb is not None:
            g = g + refs[4][...]
        g_ref, d_ref, nm_ref, nv_ref = refs[-4:]
        nm = ADAM_B1 * m_ref[...] + (1.0 - ADAM_B1) * g
        nv = ADAM_B2 * v_ref[...] + (1.0 - ADAM_B2) * (g * g)
        m_hat = nm / (1.0 - ADAM_B1 ** ADAM_STEP)
        v_hat = nv / (1.0 - ADAM_B2 ** ADAM_STEP)
        g_ref[...] = g
        d_ref[...] = -ADAM_LR * (m_hat / (jnp.sqrt(v_hat) + ADAM_EPS) + ADAM_WD * w_ref[...])
        nm_ref[...] = nm
        nv_ref[...] = nv

    spec = pl.BlockSpec((tr, C), lambda i: (i, 0))
    return pl.pallas_call(
        body, name=name, grid=(R // tr,), in_specs=[spec] * (3 + len(gs)) + [_ANY] * len(extra), out_specs=[spec] * 4,
        out_shape=[jax.ShapeDtypeStruct((R, C), F32)] * 4, compiler_params=_cp(("parallel",)),
    )(w, m, v, *gs, *extra)


def _pack(arrs):
    rows = []
    for a in arrs:
        flat = a.reshape(-1)
        rows.append(jnp.pad(flat, (0, -flat.shape[0] % 1024)).reshape(-1, 1024))
    p = jnp.concatenate(rows, axis=0)
    return jnp.pad(p, ((0, -p.shape[0] % 8), (0, 0)))


def _unpack(p, shapes):
    out, r = [], 0
    for s in shapes:
        n = 1
        for d in s:
            n *= d
        nr = -(-n // 1024)
        out.append(p[r:r + nr].reshape(-1)[:n].reshape(s))
        r += nr
    return out


BIG = ("f1_gate", "f1_up", "f1_down", "w_in", "w_out", "w_xq", "w_xkv", "w_xo", "f2_gate", "f2_up", "f2_down")
BIG_KIND = {"f1_gate": "col", "f1_up": "col", "f1_down": "row", "w_in": "row", "w_out": "row", "w_xq": "row", "w_xkv": "col",
            "w_xo": "row", "f2_gate": "col", "f2_up": "col", "f2_down": "row"}
LORA = ("rw_decay_up", "rw_aaa_up", "rw_gate_up")
WEIGHTS = ("f1_norm", "f1_gate", "f1_up", "f1_down", "mix_norm", "w_in", "b_in_attn", "rw_mu", "rw_w0", "rw_decay_up", "rw_a0",
           "rw_aaa_up", "rw_gate_up", "rw_k_k", "rw_k_a", "rw_r_k", "rw_lnx_w", "rw_lnx_b", "attn_sinks", "w_out", "b_out", "xa_norm",
           "mem_norm", "w_xq", "w_xkv", "w_xo", "f2_norm", "f2_gate", "f2_up", "f2_down", "final_norm")
SMALL = tuple(n for n in WEIGHTS if n not in BIG)
GROUP_ORDER = ("f1", "mix", "xattn", "f2")
GROUPS = {"f1": ("f1_gate", "f1_up", "f1_down"), "mix": ("w_in", "w_out") + LORA, "xattn": ("w_xq", "w_xkv", "w_xo"),
          "f2": ("f2_gate", "f2_up", "f2_down")}


def kernel(x, mem, f1_norm, f1_gate, f1_up, f1_down, mix_norm, w_in, b_in_attn, rw_mu, rw_w0, rw_decay_up, rw_a0, rw_aaa_up, rw_gate_up, rw_k_k, rw_k_a, rw_r_k, rw_lnx_w, rw_lnx_b, attn_sinks, w_out, b_out, xa_norm, mem_norm, w_xq, w_xkv, w_xo, f2_norm, f2_gate, f2_up, f2_down, final_norm, loss_target, m_f1_norm, m_f1_gate, m_f1_up, m_f1_down, m_mix_norm, m_w_in, m_b_in_attn, m_rw_mu, m_rw_w0, m_rw_decay_up, m_rw_a0, m_rw_aaa_up, m_rw_gate_up, m_rw_k_k, m_rw_k_a, m_rw_r_k, m_rw_lnx_w, m_rw_lnx_b, m_attn_sinks, m_w_out, m_b_out, m_xa_norm, m_mem_norm, m_w_xq, m_w_xkv, m_w_xo, m_f2_norm, m_f2_gate, m_f2_up, m_f2_down, m_final_norm, v_f1_norm, v_f1_gate, v_f1_up, v_f1_down, v_mix_norm, v_w_in, v_b_in_attn, v_rw_mu, v_rw_w0, v_rw_decay_up, v_rw_a0, v_rw_aaa_up, v_rw_gate_up, v_rw_k_k, v_rw_k_a, v_rw_r_k, v_rw_lnx_w, v_rw_lnx_b, v_attn_sinks, v_w_out, v_b_out, v_xa_norm, v_mem_norm, v_w_xq, v_w_xkv, v_w_xo, v_f2_norm, v_f2_gate, v_f2_up, v_f2_down, v_final_norm):
    a = dict(locals())
    w = {n: a[n] for n in WEIGHTS}
    m = {n: a["m_" + n] for n in WEIGHTS}
    v = {n: a["v_" + n] for n in WEIGHTS}
    sq = lambda t: t.reshape(t.shape[-2:]) if t.ndim == 3 else t.reshape(1, -1)

    local_name = lambda n: "w_inT" if n == "w_in" else n
    kind_of = lambda n: BIG_KIND.get(n, "col")
    payload = lambda n: sq(w[n]).T if n == "w_in" else sq(w[n]) if n in LORA else sq(w[n]).astype(BF16)
    gathers = {}

    def start_gather(name, grps, after):
        shards = [payload(n) for g in grps for n in GROUPS[g]]
        kinds = [kind_of(n) for g in grps for n in GROUPS[g]]
        groups, at = [], 0
        for g in grps:
            groups.append(list(range(at, at + len(GROUPS[g]))))
            at += len(GROUPS[g])
        sems, src_thru, land_thru, token = _gather_start(name, shards, kinds, groups, after)
        for gi, g in enumerate(grps):
            gathers[g] = (sems[3 * gi:3 * gi + 3], [src_thru[i] for i in groups[gi]], [land_thru[i] for i in groups[gi]],
                          [kinds[i] for i in groups[gi]], token)

    def get_w(grp, after):
        if grp == GROUP_ORDER[0]:
            got = _gather_now("gather_" + grp, [payload(n) for n in GROUPS[grp]], [kind_of(n) for n in GROUPS[grp]])
            start_gather("gather_start_rest", GROUP_ORDER[1:], got[0])
            out = {"_after": gathers[GROUP_ORDER[1]][4]}
        else:
            g_sems, g_src, g_land, g_kinds, _ = gathers[grp]
            got = _gather_wait("gather_wait_" + grp, g_sems, g_src, g_land, g_kinds, after)
            got = _swap_halves("gather_swap_" + grp, got, [s.shape for s in g_src], g_kinds)
            out = {}
        out.update({local_name(n): f for n, f in zip(GROUPS[grp], got)})
        return out

    in_flight = []

    def put_g(label, gw):
        names = list(gw)
        *flight, sent = _scatter_start("scatter_start_" + label, [gw[n] for n in names], [kind_of(n) for n in names])
        in_flight.append((label, names, flight))
        return sent

    P = {n: sq(w[n]) for n in SMALL if n not in LORA}
    P["attn_sinks"] = jnp.pad(P["attn_sinks"], ((0, 0), (0, 128 - P["attn_sinks"].shape[1])))
    P["rw_r_k"] = w["rw_r_k"].reshape(1, RW_W)
    loss_part, grad_x, gs = _local_step(x[0], mem[0], loss_target[0], get_w, P, put_g)
    loss = lax.psum(loss_part[0, 0], ("x", "y", "c"))

    gs["attn_sinks"] = gs["attn_sinks"][:, :16]
    small_flight = _small_start(_pack([gs[n] for n in SMALL]), grad_x)

    out, after = {}, small_flight[-1]
    for label, names, (g_sems, g_thru, l_thru) in in_flight:
        stacks = _scatter_wait("scatter_wait_" + label, g_sems, g_thru, l_thru, [kind_of(n) for n in names], after)
        partial = [_sum_slots(s, "sum_chips_" + n) for s, n in zip(stacks, names)]
        sibling = _swap_with_sibling(partial, "swap_" + label)
        chain = None
        for n, pa, sb in zip(names, partial, sibling):
            if n == "w_in":
                pa, sb = pa.T, sb.T
            out[n] = _adamw(sq(w[n]), sq(m[n]), sq(v[n]), pa, sb, "adamw_" + n, after=chain)
            chain = out[n][1]
        after = chain

    gsum = _sum_slots(_small_wait(*small_flight[:-1], after), "sum_small")
    g_small = dict(zip(SMALL, _unpack(gsum, [gs[n].shape for n in SMALL])))
    shard = 2 * lax.axis_index("x") + lax.axis_index("y")
    for n in LORA:
        cols = w[n].shape[-1]
        g_small[n] = lax.dynamic_slice_in_dim(g_small[n], shard * cols, cols, axis=1)

    flat = lambda d: _pack([d[n] for n in SMALL])
    res = _adamw(flat(w), flat(m), flat(v), _pack([g_small[n] for n in SMALL]), None, "adamw_small")
    shapes = [w[n].shape for n in SMALL]
    for k, p in enumerate(res):
        for n, t in zip(SMALL, _unpack(p, shapes)):
            out.setdefault(n, [None] * 4)[k] = t
    outs = [loss, grad_x.reshape(x.shape)]
    for k in range(4):
        outs += [out[n][k].reshape(w[n].shape) for n in WEIGHTS]
    return tuple(outs)
```

```python
import functools

import jax
import jax.numpy as jnp
from jax import lax
from jax.experimental import pallas as pl
from jax.experimental.pallas import tpu as pltpu

F32, BF16 = jnp.float32, jnp.bfloat16
MESH = pl.DeviceIdType.MESH

HEAD = 64
RW_HEADS = 16
RW_W = 1024
SWA_W = 1024
KV_W = 128
DECAY_LORA, AAA_LORA, GATE_LORA = 64, 64, 160
LORA_W = DECAY_LORA + AAA_LORA + GATE_LORA
SHIFT_COLS = 3 * RW_W + LORA_W
XH = 4
XHD = 512
MEM_LEN = 256
WINDOW = 128
GN_EPS = 64e-5
RMS_EPS = 1e-6
NEG_INF = -1e30
ADAM_LR, ADAM_B1, ADAM_B2, ADAM_EPS, ADAM_WD, ADAM_STEP = 0.001, 0.9, 0.999, 1e-08, 0.01, 10

VMEM_LIMIT = 56 * 1024 * 1024


def _cp(sem=None, **kw):
    return pltpu.CompilerParams(dimension_semantics=sem, vmem_limit_bytes=VMEM_LIMIT, **kw)


def _pick(dim, target):
    if dim <= target:
        return dim
    best = None
    for t in range(128, target + 1, 128):
        if dim % t == 0:
            best = t
    assert best is not None, (dim, target)
    return best


_DIMS = {"nn": (((1,), (0,)), ((), ())), "nt": (((1,), (1,)), ((), ())), "tn": (((0,), (0,)), ((), ()))}


def _mm(a, b, mode, name, out_dtype=F32, alpha=1.0, res=None, bias=None, tm=1024, tn=1024, tk=2048, after=None):
    if mode == "nn":
        (M, K), (K2, N) = a.shape, b.shape
    elif mode == "nt":
        (M, K), (N, K2) = a.shape, b.shape
    else:
        (K, M), (K2, N) = a.shape, b.shape
    assert K == K2, (name, a.shape, b.shape)
    tm, tn, tk = _pick(M, tm), _pick(N, tn), _pick(K, tk)
    nk = K // tk
    a_spec = pl.BlockSpec((tk, tm), lambda i, j, k: (k, i)) if mode == "tn" else pl.BlockSpec((tm, tk), lambda i, j, k: (i, k))
    b_spec = pl.BlockSpec((tn, tk), lambda i, j, k: (j, k)) if mode == "nt" else pl.BlockSpec((tk, tn), lambda i, j, k: (k, j))
    o_spec = pl.BlockSpec((tm, tn), lambda i, j, k: (i, j))
    ins, specs = [a, b], [a_spec, b_spec]
    if res is not None:
        ins.append(res)
        specs.append(o_spec)
    if bias is not None:
        ins.append(bias)
        specs.append(pl.BlockSpec((1, tn), lambda i, j, k: (0, j)))
    if after is not None:
        ins.append(after)
        specs.append(pl.BlockSpec(memory_space=pl.ANY))
    dims = _DIMS[mode]

    def body(*refs):
        a_ref, b_ref = refs[0], refs[1]
        part = lax.dot_general(a_ref[...].astype(BF16), b_ref[...].astype(BF16), dims, preferred_element_type=F32)

        def finish(o, o_ref):
            if alpha != 1.0:
                o = o * alpha
            p = 2
            if res is not None:
                o = o + refs[p][...].astype(F32)
                p += 1
            if bias is not None:
                o = o + refs[p][...]
            o_ref[...] = o.astype(out_dtype)

        if nk == 1:
            finish(part, refs[-1])
            return
        o_ref, acc_ref = refs[-2], refs[-1]
        k = pl.program_id(2)

        @pl.when(k == 0)
        def _():
            acc_ref[...] = part

        @pl.when(k > 0)
        def _():
            acc_ref[...] += part

        @pl.when(k == nk - 1)
        def _():
            finish(acc_ref[...], o_ref)

    return pl.pallas_call(
        body, name=name, grid=(M // tm, N // tn, nk), in_specs=specs, out_specs=o_spec,
        out_shape=jax.ShapeDtypeStruct((M, N), out_dtype), scratch_shapes=[pltpu.VMEM((tm, tn), F32)] * (nk > 1),
        compiler_params=_cp(("parallel", "parallel", "arbitrary")),
    )(*ins)


def _rows(fn, name, T, tm, tiled, full, out_tiled, out_acc, extra=(), reverse=False, scratch=()):
    n = T // tm
    idx = (lambda i: n - 1 - i) if reverse else (lambda i: i)
    in_specs = [pl.BlockSpec((tm, a.shape[1]), lambda i: (idx(i), 0)) for a in tiled]
    in_specs += [mk(idx) for _, mk in extra]
    in_specs += [pl.BlockSpec(a.shape, lambda i, nd=a.ndim: (0,) * nd) for a in full]
    out_specs = [pl.BlockSpec((tm, c), lambda i: (idx(i), 0)) for c, _ in out_tiled]
    out_specs += [pl.BlockSpec(s, lambda i, nd=len(s): (0,) * nd) for s, _ in out_acc]
    out_shape = [jax.ShapeDtypeStruct((T, c), d) for c, d in out_tiled] + [jax.ShapeDtypeStruct(s, d) for s, d in out_acc]
    n_in = len(tiled) + len(extra) + len(full)
    n_t, n_a = len(out_tiled), len(out_acc)

    def body(*refs):
        step = pl.program_id(0)
        vals = [r[...] for r in refs[:n_in]]
        outs = fn(idx(step), *vals, *refs[n_in + n_t + n_a:])
        for r, v in zip(refs[n_in:n_in + n_t], outs[:n_t]):
            r[...] = v.astype(r.dtype)
        for r, v in zip(refs[n_in + n_t:n_in + n_t + n_a], outs[n_t:]):
            @pl.when(step == 0)
            def _(r=r):
                r[...] = jnp.zeros_like(r)

            r[...] += v

    return pl.pallas_call(
        body, name=name, grid=(n,), in_specs=in_specs, out_specs=out_specs, out_shape=out_shape,
        scratch_shapes=list(scratch), compiler_params=_cp(("arbitrary",)),
    )(*tiled, *[a for a, _ in extra], *full)


def _rms(x, g):
    return x * lax.rsqrt(jnp.mean(x * x, axis=-1, keepdims=True) + RMS_EPS) * g


def _rms_fwd(x, g, name, tm=256):
    (h,) = _rows(lambda i, x, g: (_rms(x, g),), name, x.shape[0], min(tm, x.shape[0]), [x], [g], [(x.shape[1], BF16)], [])
    return h


def _rms_bwd(x, g, dh, dres, name, tm=256):
    D = x.shape[1]

    def fn(i, x, dh, dres, g):
        _, vjp = jax.vjp(_rms, x, g)
        dx, dg = vjp(dh.astype(F32))
        dx = dx + dres
        return dx, dg, jnp.sum(dx, axis=0, keepdims=True)

    return _rows(fn, name, x.shape[0], tm, [x, dh, dres], [g], [(D, F32)], [((1, D), F32), ((1, D), F32)])


def _ffn_up(h, wg, wu, name, tm=1024, tn=512, after=None):
    (M, K), N = h.shape, wg.shape[1]
    tm, tn = _pick(M, tm), _pick(N, tn)

    def body(*refs):
        h_ref, wg_ref, wu_ref = refs[:3]
        g_ref, u_ref, a_ref = refs[-3:]
        hb = h_ref[...].astype(BF16)
        g = jnp.dot(hb, wg_ref[...].astype(BF16), preferred_element_type=F32)
        u = jnp.dot(hb, wu_ref[...].astype(BF16), preferred_element_type=F32)
        g_ref[...] = g
        u_ref[...] = u
        a_ref[...] = (g * jax.nn.sigmoid(g) * u).astype(BF16)

    o_spec = pl.BlockSpec((tm, tn), lambda i, j: (i, j))
    w_spec = pl.BlockSpec((K, tn), lambda i, j: (0, j))
    extra = [] if after is None else [after]
    return pl.pallas_call(
        body, name=name, grid=(M // tm, N // tn),
        in_specs=[pl.BlockSpec((tm, K), lambda i, j: (i, 0)), w_spec, w_spec] + [pl.BlockSpec(memory_space=pl.ANY)] * len(extra),
        out_specs=[o_spec] * 3, out_shape=[jax.ShapeDtypeStruct((M, N), F32)] * 2 + [jax.ShapeDtypeStruct((M, N), BF16)],
        compiler_params=_cp(("parallel", "parallel")),
    )(h, wg, wu, *extra)


def _ffn_dact(dxo, wd, g, u, name, tm=1024, tn=512):
    (M, K), N = dxo.shape, wd.shape[0]
    tm, tn = _pick(M, tm), _pick(N, tn)

    def body(dx_ref, wd_ref, g_ref, u_ref, dg_ref, du_ref):
        da = 0.5 * lax.dot_general(dx_ref[...].astype(BF16), wd_ref[...].astype(BF16), _DIMS["nt"], preferred_element_type=F32)
        g = g_ref[...]
        s = jax.nn.sigmoid(g)
        dg_ref[...] = (da * u_ref[...] * (s * (1.0 + g * (1.0 - s)))).astype(BF16)
        du_ref[...] = (da * (g * s)).astype(BF16)

    t_spec = pl.BlockSpec((tm, tn), lambda i, j: (i, j))
    return pl.pallas_call(
        body, name=name, grid=(M // tm, N // tn),
        in_specs=[pl.BlockSpec((tm, K), lambda i, j: (i, 0)), pl.BlockSpec((tn, K), lambda i, j: (j, 0)), t_spec, t_spec],
        out_specs=[t_spec, t_spec], out_shape=[jax.ShapeDtypeStruct((M, N), BF16)] * 2, compiler_params=_cp(("parallel", "parallel")),
    )(dxo, wd, g, u)


def _ffn_fwd(x, gain, wg, wu, wd, tag, after=None):
    h = _rms_fwd(x, gain, tag + "_norm")
    G, U, A = _ffn_up(h, wg, wu, tag + "_up", after=after)
    xo = _mm(A, wd(A) if callable(wd) else wd, "nn", tag + "_down", alpha=0.5, res=x)
    return xo, (h, G, U, A)


def _ffn_bwd(x, gain, wg, wu, wd, saved, dxo, tag, send):
    h, G, U, A = saved
    dwd = _mm(A, dxo, "tn", tag + "_dwd", out_dtype=BF16, alpha=0.5)
    sent = send(tag + "_down", {tag + "_down": dwd})
    dG, dU = _ffn_dact(dxo, wd, G, U, tag + "_dact")
    dwu = _mm(h, dU, "tn", tag + "_dwu", out_dtype=BF16, after=sent)
    sent = send(tag + "_up", {tag + "_up": dwu})
    dwg = _mm(h, dG, "tn", tag + "_dwg", out_dtype=BF16, after=sent)
    sent = send(tag + "_gate", {tag + "_gate": dwg})
    dh = _mm(dG, wg, "nt", tag + "_dh_g", after=sent)
    dh = _mm(dU, wu, "nt", tag + "_dh_u", res=dh)
    dx, dgain, _ = _rms_bwd(x, gain, dh, dxo, tag + "_norm_bwd")
    return dx, dgain


def _segsum64_impl(x):
    r = lax.broadcasted_iota(jnp.int32, (128, 128), 0) // HEAD
    c = lax.broadcasted_iota(jnp.int32, (128, 128), 1) // HEAD
    ones = (r == c).astype(BF16)
    hi = x.astype(BF16)
    lo = (x - hi.astype(F32)).astype(BF16)
    outs = []
    for q in range(x.shape[1] // 128):
        sl = slice(q * 128, (q + 1) * 128)
        outs.append(jnp.dot(hi[:, sl], ones, preferred_element_type=F32) + jnp.dot(lo[:, sl], ones, preferred_element_type=F32))
    return outs[0] if len(outs) == 1 else jnp.concatenate(outs, axis=1)


@jax.custom_vjp
def _segsum64(x):
    return _segsum64_impl(x)


_segsum64.defvjp(lambda x: (_segsum64_impl(x), None), lambda _, ct: (_segsum64_impl(ct),))


def _swap32(x):
    lane = lax.broadcasted_iota(jnp.int32, (x.shape[0], 128), 1)
    outs = [jnp.take_along_axis(x[:, q * 128:(q + 1) * 128], lane ^ 32, axis=1) for q in range(x.shape[1] // 128)]
    return outs[0] if len(outs) == 1 else jnp.concatenate(outs, axis=1)


def _tree_sum(xs):
    xs = list(xs)
    while len(xs) > 1:
        nxt = [xs[i] + xs[i + 1] for i in range(0, len(xs) - 1, 2)]
        if len(xs) % 2:
            nxt.append(xs[-1])
        xs = nxt
    return xs[0]


class _Acc:
    def __init__(self, ways=4):
        self.parts = [None] * ways

    def add(self, i, term):
        k = i % len(self.parts)
        self.parts[k] = term if self.parts[k] is None else self.parts[k] + term

    def total(self):
        return _tree_sum([p for p in self.parts if p is not None])


def _softplus(x):
    return jnp.maximum(x, 0.0) + jnp.log(1.0 + jnp.exp(-jnp.abs(x)))


def _pre_core(k, da, gd, w0, a0, k_k, k_a, w_da, gate_up):
    lane = lax.broadcasted_iota(jnp.int32, da.shape, 1)
    w_da = w_da.astype(BF16)
    l1 = jnp.dot(jnp.where(lane < DECAY_LORA, jnp.tanh(da), 0.0).astype(BF16), w_da, preferred_element_type=F32)
    l2 = jnp.dot(jnp.where(lane >= DECAY_LORA, da, 0.0).astype(BF16), w_da, preferred_element_type=F32)
    wlog = -_softplus(-(w0 + l1)) - 0.5
    decay = jnp.exp(-jnp.exp(wlog))
    a = jax.nn.sigmoid(a0 + l2)
    g = jnp.dot(jax.nn.sigmoid(gd).astype(BF16), gate_up.astype(BF16), preferred_element_type=F32)
    kk = k * k_k
    kkn = kk / jnp.maximum(jnp.sqrt(_segsum64(kk * kk)), 1e-12)
    k2 = k * (1.0 + (a - 1.0) * k_a)
    return decay, k2, -kkn, kkn * a, g


def _pre_shift(i, zr, zl, zr8, zl8, mu, mul):
    live = (i > 0).astype(F32)
    dz = _shift_down(zr, zr8[7:8, :] * live) - zr
    dzl = _shift_down(zl, zl8[7:8, :] * live) - zl
    return zr + dz * mu, zl + dzl * mul, dz, dzl


def _shift_down(x, first_row):
    rolled = pltpu.roll(x, 1, 0)
    row = lax.broadcasted_iota(jnp.int32, x.shape, 0)
    return jnp.where(row == 0, first_row, rolled)


def _shift_up(x, last_row):
    rolled = pltpu.roll(x, x.shape[0] - 1, 0)
    row = lax.broadcasted_iota(jnp.int32, x.shape, 0)
    return jnp.where(row == x.shape[0] - 1, last_row, rolled)


def _prev_rows_spec(tm, cols):
    return lambda idx: pl.BlockSpec((8, cols), lambda i: (jnp.maximum(idx(i) * (tm // 8) - 1, 0), 0))


def _rwkv_pre(p_rkv, p_lora, params, tm=256):
    T = p_rkv.shape[0]

    def fn(i, zr, zl, zr8, zl8, mu, mul, *ps):
        z, z2, _, _ = _pre_shift(i, zr, zl, zr8, zl8, mu, mul)
        decay, k2, an, bn, g = _pre_core(z[:, RW_W:2 * RW_W], z2[:, :128], z2[:, 128:], *ps)
        return z[:, :RW_W], decay, k2, z[:, 2 * RW_W:], an, bn, g

    extra = [(p_rkv, _prev_rows_spec(tm, 3 * RW_W)), (p_lora, _prev_rows_spec(tm, LORA_W))]
    return _rows(fn, "rwkv_pre", T, tm, [p_rkv, p_lora], list(params), [(RW_W, F32)] * 7, [], extra=extra)


def _rwkv_pre_bwd(p_rkv, p_lora, params, cts, tm=256):
    T = p_rkv.shape[0]
    n = T // tm

    def fn(i, zr, zl, cr, cdec, ck2, cv, can, cbn, cg, cr_b, ck2_b, cv_b, zr8, zl8, mu, mul, *rest):
        ps, (car, carl) = rest[:-2], rest[-2:]
        cr, ck2, cv = cr + cr_b, ck2 + ck2_b, cv + cv_b
        z, z2, dif, difl = _pre_shift(i, zr, zl, zr8, zl8, mu, mul)
        _, vjp = jax.vjp(_pre_core, z[:, RW_W:2 * RW_W], z2[:, :128], z2[:, 128:], *ps)
        dk, dda, dgd, *dps = vjp((cdec, ck2, can, cbn, cg))
        dz = jnp.concatenate([cr, dk, cv], axis=1)
        dz2 = jnp.concatenate([dda, dgd], axis=1)
        dzp, dzlp = dz * mu, dz2 * mul

        @pl.when(i == n - 1)
        def _():
            car[...] = jnp.zeros_like(car)
            carl[...] = jnp.zeros_like(carl)

        d_rkv = dz - dzp + _shift_up(dzp, car[0:1, :])
        d_lora = dz2 - dzlp + _shift_up(dzlp, carl[0:1, :])
        car[0:1, :] = dzp[0:1, :]
        carl[0:1, :] = dzlp[0:1, :]
        return (d_rkv, d_lora, jnp.sum(dz * dif, axis=0, keepdims=True), jnp.sum(dz2 * difl, axis=0, keepdims=True), *dps)

    extra = [(p_rkv, _prev_rows_spec(tm, 3 * RW_W)), (p_lora, _prev_rows_spec(tm, LORA_W))]
    acc = [(p.shape, F32) for p in params]
    return _rows(fn, "rwkv_pre_bwd", T, tm, [p_rkv, p_lora, *cts], list(params), [(3 * RW_W, BF16), (LORA_W, BF16)], acc,
                 extra=extra, reverse=True, scratch=[pltpu.VMEM((8, 3 * RW_W), F32), pltpu.VMEM((8, LORA_W), F32)])


def _post_core(y, r, k2, v, g, lw, lb, rk):
    mu = _segsum64(y) * (1.0 / HEAD)
    yc = y - mu
    var = _segsum64(yc * yc) * (1.0 / HEAD)
    yn = yc * lax.rsqrt(var + GN_EPS) * lw + lb
    return (yn + _segsum64(r * k2 * rk) * v) * g


def _rwkv_post(y, r, k2, v, g, lw, lb, rk, tm=256):
    (o,) = _rows(lambda i, *a: (_post_core(*a),), "rwkv_post", y.shape[0], tm, [y, r, k2, v, g], [lw, lb, rk], [(RW_W, BF16)], [])
    return o


def _rwkv_post_bwd(y, r, k2, v, g, lw, lb, rk, do, tm=256):
    def fn(i, y, r, k2, v, g, do, lw, lb, rk):
        _, vjp = jax.vjp(_post_core, y, r, k2, v, g, lw, lb, rk)
        return vjp(do.astype(F32))

    return _rows(fn, "rwkv_post_bwd", y.shape[0], tm, [y, r, k2, v, g, do], [lw, lb, rk], [(RW_W, F32)] * 5, [((1, RW_W), F32)] * 3)


SCAN_L = 32


def _to_tile(x):
    T = x.shape[0]
    return x.reshape(T, RW_HEADS, 8, 8).transpose(0, 2, 1, 3).reshape(T, 8, 128)


def _from_tile(x):
    T = x.shape[0]
    return x.reshape(T, 8, RW_HEADS, 8).transpose(0, 2, 1, 3).reshape(T, RW_W)


def _to_perm(x):
    T = x.shape[0]
    return x.reshape(T, RW_HEADS, HEAD).transpose(0, 2, 1).reshape(T, 8, 128)


def _expander(srcs):
    s = lax.broadcasted_iota(jnp.int32, (8, 128), 0)
    lane = lax.broadcasted_iota(jnp.int32, (8, 128), 1)
    idx = 16 * s + lane // 8

    def expand(t, e_ref):
        for m, r in enumerate(srcs):
            for g in range(8):
                row = jnp.broadcast_to(r[t, pl.ds(g, 1), :], (8, 128))
                e_ref[m, g * 8:(g + 1) * 8, :] = jnp.take_along_axis(row, idx, axis=1)

    return expand


def _ck_a_to_b(ck):
    n = ck.shape[0]
    return ck.reshape(n, 8, 8, 8, RW_HEADS, 8).transpose(0, 3, 5, 1, 4, 2).reshape(n, HEAD, 8, 128)


def _scan_fwd(xes, vi):
    T, L = vi.shape[0], SCAN_L
    nch = T // L

    def body(*refs):
        xr, (vi_ref, yi_ref, sa_ref, ck_ref, st_ref, e0, e1) = refs[:5], refs[5:]

        @pl.when(pl.program_id(0) == 0)
        def _():
            st_ref[...] = jnp.zeros_like(st_ref)

        ck_ref[0] = st_ref[...]
        expand = _expander(xr)
        expand(0, e0)

        def step(t, e_ref):
            v = vi_ref[t]
            row = lambda m, j: jnp.broadcast_to(e_ref[m, pl.ds(j, 1), :], (8, 128))
            sa = _Acc()
            for j in range(HEAD):
                sa.add(j, st_ref[j] * row(0, j))
            sa = sa.total()
            sa_ref[t] = sa
            y = _Acc()
            for j in range(HEAD):
                s = st_ref[j] * row(1, j) + row(2, j) * sa + row(3, j) * v
                st_ref[j] = s
                y.add(j, s * row(4, j))
            yi_ref[t] = y.total()

        def pair(p, carry):
            t = 2 * p
            expand(t + 1, e1)
            step(t, e0)
            expand(jnp.minimum(t + 2, L - 1), e0)
            step(t + 1, e1)
            return carry

        lax.fori_loop(0, L // 2, pair, 0)

    tile = pl.BlockSpec((L, 8, 128), lambda c: (c, 0, 0))
    return pl.pallas_call(
        body, name="rwkv_scan_fwd", grid=(nch,), in_specs=[tile] * 6,
        out_specs=[tile, tile, pl.BlockSpec((1, HEAD, 8, 128), lambda c: (c, 0, 0, 0))],
        out_shape=[jax.ShapeDtypeStruct((T, 8, 128), F32)] * 2 + [jax.ShapeDtypeStruct((nch, HEAD, 8, 128), F32)],
        scratch_shapes=[pltpu.VMEM((HEAD, 8, 128), F32)] + [pltpu.VMEM((5, HEAD, 128), F32)] * 2, compiler_params=_cp(("arbitrary",)),
    )(*xes, vi)


def _scan_bwd_a(xes, dyi):
    T, L = dyi.shape[0], SCAN_L
    nch = T // L

    def body(*refs):
        xr, (dy_ref, dsa_ref, dv_ref, g_ref, e0, e1) = refs[:5], refs[5:]

        @pl.when(pl.program_id(0) == 0)
        def _():
            g_ref[...] = jnp.zeros_like(g_ref)

        expand = _expander(xr)
        expand(L - 1, e0)

        def step(t, e_ref):
            dy = dy_ref[t]
            row = lambda m, j: jnp.broadcast_to(e_ref[m, pl.ds(j, 1), :], (8, 128))
            dsa, dv = _Acc(), _Acc()
            for j in range(HEAD):
                g = g_ref[j] + row(4, j) * dy
                g_ref[j] = g
                dsa.add(j, g * row(2, j))
                dv.add(j, g * row(3, j))
            dsa = dsa.total()
            dsa_ref[t] = dsa
            dv_ref[t] = dv.total()
            for j in range(HEAD):
                g_ref[j] = g_ref[j] * row(1, j) + row(0, j) * dsa

        def pair(p, carry):
            t = L - 1 - 2 * p
            expand(t - 1, e1)
            step(t, e0)
            expand(jnp.maximum(t - 2, 0), e0)
            step(t - 1, e1)
            return carry

        lax.fori_loop(0, L // 2, pair, 0)

    tile = pl.BlockSpec((L, 8, 128), lambda c: (nch - 1 - c, 0, 0))
    return pl.pallas_call(
        body, name="rwkv_scan_bwd_a", grid=(nch,), in_specs=[tile] * 6, out_specs=[tile, tile],
        out_shape=[jax.ShapeDtypeStruct((T, 8, 128), F32)] * 2,
        scratch_shapes=[pltpu.VMEM((HEAD, 8, 128), F32)] + [pltpu.VMEM((5, HEAD, 128), F32)] * 2, compiler_params=_cp(("arbitrary",)),
    )(*xes, dyi)


def _scan_bwd_b(xts, ies, ckb):
    T, L = xts[0].shape[0], SCAN_L
    nch = T // L

    def body(*refs):
        xr, er, ck_ref, dj, (hist, g_ref, e0, e1) = refs[:5], refs[5:9], refs[9], refs[10:15], refs[15:]

        @pl.when(pl.program_id(0) == 0)
        def _():
            g_ref[...] = jnp.zeros_like(g_ref)

        hist[0] = ck_ref[0]
        expand_vs = _expander(er[:2])
        expand = _expander(er)
        expand_vs(0, e0)

        def fstep(t, e_ref):
            w, B, k = xr[1][t], xr[2][t], xr[3][t]
            row = lambda m, i: jnp.broadcast_to(e_ref[m, pl.ds(i, 1), :], (8, 128))
            for i in range(HEAD):
                hist[t + 1, i] = hist[t, i] * w + row(1, i) * B + row(0, i) * k

        def fpair(p, carry):
            t = 2 * p
            expand_vs(t + 1, e1)
            fstep(t, e0)
            expand_vs(jnp.minimum(t + 2, L - 1), e0)
            fstep(t + 1, e1)
            return carry

        lax.fori_loop(0, L // 2, fpair, 0)
        expand(L - 1, e0)

        def bstep(t, e_ref):
            A, w, r = xr[0][t], xr[1][t], xr[4][t]
            row = lambda m, i: jnp.broadcast_to(e_ref[m, pl.ds(i, 1), :], (8, 128))
            acc = [_Acc() for _ in range(5)]
            for i in range(HEAD):
                dy_i, dsa_i = row(2, i), row(3, i)
                g = g_ref[i] + dy_i * r
                sp = hist[t, i]
                acc[4].add(i, hist[t + 1, i] * dy_i)
                acc[1].add(i, g * sp)
                acc[2].add(i, g * row(1, i))
                acc[3].add(i, g * row(0, i))
                acc[0].add(i, sp * dsa_i)
                g_ref[i] = g * w + dsa_i * A
            for m in range(5):
                dj[m][t] = acc[m].total()

        def bpair(p, carry):
            t = L - 1 - 2 * p
            expand(t - 1, e1)
            bstep(t, e0)
            expand(jnp.maximum(t - 2, 0), e0)
            bstep(t - 1, e1)
            return carry

        lax.fori_loop(0, L // 2, bpair, 0)

    tile = pl.BlockSpec((L, 8, 128), lambda c: (nch - 1 - c, 0, 0))
    return pl.pallas_call(
        body, name="rwkv_scan_bwd_b", grid=(nch,),
        in_specs=[tile] * 9 + [pl.BlockSpec((1, HEAD, 8, 128), lambda c: (nch - 1 - c, 0, 0, 0))],
        out_specs=[tile] * 5, out_shape=[jax.ShapeDtypeStruct((T, 8, 128), F32)] * 5,
        scratch_shapes=[pltpu.VMEM((L + 1, HEAD, 8, 128), F32), pltpu.VMEM((HEAD, 8, 128), F32)] + [pltpu.VMEM((4, HEAD, 128), F32)] * 2,
        compiler_params=_cp(("arbitrary",)),
    )(*xts, *ies, ckb)


SWA_COLS = SWA_W + 2 * KV_W
BLK = 128


def _swa_core(n, k2a, k2b, vla, vra, vlb, vrb, sinks, *qps):
    iq = lax.broadcasted_iota(jnp.int32, (BLK, 2 * BLK), 0)
    ik = lax.broadcasted_iota(jnp.int32, (BLK, 2 * BLK), 1)
    diff = BLK + iq - ik
    valid = (diff >= 0) & (diff < WINDOW) & ((n > 0) | (ik >= BLK))
    lane = lax.broadcasted_iota(jnp.int32, (BLK, 128), 1)
    lane1 = lax.broadcasted_iota(jnp.int32, (1, 128), 1)
    nt = (((1,), (1,)), ((), ()))
    outs = []
    for pp in range(8):
        k2, vl, vr = (k2a, vla, vra) if pp < 4 else (k2b, vlb, vrb)
        qp = qps[pp]
        o = None
        for half, vv in ((0, vl), (1, vr)):
            qh = jnp.where((lane >= HEAD) == (half == 1), qp, 0.0).astype(BF16)
            s = lax.dot_general(qh, k2.astype(BF16), nt, preferred_element_type=F32) * (HEAD ** -0.5)
            s = jnp.where(valid, s, NEG_INF)
            sink = jnp.sum(jnp.where(lane1 == 2 * pp + half, sinks, 0.0), axis=1, keepdims=True)
            m = jnp.maximum(jnp.max(s, axis=1, keepdims=True), sink)
            p = jnp.exp(s - m)
            den = jnp.sum(p, axis=1, keepdims=True) + jnp.exp(sink - m)
            oh = jnp.dot((p / den).astype(BF16), vv.astype(BF16), preferred_element_type=F32)
            o = oh if o is None else o + oh
        outs.append(o)
    return jnp.concatenate(outs, axis=1)


def _swa_prep(pc, pp, b, cq, sq, ckc, skc, ckp, skp):
    zc, zp = pc + b, pp + b
    qr = zc[:, :SWA_W] * cq + _swap32(zc[:, :SWA_W]) * sq
    kc, kp = zc[:, SWA_W:SWA_W + KV_W], zp[:, SWA_W:SWA_W + KV_W]
    kb = jnp.concatenate([kp * ckp + _swap32(kp) * skp, kc * ckc + _swap32(kc) * skc], axis=0)
    vb = jnp.concatenate([zp[:, SWA_W + KV_W:], zc[:, SWA_W + KV_W:]], axis=0)
    lane = lax.broadcasted_iota(jnp.int32, kb.shape, 1)
    left = lane < HEAD
    kbr, vbr = pltpu.roll(kb, HEAD, 1), pltpu.roll(vb, HEAD, 1)
    return (jnp.where(left, kb, kbr), jnp.where(left, kbr, kb), jnp.where(left, vb, 0.0), jnp.where(left, 0.0, vbr),
            jnp.where(left, vbr, 0.0), jnp.where(left, 0.0, vb)), [qr[:, q * 128:(q + 1) * 128] for q in range(8)]


def _swa_specs(T, tabs_q, tabs_k):
    cur = lambda c: pl.BlockSpec((BLK, c), lambda n: (n, 0))
    prev = lambda c: pl.BlockSpec((BLK, c), lambda n: (jnp.maximum(n - 1, 0), 0))
    return cur, prev


def _swa_fwd(p_swa, b, sinks, cq, sq, ck, sk):
    T = p_swa.shape[0]
    cur, prev = _swa_specs(T, None, None)

    def body(pc, pp, b_ref, s_ref, cq_r, sq_r, ckc, skc, ckp, skp, o_ref):
        ops, qps = _swa_prep(pc[...], pp[...], b_ref[...], cq_r[...], sq_r[...], ckc[...], skc[...], ckp[...], skp[...])
        o_ref[...] = _swa_core(pl.program_id(0), *ops, s_ref[...], *qps).astype(o_ref.dtype)

    full = lambda a: pl.BlockSpec(a.shape, lambda n: (0, 0))
    return pl.pallas_call(
        body, name="swa_fwd", grid=(T // BLK,),
        in_specs=[cur(SWA_COLS), prev(SWA_COLS), full(b), full(sinks), cur(SWA_W), cur(SWA_W), cur(KV_W), cur(KV_W), prev(KV_W), prev(KV_W)],
        out_specs=cur(SWA_W), out_shape=jax.ShapeDtypeStruct((T, SWA_W), BF16), compiler_params=_cp(("arbitrary",)),
    )(p_swa, p_swa, b, sinks, cq, sq, ck, sk, ck, sk)


def _swa_bwd(p_swa, b, sinks, cq, sq, ck, sk, do):
    T = p_swa.shape[0]
    nb = T // BLK
    cur = lambda c: pl.BlockSpec((BLK, c), lambda s: (nb - 1 - s, 0))
    prev = lambda c: pl.BlockSpec((BLK, c), lambda s: (jnp.maximum(nb - 2 - s, 0), 0))

    def body(pc, pp, b_ref, s_ref, cq_r, sq_r, ckc, skc, ckp, skp, do_ref, dcur, db, dsk, carry):
        step = pl.program_id(0)
        n = nb - 1 - step

        @pl.when(step == 0)
        def _():
            carry[...] = jnp.zeros_like(carry)
            db[...] = jnp.zeros_like(db)
            dsk[...] = jnp.zeros_like(dsk)

        ops, qps = _swa_prep(pc[...], pp[...], b_ref[...], cq_r[...], sq_r[...], ckc[...], skc[...], ckp[...], skp[...])
        _, vjp = jax.vjp(functools.partial(_swa_core, n), *ops, s_ref[...], *qps)
        dk2a, dk2b, dvla, dvra, dvlb, dvrb, dsinks, *dqps = vjp(do_ref[...].astype(F32))
        dqr = jnp.concatenate(dqps, axis=1)
        lane = lax.broadcasted_iota(jnp.int32, dk2a.shape, 1)
        left = lane < HEAD
        dkb = jnp.where(left, dk2a + pltpu.roll(dk2a, HEAD, 1), dk2b + pltpu.roll(dk2b, HEAD, 1))
        dvb = jnp.where(left, dvla + pltpu.roll(dvra, HEAD, 1), pltpu.roll(dvlb, HEAD, 1) + dvrb)
        dq = dqr * cq_r[...] + _swap32(dqr * sq_r[...])
        dkp, dkc = dkb[:BLK], dkb[BLK:]
        dkp = dkp * ckp[...] + _swap32(dkp * skp[...])
        dkc = dkc * ckc[...] + _swap32(dkc * skc[...])
        dc = jnp.concatenate([dq, jnp.concatenate([dkc, dvb[BLK:]], axis=1) + carry[...]], axis=1)
        carry[...] = jnp.concatenate([dkp, dvb[:BLK]], axis=1)
        dcur[...] = dc.astype(dcur.dtype)
        db[...] += jnp.sum(dc, axis=0, keepdims=True)
        dsk[...] += dsinks

    full = lambda a: pl.BlockSpec(a.shape, lambda s: (0, 0))
    return pl.pallas_call(
        body, name="swa_bwd", grid=(nb,),
        in_specs=[cur(SWA_COLS), prev(SWA_COLS), full(b), full(sinks), cur(SWA_W), cur(SWA_W), cur(KV_W), cur(KV_W), prev(KV_W), prev(KV_W),
                  cur(SWA_W)],
        out_specs=[cur(SWA_COLS), full(b), full(sinks)],
        out_shape=[jax.ShapeDtypeStruct((T, SWA_COLS), BF16), jax.ShapeDtypeStruct(b.shape, F32), jax.ShapeDtypeStruct(sinks.shape, F32)],
        scratch_shapes=[pltpu.VMEM((BLK, 2 * KV_W), F32)], compiler_params=_cp(("arbitrary",)),
    )(p_swa, p_swa, b, sinks, cq, sq, ck, sk, ck, sk, do)


def _rope_tables(T):
    inv = 10000.0 ** (-jnp.arange(0, HEAD, 2, dtype=F32) / HEAD)
    ang = jnp.arange(T, dtype=F32)[:, None] * inv[None, :]
    c = jnp.concatenate([jnp.cos(ang), jnp.cos(ang)], axis=1)
    s = jnp.concatenate([-jnp.sin(ang), jnp.sin(ang)], axis=1)
    return jnp.tile(c, (1, 16)), jnp.tile(s, (1, 16)), jnp.tile(c, (1, 2)), jnp.tile(s, (1, 2))


def _xattn_core(*qkv):
    outs = []
    for h in range(XH):
        qh, kh, vh = qkv[h], qkv[XH + h], qkv[2 * XH + h]
        s = lax.dot_general(qh.astype(BF16), kh.astype(BF16), (((1,), (1,)), ((), ())), preferred_element_type=F32) * (XHD ** -0.5)
        p = jnp.exp(s - jnp.max(s, axis=1, keepdims=True))
        p = p / jnp.sum(p, axis=1, keepdims=True)
        outs.append(jnp.dot(p.astype(BF16), vh.astype(BF16), preferred_element_type=F32))
    return jnp.concatenate(outs, axis=1)


def _xattn_split(q, kv):
    return [q[:, h * XHD:(h + 1) * XHD] for h in range(XH)] + [kv[:, h * XHD:(h + 1) * XHD] for h in range(2 * XH)]


def _xattn_fwd(q, kv, tm=256):
    (o,) = _rows(lambda i, q, kv: (_xattn_core(*_xattn_split(q, kv)),), "xattn_fwd", q.shape[0], tm, [q], [kv], [(q.shape[1], BF16)], [])
    return o


def _xattn_bwd(q, kv, do, tm=256):
    def fn(i, q, do, kv):
        _, vjp = jax.vjp(_xattn_core, *_xattn_split(q, kv))
        d = vjp(do.astype(F32))
        return jnp.concatenate(d[:XH], axis=1), jnp.concatenate(d[XH:], axis=1)

    return _rows(fn, "xattn_bwd", q.shape[0], tm, [q, do], [kv], [(q.shape[1], BF16)], [(kv.shape, F32)])


def _loss_head(x, g, tgt, tm=256):
    D = x.shape[1]

    def fn(i, x, tgt, g):
        y, vjp = jax.vjp(_rms, x, g)
        err = y - tgt
        dx, dg = vjp(err * (1.0 / D))
        part = 0.5 / D * jnp.sum(jnp.sum(err * err, axis=1, keepdims=True), axis=0, keepdims=True)
        return dx, jnp.broadcast_to(part, (1, 128)), dg

    return _rows(fn, "loss_head", x.shape[0], tm, [x, tgt], [g], [(D, F32)], [((1, 128), F32), ((1, D), F32)])


def _local_step(x, mem, tgt, get_w, P, put_g):
    T = x.shape[0]
    W = dict(get_w("f1", None))

    def f1_down(after):
        W.update(get_w("f1d", after))
        return W["f1_down"]

    x1, s1 = _ffn_fwd(x, P["f1_norm"], W["f1_gate"], W["f1_up"], f1_down, "f1", after=W.get("_after"))

    W.update(get_w("mix", x1))
    h2 = _rms_fwd(x1, P["mix_norm"], "mix_norm")
    w_rkv, w_lora, w_swa = W["w_inT"][:3 * RW_W], W["w_inT"][3 * RW_W:SHIFT_COLS], W["w_inT"][SHIFT_COLS:]
    p_rkv = _mm(h2, w_rkv, "nt", "in_rkv")
    p_lora = _mm(h2, w_lora, "nt", "in_lora")
    p_swa = _mm(h2, w_swa, "nt", "in_swa")
    w_da = jnp.concatenate([W["rw_decay_up"], W["rw_aaa_up"]], axis=0)
    pre_params = (P["rw_mu"][:, :3 * RW_W], P["rw_mu"][:, 3 * RW_W:], P["rw_w0"], P["rw_a0"], P["rw_k_k"], P["rw_k_a"], w_da,
                  W["rw_gate_up"])
    r, decay, k2, v, an, bn, g = _rwkv_pre(p_rkv, p_lora, pre_params)
    scan_vecs = (an, decay, bn, k2, r)
    xes = [_to_perm(a) for a in scan_vecs]
    yi, sai, ck = _scan_fwd(xes, _to_tile(v))
    y_scan = _from_tile(yi)
    y_rw = _rwkv_post(y_scan, r, k2, v, g, P["rw_lnx_w"], P["rw_lnx_b"], P["rw_r_k"])
    cq, sq, ckt, skt = _rope_tables(T)
    y_swa = _swa_fwd(p_swa, P["b_in_attn"], P["attn_sinks"], cq, sq, ckt, skt)
    ycat = jnp.concatenate([y_rw, y_swa], axis=1)
    x2 = _mm(ycat, W["w_out"], "nn", "out_proj", res=x1, bias=P["b_out"])

    W.update(get_w("xattn", x2))
    hx = _rms_fwd(x2, P["xa_norm"], "xa_norm")
    mn = _rms_fwd(mem, P["mem_norm"], "mem_norm")
    q = _mm(hx, W["w_xq"], "nn", "xq", out_dtype=BF16)
    kv = _mm(mn, W["w_xkv"], "nn", "xkv", out_dtype=BF16)
    o = _xattn_fwd(q, kv)
    x3 = _mm(o, W["w_xo"], "nn", "xo", res=x2)

    W.update(get_w("f2", x3))
    x4, s2 = _ffn_fwd(x3, P["f2_norm"], W["f2_gate"], W["f2_up"], W["f2_down"], "f2")
    dx4, loss_part, d_final = _loss_head(x4, P["final_norm"], tgt)

    gs = {"final_norm": d_final}
    dx3, gs["f2_norm"] = _ffn_bwd(x3, P["f2_norm"], W["f2_gate"], W["f2_up"], W["f2_down"], s2, dx4, "f2", put_g)

    do = _mm(dx3, W["w_xo"], "nt", "xo_do", out_dtype=BF16)
    dw_xo = _mm(o, dx3, "tn", "xo_dw", out_dtype=BF16)
    dq, dkv = _xattn_bwd(q, kv, do)
    dw_xq = _mm(hx, dq, "tn", "xq_dw", out_dtype=BF16)
    dw_xkv = _mm(mn, dkv, "tn", "xkv_dw", out_dtype=BF16)
    sent = put_g("xattn", {"w_xq": dw_xq, "w_xkv": dw_xkv, "w_xo": dw_xo})
    dhx = _mm(dq, W["w_xq"], "nt", "xq_dh", after=sent)
    dmn = _mm(dkv, W["w_xkv"], "nt", "xkv_dmn")
    _, gs["mem_norm"], _ = _rms_bwd(mem, P["mem_norm"], dmn, jnp.zeros_like(mem), "mem_norm_bwd")
    dx2, gs["xa_norm"], gs["b_out"] = _rms_bwd(x2, P["xa_norm"], dhx, dx3, "xa_norm_bwd")

    dycat = _mm(dx2, W["w_out"], "nt", "out_dy")
    dw_out = _mm(ycat, dx2, "tn", "out_dw", out_dtype=BF16)
    dp_swa, gs["b_in_attn"], gs["attn_sinks"] = _swa_bwd(p_swa, P["b_in_attn"], P["attn_sinks"], cq, sq, ckt, skt, dycat[:, RW_W:])
    dy_scan, dr_b, dk2_b, dv_b, dg, gs["rw_lnx_w"], gs["rw_lnx_b"], gs["rw_r_k"] = _rwkv_post_bwd(
        y_scan, r, k2, v, g, P["rw_lnx_w"], P["rw_lnx_b"], P["rw_r_k"], dycat[:, :RW_W])
    dsai, dvi = _scan_bwd_a(xes, _to_tile(dy_scan))
    ies = [_to_perm(v), _to_perm(_from_tile(sai)), _to_perm(dy_scan), _to_perm(_from_tile(dsai))]
    dj = _scan_bwd_b([_to_tile(a) for a in scan_vecs], ies, _ck_a_to_b(ck))
    dan, ddecay, dbn, dk2_s, dr_s = (_from_tile(d) for d in dj)
    cts = (dr_s, ddecay, dk2_s, _from_tile(dvi), dan, dbn, dg, dr_b, dk2_b, dv_b)
    dp_rkv, dp_lora, dmu, dmul, gs["rw_w0"], gs["rw_a0"], gs["rw_k_k"], gs["rw_k_a"], dw_da, gs["rw_gate_up"] = _rwkv_pre_bwd(
        p_rkv, p_lora, pre_params, cts)
    gs["rw_mu"] = jnp.concatenate([dmu, dmul], axis=1)
    gs["rw_decay_up"], gs["rw_aaa_up"] = dw_da[:DECAY_LORA], dw_da[DECAY_LORA:]
    dw_inT = jnp.concatenate([_mm(dp_rkv, h2, "tn", "in_dw_rkv"), _mm(dp_lora, h2, "tn", "in_dw_lora"),
                              _mm(dp_swa, h2, "tn", "in_dw_swa")], axis=0)
    sent = put_g("mix", {"w_in": dw_inT, "w_out": dw_out})
    dh2 = _mm(dp_rkv, w_rkv, "nn", "in_dh_rkv", after=sent)
    dh2 = _mm(dp_lora, w_lora, "nn", "in_dh_lora", res=dh2)
    dh2 = _mm(dp_swa, w_swa, "nn", "in_dh_swa", res=dh2)
    dx1, gs["mix_norm"], _ = _rms_bwd(x1, P["mix_norm"], dh2, dx2, "mix_norm_bwd")

    dx0, gs["f1_norm"] = _ffn_bwd(x, P["f1_norm"], W["f1_gate"], W["f1_up"], W["f1_down"], s1, dx1, "f1", put_g)
    return loss_part, dx0, gs


_ANY = pl.BlockSpec(memory_space=pl.ANY)
_OTHER_CHIPS = ((1, 0), (0, 1), (1, 1))


def _mesh_pos():
    return lax.axis_index("x"), lax.axis_index("y"), lax.axis_index("c")


def _slot(ref, kind, s, rows, cols):
    if kind == "row":
        return ref.at[pl.ds(pl.multiple_of(s * rows, 8), rows), :]
    return ref.at[:, pl.ds(pl.multiple_of(s * cols, 128), cols)]


_HBM = pl.BlockSpec(memory_space=pltpu.HBM)
_SEMS = pl.BlockSpec(memory_space=pltpu.SEMAPHORE)
_SPLIT = dict(compiler_params=pltpu.CompilerParams(has_side_effects=pltpu.SideEffectType.DATAFLOW_SIDE_EFFECTING))


def _in_hbm(a):
    return pltpu.with_memory_space_constraint(a, pltpu.HBM)


def _full_shape(s, kind):
    return (4 * s.shape[0], s.shape[1]) if kind == "row" else (s.shape[0], 4 * s.shape[1])


def _half(ref, shape, h):
    rows, cols = shape
    if rows % 32 == 0:
        return ref.at[pl.ds(pl.multiple_of(h * (rows // 2), 16), rows // 2), :]
    assert cols % 256 == 0, shape
    return ref.at[:, pl.ds(pl.multiple_of(h * (cols // 2), 128), cols // 2)]


def _swap_halves(name, fulls, shard_shapes, kinds):
    n = len(fulls)

    def body(*refs):
        out, send, recv = refs[n:2 * n], refs[2 * n], refs[2 * n + 1]
        x, y, c = _mesh_pos()
        sent = []
        for i in range(n):
            for r, (dx, dy) in enumerate(_OTHER_CHIPS):
                theirs = _slot(out[i], kinds[i], 2 * ((x + dx) % 2) + (y + dy) % 2, *shard_shapes[i])
                have = _half(theirs, shard_shapes[i], c)
                rc = pltpu.make_async_remote_copy(have, have, send.at[3 * i + r], recv.at[3 * i + r], device_id=(x, y, 1 - c),
                                                  device_id_type=MESH)
                rc.start()
                sent.append(rc)
        for i in range(n):
            for r, (dx, dy) in enumerate(_OTHER_CHIPS):
                theirs = _slot(out[i], kinds[i], 2 * ((x + dx) % 2) + (y + dy) % 2, *shard_shapes[i])
                need = _half(theirs, shard_shapes[i], 1 - c)
                pltpu.make_async_remote_copy(need, need, send.at[3 * i + r], recv.at[3 * i + r], device_id=(x, y, c),
                                             device_id_type=MESH).wait_recv()
        for rc in sent:
            rc.wait_send()

    return pl.pallas_call(
        body, name=name, in_specs=[_ANY] * n, out_specs=[_ANY] * n, out_shape=[jax.ShapeDtypeStruct(f.shape, f.dtype) for f in fulls],
        input_output_aliases={i: i for i in range(n)},
        scratch_shapes=[pltpu.SemaphoreType.DMA((3 * n,)), pltpu.SemaphoreType.DMA((3 * n,))],
    )(*fulls)


def _gather_now(name, shards, kinds):
    n = len(shards)

    def body(*refs):
        src, out = refs[:n], refs[n:2 * n]
        ici_send, ici_recv, d2d_send, d2d_recv, loc = refs[2 * n:]
        x, y, c = _mesh_pos()
        me = 2 * x + y
        half = lambda ref, rows, h: ref.at[pl.ds(pl.multiple_of(h * (rows // 2), 8), rows // 2), :]
        own, sent = [], []
        for i in range(n):
            rows, cols = src[i].shape
            mine = _slot(out[i], kinds[i], me, rows, cols)
            cp = pltpu.make_async_copy(src[i], mine, loc.at[i])
            cp.start()
            own.append(cp)
            for r, (dx, dy) in enumerate(_OTHER_CHIPS):
                rc = pltpu.make_async_remote_copy(half(src[i], rows, c), half(mine, rows, c), ici_send.at[3 * i + r], ici_recv.at[3 * i + r],
                                                  device_id=((x + dx) % 2, (y + dy) % 2, c), device_id_type=MESH)
                rc.start()
                sent.append(rc)
        for i in range(n):
            rows, cols = src[i].shape
            for r, (dx, dy) in enumerate(_OTHER_CHIPS):
                theirs = _slot(out[i], kinds[i], 2 * ((x + dx) % 2) + (y + dy) % 2, rows, cols)
                landed = half(theirs, rows, c)
                pltpu.make_async_remote_copy(landed, landed, ici_send.at[3 * i + r], ici_recv.at[3 * i + r],
                                             device_id=(x, y, c), device_id_type=MESH).wait_recv()
                fw = pltpu.make_async_remote_copy(landed, landed, d2d_send.at[3 * i + r], d2d_recv.at[3 * i + r],
                                                  device_id=(x, y, 1 - c), device_id_type=MESH)
                fw.start()
                sent.append(fw)
        for i in range(n):
            rows, cols = src[i].shape
            for r, (dx, dy) in enumerate(_OTHER_CHIPS):
                other = half(_slot(out[i], kinds[i], 2 * ((x + dx) % 2) + (y + dy) % 2, rows, cols), rows, 1 - c)
                pltpu.make_async_remote_copy(other, other, d2d_send.at[3 * i + r], d2d_recv.at[3 * i + r],
                                             device_id=(x, y, c), device_id_type=MESH).wait_recv()
        for cp in sent:
            cp.wait_send()
        for cp in own:
            cp.wait()

    return pl.pallas_call(
        body, name=name, in_specs=[_ANY] * n, out_specs=[_ANY] * n,
        out_shape=[jax.ShapeDtypeStruct(_full_shape(s, k), s.dtype) for s, k in zip(shards, kinds)],
        scratch_shapes=[pltpu.SemaphoreType.DMA((3 * n,))] * 4 + [pltpu.SemaphoreType.DMA((n,))],
    )(*shards)


def _gather_start(name, shards, kinds, groups, after=None):
    n, ng = len(shards), len(groups)
    lands = [_in_hbm(lax.empty(_full_shape(s, k), s.dtype)) for s, k in zip(shards, kinds)]
    n_in = 2 * n + (after is not None)

    def body(*refs):
        src, land, sems, token = refs[:n], refs[n:2 * n], refs[n_in:n_in + 3 * ng], refs[-1]
        x, y, c = _mesh_pos()
        me = 2 * x + y
        for gi, idxs in enumerate(groups):
            send, recv, own = sems[3 * gi:3 * gi + 3]
            for k, i in enumerate(idxs):
                mine = _slot(land[i], kinds[i], me, *src[i].shape)
                for r, (dx, dy) in enumerate(_OTHER_CHIPS):
                    pltpu.make_async_remote_copy(_half(src[i], src[i].shape, c), _half(mine, src[i].shape, c), send.at[3 * k + r],
                                                 recv.at[3 * k + r], device_id=((x + dx) % 2, (y + dy) % 2, c), device_id_type=MESH).start()
                pltpu.make_async_copy(src[i], mine, own.at[k]).start()
        token[...] = jnp.zeros_like(token)

    sem_shapes = [pltpu.SemaphoreType.DMA((w * len(g),)) for g in groups for w in (3, 3, 1)]
    thru = [pltpu.HBM(a.shape, a.dtype) for a in (*shards, *lands)]
    res = pl.pallas_call(
        body, name=name, in_specs=[_HBM] * (2 * n) + [_ANY] * (after is not None),
        out_specs=[_SEMS] * (3 * ng) + [_HBM] * (2 * n) + [pl.BlockSpec(memory_space=pltpu.VMEM)],
        out_shape=sem_shapes + thru + [jax.ShapeDtypeStruct((8, 128), F32)],
        input_output_aliases={i: 3 * ng + i for i in range(2 * n)}, **_SPLIT,
    )(*[_in_hbm(s) for s in shards], *lands, *([] if after is None else [after]))
    return res[:3 * ng], res[3 * ng:3 * ng + n], res[3 * ng + n:3 * ng + 2 * n], res[-1]


def _gather_wait(name, sems, shards, lands, kinds, after):
    m = len(shards)

    def body(*refs):
        src, land, (send, recv, own) = refs[:m], refs[m:2 * m], refs[2 * m:2 * m + 3]
        x, y, c = _mesh_pos()
        me = 2 * x + y
        for k in range(m):
            mine = _slot(land[k], kinds[k], me, *src[k].shape)
            for r in range(3):
                cp = pltpu.make_async_remote_copy(_half(src[k], src[k].shape, c), _half(mine, src[k].shape, c), send.at[3 * k + r],
                                                  recv.at[3 * k + r], device_id=(x, y, c), device_id_type=MESH)
                cp.wait_send()
                cp.wait_recv()
            pltpu.make_async_copy(src[k], mine, own.at[k]).wait()

    thru = [pltpu.HBM(a.shape, a.dtype) for a in (*shards, *lands)]
    res = pl.pallas_call(
        body, name=name, in_specs=[_HBM] * (2 * m) + [_SEMS] * 3 + [pl.BlockSpec(memory_space=pl.ANY)],
        out_specs=[_HBM] * (2 * m), out_shape=thru, input_output_aliases={i: i for i in range(2 * m)}, **_SPLIT,
    )(*shards, *lands, *sems, after)
    return res[m:]


def _scatter_start(name, grads, kinds):
    m = len(grads)
    shard_shape = [(g.shape[0] // 4, g.shape[1]) if k == "row" else (g.shape[0], g.shape[1] // 4) for g, k in zip(grads, kinds)]
    lands = [_in_hbm(lax.empty((4, *s), g.dtype)) for s, g in zip(shard_shape, grads)]

    def body(*refs):
        src, land, (send, recv, own) = refs[:m], refs[m:2 * m], refs[2 * m:2 * m + 3]
        x, y, c = _mesh_pos()
        me = 2 * x + y
        for k in range(m):
            for r, (dx, dy) in enumerate(_OTHER_CHIPS):
                tx, ty = (x + dx) % 2, (y + dy) % 2
                pltpu.make_async_remote_copy(_slot(src[k], kinds[k], 2 * tx + ty, *shard_shape[k]), land[k].at[me],
                                             send.at[3 * k + r], recv.at[3 * k + r], device_id=(tx, ty, c), device_id_type=MESH).start()
            pltpu.make_async_copy(_slot(src[k], kinds[k], me, *shard_shape[k]), land[k].at[me], own.at[k]).start()
        refs[-1][...] = jnp.zeros_like(refs[-1])

    thru = [pltpu.HBM(a.shape, a.dtype) for a in (*grads, *lands)]
    res = pl.pallas_call(
        body, name=name, in_specs=[_HBM] * (2 * m),
        out_specs=[_SEMS] * 3 + [_HBM] * (2 * m) + [pl.BlockSpec(memory_space=pltpu.VMEM)],
        out_shape=[pltpu.SemaphoreType.DMA((3 * m,))] * 2 + [pltpu.SemaphoreType.DMA((m,))] + thru + [jax.ShapeDtypeStruct((8, 128), F32)],
        input_output_aliases={i: 3 + i for i in range(2 * m)}, **_SPLIT,
    )(*[_in_hbm(g) for g in grads], *lands)
    return res[:3], res[3:3 + m], res[3 + m:3 + 2 * m], res[-1]


def _scatter_wait(name, sems, grads, lands, kinds, after):
    m = len(grads)

    def body(*refs):
        src, land, (send, recv, own) = refs[:m], refs[m:2 * m], refs[2 * m:2 * m + 3]
        x, y, c = _mesh_pos()
        me = 2 * x + y
        for k in range(m):
            mine = _slot(src[k], kinds[k], me, *land[k].shape[1:])
            for r in range(3):
                cp = pltpu.make_async_remote_copy(mine, land[k].at[me], send.at[3 * k + r], recv.at[3 * k + r],
                                                  device_id=(x, y, c), device_id_type=MESH)
                cp.wait_send()
                cp.wait_recv()
            pltpu.make_async_copy(mine, land[k].at[me], own.at[k]).wait()

    thru = [pltpu.HBM(a.shape, a.dtype) for a in (*grads, *lands)]
    res = pl.pallas_call(
        body, name=name, in_specs=[_HBM] * (2 * m) + [_SEMS] * 3 + [pl.BlockSpec(memory_space=pl.ANY)],
        out_specs=[_HBM] * (2 * m), out_shape=thru, input_output_aliases={i: i for i in range(2 * m)}, **_SPLIT,
    )(*grads, *lands, *sems, after)
    return res[m:]


def _swap_with_sibling(arrs, name):
    n = len(arrs)

    def body(*refs):
        ins, outs = refs[:n], refs[n:2 * n]
        send, recv = refs[2 * n:]
        x, y, c = _mesh_pos()
        copies = []
        for i in range(n):
            rc = pltpu.make_async_remote_copy(ins[i], outs[i], send.at[i], recv.at[i], device_id=(x, y, 1 - c), device_id_type=MESH)
            rc.start()
            copies.append(rc)
        for rc in copies:
            rc.wait()

    return pl.pallas_call(
        body, name=name, in_specs=[_ANY] * n, out_specs=[_ANY] * n,
        out_shape=[jax.ShapeDtypeStruct(a.shape, a.dtype) for a in arrs],
        scratch_shapes=[pltpu.SemaphoreType.DMA((n,)), pltpu.SemaphoreType.DMA((n,))],
    )(*arrs)


def _small_start(pack, after):
    land = _in_hbm(lax.empty((8, *pack.shape), pack.dtype))

    def body(in_ref, land_ref, after_ref, send, recv, own, in_thru, land_thru, token):
        x, y, c = _mesh_pos()
        me = 4 * x + 2 * y + c
        for r in range(1, 8):
            dx, dy, dc = r // 4, (r // 2) % 2, r % 2
            pltpu.make_async_remote_copy(in_ref, land_ref.at[me], send.at[r - 1], recv.at[r - 1],
                                         device_id=((x + dx) % 2, (y + dy) % 2, (c + dc) % 2), device_id_type=MESH).start()
        pltpu.make_async_copy(in_ref, land_ref.at[me], own.at[0]).start()
        token[...] = jnp.zeros_like(token)

    res = pl.pallas_call(
        body, name="small_start", in_specs=[_HBM, _HBM, _ANY],
        out_specs=[_SEMS] * 3 + [_HBM, _HBM, pl.BlockSpec(memory_space=pltpu.VMEM)],
        out_shape=[pltpu.SemaphoreType.DMA((7,)), pltpu.SemaphoreType.DMA((7,)), pltpu.SemaphoreType.DMA((1,)),
                   pltpu.HBM(pack.shape, pack.dtype), pltpu.HBM(land.shape, land.dtype), jax.ShapeDtypeStruct((8, 128), F32)],
        input_output_aliases={0: 3, 1: 4}, **_SPLIT,
    )(_in_hbm(pack), land, after)
    return res[:3], res[3], res[4], res[5]


def _small_wait(sems, pack, land, after):
    def body(in_ref, land_ref, send, recv, own, after_ref, in_dead, got):
        x, y, c = _mesh_pos()
        me = 4 * x + 2 * y + c
        for r in range(1, 8):
            cp = pltpu.make_async_remote_copy(in_ref, land_ref.at[me], send.at[r - 1], recv.at[r - 1], device_id=(x, y, c),
                                              device_id_type=MESH)
            cp.wait_send()
            cp.wait_recv()
        pltpu.make_async_copy(in_ref, land_ref.at[me], own.at[0]).wait()

    res = pl.pallas_call(
        body, name="small_wait", in_specs=[_HBM, _HBM] + [_SEMS] * 3 + [_ANY], out_specs=[_HBM, _HBM],
        out_shape=[pltpu.HBM(pack.shape, pack.dtype), pltpu.HBM(land.shape, land.dtype)], input_output_aliases={0: 0, 1: 1}, **_SPLIT,
    )(pack, land, *sems, after)
    return res[1]


def _row_tile(R, dtype, target=256):
    mult = 8 * 4 // jnp.dtype(dtype).itemsize
    best = R
    for t in range(mult, min(R, target) + 1, mult):
        if R % t == 0:
            best = t
    return best


def _sum_slots(stack, name, out_dtype=F32):
    k, R, C = stack.shape
    tr = _row_tile(R, stack.dtype)

    def body(s_ref, o_ref):
        acc = s_ref[0].astype(F32)
        for j in range(1, k):
            acc = acc + s_ref[j].astype(F32)
        o_ref[...] = acc.astype(out_dtype)

    return pl.pallas_call(
        body, name=name, grid=(R // tr,), in_specs=[pl.BlockSpec((k, tr, C), lambda i: (0, i, 0))],
        out_specs=pl.BlockSpec((tr, C), lambda i: (i, 0)), out_shape=jax.ShapeDtypeStruct((R, C), out_dtype),
        compiler_params=_cp(("parallel",)),
    )(stack)


def _adamw(w, m, v, ga, gb, name, after=None):
    R, C = w.shape
    tr = _row_tile(R, F32, 128)
    gs = [ga] if gb is None else [ga, gb]
    extra = [] if after is None else [after]

    def body(*refs):
        w_ref, m_ref, v_ref = refs[:3]
        g = refs[3][...].astype(F32)
        if gb is not None:
            g = g + refs[4][...].astype(F32)
        g_ref, d_ref, nm_ref, nv_ref = refs[-4:]
        nm = ADAM_B1 * m_ref[...] + (1.0 - ADAM_B1) * g
        nv = ADAM_B2 * v_ref[...] + (1.0 - ADAM_B2) * (g * g)
        m_hat = nm / (1.0 - ADAM_B1 ** ADAM_STEP)
        v_hat = nv / (1.0 - ADAM_B2 ** ADAM_STEP)
        g_ref[...] = g
        d_ref[...] = -ADAM_LR * (m_hat / (jnp.sqrt(v_hat) + ADAM_EPS) + ADAM_WD * w_ref[...])
        nm_ref[...] = nm
        nv_ref[...] = nv

    spec = pl.BlockSpec((tr, C), lambda i: (i, 0))
    return pl.pallas_call(
        body, name=name, grid=(R // tr,), in_specs=[spec] * (3 + len(gs)) + [_ANY] * len(extra), out_specs=[spec] * 4,
        out_shape=[jax.ShapeDtypeStruct((R, C), F32)] * 4, compiler_params=_cp(("parallel",)),
    )(w, m, v, *gs, *extra)


def _pack(arrs):
    rows = []
    for a in arrs:
        flat = a.reshape(-1)
        rows.append(jnp.pad(flat, (0, -flat.shape[0] % 1024)).reshape(-1, 1024))
    p = jnp.concatenate(rows, axis=0)
    return jnp.pad(p, ((0, -p.shape[0] % 8), (0, 0)))


def _unpack(p, shapes):
    out, r = [], 0
    for s in shapes:
        n = 1
        for d in s:
            n *= d
        nr = -(-n // 1024)
        out.append(p[r:r + nr].reshape(-1)[:n].reshape(s))
        r += nr
    return out


BIG = ("f1_gate", "f1_up", "f1_down", "w_in", "w_out", "w_xq", "w_xkv", "w_xo", "f2_gate", "f2_up", "f2_down")
BIG_KIND = {"f1_gate": "col", "f1_up": "col", "f1_down": "row", "w_in": "row", "w_out": "row", "w_xq": "row", "w_xkv": "col",
            "w_xo": "row", "f2_gate": "col", "f2_up": "col", "f2_down": "row"}
LORA = ("rw_decay_up", "rw_aaa_up", "rw_gate_up")
WEIGHTS = ("f1_norm", "f1_gate", "f1_up", "f1_down", "mix_norm", "w_in", "b_in_attn", "rw_mu", "rw_w0", "rw_decay_up", "rw_a0",
           "rw_aaa_up", "rw_gate_up", "rw_k_k", "rw_k_a", "rw_r_k", "rw_lnx_w", "rw_lnx_b", "attn_sinks", "w_out", "b_out", "xa_norm",
           "mem_norm", "w_xq", "w_xkv", "w_xo", "f2_norm", "f2_gate", "f2_up", "f2_down", "final_norm")
SMALL = tuple(n for n in WEIGHTS if n not in BIG)
GROUP_ORDER = ("f1", "f1d", "mix", "xattn", "f2")
GROUPS = {"f1": ("f1_gate", "f1_up"), "f1d": ("f1_down",), "mix": ("w_in", "w_out") + LORA, "xattn": ("w_xq", "w_xkv", "w_xo"),
          "f2": ("f2_gate", "f2_up", "f2_down")}


def kernel(x, mem, f1_norm, f1_gate, f1_up, f1_down, mix_norm, w_in, b_in_attn, rw_mu, rw_w0, rw_decay_up, rw_a0, rw_aaa_up, rw_gate_up, rw_k_k, rw_k_a, rw_r_k, rw_lnx_w, rw_lnx_b, attn_sinks, w_out, b_out, xa_norm, mem_norm, w_xq, w_xkv, w_xo, f2_norm, f2_gate, f2_up, f2_down, final_norm, loss_target, m_f1_norm, m_f1_gate, m_f1_up, m_f1_down, m_mix_norm, m_w_in, m_b_in_attn, m_rw_mu, m_rw_w0, m_rw_decay_up, m_rw_a0, m_rw_aaa_up, m_rw_gate_up, m_rw_k_k, m_rw_k_a, m_rw_r_k, m_rw_lnx_w, m_rw_lnx_b, m_attn_sinks, m_w_out, m_b_out, m_xa_norm, m_mem_norm, m_w_xq, m_w_xkv, m_w_xo, m_f2_norm, m_f2_gate, m_f2_up, m_f2_down, m_final_norm, v_f1_norm, v_f1_gate, v_f1_up, v_f1_down, v_mix_norm, v_w_in, v_b_in_attn, v_rw_mu, v_rw_w0, v_rw_decay_up, v_rw_a0, v_rw_aaa_up, v_rw_gate_up, v_rw_k_k, v_rw_k_a, v_rw_r_k, v_rw_lnx_w, v_rw_lnx_b, v_attn_sinks, v_w_out, v_b_out, v_xa_norm, v_mem_norm, v_w_xq, v_w_xkv, v_w_xo, v_f2_norm, v_f2_gate, v_f2_up, v_f2_down, v_final_norm):
    a = dict(locals())
    w = {n: a[n] for n in WEIGHTS}
    m = {n: a["m_" + n] for n in WEIGHTS}
    v = {n: a["v_" + n] for n in WEIGHTS}
    sq = lambda t: t.reshape(t.shape[-2:]) if t.ndim == 3 else t.reshape(1, -1)

    local_name = lambda n: "w_inT" if n == "w_in" else n
    kind_of = lambda n: BIG_KIND.get(n, "col")
    payload = lambda n: sq(w[n]).T if n == "w_in" else sq(w[n]) if n in LORA else sq(w[n]).astype(BF16)
    gathers = {}

    def start_gather(name, grps, after):
        shards = [payload(n) for g in grps for n in GROUPS[g]]
        kinds = [kind_of(n) for g in grps for n in GROUPS[g]]
        groups, at = [], 0
        for g in grps:
            groups.append(list(range(at, at + len(GROUPS[g]))))
            at += len(GROUPS[g])
        sems, src_thru, land_thru, token = _gather_start(name, shards, kinds, groups, after)
        for gi, g in enumerate(grps):
            gathers[g] = (sems[3 * gi:3 * gi + 3], [src_thru[i] for i in groups[gi]], [land_thru[i] for i in groups[gi]],
                          [kinds[i] for i in groups[gi]], token)

    def get_w(grp, after):
        if grp == GROUP_ORDER[0]:
            got = _gather_now("gather_" + grp, [payload(n) for n in GROUPS[grp]], [kind_of(n) for n in GROUPS[grp]])
            start_gather("gather_start_rest", GROUP_ORDER[1:], got[0])
            out = {"_after": gathers[GROUP_ORDER[1]][4]}
        else:
            g_sems, g_src, g_land, g_kinds, _ = gathers[grp]
            got = _gather_wait("gather_wait_" + grp, g_sems, g_src, g_land, g_kinds, after)
            got = _swap_halves("gather_swap_" + grp, got, [s.shape for s in g_src], g_kinds)
            out = {}
        out.update({local_name(n): f for n, f in zip(GROUPS[grp], got)})
        return out

    in_flight = []

    def put_g(label, gw):
        names = list(gw)
        *flight, sent = _scatter_start("scatter_start_" + label, [gw[n] for n in names], [kind_of(n) for n in names])
        in_flight.append((label, names, flight))
        return sent

    P = {n: sq(w[n]) for n in SMALL if n not in LORA}
    P["attn_sinks"] = jnp.pad(P["attn_sinks"], ((0, 0), (0, 128 - P["attn_sinks"].shape[1])))
    P["rw_r_k"] = w["rw_r_k"].reshape(1, RW_W)
    loss_part, grad_x, gs = _local_step(x[0], mem[0], loss_target[0], get_w, P, put_g)
    loss = lax.psum(loss_part[0, 0], ("x", "y", "c"))

    gs["attn_sinks"] = gs["attn_sinks"][:, :16]
    small_flight = _small_start(_pack([gs[n] for n in SMALL]), grad_x)

    out, after = {}, small_flight[-1]
    for label, names, (g_sems, g_thru, l_thru) in in_flight:
        stacks = _scatter_wait("scatter_wait_" + label, g_sems, g_thru, l_thru, [kind_of(n) for n in names], after)
        partial = [_sum_slots(s, "sum_chips_" + n, F32 if n == "w_in" else BF16) for s, n in zip(stacks, names)]
        sibling = _swap_with_sibling(partial, "swap_" + label)
        chain = None
        for n, pa, sb in zip(names, partial, sibling):
            if n == "w_in":
                pa, sb = pa.T, sb.T
            out[n] = _adamw(sq(w[n]), sq(m[n]), sq(v[n]), pa, sb, "adamw_" + n, after=chain)
            chain = out[n][1]
        after = chain

    gsum = _sum_slots(_small_wait(*small_flight[:-1], after), "sum_small")
    g_small = dict(zip(SMALL, _unpack(gsum, [gs[n].shape for n in SMALL])))
    shard = 2 * lax.axis_index("x") + lax.axis_index("y")
    for n in LORA:
        cols = w[n].shape[-1]
        g_small[n] = lax.dynamic_slice_in_dim(g_small[n], shard * cols, cols, axis=1)

    flat = lambda d: _pack([d[n] for n in SMALL])
    res = _adamw(flat(w), flat(m), flat(v), _pack([g_small[n] for n in SMALL]), None, "adamw_small")
    shapes = [w[n].shape for n in SMALL]
    for k, p in enumerate(res):
        for n, t in zip(SMALL, _unpack(p, shapes)):
            out.setdefault(n, [None] * 4)[k] = t
    outs = [loss, grad_x.reshape(x.shape)]
    for k in range(4):
        outs += [out[n][k].reshape(w[n].shape) for n in WEIGHTS]
    return tuple(outs)
```

```python
import functools

import jax
import jax.numpy as jnp
from jax import lax
from jax.experimental import pallas as pl
from jax.experimental.pallas import tpu as pltpu

F32, BF16 = jnp.float32, jnp.bfloat16
MESH = pl.DeviceIdType.MESH

HEAD = 64
RW_HEADS = 16
RW_W = 1024
SWA_W = 1024
KV_W = 128
DECAY_LORA, AAA_LORA, GATE_LORA = 64, 64, 160
LORA_W = DECAY_LORA + AAA_LORA + GATE_LORA
SHIFT_COLS = 3 * RW_W + LORA_W
XH = 4
XHD = 512
MEM_LEN = 256
WINDOW = 128
GN_EPS = 64e-5
RMS_EPS = 1e-6
NEG_INF = -1e30
ADAM_LR, ADAM_B1, ADAM_B2, ADAM_EPS, ADAM_WD, ADAM_STEP = 0.001, 0.9, 0.999, 1e-08, 0.01, 10

VMEM_LIMIT = 56 * 1024 * 1024


def _cp(sem=None, **kw):
    return pltpu.CompilerParams(dimension_semantics=sem, vmem_limit_bytes=VMEM_LIMIT, **kw)


def _pick(dim, target):
    if dim <= target:
        return dim
    best = None
    for t in range(128, target + 1, 128):
        if dim % t == 0:
            best = t
    assert best is not None, (dim, target)
    return best


_DIMS = {"nn": (((1,), (0,)), ((), ())), "nt": (((1,), (1,)), ((), ())), "tn": (((0,), (0,)), ((), ()))}


def _mm(a, b, mode, name, out_dtype=F32, alpha=1.0, res=None, bias=None, tm=1024, tn=1024, tk=2048, after=None):
    if mode == "nn":
        (M, K), (K2, N) = a.shape, b.shape
    elif mode == "nt":
        (M, K), (N, K2) = a.shape, b.shape
    else:
        (K, M), (K2, N) = a.shape, b.shape
    assert K == K2, (name, a.shape, b.shape)
    tm, tn, tk = _pick(M, tm), _pick(N, tn), _pick(K, tk)
    nk = K // tk
    a_spec = pl.BlockSpec((tk, tm), lambda i, j, k: (k, i)) if mode == "tn" else pl.BlockSpec((tm, tk), lambda i, j, k: (i, k))
    b_spec = pl.BlockSpec((tn, tk), lambda i, j, k: (j, k)) if mode == "nt" else pl.BlockSpec((tk, tn), lambda i, j, k: (k, j))
    o_spec = pl.BlockSpec((tm, tn), lambda i, j, k: (i, j))
    ins, specs = [a, b], [a_spec, b_spec]
    if res is not None:
        ins.append(res)
        specs.append(o_spec)
    if bias is not None:
        ins.append(bias)
        specs.append(pl.BlockSpec((1, tn), lambda i, j, k: (0, j)))
    if after is not None:
        ins.append(after)
        specs.append(pl.BlockSpec(memory_space=pl.ANY))
    dims = _DIMS[mode]

    def body(*refs):
        a_ref, b_ref = refs[0], refs[1]
        part = lax.dot_general(a_ref[...].astype(BF16), b_ref[...].astype(BF16), dims, preferred_element_type=F32)

        def finish(o, o_ref):
            if alpha != 1.0:
                o = o * alpha
            p = 2
            if res is not None:
                o = o + refs[p][...].astype(F32)
                p += 1
            if bias is not None:
                o = o + refs[p][...]
            o_ref[...] = o.astype(out_dtype)

        if nk == 1:
            finish(part, refs[-1])
            return
        o_ref, acc_ref = refs[-2], refs[-1]
        k = pl.program_id(2)

        @pl.when(k == 0)
        def _():
            acc_ref[...] = part

        @pl.when(k > 0)
        def _():
            acc_ref[...] += part

        @pl.when(k == nk - 1)
        def _():
            finish(acc_ref[...], o_ref)

    return pl.pallas_call(
        body, name=name, grid=(M // tm, N // tn, nk), in_specs=specs, out_specs=o_spec,
        out_shape=jax.ShapeDtypeStruct((M, N), out_dtype), scratch_shapes=[pltpu.VMEM((tm, tn), F32)] * (nk > 1),
        compiler_params=_cp(("parallel", "parallel", "arbitrary")),
    )(*ins)


def _rows(fn, name, T, tm, tiled, full, out_tiled, out_acc, extra=(), reverse=False, scratch=()):
    n = T // tm
    idx = (lambda i: n - 1 - i) if reverse else (lambda i: i)
    in_specs = [pl.BlockSpec((tm, a.shape[1]), lambda i: (idx(i), 0)) for a in tiled]
    in_specs += [mk(idx) for _, mk in extra]
    in_specs += [pl.BlockSpec(a.shape, lambda i, nd=a.ndim: (0,) * nd) for a in full]
    out_specs = [pl.BlockSpec((tm, c), lambda i: (idx(i), 0)) for c, _ in out_tiled]
    out_specs += [pl.BlockSpec(s, lambda i, nd=len(s): (0,) * nd) for s, _ in out_acc]
    out_shape = [jax.ShapeDtypeStruct((T, c), d) for c, d in out_tiled] + [jax.ShapeDtypeStruct(s, d) for s, d in out_acc]
    n_in = len(tiled) + len(extra) + len(full)
    n_t, n_a = len(out_tiled), len(out_acc)

    def body(*refs):
        step = pl.program_id(0)
        vals = [r[...] for r in refs[:n_in]]
        outs = fn(idx(step), *vals, *refs[n_in + n_t + n_a:])
        for r, v in zip(refs[n_in:n_in + n_t], outs[:n_t]):
            r[...] = v.astype(r.dtype)
        for r, v in zip(refs[n_in + n_t:n_in + n_t + n_a], outs[n_t:]):
            @pl.when(step == 0)
            def _(r=r):
                r[...] = jnp.zeros_like(r)

            r[...] += v

    return pl.pallas_call(
        body, name=name, grid=(n,), in_specs=in_specs, out_specs=out_specs, out_shape=out_shape,
        scratch_shapes=list(scratch), compiler_params=_cp(("arbitrary",)),
    )(*tiled, *[a for a, _ in extra], *full)


def _rms(x, g):
    return x * lax.rsqrt(jnp.mean(x * x, axis=-1, keepdims=True) + RMS_EPS) * g


def _rms_fwd(x, g, name, tm=256):
    (h,) = _rows(lambda i, x, g: (_rms(x, g),), name, x.shape[0], min(tm, x.shape[0]), [x], [g], [(x.shape[1], BF16)], [])
    return h


def _rms_bwd(x, g, dh, dres, name, tm=256):
    D = x.shape[1]

    def fn(i, x, dh, dres, g):
        _, vjp = jax.vjp(_rms, x, g)
        dx, dg = vjp(dh.astype(F32))
        dx = dx + dres
        return dx, dg, jnp.sum(dx, axis=0, keepdims=True)

    return _rows(fn, name, x.shape[0], tm, [x, dh, dres], [g], [(D, F32)], [((1, D), F32), ((1, D), F32)])


def _ffn_up(h, wg, wu, name, tm=1024, tn=512, after=None):
    (M, K), N = h.shape, wg.shape[1]
    tm, tn = _pick(M, tm), _pick(N, tn)

    def body(*refs):
        h_ref, wg_ref, wu_ref = refs[:3]
        g_ref, u_ref, a_ref = refs[-3:]
        hb = h_ref[...].astype(BF16)
        g = jnp.dot(hb, wg_ref[...].astype(BF16), preferred_element_type=F32)
        u = jnp.dot(hb, wu_ref[...].astype(BF16), preferred_element_type=F32)
        g_ref[...] = g
        u_ref[...] = u
        a_ref[...] = (g * jax.nn.sigmoid(g) * u).astype(BF16)

    o_spec = pl.BlockSpec((tm, tn), lambda i, j: (i, j))
    w_spec = pl.BlockSpec((K, tn), lambda i, j: (0, j))
    extra = [] if after is None else [after]
    return pl.pallas_call(
        body, name=name, grid=(M // tm, N // tn),
        in_specs=[pl.BlockSpec((tm, K), lambda i, j: (i, 0)), w_spec, w_spec] + [pl.BlockSpec(memory_space=pl.ANY)] * len(extra),
        out_specs=[o_spec] * 3, out_shape=[jax.ShapeDtypeStruct((M, N), F32)] * 2 + [jax.ShapeDtypeStruct((M, N), BF16)],
        compiler_params=_cp(("parallel", "parallel")),
    )(h, wg, wu, *extra)


def _ffn_dact(dxo, wd, g, u, name, tm=1024, tn=512):
    (M, K), N = dxo.shape, wd.shape[0]
    tm, tn = _pick(M, tm), _pick(N, tn)

    def body(dx_ref, wd_ref, g_ref, u_ref, dg_ref, du_ref):
        da = 0.5 * lax.dot_general(dx_ref[...].astype(BF16), wd_ref[...].astype(BF16), _DIMS["nt"], preferred_element_type=F32)
        g = g_ref[...]
        s = jax.nn.sigmoid(g)
        dg_ref[...] = (da * u_ref[...] * (s * (1.0 + g * (1.0 - s)))).astype(BF16)
        du_ref[...] = (da * (g * s)).astype(BF16)

    t_spec = pl.BlockSpec((tm, tn), lambda i, j: (i, j))
    return pl.pallas_call(
        body, name=name, grid=(M // tm, N // tn),
        in_specs=[pl.BlockSpec((tm, K), lambda i, j: (i, 0)), pl.BlockSpec((tn, K), lambda i, j: (j, 0)), t_spec, t_spec],
        out_specs=[t_spec, t_spec], out_shape=[jax.ShapeDtypeStruct((M, N), BF16)] * 2, compiler_params=_cp(("parallel", "parallel")),
    )(dxo, wd, g, u)


def _ffn_fwd(x, gain, wg, wu, wd, tag, after=None):
    h = _rms_fwd(x, gain, tag + "_norm")
    G, U, A = _ffn_up(h, wg, wu, tag + "_up", after=after)
    xo = _mm(A, wd(A) if callable(wd) else wd, "nn", tag + "_down", alpha=0.5, res=x)
    return xo, (h, G, U, A)


def _ffn_bwd(x, gain, wg, wu, wd, saved, dxo, tag, send):
    h, G, U, A = saved
    dwd = _mm(A, dxo, "tn", tag + "_dwd", out_dtype=BF16, alpha=0.5)
    sent = send(tag + "_down", {tag + "_down": dwd})
    dG, dU = _ffn_dact(dxo, wd, G, U, tag + "_dact")
    dwu = _mm(h, dU, "tn", tag + "_dwu", out_dtype=BF16, after=sent)
    sent = send(tag + "_up", {tag + "_up": dwu})
    dwg = _mm(h, dG, "tn", tag + "_dwg", out_dtype=BF16, after=sent)
    sent = send(tag + "_gate", {tag + "_gate": dwg})
    dh = _mm(dG, wg, "nt", tag + "_dh_g", after=sent)
    dh = _mm(dU, wu, "nt", tag + "_dh_u", res=dh)
    dx, dgain, _ = _rms_bwd(x, gain, dh, dxo, tag + "_norm_bwd")
    return dx, dgain


def _segsum64_impl(x):
    r = lax.broadcasted_iota(jnp.int32, (128, 128), 0) // HEAD
    c = lax.broadcasted_iota(jnp.int32, (128, 128), 1) // HEAD
    ones = (r == c).astype(BF16)
    hi = x.astype(BF16)
    lo = (x - hi.astype(F32)).astype(BF16)
    outs = []
    for q in range(x.shape[1] // 128):
        sl = slice(q * 128, (q + 1) * 128)
        outs.append(jnp.dot(hi[:, sl], ones, preferred_element_type=F32) + jnp.dot(lo[:, sl], ones, preferred_element_type=F32))
    return outs[0] if len(outs) == 1 else jnp.concatenate(outs, axis=1)


@jax.custom_vjp
def _segsum64(x):
    return _segsum64_impl(x)


_segsum64.defvjp(lambda x: (_segsum64_impl(x), None), lambda _, ct: (_segsum64_impl(ct),))


def _swap32(x):
    lane = lax.broadcasted_iota(jnp.int32, (x.shape[0], 128), 1)
    outs = [jnp.take_along_axis(x[:, q * 128:(q + 1) * 128], lane ^ 32, axis=1) for q in range(x.shape[1] // 128)]
    return outs[0] if len(outs) == 1 else jnp.concatenate(outs, axis=1)


def _tree_sum(xs):
    xs = list(xs)
    while len(xs) > 1:
        nxt = [xs[i] + xs[i + 1] for i in range(0, len(xs) - 1, 2)]
        if len(xs) % 2:
            nxt.append(xs[-1])
        xs = nxt
    return xs[0]


class _Acc:
    def __init__(self, ways=4):
        self.parts = [None] * ways

    def add(self, i, term):
        k = i % len(self.parts)
        self.parts[k] = term if self.parts[k] is None else self.parts[k] + term

    def total(self):
        return _tree_sum([p for p in self.parts if p is not None])


def _softplus(x):
    return jnp.maximum(x, 0.0) + jnp.log(1.0 + jnp.exp(-jnp.abs(x)))


def _pre_core(k, da, gd, w0, a0, k_k, k_a, w_da, gate_up):
    lane = lax.broadcasted_iota(jnp.int32, da.shape, 1)
    w_da = w_da.astype(BF16)
    l1 = jnp.dot(jnp.where(lane < DECAY_LORA, jnp.tanh(da), 0.0).astype(BF16), w_da, preferred_element_type=F32)
    l2 = jnp.dot(jnp.where(lane >= DECAY_LORA, da, 0.0).astype(BF16), w_da, preferred_element_type=F32)
    wlog = -_softplus(-(w0 + l1)) - 0.5
    decay = jnp.exp(-jnp.exp(wlog))
    a = jax.nn.sigmoid(a0 + l2)
    g = jnp.dot(jax.nn.sigmoid(gd).astype(BF16), gate_up.astype(BF16), preferred_element_type=F32)
    kk = k * k_k
    kkn = kk / jnp.maximum(jnp.sqrt(_segsum64(kk * kk)), 1e-12)
    k2 = k * (1.0 + (a - 1.0) * k_a)
    return decay, k2, -kkn, kkn * a, g


def _pre_shift(i, zr, zl, zr8, zl8, mu, mul):
    live = (i > 0).astype(F32)
    dz = _shift_down(zr, zr8[7:8, :] * live) - zr
    dzl = _shift_down(zl, zl8[7:8, :] * live) - zl
    return zr + dz * mu, zl + dzl * mul, dz, dzl


def _shift_down(x, first_row):
    rolled = pltpu.roll(x, 1, 0)
    row = lax.broadcasted_iota(jnp.int32, x.shape, 0)
    return jnp.where(row == 0, first_row, rolled)


def _shift_up(x, last_row):
    rolled = pltpu.roll(x, x.shape[0] - 1, 0)
    row = lax.broadcasted_iota(jnp.int32, x.shape, 0)
    return jnp.where(row == x.shape[0] - 1, last_row, rolled)


def _prev_rows_spec(tm, cols):
    return lambda idx: pl.BlockSpec((8, cols), lambda i: (jnp.maximum(idx(i) * (tm // 8) - 1, 0), 0))


def _rwkv_pre(p_rkv, p_lora, params, tm=256):
    T = p_rkv.shape[0]

    def fn(i, zr, zl, zr8, zl8, mu, mul, *ps):
        z, z2, _, _ = _pre_shift(i, zr, zl, zr8, zl8, mu, mul)
        decay, k2, an, bn, g = _pre_core(z[:, RW_W:2 * RW_W], z2[:, :128], z2[:, 128:], *ps)
        return z[:, :RW_W], decay, k2, z[:, 2 * RW_W:], an, bn, g

    extra = [(p_rkv, _prev_rows_spec(tm, 3 * RW_W)), (p_lora, _prev_rows_spec(tm, LORA_W))]
    return _rows(fn, "rwkv_pre", T, tm, [p_rkv, p_lora], list(params), [(RW_W, F32)] * 7, [], extra=extra)


def _rwkv_pre_bwd(p_rkv, p_lora, params, cts, tm=256):
    T = p_rkv.shape[0]
    n = T // tm

    def fn(i, zr, zl, cr, cdec, ck2, cv, can, cbn, cg, cr_b, ck2_b, cv_b, zr8, zl8, mu, mul, *rest):
        ps, (car, carl) = rest[:-2], rest[-2:]
        cr, ck2, cv = cr + cr_b, ck2 + ck2_b, cv + cv_b
        z, z2, dif, difl = _pre_shift(i, zr, zl, zr8, zl8, mu, mul)
        _, vjp = jax.vjp(_pre_core, z[:, RW_W:2 * RW_W], z2[:, :128], z2[:, 128:], *ps)
        dk, dda, dgd, *dps = vjp((cdec, ck2, can, cbn, cg))
        dz = jnp.concatenate([cr, dk, cv], axis=1)
        dz2 = jnp.concatenate([dda, dgd], axis=1)
        dzp, dzlp = dz * mu, dz2 * mul

        @pl.when(i == n - 1)
        def _():
            car[...] = jnp.zeros_like(car)
            carl[...] = jnp.zeros_like(carl)

        d_rkv = dz - dzp + _shift_up(dzp, car[0:1, :])
        d_lora = dz2 - dzlp + _shift_up(dzlp, carl[0:1, :])
        car[0:1, :] = dzp[0:1, :]
        carl[0:1, :] = dzlp[0:1, :]
        return (d_rkv, d_lora, jnp.sum(dz * dif, axis=0, keepdims=True), jnp.sum(dz2 * difl, axis=0, keepdims=True), *dps)

    extra = [(p_rkv, _prev_rows_spec(tm, 3 * RW_W)), (p_lora, _prev_rows_spec(tm, LORA_W))]
    acc = [(p.shape, F32) for p in params]
    return _rows(fn, "rwkv_pre_bwd", T, tm, [p_rkv, p_lora, *cts], list(params), [(3 * RW_W, BF16), (LORA_W, BF16)], acc,
                 extra=extra, reverse=True, scratch=[pltpu.VMEM((8, 3 * RW_W), F32), pltpu.VMEM((8, LORA_W), F32)])


def _post_core(y, r, k2, v, g, lw, lb, rk):
    mu = _segsum64(y) * (1.0 / HEAD)
    yc = y - mu
    var = _segsum64(yc * yc) * (1.0 / HEAD)
    yn = yc * lax.rsqrt(var + GN_EPS) * lw + lb
    return (yn + _segsum64(r * k2 * rk) * v) * g


def _rwkv_post(y, r, k2, v, g, lw, lb, rk, tm=256):
    (o,) = _rows(lambda i, *a: (_post_core(*a),), "rwkv_post", y.shape[0], tm, [y, r, k2, v, g], [lw, lb, rk], [(RW_W, BF16)], [])
    return o


def _rwkv_post_bwd(y, r, k2, v, g, lw, lb, rk, do, tm=256):
    def fn(i, y, r, k2, v, g, do, lw, lb, rk):
        _, vjp = jax.vjp(_post_core, y, r, k2, v, g, lw, lb, rk)
        return vjp(do.astype(F32))

    return _rows(fn, "rwkv_post_bwd", y.shape[0], tm, [y, r, k2, v, g, do], [lw, lb, rk], [(RW_W, F32)] * 5, [((1, RW_W), F32)] * 3)


SCAN_L = 32


def _to_perm(x):
    T = x.shape[0]
    return x.reshape(T, RW_HEADS, HEAD).transpose(0, 2, 1).reshape(T, 8, 128)


def _from_perm(x):
    T = x.shape[0]
    return x.reshape(T, HEAD, RW_HEADS).transpose(0, 2, 1).reshape(T, RW_W)


def _as_tile(p):
    lane = lax.broadcasted_iota(jnp.int32, (8, 128), 1)
    return jnp.take_along_axis(p, (lane % 8) * 16 + lane // 8, axis=1)


def _as_perm(t):
    lane = lax.broadcasted_iota(jnp.int32, (8, 128), 1)
    return jnp.take_along_axis(t, (lane % 16) * 8 + lane // 16, axis=1)


def _expander(srcs):
    s = lax.broadcasted_iota(jnp.int32, (8, 128), 0)
    lane = lax.broadcasted_iota(jnp.int32, (8, 128), 1)
    idx = 16 * s + lane // 8

    def expand(t, e_ref):
        for m, r in enumerate(srcs):
            for g in range(8):
                row = jnp.broadcast_to(r[t, pl.ds(g, 1), :], (8, 128))
                e_ref[m, g * 8:(g + 1) * 8, :] = jnp.take_along_axis(row, idx, axis=1)

    return expand


def _ck_a_to_b(ck):
    n = ck.shape[0]
    return ck.reshape(n, 8, 8, 8, RW_HEADS, 8).transpose(0, 3, 5, 1, 4, 2).reshape(n, HEAD, 8, 128)


def _scan_fwd(xes, vi):
    T, L = vi.shape[0], SCAN_L
    nch = T // L

    def body(*refs):
        xr, (vi_ref, yi_ref, sa_ref, ck_ref, st_ref, e0, e1) = refs[:5], refs[5:]

        @pl.when(pl.program_id(0) == 0)
        def _():
            st_ref[...] = jnp.zeros_like(st_ref)

        ck_ref[0] = st_ref[...]
        expand = _expander(xr)
        expand(0, e0)

        def step(t, e_ref):
            v = _as_tile(vi_ref[t])
            row = lambda m, j: jnp.broadcast_to(e_ref[m, pl.ds(j, 1), :], (8, 128))
            sa = _Acc()
            for j in range(HEAD):
                sa.add(j, st_ref[j] * row(0, j))
            sa = sa.total()
            sa_ref[t] = _as_perm(sa)
            y = _Acc()
            for j in range(HEAD):
                s = st_ref[j] * row(1, j) + row(2, j) * sa + row(3, j) * v
                st_ref[j] = s
                y.add(j, s * row(4, j))
            yi_ref[t] = _as_perm(y.total())

        def pair(p, carry):
            t = 2 * p
            expand(t + 1, e1)
            step(t, e0)
            expand(jnp.minimum(t + 2, L - 1), e0)
            step(t + 1, e1)
            return carry

        lax.fori_loop(0, L // 2, pair, 0)

    tile = pl.BlockSpec((L, 8, 128), lambda c: (c, 0, 0))
    return pl.pallas_call(
        body, name="rwkv_scan_fwd", grid=(nch,), in_specs=[tile] * 6,
        out_specs=[tile, tile, pl.BlockSpec((1, HEAD, 8, 128), lambda c: (c, 0, 0, 0))],
        out_shape=[jax.ShapeDtypeStruct((T, 8, 128), F32)] * 2 + [jax.ShapeDtypeStruct((nch, HEAD, 8, 128), F32)],
        scratch_shapes=[pltpu.VMEM((HEAD, 8, 128), F32)] + [pltpu.VMEM((5, HEAD, 128), F32)] * 2, compiler_params=_cp(("arbitrary",)),
    )(*xes, vi)


def _scan_bwd_a(xes, dyi):
    T, L = dyi.shape[0], SCAN_L
    nch = T // L

    def body(*refs):
        xr, (dy_ref, dsa_ref, dv_ref, g_ref, e0, e1) = refs[:5], refs[5:]

        @pl.when(pl.program_id(0) == 0)
        def _():
            g_ref[...] = jnp.zeros_like(g_ref)

        expand = _expander(xr)
        expand(L - 1, e0)

        def step(t, e_ref):
            dy = _as_tile(dy_ref[t])
            row = lambda m, j: jnp.broadcast_to(e_ref[m, pl.ds(j, 1), :], (8, 128))
            dsa, dv = _Acc(), _Acc()
            for j in range(HEAD):
                g = g_ref[j] + row(4, j) * dy
                g_ref[j] = g
                dsa.add(j, g * row(2, j))
                dv.add(j, g * row(3, j))
            dsa = dsa.total()
            dsa_ref[t] = _as_perm(dsa)
            dv_ref[t] = _as_perm(dv.total())
            for j in range(HEAD):
                g_ref[j] = g_ref[j] * row(1, j) + row(0, j) * dsa

        def pair(p, carry):
            t = L - 1 - 2 * p
            expand(t - 1, e1)
            step(t, e0)
            expand(jnp.maximum(t - 2, 0), e0)
            step(t - 1, e1)
            return carry

        lax.fori_loop(0, L // 2, pair, 0)

    tile = pl.BlockSpec((L, 8, 128), lambda c: (nch - 1 - c, 0, 0))
    return pl.pallas_call(
        body, name="rwkv_scan_bwd_a", grid=(nch,), in_specs=[tile] * 6, out_specs=[tile, tile],
        out_shape=[jax.ShapeDtypeStruct((T, 8, 128), F32)] * 2,
        scratch_shapes=[pltpu.VMEM((HEAD, 8, 128), F32)] + [pltpu.VMEM((5, HEAD, 128), F32)] * 2, compiler_params=_cp(("arbitrary",)),
    )(*xes, dyi)


def _scan_bwd_b(xts, ies, ckb):
    T, L = xts[0].shape[0], SCAN_L
    nch = T // L

    def body(*refs):
        xr, er, ck_ref, dj, (hist, g_ref, e0, e1) = refs[:5], refs[5:9], refs[9], refs[10:15], refs[15:]

        @pl.when(pl.program_id(0) == 0)
        def _():
            g_ref[...] = jnp.zeros_like(g_ref)

        hist[0] = ck_ref[0]
        expand_vs = _expander(er[:2])
        expand = _expander(er)
        expand_vs(0, e0)

        def fstep(t, e_ref):
            w, B, k = _as_tile(xr[1][t]), _as_tile(xr[2][t]), _as_tile(xr[3][t])
            row = lambda m, i: jnp.broadcast_to(e_ref[m, pl.ds(i, 1), :], (8, 128))
            for i in range(HEAD):
                hist[t + 1, i] = hist[t, i] * w + row(1, i) * B + row(0, i) * k

        def fpair(p, carry):
            t = 2 * p
            expand_vs(t + 1, e1)
            fstep(t, e0)
            expand_vs(jnp.minimum(t + 2, L - 1), e0)
            fstep(t + 1, e1)
            return carry

        lax.fori_loop(0, L // 2, fpair, 0)
        expand(L - 1, e0)

        def bstep(t, e_ref):
            A, w, r = _as_tile(xr[0][t]), _as_tile(xr[1][t]), _as_tile(xr[4][t])
            row = lambda m, i: jnp.broadcast_to(e_ref[m, pl.ds(i, 1), :], (8, 128))
            acc = [_Acc() for _ in range(5)]
            for i in range(HEAD):
                dy_i, dsa_i = row(2, i), row(3, i)
                g = g_ref[i] + dy_i * r
                sp = hist[t, i]
                acc[4].add(i, hist[t + 1, i] * dy_i)
                acc[1].add(i, g * sp)
                acc[2].add(i, g * row(1, i))
                acc[3].add(i, g * row(0, i))
                acc[0].add(i, sp * dsa_i)
                g_ref[i] = g * w + dsa_i * A
            for m in range(5):
                dj[m][t] = _as_perm(acc[m].total())

        def bpair(p, carry):
            t = L - 1 - 2 * p
            expand(t - 1, e1)
            bstep(t, e0)
            expand(jnp.maximum(t - 2, 0), e0)
            bstep(t - 1, e1)
            return carry

        lax.fori_loop(0, L // 2, bpair, 0)

    tile = pl.BlockSpec((L, 8, 128), lambda c: (nch - 1 - c, 0, 0))
    return pl.pallas_call(
        body, name="rwkv_scan_bwd_b", grid=(nch,),
        in_specs=[tile] * 9 + [pl.BlockSpec((1, HEAD, 8, 128), lambda c: (nch - 1 - c, 0, 0, 0))],
        out_specs=[tile] * 5, out_shape=[jax.ShapeDtypeStruct((T, 8, 128), F32)] * 5,
        scratch_shapes=[pltpu.VMEM((L + 1, HEAD, 8, 128), F32), pltpu.VMEM((HEAD, 8, 128), F32)] + [pltpu.VMEM((4, HEAD, 128), F32)] * 2,
        compiler_params=_cp(("arbitrary",)),
    )(*xts, *ies, ckb)


SWA_COLS = SWA_W + 2 * KV_W
BLK = 128


def _swa_core(n, k2a, k2b, vla, vra, vlb, vrb, sinks, *qps):
    iq = lax.broadcasted_iota(jnp.int32, (BLK, 2 * BLK), 0)
    ik = lax.broadcasted_iota(jnp.int32, (BLK, 2 * BLK), 1)
    diff = BLK + iq - ik
    valid = (diff >= 0) & (diff < WINDOW) & ((n > 0) | (ik >= BLK))
    lane = lax.broadcasted_iota(jnp.int32, (BLK, 128), 1)
    lane1 = lax.broadcasted_iota(jnp.int32, (1, 128), 1)
    nt = (((1,), (1,)), ((), ()))
    outs = []
    for pp in range(8):
        k2, vl, vr = (k2a, vla, vra) if pp < 4 else (k2b, vlb, vrb)
        qp = qps[pp]
        o = None
        for half, vv in ((0, vl), (1, vr)):
            qh = jnp.where((lane >= HEAD) == (half == 1), qp, 0.0).astype(BF16)
            s = lax.dot_general(qh, k2.astype(BF16), nt, preferred_element_type=F32) * (HEAD ** -0.5)
            s = jnp.where(valid, s, NEG_INF)
            sink = jnp.sum(jnp.where(lane1 == 2 * pp + half, sinks, 0.0), axis=1, keepdims=True)
            m = jnp.maximum(jnp.max(s, axis=1, keepdims=True), sink)
            p = jnp.exp(s - m)
            den = jnp.sum(p, axis=1, keepdims=True) + jnp.exp(sink - m)
            oh = jnp.dot((p / den).astype(BF16), vv.astype(BF16), preferred_element_type=F32)
            o = oh if o is None else o + oh
        outs.append(o)
    return jnp.concatenate(outs, axis=1)


def _swa_prep(pc, pp, b, cq, sq, ckc, skc, ckp, skp):
    zc, zp = pc + b, pp + b
    qr = zc[:, :SWA_W] * cq + _swap32(zc[:, :SWA_W]) * sq
    kc, kp = zc[:, SWA_W:SWA_W + KV_W], zp[:, SWA_W:SWA_W + KV_W]
    kb = jnp.concatenate([kp * ckp + _swap32(kp) * skp, kc * ckc + _swap32(kc) * skc], axis=0)
    vb = jnp.concatenate([zp[:, SWA_W + KV_W:], zc[:, SWA_W + KV_W:]], axis=0)
    lane = lax.broadcasted_iota(jnp.int32, kb.shape, 1)
    left = lane < HEAD
    kbr, vbr = pltpu.roll(kb, HEAD, 1), pltpu.roll(vb, HEAD, 1)
    return (jnp.where(left, kb, kbr), jnp.where(left, kbr, kb), jnp.where(left, vb, 0.0), jnp.where(left, 0.0, vbr),
            jnp.where(left, vbr, 0.0), jnp.where(left, 0.0, vb)), [qr[:, q * 128:(q + 1) * 128] for q in range(8)]


def _swa_specs(T, tabs_q, tabs_k):
    cur = lambda c: pl.BlockSpec((BLK, c), lambda n: (n, 0))
    prev = lambda c: pl.BlockSpec((BLK, c), lambda n: (jnp.maximum(n - 1, 0), 0))
    return cur, prev


def _swa_fwd(p_swa, b, sinks, cq, sq, ck, sk):
    T = p_swa.shape[0]
    cur, prev = _swa_specs(T, None, None)

    def body(pc, pp, b_ref, s_ref, cq_r, sq_r, ckc, skc, ckp, skp, o_ref):
        ops, qps = _swa_prep(pc[...], pp[...], b_ref[...], cq_r[...], sq_r[...], ckc[...], skc[...], ckp[...], skp[...])
        o_ref[...] = _swa_core(pl.program_id(0), *ops, s_ref[...], *qps).astype(o_ref.dtype)

    full = lambda a: pl.BlockSpec(a.shape, lambda n: (0, 0))
    return pl.pallas_call(
        body, name="swa_fwd", grid=(T // BLK,),
        in_specs=[cur(SWA_COLS), prev(SWA_COLS), full(b), full(sinks), cur(SWA_W), cur(SWA_W), cur(KV_W), cur(KV_W), prev(KV_W), prev(KV_W)],
        out_specs=cur(SWA_W), out_shape=jax.ShapeDtypeStruct((T, SWA_W), BF16), compiler_params=_cp(("arbitrary",)),
    )(p_swa, p_swa, b, sinks, cq, sq, ck, sk, ck, sk)


def _swa_bwd(p_swa, b, sinks, cq, sq, ck, sk, do):
    T = p_swa.shape[0]
    nb = T // BLK
    cur = lambda c: pl.BlockSpec((BLK, c), lambda s: (nb - 1 - s, 0))
    prev = lambda c: pl.BlockSpec((BLK, c), lambda s: (jnp.maximum(nb - 2 - s, 0), 0))

    def body(pc, pp, b_ref, s_ref, cq_r, sq_r, ckc, skc, ckp, skp, do_ref, dcur, db, dsk, carry):
        step = pl.program_id(0)
        n = nb - 1 - step

        @pl.when(step == 0)
        def _():
            carry[...] = jnp.zeros_like(carry)
            db[...] = jnp.zeros_like(db)
            dsk[...] = jnp.zeros_like(dsk)

        ops, qps = _swa_prep(pc[...], pp[...], b_ref[...], cq_r[...], sq_r[...], ckc[...], skc[...], ckp[...], skp[...])
        _, vjp = jax.vjp(functools.partial(_swa_core, n), *ops, s_ref[...], *qps)
        dk2a, dk2b, dvla, dvra, dvlb, dvrb, dsinks, *dqps = vjp(do_ref[...].astype(F32))
        dqr = jnp.concatenate(dqps, axis=1)
        lane = lax.broadcasted_iota(jnp.int32, dk2a.shape, 1)
        left = lane < HEAD
        dkb = jnp.where(left, dk2a + pltpu.roll(dk2a, HEAD, 1), dk2b + pltpu.roll(dk2b, HEAD, 1))
        dvb = jnp.where(left, dvla + pltpu.roll(dvra, HEAD, 1), pltpu.roll(dvlb, HEAD, 1) + dvrb)
        dq = dqr * cq_r[...] + _swap32(dqr * sq_r[...])
        dkp, dkc = dkb[:BLK], dkb[BLK:]
        dkp = dkp * ckp[...] + _swap32(dkp * skp[...])
        dkc = dkc * ckc[...] + _swap32(dkc * skc[...])
        dc = jnp.concatenate([dq, jnp.concatenate([dkc, dvb[BLK:]], axis=1) + carry[...]], axis=1)
        carry[...] = jnp.concatenate([dkp, dvb[:BLK]], axis=1)
        dcur[...] = dc.astype(dcur.dtype)
        db[...] += jnp.sum(dc, axis=0, keepdims=True)
        dsk[...] += dsinks

    full = lambda a: pl.BlockSpec(a.shape, lambda s: (0, 0))
    return pl.pallas_call(
        body, name="swa_bwd", grid=(nb,),
        in_specs=[cur(SWA_COLS), prev(SWA_COLS), full(b), full(sinks), cur(SWA_W), cur(SWA_W), cur(KV_W), cur(KV_W), prev(KV_W), prev(KV_W),
                  cur(SWA_W)],
        out_specs=[cur(SWA_COLS), full(b), full(sinks)],
        out_shape=[jax.ShapeDtypeStruct((T, SWA_COLS), BF16), jax.ShapeDtypeStruct(b.shape, F32), jax.ShapeDtypeStruct(sinks.shape, F32)],
        scratch_shapes=[pltpu.VMEM((BLK, 2 * KV_W), F32)], compiler_params=_cp(("arbitrary",)),
    )(p_swa, p_swa, b, sinks, cq, sq, ck, sk, ck, sk, do)


def _rope_tables(T):
    inv = 10000.0 ** (-jnp.arange(0, HEAD, 2, dtype=F32) / HEAD)
    ang = jnp.arange(T, dtype=F32)[:, None] * inv[None, :]
    c = jnp.concatenate([jnp.cos(ang), jnp.cos(ang)], axis=1)
    s = jnp.concatenate([-jnp.sin(ang), jnp.sin(ang)], axis=1)
    return jnp.tile(c, (1, 16)), jnp.tile(s, (1, 16)), jnp.tile(c, (1, 2)), jnp.tile(s, (1, 2))


def _xattn_core(*qkv):
    outs = []
    for h in range(XH):
        qh, kh, vh = qkv[h], qkv[XH + h], qkv[2 * XH + h]
        s = lax.dot_general(qh.astype(BF16), kh.astype(BF16), (((1,), (1,)), ((), ())), preferred_element_type=F32) * (XHD ** -0.5)
        p = jnp.exp(s - jnp.max(s, axis=1, keepdims=True))
        p = p / jnp.sum(p, axis=1, keepdims=True)
        outs.append(jnp.dot(p.astype(BF16), vh.astype(BF16), preferred_element_type=F32))
    return jnp.concatenate(outs, axis=1)


def _xattn_split(q, kv):
    return [q[:, h * XHD:(h + 1) * XHD] for h in range(XH)] + [kv[:, h * XHD:(h + 1) * XHD] for h in range(2 * XH)]


def _xattn_fwd(q, kv, tm=256):
    (o,) = _rows(lambda i, q, kv: (_xattn_core(*_xattn_split(q, kv)),), "xattn_fwd", q.shape[0], tm, [q], [kv], [(q.shape[1], BF16)], [])
    return o


def _xattn_bwd(q, kv, do, tm=256):
    def fn(i, q, do, kv):
        _, vjp = jax.vjp(_xattn_core, *_xattn_split(q, kv))
        d = vjp(do.astype(F32))
        return jnp.concatenate(d[:XH], axis=1), jnp.concatenate(d[XH:], axis=1)

    return _rows(fn, "xattn_bwd", q.shape[0], tm, [q, do], [kv], [(q.shape[1], BF16)], [(kv.shape, F32)])


def _loss_head(x, g, tgt, tm=256):
    D = x.shape[1]

    def fn(i, x, tgt, g):
        y, vjp = jax.vjp(_rms, x, g)
        err = y - tgt
        dx, dg = vjp(err * (1.0 / D))
        part = 0.5 / D * jnp.sum(jnp.sum(err * err, axis=1, keepdims=True), axis=0, keepdims=True)
        return dx, jnp.broadcast_to(part, (1, 128)), dg

    return _rows(fn, "loss_head", x.shape[0], tm, [x, tgt], [g], [(D, F32)], [((1, 128), F32), ((1, D), F32)])


def _local_step(x, mem, tgt, get_w, P, put_g):
    T = x.shape[0]
    W = dict(get_w("f1", None))

    def f1_down(after):
        W.update(get_w("f1d", after))
        return W["f1_down"]

    x1, s1 = _ffn_fwd(x, P["f1_norm"], W["f1_gate"], W["f1_up"], f1_down, "f1", after=W.get("_after"))

    W.update(get_w("mix", x1))
    h2 = _rms_fwd(x1, P["mix_norm"], "mix_norm")
    w_rkv, w_lora, w_swa = W["w_inT"][:3 * RW_W], W["w_inT"][3 * RW_W:SHIFT_COLS], W["w_inT"][SHIFT_COLS:]
    p_rkv = _mm(h2, w_rkv, "nt", "in_rkv")
    p_lora = _mm(h2, w_lora, "nt", "in_lora")
    p_swa = _mm(h2, w_swa, "nt", "in_swa")
    w_da = jnp.concatenate([W["rw_decay_up"], W["rw_aaa_up"]], axis=0)
    pre_params = (P["rw_mu"][:, :3 * RW_W], P["rw_mu"][:, 3 * RW_W:], P["rw_w0"], P["rw_a0"], P["rw_k_k"], P["rw_k_a"], w_da,
                  W["rw_gate_up"])
    r, decay, k2, v, an, bn, g = _rwkv_pre(p_rkv, p_lora, pre_params)
    scan_vecs = (an, decay, bn, k2, r)
    xes = [_to_perm(a) for a in scan_vecs]
    v_p = _to_perm(v)
    yi, sai, ck = _scan_fwd(xes, v_p)
    y_scan = _from_perm(yi)
    y_rw = _rwkv_post(y_scan, r, k2, v, g, P["rw_lnx_w"], P["rw_lnx_b"], P["rw_r_k"])
    cq, sq, ckt, skt = _rope_tables(T)
    y_swa = _swa_fwd(p_swa, P["b_in_attn"], P["attn_sinks"], cq, sq, ckt, skt)
    ycat = jnp.concatenate([y_rw, y_swa], axis=1)
    x2 = _mm(ycat, W["w_out"], "nn", "out_proj", res=x1, bias=P["b_out"])

    W.update(get_w("xattn", x2))
    hx = _rms_fwd(x2, P["xa_norm"], "xa_norm")
    mn = _rms_fwd(mem, P["mem_norm"], "mem_norm")
    q = _mm(hx, W["w_xq"], "nn", "xq", out_dtype=BF16)
    kv = _mm(mn, W["w_xkv"], "nn", "xkv", out_dtype=BF16)
    o = _xattn_fwd(q, kv)
    x3 = _mm(o, W["w_xo"], "nn", "xo", res=x2)

    W.update(get_w("f2", x3))
    x4, s2 = _ffn_fwd(x3, P["f2_norm"], W["f2_gate"], W["f2_up"], W["f2_down"], "f2")
    dx4, loss_part, d_final = _loss_head(x4, P["final_norm"], tgt)

    gs = {"final_norm": d_final}
    dx3, gs["f2_norm"] = _ffn_bwd(x3, P["f2_norm"], W["f2_gate"], W["f2_up"], W["f2_down"], s2, dx4, "f2", put_g)

    do = _mm(dx3, W["w_xo"], "nt", "xo_do", out_dtype=BF16)
    dw_xo = _mm(o, dx3, "tn", "xo_dw", out_dtype=BF16)
    dq, dkv = _xattn_bwd(q, kv, do)
    dw_xq = _mm(hx, dq, "tn", "xq_dw", out_dtype=BF16)
    dw_xkv = _mm(mn, dkv, "tn", "xkv_dw", out_dtype=BF16)
    sent = put_g("xattn", {"w_xq": dw_xq, "w_xkv": dw_xkv, "w_xo": dw_xo})
    dhx = _mm(dq, W["w_xq"], "nt", "xq_dh", after=sent)
    dmn = _mm(dkv, W["w_xkv"], "nt", "xkv_dmn")
    _, gs["mem_norm"], _ = _rms_bwd(mem, P["mem_norm"], dmn, jnp.zeros_like(mem), "mem_norm_bwd")
    dx2, gs["xa_norm"], gs["b_out"] = _rms_bwd(x2, P["xa_norm"], dhx, dx3, "xa_norm_bwd")

    dycat = _mm(dx2, W["w_out"], "nt", "out_dy")
    dw_out = _mm(ycat, dx2, "tn", "out_dw", out_dtype=BF16)
    dp_swa, gs["b_in_attn"], gs["attn_sinks"] = _swa_bwd(p_swa, P["b_in_attn"], P["attn_sinks"], cq, sq, ckt, skt, dycat[:, RW_W:])
    dy_scan, dr_b, dk2_b, dv_b, dg, gs["rw_lnx_w"], gs["rw_lnx_b"], gs["rw_r_k"] = _rwkv_post_bwd(
        y_scan, r, k2, v, g, P["rw_lnx_w"], P["rw_lnx_b"], P["rw_r_k"], dycat[:, :RW_W])
    dy_p = _to_perm(dy_scan)
    dsai, dvi = _scan_bwd_a(xes, dy_p)
    dj = _scan_bwd_b(xes, [v_p, sai, dy_p, dsai], _ck_a_to_b(ck))
    dan, ddecay, dbn, dk2_s, dr_s = (_from_perm(d) for d in dj)
    cts = (dr_s, ddecay, dk2_s, _from_perm(dvi), dan, dbn, dg, dr_b, dk2_b, dv_b)
    dp_rkv, dp_lora, dmu, dmul, gs["rw_w0"], gs["rw_a0"], gs["rw_k_k"], gs["rw_k_a"], dw_da, gs["rw_gate_up"] = _rwkv_pre_bwd(
        p_rkv, p_lora, pre_params, cts)
    gs["rw_mu"] = jnp.concatenate([dmu, dmul], axis=1)
    gs["rw_decay_up"], gs["rw_aaa_up"] = dw_da[:DECAY_LORA], dw_da[DECAY_LORA:]
    dw_inT = jnp.concatenate([_mm(dp_rkv, h2, "tn", "in_dw_rkv"), _mm(dp_lora, h2, "tn", "in_dw_lora"),
                              _mm(dp_swa, h2, "tn", "in_dw_swa")], axis=0)
    sent = put_g("mix", {"w_in": dw_inT, "w_out": dw_out})
    dh2 = _mm(dp_rkv, w_rkv, "nn", "in_dh_rkv", after=sent)
    dh2 = _mm(dp_lora, w_lora, "nn", "in_dh_lora", res=dh2)
    dh2 = _mm(dp_swa, w_swa, "nn", "in_dh_swa", res=dh2)
    dx1, gs["mix_norm"], _ = _rms_bwd(x1, P["mix_norm"], dh2, dx2, "mix_norm_bwd")

    dx0, gs["f1_norm"] = _ffn_bwd(x, P["f1_norm"], W["f1_gate"], W["f1_up"], W["f1_down"], s1, dx1, "f1", put_g)
    return loss_part, dx0, gs


_ANY = pl.BlockSpec(memory_space=pl.ANY)
_OTHER_CHIPS = ((1, 0), (0, 1), (1, 1))


def _mesh_pos():
    return lax.axis_index("x"), lax.axis_index("y"), lax.axis_index("c")


def _slot(ref, kind, s, rows, cols):
    if kind == "row":
        return ref.at[pl.ds(pl.multiple_of(s * rows, 8), rows), :]
    return ref.at[:, pl.ds(pl.multiple_of(s * cols, 128), cols)]


_HBM = pl.BlockSpec(memory_space=pltpu.HBM)
_SEMS = pl.BlockSpec(memory_space=pltpu.SEMAPHORE)
_SPLIT = dict(compiler_params=pltpu.CompilerParams(has_side_effects=pltpu.SideEffectType.DATAFLOW_SIDE_EFFECTING))


def _in_hbm(a):
    return pltpu.with_memory_space_constraint(a, pltpu.HBM)


def _full_shape(s, kind):
    return (4 * s.shape[0], s.shape[1]) if kind == "row" else (s.shape[0], 4 * s.shape[1])


def _half(ref, shape, h):
    rows, cols = shape
    if rows % 32 == 0:
        return ref.at[pl.ds(pl.multiple_of(h * (rows // 2), 16), rows // 2), :]
    assert cols % 256 == 0, shape
    return ref.at[:, pl.ds(pl.multiple_of(h * (cols // 2), 128), cols // 2)]


def _swap_halves(name, fulls, shard_shapes, kinds):
    n = len(fulls)

    def body(*refs):
        out, send, recv = refs[n:2 * n], refs[2 * n], refs[2 * n + 1]
        x, y, c = _mesh_pos()
        sent = []
        for i in range(n):
            for r, (dx, dy) in enumerate(_OTHER_CHIPS):
                theirs = _slot(out[i], kinds[i], 2 * ((x + dx) % 2) + (y + dy) % 2, *shard_shapes[i])
                have = _half(theirs, shard_shapes[i], c)
                rc = pltpu.make_async_remote_copy(have, have, send.at[3 * i + r], recv.at[3 * i + r], device_id=(x, y, 1 - c),
                                                  device_id_type=MESH)
                rc.start()
                sent.append(rc)
        for i in range(n):
            for r, (dx, dy) in enumerate(_OTHER_CHIPS):
                theirs = _slot(out[i], kinds[i], 2 * ((x + dx) % 2) + (y + dy) % 2, *shard_shapes[i])
                need = _half(theirs, shard_shapes[i], 1 - c)
                pltpu.make_async_remote_copy(need, need, send.at[3 * i + r], recv.at[3 * i + r], device_id=(x, y, c),
                                             device_id_type=MESH).wait_recv()
        for rc in sent:
            rc.wait_send()

    return pl.pallas_call(
        body, name=name, in_specs=[_ANY] * n, out_specs=[_ANY] * n, out_shape=[jax.ShapeDtypeStruct(f.shape, f.dtype) for f in fulls],
        input_output_aliases={i: i for i in range(n)},
        scratch_shapes=[pltpu.SemaphoreType.DMA((3 * n,)), pltpu.SemaphoreType.DMA((3 * n,))],
    )(*fulls)


def _gather_start(name, shards, kinds, groups, after=None):
    n, ng = len(shards), len(groups)
    lands = [_in_hbm(lax.empty(_full_shape(s, k), s.dtype)) for s, k in zip(shards, kinds)]
    n_in = 2 * n + (after is not None)

    def body(*refs):
        src, land, sems, token = refs[:n], refs[n:2 * n], refs[n_in:n_in + 3 * ng], refs[-1]
        x, y, c = _mesh_pos()
        me = 2 * x + y
        for gi, idxs in enumerate(groups):
            send, recv, own = sems[3 * gi:3 * gi + 3]
            for k, i in enumerate(idxs):
                mine = _slot(land[i], kinds[i], me, *src[i].shape)
                for r, (dx, dy) in enumerate(_OTHER_CHIPS):
                    pltpu.make_async_remote_copy(_half(src[i], src[i].shape, c), _half(mine, src[i].shape, c), send.at[3 * k + r],
                                                 recv.at[3 * k + r], device_id=((x + dx) % 2, (y + dy) % 2, c), device_id_type=MESH).start()
                pltpu.make_async_copy(src[i], mine, own.at[k]).start()
        token[...] = jnp.zeros_like(token)

    sem_shapes = [pltpu.SemaphoreType.DMA((w * len(g),)) for g in groups for w in (3, 3, 1)]
    thru = [pltpu.HBM(a.shape, a.dtype) for a in (*shards, *lands)]
    res = pl.pallas_call(
        body, name=name, in_specs=[_HBM] * (2 * n) + [_ANY] * (after is not None),
        out_specs=[_SEMS] * (3 * ng) + [_HBM] * (2 * n) + [pl.BlockSpec(memory_space=pltpu.VMEM)],
        out_shape=sem_shapes + thru + [jax.ShapeDtypeStruct((8, 128), F32)],
        input_output_aliases={i: 3 * ng + i for i in range(2 * n)}, **_SPLIT,
    )(*[_in_hbm(s) for s in shards], *lands, *([] if after is None else [after]))
    return res[:3 * ng], res[3 * ng:3 * ng + n], res[3 * ng + n:3 * ng + 2 * n], res[-1]


def _gather_wait(name, sems, shards, lands, kinds, after):
    m = len(shards)

    def body(*refs):
        src, land, (send, recv, own) = refs[:m], refs[m:2 * m], refs[2 * m:2 * m + 3]
        x, y, c = _mesh_pos()
        me = 2 * x + y
        for k in range(m):
            mine = _slot(land[k], kinds[k], me, *src[k].shape)
            for r in range(3):
                cp = pltpu.make_async_remote_copy(_half(src[k], src[k].shape, c), _half(mine, src[k].shape, c), send.at[3 * k + r],
                                                  recv.at[3 * k + r], device_id=(x, y, c), device_id_type=MESH)
                cp.wait_send()
                cp.wait_recv()
            pltpu.make_async_copy(src[k], mine, own.at[k]).wait()

    thru = [pltpu.HBM(a.shape, a.dtype) for a in (*shards, *lands)]
    res = pl.pallas_call(
        body, name=name, in_specs=[_HBM] * (2 * m) + [_SEMS] * 3 + [pl.BlockSpec(memory_space=pl.ANY)],
        out_specs=[_HBM] * (2 * m), out_shape=thru, input_output_aliases={i: i for i in range(2 * m)}, **_SPLIT,
    )(*shards, *lands, *sems, after)
    return res[m:]


def _scatter_start(name, grads, kinds):
    m = len(grads)
    shard_shape = [(g.shape[0] // 4, g.shape[1]) if k == "row" else (g.shape[0], g.shape[1] // 4) for g, k in zip(grads, kinds)]
    lands = [_in_hbm(lax.empty((4, *s), g.dtype)) for s, g in zip(shard_shape, grads)]

    def body(*refs):
        src, land, (send, recv, own) = refs[:m], refs[m:2 * m], refs[2 * m:2 * m + 3]
        x, y, c = _mesh_pos()
        me = 2 * x + y
        for k in range(m):
            for r, (dx, dy) in enumerate(_OTHER_CHIPS):
                tx, ty = (x + dx) % 2, (y + dy) % 2
                pltpu.make_async_remote_copy(_slot(src[k], kinds[k], 2 * tx + ty, *shard_shape[k]), land[k].at[me],
                                             send.at[3 * k + r], recv.at[3 * k + r], device_id=(tx, ty, c), device_id_type=MESH).start()
            pltpu.make_async_copy(_slot(src[k], kinds[k], me, *shard_shape[k]), land[k].at[me], own.at[k]).start()
        refs[-1][...] = jnp.zeros_like(refs[-1])

    thru = [pltpu.HBM(a.shape, a.dtype) for a in (*grads, *lands)]
    res = pl.pallas_call(
        body, name=name, in_specs=[_HBM] * (2 * m),
        out_specs=[_SEMS] * 3 + [_HBM] * (2 * m) + [pl.BlockSpec(memory_space=pltpu.VMEM)],
        out_shape=[pltpu.SemaphoreType.DMA((3 * m,))] * 2 + [pltpu.SemaphoreType.DMA((m,))] + thru + [jax.ShapeDtypeStruct((8, 128), F32)],
        input_output_aliases={i: 3 + i for i in range(2 * m)}, **_SPLIT,
    )(*[_in_hbm(g) for g in grads], *lands)
    return res[:3], res[3:3 + m], res[3 + m:3 + 2 * m], res[-1]


def _scatter_wait(name, sems, grads, lands, kinds, after):
    m = len(grads)

    def body(*refs):
        src, land, (send, recv, own) = refs[:m], refs[m:2 * m], refs[2 * m:2 * m + 3]
        x, y, c = _mesh_pos()
        me = 2 * x + y
        for k in range(m):
            mine = _slot(src[k], kinds[k], me, *land[k].shape[1:])
            for r in range(3):
                cp = pltpu.make_async_remote_copy(mine, land[k].at[me], send.at[3 * k + r], recv.at[3 * k + r],
                                                  device_id=(x, y, c), device_id_type=MESH)
                cp.wait_send()
                cp.wait_recv()
            pltpu.make_async_copy(mine, land[k].at[me], own.at[k]).wait()

    thru = [pltpu.HBM(a.shape, a.dtype) for a in (*grads, *lands)]
    res = pl.pallas_call(
        body, name=name, in_specs=[_HBM] * (2 * m) + [_SEMS] * 3 + [pl.BlockSpec(memory_space=pl.ANY)],
        out_specs=[_HBM] * (2 * m), out_shape=thru, input_output_aliases={i: i for i in range(2 * m)}, **_SPLIT,
    )(*grads, *lands, *sems, after)
    return res[m:]


def _swap_with_sibling(arrs, name):
    n = len(arrs)

    def body(*refs):
        ins, outs = refs[:n], refs[n:2 * n]
        send, recv = refs[2 * n:]
        x, y, c = _mesh_pos()
        copies = []
        for i in range(n):
            rc = pltpu.make_async_remote_copy(ins[i], outs[i], send.at[i], recv.at[i], device_id=(x, y, 1 - c), device_id_type=MESH)
            rc.start()
            copies.append(rc)
        for rc in copies:
            rc.wait()

    return pl.pallas_call(
        body, name=name, in_specs=[_ANY] * n, out_specs=[_ANY] * n,
        out_shape=[jax.ShapeDtypeStruct(a.shape, a.dtype) for a in arrs],
        scratch_shapes=[pltpu.SemaphoreType.DMA((n,)), pltpu.SemaphoreType.DMA((n,))],
    )(*arrs)


def _small_start(pack, after):
    land = _in_hbm(lax.empty((8, *pack.shape), pack.dtype))

    def body(in_ref, land_ref, after_ref, send, recv, own, in_thru, land_thru, token):
        x, y, c = _mesh_pos()
        me = 4 * x + 2 * y + c
        for r in range(1, 8):
            dx, dy, dc = r // 4, (r // 2) % 2, r % 2
            pltpu.make_async_remote_copy(in_ref, land_ref.at[me], send.at[r - 1], recv.at[r - 1],
                                         device_id=((x + dx) % 2, (y + dy) % 2, (c + dc) % 2), device_id_type=MESH).start()
        pltpu.make_async_copy(in_ref, land_ref.at[me], own.at[0]).start()
        token[...] = jnp.zeros_like(token)

    res = pl.pallas_call(
        body, name="small_start", in_specs=[_HBM, _HBM, _ANY],
        out_specs=[_SEMS] * 3 + [_HBM, _HBM, pl.BlockSpec(memory_space=pltpu.VMEM)],
        out_shape=[pltpu.SemaphoreType.DMA((7,)), pltpu.SemaphoreType.DMA((7,)), pltpu.SemaphoreType.DMA((1,)),
                   pltpu.HBM(pack.shape, pack.dtype), pltpu.HBM(land.shape, land.dtype), jax.ShapeDtypeStruct((8, 128), F32)],
        input_output_aliases={0: 3, 1: 4}, **_SPLIT,
    )(_in_hbm(pack), land, after)
    return res[:3], res[3], res[4], res[5]


def _small_wait(sems, pack, land, after):
    def body(in_ref, land_ref, send, recv, own, after_ref, in_dead, got):
        x, y, c = _mesh_pos()
        me = 4 * x + 2 * y + c
        for r in range(1, 8):
            cp = pltpu.make_async_remote_copy(in_ref, land_ref.at[me], send.at[r - 1], recv.at[r - 1], device_id=(x, y, c),
                                              device_id_type=MESH)
            cp.wait_send()
            cp.wait_recv()
        pltpu.make_async_copy(in_ref, land_ref.at[me], own.at[0]).wait()

    res = pl.pallas_call(
        body, name="small_wait", in_specs=[_HBM, _HBM] + [_SEMS] * 3 + [_ANY], out_specs=[_HBM, _HBM],
        out_shape=[pltpu.HBM(pack.shape, pack.dtype), pltpu.HBM(land.shape, land.dtype)], input_output_aliases={0: 0, 1: 1}, **_SPLIT,
    )(pack, land, *sems, after)
    return res[1]


def _row_tile(R, dtype, target=256):
    mult = 8 * 4 // jnp.dtype(dtype).itemsize
    best = R
    for t in range(mult, min(R, target) + 1, mult):
        if R % t == 0:
            best = t
    return best


def _sum_slots(stack, name, out_dtype=F32):
    k, R, C = stack.shape
    tr = _row_tile(R, stack.dtype)

    def body(s_ref, o_ref):
        acc = s_ref[0].astype(F32)
        for j in range(1, k):
            acc = acc + s_ref[j].astype(F32)
        o_ref[...] = acc.astype(out_dtype)

    return pl.pallas_call(
        body, name=name, grid=(R // tr,), in_specs=[pl.BlockSpec((k, tr, C), lambda i: (0, i, 0))],
        out_specs=pl.BlockSpec((tr, C), lambda i: (i, 0)), out_shape=jax.ShapeDtypeStruct((R, C), out_dtype),
        compiler_params=_cp(("parallel",)),
    )(stack)


def _adamw(w, m, v, ga, gb, name, after=None):
    R, C = w.shape
    tr = _row_tile(R, F32, 128)
    gs = [ga] if gb is None else [ga, gb]
    extra = [] if after is None else [after]

    def body(*refs):
        w_ref, m_ref, v_ref = refs[:3]
        g = refs[3][...].astype(F32)
        if gb is not None:
            g = g + refs[4][...].astype(F32)
        g_ref, d_ref, nm_ref, nv_ref = refs[-4:]
        nm = ADAM_B1 * m_ref[...] + (1.0 - ADAM_B1) * g
        nv = ADAM_B2 * v_ref[...] + (1.0 - ADAM_B2) * (g * g)
        m_hat = nm / (1.0 - ADAM_B1 ** ADAM_STEP)
        v_hat = nv / (1.0 - ADAM_B2 ** ADAM_STEP)
        g_ref[...] = g
        d_ref[...] = -ADAM_LR * (m_hat / (jnp.sqrt(v_hat) + ADAM_EPS) + ADAM_WD * w_ref[...])
        nm_ref[...] = nm
        nv_ref[...] = nv

    spec = pl.BlockSpec((tr, C), lambda i: (i, 0))
    return pl.pallas_call(
        body, name=name, grid=(R // tr,), in_specs=[spec] * (3 + len(gs)) + [_ANY] * len(extra), out_specs=[spec] * 4,
        out_shape=[jax.ShapeDtypeStruct((R, C), F32)] * 4, compiler_params=_cp(("parallel",)),
    )(w, m, v, *gs, *extra)


def _pack(arrs):
    rows = []
    for a in arrs:
        flat = a.reshape(-1)
        rows.append(jnp.pad(flat, (0, -flat.shape[0] % 1024)).reshape(-1, 1024))
    p = jnp.concatenate(rows, axis=0)
    return jnp.pad(p, ((0, -p.shape[0] % 8), (0, 0)))


def _unpack(p, shapes):
    out, r = [], 0
    for s in shapes:
        n = 1
        for d in s:
            n *= d
        nr = -(-n // 1024)
        out.append(p[r:r + nr].reshape(-1)[:n].reshape(s))
        r += nr
    return out


BIG = ("f1_gate", "f1_up", "f1_down", "w_in", "w_out", "w_xq", "w_xkv", "w_xo", "f2_gate", "f2_up", "f2_down")
BIG_KIND = {"f1_gate": "col", "f1_up": "col", "f1_down": "row", "w_in": "row", "w_out": "row", "w_xq": "row", "w_xkv": "col",
            "w_xo": "row", "f2_gate": "col", "f2_up": "col", "f2_down": "row"}
LORA = ("rw_decay_up", "rw_aaa_up", "rw_gate_up")
WEIGHTS = ("f1_norm", "f1_gate", "f1_up", "f1_down", "mix_norm", "w_in", "b_in_attn", "rw_mu", "rw_w0", "rw_decay_up", "rw_a0",
           "rw_aaa_up", "rw_gate_up", "rw_k_k", "rw_k_a", "rw_r_k", "rw_lnx_w", "rw_lnx_b", "attn_sinks", "w_out", "b_out", "xa_norm",
           "mem_norm", "w_xq", "w_xkv", "w_xo", "f2_norm", "f2_gate", "f2_up", "f2_down", "final_norm")
SMALL = tuple(n for n in WEIGHTS if n not in BIG)
GROUP_ORDER = ("f1", "f1d", "mix", "xattn", "f2")
GROUPS = {"f1": ("f1_gate", "f1_up"), "f1d": ("f1_down",), "mix": ("w_in", "w_out") + LORA, "xattn": ("w_xq", "w_xkv", "w_xo"),
          "f2": ("f2_gate", "f2_up", "f2_down")}


def kernel(x, mem, f1_norm, f1_gate, f1_up, f1_down, mix_norm, w_in, b_in_attn, rw_mu, rw_w0, rw_decay_up, rw_a0, rw_aaa_up, rw_gate_up, rw_k_k, rw_k_a, rw_r_k, rw_lnx_w, rw_lnx_b, attn_sinks, w_out, b_out, xa_norm, mem_norm, w_xq, w_xkv, w_xo, f2_norm, f2_gate, f2_up, f2_down, final_norm, loss_target, m_f1_norm, m_f1_gate, m_f1_up, m_f1_down, m_mix_norm, m_w_in, m_b_in_attn, m_rw_mu, m_rw_w0, m_rw_decay_up, m_rw_a0, m_rw_aaa_up, m_rw_gate_up, m_rw_k_k, m_rw_k_a, m_rw_r_k, m_rw_lnx_w, m_rw_lnx_b, m_attn_sinks, m_w_out, m_b_out, m_xa_norm, m_mem_norm, m_w_xq, m_w_xkv, m_w_xo, m_f2_norm, m_f2_gate, m_f2_up, m_f2_down, m_final_norm, v_f1_norm, v_f1_gate, v_f1_up, v_f1_down, v_mix_norm, v_w_in, v_b_in_attn, v_rw_mu, v_rw_w0, v_rw_decay_up, v_rw_a0, v_rw_aaa_up, v_rw_gate_up, v_rw_k_k, v_rw_k_a, v_rw_r_k, v_rw_lnx_w, v_rw_lnx_b, v_attn_sinks, v_w_out, v_b_out, v_xa_norm, v_mem_norm, v_w_xq, v_w_xkv, v_w_xo, v_f2_norm, v_f2_gate, v_f2_up, v_f2_down, v_final_norm):
    a = dict(locals())
    w = {n: a[n] for n in WEIGHTS}
    m = {n: a["m_" + n] for n in WEIGHTS}
    v = {n: a["v_" + n] for n in WEIGHTS}
    sq = lambda t: t.reshape(t.shape[-2:]) if t.ndim == 3 else t.reshape(1, -1)

    local_name = lambda n: "w_inT" if n == "w_in" else n
    kind_of = lambda n: BIG_KIND.get(n, "col")
    payload = lambda n: sq(w[n]).T if n == "w_in" else sq(w[n]) if n in LORA else sq(w[n]).astype(BF16)
    gathers = {}

    def start_gather(name, grps, after):
        shards = [payload(n) for g in grps for n in GROUPS[g]]
        kinds = [kind_of(n) for g in grps for n in GROUPS[g]]
        groups, at = [], 0
        for g in grps:
            groups.append(list(range(at, at + len(GROUPS[g]))))
            at += len(GROUPS[g])
        sems, src_thru, land_thru, token = _gather_start(name, shards, kinds, groups, after)
        for gi, g in enumerate(grps):
            gathers[g] = (sems[3 * gi:3 * gi + 3], [src_thru[i] for i in groups[gi]], [land_thru[i] for i in groups[gi]],
                          [kinds[i] for i in groups[gi]], token)

    start_gather("gather_start", GROUP_ORDER, None)

    def get_w(grp, after):
        g_sems, g_src, g_land, g_kinds, token = gathers[grp]
        got = _gather_wait("gather_wait_" + grp, g_sems, g_src, g_land, g_kinds, token if after is None else after)
        got = _swap_halves("gather_swap_" + grp, got, [s.shape for s in g_src], g_kinds)
        return {local_name(n): f for n, f in zip(GROUPS[grp], got)}

    in_flight = []

    def put_g(label, gw):
        names = list(gw)
        *flight, sent = _scatter_start("scatter_start_" + label, [gw[n] for n in names], [kind_of(n) for n in names])
        in_flight.append((label, names, flight))
        return sent

    P = {n: sq(w[n]) for n in SMALL if n not in LORA}
    P["attn_sinks"] = jnp.pad(P["attn_sinks"], ((0, 0), (0, 128 - P["attn_sinks"].shape[1])))
    P["rw_r_k"] = w["rw_r_k"].reshape(1, RW_W)
    loss_part, grad_x, gs = _local_step(x[0], mem[0], loss_target[0], get_w, P, put_g)
    loss = lax.psum(loss_part[0, 0], ("x", "y", "c"))

    gs["attn_sinks"] = gs["attn_sinks"][:, :16]
    small_flight = _small_start(_pack([gs[n] for n in SMALL]), grad_x)

    out, after = {}, small_flight[-1]
    for label, names, (g_sems, g_thru, l_thru) in in_flight:
        stacks = _scatter_wait("scatter_wait_" + label, g_sems, g_thru, l_thru, [kind_of(n) for n in names], after)
        partial = [_sum_slots(s, "sum_chips_" + n, F32 if n == "w_in" else BF16) for s, n in zip(stacks, names)]
        sibling = _swap_with_sibling(partial, "swap_" + label)
        chain = None
        for n, pa, sb in zip(names, partial, sibling):
            if n == "w_in":
                pa, sb = pa.T, sb.T
            out[n] = _adamw(sq(w[n]), sq(m[n]), sq(v[n]), pa, sb, "adamw_" + n, after=chain)
            chain = out[n][1]
        after = chain

    gsum = _sum_slots(_small_wait(*small_flight[:-1], after), "sum_small")
    g_small = dict(zip(SMALL, _unpack(gsum, [gs[n].shape for n in SMALL])))
    shard = 2 * lax.axis_index("x") + lax.axis_index("y")
    for n in LORA:
        cols = w[n].shape[-1]
        g_small[n] = lax.dynamic_slice_in_dim(g_small[n], shard * cols, cols, axis=1)

    flat = lambda d: _pack([d[n] for n in SMALL])
    res = _adamw(flat(w), flat(m), flat(v), _pack([g_small[n] for n in SMALL]), None, "adamw_small")
    shapes = [w[n].shape for n in SMALL]
    for k, p in enumerate(res):
        for n, t in zip(SMALL, _unpack(p, shapes)):
            out.setdefault(n, [None] * 4)[k] = t
    outs = [loss, grad_x.reshape(x.shape)]
    for k in range(4):
        outs += [out[n][k].reshape(w[n].shape) for n in WEIGHTS]
    return tuple(outs)
```

```python
import functools

import jax
import jax.numpy as jnp
from jax import lax
from jax.experimental import pallas as pl
from jax.experimental.pallas import tpu as pltpu

F32, BF16 = jnp.float32, jnp.bfloat16
MESH = pl.DeviceIdType.MESH

HEAD = 64
RW_HEADS = 16
RW_W = 1024
SWA_W = 1024
KV_W = 128
DECAY_LORA, AAA_LORA, GATE_LORA = 64, 64, 160
LORA_W = DECAY_LORA + AAA_LORA + GATE_LORA
SHIFT_COLS = 3 * RW_W + LORA_W
XH = 4
XHD = 512
MEM_LEN = 256
WINDOW = 128
GN_EPS = 64e-5
RMS_EPS = 1e-6
NEG_INF = -1e30
ADAM_LR, ADAM_B1, ADAM_B2, ADAM_EPS, ADAM_WD, ADAM_STEP = 0.001, 0.9, 0.999, 1e-08, 0.01, 10

VMEM_LIMIT = 56 * 1024 * 1024


def _cp(sem=None, **kw):
    return pltpu.CompilerParams(dimension_semantics=sem, vmem_limit_bytes=VMEM_LIMIT, **kw)


def _pick(dim, target):
    if dim <= target:
        return dim
    best = None
    for t in range(128, target + 1, 128):
        if dim % t == 0:
            best = t
    assert best is not None, (dim, target)
    return best


_DIMS = {"nn": (((1,), (0,)), ((), ())), "nt": (((1,), (1,)), ((), ())), "tn": (((0,), (0,)), ((), ()))}


def _mm(a, b, mode, name, out_dtype=F32, alpha=1.0, res=None, bias=None, tm=1024, tn=1024, tk=2048, after=None):
    if mode == "nn":
        (M, K), (K2, N) = a.shape, b.shape
    elif mode == "nt":
        (M, K), (N, K2) = a.shape, b.shape
    else:
        (K, M), (K2, N) = a.shape, b.shape
    assert K == K2, (name, a.shape, b.shape)
    tm, tn, tk = _pick(M, tm), _pick(N, tn), _pick(K, tk)
    nk = K // tk
    a_spec = pl.BlockSpec((tk, tm), lambda i, j, k: (k, i)) if mode == "tn" else pl.BlockSpec((tm, tk), lambda i, j, k: (i, k))
    b_spec = pl.BlockSpec((tn, tk), lambda i, j, k: (j, k)) if mode == "nt" else pl.BlockSpec((tk, tn), lambda i, j, k: (k, j))
    o_spec = pl.BlockSpec((tm, tn), lambda i, j, k: (i, j))
    ins, specs = [a, b], [a_spec, b_spec]
    if res is not None:
        ins.append(res)
        specs.append(o_spec)
    if bias is not None:
        ins.append(bias)
        specs.append(pl.BlockSpec((1, tn), lambda i, j, k: (0, j)))
    if after is not None:
        ins.append(after)
        specs.append(pl.BlockSpec(memory_space=pl.ANY))
    dims = _DIMS[mode]

    def body(*refs):
        a_ref, b_ref = refs[0], refs[1]
        part = lax.dot_general(a_ref[...].astype(BF16), b_ref[...].astype(BF16), dims, preferred_element_type=F32)

        def finish(o, o_ref):
            if alpha != 1.0:
                o = o * alpha
            p = 2
            if res is not None:
                o = o + refs[p][...].astype(F32)
                p += 1
            if bias is not None:
                o = o + refs[p][...]
            o_ref[...] = o.astype(out_dtype)

        if nk == 1:
            finish(part, refs[-1])
            return
        o_ref, acc_ref = refs[-2], refs[-1]
        k = pl.program_id(2)

        @pl.when(k == 0)
        def _():
            acc_ref[...] = part

        @pl.when(k > 0)
        def _():
            acc_ref[...] += part

        @pl.when(k == nk - 1)
        def _():
            finish(acc_ref[...], o_ref)

    return pl.pallas_call(
        body, name=name, grid=(M // tm, N // tn, nk), in_specs=specs, out_specs=o_spec,
        out_shape=jax.ShapeDtypeStruct((M, N), out_dtype), scratch_shapes=[pltpu.VMEM((tm, tn), F32)] * (nk > 1),
        compiler_params=_cp(("parallel", "parallel", "arbitrary")),
    )(*ins)


def _rows(fn, name, T, tm, tiled, full, out_tiled, out_acc, extra=(), reverse=False, scratch=()):
    n = T // tm
    idx = (lambda i: n - 1 - i) if reverse else (lambda i: i)
    in_specs = [pl.BlockSpec((tm, a.shape[1]), lambda i: (idx(i), 0)) for a in tiled]
    in_specs += [mk(idx) for _, mk in extra]
    in_specs += [pl.BlockSpec(a.shape, lambda i, nd=a.ndim: (0,) * nd) for a in full]
    out_specs = [pl.BlockSpec((tm, c), lambda i: (idx(i), 0)) for c, _ in out_tiled]
    out_specs += [pl.BlockSpec(s, lambda i, nd=len(s): (0,) * nd) for s, _ in out_acc]
    out_shape = [jax.ShapeDtypeStruct((T, c), d) for c, d in out_tiled] + [jax.ShapeDtypeStruct(s, d) for s, d in out_acc]
    n_in = len(tiled) + len(extra) + len(full)
    n_t, n_a = len(out_tiled), len(out_acc)

    def body(*refs):
        step = pl.program_id(0)
        vals = [r[...] for r in refs[:n_in]]
        outs = fn(idx(step), *vals, *refs[n_in + n_t + n_a:])
        for r, v in zip(refs[n_in:n_in + n_t], outs[:n_t]):
            r[...] = v.astype(r.dtype)
        for r, v in zip(refs[n_in + n_t:n_in + n_t + n_a], outs[n_t:]):
            @pl.when(step == 0)
            def _(r=r):
                r[...] = jnp.zeros_like(r)

            r[...] += v

    return pl.pallas_call(
        body, name=name, grid=(n,), in_specs=in_specs, out_specs=out_specs, out_shape=out_shape,
        scratch_shapes=list(scratch), compiler_params=_cp(("arbitrary",)),
    )(*tiled, *[a for a, _ in extra], *full)


def _rms(x, g):
    return x * lax.rsqrt(jnp.mean(x * x, axis=-1, keepdims=True) + RMS_EPS) * g


def _rms_fwd(x, g, name, tm=256):
    (h,) = _rows(lambda i, x, g: (_rms(x, g),), name, x.shape[0], min(tm, x.shape[0]), [x], [g], [(x.shape[1], BF16)], [])
    return h


def _rms_bwd(x, g, dh, dres, name, tm=256):
    D = x.shape[1]

    def fn(i, x, dh, dres, g):
        _, vjp = jax.vjp(_rms, x, g)
        dx, dg = vjp(dh.astype(F32))
        dx = dx + dres
        return dx, dg, jnp.sum(dx, axis=0, keepdims=True)

    return _rows(fn, name, x.shape[0], tm, [x, dh, dres], [g], [(D, F32)], [((1, D), F32), ((1, D), F32)])


def _ffn_up(h, wg, wu, name, tm=1024, tn=512, after=None):
    (M, K), N = h.shape, wg.shape[1]
    tm, tn = _pick(M, tm), _pick(N, tn)

    def body(*refs):
        h_ref, wg_ref, wu_ref = refs[:3]
        g_ref, u_ref, a_ref = refs[-3:]
        hb = h_ref[...].astype(BF16)
        g = jnp.dot(hb, wg_ref[...].astype(BF16), preferred_element_type=F32)
        u = jnp.dot(hb, wu_ref[...].astype(BF16), preferred_element_type=F32)
        g_ref[...] = g
        u_ref[...] = u
        a_ref[...] = (g * jax.nn.sigmoid(g) * u).astype(BF16)

    o_spec = pl.BlockSpec((tm, tn), lambda i, j: (i, j))
    w_spec = pl.BlockSpec((K, tn), lambda i, j: (0, j))
    extra = [] if after is None else [after]
    return pl.pallas_call(
        body, name=name, grid=(M // tm, N // tn),
        in_specs=[pl.BlockSpec((tm, K), lambda i, j: (i, 0)), w_spec, w_spec] + [pl.BlockSpec(memory_space=pl.ANY)] * len(extra),
        out_specs=[o_spec] * 3, out_shape=[jax.ShapeDtypeStruct((M, N), F32)] * 2 + [jax.ShapeDtypeStruct((M, N), BF16)],
        compiler_params=_cp(("parallel", "parallel")),
    )(h, wg, wu, *extra)


def _ffn_dact(dxo, wd, g, u, name, tm=1024, tn=512):
    (M, K), N = dxo.shape, wd.shape[0]
    tm, tn = _pick(M, tm), _pick(N, tn)

    def body(dx_ref, wd_ref, g_ref, u_ref, dg_ref, du_ref):
        da = 0.5 * lax.dot_general(dx_ref[...].astype(BF16), wd_ref[...].astype(BF16), _DIMS["nt"], preferred_element_type=F32)
        g = g_ref[...]
        s = jax.nn.sigmoid(g)
        dg_ref[...] = (da * u_ref[...] * (s * (1.0 + g * (1.0 - s)))).astype(BF16)
        du_ref[...] = (da * (g * s)).astype(BF16)

    t_spec = pl.BlockSpec((tm, tn), lambda i, j: (i, j))
    return pl.pallas_call(
        body, name=name, grid=(M // tm, N // tn),
        in_specs=[pl.BlockSpec((tm, K), lambda i, j: (i, 0)), pl.BlockSpec((tn, K), lambda i, j: (j, 0)), t_spec, t_spec],
        out_specs=[t_spec, t_spec], out_shape=[jax.ShapeDtypeStruct((M, N), BF16)] * 2, compiler_params=_cp(("parallel", "parallel")),
    )(dxo, wd, g, u)


def _ffn_fwd(x, gain, wg, wu, wd, tag, after=None):
    h = _rms_fwd(x, gain, tag + "_norm")
    G, U, A = _ffn_up(h, wg, wu, tag + "_up", after=after)
    xo = _mm(A, wd(A) if callable(wd) else wd, "nn", tag + "_down", alpha=0.5, res=x)
    return xo, (h, G, U, A)


def _ffn_bwd(x, gain, wg, wu, wd, saved, dxo, tag, send):
    h, G, U, A = saved
    dwd = _mm(A, dxo, "tn", tag + "_dwd", out_dtype=BF16, alpha=0.5)
    sent = send(tag + "_down", {tag + "_down": dwd})
    dG, dU = _ffn_dact(dxo, wd, G, U, tag + "_dact")
    dwu = _mm(h, dU, "tn", tag + "_dwu", out_dtype=BF16, after=sent)
    sent = send(tag + "_up", {tag + "_up": dwu})
    dwg = _mm(h, dG, "tn", tag + "_dwg", out_dtype=BF16, after=sent)
    sent = send(tag + "_gate", {tag + "_gate": dwg})
    dh = _mm(dG, wg, "nt", tag + "_dh_g", after=sent)
    dh = _mm(dU, wu, "nt", tag + "_dh_u", res=dh)
    dx, dgain, _ = _rms_bwd(x, gain, dh, dxo, tag + "_norm_bwd")
    return dx, dgain


def _segsum64_impl(x):
    r = lax.broadcasted_iota(jnp.int32, (128, 128), 0) // HEAD
    c = lax.broadcasted_iota(jnp.int32, (128, 128), 1) // HEAD
    ones = (r == c).astype(BF16)
    hi = x.astype(BF16)
    lo = (x - hi.astype(F32)).astype(BF16)
    outs = []
    for q in range(x.shape[1] // 128):
        sl = slice(q * 128, (q + 1) * 128)
        outs.append(jnp.dot(hi[:, sl], ones, preferred_element_type=F32) + jnp.dot(lo[:, sl], ones, preferred_element_type=F32))
    return outs[0] if len(outs) == 1 else jnp.concatenate(outs, axis=1)


@jax.custom_vjp
def _segsum64(x):
    return _segsum64_impl(x)


_segsum64.defvjp(lambda x: (_segsum64_impl(x), None), lambda _, ct: (_segsum64_impl(ct),))


def _swap32(x):
    lane = lax.broadcasted_iota(jnp.int32, (x.shape[0], 128), 1)
    outs = [jnp.take_along_axis(x[:, q * 128:(q + 1) * 128], lane ^ 32, axis=1) for q in range(x.shape[1] // 128)]
    return outs[0] if len(outs) == 1 else jnp.concatenate(outs, axis=1)


def _tree_sum(xs):
    xs = list(xs)
    while len(xs) > 1:
        nxt = [xs[i] + xs[i + 1] for i in range(0, len(xs) - 1, 2)]
        if len(xs) % 2:
            nxt.append(xs[-1])
        xs = nxt
    return xs[0]


class _Acc:
    def __init__(self, ways=4):
        self.parts = [None] * ways

    def add(self, i, term):
        k = i % len(self.parts)
        self.parts[k] = term if self.parts[k] is None else self.parts[k] + term

    def total(self):
        return _tree_sum([p for p in self.parts if p is not None])


def _softplus(x):
    return jnp.maximum(x, 0.0) + jnp.log(1.0 + jnp.exp(-jnp.abs(x)))


def _pre_core(k, da, gd, w0, a0, k_k, k_a, w_da, gate_up):
    lane = lax.broadcasted_iota(jnp.int32, da.shape, 1)
    w_da = w_da.astype(BF16)
    l1 = jnp.dot(jnp.where(lane < DECAY_LORA, jnp.tanh(da), 0.0).astype(BF16), w_da, preferred_element_type=F32)
    l2 = jnp.dot(jnp.where(lane >= DECAY_LORA, da, 0.0).astype(BF16), w_da, preferred_element_type=F32)
    wlog = -_softplus(-(w0 + l1)) - 0.5
    decay = jnp.exp(-jnp.exp(wlog))
    a = jax.nn.sigmoid(a0 + l2)
    g = jnp.dot(jax.nn.sigmoid(gd).astype(BF16), gate_up.astype(BF16), preferred_element_type=F32)
    kk = k * k_k
    kkn = kk / jnp.maximum(jnp.sqrt(_segsum64(kk * kk)), 1e-12)
    k2 = k * (1.0 + (a - 1.0) * k_a)
    return decay, k2, -kkn, kkn * a, g


def _pre_shift(i, zr, zl, zr8, zl8, mu, mul):
    live = (i > 0).astype(F32)
    dz = _shift_down(zr, zr8[7:8, :] * live) - zr
    dzl = _shift_down(zl, zl8[7:8, :] * live) - zl
    return zr + dz * mu, zl + dzl * mul, dz, dzl


def _shift_down(x, first_row):
    rolled = pltpu.roll(x, 1, 0)
    row = lax.broadcasted_iota(jnp.int32, x.shape, 0)
    return jnp.where(row == 0, first_row, rolled)


def _shift_up(x, last_row):
    rolled = pltpu.roll(x, x.shape[0] - 1, 0)
    row = lax.broadcasted_iota(jnp.int32, x.shape, 0)
    return jnp.where(row == x.shape[0] - 1, last_row, rolled)


def _prev_rows_spec(tm, cols):
    return lambda idx: pl.BlockSpec((8, cols), lambda i: (jnp.maximum(idx(i) * (tm // 8) - 1, 0), 0))


def _rwkv_pre(p_rkv, p_lora, params, tm=256):
    T = p_rkv.shape[0]

    def fn(i, zr, zl, zr8, zl8, mu, mul, *ps):
        z, z2, _, _ = _pre_shift(i, zr, zl, zr8, zl8, mu, mul)
        decay, k2, an, bn, g = _pre_core(z[:, RW_W:2 * RW_W], z2[:, :128], z2[:, 128:], *ps)
        return z[:, :RW_W], decay, k2, z[:, 2 * RW_W:], an, bn, g

    extra = [(p_rkv, _prev_rows_spec(tm, 3 * RW_W)), (p_lora, _prev_rows_spec(tm, LORA_W))]
    return _rows(fn, "rwkv_pre", T, tm, [p_rkv, p_lora], list(params), [(RW_W, F32)] * 7, [], extra=extra)


def _rwkv_pre_bwd(p_rkv, p_lora, params, cts, tm=256):
    T = p_rkv.shape[0]
    n = T // tm

    def fn(i, zr, zl, cr, cdec, ck2, cv, can, cbn, cg, cr_b, ck2_b, cv_b, zr8, zl8, mu, mul, *rest):
        ps, (car, carl) = rest[:-2], rest[-2:]
        cr, ck2, cv = cr + cr_b, ck2 + ck2_b, cv + cv_b
        z, z2, dif, difl = _pre_shift(i, zr, zl, zr8, zl8, mu, mul)
        _, vjp = jax.vjp(_pre_core, z[:, RW_W:2 * RW_W], z2[:, :128], z2[:, 128:], *ps)
        dk, dda, dgd, *dps = vjp((cdec, ck2, can, cbn, cg))
        dz = jnp.concatenate([cr, dk, cv], axis=1)
        dz2 = jnp.concatenate([dda, dgd], axis=1)
        dzp, dzlp = dz * mu, dz2 * mul

        @pl.when(i == n - 1)
        def _():
            car[...] = jnp.zeros_like(car)
            carl[...] = jnp.zeros_like(carl)

        d_rkv = dz - dzp + _shift_up(dzp, car[0:1, :])
        d_lora = dz2 - dzlp + _shift_up(dzlp, carl[0:1, :])
        car[0:1, :] = dzp[0:1, :]
        carl[0:1, :] = dzlp[0:1, :]
        return (d_rkv, d_lora, jnp.sum(dz * dif, axis=0, keepdims=True), jnp.sum(dz2 * difl, axis=0, keepdims=True), *dps)

    extra = [(p_rkv, _prev_rows_spec(tm, 3 * RW_W)), (p_lora, _prev_rows_spec(tm, LORA_W))]
    acc = [(p.shape, F32) for p in params]
    return _rows(fn, "rwkv_pre_bwd", T, tm, [p_rkv, p_lora, *cts], list(params), [(3 * RW_W, BF16), (LORA_W, BF16)], acc,
                 extra=extra, reverse=True, scratch=[pltpu.VMEM((8, 3 * RW_W), F32), pltpu.VMEM((8, LORA_W), F32)])


def _post_core(y, r, k2, v, g, lw, lb, rk):
    mu = _segsum64(y) * (1.0 / HEAD)
    yc = y - mu
    var = _segsum64(yc * yc) * (1.0 / HEAD)
    yn = yc * lax.rsqrt(var + GN_EPS) * lw + lb
    return (yn + _segsum64(r * k2 * rk) * v) * g


def _rwkv_post(y, r, k2, v, g, lw, lb, rk, tm=256):
    (o,) = _rows(lambda i, *a: (_post_core(*a),), "rwkv_post", y.shape[0], tm, [y, r, k2, v, g], [lw, lb, rk], [(RW_W, BF16)], [])
    return o


def _rwkv_post_bwd(y, r, k2, v, g, lw, lb, rk, do, tm=256):
    def fn(i, y, r, k2, v, g, do, lw, lb, rk):
        _, vjp = jax.vjp(_post_core, y, r, k2, v, g, lw, lb, rk)
        return vjp(do.astype(F32))

    return _rows(fn, "rwkv_post_bwd", y.shape[0], tm, [y, r, k2, v, g, do], [lw, lb, rk], [(RW_W, F32)] * 5, [((1, RW_W), F32)] * 3)


SCAN_L = 32


def _to_perm(x):
    T = x.shape[0]
    return x.reshape(T, RW_HEADS, HEAD).transpose(0, 2, 1).reshape(T, 8, 128)


def _from_perm(x):
    T = x.shape[0]
    return x.reshape(T, HEAD, RW_HEADS).transpose(0, 2, 1).reshape(T, RW_W)


def _as_tile(p):
    lane = lax.broadcasted_iota(jnp.int32, (8, 128), 1)
    return jnp.take_along_axis(p, (lane % 8) * 16 + lane // 8, axis=1)


def _as_perm(t):
    lane = lax.broadcasted_iota(jnp.int32, (8, 128), 1)
    return jnp.take_along_axis(t, (lane % 16) * 8 + lane // 16, axis=1)


def _tiles_to_perm(refs, L):
    for r in refs:
        for t in range(L):
            r[t] = _as_perm(r[t])


def _expander(srcs, tiles=()):
    s = lax.broadcasted_iota(jnp.int32, (8, 128), 0)
    lane = lax.broadcasted_iota(jnp.int32, (8, 128), 1)
    idx = 16 * s + lane // 8

    def expand(t, e_ref):
        for m, r in enumerate(srcs):
            for g in range(8):
                row = jnp.broadcast_to(r[t, pl.ds(g, 1), :], (8, 128))
                e_ref[m, g * 8:(g + 1) * 8, :] = jnp.take_along_axis(row, idx, axis=1)
        for k, r in enumerate(tiles):
            e_ref[len(srcs) + k, 0:8, :] = _as_tile(r[t])

    return expand


def _ck_a_to_b(ck):
    n = ck.shape[0]
    return ck.reshape(n, 8, 8, 8, RW_HEADS, 8).transpose(0, 3, 5, 1, 4, 2).reshape(n, HEAD, 8, 128)


def _scan_fwd(xes, vi):
    T, L = vi.shape[0], SCAN_L
    nch = T // L

    def body(*refs):
        xr, (vi_ref, yi_ref, sa_ref, ck_ref, st_ref, e0, e1) = refs[:5], refs[5:]

        @pl.when(pl.program_id(0) == 0)
        def _():
            st_ref[...] = jnp.zeros_like(st_ref)

        ck_ref[0] = st_ref[...]
        expand = _expander(xr, [vi_ref])
        expand(0, e0)

        def step(t, e_ref):
            v = e_ref[5, 0:8, :]
            row = lambda m, j: jnp.broadcast_to(e_ref[m, pl.ds(j, 1), :], (8, 128))
            sa = _Acc()
            for j in range(HEAD):
                sa.add(j, st_ref[j] * row(0, j))
            sa = sa.total()
            sa_ref[t] = sa
            y = _Acc()
            for j in range(HEAD):
                s = st_ref[j] * row(1, j) + row(2, j) * sa + row(3, j) * v
                st_ref[j] = s
                y.add(j, s * row(4, j))
            yi_ref[t] = y.total()

        def pair(p, carry):
            t = 2 * p
            expand(t + 1, e1)
            step(t, e0)
            expand(jnp.minimum(t + 2, L - 1), e0)
            step(t + 1, e1)
            return carry

        lax.fori_loop(0, L // 2, pair, 0)
        _tiles_to_perm([yi_ref, sa_ref], L)

    tile = pl.BlockSpec((L, 8, 128), lambda c: (c, 0, 0))
    return pl.pallas_call(
        body, name="rwkv_scan_fwd", grid=(nch,), in_specs=[tile] * 6,
        out_specs=[tile, tile, pl.BlockSpec((1, HEAD, 8, 128), lambda c: (c, 0, 0, 0))],
        out_shape=[jax.ShapeDtypeStruct((T, 8, 128), F32)] * 2 + [jax.ShapeDtypeStruct((nch, HEAD, 8, 128), F32)],
        scratch_shapes=[pltpu.VMEM((HEAD, 8, 128), F32)] + [pltpu.VMEM((6, HEAD, 128), F32)] * 2, compiler_params=_cp(("arbitrary",)),
    )(*xes, vi)


def _scan_bwd_a(xes, dyi):
    T, L = dyi.shape[0], SCAN_L
    nch = T // L

    def body(*refs):
        xr, (dy_ref, dsa_ref, dv_ref, g_ref, e0, e1) = refs[:5], refs[5:]

        @pl.when(pl.program_id(0) == 0)
        def _():
            g_ref[...] = jnp.zeros_like(g_ref)

        expand = _expander(xr, [dy_ref])
        expand(L - 1, e0)

        def step(t, e_ref):
            dy = e_ref[5, 0:8, :]
            row = lambda m, j: jnp.broadcast_to(e_ref[m, pl.ds(j, 1), :], (8, 128))
            dsa, dv = _Acc(), _Acc()
            for j in range(HEAD):
                g = g_ref[j] + row(4, j) * dy
                g_ref[j] = g
                dsa.add(j, g * row(2, j))
                dv.add(j, g * row(3, j))
            dsa = dsa.total()
            dsa_ref[t] = dsa
            dv_ref[t] = dv.total()
            for j in range(HEAD):
                g_ref[j] = g_ref[j] * row(1, j) + row(0, j) * dsa

        def pair(p, carry):
            t = L - 1 - 2 * p
            expand(t - 1, e1)
            step(t, e0)
            expand(jnp.maximum(t - 2, 0), e0)
            step(t - 1, e1)
            return carry

        lax.fori_loop(0, L // 2, pair, 0)
        _tiles_to_perm([dsa_ref, dv_ref], L)

    tile = pl.BlockSpec((L, 8, 128), lambda c: (nch - 1 - c, 0, 0))
    return pl.pallas_call(
        body, name="rwkv_scan_bwd_a", grid=(nch,), in_specs=[tile] * 6, out_specs=[tile, tile],
        out_shape=[jax.ShapeDtypeStruct((T, 8, 128), F32)] * 2,
        scratch_shapes=[pltpu.VMEM((HEAD, 8, 128), F32)] + [pltpu.VMEM((6, HEAD, 128), F32)] * 2, compiler_params=_cp(("arbitrary",)),
    )(*xes, dyi)


def _scan_bwd_b(xts, ies, ckb):
    T, L = xts[0].shape[0], SCAN_L
    nch = T // L

    def body(*refs):
        xr, er, ck_ref, dj, (hist, g_ref, e0, e1) = refs[:5], refs[5:9], refs[9], refs[10:15], refs[15:]

        @pl.when(pl.program_id(0) == 0)
        def _():
            g_ref[...] = jnp.zeros_like(g_ref)

        hist[0] = ck_ref[0]
        expand_vs = _expander(er[:2], [xr[1], xr[2], xr[3]])
        expand = _expander(er, [xr[0], xr[1], xr[4]])
        expand_vs(0, e0)

        def fstep(t, e_ref):
            w, B, k = e_ref[2, 0:8, :], e_ref[3, 0:8, :], e_ref[4, 0:8, :]
            row = lambda m, i: jnp.broadcast_to(e_ref[m, pl.ds(i, 1), :], (8, 128))
            for i in range(HEAD):
                hist[t + 1, i] = hist[t, i] * w + row(1, i) * B + row(0, i) * k

        def fpair(p, carry):
            t = 2 * p
            expand_vs(t + 1, e1)
            fstep(t, e0)
            expand_vs(jnp.minimum(t + 2, L - 1), e0)
            fstep(t + 1, e1)
            return carry

        lax.fori_loop(0, L // 2, fpair, 0)
        expand(L - 1, e0)

        def bstep(t, e_ref):
            A, w, r = e_ref[4, 0:8, :], e_ref[5, 0:8, :], e_ref[6, 0:8, :]
            row = lambda m, i: jnp.broadcast_to(e_ref[m, pl.ds(i, 1), :], (8, 128))
            acc = [_Acc() for _ in range(5)]
            for i in range(HEAD):
                dy_i, dsa_i = row(2, i), row(3, i)
                g = g_ref[i] + dy_i * r
                sp = hist[t, i]
                acc[4].add(i, hist[t + 1, i] * dy_i)
                acc[1].add(i, g * sp)
                acc[2].add(i, g * row(1, i))
                acc[3].add(i, g * row(0, i))
                acc[0].add(i, sp * dsa_i)
                g_ref[i] = g * w + dsa_i * A
            for m in range(5):
                dj[m][t] = acc[m].total()

        def bpair(p, carry):
            t = L - 1 - 2 * p
            expand(t - 1, e1)
            bstep(t, e0)
            expand(jnp.maximum(t - 2, 0), e0)
            bstep(t - 1, e1)
            return carry

        lax.fori_loop(0, L // 2, bpair, 0)
        _tiles_to_perm(dj, L)

    tile = pl.BlockSpec((L, 8, 128), lambda c: (nch - 1 - c, 0, 0))
    return pl.pallas_call(
        body, name="rwkv_scan_bwd_b", grid=(nch,),
        in_specs=[tile] * 9 + [pl.BlockSpec((1, HEAD, 8, 128), lambda c: (nch - 1 - c, 0, 0, 0))],
        out_specs=[tile] * 5, out_shape=[jax.ShapeDtypeStruct((T, 8, 128), F32)] * 5,
        scratch_shapes=[pltpu.VMEM((L + 1, HEAD, 8, 128), F32), pltpu.VMEM((HEAD, 8, 128), F32)] + [pltpu.VMEM((7, HEAD, 128), F32)] * 2,
        compiler_params=_cp(("arbitrary",)),
    )(*xts, *ies, ckb)


SWA_COLS = SWA_W + 2 * KV_W
BLK = 128


def _swa_core(n, k2a, k2b, vla, vra, vlb, vrb, sinks, *qps):
    iq = lax.broadcasted_iota(jnp.int32, (BLK, 2 * BLK), 0)
    ik = lax.broadcasted_iota(jnp.int32, (BLK, 2 * BLK), 1)
    diff = BLK + iq - ik
    valid = (diff >= 0) & (diff < WINDOW) & ((n > 0) | (ik >= BLK))
    lane = lax.broadcasted_iota(jnp.int32, (BLK, 128), 1)
    lane1 = lax.broadcasted_iota(jnp.int32, (1, 128), 1)
    nt = (((1,), (1,)), ((), ()))
    outs = []
    for pp in range(8):
        k2, vl, vr = (k2a, vla, vra) if pp < 4 else (k2b, vlb, vrb)
        qp = qps[pp]
        o = None
        for half, vv in ((0, vl), (1, vr)):
            qh = jnp.where((lane >= HEAD) == (half == 1), qp, 0.0).astype(BF16)
            s = lax.dot_general(qh, k2.astype(BF16), nt, preferred_element_type=F32) * (HEAD ** -0.5)
            s = jnp.where(valid, s, NEG_INF)
            sink = jnp.sum(jnp.where(lane1 == 2 * pp + half, sinks, 0.0), axis=1, keepdims=True)
            m = jnp.maximum(jnp.max(s, axis=1, keepdims=True), sink)
            p = jnp.exp(s - m)
            den = jnp.sum(p, axis=1, keepdims=True) + jnp.exp(sink - m)
            oh = jnp.dot((p / den).astype(BF16), vv.astype(BF16), preferred_element_type=F32)
            o = oh if o is None else o + oh
        outs.append(o)
    return jnp.concatenate(outs, axis=1)


def _swa_prep(pc, pp, b, cq, sq, ckc, skc, ckp, skp):
    zc, zp = pc + b, pp + b
    qr = zc[:, :SWA_W] * cq + _swap32(zc[:, :SWA_W]) * sq
    kc, kp = zc[:, SWA_W:SWA_W + KV_W], zp[:, SWA_W:SWA_W + KV_W]
    kb = jnp.concatenate([kp * ckp + _swap32(kp) * skp, kc * ckc + _swap32(kc) * skc], axis=0)
    vb = jnp.concatenate([zp[:, SWA_W + KV_W:], zc[:, SWA_W + KV_W:]], axis=0)
    lane = lax.broadcasted_iota(jnp.int32, kb.shape, 1)
    left = lane < HEAD
    kbr, vbr = pltpu.roll(kb, HEAD, 1), pltpu.roll(vb, HEAD, 1)
    return (jnp.where(left, kb, kbr), jnp.where(left, kbr, kb), jnp.where(left, vb, 0.0), jnp.where(left, 0.0, vbr),
            jnp.where(left, vbr, 0.0), jnp.where(left, 0.0, vb)), [qr[:, q * 128:(q + 1) * 128] for q in range(8)]


def _swa_specs(T, tabs_q, tabs_k):
    cur = lambda c: pl.BlockSpec((BLK, c), lambda n: (n, 0))
    prev = lambda c: pl.BlockSpec((BLK, c), lambda n: (jnp.maximum(n - 1, 0), 0))
    return cur, prev


def _swa_fwd(p_swa, b, sinks, cq, sq, ck, sk):
    T = p_swa.shape[0]
    cur, prev = _swa_specs(T, None, None)

    def body(pc, pp, b_ref, s_ref, cq_r, sq_r, ckc, skc, ckp, skp, o_ref):
        ops, qps = _swa_prep(pc[...], pp[...], b_ref[...], cq_r[...], sq_r[...], ckc[...], skc[...], ckp[...], skp[...])
        o_ref[...] = _swa_core(pl.program_id(0), *ops, s_ref[...], *qps).astype(o_ref.dtype)

    full = lambda a: pl.BlockSpec(a.shape, lambda n: (0, 0))
    return pl.pallas_call(
        body, name="swa_fwd", grid=(T // BLK,),
        in_specs=[cur(SWA_COLS), prev(SWA_COLS), full(b), full(sinks), cur(SWA_W), cur(SWA_W), cur(KV_W), cur(KV_W), prev(KV_W), prev(KV_W)],
        out_specs=cur(SWA_W), out_shape=jax.ShapeDtypeStruct((T, SWA_W), BF16), compiler_params=_cp(("arbitrary",)),
    )(p_swa, p_swa, b, sinks, cq, sq, ck, sk, ck, sk)


def _swa_bwd(p_swa, b, sinks, cq, sq, ck, sk, do):
    T = p_swa.shape[0]
    nb = T // BLK
    cur = lambda c: pl.BlockSpec((BLK, c), lambda s: (nb - 1 - s, 0))
    prev = lambda c: pl.BlockSpec((BLK, c), lambda s: (jnp.maximum(nb - 2 - s, 0), 0))

    def body(pc, pp, b_ref, s_ref, cq_r, sq_r, ckc, skc, ckp, skp, do_ref, dcur, db, dsk, carry):
        step = pl.program_id(0)
        n = nb - 1 - step

        @pl.when(step == 0)
        def _():
            carry[...] = jnp.zeros_like(carry)
            db[...] = jnp.zeros_like(db)
            dsk[...] = jnp.zeros_like(dsk)

        ops, qps = _swa_prep(pc[...], pp[...], b_ref[...], cq_r[...], sq_r[...], ckc[...], skc[...], ckp[...], skp[...])
        _, vjp = jax.vjp(functools.partial(_swa_core, n), *ops, s_ref[...], *qps)
        dk2a, dk2b, dvla, dvra, dvlb, dvrb, dsinks, *dqps = vjp(do_ref[...].astype(F32))
        dqr = jnp.concatenate(dqps, axis=1)
        lane = lax.broadcasted_iota(jnp.int32, dk2a.shape, 1)
        left = lane < HEAD
        dkb = jnp.where(left, dk2a + pltpu.roll(dk2a, HEAD, 1), dk2b + pltpu.roll(dk2b, HEAD, 1))
        dvb = jnp.where(left, dvla + pltpu.roll(dvra, HEAD, 1), pltpu.roll(dvlb, HEAD, 1) + dvrb)
        dq = dqr * cq_r[...] + _swap32(dqr * sq_r[...])
        dkp, dkc = dkb[:BLK], dkb[BLK:]
        dkp = dkp * ckp[...] + _swap32(dkp * skp[...])
        dkc = dkc * ckc[...] + _swap32(dkc * skc[...])
        dc = jnp.concatenate([dq, jnp.concatenate([dkc, dvb[BLK:]], axis=1) + carry[...]], axis=1)
        carry[...] = jnp.concatenate([dkp, dvb[:BLK]], axis=1)
        dcur[...] = dc.astype(dcur.dtype)
        db[...] += jnp.sum(dc, axis=0, keepdims=True)
        dsk[...] += dsinks

    full = lambda a: pl.BlockSpec(a.shape, lambda s: (0, 0))
    return pl.pallas_call(
        body, name="swa_bwd", grid=(nb,),
        in_specs=[cur(SWA_COLS), prev(SWA_COLS), full(b), full(sinks), cur(SWA_W), cur(SWA_W), cur(KV_W), cur(KV_W), prev(KV_W), prev(KV_W),
                  cur(SWA_W)],
        out_specs=[cur(SWA_COLS), full(b), full(sinks)],
        out_shape=[jax.ShapeDtypeStruct((T, SWA_COLS), BF16), jax.ShapeDtypeStruct(b.shape, F32), jax.ShapeDtypeStruct(sinks.shape, F32)],
        scratch_shapes=[pltpu.VMEM((BLK, 2 * KV_W), F32)], compiler_params=_cp(("arbitrary",)),
    )(p_swa, p_swa, b, sinks, cq, sq, ck, sk, ck, sk, do)


def _rope_tables(T):
    inv = 10000.0 ** (-jnp.arange(0, HEAD, 2, dtype=F32) / HEAD)
    ang = jnp.arange(T, dtype=F32)[:, None] * inv[None, :]
    c = jnp.concatenate([jnp.cos(ang), jnp.cos(ang)], axis=1)
    s = jnp.concatenate([-jnp.sin(ang), jnp.sin(ang)], axis=1)
    return jnp.tile(c, (1, 16)), jnp.tile(s, (1, 16)), jnp.tile(c, (1, 2)), jnp.tile(s, (1, 2))


def _xattn_core(*qkv):
    outs = []
    for h in range(XH):
        qh, kh, vh = qkv[h], qkv[XH + h], qkv[2 * XH + h]
        s = lax.dot_general(qh.astype(BF16), kh.astype(BF16), (((1,), (1,)), ((), ())), preferred_element_type=F32) * (XHD ** -0.5)
        p = jnp.exp(s - jnp.max(s, axis=1, keepdims=True))
        p = p / jnp.sum(p, axis=1, keepdims=True)
        outs.append(jnp.dot(p.astype(BF16), vh.astype(BF16), preferred_element_type=F32))
    return jnp.concatenate(outs, axis=1)


def _xattn_split(q, kv):
    return [q[:, h * XHD:(h + 1) * XHD] for h in range(XH)] + [kv[:, h * XHD:(h + 1) * XHD] for h in range(2 * XH)]


def _xattn_fwd(q, kv, tm=256):
    (o,) = _rows(lambda i, q, kv: (_xattn_core(*_xattn_split(q, kv)),), "xattn_fwd", q.shape[0], tm, [q], [kv], [(q.shape[1], BF16)], [])
    return o


def _xattn_bwd(q, kv, do, tm=256):
    def fn(i, q, do, kv):
        _, vjp = jax.vjp(_xattn_core, *_xattn_split(q, kv))
        d = vjp(do.astype(F32))
        return jnp.concatenate(d[:XH], axis=1), jnp.concatenate(d[XH:], axis=1)

    return _rows(fn, "xattn_bwd", q.shape[0], tm, [q, do], [kv], [(q.shape[1], BF16)], [(kv.shape, F32)])


def _loss_head(x, g, tgt, tm=256):
    D = x.shape[1]

    def fn(i, x, tgt, g):
        y, vjp = jax.vjp(_rms, x, g)
        err = y - tgt
        dx, dg = vjp(err * (1.0 / D))
        part = 0.5 / D * jnp.sum(jnp.sum(err * err, axis=1, keepdims=True), axis=0, keepdims=True)
        return dx, jnp.broadcast_to(part, (1, 128)), dg

    return _rows(fn, "loss_head", x.shape[0], tm, [x, tgt], [g], [(D, F32)], [((1, 128), F32), ((1, D), F32)])


def _local_step(x, mem, tgt, get_w, P, put_g):
    T = x.shape[0]
    W = dict(get_w("f1", None))

    def f1_down(after):
        W.update(get_w("f1d", after))
        return W["f1_down"]

    x1, s1 = _ffn_fwd(x, P["f1_norm"], W["f1_gate"], W["f1_up"], f1_down, "f1")

    W.update(get_w("mix", x1))
    h2 = _rms_fwd(x1, P["mix_norm"], "mix_norm")
    w_rkv, w_lora, w_swa = W["w_inT"][:3 * RW_W], W["w_inT"][3 * RW_W:SHIFT_COLS], W["w_inT"][SHIFT_COLS:]
    p_rkv = _mm(h2, w_rkv, "nt", "in_rkv", after=W.get("_after"))
    p_lora = _mm(h2, w_lora, "nt", "in_lora")
    p_swa = _mm(h2, w_swa, "nt", "in_swa")
    w_da = jnp.concatenate([W["rw_decay_up"], W["rw_aaa_up"]], axis=0)
    pre_params = (P["rw_mu"][:, :3 * RW_W], P["rw_mu"][:, 3 * RW_W:], P["rw_w0"], P["rw_a0"], P["rw_k_k"], P["rw_k_a"], w_da,
                  W["rw_gate_up"])
    r, decay, k2, v, an, bn, g = _rwkv_pre(p_rkv, p_lora, pre_params)
    scan_vecs = (an, decay, bn, k2, r)
    xes = [_to_perm(a) for a in scan_vecs]
    v_p = _to_perm(v)
    yi, sai, ck = _scan_fwd(xes, v_p)
    y_scan = _from_perm(yi)
    y_rw = _rwkv_post(y_scan, r, k2, v, g, P["rw_lnx_w"], P["rw_lnx_b"], P["rw_r_k"])
    cq, sq, ckt, skt = _rope_tables(T)
    y_swa = _swa_fwd(p_swa, P["b_in_attn"], P["attn_sinks"], cq, sq, ckt, skt)
    ycat = jnp.concatenate([y_rw, y_swa], axis=1)
    x2 = _mm(ycat, W["w_out"], "nn", "out_proj", res=x1, bias=P["b_out"])

    W.update(get_w("xattn", x2))
    hx = _rms_fwd(x2, P["xa_norm"], "xa_norm")
    mn = _rms_fwd(mem, P["mem_norm"], "mem_norm")
    q = _mm(hx, W["w_xq"], "nn", "xq", out_dtype=BF16)
    kv = _mm(mn, W["w_xkv"], "nn", "xkv", out_dtype=BF16)
    o = _xattn_fwd(q, kv)
    x3 = _mm(o, W["w_xo"], "nn", "xo", res=x2)

    W.update(get_w("f2", x3))
    x4, s2 = _ffn_fwd(x3, P["f2_norm"], W["f2_gate"], W["f2_up"], W["f2_down"], "f2")
    dx4, loss_part, d_final = _loss_head(x4, P["final_norm"], tgt)

    gs = {"final_norm": d_final}
    dx3, gs["f2_norm"] = _ffn_bwd(x3, P["f2_norm"], W["f2_gate"], W["f2_up"], W["f2_down"], s2, dx4, "f2", put_g)

    do = _mm(dx3, W["w_xo"], "nt", "xo_do", out_dtype=BF16)
    dw_xo = _mm(o, dx3, "tn", "xo_dw", out_dtype=BF16)
    dq, dkv = _xattn_bwd(q, kv, do)
    dw_xq = _mm(hx, dq, "tn", "xq_dw", out_dtype=BF16)
    dw_xkv = _mm(mn, dkv, "tn", "xkv_dw", out_dtype=BF16)
    sent = put_g("xattn", {"w_xq": dw_xq, "w_xkv": dw_xkv, "w_xo": dw_xo})
    dhx = _mm(dq, W["w_xq"], "nt", "xq_dh", after=sent)
    dmn = _mm(dkv, W["w_xkv"], "nt", "xkv_dmn")
    _, gs["mem_norm"], _ = _rms_bwd(mem, P["mem_norm"], dmn, jnp.zeros_like(mem), "mem_norm_bwd")
    dx2, gs["xa_norm"], gs["b_out"] = _rms_bwd(x2, P["xa_norm"], dhx, dx3, "xa_norm_bwd")

    dycat = _mm(dx2, W["w_out"], "nt", "out_dy")
    dw_out = _mm(ycat, dx2, "tn", "out_dw", out_dtype=BF16)
    dp_swa, gs["b_in_attn"], gs["attn_sinks"] = _swa_bwd(p_swa, P["b_in_attn"], P["attn_sinks"], cq, sq, ckt, skt, dycat[:, RW_W:])
    dy_scan, dr_b, dk2_b, dv_b, dg, gs["rw_lnx_w"], gs["rw_lnx_b"], gs["rw_r_k"] = _rwkv_post_bwd(
        y_scan, r, k2, v, g, P["rw_lnx_w"], P["rw_lnx_b"], P["rw_r_k"], dycat[:, :RW_W])
    dy_p = _to_perm(dy_scan)
    dsai, dvi = _scan_bwd_a(xes, dy_p)
    dj = _scan_bwd_b(xes, [v_p, sai, dy_p, dsai], _ck_a_to_b(ck))
    dan, ddecay, dbn, dk2_s, dr_s = (_from_perm(d) for d in dj)
    cts = (dr_s, ddecay, dk2_s, _from_perm(dvi), dan, dbn, dg, dr_b, dk2_b, dv_b)
    dp_rkv, dp_lora, dmu, dmul, gs["rw_w0"], gs["rw_a0"], gs["rw_k_k"], gs["rw_k_a"], dw_da, gs["rw_gate_up"] = _rwkv_pre_bwd(
        p_rkv, p_lora, pre_params, cts)
    gs["rw_mu"] = jnp.concatenate([dmu, dmul], axis=1)
    gs["rw_decay_up"], gs["rw_aaa_up"] = dw_da[:DECAY_LORA], dw_da[DECAY_LORA:]
    dw_inT = jnp.concatenate([_mm(dp_rkv, h2, "tn", "in_dw_rkv"), _mm(dp_lora, h2, "tn", "in_dw_lora"),
                              _mm(dp_swa, h2, "tn", "in_dw_swa")], axis=0)
    sent = put_g("mix", {"w_in": dw_inT, "w_out": dw_out})
    dh2 = _mm(dp_rkv, w_rkv, "nn", "in_dh_rkv", after=sent)
    dh2 = _mm(dp_lora, w_lora, "nn", "in_dh_lora", res=dh2)
    dh2 = _mm(dp_swa, w_swa, "nn", "in_dh_swa", res=dh2)
    dx1, gs["mix_norm"], _ = _rms_bwd(x1, P["mix_norm"], dh2, dx2, "mix_norm_bwd")

    dx0, gs["f1_norm"] = _ffn_bwd(x, P["f1_norm"], W["f1_gate"], W["f1_up"], W["f1_down"], s1, dx1, "f1", put_g)
    return loss_part, dx0, gs


_ANY = pl.BlockSpec(memory_space=pl.ANY)
_OTHER_CHIPS = ((1, 0), (0, 1), (1, 1))


def _mesh_pos():
    return lax.axis_index("x"), lax.axis_index("y"), lax.axis_index("c")


def _slot(ref, kind, s, rows, cols):
    if kind == "row":
        return ref.at[pl.ds(pl.multiple_of(s * rows, 8), rows), :]
    return ref.at[:, pl.ds(pl.multiple_of(s * cols, 128), cols)]


_HBM = pl.BlockSpec(memory_space=pltpu.HBM)
_SEMS = pl.BlockSpec(memory_space=pltpu.SEMAPHORE)
_SPLIT = dict(compiler_params=pltpu.CompilerParams(has_side_effects=pltpu.SideEffectType.DATAFLOW_SIDE_EFFECTING))


def _in_hbm(a):
    return pltpu.with_memory_space_constraint(a, pltpu.HBM)


def _full_shape(s, kind):
    return (4 * s.shape[0], s.shape[1]) if kind == "row" else (s.shape[0], 4 * s.shape[1])


def _half(ref, shape, h):
    rows, cols = shape
    if rows % 32 == 0:
        return ref.at[pl.ds(pl.multiple_of(h * (rows // 2), 16), rows // 2), :]
    assert cols % 256 == 0, shape
    return ref.at[:, pl.ds(pl.multiple_of(h * (cols // 2), 128), cols // 2)]


def _swap_halves(name, fulls, shard_shapes, kinds):
    n = len(fulls)

    def body(*refs):
        out, send, recv = refs[n:2 * n], refs[2 * n], refs[2 * n + 1]
        x, y, c = _mesh_pos()
        sent = []
        for i in range(n):
            for r, (dx, dy) in enumerate(_OTHER_CHIPS):
                theirs = _slot(out[i], kinds[i], 2 * ((x + dx) % 2) + (y + dy) % 2, *shard_shapes[i])
                have = _half(theirs, shard_shapes[i], c)
                rc = pltpu.make_async_remote_copy(have, have, send.at[3 * i + r], recv.at[3 * i + r], device_id=(x, y, 1 - c),
                                                  device_id_type=MESH)
                rc.start()
                sent.append(rc)
        for i in range(n):
            for r, (dx, dy) in enumerate(_OTHER_CHIPS):
                theirs = _slot(out[i], kinds[i], 2 * ((x + dx) % 2) + (y + dy) % 2, *shard_shapes[i])
                need = _half(theirs, shard_shapes[i], 1 - c)
                pltpu.make_async_remote_copy(need, need, send.at[3 * i + r], recv.at[3 * i + r], device_id=(x, y, c),
                                             device_id_type=MESH).wait_recv()
        for rc in sent:
            rc.wait_send()

    return pl.pallas_call(
        body, name=name, in_specs=[_ANY] * n, out_specs=[_ANY] * n, out_shape=[jax.ShapeDtypeStruct(f.shape, f.dtype) for f in fulls],
        input_output_aliases={i: i for i in range(n)},
        scratch_shapes=[pltpu.SemaphoreType.DMA((3 * n,)), pltpu.SemaphoreType.DMA((3 * n,))],
    )(*fulls)


def _gather_start(name, shards, kinds, groups, after=None):
    n, ng = len(shards), len(groups)
    lands = [_in_hbm(lax.empty(_full_shape(s, k), s.dtype)) for s, k in zip(shards, kinds)]
    n_in = 2 * n + (after is not None)

    def body(*refs):
        src, land, sems, token = refs[:n], refs[n:2 * n], refs[n_in:n_in + 3 * ng], refs[-1]
        x, y, c = _mesh_pos()
        me = 2 * x + y
        for gi, idxs in enumerate(groups):
            send, recv, own = sems[3 * gi:3 * gi + 3]
            for k, i in enumerate(idxs):
                mine = _slot(land[i], kinds[i], me, *src[i].shape)
                for r, (dx, dy) in enumerate(_OTHER_CHIPS):
                    pltpu.make_async_remote_copy(_half(src[i], src[i].shape, c), _half(mine, src[i].shape, c), send.at[3 * k + r],
                                                 recv.at[3 * k + r], device_id=((x + dx) % 2, (y + dy) % 2, c), device_id_type=MESH).start()
                pltpu.make_async_copy(src[i], mine, own.at[k]).start()
        token[...] = jnp.zeros_like(token)

    sem_shapes = [pltpu.SemaphoreType.DMA((w * len(g),)) for g in groups for w in (3, 3, 1)]
    thru = [pltpu.HBM(a.shape, a.dtype) for a in (*shards, *lands)]
    res = pl.pallas_call(
        body, name=name, in_specs=[_HBM] * (2 * n) + [_ANY] * (after is not None),
        out_specs=[_SEMS] * (3 * ng) + [_HBM] * (2 * n) + [pl.BlockSpec(memory_space=pltpu.VMEM)],
        out_shape=sem_shapes + thru + [jax.ShapeDtypeStruct((8, 128), F32)],
        input_output_aliases={i: 3 * ng + i for i in range(2 * n)}, **_SPLIT,
    )(*[_in_hbm(s) for s in shards], *lands, *([] if after is None else [after]))
    return res[:3 * ng], res[3 * ng:3 * ng + n], res[3 * ng + n:3 * ng + 2 * n], res[-1]


def _gather_wait(name, sems, shards, lands, kinds, after):
    m = len(shards)

    def body(*refs):
        src, land, (send, recv, own) = refs[:m], refs[m:2 * m], refs[2 * m:2 * m + 3]
        x, y, c = _mesh_pos()
        me = 2 * x + y
        for k in range(m):
            mine = _slot(land[k], kinds[k], me, *src[k].shape)
            for r in range(3):
                cp = pltpu.make_async_remote_copy(_half(src[k], src[k].shape, c), _half(mine, src[k].shape, c), send.at[3 * k + r],
                                                  recv.at[3 * k + r], device_id=(x, y, c), device_id_type=MESH)
                cp.wait_send()
                cp.wait_recv()
            pltpu.make_async_copy(src[k], mine, own.at[k]).wait()

    thru = [pltpu.HBM(a.shape, a.dtype) for a in (*shards, *lands)]
    res = pl.pallas_call(
        body, name=name, in_specs=[_HBM] * (2 * m) + [_SEMS] * 3 + [pl.BlockSpec(memory_space=pl.ANY)],
        out_specs=[_HBM] * (2 * m), out_shape=thru, input_output_aliases={i: i for i in range(2 * m)}, **_SPLIT,
    )(*shards, *lands, *sems, after)
    return res[m:]


def _scatter_start(name, grads, kinds):
    m = len(grads)
    shard_shape = [(g.shape[0] // 4, g.shape[1]) if k == "row" else (g.shape[0], g.shape[1] // 4) for g, k in zip(grads, kinds)]
    lands = [_in_hbm(lax.empty((4, *s), g.dtype)) for s, g in zip(shard_shape, grads)]

    def body(*refs):
        src, land, (send, recv, own) = refs[:m], refs[m:2 * m], refs[2 * m:2 * m + 3]
        x, y, c = _mesh_pos()
        me = 2 * x + y
        for k in range(m):
            for r, (dx, dy) in enumerate(_OTHER_CHIPS):
                tx, ty = (x + dx) % 2, (y + dy) % 2
                pltpu.make_async_remote_copy(_slot(src[k], kinds[k], 2 * tx + ty, *shard_shape[k]), land[k].at[me],
                                             send.at[3 * k + r], recv.at[3 * k + r], device_id=(tx, ty, c), device_id_type=MESH).start()
            pltpu.make_async_copy(_slot(src[k], kinds[k], me, *shard_shape[k]), land[k].at[me], own.at[k]).start()
        refs[-1][...] = jnp.zeros_like(refs[-1])

    thru = [pltpu.HBM(a.shape, a.dtype) for a in (*grads, *lands)]
    res = pl.pallas_call(
        body, name=name, in_specs=[_HBM] * (2 * m),
        out_specs=[_SEMS] * 3 + [_HBM] * (2 * m) + [pl.BlockSpec(memory_space=pltpu.VMEM)],
        out_shape=[pltpu.SemaphoreType.DMA((3 * m,))] * 2 + [pltpu.SemaphoreType.DMA((m,))] + thru + [jax.ShapeDtypeStruct((8, 128), F32)],
        input_output_aliases={i: 3 + i for i in range(2 * m)}, **_SPLIT,
    )(*[_in_hbm(g) for g in grads], *lands)
    return res[:3], res[3:3 + m], res[3 + m:3 + 2 * m], res[-1]


def _scatter_wait(name, sems, grads, lands, kinds, after):
    m = len(grads)

    def body(*refs):
        src, land, (send, recv, own) = refs[:m], refs[m:2 * m], refs[2 * m:2 * m + 3]
        x, y, c = _mesh_pos()
        me = 2 * x + y
        for k in range(m):
            mine = _slot(src[k], kinds[k], me, *land[k].shape[1:])
            for r in range(3):
                cp = pltpu.make_async_remote_copy(mine, land[k].at[me], send.at[3 * k + r], recv.at[3 * k + r],
                                                  device_id=(x, y, c), device_id_type=MESH)
                cp.wait_send()
                cp.wait_recv()
            pltpu.make_async_copy(mine, land[k].at[me], own.at[k]).wait()

    thru = [pltpu.HBM(a.shape, a.dtype) for a in (*grads, *lands)]
    res = pl.pallas_call(
        body, name=name, in_specs=[_HBM] * (2 * m) + [_SEMS] * 3 + [pl.BlockSpec(memory_space=pl.ANY)],
        out_specs=[_HBM] * (2 * m), out_shape=thru, input_output_aliases={i: i for i in range(2 * m)}, **_SPLIT,
    )(*grads, *lands, *sems, after)
    return res[m:]


def _swap_with_sibling(arrs, name):
    n = len(arrs)

    def body(*refs):
        ins, outs = refs[:n], refs[n:2 * n]
        send, recv = refs[2 * n:]
        x, y, c = _mesh_pos()
        copies = []
        for i in range(n):
            rc = pltpu.make_async_remote_copy(ins[i], outs[i], send.at[i], recv.at[i], device_id=(x, y, 1 - c), device_id_type=MESH)
            rc.start()
            copies.append(rc)
        for rc in copies:
            rc.wait()

    return pl.pallas_call(
        body, name=name, in_specs=[_ANY] * n, out_specs=[_ANY] * n,
        out_shape=[jax.ShapeDtypeStruct(a.shape, a.dtype) for a in arrs],
        scratch_shapes=[pltpu.SemaphoreType.DMA((n,)), pltpu.SemaphoreType.DMA((n,))],
    )(*arrs)


def _small_start(pack, after):
    land = _in_hbm(lax.empty((8, *pack.shape), pack.dtype))

    def body(in_ref, land_ref, after_ref, send, recv, own, in_thru, land_thru, token):
        x, y, c = _mesh_pos()
        me = 4 * x + 2 * y + c
        for r in range(1, 8):
            dx, dy, dc = r // 4, (r // 2) % 2, r % 2
            pltpu.make_async_remote_copy(in_ref, land_ref.at[me], send.at[r - 1], recv.at[r - 1],
                                         device_id=((x + dx) % 2, (y + dy) % 2, (c + dc) % 2), device_id_type=MESH).start()
        pltpu.make_async_copy(in_ref, land_ref.at[me], own.at[0]).start()
        token[...] = jnp.zeros_like(token)

    res = pl.pallas_call(
        body, name="small_start", in_specs=[_HBM, _HBM, _ANY],
        out_specs=[_SEMS] * 3 + [_HBM, _HBM, pl.BlockSpec(memory_space=pltpu.VMEM)],
        out_shape=[pltpu.SemaphoreType.DMA((7,)), pltpu.SemaphoreType.DMA((7,)), pltpu.SemaphoreType.DMA((1,)),
                   pltpu.HBM(pack.shape, pack.dtype), pltpu.HBM(land.shape, land.dtype), jax.ShapeDtypeStruct((8, 128), F32)],
        input_output_aliases={0: 3, 1: 4}, **_SPLIT,
    )(_in_hbm(pack), land, after)
    return res[:3], res[3], res[4], res[5]


def _small_wait(sems, pack, land, after):
    def body(in_ref, land_ref, send, recv, own, after_ref, in_dead, got):
        x, y, c = _mesh_pos()
        me = 4 * x + 2 * y + c
        for r in range(1, 8):
            cp = pltpu.make_async_remote_copy(in_ref, land_ref.at[me], send.at[r - 1], recv.at[r - 1], device_id=(x, y, c),
                                              device_id_type=MESH)
            cp.wait_send()
            cp.wait_recv()
        pltpu.make_async_copy(in_ref, land_ref.at[me], own.at[0]).wait()

    res = pl.pallas_call(
        body, name="small_wait", in_specs=[_HBM, _HBM] + [_SEMS] * 3 + [_ANY], out_specs=[_HBM, _HBM],
        out_shape=[pltpu.HBM(pack.shape, pack.dtype), pltpu.HBM(land.shape, land.dtype)], input_output_aliases={0: 0, 1: 1}, **_SPLIT,
    )(pack, land, *sems, after)
    return res[1]


def _row_tile(R, dtype, target=256):
    mult = 8 * 4 // jnp.dtype(dtype).itemsize
    best = R
    for t in range(mult, min(R, target) + 1, mult):
        if R % t == 0:
            best = t
    return best


def _sum_slots(stack, name, out_dtype=F32):
    k, R, C = stack.shape
    tr = _row_tile(R, stack.dtype)

    def body(s_ref, o_ref):
        acc = s_ref[0].astype(F32)
        for j in range(1, k):
            acc = acc + s_ref[j].astype(F32)
        o_ref[...] = acc.astype(out_dtype)

    return pl.pallas_call(
        body, name=name, grid=(R // tr,), in_specs=[pl.BlockSpec((k, tr, C), lambda i: (0, i, 0))],
        out_specs=pl.BlockSpec((tr, C), lambda i: (i, 0)), out_shape=jax.ShapeDtypeStruct((R, C), out_dtype),
        compiler_params=_cp(("parallel",)),
    )(stack)


def _adamw(w, m, v, ga, gb, name, after=None):
    R, C = w.shape
    tr = _row_tile(R, F32, 128)
    gs = [ga] if gb is None else [ga, gb]
    extra = [] if after is None else [after]

    def body(*refs):
        w_ref, m_ref, v_ref = refs[:3]
        g = refs[3][...].astype(F32)
        if gb is not None:
            g = g + refs[4][...].astype(F32)
        g_ref, d_ref, nm_ref, nv_ref = refs[-4:]
        nm = ADAM_B1 * m_ref[...] + (1.0 - ADAM_B1) * g
        nv = ADAM_B2 * v_ref[...] + (1.0 - ADAM_B2) * (g * g)
        m_hat = nm / (1.0 - ADAM_B1 ** ADAM_STEP)
        v_hat = nv / (1.0 - ADAM_B2 ** ADAM_STEP)
        g_ref[...] = g
        d_ref[...] = -ADAM_LR * (m_hat / (jnp.sqrt(v_hat) + ADAM_EPS) + ADAM_WD * w_ref[...])
        nm_ref[...] = nm
        nv_ref[...] = nv

    spec = pl.BlockSpec((tr, C), lambda i: (i, 0))
    return pl.pallas_call(
        body, name=name, grid=(R // tr,), in_specs=[spec] * (3 + len(gs)) + [_ANY] * len(extra), out_specs=[spec] * 4,
        out_shape=[jax.ShapeDtypeStruct((R, C), F32)] * 4, compiler_params=_cp(("parallel",)),
    )(w, m, v, *gs, *extra)


def _pack(arrs):
    rows = []
    for a in arrs:
        flat = a.reshape(-1)
        rows.append(jnp.pad(flat, (0, -flat.shape[0] % 1024)).reshape(-1, 1024))
    p = jnp.concatenate(rows, axis=0)
    return jnp.pad(p, ((0, -p.shape[0] % 8), (0, 0)))


def _unpack(p, shapes):
    out, r = [], 0
    for s in shapes:
        n = 1
        for d in s:
            n *= d
        nr = -(-n // 1024)
        out.append(p[r:r + nr].reshape(-1)[:n].reshape(s))
        r += nr
    return out


BIG = ("f1_gate", "f1_up", "f1_down", "w_in", "w_out", "w_xq", "w_xkv", "w_xo", "f2_gate", "f2_up", "f2_down")
BIG_KIND = {"f1_gate": "col", "f1_up": "col", "f1_down": "row", "w_in": "row", "w_out": "row", "w_xq": "row", "w_xkv": "col",
            "w_xo": "row", "f2_gate": "col", "f2_up": "col", "f2_down": "row"}
LORA = ("rw_decay_up", "rw_aaa_up", "rw_gate_up")
WEIGHTS = ("f1_norm", "f1_gate", "f1_up", "f1_down", "mix_norm", "w_in", "b_in_attn", "rw_mu", "rw_w0", "rw_decay_up", "rw_a0",
           "rw_aaa_up", "rw_gate_up", "rw_k_k", "rw_k_a", "rw_r_k", "rw_lnx_w", "rw_lnx_b", "attn_sinks", "w_out", "b_out", "xa_norm",
           "mem_norm", "w_xq", "w_xkv", "w_xo", "f2_norm", "f2_gate", "f2_up", "f2_down", "final_norm")
SMALL = tuple(n for n in WEIGHTS if n not in BIG)
GROUP_ORDER = ("f1", "f1d", "mix", "xattn", "f2")
GROUPS = {"f1": ("f1_gate", "f1_up"), "f1d": ("f1_down",), "mix": ("w_in", "w_out") + LORA, "xattn": ("w_xq", "w_xkv", "w_xo"),
          "f2": ("f2_gate", "f2_up", "f2_down")}


def kernel(x, mem, f1_norm, f1_gate, f1_up, f1_down, mix_norm, w_in, b_in_attn, rw_mu, rw_w0, rw_decay_up, rw_a0, rw_aaa_up, rw_gate_up, rw_k_k, rw_k_a, rw_r_k, rw_lnx_w, rw_lnx_b, attn_sinks, w_out, b_out, xa_norm, mem_norm, w_xq, w_xkv, w_xo, f2_norm, f2_gate, f2_up, f2_down, final_norm, loss_target, m_f1_norm, m_f1_gate, m_f1_up, m_f1_down, m_mix_norm, m_w_in, m_b_in_attn, m_rw_mu, m_rw_w0, m_rw_decay_up, m_rw_a0, m_rw_aaa_up, m_rw_gate_up, m_rw_k_k, m_rw_k_a, m_rw_r_k, m_rw_lnx_w, m_rw_lnx_b, m_attn_sinks, m_w_out, m_b_out, m_xa_norm, m_mem_norm, m_w_xq, m_w_xkv, m_w_xo, m_f2_norm, m_f2_gate, m_f2_up, m_f2_down, m_final_norm, v_f1_norm, v_f1_gate, v_f1_up, v_f1_down, v_mix_norm, v_w_in, v_b_in_attn, v_rw_mu, v_rw_w0, v_rw_decay_up, v_rw_a0, v_rw_aaa_up, v_rw_gate_up, v_rw_k_k, v_rw_k_a, v_rw_r_k, v_rw_lnx_w, v_rw_lnx_b, v_attn_sinks, v_w_out, v_b_out, v_xa_norm, v_mem_norm, v_w_xq, v_w_xkv, v_w_xo, v_f2_norm, v_f2_gate, v_f2_up, v_f2_down, v_final_norm):
    a = dict(locals())
    w = {n: a[n] for n in WEIGHTS}
    m = {n: a["m_" + n] for n in WEIGHTS}
    v = {n: a["v_" + n] for n in WEIGHTS}
    sq = lambda t: t.reshape(t.shape[-2:]) if t.ndim == 3 else t.reshape(1, -1)

    local_name = lambda n: "w_inT" if n == "w_in" else n
    kind_of = lambda n: BIG_KIND.get(n, "col")
    payload = lambda n: sq(w[n]).T if n == "w_in" else sq(w[n]) if n in LORA else sq(w[n]).astype(BF16)
    gathers = {}

    def start_gather(name, grps, after):
        shards = [payload(n) for g in grps for n in GROUPS[g]]
        kinds = [kind_of(n) for g in grps for n in GROUPS[g]]
        groups, at = [], 0
        for g in grps:
            groups.append(list(range(at, at + len(GROUPS[g]))))
            at += len(GROUPS[g])
        sems, src_thru, land_thru, token = _gather_start(name, shards, kinds, groups, after)
        for gi, g in enumerate(grps):
            gathers[g] = (sems[3 * gi:3 * gi + 3], [src_thru[i] for i in groups[gi]], [land_thru[i] for i in groups[gi]],
                          [kinds[i] for i in groups[gi]], token)

    early = GROUP_ORDER[:3]
    start_gather("gather_start", early, None)

    def get_w(grp, after):
        g_sems, g_src, g_land, g_kinds, token = gathers[grp]
        got = _gather_wait("gather_wait_" + grp, g_sems, g_src, g_land, g_kinds, token if after is None else after)
        got = _swap_halves("gather_swap_" + grp, got, [s.shape for s in g_src], g_kinds)
        out = {local_name(n): f for n, f in zip(GROUPS[grp], got)}
        if grp == early[-1]:
            start_gather("gather_start_late", GROUP_ORDER[3:], got[0])
            out["_after"] = gathers[GROUP_ORDER[3]][4]
        return out

    in_flight = []

    def put_g(label, gw):
        names = list(gw)
        *flight, sent = _scatter_start("scatter_start_" + label, [gw[n] for n in names], [kind_of(n) for n in names])
        in_flight.append((label, names, flight))
        return sent

    P = {n: sq(w[n]) for n in SMALL if n not in LORA}
    P["attn_sinks"] = jnp.pad(P["attn_sinks"], ((0, 0), (0, 128 - P["attn_sinks"].shape[1])))
    P["rw_r_k"] = w["rw_r_k"].reshape(1, RW_W)
    loss_part, grad_x, gs = _local_step(x[0], mem[0], loss_target[0], get_w, P, put_g)
    loss = lax.psum(loss_part[0, 0], ("x", "y", "c"))

    gs["attn_sinks"] = gs["attn_sinks"][:, :16]
    small_flight = _small_start(_pack([gs[n] for n in SMALL]), grad_x)

    out, after = {}, small_flight[-1]
    for label, names, (g_sems, g_thru, l_thru) in in_flight:
        stacks = _scatter_wait("scatter_wait_" + label, g_sems, g_thru, l_thru, [kind_of(n) for n in names], after)
        partial = [_sum_slots(s, "sum_chips_" + n, F32 if n == "w_in" else BF16) for s, n in zip(stacks, names)]
        sibling = _swap_with_sibling(partial, "swap_" + label)
        chain = None
        for n, pa, sb in zip(names, partial, sibling):
            if n == "w_in":
                pa, sb = pa.T, sb.T
            out[n] = _adamw(sq(w[n]), sq(m[n]), sq(v[n]), pa, sb, "adamw_" + n, after=chain)
            chain = out[n][1]
        after = chain

    gsum = _sum_slots(_small_wait(*small_flight[:-1], after), "sum_small")
    g_small = dict(zip(SMALL, _unpack(gsum, [gs[n].shape for n in SMALL])))
    shard = 2 * lax.axis_index("x") + lax.axis_index("y")
    for n in LORA:
        cols = w[n].shape[-1]
        g_small[n] = lax.dynamic_slice_in_dim(g_small[n], shard * cols, cols, axis=1)

    flat = lambda d: _pack([d[n] for n in SMALL])
    res = _adamw(flat(w), flat(m), flat(v), _pack([g_small[n] for n in SMALL]), None, "adamw_small")
    shapes = [w[n].shape for n in SMALL]
    for k, p in enumerate(res):
        for n, t in zip(SMALL, _unpack(p, shapes)):
            out.setdefault(n, [None] * 4)[k] = t
    outs = [loss, grad_x.reshape(x.shape)]
    for k in range(4):
        outs += [out[n][k].reshape(w[n].shape) for n in WEIGHTS]
    return tuple(outs)
```

```python
import functools

import jax
import jax.numpy as jnp
from jax import lax
from jax.experimental import pallas as pl
from jax.experimental.pallas import tpu as pltpu

F32, BF16 = jnp.float32, jnp.bfloat16
MESH = pl.DeviceIdType.MESH

HEAD = 64
RW_HEADS = 16
RW_W = 1024
SWA_W = 1024
KV_W = 128
DECAY_LORA, AAA_LORA, GATE_LORA = 64, 64, 160
LORA_W = DECAY_LORA + AAA_LORA + GATE_LORA
SHIFT_COLS = 3 * RW_W + LORA_W
XH = 4
XHD = 512
MEM_LEN = 256
WINDOW = 128
GN_EPS = 64e-5
RMS_EPS = 1e-6
NEG_INF = -1e30
ADAM_LR, ADAM_B1, ADAM_B2, ADAM_EPS, ADAM_WD, ADAM_STEP = 0.001, 0.9, 0.999, 1e-08, 0.01, 10

VMEM_LIMIT = 56 * 1024 * 1024


def _cp(sem=None, **kw):
    return pltpu.CompilerParams(dimension_semantics=sem, vmem_limit_bytes=VMEM_LIMIT, **kw)


def _pick(dim, target):
    if dim <= target:
        return dim
    best = None
    for t in range(128, target + 1, 128):
        if dim % t == 0:
            best = t
    assert best is not None, (dim, target)
    return best


_DIMS = {"nn": (((1,), (0,)), ((), ())), "nt": (((1,), (1,)), ((), ())), "tn": (((0,), (0,)), ((), ()))}


def _mm(a, b, mode, name, out_dtype=F32, alpha=1.0, res=None, bias=None, tm=1024, tn=1024, tk=2048, after=None):
    if mode == "nn":
        (M, K), (K2, N) = a.shape, b.shape
    elif mode == "nt":
        (M, K), (N, K2) = a.shape, b.shape
    else:
        (K, M), (K2, N) = a.shape, b.shape
    assert K == K2, (name, a.shape, b.shape)
    tm, tn, tk = _pick(M, tm), _pick(N, tn), _pick(K, tk)
    nk = K // tk
    a_spec = pl.BlockSpec((tk, tm), lambda i, j, k: (k, i)) if mode == "tn" else pl.BlockSpec((tm, tk), lambda i, j, k: (i, k))
    b_spec = pl.BlockSpec((tn, tk), lambda i, j, k: (j, k)) if mode == "nt" else pl.BlockSpec((tk, tn), lambda i, j, k: (k, j))
    o_spec = pl.BlockSpec((tm, tn), lambda i, j, k: (i, j))
    ins, specs = [a, b], [a_spec, b_spec]
    if res is not None:
        ins.append(res)
        specs.append(o_spec)
    if bias is not None:
        ins.append(bias)
        specs.append(pl.BlockSpec((1, tn), lambda i, j, k: (0, j)))
    if after is not None:
        ins.append(after)
        specs.append(pl.BlockSpec(memory_space=pl.ANY))
    dims = _DIMS[mode]

    def body(*refs):
        a_ref, b_ref = refs[0], refs[1]
        part = lax.dot_general(a_ref[...].astype(BF16), b_ref[...].astype(BF16), dims, preferred_element_type=F32)

        def finish(o, o_ref):
            if alpha != 1.0:
                o = o * alpha
            p = 2
            if res is not None:
                o = o + refs[p][...].astype(F32)
                p += 1
            if bias is not None:
                o = o + refs[p][...]
            o_ref[...] = o.astype(out_dtype)

        if nk == 1:
            finish(part, refs[-1])
            return
        o_ref, acc_ref = refs[-2], refs[-1]
        k = pl.program_id(2)

        @pl.when(k == 0)
        def _():
            acc_ref[...] = part

        @pl.when(k > 0)
        def _():
            acc_ref[...] += part

        @pl.when(k == nk - 1)
        def _():
            finish(acc_ref[...], o_ref)

    return pl.pallas_call(
        body, name=name, grid=(M // tm, N // tn, nk), in_specs=specs, out_specs=o_spec,
        out_shape=jax.ShapeDtypeStruct((M, N), out_dtype), scratch_shapes=[pltpu.VMEM((tm, tn), F32)] * (nk > 1),
        compiler_params=_cp(("parallel", "parallel", "arbitrary")),
    )(*ins)


def _rows(fn, name, T, tm, tiled, full, out_tiled, out_acc, extra=(), reverse=False, scratch=()):
    n = T // tm
    idx = (lambda i: n - 1 - i) if reverse else (lambda i: i)
    in_specs = [pl.BlockSpec((tm, a.shape[1]), lambda i: (idx(i), 0)) for a in tiled]
    in_specs += [mk(idx) for _, mk in extra]
    in_specs += [pl.BlockSpec(a.shape, lambda i, nd=a.ndim: (0,) * nd) for a in full]
    out_specs = [pl.BlockSpec((tm, c), lambda i: (idx(i), 0)) for c, _ in out_tiled]
    out_specs += [pl.BlockSpec(s, lambda i, nd=len(s): (0,) * nd) for s, _ in out_acc]
    out_shape = [jax.ShapeDtypeStruct((T, c), d) for c, d in out_tiled] + [jax.ShapeDtypeStruct(s, d) for s, d in out_acc]
    n_in = len(tiled) + len(extra) + len(full)
    n_t, n_a = len(out_tiled), len(out_acc)

    def body(*refs):
        step = pl.program_id(0)
        vals = [r[...] for r in refs[:n_in]]
        outs = fn(idx(step), *vals, *refs[n_in + n_t + n_a:])
        for r, v in zip(refs[n_in:n_in + n_t], outs[:n_t]):
            r[...] = v.astype(r.dtype)
        for r, v in zip(refs[n_in + n_t:n_in + n_t + n_a], outs[n_t:]):
            @pl.when(step == 0)
            def _(r=r):
                r[...] = jnp.zeros_like(r)

            r[...] += v

    return pl.pallas_call(
        body, name=name, grid=(n,), in_specs=in_specs, out_specs=out_specs, out_shape=out_shape,
        scratch_shapes=list(scratch), compiler_params=_cp(("arbitrary",)),
    )(*tiled, *[a for a, _ in extra], *full)


def _rms(x, g):
    return x * lax.rsqrt(jnp.mean(x * x, axis=-1, keepdims=True) + RMS_EPS) * g


def _rms_fwd(x, g, name, tm=256):
    (h,) = _rows(lambda i, x, g: (_rms(x, g),), name, x.shape[0], min(tm, x.shape[0]), [x], [g], [(x.shape[1], BF16)], [])
    return h


def _rms_bwd(x, g, dh, dres, name, tm=256):
    D = x.shape[1]

    def fn(i, x, dh, dres, g):
        _, vjp = jax.vjp(_rms, x, g)
        dx, dg = vjp(dh.astype(F32))
        dx = dx + dres
        return dx, dg, jnp.sum(dx, axis=0, keepdims=True)

    return _rows(fn, name, x.shape[0], tm, [x, dh, dres], [g], [(D, F32)], [((1, D), F32), ((1, D), F32)])


def _ffn_up(h, wg, wu, name, tm=1024, tn=512, after=None):
    (M, K), N = h.shape, wg.shape[1]
    tm, tn = _pick(M, tm), _pick(N, tn)

    def body(*refs):
        h_ref, wg_ref, wu_ref = refs[:3]
        g_ref, u_ref, a_ref = refs[-3:]
        hb = h_ref[...].astype(BF16)
        g = jnp.dot(hb, wg_ref[...].astype(BF16), preferred_element_type=F32)
        u = jnp.dot(hb, wu_ref[...].astype(BF16), preferred_element_type=F32)
        g_ref[...] = g
        u_ref[...] = u
        a_ref[...] = (g * jax.nn.sigmoid(g) * u).astype(BF16)

    o_spec = pl.BlockSpec((tm, tn), lambda i, j: (i, j))
    w_spec = pl.BlockSpec((K, tn), lambda i, j: (0, j))
    extra = [] if after is None else [after]
    return pl.pallas_call(
        body, name=name, grid=(M // tm, N // tn),
        in_specs=[pl.BlockSpec((tm, K), lambda i, j: (i, 0)), w_spec, w_spec] + [pl.BlockSpec(memory_space=pl.ANY)] * len(extra),
        out_specs=[o_spec] * 3, out_shape=[jax.ShapeDtypeStruct((M, N), F32)] * 2 + [jax.ShapeDtypeStruct((M, N), BF16)],
        compiler_params=_cp(("parallel", "parallel")),
    )(h, wg, wu, *extra)


def _ffn_dact(dxo, wd, g, u, name, tm=1024, tn=512):
    (M, K), N = dxo.shape, wd.shape[0]
    tm, tn = _pick(M, tm), _pick(N, tn)

    def body(dx_ref, wd_ref, g_ref, u_ref, dg_ref, du_ref):
        da = 0.5 * lax.dot_general(dx_ref[...].astype(BF16), wd_ref[...].astype(BF16), _DIMS["nt"], preferred_element_type=F32)
        g = g_ref[...]
        s = jax.nn.sigmoid(g)
        dg_ref[...] = (da * u_ref[...] * (s * (1.0 + g * (1.0 - s)))).astype(BF16)
        du_ref[...] = (da * (g * s)).astype(BF16)

    t_spec = pl.BlockSpec((tm, tn), lambda i, j: (i, j))
    return pl.pallas_call(
        body, name=name, grid=(M // tm, N // tn),
        in_specs=[pl.BlockSpec((tm, K), lambda i, j: (i, 0)), pl.BlockSpec((tn, K), lambda i, j: (j, 0)), t_spec, t_spec],
        out_specs=[t_spec, t_spec], out_shape=[jax.ShapeDtypeStruct((M, N), BF16)] * 2, compiler_params=_cp(("parallel", "parallel")),
    )(dxo, wd, g, u)


def _ffn_fwd(x, gain, wg, wu, wd, tag, after=None):
    h = _rms_fwd(x, gain, tag + "_norm")
    G, U, A = _ffn_up(h, wg, wu, tag + "_up", after=after)
    xo = _mm(A, wd(A) if callable(wd) else wd, "nn", tag + "_down", alpha=0.5, res=x)
    return xo, (h, G, U, A)


def _ffn_bwd(x, gain, wg, wu, wd, saved, dxo, tag, send):
    h, G, U, A = saved
    dwd = _mm(A, dxo, "tn", tag + "_dwd", out_dtype=BF16, alpha=0.5)
    sent = send(tag + "_down", {tag + "_down": dwd})
    dG, dU = _ffn_dact(dxo, wd, G, U, tag + "_dact")
    dwu = _mm(h, dU, "tn", tag + "_dwu", out_dtype=BF16, after=sent)
    sent = send(tag + "_up", {tag + "_up": dwu})
    dwg = _mm(h, dG, "tn", tag + "_dwg", out_dtype=BF16, after=sent)
    sent = send(tag + "_gate", {tag + "_gate": dwg})
    dh = _mm(dG, wg, "nt", tag + "_dh_g", after=sent)
    dh = _mm(dU, wu, "nt", tag + "_dh_u", res=dh)
    dx, dgain, _ = _rms_bwd(x, gain, dh, dxo, tag + "_norm_bwd")
    return dx, dgain


def _segsum64_impl(x):
    r = lax.broadcasted_iota(jnp.int32, (128, 128), 0) // HEAD
    c = lax.broadcasted_iota(jnp.int32, (128, 128), 1) // HEAD
    ones = (r == c).astype(BF16)
    hi = x.astype(BF16)
    lo = (x - hi.astype(F32)).astype(BF16)
    outs = []
    for q in range(x.shape[1] // 128):
        sl = slice(q * 128, (q + 1) * 128)
        outs.append(jnp.dot(hi[:, sl], ones, preferred_element_type=F32) + jnp.dot(lo[:, sl], ones, preferred_element_type=F32))
    return outs[0] if len(outs) == 1 else jnp.concatenate(outs, axis=1)


@jax.custom_vjp
def _segsum64(x):
    return _segsum64_impl(x)


_segsum64.defvjp(lambda x: (_segsum64_impl(x), None), lambda _, ct: (_segsum64_impl(ct),))


def _swap32(x):
    lane = lax.broadcasted_iota(jnp.int32, (x.shape[0], 128), 1)
    outs = [jnp.take_along_axis(x[:, q * 128:(q + 1) * 128], lane ^ 32, axis=1) for q in range(x.shape[1] // 128)]
    return outs[0] if len(outs) == 1 else jnp.concatenate(outs, axis=1)


def _tree_sum(xs):
    xs = list(xs)
    while len(xs) > 1:
        nxt = [xs[i] + xs[i + 1] for i in range(0, len(xs) - 1, 2)]
        if len(xs) % 2:
            nxt.append(xs[-1])
        xs = nxt
    return xs[0]


class _Acc:
    def __init__(self, ways=4):
        self.parts = [None] * ways

    def add(self, i, term):
        k = i % len(self.parts)
        self.parts[k] = term if self.parts[k] is None else self.parts[k] + term

    def total(self):
        return _tree_sum([p for p in self.parts if p is not None])


def _softplus(x):
    return jnp.maximum(x, 0.0) + jnp.log(1.0 + jnp.exp(-jnp.abs(x)))


def _pre_core(k, da, gd, w0, a0, k_k, k_a, w_da, gate_up):
    lane = lax.broadcasted_iota(jnp.int32, da.shape, 1)
    w_da = w_da.astype(BF16)
    l1 = jnp.dot(jnp.where(lane < DECAY_LORA, jnp.tanh(da), 0.0).astype(BF16), w_da, preferred_element_type=F32)
    l2 = jnp.dot(jnp.where(lane >= DECAY_LORA, da, 0.0).astype(BF16), w_da, preferred_element_type=F32)
    wlog = -_softplus(-(w0 + l1)) - 0.5
    decay = jnp.exp(-jnp.exp(wlog))
    a = jax.nn.sigmoid(a0 + l2)
    g = jnp.dot(jax.nn.sigmoid(gd).astype(BF16), gate_up.astype(BF16), preferred_element_type=F32)
    kk = k * k_k
    kkn = kk / jnp.maximum(jnp.sqrt(_segsum64(kk * kk)), 1e-12)
    k2 = k * (1.0 + (a - 1.0) * k_a)
    return decay, k2, -kkn, kkn * a, g


def _pre_shift(i, zr, zl, zr8, zl8, mu, mul):
    live = (i > 0).astype(F32)
    dz = _shift_down(zr, zr8[7:8, :] * live) - zr
    dzl = _shift_down(zl, zl8[7:8, :] * live) - zl
    return zr + dz * mu, zl + dzl * mul, dz, dzl


def _shift_down(x, first_row):
    rolled = pltpu.roll(x, 1, 0)
    row = lax.broadcasted_iota(jnp.int32, x.shape, 0)
    return jnp.where(row == 0, first_row, rolled)


def _shift_up(x, last_row):
    rolled = pltpu.roll(x, x.shape[0] - 1, 0)
    row = lax.broadcasted_iota(jnp.int32, x.shape, 0)
    return jnp.where(row == x.shape[0] - 1, last_row, rolled)


def _prev_rows_spec(tm, cols):
    return lambda idx: pl.BlockSpec((8, cols), lambda i: (jnp.maximum(idx(i) * (tm // 8) - 1, 0), 0))


def _rwkv_pre(p_rkv, p_lora, params, tm=256):
    T = p_rkv.shape[0]

    def fn(i, zr, zl, zr8, zl8, mu, mul, *ps):
        z, z2, _, _ = _pre_shift(i, zr, zl, zr8, zl8, mu, mul)
        decay, k2, an, bn, g = _pre_core(z[:, RW_W:2 * RW_W], z2[:, :128], z2[:, 128:], *ps)
        return z[:, :RW_W], decay, k2, z[:, 2 * RW_W:], an, bn, g

    extra = [(p_rkv, _prev_rows_spec(tm, 3 * RW_W)), (p_lora, _prev_rows_spec(tm, LORA_W))]
    return _rows(fn, "rwkv_pre", T, tm, [p_rkv, p_lora], list(params), [(RW_W, F32)] * 7, [], extra=extra)


def _rwkv_pre_bwd(p_rkv, p_lora, params, cts, tm=256):
    T = p_rkv.shape[0]
    n = T // tm

    def fn(i, zr, zl, cr, cdec, ck2, cv, can, cbn, cg, cr_b, ck2_b, cv_b, zr8, zl8, mu, mul, *rest):
        ps, (car, carl) = rest[:-2], rest[-2:]
        cr, ck2, cv = cr + cr_b, ck2 + ck2_b, cv + cv_b
        z, z2, dif, difl = _pre_shift(i, zr, zl, zr8, zl8, mu, mul)
        _, vjp = jax.vjp(_pre_core, z[:, RW_W:2 * RW_W], z2[:, :128], z2[:, 128:], *ps)
        dk, dda, dgd, *dps = vjp((cdec, ck2, can, cbn, cg))
        dz = jnp.concatenate([cr, dk, cv], axis=1)
        dz2 = jnp.concatenate([dda, dgd], axis=1)
        dzp, dzlp = dz * mu, dz2 * mul

        @pl.when(i == n - 1)
        def _():
            car[...] = jnp.zeros_like(car)
            carl[...] = jnp.zeros_like(carl)

        d_rkv = dz - dzp + _shift_up(dzp, car[0:1, :])
        d_lora = dz2 - dzlp + _shift_up(dzlp, carl[0:1, :])
        car[0:1, :] = dzp[0:1, :]
        carl[0:1, :] = dzlp[0:1, :]
        return (d_rkv, d_lora, jnp.sum(dz * dif, axis=0, keepdims=True), jnp.sum(dz2 * difl, axis=0, keepdims=True), *dps)

    extra = [(p_rkv, _prev_rows_spec(tm, 3 * RW_W)), (p_lora, _prev_rows_spec(tm, LORA_W))]
    acc = [(p.shape, F32) for p in params]
    return _rows(fn, "rwkv_pre_bwd", T, tm, [p_rkv, p_lora, *cts], list(params), [(3 * RW_W, BF16), (LORA_W, BF16)], acc,
                 extra=extra, reverse=True, scratch=[pltpu.VMEM((8, 3 * RW_W), F32), pltpu.VMEM((8, LORA_W), F32)])


def _post_core(y, r, k2, v, g, lw, lb, rk):
    mu = _segsum64(y) * (1.0 / HEAD)
    yc = y - mu
    var = _segsum64(yc * yc) * (1.0 / HEAD)
    yn = yc * lax.rsqrt(var + GN_EPS) * lw + lb
    return (yn + _segsum64(r * k2 * rk) * v) * g


def _rwkv_post(y, r, k2, v, g, lw, lb, rk, tm=256):
    (o,) = _rows(lambda i, *a: (_post_core(*a),), "rwkv_post", y.shape[0], tm, [y, r, k2, v, g], [lw, lb, rk], [(RW_W, BF16)], [])
    return o


def _rwkv_post_bwd(y, r, k2, v, g, lw, lb, rk, do, tm=256):
    def fn(i, y, r, k2, v, g, do, lw, lb, rk):
        _, vjp = jax.vjp(_post_core, y, r, k2, v, g, lw, lb, rk)
        return vjp(do.astype(F32))

    return _rows(fn, "rwkv_post_bwd", y.shape[0], tm, [y, r, k2, v, g, do], [lw, lb, rk], [(RW_W, F32)] * 5, [((1, RW_W), F32)] * 3)


SCAN_L = 32


def _to_perm(x):
    T = x.shape[0]
    return x.reshape(T, RW_HEADS, HEAD).transpose(0, 2, 1).reshape(T, 8, 128)


def _from_perm(x):
    T = x.shape[0]
    return x.reshape(T, HEAD, RW_HEADS).transpose(0, 2, 1).reshape(T, RW_W)


def _as_tile(p):
    lane = lax.broadcasted_iota(jnp.int32, (8, 128), 1)
    return jnp.take_along_axis(p, (lane % 8) * 16 + lane // 8, axis=1)


def _as_perm(t):
    lane = lax.broadcasted_iota(jnp.int32, (8, 128), 1)
    return jnp.take_along_axis(t, (lane % 16) * 8 + lane // 16, axis=1)


def _tiles_to_perm(refs, L):
    for r in refs:
        for t in range(L):
            r[t] = _as_perm(r[t])


def _expander(srcs, tiles=()):
    s = lax.broadcasted_iota(jnp.int32, (8, 128), 0)
    lane = lax.broadcasted_iota(jnp.int32, (8, 128), 1)
    idx = 16 * s + lane // 8

    def expand(t, e_ref):
        for m, r in enumerate(srcs):
            for g in range(8):
                row = jnp.broadcast_to(r[t, pl.ds(g, 1), :], (8, 128))
                e_ref[m, g * 8:(g + 1) * 8, :] = jnp.take_along_axis(row, idx, axis=1)
        for k, r in enumerate(tiles):
            e_ref[len(srcs) + k, 0:8, :] = _as_tile(r[t])

    return expand


def _ck_a_to_b(ck):
    n = ck.shape[0]
    return ck.reshape(n, 8, 8, 8, RW_HEADS, 8).transpose(0, 3, 5, 1, 4, 2).reshape(n, HEAD, 8, 128)


def _scan_fwd(xes, vi):
    T, L = vi.shape[0], SCAN_L
    nch = T // L

    def body(*refs):
        xr, (vi_ref, yi_ref, sa_ref, ck_ref, st_ref, e0, e1) = refs[:5], refs[5:]

        @pl.when(pl.program_id(0) == 0)
        def _():
            st_ref[...] = jnp.zeros_like(st_ref)

        ck_ref[0] = st_ref[...]
        expand = _expander(xr, [vi_ref])
        expand(0, e0)

        def step(t, e_ref):
            v = e_ref[5, 0:8, :]
            row = lambda m, j: jnp.broadcast_to(e_ref[m, pl.ds(j, 1), :], (8, 128))
            sa = _Acc()
            for j in range(HEAD):
                sa.add(j, st_ref[j] * row(0, j))
            sa = sa.total()
            sa_ref[t] = sa
            y = _Acc()
            for j in range(HEAD):
                s = st_ref[j] * row(1, j) + row(2, j) * sa + row(3, j) * v
                st_ref[j] = s
                y.add(j, s * row(4, j))
            yi_ref[t] = y.total()

        def pair(p, carry):
            t = 2 * p
            expand(t + 1, e1)
            step(t, e0)
            expand(jnp.minimum(t + 2, L - 1), e0)
            step(t + 1, e1)
            return carry

        lax.fori_loop(0, L // 2, pair, 0)
        _tiles_to_perm([yi_ref, sa_ref], L)

    tile = pl.BlockSpec((L, 8, 128), lambda c: (c, 0, 0))
    return pl.pallas_call(
        body, name="rwkv_scan_fwd", grid=(nch,), in_specs=[tile] * 6,
        out_specs=[tile, tile, pl.BlockSpec((1, HEAD, 8, 128), lambda c: (c, 0, 0, 0))],
        out_shape=[jax.ShapeDtypeStruct((T, 8, 128), F32)] * 2 + [jax.ShapeDtypeStruct((nch, HEAD, 8, 128), F32)],
        scratch_shapes=[pltpu.VMEM((HEAD, 8, 128), F32)] + [pltpu.VMEM((6, HEAD, 128), F32)] * 2, compiler_params=_cp(("arbitrary",)),
    )(*xes, vi)


def _scan_bwd_a(xes, dyi):
    T, L = dyi.shape[0], SCAN_L
    nch = T // L

    def body(*refs):
        xr, (dy_ref, dsa_ref, dv_ref, g_ref, e0, e1) = refs[:5], refs[5:]

        @pl.when(pl.program_id(0) == 0)
        def _():
            g_ref[...] = jnp.zeros_like(g_ref)

        expand = _expander(xr, [dy_ref])
        expand(L - 1, e0)

        def step(t, e_ref):
            dy = e_ref[5, 0:8, :]
            row = lambda m, j: jnp.broadcast_to(e_ref[m, pl.ds(j, 1), :], (8, 128))
            dsa, dv = _Acc(), _Acc()
            for j in range(HEAD):
                g = g_ref[j] + row(4, j) * dy
                g_ref[j] = g
                dsa.add(j, g * row(2, j))
                dv.add(j, g * row(3, j))
            dsa = dsa.total()
            dsa_ref[t] = dsa
            dv_ref[t] = dv.total()
            for j in range(HEAD):
                g_ref[j] = g_ref[j] * row(1, j) + row(0, j) * dsa

        def pair(p, carry):
            t = L - 1 - 2 * p
            expand(t - 1, e1)
            step(t, e0)
            expand(jnp.maximum(t - 2, 0), e0)
            step(t - 1, e1)
            return carry

        lax.fori_loop(0, L // 2, pair, 0)
        _tiles_to_perm([dsa_ref, dv_ref], L)

    tile = pl.BlockSpec((L, 8, 128), lambda c: (nch - 1 - c, 0, 0))
    return pl.pallas_call(
        body, name="rwkv_scan_bwd_a", grid=(nch,), in_specs=[tile] * 6, out_specs=[tile, tile],
        out_shape=[jax.ShapeDtypeStruct((T, 8, 128), F32)] * 2,
        scratch_shapes=[pltpu.VMEM((HEAD, 8, 128), F32)] + [pltpu.VMEM((6, HEAD, 128), F32)] * 2, compiler_params=_cp(("arbitrary",)),
    )(*xes, dyi)


def _scan_bwd_b(xts, ies, ckb):
    T, L = xts[0].shape[0], SCAN_L
    nch = T // L

    def body(*refs):
        xr, er, ck_ref, dj, (hist, g_ref, e0, e1) = refs[:5], refs[5:9], refs[9], refs[10:15], refs[15:]

        @pl.when(pl.program_id(0) == 0)
        def _():
            g_ref[...] = jnp.zeros_like(g_ref)

        hist[0] = ck_ref[0]
        expand_vs = _expander(er[:2], [xr[1], xr[2], xr[3]])
        expand = _expander(er, [xr[0], xr[1], xr[4]])
        expand_vs(0, e0)

        def fstep(t, e_ref):
            w, B, k = e_ref[2, 0:8, :], e_ref[3, 0:8, :], e_ref[4, 0:8, :]
            row = lambda m, i: jnp.broadcast_to(e_ref[m, pl.ds(i, 1), :], (8, 128))
            for i in range(HEAD):
                hist[t + 1, i] = hist[t, i] * w + row(1, i) * B + row(0, i) * k

        def fpair(p, carry):
            t = 2 * p
            expand_vs(t + 1, e1)
            fstep(t, e0)
            expand_vs(jnp.minimum(t + 2, L - 1), e0)
            fstep(t + 1, e1)
            return carry

        lax.fori_loop(0, L // 2, fpair, 0)
        expand(L - 1, e0)

        def bstep(t, e_ref):
            A, w, r = e_ref[4, 0:8, :], e_ref[5, 0:8, :], e_ref[6, 0:8, :]
            row = lambda m, i: jnp.broadcast_to(e_ref[m, pl.ds(i, 1), :], (8, 128))
            acc = [_Acc() for _ in range(5)]
            for i in range(HEAD):
                dy_i, dsa_i = row(2, i), row(3, i)
                g = g_ref[i] + dy_i * r
                sp = hist[t, i]
                acc[4].add(i, hist[t + 1, i] * dy_i)
                acc[1].add(i, g * sp)
                acc[2].add(i, g * row(1, i))
                acc[3].add(i, g * row(0, i))
                acc[0].add(i, sp * dsa_i)
                g_ref[i] = g * w + dsa_i * A
            for m in range(5):
                dj[m][t] = acc[m].total()

        def bpair(p, carry):
            t = L - 1 - 2 * p
            expand(t - 1, e1)
            bstep(t, e0)
            expand(jnp.maximum(t - 2, 0), e0)
            bstep(t - 1, e1)
            return carry

        lax.fori_loop(0, L // 2, bpair, 0)
        _tiles_to_perm(dj, L)

    tile = pl.BlockSpec((L, 8, 128), lambda c: (nch - 1 - c, 0, 0))
    return pl.pallas_call(
        body, name="rwkv_scan_bwd_b", grid=(nch,),
        in_specs=[tile] * 9 + [pl.BlockSpec((1, HEAD, 8, 128), lambda c: (nch - 1 - c, 0, 0, 0))],
        out_specs=[tile] * 5, out_shape=[jax.ShapeDtypeStruct((T, 8, 128), F32)] * 5,
        scratch_shapes=[pltpu.VMEM((L + 1, HEAD, 8, 128), F32), pltpu.VMEM((HEAD, 8, 128), F32)] + [pltpu.VMEM((7, HEAD, 128), F32)] * 2,
        compiler_params=_cp(("arbitrary",)),
    )(*xts, *ies, ckb)


SWA_COLS = SWA_W + 2 * KV_W
BLK = 128


def _swa_core(n, k2a, k2b, vla, vra, vlb, vrb, sinks, *qps):
    iq = lax.broadcasted_iota(jnp.int32, (BLK, 2 * BLK), 0)
    ik = lax.broadcasted_iota(jnp.int32, (BLK, 2 * BLK), 1)
    diff = BLK + iq - ik
    valid = (diff >= 0) & (diff < WINDOW) & ((n > 0) | (ik >= BLK))
    lane = lax.broadcasted_iota(jnp.int32, (BLK, 128), 1)
    lane1 = lax.broadcasted_iota(jnp.int32, (1, 128), 1)
    nt = (((1,), (1,)), ((), ()))
    outs = []
    for pp in range(8):
        k2, vl, vr = (k2a, vla, vra) if pp < 4 else (k2b, vlb, vrb)
        qp = qps[pp]
        o = None
        for half, vv in ((0, vl), (1, vr)):
            qh = jnp.where((lane >= HEAD) == (half == 1), qp, 0.0).astype(BF16)
            s = lax.dot_general(qh, k2.astype(BF16), nt, preferred_element_type=F32) * (HEAD ** -0.5)
            s = jnp.where(valid, s, NEG_INF)
            sink = jnp.sum(jnp.where(lane1 == 2 * pp + half, sinks, 0.0), axis=1, keepdims=True)
            m = jnp.maximum(jnp.max(s, axis=1, keepdims=True), sink)
            p = jnp.exp(s - m)
            den = jnp.sum(p, axis=1, keepdims=True) + jnp.exp(sink - m)
            oh = jnp.dot((p / den).astype(BF16), vv.astype(BF16), preferred_element_type=F32)
            o = oh if o is None else o + oh
        outs.append(o)
    return jnp.concatenate(outs, axis=1)


def _swa_prep(pc, pp, b, cq, sq, ckc, skc, ckp, skp):
    zc, zp = pc + b, pp + b
    qr = zc[:, :SWA_W] * cq + _swap32(zc[:, :SWA_W]) * sq
    kc, kp = zc[:, SWA_W:SWA_W + KV_W], zp[:, SWA_W:SWA_W + KV_W]
    kb = jnp.concatenate([kp * ckp + _swap32(kp) * skp, kc * ckc + _swap32(kc) * skc], axis=0)
    vb = jnp.concatenate([zp[:, SWA_W + KV_W:], zc[:, SWA_W + KV_W:]], axis=0)
    lane = lax.broadcasted_iota(jnp.int32, kb.shape, 1)
    left = lane < HEAD
    kbr, vbr = pltpu.roll(kb, HEAD, 1), pltpu.roll(vb, HEAD, 1)
    return (jnp.where(left, kb, kbr), jnp.where(left, kbr, kb), jnp.where(left, vb, 0.0), jnp.where(left, 0.0, vbr),
            jnp.where(left, vbr, 0.0), jnp.where(left, 0.0, vb)), [qr[:, q * 128:(q + 1) * 128] for q in range(8)]


def _swa_specs(T, tabs_q, tabs_k):
    cur = lambda c: pl.BlockSpec((BLK, c), lambda n: (n, 0))
    prev = lambda c: pl.BlockSpec((BLK, c), lambda n: (jnp.maximum(n - 1, 0), 0))
    return cur, prev


def _swa_fwd(p_swa, b, sinks, cq, sq, ck, sk):
    T = p_swa.shape[0]
    cur, prev = _swa_specs(T, None, None)

    def body(pc, pp, b_ref, s_ref, cq_r, sq_r, ckc, skc, ckp, skp, o_ref):
        ops, qps = _swa_prep(pc[...], pp[...], b_ref[...], cq_r[...], sq_r[...], ckc[...], skc[...], ckp[...], skp[...])
        o_ref[...] = _swa_core(pl.program_id(0), *ops, s_ref[...], *qps).astype(o_ref.dtype)

    full = lambda a: pl.BlockSpec(a.shape, lambda n: (0, 0))
    return pl.pallas_call(
        body, name="swa_fwd", grid=(T // BLK,),
        in_specs=[cur(SWA_COLS), prev(SWA_COLS), full(b), full(sinks), cur(SWA_W), cur(SWA_W), cur(KV_W), cur(KV_W), prev(KV_W), prev(KV_W)],
        out_specs=cur(SWA_W), out_shape=jax.ShapeDtypeStruct((T, SWA_W), BF16), compiler_params=_cp(("arbitrary",)),
    )(p_swa, p_swa, b, sinks, cq, sq, ck, sk, ck, sk)


def _swa_bwd(p_swa, b, sinks, cq, sq, ck, sk, do):
    T = p_swa.shape[0]
    nb = T // BLK
    cur = lambda c: pl.BlockSpec((BLK, c), lambda s: (nb - 1 - s, 0))
    prev = lambda c: pl.BlockSpec((BLK, c), lambda s: (jnp.maximum(nb - 2 - s, 0), 0))

    def body(pc, pp, b_ref, s_ref, cq_r, sq_r, ckc, skc, ckp, skp, do_ref, dcur, db, dsk, carry):
        step = pl.program_id(0)
        n = nb - 1 - step

        @pl.when(step == 0)
        def _():
            carry[...] = jnp.zeros_like(carry)
            db[...] = jnp.zeros_like(db)
            dsk[...] = jnp.zeros_like(dsk)

        ops, qps = _swa_prep(pc[...], pp[...], b_ref[...], cq_r[...], sq_r[...], ckc[...], skc[...], ckp[...], skp[...])
        _, vjp = jax.vjp(functools.partial(_swa_core, n), *ops, s_ref[...], *qps)
        dk2a, dk2b, dvla, dvra, dvlb, dvrb, dsinks, *dqps = vjp(do_ref[...].astype(F32))
        dqr = jnp.concatenate(dqps, axis=1)
        lane = lax.broadcasted_iota(jnp.int32, dk2a.shape, 1)
        left = lane < HEAD
        dkb = jnp.where(left, dk2a + pltpu.roll(dk2a, HEAD, 1), dk2b + pltpu.roll(dk2b, HEAD, 1))
        dvb = jnp.where(left, dvla + pltpu.roll(dvra, HEAD, 1), pltpu.roll(dvlb, HEAD, 1) + dvrb)
        dq = dqr * cq_r[...] + _swap32(dqr * sq_r[...])
        dkp, dkc = dkb[:BLK], dkb[BLK:]
        dkp = dkp * ckp[...] + _swap32(dkp * skp[...])
        dkc = dkc * ckc[...] + _swap32(dkc * skc[...])
        dc = jnp.concatenate([dq, jnp.concatenate([dkc, dvb[BLK:]], axis=1) + carry[...]], axis=1)
        carry[...] = jnp.concatenate([dkp, dvb[:BLK]], axis=1)
        dcur[...] = dc.astype(dcur.dtype)
        db[...] += jnp.sum(dc, axis=0, keepdims=True)
        dsk[...] += dsinks

    full = lambda a: pl.BlockSpec(a.shape, lambda s: (0, 0))
    return pl.pallas_call(
        body, name="swa_bwd", grid=(nb,),
        in_specs=[cur(SWA_COLS), prev(SWA_COLS), full(b), full(sinks), cur(SWA_W), cur(SWA_W), cur(KV_W), cur(KV_W), prev(KV_W), prev(KV_W),
                  cur(SWA_W)],
        out_specs=[cur(SWA_COLS), full(b), full(sinks)],
        out_shape=[jax.ShapeDtypeStruct((T, SWA_COLS), BF16), jax.ShapeDtypeStruct(b.shape, F32), jax.ShapeDtypeStruct(sinks.shape, F32)],
        scratch_shapes=[pltpu.VMEM((BLK, 2 * KV_W), F32)], compiler_params=_cp(("arbitrary",)),
    )(p_swa, p_swa, b, sinks, cq, sq, ck, sk, ck, sk, do)


def _rope_tables(T):
    inv = 10000.0 ** (-jnp.arange(0, HEAD, 2, dtype=F32) / HEAD)
    ang = jnp.arange(T, dtype=F32)[:, None] * inv[None, :]
    c = jnp.concatenate([jnp.cos(ang), jnp.cos(ang)], axis=1)
    s = jnp.concatenate([-jnp.sin(ang), jnp.sin(ang)], axis=1)
    return jnp.tile(c, (1, 16)), jnp.tile(s, (1, 16)), jnp.tile(c, (1, 2)), jnp.tile(s, (1, 2))


def _xattn_core(*qkv):
    outs = []
    for h in range(XH):
        qh, kh, vh = qkv[h], qkv[XH + h], qkv[2 * XH + h]
        s = lax.dot_general(qh.astype(BF16), kh.astype(BF16), (((1,), (1,)), ((), ())), preferred_element_type=F32) * (XHD ** -0.5)
        p = jnp.exp(s - jnp.max(s, axis=1, keepdims=True))
        p = p / jnp.sum(p, axis=1, keepdims=True)
        outs.append(jnp.dot(p.astype(BF16), vh.astype(BF16), preferred_element_type=F32))
    return jnp.concatenate(outs, axis=1)


def _xattn_split(q, kv):
    return [q[:, h * XHD:(h + 1) * XHD] for h in range(XH)] + [kv[:, h * XHD:(h + 1) * XHD] for h in range(2 * XH)]


def _xattn_fwd(q, kv, tm=256):
    (o,) = _rows(lambda i, q, kv: (_xattn_core(*_xattn_split(q, kv)),), "xattn_fwd", q.shape[0], tm, [q], [kv], [(q.shape[1], BF16)], [])
    return o


def _xattn_bwd(q, kv, do, tm=256):
    def fn(i, q, do, kv):
        _, vjp = jax.vjp(_xattn_core, *_xattn_split(q, kv))
        d = vjp(do.astype(F32))
        return jnp.concatenate(d[:XH], axis=1), jnp.concatenate(d[XH:], axis=1)

    return _rows(fn, "xattn_bwd", q.shape[0], tm, [q, do], [kv], [(q.shape[1], BF16)], [(kv.shape, F32)])


def _loss_head(x, g, tgt, tm=256):
    D = x.shape[1]

    def fn(i, x, tgt, g):
        y, vjp = jax.vjp(_rms, x, g)
        err = y - tgt
        dx, dg = vjp(err * (1.0 / D))
        part = 0.5 / D * jnp.sum(jnp.sum(err * err, axis=1, keepdims=True), axis=0, keepdims=True)
        return dx, jnp.broadcast_to(part, (1, 128)), dg

    return _rows(fn, "loss_head", x.shape[0], tm, [x, tgt], [g], [(D, F32)], [((1, 128), F32), ((1, D), F32)])


def _local_step(x, mem, tgt, get_w, P, put_g):
    T = x.shape[0]
    W = dict(get_w("f1", None))

    def f1_down(after):
        W.update(get_w("f1d", after))
        return W["f1_down"]

    x1, s1 = _ffn_fwd(x, P["f1_norm"], W["f1_gate"], W["f1_up"], f1_down, "f1")

    W.update(get_w("mix", x1))
    h2 = _rms_fwd(x1, P["mix_norm"], "mix_norm")
    w_rkv, w_lora, w_swa = W["w_inT"][:3 * RW_W], W["w_inT"][3 * RW_W:SHIFT_COLS], W["w_inT"][SHIFT_COLS:]
    p_rkv = _mm(h2, w_rkv, "nt", "in_rkv", after=W.get("_after"))
    p_lora = _mm(h2, w_lora, "nt", "in_lora")
    p_swa = _mm(h2, w_swa, "nt", "in_swa")
    w_da = jnp.concatenate([W["rw_decay_up"], W["rw_aaa_up"]], axis=0)
    pre_params = (P["rw_mu"][:, :3 * RW_W], P["rw_mu"][:, 3 * RW_W:], P["rw_w0"], P["rw_a0"], P["rw_k_k"], P["rw_k_a"], w_da,
                  W["rw_gate_up"])
    r, decay, k2, v, an, bn, g = _rwkv_pre(p_rkv, p_lora, pre_params)
    scan_vecs = (an, decay, bn, k2, r)
    xes = [_to_perm(a) for a in scan_vecs]
    v_p = _to_perm(v)
    yi, sai, ck = _scan_fwd(xes, v_p)
    y_scan = _from_perm(yi)
    y_rw = _rwkv_post(y_scan, r, k2, v, g, P["rw_lnx_w"], P["rw_lnx_b"], P["rw_r_k"])
    cq, sq, ckt, skt = _rope_tables(T)
    y_swa = _swa_fwd(p_swa, P["b_in_attn"], P["attn_sinks"], cq, sq, ckt, skt)
    ycat = jnp.concatenate([y_rw, y_swa], axis=1)
    W.update(get_w("out", ycat))
    x2 = _mm(ycat, W["w_out"], "nn", "out_proj", res=x1, bias=P["b_out"])

    W.update(get_w("xattn", x2))
    hx = _rms_fwd(x2, P["xa_norm"], "xa_norm")
    mn = _rms_fwd(mem, P["mem_norm"], "mem_norm")
    q = _mm(hx, W["w_xq"], "nn", "xq", out_dtype=BF16)
    kv = _mm(mn, W["w_xkv"], "nn", "xkv", out_dtype=BF16)
    o = _xattn_fwd(q, kv)
    x3 = _mm(o, W["w_xo"], "nn", "xo", res=x2)

    W.update(get_w("f2", x3))
    x4, s2 = _ffn_fwd(x3, P["f2_norm"], W["f2_gate"], W["f2_up"], W["f2_down"], "f2")
    dx4, loss_part, d_final = _loss_head(x4, P["final_norm"], tgt)

    gs = {"final_norm": d_final}
    dx3, gs["f2_norm"] = _ffn_bwd(x3, P["f2_norm"], W["f2_gate"], W["f2_up"], W["f2_down"], s2, dx4, "f2", put_g)

    do = _mm(dx3, W["w_xo"], "nt", "xo_do", out_dtype=BF16)
    dw_xo = _mm(o, dx3, "tn", "xo_dw", out_dtype=BF16)
    dq, dkv = _xattn_bwd(q, kv, do)
    dw_xq = _mm(hx, dq, "tn", "xq_dw", out_dtype=BF16)
    dw_xkv = _mm(mn, dkv, "tn", "xkv_dw", out_dtype=BF16)
    sent = put_g("xattn", {"w_xq": dw_xq, "w_xkv": dw_xkv, "w_xo": dw_xo})
    dhx = _mm(dq, W["w_xq"], "nt", "xq_dh", after=sent)
    dmn = _mm(dkv, W["w_xkv"], "nt", "xkv_dmn")
    _, gs["mem_norm"], _ = _rms_bwd(mem, P["mem_norm"], dmn, jnp.zeros_like(mem), "mem_norm_bwd")
    dx2, gs["xa_norm"], gs["b_out"] = _rms_bwd(x2, P["xa_norm"], dhx, dx3, "xa_norm_bwd")

    dycat = _mm(dx2, W["w_out"], "nt", "out_dy")
    dw_out = _mm(ycat, dx2, "tn", "out_dw", out_dtype=BF16)
    dp_swa, gs["b_in_attn"], gs["attn_sinks"] = _swa_bwd(p_swa, P["b_in_attn"], P["attn_sinks"], cq, sq, ckt, skt, dycat[:, RW_W:])
    dy_scan, dr_b, dk2_b, dv_b, dg, gs["rw_lnx_w"], gs["rw_lnx_b"], gs["rw_r_k"] = _rwkv_post_bwd(
        y_scan, r, k2, v, g, P["rw_lnx_w"], P["rw_lnx_b"], P["rw_r_k"], dycat[:, :RW_W])
    dy_p = _to_perm(dy_scan)
    dsai, dvi = _scan_bwd_a(xes, dy_p)
    dj = _scan_bwd_b(xes, [v_p, sai, dy_p, dsai], _ck_a_to_b(ck))
    dan, ddecay, dbn, dk2_s, dr_s = (_from_perm(d) for d in dj)
    cts = (dr_s, ddecay, dk2_s, _from_perm(dvi), dan, dbn, dg, dr_b, dk2_b, dv_b)
    dp_rkv, dp_lora, dmu, dmul, gs["rw_w0"], gs["rw_a0"], gs["rw_k_k"], gs["rw_k_a"], dw_da, gs["rw_gate_up"] = _rwkv_pre_bwd(
        p_rkv, p_lora, pre_params, cts)
    gs["rw_mu"] = jnp.concatenate([dmu, dmul], axis=1)
    gs["rw_decay_up"], gs["rw_aaa_up"] = dw_da[:DECAY_LORA], dw_da[DECAY_LORA:]
    dw_inT = jnp.concatenate([_mm(dp_rkv, h2, "tn", "in_dw_rkv"), _mm(dp_lora, h2, "tn", "in_dw_lora"),
                              _mm(dp_swa, h2, "tn", "in_dw_swa")], axis=0)
    sent = put_g("mix", {"w_in": dw_inT, "w_out": dw_out})
    dh2 = _mm(dp_rkv, w_rkv, "nn", "in_dh_rkv", after=sent)
    dh2 = _mm(dp_lora, w_lora, "nn", "in_dh_lora", res=dh2)
    dh2 = _mm(dp_swa, w_swa, "nn", "in_dh_swa", res=dh2)
    dx1, gs["mix_norm"], _ = _rms_bwd(x1, P["mix_norm"], dh2, dx2, "mix_norm_bwd")

    dx0, gs["f1_norm"] = _ffn_bwd(x, P["f1_norm"], W["f1_gate"], W["f1_up"], W["f1_down"], s1, dx1, "f1", put_g)
    return loss_part, dx0, gs


_ANY = pl.BlockSpec(memory_space=pl.ANY)
_OTHER_CHIPS = ((1, 0), (0, 1), (1, 1))


def _mesh_pos():
    return lax.axis_index("x"), lax.axis_index("y"), lax.axis_index("c")


def _slot(ref, kind, s, rows, cols):
    if kind == "row":
        return ref.at[pl.ds(pl.multiple_of(s * rows, 8), rows), :]
    return ref.at[:, pl.ds(pl.multiple_of(s * cols, 128), cols)]


_HBM = pl.BlockSpec(memory_space=pltpu.HBM)
_SEMS = pl.BlockSpec(memory_space=pltpu.SEMAPHORE)
_SPLIT = dict(compiler_params=pltpu.CompilerParams(has_side_effects=pltpu.SideEffectType.DATAFLOW_SIDE_EFFECTING))


def _in_hbm(a):
    return pltpu.with_memory_space_constraint(a, pltpu.HBM)


def _full_shape(s, kind):
    return (4 * s.shape[0], s.shape[1]) if kind == "row" else (s.shape[0], 4 * s.shape[1])


def _half(ref, shape, h):
    rows, cols = shape
    if rows % 32 == 0:
        return ref.at[pl.ds(pl.multiple_of(h * (rows // 2), 16), rows // 2), :]
    assert cols % 256 == 0, shape
    return ref.at[:, pl.ds(pl.multiple_of(h * (cols // 2), 128), cols // 2)]


def _half_shape(shape):
    rows, cols = shape
    return (rows // 2, cols) if rows % 32 == 0 else (rows, cols // 2)


def _streams(src, dst, shape, c):
    hs = _half_shape(shape)
    s, d = _half(src, shape, c), _half(dst, shape, c)
    return [(_half(s, hs, q), _half(d, hs, q)) for q in range(2)]


def _swap_halves(name, fulls, shard_shapes, kinds):
    n = len(fulls)

    def body(*refs):
        out, send, recv = refs[n:2 * n], refs[2 * n], refs[2 * n + 1]
        x, y, c = _mesh_pos()
        sent = []
        for i in range(n):
            for r, (dx, dy) in enumerate(_OTHER_CHIPS):
                theirs = _slot(out[i], kinds[i], 2 * ((x + dx) % 2) + (y + dy) % 2, *shard_shapes[i])
                have = _half(theirs, shard_shapes[i], c)
                rc = pltpu.make_async_remote_copy(have, have, send.at[3 * i + r], recv.at[3 * i + r], device_id=(x, y, 1 - c),
                                                  device_id_type=MESH)
                rc.start()
                sent.append(rc)
        for i in range(n):
            for r, (dx, dy) in enumerate(_OTHER_CHIPS):
                theirs = _slot(out[i], kinds[i], 2 * ((x + dx) % 2) + (y + dy) % 2, *shard_shapes[i])
                need = _half(theirs, shard_shapes[i], 1 - c)
                pltpu.make_async_remote_copy(need, need, send.at[3 * i + r], recv.at[3 * i + r], device_id=(x, y, c),
                                             device_id_type=MESH).wait_recv()
        for rc in sent:
            rc.wait_send()

    return pl.pallas_call(
        body, name=name, in_specs=[_ANY] * n, out_specs=[_ANY] * n, out_shape=[jax.ShapeDtypeStruct(f.shape, f.dtype) for f in fulls],
        input_output_aliases={i: i for i in range(n)},
        scratch_shapes=[pltpu.SemaphoreType.DMA((3 * n,)), pltpu.SemaphoreType.DMA((3 * n,))],
    )(*fulls)


def _gather_start(name, shards, kinds, groups, after=None):
    n, ng = len(shards), len(groups)
    lands = [_in_hbm(lax.empty(_full_shape(s, k), s.dtype)) for s, k in zip(shards, kinds)]
    n_in = 2 * n + (after is not None)

    def body(*refs):
        src, land, sems, token = refs[:n], refs[n:2 * n], refs[n_in:n_in + 3 * ng], refs[-1]
        x, y, c = _mesh_pos()
        me = 2 * x + y
        for gi, idxs in enumerate(groups):
            send, recv, own = sems[3 * gi:3 * gi + 3]
            for k, i in enumerate(idxs):
                mine = _slot(land[i], kinds[i], me, *src[i].shape)
                for r, (dx, dy) in enumerate(_OTHER_CHIPS):
                    for q, (s, d) in enumerate(_streams(src[i], mine, src[i].shape, c)):
                        pltpu.make_async_remote_copy(s, d, send.at[6 * k + 2 * r + q], recv.at[6 * k + 2 * r + q],
                                                     device_id=((x + dx) % 2, (y + dy) % 2, c), device_id_type=MESH).start()
                pltpu.make_async_copy(src[i], mine, own.at[k]).start()
        token[...] = jnp.zeros_like(token)

    sem_shapes = [pltpu.SemaphoreType.DMA((w * len(g),)) for g in groups for w in (6, 6, 1)]
    thru = [pltpu.HBM(a.shape, a.dtype) for a in (*shards, *lands)]
    res = pl.pallas_call(
        body, name=name, in_specs=[_HBM] * (2 * n) + [_ANY] * (after is not None),
        out_specs=[_SEMS] * (3 * ng) + [_HBM] * (2 * n) + [pl.BlockSpec(memory_space=pltpu.VMEM)],
        out_shape=sem_shapes + thru + [jax.ShapeDtypeStruct((8, 128), F32)],
        input_output_aliases={i: 3 * ng + i for i in range(2 * n)}, **_SPLIT,
    )(*[_in_hbm(s) for s in shards], *lands, *([] if after is None else [after]))
    return res[:3 * ng], res[3 * ng:3 * ng + n], res[3 * ng + n:3 * ng + 2 * n], res[-1]


def _gather_wait(name, sems, shards, lands, kinds, after):
    m = len(shards)

    def body(*refs):
        src, land, (send, recv, own) = refs[:m], refs[m:2 * m], refs[2 * m:2 * m + 3]
        x, y, c = _mesh_pos()
        me = 2 * x + y
        for k in range(m):
            mine = _slot(land[k], kinds[k], me, *src[k].shape)
            for r in range(3):
                for q, (s, d) in enumerate(_streams(src[k], mine, src[k].shape, c)):
                    cp = pltpu.make_async_remote_copy(s, d, send.at[6 * k + 2 * r + q], recv.at[6 * k + 2 * r + q], device_id=(x, y, c),
                                                      device_id_type=MESH)
                    cp.wait_send()
                    cp.wait_recv()
            pltpu.make_async_copy(src[k], mine, own.at[k]).wait()

    thru = [pltpu.HBM(a.shape, a.dtype) for a in (*shards, *lands)]
    res = pl.pallas_call(
        body, name=name, in_specs=[_HBM] * (2 * m) + [_SEMS] * 3 + [pl.BlockSpec(memory_space=pl.ANY)],
        out_specs=[_HBM] * (2 * m), out_shape=thru, input_output_aliases={i: i for i in range(2 * m)}, **_SPLIT,
    )(*shards, *lands, *sems, after)
    return res[m:]


def _scatter_start(name, grads, kinds):
    m = len(grads)
    shard_shape = [(g.shape[0] // 4, g.shape[1]) if k == "row" else (g.shape[0], g.shape[1] // 4) for g, k in zip(grads, kinds)]
    lands = [_in_hbm(lax.empty((4, *s), g.dtype)) for s, g in zip(shard_shape, grads)]

    def body(*refs):
        src, land, (send, recv, own) = refs[:m], refs[m:2 * m], refs[2 * m:2 * m + 3]
        x, y, c = _mesh_pos()
        me = 2 * x + y
        for k in range(m):
            for r, (dx, dy) in enumerate(_OTHER_CHIPS):
                tx, ty = (x + dx) % 2, (y + dy) % 2
                pltpu.make_async_remote_copy(_slot(src[k], kinds[k], 2 * tx + ty, *shard_shape[k]), land[k].at[me],
                                             send.at[3 * k + r], recv.at[3 * k + r], device_id=(tx, ty, c), device_id_type=MESH).start()
            pltpu.make_async_copy(_slot(src[k], kinds[k], me, *shard_shape[k]), land[k].at[me], own.at[k]).start()
        refs[-1][...] = jnp.zeros_like(refs[-1])

    thru = [pltpu.HBM(a.shape, a.dtype) for a in (*grads, *lands)]
    res = pl.pallas_call(
        body, name=name, in_specs=[_HBM] * (2 * m),
        out_specs=[_SEMS] * 3 + [_HBM] * (2 * m) + [pl.BlockSpec(memory_space=pltpu.VMEM)],
        out_shape=[pltpu.SemaphoreType.DMA((3 * m,))] * 2 + [pltpu.SemaphoreType.DMA((m,))] + thru + [jax.ShapeDtypeStruct((8, 128), F32)],
        input_output_aliases={i: 3 + i for i in range(2 * m)}, **_SPLIT,
    )(*[_in_hbm(g) for g in grads], *lands)
    return res[:3], res[3:3 + m], res[3 + m:3 + 2 * m], res[-1]


def _scatter_wait(name, sems, grads, lands, kinds, after):
    m = len(grads)

    def body(*refs):
        src, land, (send, recv, own) = refs[:m], refs[m:2 * m], refs[2 * m:2 * m + 3]
        x, y, c = _mesh_pos()
        me = 2 * x + y
        for k in range(m):
            mine = _slot(src[k], kinds[k], me, *land[k].shape[1:])
            for r in range(3):
                cp = pltpu.make_async_remote_copy(mine, land[k].at[me], send.at[3 * k + r], recv.at[3 * k + r],
                                                  device_id=(x, y, c), device_id_type=MESH)
                cp.wait_send()
                cp.wait_recv()
            pltpu.make_async_copy(mine, land[k].at[me], own.at[k]).wait()

    thru = [pltpu.HBM(a.shape, a.dtype) for a in (*grads, *lands)]
    res = pl.pallas_call(
        body, name=name, in_specs=[_HBM] * (2 * m) + [_SEMS] * 3 + [pl.BlockSpec(memory_space=pl.ANY)],
        out_specs=[_HBM] * (2 * m), out_shape=thru, input_output_aliases={i: i for i in range(2 * m)}, **_SPLIT,
    )(*grads, *lands, *sems, after)
    return res[m:]


def _swap_with_sibling(arrs, name):
    n = len(arrs)

    def body(*refs):
        ins, outs = refs[:n], refs[n:2 * n]
        send, recv = refs[2 * n:]
        x, y, c = _mesh_pos()
        copies = []
        for i in range(n):
            rc = pltpu.make_async_remote_copy(ins[i], outs[i], send.at[i], recv.at[i], device_id=(x, y, 1 - c), device_id_type=MESH)
            rc.start()
            copies.append(rc)
        for rc in copies:
            rc.wait()

    return pl.pallas_call(
        body, name=name, in_specs=[_ANY] * n, out_specs=[_ANY] * n,
        out_shape=[jax.ShapeDtypeStruct(a.shape, a.dtype) for a in arrs],
        scratch_shapes=[pltpu.SemaphoreType.DMA((n,)), pltpu.SemaphoreType.DMA((n,))],
    )(*arrs)


def _small_start(pack, after):
    land = _in_hbm(lax.empty((8, *pack.shape), pack.dtype))

    def body(in_ref, land_ref, after_ref, send, recv, own, in_thru, land_thru, token):
        x, y, c = _mesh_pos()
        me = 4 * x + 2 * y + c
        for r in range(1, 8):
            dx, dy, dc = r // 4, (r // 2) % 2, r % 2
            pltpu.make_async_remote_copy(in_ref, land_ref.at[me], send.at[r - 1], recv.at[r - 1],
                                         device_id=((x + dx) % 2, (y + dy) % 2, (c + dc) % 2), device_id_type=MESH).start()
        pltpu.make_async_copy(in_ref, land_ref.at[me], own.at[0]).start()
        token[...] = jnp.zeros_like(token)

    res = pl.pallas_call(
        body, name="small_start", in_specs=[_HBM, _HBM, _ANY],
        out_specs=[_SEMS] * 3 + [_HBM, _HBM, pl.BlockSpec(memory_space=pltpu.VMEM)],
        out_shape=[pltpu.SemaphoreType.DMA((7,)), pltpu.SemaphoreType.DMA((7,)), pltpu.SemaphoreType.DMA((1,)),
                   pltpu.HBM(pack.shape, pack.dtype), pltpu.HBM(land.shape, land.dtype), jax.ShapeDtypeStruct((8, 128), F32)],
        input_output_aliases={0: 3, 1: 4}, **_SPLIT,
    )(_in_hbm(pack), land, after)
    return res[:3], res[3], res[4], res[5]


def _small_wait(sems, pack, land, after):
    def body(in_ref, land_ref, send, recv, own, after_ref, in_dead, got):
        x, y, c = _mesh_pos()
        me = 4 * x + 2 * y + c
        for r in range(1, 8):
            cp = pltpu.make_async_remote_copy(in_ref, land_ref.at[me], send.at[r - 1], recv.at[r - 1], device_id=(x, y, c),
                                              device_id_type=MESH)
            cp.wait_send()
            cp.wait_recv()
        pltpu.make_async_copy(in_ref, land_ref.at[me], own.at[0]).wait()

    res = pl.pallas_call(
        body, name="small_wait", in_specs=[_HBM, _HBM] + [_SEMS] * 3 + [_ANY], out_specs=[_HBM, _HBM],
        out_shape=[pltpu.HBM(pack.shape, pack.dtype), pltpu.HBM(land.shape, land.dtype)], input_output_aliases={0: 0, 1: 1}, **_SPLIT,
    )(pack, land, *sems, after)
    return res[1]


def _row_tile(R, dtype, target=256):
    mult = 8 * 4 // jnp.dtype(dtype).itemsize
    best = R
    for t in range(mult, min(R, target) + 1, mult):
        if R % t == 0:
            best = t
    return best


def _sum_slots(stack, name, out_dtype=F32):
    k, R, C = stack.shape
    tr = _row_tile(R, stack.dtype)

    def body(s_ref, o_ref):
        acc = s_ref[0].astype(F32)
        for j in range(1, k):
            acc = acc + s_ref[j].astype(F32)
        o_ref[...] = acc.astype(out_dtype)

    return pl.pallas_call(
        body, name=name, grid=(R // tr,), in_specs=[pl.BlockSpec((k, tr, C), lambda i: (0, i, 0))],
        out_specs=pl.BlockSpec((tr, C), lambda i: (i, 0)), out_shape=jax.ShapeDtypeStruct((R, C), out_dtype),
        compiler_params=_cp(("parallel",)),
    )(stack)


def _adamw(w, m, v, ga, gb, name, after=None):
    R, C = w.shape
    tr = _row_tile(R, F32, 128)
    gs = [ga] if gb is None else [ga, gb]
    extra = [] if after is None else [after]

    def body(*refs):
        w_ref, m_ref, v_ref = refs[:3]
        g = refs[3][...].astype(F32)
        if gb is not None:
            g = g + refs[4][...].astype(F32)
        g_ref, d_ref, nm_ref, nv_ref = refs[-4:]
        nm = ADAM_B1 * m_ref[...] + (1.0 - ADAM_B1) * g
        nv = ADAM_B2 * v_ref[...] + (1.0 - ADAM_B2) * (g * g)
        m_hat = nm / (1.0 - ADAM_B1 ** ADAM_STEP)
        v_hat = nv / (1.0 - ADAM_B2 ** ADAM_STEP)
        g_ref[...] = g
        d_ref[...] = -ADAM_LR * (m_hat / (jnp.sqrt(v_hat) + ADAM_EPS) + ADAM_WD * w_ref[...])
        nm_ref[...] = nm
        nv_ref[...] = nv

    spec = pl.BlockSpec((tr, C), lambda i: (i, 0))
    return pl.pallas_call(
        body, name=name, grid=(R // tr,), in_specs=[spec] * (3 + len(gs)) + [_ANY] * len(extra), out_specs=[spec] * 4,
        out_shape=[jax.ShapeDtypeStruct((R, C), F32)] * 4, compiler_params=_cp(("parallel",)),
    )(w, m, v, *gs, *extra)


def _pack(arrs):
    rows = []
    for a in arrs:
        flat = a.reshape(-1)
        rows.append(jnp.pad(flat, (0, -flat.shape[0] % 1024)).reshape(-1, 1024))
    p = jnp.concatenate(rows, axis=0)
    return jnp.pad(p, ((0, -p.shape[0] % 8), (0, 0)))


def _unpack(p, shapes):
    out, r = [], 0
    for s in shapes:
        n = 1
        for d in s:
            n *= d
        nr = -(-n // 1024)
        out.append(p[r:r + nr].reshape(-1)[:n].reshape(s))
        r += nr
    return out


BIG = ("f1_gate", "f1_up", "f1_down", "w_in", "w_out", "w_xq", "w_xkv", "w_xo", "f2_gate", "f2_up", "f2_down")
BIG_KIND = {"f1_gate": "col", "f1_up": "col", "f1_down": "row", "w_in": "row", "w_out": "row", "w_xq": "row", "w_xkv": "col",
            "w_xo": "row", "f2_gate": "col", "f2_up": "col", "f2_down": "row"}
LORA = ("rw_decay_up", "rw_aaa_up", "rw_gate_up")
WEIGHTS = ("f1_norm", "f1_gate", "f1_up", "f1_down", "mix_norm", "w_in", "b_in_attn", "rw_mu", "rw_w0", "rw_decay_up", "rw_a0",
           "rw_aaa_up", "rw_gate_up", "rw_k_k", "rw_k_a", "rw_r_k", "rw_lnx_w", "rw_lnx_b", "attn_sinks", "w_out", "b_out", "xa_norm",
           "mem_norm", "w_xq", "w_xkv", "w_xo", "f2_norm", "f2_gate", "f2_up", "f2_down", "final_norm")
SMALL = tuple(n for n in WEIGHTS if n not in BIG)
GROUP_ORDER = ("f1", "f1d", "mix", "out", "xattn", "f2")
GROUPS = {"f1": ("f1_gate", "f1_up"), "f1d": ("f1_down",), "mix": ("w_in",) + LORA, "out": ("w_out",), "xattn": ("w_xq", "w_xkv", "w_xo"),
          "f2": ("f2_gate", "f2_up", "f2_down")}


def kernel(x, mem, f1_norm, f1_gate, f1_up, f1_down, mix_norm, w_in, b_in_attn, rw_mu, rw_w0, rw_decay_up, rw_a0, rw_aaa_up, rw_gate_up, rw_k_k, rw_k_a, rw_r_k, rw_lnx_w, rw_lnx_b, attn_sinks, w_out, b_out, xa_norm, mem_norm, w_xq, w_xkv, w_xo, f2_norm, f2_gate, f2_up, f2_down, final_norm, loss_target, m_f1_norm, m_f1_gate, m_f1_up, m_f1_down, m_mix_norm, m_w_in, m_b_in_attn, m_rw_mu, m_rw_w0, m_rw_decay_up, m_rw_a0, m_rw_aaa_up, m_rw_gate_up, m_rw_k_k, m_rw_k_a, m_rw_r_k, m_rw_lnx_w, m_rw_lnx_b, m_attn_sinks, m_w_out, m_b_out, m_xa_norm, m_mem_norm, m_w_xq, m_w_xkv, m_w_xo, m_f2_norm, m_f2_gate, m_f2_up, m_f2_down, m_final_norm, v_f1_norm, v_f1_gate, v_f1_up, v_f1_down, v_mix_norm, v_w_in, v_b_in_attn, v_rw_mu, v_rw_w0, v_rw_decay_up, v_rw_a0, v_rw_aaa_up, v_rw_gate_up, v_rw_k_k, v_rw_k_a, v_rw_r_k, v_rw_lnx_w, v_rw_lnx_b, v_attn_sinks, v_w_out, v_b_out, v_xa_norm, v_mem_norm, v_w_xq, v_w_xkv, v_w_xo, v_f2_norm, v_f2_gate, v_f2_up, v_f2_down, v_final_norm):
    a = dict(locals())
    w = {n: a[n] for n in WEIGHTS}
    m = {n: a["m_" + n] for n in WEIGHTS}
    v = {n: a["v_" + n] for n in WEIGHTS}
    sq = lambda t: t.reshape(t.shape[-2:]) if t.ndim == 3 else t.reshape(1, -1)

    local_name = lambda n: "w_inT" if n == "w_in" else n
    kind_of = lambda n: BIG_KIND.get(n, "col")
    payload = lambda n: sq(w[n]).T if n == "w_in" else sq(w[n]) if n in LORA else sq(w[n]).astype(BF16)
    gathers = {}

    def start_gather(name, grps, after):
        shards = [payload(n) for g in grps for n in GROUPS[g]]
        kinds = [kind_of(n) for g in grps for n in GROUPS[g]]
        groups, at = [], 0
        for g in grps:
            groups.append(list(range(at, at + len(GROUPS[g]))))
            at += len(GROUPS[g])
        sems, src_thru, land_thru, token = _gather_start(name, shards, kinds, groups, after)
        for gi, g in enumerate(grps):
            gathers[g] = (sems[3 * gi:3 * gi + 3], [src_thru[i] for i in groups[gi]], [land_thru[i] for i in groups[gi]],
                          [kinds[i] for i in groups[gi]], token)

    early = GROUP_ORDER[:3]
    start_gather("gather_start", early, None)

    def get_w(grp, after):
        g_sems, g_src, g_land, g_kinds, token = gathers[grp]
        got = _gather_wait("gather_wait_" + grp, g_sems, g_src, g_land, g_kinds, token if after is None else after)
        got = _swap_halves("gather_swap_" + grp, got, [s.shape for s in g_src], g_kinds)
        out = {local_name(n): f for n, f in zip(GROUPS[grp], got)}
        if grp == early[-1]:
            start_gather("gather_start_late", GROUP_ORDER[3:], got[0])
            out["_after"] = gathers[GROUP_ORDER[3]][4]
        return out

    in_flight = []

    def put_g(label, gw):
        names = list(gw)
        *flight, sent = _scatter_start("scatter_start_" + label, [gw[n] for n in names], [kind_of(n) for n in names])
        in_flight.append((label, names, flight))
        return sent

    P = {n: sq(w[n]) for n in SMALL if n not in LORA}
    P["attn_sinks"] = jnp.pad(P["attn_sinks"], ((0, 0), (0, 128 - P["attn_sinks"].shape[1])))
    P["rw_r_k"] = w["rw_r_k"].reshape(1, RW_W)
    loss_part, grad_x, gs = _local_step(x[0], mem[0], loss_target[0], get_w, P, put_g)

    gs["attn_sinks"] = gs["attn_sinks"][:, :16]
    small_flight = _small_start(_pack([gs[n] for n in SMALL] + [loss_part]), grad_x)

    out, after = {}, small_flight[-1]
    for label, names, (g_sems, g_thru, l_thru) in in_flight:
        stacks = _scatter_wait("scatter_wait_" + label, g_sems, g_thru, l_thru, [kind_of(n) for n in names], after)
        partial = [_sum_slots(s, "sum_chips_" + n, F32 if n == "w_in" else BF16) for s, n in zip(stacks, names)]
        sibling = _swap_with_sibling(partial, "swap_" + label)
        chain = None
        for n, pa, sb in zip(names, partial, sibling):
            if n == "w_in":
                pa, sb = pa.T, sb.T
            out[n] = _adamw(sq(w[n]), sq(m[n]), sq(v[n]), pa, sb, "adamw_" + n, after=chain)
            chain = out[n][1]
        after = chain

    gsum = _sum_slots(_small_wait(*small_flight[:-1], after), "sum_small")
    *summed, loss_row = _unpack(gsum, [gs[n].shape for n in SMALL] + [loss_part.shape])
    g_small = dict(zip(SMALL, summed))
    loss = loss_row[0, 0]
    shard = 2 * lax.axis_index("x") + lax.axis_index("y")
    for n in LORA:
        cols = w[n].shape[-1]
        g_small[n] = lax.dynamic_slice_in_dim(g_small[n], shard * cols, cols, axis=1)

    flat = lambda d: _pack([d[n] for n in SMALL])
    res = _adamw(flat(w), flat(m), flat(v), _pack([g_small[n] for n in SMALL]), None, "adamw_small")
    shapes = [w[n].shape for n in SMALL]
    for k, p in enumerate(res):
        for n, t in zip(SMALL, _unpack(p, shapes)):
            out.setdefault(n, [None] * 4)[k] = t
    outs = [loss, grad_x.reshape(x.shape)]
    for k in range(4):
        outs += [out[n][k].reshape(w[n].shape) for n in WEIGHTS]
    return tuple(outs)
```

```python
import functools

import jax
import jax.numpy as jnp
from jax import lax
from jax.experimental import pallas as pl
from jax.experimental.pallas import tpu as pltpu

F32, BF16 = jnp.float32, jnp.bfloat16
MESH = pl.DeviceIdType.MESH

HEAD = 64
RW_HEADS = 16
RW_W = 1024
SWA_W = 1024
KV_W = 128
DECAY_LORA, AAA_LORA, GATE_LORA = 64, 64, 160
LORA_W = DECAY_LORA + AAA_LORA + GATE_LORA
SHIFT_COLS = 3 * RW_W + LORA_W
XH = 4
XHD = 512
MEM_LEN = 256
WINDOW = 128
GN_EPS = 64e-5
RMS_EPS = 1e-6
NEG_INF = -1e30
ADAM_LR, ADAM_B1, ADAM_B2, ADAM_EPS, ADAM_WD, ADAM_STEP = 0.001, 0.9, 0.999, 1e-08, 0.01, 10

VMEM_LIMIT = 56 * 1024 * 1024


def _cp(sem=None, **kw):
    return pltpu.CompilerParams(dimension_semantics=sem, vmem_limit_bytes=VMEM_LIMIT, **kw)


def _pick(dim, target):
    if dim <= target:
        return dim
    best = None
    for t in range(128, target + 1, 128):
        if dim % t == 0:
            best = t
    assert best is not None, (dim, target)
    return best


_DIMS = {"nn": (((1,), (0,)), ((), ())), "nt": (((1,), (1,)), ((), ())), "tn": (((0,), (0,)), ((), ()))}


def _mm(a, b, mode, name, out_dtype=F32, alpha=1.0, res=None, bias=None, tm=1024, tn=1024, tk=2048, after=None):
    if mode == "nn":
        (M, K), (K2, N) = a.shape, b.shape
    elif mode == "nt":
        (M, K), (N, K2) = a.shape, b.shape
    else:
        (K, M), (K2, N) = a.shape, b.shape
    assert K == K2, (name, a.shape, b.shape)
    tm, tn, tk = _pick(M, tm), _pick(N, tn), _pick(K, tk)
    nk = K // tk
    a_spec = pl.BlockSpec((tk, tm), lambda i, j, k: (k, i)) if mode == "tn" else pl.BlockSpec((tm, tk), lambda i, j, k: (i, k))
    b_spec = pl.BlockSpec((tn, tk), lambda i, j, k: (j, k)) if mode == "nt" else pl.BlockSpec((tk, tn), lambda i, j, k: (k, j))
    o_spec = pl.BlockSpec((tm, tn), lambda i, j, k: (i, j))
    ins, specs = [a, b], [a_spec, b_spec]
    if res is not None:
        ins.append(res)
        specs.append(o_spec)
    if bias is not None:
        ins.append(bias)
        specs.append(pl.BlockSpec((1, tn), lambda i, j, k: (0, j)))
    if after is not None:
        ins.append(after)
        specs.append(pl.BlockSpec(memory_space=pl.ANY))
    dims = _DIMS[mode]

    def body(*refs):
        a_ref, b_ref = refs[0], refs[1]
        part = lax.dot_general(a_ref[...].astype(BF16), b_ref[...].astype(BF16), dims, preferred_element_type=F32)

        def finish(o, o_ref):
            if alpha != 1.0:
                o = o * alpha
            p = 2
            if res is not None:
                o = o + refs[p][...].astype(F32)
                p += 1
            if bias is not None:
                o = o + refs[p][...]
            o_ref[...] = o.astype(out_dtype)

        if nk == 1:
            finish(part, refs[-1])
            return
        o_ref, acc_ref = refs[-2], refs[-1]
        k = pl.program_id(2)

        @pl.when(k == 0)
        def _():
            acc_ref[...] = part

        @pl.when(k > 0)
        def _():
            acc_ref[...] += part

        @pl.when(k == nk - 1)
        def _():
            finish(acc_ref[...], o_ref)

    return pl.pallas_call(
        body, name=name, grid=(M // tm, N // tn, nk), in_specs=specs, out_specs=o_spec,
        out_shape=jax.ShapeDtypeStruct((M, N), out_dtype), scratch_shapes=[pltpu.VMEM((tm, tn), F32)] * (nk > 1),
        compiler_params=_cp(("parallel", "parallel", "arbitrary")),
    )(*ins)


def _rows(fn, name, T, tm, tiled, full, out_tiled, out_acc, extra=(), reverse=False, scratch=()):
    n = T // tm
    idx = (lambda i: n - 1 - i) if reverse else (lambda i: i)
    in_specs = [pl.BlockSpec((tm, a.shape[1]), lambda i: (idx(i), 0)) for a in tiled]
    in_specs += [mk(idx) for _, mk in extra]
    in_specs += [pl.BlockSpec(a.shape, lambda i, nd=a.ndim: (0,) * nd) for a in full]
    out_specs = [pl.BlockSpec((tm, c), lambda i: (idx(i), 0)) for c, _ in out_tiled]
    out_specs += [pl.BlockSpec(s, lambda i, nd=len(s): (0,) * nd) for s, _ in out_acc]
    out_shape = [jax.ShapeDtypeStruct((T, c), d) for c, d in out_tiled] + [jax.ShapeDtypeStruct(s, d) for s, d in out_acc]
    n_in = len(tiled) + len(extra) + len(full)
    n_t, n_a = len(out_tiled), len(out_acc)

    def body(*refs):
        step = pl.program_id(0)
        vals = [r[...] for r in refs[:n_in]]
        outs = fn(idx(step), *vals, *refs[n_in + n_t + n_a:])
        for r, v in zip(refs[n_in:n_in + n_t], outs[:n_t]):
            r[...] = v.astype(r.dtype)
        for r, v in zip(refs[n_in + n_t:n_in + n_t + n_a], outs[n_t:]):
            @pl.when(step == 0)
            def _(r=r):
                r[...] = jnp.zeros_like(r)

            r[...] += v

    return pl.pallas_call(
        body, name=name, grid=(n,), in_specs=in_specs, out_specs=out_specs, out_shape=out_shape,
        scratch_shapes=list(scratch), compiler_params=_cp(("arbitrary",)),
    )(*tiled, *[a for a, _ in extra], *full)


def _rms(x, g):
    return x * lax.rsqrt(jnp.mean(x * x, axis=-1, keepdims=True) + RMS_EPS) * g


def _rms_fwd(x, g, name, tm=256):
    (h,) = _rows(lambda i, x, g: (_rms(x, g),), name, x.shape[0], min(tm, x.shape[0]), [x], [g], [(x.shape[1], BF16)], [])
    return h


def _rms_bwd(x, g, dh, dres, name, tm=256):
    D = x.shape[1]

    def fn(i, x, dh, dres, g):
        _, vjp = jax.vjp(_rms, x, g)
        dx, dg = vjp(dh.astype(F32))
        dx = dx + dres
        return dx, dg, jnp.sum(dx, axis=0, keepdims=True)

    return _rows(fn, name, x.shape[0], tm, [x, dh, dres], [g], [(D, F32)], [((1, D), F32), ((1, D), F32)])


def _ffn_up(h, wg, wu, name, tm=1024, tn=512, after=None):
    (M, K), N = h.shape, wg.shape[1]
    tm, tn = _pick(M, tm), _pick(N, tn)

    def body(*refs):
        h_ref, wg_ref, wu_ref = refs[:3]
        g_ref, u_ref, a_ref = refs[-3:]
        hb = h_ref[...].astype(BF16)
        g = jnp.dot(hb, wg_ref[...].astype(BF16), preferred_element_type=F32)
        u = jnp.dot(hb, wu_ref[...].astype(BF16), preferred_element_type=F32)
        g_ref[...] = g
        u_ref[...] = u
        a_ref[...] = (g * jax.nn.sigmoid(g) * u).astype(BF16)

    o_spec = pl.BlockSpec((tm, tn), lambda i, j: (i, j))
    w_spec = pl.BlockSpec((K, tn), lambda i, j: (0, j))
    extra = [] if after is None else [after]
    return pl.pallas_call(
        body, name=name, grid=(M // tm, N // tn),
        in_specs=[pl.BlockSpec((tm, K), lambda i, j: (i, 0)), w_spec, w_spec] + [pl.BlockSpec(memory_space=pl.ANY)] * len(extra),
        out_specs=[o_spec] * 3, out_shape=[jax.ShapeDtypeStruct((M, N), F32)] * 2 + [jax.ShapeDtypeStruct((M, N), BF16)],
        compiler_params=_cp(("parallel", "parallel")),
    )(h, wg, wu, *extra)


def _ffn_dact(dxo, wd, g, u, name, tm=1024, tn=512):
    (M, K), N = dxo.shape, wd.shape[0]
    tm, tn = _pick(M, tm), _pick(N, tn)

    def body(dx_ref, wd_ref, g_ref, u_ref, dg_ref, du_ref):
        da = 0.5 * lax.dot_general(dx_ref[...].astype(BF16), wd_ref[...].astype(BF16), _DIMS["nt"], preferred_element_type=F32)
        g = g_ref[...]
        s = jax.nn.sigmoid(g)
        dg_ref[...] = (da * u_ref[...] * (s * (1.0 + g * (1.0 - s)))).astype(BF16)
        du_ref[...] = (da * (g * s)).astype(BF16)

    t_spec = pl.BlockSpec((tm, tn), lambda i, j: (i, j))
    return pl.pallas_call(
        body, name=name, grid=(M // tm, N // tn),
        in_specs=[pl.BlockSpec((tm, K), lambda i, j: (i, 0)), pl.BlockSpec((tn, K), lambda i, j: (j, 0)), t_spec, t_spec],
        out_specs=[t_spec, t_spec], out_shape=[jax.ShapeDtypeStruct((M, N), BF16)] * 2, compiler_params=_cp(("parallel", "parallel")),
    )(dxo, wd, g, u)


def _ffn_fwd(x, gain, wg, wu, wd, tag, after=None):
    h = _rms_fwd(x, gain, tag + "_norm")
    G, U, A = _ffn_up(h, wg, wu, tag + "_up", after=after)
    xo = _mm(A, wd(A) if callable(wd) else wd, "nn", tag + "_down", alpha=0.5, res=x)
    return xo, (h, G, U, A)


def _ffn_bwd(x, gain, wg, wu, wd, saved, dxo, tag, send):
    h, G, U, A = saved
    dwd = _mm(A, dxo, "tn", tag + "_dwd", out_dtype=BF16, alpha=0.5, tm=1408)
    sent = send(tag + "_down", {tag + "_down": dwd})
    dG, dU = _ffn_dact(dxo, wd, G, U, tag + "_dact")
    dwu = _mm(h, dU, "tn", tag + "_dwu", out_dtype=BF16, after=sent)
    sent = send(tag + "_up", {tag + "_up": dwu})
    dwg = _mm(h, dG, "tn", tag + "_dwg", out_dtype=BF16, after=sent)
    sent = send(tag + "_gate", {tag + "_gate": dwg})
    dh = _mm(dG, wg, "nt", tag + "_dh_g", after=sent)
    dh = _mm(dU, wu, "nt", tag + "_dh_u", res=dh)
    dx, dgain, _ = _rms_bwd(x, gain, dh, dxo, tag + "_norm_bwd")
    return dx, dgain


def _segsum64_impl(x):
    r = lax.broadcasted_iota(jnp.int32, (128, 128), 0) // HEAD
    c = lax.broadcasted_iota(jnp.int32, (128, 128), 1) // HEAD
    ones = (r == c).astype(BF16)
    hi = x.astype(BF16)
    lo = (x - hi.astype(F32)).astype(BF16)
    outs = []
    for q in range(x.shape[1] // 128):
        sl = slice(q * 128, (q + 1) * 128)
        outs.append(jnp.dot(hi[:, sl], ones, preferred_element_type=F32) + jnp.dot(lo[:, sl], ones, preferred_element_type=F32))
    return outs[0] if len(outs) == 1 else jnp.concatenate(outs, axis=1)


@jax.custom_vjp
def _segsum64(x):
    return _segsum64_impl(x)


_segsum64.defvjp(lambda x: (_segsum64_impl(x), None), lambda _, ct: (_segsum64_impl(ct),))


def _swap32(x):
    lane = lax.broadcasted_iota(jnp.int32, (x.shape[0], 128), 1)
    outs = [jnp.take_along_axis(x[:, q * 128:(q + 1) * 128], lane ^ 32, axis=1) for q in range(x.shape[1] // 128)]
    return outs[0] if len(outs) == 1 else jnp.concatenate(outs, axis=1)


def _tree_sum(xs):
    xs = list(xs)
    while len(xs) > 1:
        nxt = [xs[i] + xs[i + 1] for i in range(0, len(xs) - 1, 2)]
        if len(xs) % 2:
            nxt.append(xs[-1])
        xs = nxt
    return xs[0]


class _Acc:
    def __init__(self, ways=4):
        self.parts = [None] * ways

    def add(self, i, term):
        k = i % len(self.parts)
        self.parts[k] = term if self.parts[k] is None else self.parts[k] + term

    def total(self):
        return _tree_sum([p for p in self.parts if p is not None])


def _softplus(x):
    return jnp.maximum(x, 0.0) + jnp.log(1.0 + jnp.exp(-jnp.abs(x)))


def _pre_core(k, da, gd, w0, a0, k_k, k_a, w_da, gate_up):
    lane = lax.broadcasted_iota(jnp.int32, da.shape, 1)
    w_da = w_da.astype(BF16)
    l1 = jnp.dot(jnp.where(lane < DECAY_LORA, jnp.tanh(da), 0.0).astype(BF16), w_da, preferred_element_type=F32)
    l2 = jnp.dot(jnp.where(lane >= DECAY_LORA, da, 0.0).astype(BF16), w_da, preferred_element_type=F32)
    wlog = -_softplus(-(w0 + l1)) - 0.5
    decay = jnp.exp(-jnp.exp(wlog))
    a = jax.nn.sigmoid(a0 + l2)
    g = jnp.dot(jax.nn.sigmoid(gd).astype(BF16), gate_up.astype(BF16), preferred_element_type=F32)
    kk = k * k_k
    kkn = kk / jnp.maximum(jnp.sqrt(_segsum64(kk * kk)), 1e-12)
    k2 = k * (1.0 + (a - 1.0) * k_a)
    return decay, k2, -kkn, kkn * a, g


def _pre_shift(i, zr, zl, zr8, zl8, mu, mul):
    live = (i > 0).astype(F32)
    dz = _shift_down(zr, zr8[7:8, :] * live) - zr
    dzl = _shift_down(zl, zl8[7:8, :] * live) - zl
    return zr + dz * mu, zl + dzl * mul, dz, dzl


def _shift_down(x, first_row):
    rolled = pltpu.roll(x, 1, 0)
    row = lax.broadcasted_iota(jnp.int32, x.shape, 0)
    return jnp.where(row == 0, first_row, rolled)


def _shift_up(x, last_row):
    rolled = pltpu.roll(x, x.shape[0] - 1, 0)
    row = lax.broadcasted_iota(jnp.int32, x.shape, 0)
    return jnp.where(row == x.shape[0] - 1, last_row, rolled)


def _prev_rows_spec(tm, cols):
    return lambda idx: pl.BlockSpec((8, cols), lambda i: (jnp.maximum(idx(i) * (tm // 8) - 1, 0), 0))


def _rwkv_pre(p_rkv, p_lora, params, tm=256):
    T = p_rkv.shape[0]

    def fn(i, zr, zl, zr8, zl8, mu, mul, *ps):
        z, z2, _, _ = _pre_shift(i, zr, zl, zr8, zl8, mu, mul)
        decay, k2, an, bn, g = _pre_core(z[:, RW_W:2 * RW_W], z2[:, :128], z2[:, 128:], *ps)
        return z[:, :RW_W], decay, k2, z[:, 2 * RW_W:], an, bn, g

    extra = [(p_rkv, _prev_rows_spec(tm, 3 * RW_W)), (p_lora, _prev_rows_spec(tm, LORA_W))]
    return _rows(fn, "rwkv_pre", T, tm, [p_rkv, p_lora], list(params), [(RW_W, F32)] * 7, [], extra=extra)


def _rwkv_pre_bwd(p_rkv, p_lora, params, cts, tm=256):
    T = p_rkv.shape[0]
    n = T // tm

    def fn(i, zr, zl, cr, cdec, ck2, cv, can, cbn, cg, cr_b, ck2_b, cv_b, zr8, zl8, mu, mul, *rest):
        ps, (car, carl) = rest[:-2], rest[-2:]
        cr, ck2, cv = cr + cr_b, ck2 + ck2_b, cv + cv_b
        z, z2, dif, difl = _pre_shift(i, zr, zl, zr8, zl8, mu, mul)
        _, vjp = jax.vjp(_pre_core, z[:, RW_W:2 * RW_W], z2[:, :128], z2[:, 128:], *ps)
        dk, dda, dgd, *dps = vjp((cdec, ck2, can, cbn, cg))
        dz = jnp.concatenate([cr, dk, cv], axis=1)
        dz2 = jnp.concatenate([dda, dgd], axis=1)
        dzp, dzlp = dz * mu, dz2 * mul

        @pl.when(i == n - 1)
        def _():
            car[...] = jnp.zeros_like(car)
            carl[...] = jnp.zeros_like(carl)

        d_rkv = dz - dzp + _shift_up(dzp, car[0:1, :])
        d_lora = dz2 - dzlp + _shift_up(dzlp, carl[0:1, :])
        car[0:1, :] = dzp[0:1, :]
        carl[0:1, :] = dzlp[0:1, :]
        return (d_rkv, d_lora, jnp.sum(dz * dif, axis=0, keepdims=True), jnp.sum(dz2 * difl, axis=0, keepdims=True), *dps)

    extra = [(p_rkv, _prev_rows_spec(tm, 3 * RW_W)), (p_lora, _prev_rows_spec(tm, LORA_W))]
    acc = [(p.shape, F32) for p in params]
    return _rows(fn, "rwkv_pre_bwd", T, tm, [p_rkv, p_lora, *cts], list(params), [(3 * RW_W, BF16), (LORA_W, BF16)], acc,
                 extra=extra, reverse=True, scratch=[pltpu.VMEM((8, 3 * RW_W), F32), pltpu.VMEM((8, LORA_W), F32)])


def _post_core(y, r, k2, v, g, lw, lb, rk):
    mu = _segsum64(y) * (1.0 / HEAD)
    yc = y - mu
    var = _segsum64(yc * yc) * (1.0 / HEAD)
    yn = yc * lax.rsqrt(var + GN_EPS) * lw + lb
    return (yn + _segsum64(r * k2 * rk) * v) * g


def _rwkv_post(y, r, k2, v, g, lw, lb, rk, tm=256):
    (o,) = _rows(lambda i, *a: (_post_core(*a),), "rwkv_post", y.shape[0], tm, [y, r, k2, v, g], [lw, lb, rk], [(RW_W, BF16)], [])
    return o


def _rwkv_post_bwd(y, r, k2, v, g, lw, lb, rk, do, tm=256):
    def fn(i, y, r, k2, v, g, do, lw, lb, rk):
        _, vjp = jax.vjp(_post_core, y, r, k2, v, g, lw, lb, rk)
        return vjp(do.astype(F32))

    return _rows(fn, "rwkv_post_bwd", y.shape[0], tm, [y, r, k2, v, g, do], [lw, lb, rk], [(RW_W, F32)] * 5, [((1, RW_W), F32)] * 3)


SCAN_L = 32


def _to_perm(x):
    T = x.shape[0]
    return x.reshape(T, RW_HEADS, HEAD).transpose(0, 2, 1).reshape(T, 8, 128)


def _from_perm(x):
    T = x.shape[0]
    return x.reshape(T, HEAD, RW_HEADS).transpose(0, 2, 1).reshape(T, RW_W)


def _as_tile(p):
    lane = lax.broadcasted_iota(jnp.int32, (8, 128), 1)
    return jnp.take_along_axis(p, (lane % 8) * 16 + lane // 8, axis=1)


def _as_perm(t):
    lane = lax.broadcasted_iota(jnp.int32, (8, 128), 1)
    return jnp.take_along_axis(t, (lane % 16) * 8 + lane // 16, axis=1)


def _tiles_to_perm(refs, L):
    for r in refs:
        for t in range(L):
            r[t] = _as_perm(r[t])


def _expander(srcs, tiles=()):
    s = lax.broadcasted_iota(jnp.int32, (8, 128), 0)
    lane = lax.broadcasted_iota(jnp.int32, (8, 128), 1)
    idx = 16 * s + lane // 8

    def expand(t, e_ref):
        for m, r in enumerate(srcs):
            for g in range(8):
                row = jnp.broadcast_to(r[t, pl.ds(g, 1), :], (8, 128))
                e_ref[m, g * 8:(g + 1) * 8, :] = jnp.take_along_axis(row, idx, axis=1)
        for k, r in enumerate(tiles):
            e_ref[len(srcs) + k, 0:8, :] = _as_tile(r[t])

    return expand


def _ck_a_to_b(ck):
    n = ck.shape[0]
    return ck.reshape(n, 8, 8, 8, RW_HEADS, 8).transpose(0, 3, 5, 1, 4, 2).reshape(n, HEAD, 8, 128)


def _scan_fwd(xes, vi):
    T, L = vi.shape[0], SCAN_L
    nch = T // L

    def body(*refs):
        xr, (vi_ref, yi_ref, sa_ref, ck_ref, st_ref, e0, e1) = refs[:5], refs[5:]

        @pl.when(pl.program_id(0) == 0)
        def _():
            st_ref[...] = jnp.zeros_like(st_ref)

        ck_ref[0] = st_ref[...]
        expand = _expander(xr, [vi_ref])
        expand(0, e0)

        def step(t, e_ref):
            v = e_ref[5, 0:8, :]
            row = lambda m, j: jnp.broadcast_to(e_ref[m, pl.ds(j, 1), :], (8, 128))
            sa = _Acc()
            for j in range(HEAD):
                sa.add(j, st_ref[j] * row(0, j))
            sa = sa.total()
            sa_ref[t] = sa
            y = _Acc()
            for j in range(HEAD):
                s = st_ref[j] * row(1, j) + row(2, j) * sa + row(3, j) * v
                st_ref[j] = s
                y.add(j, s * row(4, j))
            yi_ref[t] = y.total()

        def pair(p, carry):
            t = 2 * p
            expand(t + 1, e1)
            step(t, e0)
            expand(jnp.minimum(t + 2, L - 1), e0)
            step(t + 1, e1)
            return carry

        lax.fori_loop(0, L // 2, pair, 0)
        _tiles_to_perm([yi_ref, sa_ref], L)

    tile = pl.BlockSpec((L, 8, 128), lambda c: (c, 0, 0))
    return pl.pallas_call(
        body, name="rwkv_scan_fwd", grid=(nch,), in_specs=[tile] * 6,
        out_specs=[tile, tile, pl.BlockSpec((1, HEAD, 8, 128), lambda c: (c, 0, 0, 0))],
        out_shape=[jax.ShapeDtypeStruct((T, 8, 128), F32)] * 2 + [jax.ShapeDtypeStruct((nch, HEAD, 8, 128), F32)],
        scratch_shapes=[pltpu.VMEM((HEAD, 8, 128), F32)] + [pltpu.VMEM((6, HEAD, 128), F32)] * 2, compiler_params=_cp(("arbitrary",)),
    )(*xes, vi)


def _scan_bwd_a(xes, dyi):
    T, L = dyi.shape[0], SCAN_L
    nch = T // L

    def body(*refs):
        xr, (dy_ref, dsa_ref, dv_ref, g_ref, e0, e1) = refs[:5], refs[5:]

        @pl.when(pl.program_id(0) == 0)
        def _():
            g_ref[...] = jnp.zeros_like(g_ref)

        expand = _expander(xr, [dy_ref])
        expand(L - 1, e0)

        def step(t, e_ref):
            dy = e_ref[5, 0:8, :]
            row = lambda m, j: jnp.broadcast_to(e_ref[m, pl.ds(j, 1), :], (8, 128))
            dsa, dv = _Acc(), _Acc()
            for j in range(HEAD):
                g = g_ref[j] + row(4, j) * dy
                g_ref[j] = g
                dsa.add(j, g * row(2, j))
                dv.add(j, g * row(3, j))
            dsa = dsa.total()
            dsa_ref[t] = dsa
            dv_ref[t] = dv.total()
            for j in range(HEAD):
                g_ref[j] = g_ref[j] * row(1, j) + row(0, j) * dsa

        def pair(p, carry):
            t = L - 1 - 2 * p
            expand(t - 1, e1)
            step(t, e0)
            expand(jnp.maximum(t - 2, 0), e0)
            step(t - 1, e1)
            return carry

        lax.fori_loop(0, L // 2, pair, 0)
        _tiles_to_perm([dsa_ref, dv_ref], L)

    tile = pl.BlockSpec((L, 8, 128), lambda c: (nch - 1 - c, 0, 0))
    return pl.pallas_call(
        body, name="rwkv_scan_bwd_a", grid=(nch,), in_specs=[tile] * 6, out_specs=[tile, tile],
        out_shape=[jax.ShapeDtypeStruct((T, 8, 128), F32)] * 2,
        scratch_shapes=[pltpu.VMEM((HEAD, 8, 128), F32)] + [pltpu.VMEM((6, HEAD, 128), F32)] * 2, compiler_params=_cp(("arbitrary",)),
    )(*xes, dyi)


def _scan_bwd_b(xts, ies, ckb):
    T, L = xts[0].shape[0], SCAN_L
    nch = T // L

    def body(*refs):
        xr, er, ck_ref, dj, (hist, g_ref, e0, e1) = refs[:5], refs[5:9], refs[9], refs[10:15], refs[15:]

        @pl.when(pl.program_id(0) == 0)
        def _():
            g_ref[...] = jnp.zeros_like(g_ref)

        hist[0] = ck_ref[0]
        expand_vs = _expander(er[:2], [xr[1], xr[2], xr[3]])
        expand = _expander(er, [xr[0], xr[1], xr[4]])
        expand_vs(0, e0)

        def fstep(t, e_ref):
            w, B, k = e_ref[2, 0:8, :], e_ref[3, 0:8, :], e_ref[4, 0:8, :]
            row = lambda m, i: jnp.broadcast_to(e_ref[m, pl.ds(i, 1), :], (8, 128))
            for i in range(HEAD):
                hist[t + 1, i] = hist[t, i] * w + row(1, i) * B + row(0, i) * k

        def fpair(p, carry):
            t = 2 * p
            expand_vs(t + 1, e1)
            fstep(t, e0)
            expand_vs(jnp.minimum(t + 2, L - 1), e0)
            fstep(t + 1, e1)
            return carry

        lax.fori_loop(0, L // 2, fpair, 0)
        expand(L - 1, e0)

        def bstep(t, e_ref):
            A, w, r = e_ref[4, 0:8, :], e_ref[5, 0:8, :], e_ref[6, 0:8, :]
            row = lambda m, i: jnp.broadcast_to(e_ref[m, pl.ds(i, 1), :], (8, 128))
            acc = [_Acc() for _ in range(5)]
            for i in range(HEAD):
                dy_i, dsa_i = row(2, i), row(3, i)
                g = g_ref[i] + dy_i * r
                sp = hist[t, i]
                acc[4].add(i, hist[t + 1, i] * dy_i)
                acc[1].add(i, g * sp)
                acc[2].add(i, g * row(1, i))
                acc[3].add(i, g * row(0, i))
                acc[0].add(i, sp * dsa_i)
                g_ref[i] = g * w + dsa_i * A
            for m in range(5):
                dj[m][t] = acc[m].total()

        def bpair(p, carry):
            t = L - 1 - 2 * p
            expand(t - 1, e1)
            bstep(t, e0)
            expand(jnp.maximum(t - 2, 0), e0)
            bstep(t - 1, e1)
            return carry

        lax.fori_loop(0, L // 2, bpair, 0)
        _tiles_to_perm(dj, L)

    tile = pl.BlockSpec((L, 8, 128), lambda c: (nch - 1 - c, 0, 0))
    return pl.pallas_call(
        body, name="rwkv_scan_bwd_b", grid=(nch,),
        in_specs=[tile] * 9 + [pl.BlockSpec((1, HEAD, 8, 128), lambda c: (nch - 1 - c, 0, 0, 0))],
        out_specs=[tile] * 5, out_shape=[jax.ShapeDtypeStruct((T, 8, 128), F32)] * 5,
        scratch_shapes=[pltpu.VMEM((L + 1, HEAD, 8, 128), F32), pltpu.VMEM((HEAD, 8, 128), F32)] + [pltpu.VMEM((7, HEAD, 128), F32)] * 2,
        compiler_params=_cp(("arbitrary",)),
    )(*xts, *ies, ckb)


SWA_COLS = SWA_W + 2 * KV_W
BLK = 128


def _swa_core(n, k2a, k2b, vla, vra, vlb, vrb, sinks, *qps):
    iq = lax.broadcasted_iota(jnp.int32, (BLK, 2 * BLK), 0)
    ik = lax.broadcasted_iota(jnp.int32, (BLK, 2 * BLK), 1)
    diff = BLK + iq - ik
    valid = (diff >= 0) & (diff < WINDOW) & ((n > 0) | (ik >= BLK))
    lane = lax.broadcasted_iota(jnp.int32, (BLK, 128), 1)
    lane1 = lax.broadcasted_iota(jnp.int32, (1, 128), 1)
    nt = (((1,), (1,)), ((), ()))
    outs = []
    for pp in range(8):
        k2, vl, vr = (k2a, vla, vra) if pp < 4 else (k2b, vlb, vrb)
        qp = qps[pp]
        o = None
        for half, vv in ((0, vl), (1, vr)):
            qh = jnp.where((lane >= HEAD) == (half == 1), qp, 0.0).astype(BF16)
            s = lax.dot_general(qh, k2.astype(BF16), nt, preferred_element_type=F32) * (HEAD ** -0.5)
            s = jnp.where(valid, s, NEG_INF)
            sink = jnp.sum(jnp.where(lane1 == 2 * pp + half, sinks, 0.0), axis=1, keepdims=True)
            m = jnp.maximum(jnp.max(s, axis=1, keepdims=True), sink)
            p = jnp.exp(s - m)
            den = jnp.sum(p, axis=1, keepdims=True) + jnp.exp(sink - m)
            oh = jnp.dot((p / den).astype(BF16), vv.astype(BF16), preferred_element_type=F32)
            o = oh if o is None else o + oh
        outs.append(o)
    return jnp.concatenate(outs, axis=1)


def _swa_prep(pc, pp, b, cq, sq, ckc, skc, ckp, skp):
    zc, zp = pc + b, pp + b
    qr = zc[:, :SWA_W] * cq + _swap32(zc[:, :SWA_W]) * sq
    kc, kp = zc[:, SWA_W:SWA_W + KV_W], zp[:, SWA_W:SWA_W + KV_W]
    kb = jnp.concatenate([kp * ckp + _swap32(kp) * skp, kc * ckc + _swap32(kc) * skc], axis=0)
    vb = jnp.concatenate([zp[:, SWA_W + KV_W:], zc[:, SWA_W + KV_W:]], axis=0)
    lane = lax.broadcasted_iota(jnp.int32, kb.shape, 1)
    left = lane < HEAD
    kbr, vbr = pltpu.roll(kb, HEAD, 1), pltpu.roll(vb, HEAD, 1)
    return (jnp.where(left, kb, kbr), jnp.where(left, kbr, kb), jnp.where(left, vb, 0.0), jnp.where(left, 0.0, vbr),
            jnp.where(left, vbr, 0.0), jnp.where(left, 0.0, vb)), [qr[:, q * 128:(q + 1) * 128] for q in range(8)]


def _swa_specs(T, tabs_q, tabs_k):
    cur = lambda c: pl.BlockSpec((BLK, c), lambda n: (n, 0))
    prev = lambda c: pl.BlockSpec((BLK, c), lambda n: (jnp.maximum(n - 1, 0), 0))
    return cur, prev


def _swa_fwd(p_swa, b, sinks, cq, sq, ck, sk):
    T = p_swa.shape[0]
    cur, prev = _swa_specs(T, None, None)

    def body(pc, pp, b_ref, s_ref, cq_r, sq_r, ckc, skc, ckp, skp, o_ref):
        ops, qps = _swa_prep(pc[...], pp[...], b_ref[...], cq_r[...], sq_r[...], ckc[...], skc[...], ckp[...], skp[...])
        o_ref[...] = _swa_core(pl.program_id(0), *ops, s_ref[...], *qps).astype(o_ref.dtype)

    full = lambda a: pl.BlockSpec(a.shape, lambda n: (0, 0))
    return pl.pallas_call(
        body, name="swa_fwd", grid=(T // BLK,),
        in_specs=[cur(SWA_COLS), prev(SWA_COLS), full(b), full(sinks), cur(SWA_W), cur(SWA_W), cur(KV_W), cur(KV_W), prev(KV_W), prev(KV_W)],
        out_specs=cur(SWA_W), out_shape=jax.ShapeDtypeStruct((T, SWA_W), BF16), compiler_params=_cp(("arbitrary",)),
    )(p_swa, p_swa, b, sinks, cq, sq, ck, sk, ck, sk)


def _swa_bwd(p_swa, b, sinks, cq, sq, ck, sk, do):
    T = p_swa.shape[0]
    nb = T // BLK
    cur = lambda c: pl.BlockSpec((BLK, c), lambda s: (nb - 1 - s, 0))
    prev = lambda c: pl.BlockSpec((BLK, c), lambda s: (jnp.maximum(nb - 2 - s, 0), 0))

    def body(pc, pp, b_ref, s_ref, cq_r, sq_r, ckc, skc, ckp, skp, do_ref, dcur, db, dsk, carry):
        step = pl.program_id(0)
        n = nb - 1 - step

        @pl.when(step == 0)
        def _():
            carry[...] = jnp.zeros_like(carry)
            db[...] = jnp.zeros_like(db)
            dsk[...] = jnp.zeros_like(dsk)

        ops, qps = _swa_prep(pc[...], pp[...], b_ref[...], cq_r[...], sq_r[...], ckc[...], skc[...], ckp[...], skp[...])
        _, vjp = jax.vjp(functools.partial(_swa_core, n), *ops, s_ref[...], *qps)
        dk2a, dk2b, dvla, dvra, dvlb, dvrb, dsinks, *dqps = vjp(do_ref[...].astype(F32))
        dqr = jnp.concatenate(dqps, axis=1)
        lane = lax.broadcasted_iota(jnp.int32, dk2a.shape, 1)
        left = lane < HEAD
        dkb = jnp.where(left, dk2a + pltpu.roll(dk2a, HEAD, 1), dk2b + pltpu.roll(dk2b, HEAD, 1))
        dvb = jnp.where(left, dvla + pltpu.roll(dvra, HEAD, 1), pltpu.roll(dvlb, HEAD, 1) + dvrb)
        dq = dqr * cq_r[...] + _swap32(dqr * sq_r[...])
        dkp, dkc = dkb[:BLK], dkb[BLK:]
        dkp = dkp * ckp[...] + _swap32(dkp * skp[...])
        dkc = dkc * ckc[...] + _swap32(dkc * skc[...])
        dc = jnp.concatenate([dq, jnp.concatenate([dkc, dvb[BLK:]], axis=1) + carry[...]], axis=1)
        carry[...] = jnp.concatenate([dkp, dvb[:BLK]], axis=1)
        dcur[...] = dc.astype(dcur.dtype)
        db[...] += jnp.sum(dc, axis=0, keepdims=True)
        dsk[...] += dsinks

    full = lambda a: pl.BlockSpec(a.shape, lambda s: (0, 0))
    return pl.pallas_call(
        body, name="swa_bwd", grid=(nb,),
        in_specs=[cur(SWA_COLS), prev(SWA_COLS), full(b), full(sinks), cur(SWA_W), cur(SWA_W), cur(KV_W), cur(KV_W), prev(KV_W), prev(KV_W),
                  cur(SWA_W)],
        out_specs=[cur(SWA_COLS), full(b), full(sinks)],
        out_shape=[jax.ShapeDtypeStruct((T, SWA_COLS), BF16), jax.ShapeDtypeStruct(b.shape, F32), jax.ShapeDtypeStruct(sinks.shape, F32)],
        scratch_shapes=[pltpu.VMEM((BLK, 2 * KV_W), F32)], compiler_params=_cp(("arbitrary",)),
    )(p_swa, p_swa, b, sinks, cq, sq, ck, sk, ck, sk, do)


def _rope_tables(T):
    inv = 10000.0 ** (-jnp.arange(0, HEAD, 2, dtype=F32) / HEAD)
    ang = jnp.arange(T, dtype=F32)[:, None] * inv[None, :]
    c = jnp.concatenate([jnp.cos(ang), jnp.cos(ang)], axis=1)
    s = jnp.concatenate([-jnp.sin(ang), jnp.sin(ang)], axis=1)
    return jnp.tile(c, (1, 16)), jnp.tile(s, (1, 16)), jnp.tile(c, (1, 2)), jnp.tile(s, (1, 2))


def _xattn_core(*qkv):
    outs = []
    for h in range(XH):
        qh, kh, vh = qkv[h], qkv[XH + h], qkv[2 * XH + h]
        s = lax.dot_general(qh.astype(BF16), kh.astype(BF16), (((1,), (1,)), ((), ())), preferred_element_type=F32) * (XHD ** -0.5)
        p = jnp.exp(s - jnp.max(s, axis=1, keepdims=True))
        p = p / jnp.sum(p, axis=1, keepdims=True)
        outs.append(jnp.dot(p.astype(BF16), vh.astype(BF16), preferred_element_type=F32))
    return jnp.concatenate(outs, axis=1)


def _xattn_split(q, kv):
    return [q[:, h * XHD:(h + 1) * XHD] for h in range(XH)] + [kv[:, h * XHD:(h + 1) * XHD] for h in range(2 * XH)]


def _xattn_fwd(q, kv, tm=256):
    (o,) = _rows(lambda i, q, kv: (_xattn_core(*_xattn_split(q, kv)),), "xattn_fwd", q.shape[0], tm, [q], [kv], [(q.shape[1], BF16)], [])
    return o


def _xattn_bwd(q, kv, do, tm=256):
    def fn(i, q, do, kv):
        _, vjp = jax.vjp(_xattn_core, *_xattn_split(q, kv))
        d = vjp(do.astype(F32))
        return jnp.concatenate(d[:XH], axis=1), jnp.concatenate(d[XH:], axis=1)

    return _rows(fn, "xattn_bwd", q.shape[0], tm, [q, do], [kv], [(q.shape[1], BF16)], [(kv.shape, F32)])


def _loss_head(x, g, tgt, tm=256):
    D = x.shape[1]

    def fn(i, x, tgt, g):
        y, vjp = jax.vjp(_rms, x, g)
        err = y - tgt
        dx, dg = vjp(err * (1.0 / D))
        part = 0.5 / D * jnp.sum(jnp.sum(err * err, axis=1, keepdims=True), axis=0, keepdims=True)
        return dx, jnp.broadcast_to(part, (1, 128)), dg

    return _rows(fn, "loss_head", x.shape[0], tm, [x, tgt], [g], [(D, F32)], [((1, 128), F32), ((1, D), F32)])


def _local_step(x, mem, tgt, get_w, P, put_g):
    T = x.shape[0]
    W = dict(get_w("f1", None))

    def f1_down(after):
        W.update(get_w("f1d", after))
        return W["f1_down"]

    x1, s1 = _ffn_fwd(x, P["f1_norm"], W["f1_gate"], W["f1_up"], f1_down, "f1")

    W.update(get_w("mix", x1))
    h2 = _rms_fwd(x1, P["mix_norm"], "mix_norm")
    w_rkv, w_lora, w_swa = W["w_inT"][:3 * RW_W], W["w_inT"][3 * RW_W:SHIFT_COLS], W["w_inT"][SHIFT_COLS:]
    p_rkv = _mm(h2, w_rkv, "nt", "in_rkv", after=W.get("_after"))
    p_lora = _mm(h2, w_lora, "nt", "in_lora")
    p_swa = _mm(h2, w_swa, "nt", "in_swa")
    w_da = jnp.concatenate([W["rw_decay_up"], W["rw_aaa_up"]], axis=0)
    pre_params = (P["rw_mu"][:, :3 * RW_W], P["rw_mu"][:, 3 * RW_W:], P["rw_w0"], P["rw_a0"], P["rw_k_k"], P["rw_k_a"], w_da,
                  W["rw_gate_up"])
    r, decay, k2, v, an, bn, g = _rwkv_pre(p_rkv, p_lora, pre_params)
    scan_vecs = (an, decay, bn, k2, r)
    xes = [_to_perm(a) for a in scan_vecs]
    v_p = _to_perm(v)
    yi, sai, ck = _scan_fwd(xes, v_p)
    y_scan = _from_perm(yi)
    y_rw = _rwkv_post(y_scan, r, k2, v, g, P["rw_lnx_w"], P["rw_lnx_b"], P["rw_r_k"])
    cq, sq, ckt, skt = _rope_tables(T)
    y_swa = _swa_fwd(p_swa, P["b_in_attn"], P["attn_sinks"], cq, sq, ckt, skt)
    ycat = jnp.concatenate([y_rw, y_swa], axis=1)
    W.update(get_w("out", ycat))
    x2 = _mm(ycat, W["w_out"], "nn", "out_proj", res=x1, bias=P["b_out"])

    W.update(get_w("xattn", x2))
    hx = _rms_fwd(x2, P["xa_norm"], "xa_norm")
    mn = _rms_fwd(mem, P["mem_norm"], "mem_norm")
    q = _mm(hx, W["w_xq"], "nn", "xq", out_dtype=BF16)
    kv = _mm(mn, W["w_xkv"], "nn", "xkv", out_dtype=BF16)
    o = _xattn_fwd(q, kv)
    x3 = _mm(o, W["w_xo"], "nn", "xo", res=x2)

    W.update(get_w("f2", x3))
    x4, s2 = _ffn_fwd(x3, P["f2_norm"], W["f2_gate"], W["f2_up"], W["f2_down"], "f2")
    dx4, loss_part, d_final = _loss_head(x4, P["final_norm"], tgt)

    gs = {"final_norm": d_final}
    dx3, gs["f2_norm"] = _ffn_bwd(x3, P["f2_norm"], W["f2_gate"], W["f2_up"], W["f2_down"], s2, dx4, "f2", put_g)

    do = _mm(dx3, W["w_xo"], "nt", "xo_do", out_dtype=BF16)
    dw_xo = _mm(o, dx3, "tn", "xo_dw", out_dtype=BF16)
    dq, dkv = _xattn_bwd(q, kv, do)
    dw_xq = _mm(hx, dq, "tn", "xq_dw", out_dtype=BF16)
    dw_xkv = _mm(mn, dkv, "tn", "xkv_dw", out_dtype=BF16)
    sent = put_g("xattn", {"w_xq": dw_xq, "w_xkv": dw_xkv, "w_xo": dw_xo})
    dhx = _mm(dq, W["w_xq"], "nt", "xq_dh", after=sent)
    dmn = _mm(dkv, W["w_xkv"], "nt", "xkv_dmn")
    _, gs["mem_norm"], _ = _rms_bwd(mem, P["mem_norm"], dmn, jnp.zeros_like(mem), "mem_norm_bwd")
    dx2, gs["xa_norm"], gs["b_out"] = _rms_bwd(x2, P["xa_norm"], dhx, dx3, "xa_norm_bwd")

    dycat = _mm(dx2, W["w_out"], "nt", "out_dy")
    dw_out = _mm(ycat, dx2, "tn", "out_dw", out_dtype=BF16)
    dp_swa, gs["b_in_attn"], gs["attn_sinks"] = _swa_bwd(p_swa, P["b_in_attn"], P["attn_sinks"], cq, sq, ckt, skt, dycat[:, RW_W:])
    dy_scan, dr_b, dk2_b, dv_b, dg, gs["rw_lnx_w"], gs["rw_lnx_b"], gs["rw_r_k"] = _rwkv_post_bwd(
        y_scan, r, k2, v, g, P["rw_lnx_w"], P["rw_lnx_b"], P["rw_r_k"], dycat[:, :RW_W])
    dy_p = _to_perm(dy_scan)
    dsai, dvi = _scan_bwd_a(xes, dy_p)
    dj = _scan_bwd_b(xes, [v_p, sai, dy_p, dsai], _ck_a_to_b(ck))
    dan, ddecay, dbn, dk2_s, dr_s = (_from_perm(d) for d in dj)
    cts = (dr_s, ddecay, dk2_s, _from_perm(dvi), dan, dbn, dg, dr_b, dk2_b, dv_b)
    dp_rkv, dp_lora, dmu, dmul, gs["rw_w0"], gs["rw_a0"], gs["rw_k_k"], gs["rw_k_a"], dw_da, gs["rw_gate_up"] = _rwkv_pre_bwd(
        p_rkv, p_lora, pre_params, cts)
    gs["rw_mu"] = jnp.concatenate([dmu, dmul], axis=1)
    gs["rw_decay_up"], gs["rw_aaa_up"] = dw_da[:DECAY_LORA], dw_da[DECAY_LORA:]
    dw_inT = jnp.concatenate([_mm(dp_rkv, h2, "tn", "in_dw_rkv"), _mm(dp_lora, h2, "tn", "in_dw_lora"),
                              _mm(dp_swa, h2, "tn", "in_dw_swa")], axis=0)
    sent = put_g("mix", {"w_in": dw_inT, "w_out": dw_out})
    dh2 = _mm(dp_rkv, w_rkv, "nn", "in_dh_rkv", after=sent)
    dh2 = _mm(dp_lora, w_lora, "nn", "in_dh_lora", res=dh2)
    dh2 = _mm(dp_swa, w_swa, "nn", "in_dh_swa", res=dh2)
    dx1, gs["mix_norm"], _ = _rms_bwd(x1, P["mix_norm"], dh2, dx2, "mix_norm_bwd")

    dx0, gs["f1_norm"] = _ffn_bwd(x, P["f1_norm"], W["f1_gate"], W["f1_up"], W["f1_down"], s1, dx1, "f1", put_g)
    return loss_part, dx0, gs


_ANY = pl.BlockSpec(memory_space=pl.ANY)
_OTHER_CHIPS = ((1, 0), (0, 1), (1, 1))


def _mesh_pos():
    return lax.axis_index("x"), lax.axis_index("y"), lax.axis_index("c")


def _slot(ref, kind, s, rows, cols):
    if kind == "row":
        return ref.at[pl.ds(pl.multiple_of(s * rows, 8), rows), :]
    return ref.at[:, pl.ds(pl.multiple_of(s * cols, 128), cols)]


_HBM = pl.BlockSpec(memory_space=pltpu.HBM)
_SEMS = pl.BlockSpec(memory_space=pltpu.SEMAPHORE)
_SPLIT = dict(compiler_params=pltpu.CompilerParams(has_side_effects=pltpu.SideEffectType.DATAFLOW_SIDE_EFFECTING))


def _in_hbm(a):
    return pltpu.with_memory_space_constraint(a, pltpu.HBM)


def _full_shape(s, kind):
    return (4 * s.shape[0], s.shape[1]) if kind == "row" else (s.shape[0], 4 * s.shape[1])


def _half(ref, shape, h):
    rows, cols = shape
    if rows % 32 == 0:
        return ref.at[pl.ds(pl.multiple_of(h * (rows // 2), 16), rows // 2), :]
    assert cols % 256 == 0, shape
    return ref.at[:, pl.ds(pl.multiple_of(h * (cols // 2), 128), cols // 2)]


def _half_shape(shape):
    rows, cols = shape
    return (rows // 2, cols) if rows % 32 == 0 else (rows, cols // 2)


def _streams(src, dst, shape, c):
    hs = _half_shape(shape)
    s, d = _half(src, shape, c), _half(dst, shape, c)
    return [(_half(s, hs, q), _half(d, hs, q)) for q in range(2)]


def _swap_halves(name, fulls, shard_shapes, kinds):
    n = len(fulls)

    def body(*refs):
        out, send, recv = refs[n:2 * n], refs[2 * n], refs[2 * n + 1]
        x, y, c = _mesh_pos()
        sent = []
        for i in range(n):
            for r, (dx, dy) in enumerate(_OTHER_CHIPS):
                theirs = _slot(out[i], kinds[i], 2 * ((x + dx) % 2) + (y + dy) % 2, *shard_shapes[i])
                have = _half(theirs, shard_shapes[i], c)
                rc = pltpu.make_async_remote_copy(have, have, send.at[3 * i + r], recv.at[3 * i + r], device_id=(x, y, 1 - c),
                                                  device_id_type=MESH)
                rc.start()
                sent.append(rc)
        for i in range(n):
            for r, (dx, dy) in enumerate(_OTHER_CHIPS):
                theirs = _slot(out[i], kinds[i], 2 * ((x + dx) % 2) + (y + dy) % 2, *shard_shapes[i])
                need = _half(theirs, shard_shapes[i], 1 - c)
                pltpu.make_async_remote_copy(need, need, send.at[3 * i + r], recv.at[3 * i + r], device_id=(x, y, c),
                                             device_id_type=MESH).wait_recv()
        for rc in sent:
            rc.wait_send()

    return pl.pallas_call(
        body, name=name, in_specs=[_ANY] * n, out_specs=[_ANY] * n, out_shape=[jax.ShapeDtypeStruct(f.shape, f.dtype) for f in fulls],
        input_output_aliases={i: i for i in range(n)},
        scratch_shapes=[pltpu.SemaphoreType.DMA((3 * n,)), pltpu.SemaphoreType.DMA((3 * n,))],
    )(*fulls)


def _gather_start(name, shards, kinds, groups, after=None):
    n, ng = len(shards), len(groups)
    lands = [_in_hbm(lax.empty(_full_shape(s, k), s.dtype)) for s, k in zip(shards, kinds)]
    n_in = 2 * n + (after is not None)

    def body(*refs):
        src, land, sems, token = refs[:n], refs[n:2 * n], refs[n_in:n_in + 3 * ng], refs[-1]
        x, y, c = _mesh_pos()
        me = 2 * x + y
        for gi, idxs in enumerate(groups):
            send, recv, own = sems[3 * gi:3 * gi + 3]
            for k, i in enumerate(idxs):
                mine = _slot(land[i], kinds[i], me, *src[i].shape)
                for r, (dx, dy) in enumerate(_OTHER_CHIPS):
                    for q, (s, d) in enumerate(_streams(src[i], mine, src[i].shape, c)):
                        pltpu.make_async_remote_copy(s, d, send.at[6 * k + 2 * r + q], recv.at[6 * k + 2 * r + q],
                                                     device_id=((x + dx) % 2, (y + dy) % 2, c), device_id_type=MESH).start()
                pltpu.make_async_copy(src[i], mine, own.at[k]).start()
        token[...] = jnp.zeros_like(token)

    sem_shapes = [pltpu.SemaphoreType.DMA((w * len(g),)) for g in groups for w in (6, 6, 1)]
    thru = [pltpu.HBM(a.shape, a.dtype) for a in (*shards, *lands)]
    res = pl.pallas_call(
        body, name=name, in_specs=[_HBM] * (2 * n) + [_ANY] * (after is not None),
        out_specs=[_SEMS] * (3 * ng) + [_HBM] * (2 * n) + [pl.BlockSpec(memory_space=pltpu.VMEM)],
        out_shape=sem_shapes + thru + [jax.ShapeDtypeStruct((8, 128), F32)],
        input_output_aliases={i: 3 * ng + i for i in range(2 * n)}, **_SPLIT,
    )(*[_in_hbm(s) for s in shards], *lands, *([] if after is None else [after]))
    return res[:3 * ng], res[3 * ng:3 * ng + n], res[3 * ng + n:3 * ng + 2 * n], res[-1]


def _gather_wait(name, sems, shards, lands, kinds, after):
    m = len(shards)

    def body(*refs):
        src, land, (send, recv, own) = refs[:m], refs[m:2 * m], refs[2 * m:2 * m + 3]
        x, y, c = _mesh_pos()
        me = 2 * x + y
        for k in range(m):
            mine = _slot(land[k], kinds[k], me, *src[k].shape)
            for r in range(3):
                for q, (s, d) in enumerate(_streams(src[k], mine, src[k].shape, c)):
                    cp = pltpu.make_async_remote_copy(s, d, send.at[6 * k + 2 * r + q], recv.at[6 * k + 2 * r + q], device_id=(x, y, c),
                                                      device_id_type=MESH)
                    cp.wait_send()
                    cp.wait_recv()
            pltpu.make_async_copy(src[k], mine, own.at[k]).wait()

    thru = [pltpu.HBM(a.shape, a.dtype) for a in (*shards, *lands)]
    res = pl.pallas_call(
        body, name=name, in_specs=[_HBM] * (2 * m) + [_SEMS] * 3 + [pl.BlockSpec(memory_space=pl.ANY)],
        out_specs=[_HBM] * (2 * m), out_shape=thru, input_output_aliases={i: i for i in range(2 * m)}, **_SPLIT,
    )(*shards, *lands, *sems, after)
    return res[m:]


def _scatter_start(name, grads, kinds):
    m = len(grads)
    shard_shape = [(g.shape[0] // 4, g.shape[1]) if k == "row" else (g.shape[0], g.shape[1] // 4) for g, k in zip(grads, kinds)]
    lands = [_in_hbm(lax.empty((4, *s), g.dtype)) for s, g in zip(shard_shape, grads)]

    def body(*refs):
        src, land, (send, recv, own) = refs[:m], refs[m:2 * m], refs[2 * m:2 * m + 3]
        x, y, c = _mesh_pos()
        me = 2 * x + y
        for k in range(m):
            for r, (dx, dy) in enumerate(_OTHER_CHIPS):
                tx, ty = (x + dx) % 2, (y + dy) % 2
                pltpu.make_async_remote_copy(_slot(src[k], kinds[k], 2 * tx + ty, *shard_shape[k]), land[k].at[me],
                                             send.at[3 * k + r], recv.at[3 * k + r], device_id=(tx, ty, c), device_id_type=MESH).start()
            pltpu.make_async_copy(_slot(src[k], kinds[k], me, *shard_shape[k]), land[k].at[me], own.at[k]).start()
        refs[-1][...] = jnp.zeros_like(refs[-1])

    thru = [pltpu.HBM(a.shape, a.dtype) for a in (*grads, *lands)]
    res = pl.pallas_call(
        body, name=name, in_specs=[_HBM] * (2 * m),
        out_specs=[_SEMS] * 3 + [_HBM] * (2 * m) + [pl.BlockSpec(memory_space=pltpu.VMEM)],
        out_shape=[pltpu.SemaphoreType.DMA((3 * m,))] * 2 + [pltpu.SemaphoreType.DMA((m,))] + thru + [jax.ShapeDtypeStruct((8, 128), F32)],
        input_output_aliases={i: 3 + i for i in range(2 * m)}, **_SPLIT,
    )(*[_in_hbm(g) for g in grads], *lands)
    return res[:3], res[3:3 + m], res[3 + m:3 + 2 * m], res[-1]


def _scatter_wait(name, sems, grads, lands, kinds, after):
    m = len(grads)

    def body(*refs):
        src, land, (send, recv, own) = refs[:m], refs[m:2 * m], refs[2 * m:2 * m + 3]
        x, y, c = _mesh_pos()
        me = 2 * x + y
        for k in range(m):
            mine = _slot(src[k], kinds[k], me, *land[k].shape[1:])
            for r in range(3):
                cp = pltpu.make_async_remote_copy(mine, land[k].at[me], send.at[3 * k + r], recv.at[3 * k + r],
                                                  device_id=(x, y, c), device_id_type=MESH)
                cp.wait_send()
                cp.wait_recv()
            pltpu.make_async_copy(mine, land[k].at[me], own.at[k]).wait()

    thru = [pltpu.HBM(a.shape, a.dtype) for a in (*grads, *lands)]
    res = pl.pallas_call(
        body, name=name, in_specs=[_HBM] * (2 * m) + [_SEMS] * 3 + [pl.BlockSpec(memory_space=pl.ANY)],
        out_specs=[_HBM] * (2 * m), out_shape=thru, input_output_aliases={i: i for i in range(2 * m)}, **_SPLIT,
    )(*grads, *lands, *sems, after)
    return res[m:]


def _swap_with_sibling(arrs, name):
    n = len(arrs)

    def body(*refs):
        ins, outs = refs[:n], refs[n:2 * n]
        send, recv = refs[2 * n:]
        x, y, c = _mesh_pos()
        copies = []
        for i in range(n):
            rc = pltpu.make_async_remote_copy(ins[i], outs[i], send.at[i], recv.at[i], device_id=(x, y, 1 - c), device_id_type=MESH)
            rc.start()
            copies.append(rc)
        for rc in copies:
            rc.wait()

    return pl.pallas_call(
        body, name=name, in_specs=[_ANY] * n, out_specs=[_ANY] * n,
        out_shape=[jax.ShapeDtypeStruct(a.shape, a.dtype) for a in arrs],
        scratch_shapes=[pltpu.SemaphoreType.DMA((n,)), pltpu.SemaphoreType.DMA((n,))],
    )(*arrs)


def _small_start(pack, after):
    land = _in_hbm(lax.empty((8, *pack.shape), pack.dtype))

    def body(in_ref, land_ref, after_ref, send, recv, own, in_thru, land_thru, token):
        x, y, c = _mesh_pos()
        me = 4 * x + 2 * y + c
        for r in range(1, 8):
            dx, dy, dc = r // 4, (r // 2) % 2, r % 2
            pltpu.make_async_remote_copy(in_ref, land_ref.at[me], send.at[r - 1], recv.at[r - 1],
                                         device_id=((x + dx) % 2, (y + dy) % 2, (c + dc) % 2), device_id_type=MESH).start()
        pltpu.make_async_copy(in_ref, land_ref.at[me], own.at[0]).start()
        token[...] = jnp.zeros_like(token)

    res = pl.pallas_call(
        body, name="small_start", in_specs=[_HBM, _HBM, _ANY],
        out_specs=[_SEMS] * 3 + [_HBM, _HBM, pl.BlockSpec(memory_space=pltpu.VMEM)],
        out_shape=[pltpu.SemaphoreType.DMA((7,)), pltpu.SemaphoreType.DMA((7,)), pltpu.SemaphoreType.DMA((1,)),
                   pltpu.HBM(pack.shape, pack.dtype), pltpu.HBM(land.shape, land.dtype), jax.ShapeDtypeStruct((8, 128), F32)],
        input_output_aliases={0: 3, 1: 4}, **_SPLIT,
    )(_in_hbm(pack), land, after)
    return res[:3], res[3], res[4], res[5]


def _small_wait(sems, pack, land, after):
    def body(in_ref, land_ref, send, recv, own, after_ref, in_dead, got):
        x, y, c = _mesh_pos()
        me = 4 * x + 2 * y + c
        for r in range(1, 8):
            cp = pltpu.make_async_remote_copy(in_ref, land_ref.at[me], send.at[r - 1], recv.at[r - 1], device_id=(x, y, c),
                                              device_id_type=MESH)
            cp.wait_send()
            cp.wait_recv()
        pltpu.make_async_copy(in_ref, land_ref.at[me], own.at[0]).wait()

    res = pl.pallas_call(
        body, name="small_wait", in_specs=[_HBM, _HBM] + [_SEMS] * 3 + [_ANY], out_specs=[_HBM, _HBM],
        out_shape=[pltpu.HBM(pack.shape, pack.dtype), pltpu.HBM(land.shape, land.dtype)], input_output_aliases={0: 0, 1: 1}, **_SPLIT,
    )(pack, land, *sems, after)
    return res[1]


def _row_tile(R, dtype, target=256):
    mult = 8 * 4 // jnp.dtype(dtype).itemsize
    best = R
    for t in range(mult, min(R, target) + 1, mult):
        if R % t == 0:
            best = t
    return best


def _sum_slots(stack, name, out_dtype=F32):
    k, R, C = stack.shape
    tr = _row_tile(R, stack.dtype)

    def body(s_ref, o_ref):
        acc = s_ref[0].astype(F32)
        for j in range(1, k):
            acc = acc + s_ref[j].astype(F32)
        o_ref[...] = acc.astype(out_dtype)

    return pl.pallas_call(
        body, name=name, grid=(R // tr,), in_specs=[pl.BlockSpec((k, tr, C), lambda i: (0, i, 0))],
        out_specs=pl.BlockSpec((tr, C), lambda i: (i, 0)), out_shape=jax.ShapeDtypeStruct((R, C), out_dtype),
        compiler_params=_cp(("parallel",)),
    )(stack)


def _adamw(w, m, v, ga, gb, name, after=None):
    R, C = w.shape
    tr = _row_tile(R, F32, 128)
    gs = [ga] if gb is None else [ga, gb]
    extra = [] if after is None else [after]

    def body(*refs):
        w_ref, m_ref, v_ref = refs[:3]
        g = refs[3][...].astype(F32)
        if gb is not None:
            g = g + refs[4][...].astype(F32)
        g_ref, d_ref, nm_ref, nv_ref = refs[-4:]
        nm = ADAM_B1 * m_ref[...] + (1.0 - ADAM_B1) * g
        nv = ADAM_B2 * v_ref[...] + (1.0 - ADAM_B2) * (g * g)
        m_hat = nm / (1.0 - ADAM_B1 ** ADAM_STEP)
        v_hat = nv / (1.0 - ADAM_B2 ** ADAM_STEP)
        g_ref[...] = g
        d_ref[...] = -ADAM_LR * (m_hat / (jnp.sqrt(v_hat) + ADAM_EPS) + ADAM_WD * w_ref[...])
        nm_ref[...] = nm
        nv_ref[...] = nv

    spec = pl.BlockSpec((tr, C), lambda i: (i, 0))
    return pl.pallas_call(
        body, name=name, grid=(R // tr,), in_specs=[spec] * (3 + len(gs)) + [_ANY] * len(extra), out_specs=[spec] * 4,
        out_shape=[jax.ShapeDtypeStruct((R, C), F32)] * 4, compiler_params=_cp(("parallel",)),
    )(w, m, v, *gs, *extra)


def _pack(arrs):
    rows = []
    for a in arrs:
        flat = a.reshape(-1)
        rows.append(jnp.pad(flat, (0, -flat.shape[0] % 1024)).reshape(-1, 1024))
    p = jnp.concatenate(rows, axis=0)
    return jnp.pad(p, ((0, -p.shape[0] % 8), (0, 0)))


def _unpack(p, shapes):
    out, r = [], 0
    for s in shapes:
        n = 1
        for d in s:
            n *= d
        nr = -(-n // 1024)
        out.append(p[r:r + nr].reshape(-1)[:n].reshape(s))
        r += nr
    return out


BIG = ("f1_gate", "f1_up", "f1_down", "w_in", "w_out", "w_xq", "w_xkv", "w_xo", "f2_gate", "f2_up", "f2_down")
BIG_KIND = {"f1_gate": "col", "f1_up": "col", "f1_down": "row", "w_in": "row", "w_out": "row", "w_xq": "row", "w_xkv": "col",
            "w_xo": "row", "f2_gate": "col", "f2_up": "col", "f2_down": "row"}
LORA = ("rw_decay_up", "rw_aaa_up", "rw_gate_up")
WEIGHTS = ("f1_norm", "f1_gate", "f1_up", "f1_down", "mix_norm", "w_in", "b_in_attn", "rw_mu", "rw_w0", "rw_decay_up", "rw_a0",
           "rw_aaa_up", "rw_gate_up", "rw_k_k", "rw_k_a", "rw_r_k", "rw_lnx_w", "rw_lnx_b", "attn_sinks", "w_out", "b_out", "xa_norm",
           "mem_norm", "w_xq", "w_xkv", "w_xo", "f2_norm", "f2_gate", "f2_up", "f2_down", "final_norm")
SMALL = tuple(n for n in WEIGHTS if n not in BIG)
GROUP_ORDER = ("f1", "f1d", "mix", "out", "xattn", "f2")
GROUPS = {"f1": ("f1_gate", "f1_up"), "f1d": ("f1_down",), "mix": ("w_in",) + LORA, "out": ("w_out",), "xattn": ("w_xq", "w_xkv", "w_xo"),
          "f2": ("f2_gate", "f2_up", "f2_down")}


def kernel(x, mem, f1_norm, f1_gate, f1_up, f1_down, mix_norm, w_in, b_in_attn, rw_mu, rw_w0, rw_decay_up, rw_a0, rw_aaa_up, rw_gate_up, rw_k_k, rw_k_a, rw_r_k, rw_lnx_w, rw_lnx_b, attn_sinks, w_out, b_out, xa_norm, mem_norm, w_xq, w_xkv, w_xo, f2_norm, f2_gate, f2_up, f2_down, final_norm, loss_target, m_f1_norm, m_f1_gate, m_f1_up, m_f1_down, m_mix_norm, m_w_in, m_b_in_attn, m_rw_mu, m_rw_w0, m_rw_decay_up, m_rw_a0, m_rw_aaa_up, m_rw_gate_up, m_rw_k_k, m_rw_k_a, m_rw_r_k, m_rw_lnx_w, m_rw_lnx_b, m_attn_sinks, m_w_out, m_b_out, m_xa_norm, m_mem_norm, m_w_xq, m_w_xkv, m_w_xo, m_f2_norm, m_f2_gate, m_f2_up, m_f2_down, m_final_norm, v_f1_norm, v_f1_gate, v_f1_up, v_f1_down, v_mix_norm, v_w_in, v_b_in_attn, v_rw_mu, v_rw_w0, v_rw_decay_up, v_rw_a0, v_rw_aaa_up, v_rw_gate_up, v_rw_k_k, v_rw_k_a, v_rw_r_k, v_rw_lnx_w, v_rw_lnx_b, v_attn_sinks, v_w_out, v_b_out, v_xa_norm, v_mem_norm, v_w_xq, v_w_xkv, v_w_xo, v_f2_norm, v_f2_gate, v_f2_up, v_f2_down, v_final_norm):
    a = dict(locals())
    w = {n: a[n] for n in WEIGHTS}
    m = {n: a["m_" + n] for n in WEIGHTS}
    v = {n: a["v_" + n] for n in WEIGHTS}
    sq = lambda t: t.reshape(t.shape[-2:]) if t.ndim == 3 else t.reshape(1, -1)

    local_name = lambda n: "w_inT" if n == "w_in" else n
    kind_of = lambda n: BIG_KIND.get(n, "col")
    payload = lambda n: sq(w[n]).T if n == "w_in" else sq(w[n]) if n in LORA else sq(w[n]).astype(BF16)
    gathers = {}

    def start_gather(name, grps, after):
        shards = [payload(n) for g in grps for n in GROUPS[g]]
        kinds = [kind_of(n) for g in grps for n in GROUPS[g]]
        groups, at = [], 0
        for g in grps:
            groups.append(list(range(at, at + len(GROUPS[g]))))
            at += len(GROUPS[g])
        sems, src_thru, land_thru, token = _gather_start(name, shards, kinds, groups, after)
        for gi, g in enumerate(grps):
            gathers[g] = (sems[3 * gi:3 * gi + 3], [src_thru[i] for i in groups[gi]], [land_thru[i] for i in groups[gi]],
                          [kinds[i] for i in groups[gi]], token)

    early = GROUP_ORDER[:3]
    start_gather("gather_start", early, None)

    def get_w(grp, after):
        g_sems, g_src, g_land, g_kinds, token = gathers[grp]
        got = _gather_wait("gather_wait_" + grp, g_sems, g_src, g_land, g_kinds, token if after is None else after)
        got = _swap_halves("gather_swap_" + grp, got, [s.shape for s in g_src], g_kinds)
        out = {local_name(n): f for n, f in zip(GROUPS[grp], got)}
        if grp == early[-1]:
            start_gather("gather_start_late", GROUP_ORDER[3:], got[0])
            out["_after"] = gathers[GROUP_ORDER[3]][4]
        return out

    in_flight = []

    def put_g(label, gw):
        names = list(gw)
        *flight, sent = _scatter_start("scatter_start_" + label, [gw[n] for n in names], [kind_of(n) for n in names])
        in_flight.append((label, names, flight))
        return sent

    P = {n: sq(w[n]) for n in SMALL if n not in LORA}
    P["attn_sinks"] = jnp.pad(P["attn_sinks"], ((0, 0), (0, 128 - P["attn_sinks"].shape[1])))
    P["rw_r_k"] = w["rw_r_k"].reshape(1, RW_W)
    loss_part, grad_x, gs = _local_step(x[0], mem[0], loss_target[0], get_w, P, put_g)

    gs["attn_sinks"] = gs["attn_sinks"][:, :16]
    small_flight = _small_start(_pack([gs[n] for n in SMALL] + [loss_part]), grad_x)

    out, after = {}, small_flight[-1]
    for bi, batch in enumerate((in_flight[:-3], in_flight[-3:])):
        b_names, b_partial = [], []
        for label, names, (g_sems, g_thru, l_thru) in batch:
            stacks = _scatter_wait("scatter_wait_" + label, g_sems, g_thru, l_thru, [kind_of(n) for n in names], after)
            partial = [_sum_slots(s, "sum_chips_" + n, F32 if n == "w_in" else BF16) for s, n in zip(stacks, names)]
            b_names += names
            b_partial += partial
            after = partial[-1]
        sibling = _swap_with_sibling(b_partial, "swap_batch%d" % bi)
        chain = None
        for n, pa, sb in zip(b_names, b_partial, sibling):
            if n == "w_in":
                pa, sb = pa.T, sb.T
            out[n] = _adamw(sq(w[n]), sq(m[n]), sq(v[n]), pa, sb, "adamw_" + n, after=chain)
            chain = out[n][1]
        after = chain

    gsum = _sum_slots(_small_wait(*small_flight[:-1], after), "sum_small")
    *summed, loss_row = _unpack(gsum, [gs[n].shape for n in SMALL] + [loss_part.shape])
    g_small = dict(zip(SMALL, summed))
    loss = loss_row[0, 0]
    shard = 2 * lax.axis_index("x") + lax.axis_index("y")
    for n in LORA:
        cols = w[n].shape[-1]
        g_small[n] = lax.dynamic_slice_in_dim(g_small[n], shard * cols, cols, axis=1)

    flat = lambda d: _pack([d[n] for n in SMALL])
    res = _adamw(flat(w), flat(m), flat(v), _pack([g_small[n] for n in SMALL]), None, "adamw_small")
    shapes = [w[n].shape for n in SMALL]
    for k, p in enumerate(res):
        for n, t in zip(SMALL, _unpack(p, shapes)):
            out.setdefault(n, [None] * 4)[k] = t
    outs = [loss, grad_x.reshape(x.shape)]
    for k in range(4):
        outs += [out[n][k].reshape(w[n].shape) for n in WEIGHTS]
    return tuple(outs)
```

```python
import functools

import jax
import jax.numpy as jnp
from jax import lax
from jax.experimental import pallas as pl
from jax.experimental.pallas import tpu as pltpu

F32, BF16 = jnp.float32, jnp.bfloat16
MESH = pl.DeviceIdType.MESH

HEAD = 64
RW_HEADS = 16
RW_W = 1024
SWA_W = 1024
KV_W = 128
DECAY_LORA, AAA_LORA, GATE_LORA = 64, 64, 160
LORA_W = DECAY_LORA + AAA_LORA + GATE_LORA
SHIFT_COLS = 3 * RW_W + LORA_W
XH = 4
XHD = 512
MEM_LEN = 256
WINDOW = 128
GN_EPS = 64e-5
RMS_EPS = 1e-6
NEG_INF = -1e30
ADAM_LR, ADAM_B1, ADAM_B2, ADAM_EPS, ADAM_WD, ADAM_STEP = 0.001, 0.9, 0.999, 1e-08, 0.01, 10

VMEM_LIMIT = 56 * 1024 * 1024


def _cp(sem=None, **kw):
    return pltpu.CompilerParams(dimension_semantics=sem, vmem_limit_bytes=VMEM_LIMIT, **kw)


def _pick(dim, target):
    if dim <= target:
        return dim
    best = None
    for t in range(128, target + 1, 128):
        if dim % t == 0:
            best = t
    assert best is not None, (dim, target)
    return best


_DIMS = {"nn": (((1,), (0,)), ((), ())), "nt": (((1,), (1,)), ((), ())), "tn": (((0,), (0,)), ((), ()))}


def _mm(a, b, mode, name, out_dtype=F32, alpha=1.0, res=None, bias=None, tm=1024, tn=1024, tk=2048, after=None):
    if mode == "nn":
        (M, K), (K2, N) = a.shape, b.shape
    elif mode == "nt":
        (M, K), (N, K2) = a.shape, b.shape
    else:
        (K, M), (K2, N) = a.shape, b.shape
    assert K == K2, (name, a.shape, b.shape)
    tm, tn, tk = _pick(M, tm), _pick(N, tn), _pick(K, tk)
    nk = K // tk
    a_spec = pl.BlockSpec((tk, tm), lambda i, j, k: (k, i)) if mode == "tn" else pl.BlockSpec((tm, tk), lambda i, j, k: (i, k))
    b_spec = pl.BlockSpec((tn, tk), lambda i, j, k: (j, k)) if mode == "nt" else pl.BlockSpec((tk, tn), lambda i, j, k: (k, j))
    o_spec = pl.BlockSpec((tm, tn), lambda i, j, k: (i, j))
    ins, specs = [a, b], [a_spec, b_spec]
    if res is not None:
        ins.append(res)
        specs.append(o_spec)
    if bias is not None:
        ins.append(bias)
        specs.append(pl.BlockSpec((1, tn), lambda i, j, k: (0, j)))
    if after is not None:
        ins.append(after)
        specs.append(pl.BlockSpec(memory_space=pl.ANY))
    dims = _DIMS[mode]

    def body(*refs):
        a_ref, b_ref = refs[0], refs[1]
        part = lax.dot_general(a_ref[...].astype(BF16), b_ref[...].astype(BF16), dims, preferred_element_type=F32)

        def finish(o, o_ref):
            if alpha != 1.0:
                o = o * alpha
            p = 2
            if res is not None:
                o = o + refs[p][...].astype(F32)
                p += 1
            if bias is not None:
                o = o + refs[p][...]
            o_ref[...] = o.astype(out_dtype)

        if nk == 1:
            finish(part, refs[-1])
            return
        o_ref, acc_ref = refs[-2], refs[-1]
        k = pl.program_id(2)

        @pl.when(k == 0)
        def _():
            acc_ref[...] = part

        @pl.when(k > 0)
        def _():
            acc_ref[...] += part

        @pl.when(k == nk - 1)
        def _():
            finish(acc_ref[...], o_ref)

    return pl.pallas_call(
        body, name=name, grid=(M // tm, N // tn, nk), in_specs=specs, out_specs=o_spec,
        out_shape=jax.ShapeDtypeStruct((M, N), out_dtype), scratch_shapes=[pltpu.VMEM((tm, tn), F32)] * (nk > 1),
        compiler_params=_cp(("parallel", "parallel", "arbitrary")),
    )(*ins)


def _rows(fn, name, T, tm, tiled, full, out_tiled, out_acc, extra=(), reverse=False, scratch=()):
    n = T // tm
    idx = (lambda i: n - 1 - i) if reverse else (lambda i: i)
    in_specs = [pl.BlockSpec((tm, a.shape[1]), lambda i: (idx(i), 0)) for a in tiled]
    in_specs += [mk(idx) for _, mk in extra]
    in_specs += [pl.BlockSpec(a.shape, lambda i, nd=a.ndim: (0,) * nd) for a in full]
    out_specs = [pl.BlockSpec((tm, c), lambda i: (idx(i), 0)) for c, _ in out_tiled]
    out_specs += [pl.BlockSpec(s, lambda i, nd=len(s): (0,) * nd) for s, _ in out_acc]
    out_shape = [jax.ShapeDtypeStruct((T, c), d) for c, d in out_tiled] + [jax.ShapeDtypeStruct(s, d) for s, d in out_acc]
    n_in = len(tiled) + len(extra) + len(full)
    n_t, n_a = len(out_tiled), len(out_acc)

    def body(*refs):
        step = pl.program_id(0)
        vals = [r[...] for r in refs[:n_in]]
        outs = fn(idx(step), *vals, *refs[n_in + n_t + n_a:])
        for r, v in zip(refs[n_in:n_in + n_t], outs[:n_t]):
            r[...] = v.astype(r.dtype)
        for r, v in zip(refs[n_in + n_t:n_in + n_t + n_a], outs[n_t:]):
            @pl.when(step == 0)
            def _(r=r):
                r[...] = jnp.zeros_like(r)

            r[...] += v

    return pl.pallas_call(
        body, name=name, grid=(n,), in_specs=in_specs, out_specs=out_specs, out_shape=out_shape,
        scratch_shapes=list(scratch), compiler_params=_cp(("arbitrary",)),
    )(*tiled, *[a for a, _ in extra], *full)


def _rms(x, g):
    return x * lax.rsqrt(jnp.mean(x * x, axis=-1, keepdims=True) + RMS_EPS) * g


def _rms_fwd(x, g, name, tm=256):
    (h,) = _rows(lambda i, x, g: (_rms(x, g),), name, x.shape[0], min(tm, x.shape[0]), [x], [g], [(x.shape[1], BF16)], [])
    return h


def _rms_bwd(x, g, dh, dres, name, tm=256):
    D = x.shape[1]

    def fn(i, x, dh, dres, g):
        _, vjp = jax.vjp(_rms, x, g)
        dx, dg = vjp(dh.astype(F32))
        dx = dx + dres
        return dx, dg, jnp.sum(dx, axis=0, keepdims=True)

    return _rows(fn, name, x.shape[0], tm, [x, dh, dres], [g], [(D, F32)], [((1, D), F32), ((1, D), F32)])


def _ffn_up(h, wg, wu, name, tm=1024, tn=512, after=None):
    (M, K), N = h.shape, wg.shape[1]
    tm, tn = _pick(M, tm), _pick(N, tn)

    def body(*refs):
        h_ref, wg_ref, wu_ref = refs[:3]
        g_ref, u_ref, a_ref = refs[-3:]
        hb = h_ref[...].astype(BF16)
        g = jnp.dot(hb, wg_ref[...].astype(BF16), preferred_element_type=F32)
        u = jnp.dot(hb, wu_ref[...].astype(BF16), preferred_element_type=F32)
        g_ref[...] = g
        u_ref[...] = u
        a_ref[...] = (g * jax.nn.sigmoid(g) * u).astype(BF16)

    o_spec = pl.BlockSpec((tm, tn), lambda i, j: (i, j))
    w_spec = pl.BlockSpec((K, tn), lambda i, j: (0, j))
    extra = [] if after is None else [after]
    return pl.pallas_call(
        body, name=name, grid=(M // tm, N // tn),
        in_specs=[pl.BlockSpec((tm, K), lambda i, j: (i, 0)), w_spec, w_spec] + [pl.BlockSpec(memory_space=pl.ANY)] * len(extra),
        out_specs=[o_spec] * 3, out_shape=[jax.ShapeDtypeStruct((M, N), F32)] * 2 + [jax.ShapeDtypeStruct((M, N), BF16)],
        compiler_params=_cp(("parallel", "parallel")),
    )(h, wg, wu, *extra)


def _ffn_dact(dxo, wd, g, u, name, tm=1024, tn=512):
    (M, K), N = dxo.shape, wd.shape[0]
    tm, tn = _pick(M, tm), _pick(N, tn)

    def body(dx_ref, wd_ref, g_ref, u_ref, dg_ref, du_ref):
        da = 0.5 * lax.dot_general(dx_ref[...].astype(BF16), wd_ref[...].astype(BF16), _DIMS["nt"], preferred_element_type=F32)
        g = g_ref[...]
        s = jax.nn.sigmoid(g)
        dg_ref[...] = (da * u_ref[...] * (s * (1.0 + g * (1.0 - s)))).astype(BF16)
        du_ref[...] = (da * (g * s)).astype(BF16)

    t_spec = pl.BlockSpec((tm, tn), lambda i, j: (i, j))
    return pl.pallas_call(
        body, name=name, grid=(M // tm, N // tn),
        in_specs=[pl.BlockSpec((tm, K), lambda i, j: (i, 0)), pl.BlockSpec((tn, K), lambda i, j: (j, 0)), t_spec, t_spec],
        out_specs=[t_spec, t_spec], out_shape=[jax.ShapeDtypeStruct((M, N), BF16)] * 2, compiler_params=_cp(("parallel", "parallel")),
    )(dxo, wd, g, u)


def _ffn_fwd(x, gain, wg, wu, wd, tag, after=None):
    h = _rms_fwd(x, gain, tag + "_norm")
    G, U, A = _ffn_up(h, wg, wu, tag + "_up", after=after)
    xo = _mm(A, wd(A) if callable(wd) else wd, "nn", tag + "_down", alpha=0.5, res=x)
    return xo, (h, G, U, A)


def _ffn_bwd(x, gain, wg, wu, wd, saved, dxo, tag, send):
    h, G, U, A = saved
    dwd = _mm(A, dxo, "tn", tag + "_dwd", out_dtype=BF16, alpha=0.5, tm=1408)
    sent = send(tag + "_down", {tag + "_down": dwd})
    dG, dU = _ffn_dact(dxo, wd, G, U, tag + "_dact")
    dwu = _mm(h, dU, "tn", tag + "_dwu", out_dtype=BF16, after=sent)
    sent = send(tag + "_up", {tag + "_up": dwu})
    dwg = _mm(h, dG, "tn", tag + "_dwg", out_dtype=BF16, after=sent)
    sent = send(tag + "_gate", {tag + "_gate": dwg})
    dh = _mm(dG, wg, "nt", tag + "_dh_g", after=sent)
    dh = _mm(dU, wu, "nt", tag + "_dh_u", res=dh)
    dx, dgain, _ = _rms_bwd(x, gain, dh, dxo, tag + "_norm_bwd")
    return dx, dgain


def _segsum64_impl(x):
    r = lax.broadcasted_iota(jnp.int32, (128, 128), 0) // HEAD
    c = lax.broadcasted_iota(jnp.int32, (128, 128), 1) // HEAD
    ones = (r == c).astype(BF16)
    hi = x.astype(BF16)
    lo = (x - hi.astype(F32)).astype(BF16)
    outs = []
    for q in range(x.shape[1] // 128):
        sl = slice(q * 128, (q + 1) * 128)
        outs.append(jnp.dot(hi[:, sl], ones, preferred_element_type=F32) + jnp.dot(lo[:, sl], ones, preferred_element_type=F32))
    return outs[0] if len(outs) == 1 else jnp.concatenate(outs, axis=1)


@jax.custom_vjp
def _segsum64(x):
    return _segsum64_impl(x)


_segsum64.defvjp(lambda x: (_segsum64_impl(x), None), lambda _, ct: (_segsum64_impl(ct),))


def _swap32(x):
    lane = lax.broadcasted_iota(jnp.int32, (x.shape[0], 128), 1)
    outs = [jnp.take_along_axis(x[:, q * 128:(q + 1) * 128], lane ^ 32, axis=1) for q in range(x.shape[1] // 128)]
    return outs[0] if len(outs) == 1 else jnp.concatenate(outs, axis=1)


def _tree_sum(xs):
    xs = list(xs)
    while len(xs) > 1:
        nxt = [xs[i] + xs[i + 1] for i in range(0, len(xs) - 1, 2)]
        if len(xs) % 2:
            nxt.append(xs[-1])
        xs = nxt
    return xs[0]


class _Acc:
    def __init__(self, ways=4):
        self.parts = [None] * ways

    def add(self, i, term):
        k = i % len(self.parts)
        self.parts[k] = term if self.parts[k] is None else self.parts[k] + term

    def total(self):
        return _tree_sum([p for p in self.parts if p is not None])


def _softplus(x):
    return jnp.maximum(x, 0.0) + jnp.log(1.0 + jnp.exp(-jnp.abs(x)))


def _pre_core(k, da, gd, w0, a0, k_k, k_a, w_da, gate_up):
    lane = lax.broadcasted_iota(jnp.int32, da.shape, 1)
    w_da = w_da.astype(BF16)
    l1 = jnp.dot(jnp.where(lane < DECAY_LORA, jnp.tanh(da), 0.0).astype(BF16), w_da, preferred_element_type=F32)
    l2 = jnp.dot(jnp.where(lane >= DECAY_LORA, da, 0.0).astype(BF16), w_da, preferred_element_type=F32)
    wlog = -_softplus(-(w0 + l1)) - 0.5
    decay = jnp.exp(-jnp.exp(wlog))
    a = jax.nn.sigmoid(a0 + l2)
    g = jnp.dot(jax.nn.sigmoid(gd).astype(BF16), gate_up.astype(BF16), preferred_element_type=F32)
    kk = k * k_k
    kkn = kk / jnp.maximum(jnp.sqrt(_segsum64(kk * kk)), 1e-12)
    k2 = k * (1.0 + (a - 1.0) * k_a)
    return decay, k2, -kkn, kkn * a, g


def _pre_shift(i, zr, zl, zr8, zl8, mu, mul):
    live = (i > 0).astype(F32)
    dz = _shift_down(zr, zr8[7:8, :] * live) - zr
    dzl = _shift_down(zl, zl8[7:8, :] * live) - zl
    return zr + dz * mu, zl + dzl * mul, dz, dzl


def _shift_down(x, first_row):
    rolled = pltpu.roll(x, 1, 0)
    row = lax.broadcasted_iota(jnp.int32, x.shape, 0)
    return jnp.where(row == 0, first_row, rolled)


def _shift_up(x, last_row):
    rolled = pltpu.roll(x, x.shape[0] - 1, 0)
    row = lax.broadcasted_iota(jnp.int32, x.shape, 0)
    return jnp.where(row == x.shape[0] - 1, last_row, rolled)


def _prev_rows_spec(tm, cols):
    return lambda idx: pl.BlockSpec((8, cols), lambda i: (jnp.maximum(idx(i) * (tm // 8) - 1, 0), 0))


def _rwkv_pre(p_rkv, p_lora, params, tm=256):
    T = p_rkv.shape[0]

    def fn(i, zr, zl, zr8, zl8, mu, mul, *ps):
        z, z2, _, _ = _pre_shift(i, zr, zl, zr8, zl8, mu, mul)
        decay, k2, an, bn, g = _pre_core(z[:, RW_W:2 * RW_W], z2[:, :128], z2[:, 128:], *ps)
        return z[:, :RW_W], decay, k2, z[:, 2 * RW_W:], an, bn, g

    extra = [(p_rkv, _prev_rows_spec(tm, 3 * RW_W)), (p_lora, _prev_rows_spec(tm, LORA_W))]
    return _rows(fn, "rwkv_pre", T, tm, [p_rkv, p_lora], list(params), [(RW_W, F32)] * 7, [], extra=extra)


def _rwkv_pre_bwd(p_rkv, p_lora, params, cts, tm=256):
    T = p_rkv.shape[0]
    n = T // tm

    def fn(i, zr, zl, cr, cdec, ck2, cv, can, cbn, cg, cr_b, ck2_b, cv_b, zr8, zl8, mu, mul, *rest):
        ps, (car, carl) = rest[:-2], rest[-2:]
        cr, ck2, cv = cr + cr_b, ck2 + ck2_b, cv + cv_b
        z, z2, dif, difl = _pre_shift(i, zr, zl, zr8, zl8, mu, mul)
        _, vjp = jax.vjp(_pre_core, z[:, RW_W:2 * RW_W], z2[:, :128], z2[:, 128:], *ps)
        dk, dda, dgd, *dps = vjp((cdec, ck2, can, cbn, cg))
        dz = jnp.concatenate([cr, dk, cv], axis=1)
        dz2 = jnp.concatenate([dda, dgd], axis=1)
        dzp, dzlp = dz * mu, dz2 * mul

        @pl.when(i == n - 1)
        def _():
            car[...] = jnp.zeros_like(car)
            carl[...] = jnp.zeros_like(carl)

        d_rkv = dz - dzp + _shift_up(dzp, car[0:1, :])
        d_lora = dz2 - dzlp + _shift_up(dzlp, carl[0:1, :])
        car[0:1, :] = dzp[0:1, :]
        carl[0:1, :] = dzlp[0:1, :]
        return (d_rkv, d_lora, jnp.sum(dz * dif, axis=0, keepdims=True), jnp.sum(dz2 * difl, axis=0, keepdims=True), *dps)

    extra = [(p_rkv, _prev_rows_spec(tm, 3 * RW_W)), (p_lora, _prev_rows_spec(tm, LORA_W))]
    acc = [(p.shape, F32) for p in params]
    return _rows(fn, "rwkv_pre_bwd", T, tm, [p_rkv, p_lora, *cts], list(params), [(3 * RW_W, BF16), (LORA_W, BF16)], acc,
                 extra=extra, reverse=True, scratch=[pltpu.VMEM((8, 3 * RW_W), F32), pltpu.VMEM((8, LORA_W), F32)])


def _post_core(y, r, k2, v, g, lw, lb, rk):
    mu = _segsum64(y) * (1.0 / HEAD)
    yc = y - mu
    var = _segsum64(yc * yc) * (1.0 / HEAD)
    yn = yc * lax.rsqrt(var + GN_EPS) * lw + lb
    return (yn + _segsum64(r * k2 * rk) * v) * g


def _rwkv_post(y, r, k2, v, g, lw, lb, rk, tm=256):
    (o,) = _rows(lambda i, *a: (_post_core(*a),), "rwkv_post", y.shape[0], tm, [y, r, k2, v, g], [lw, lb, rk], [(RW_W, BF16)], [])
    return o


def _rwkv_post_bwd(y, r, k2, v, g, lw, lb, rk, do, tm=256):
    def fn(i, y, r, k2, v, g, do, lw, lb, rk):
        _, vjp = jax.vjp(_post_core, y, r, k2, v, g, lw, lb, rk)
        return vjp(do.astype(F32))

    return _rows(fn, "rwkv_post_bwd", y.shape[0], tm, [y, r, k2, v, g, do], [lw, lb, rk], [(RW_W, F32)] * 5, [((1, RW_W), F32)] * 3)


SCAN_L = 32


def _to_perm(x):
    T = x.shape[0]
    return x.reshape(T, RW_HEADS, HEAD).transpose(0, 2, 1).reshape(T, 8, 128)


def _from_perm(x):
    T = x.shape[0]
    return x.reshape(T, HEAD, RW_HEADS).transpose(0, 2, 1).reshape(T, RW_W)


def _as_tile(p):
    lane = lax.broadcasted_iota(jnp.int32, (8, 128), 1)
    return jnp.take_along_axis(p, (lane % 8) * 16 + lane // 8, axis=1)


def _as_perm(t):
    lane = lax.broadcasted_iota(jnp.int32, (8, 128), 1)
    return jnp.take_along_axis(t, (lane % 16) * 8 + lane // 16, axis=1)


def _tiles_to_perm(refs, L):
    for r in refs:
        for t in range(L):
            r[t] = _as_perm(r[t])


def _expander(srcs, tiles=()):
    s = lax.broadcasted_iota(jnp.int32, (8, 128), 0)
    lane = lax.broadcasted_iota(jnp.int32, (8, 128), 1)
    idx = 16 * s + lane // 8

    def expand(t, e_ref):
        for m, r in enumerate(srcs):
            for g in range(8):
                row = jnp.broadcast_to(r[t, pl.ds(g, 1), :], (8, 128))
                e_ref[m, g * 8:(g + 1) * 8, :] = jnp.take_along_axis(row, idx, axis=1)
        for k, r in enumerate(tiles):
            e_ref[len(srcs) + k, 0:8, :] = _as_tile(r[t])

    return expand


def _ck_a_to_b(ck):
    n = ck.shape[0]
    return ck.reshape(n, 8, 8, 8, RW_HEADS, 8).transpose(0, 2, 5, 1, 4, 3).reshape(n, HEAD, 8, 128)


def _bc(row):
    return jnp.broadcast_to(row, (8, 128))


def _rsum(x):
    return jnp.sum(x, axis=0, keepdims=True)


def _plus(acc, k, term):
    acc[k] = term if acc[k] is None else acc[k] + term


def _scan_fwd(xes, vi):
    T, L = vi.shape[0], SCAN_L
    nch = T // L

    def body(*refs):
        xr, (vi_ref, yi_ref, sa_ref, ck_ref, st_ref, e0, e1) = refs[:5], refs[5:]

        @pl.when(pl.program_id(0) == 0)
        def _():
            st_ref[...] = jnp.zeros_like(st_ref)

        ck_ref[0] = st_ref[...]
        expand = _expander(xr, [vi_ref])
        expand(0, e0)

        def step(t, e):
            tile = lambda m, jh: e[m, 8 * jh:8 * jh + 8, :]
            vb = [_bc(e[5, ih:ih + 1, :]) for ih in range(8)]
            acc = [None] * 8
            for jh in range(8):
                a = tile(0, jh)
                for ih in range(8):
                    _plus(acc, ih, st_ref[8 * jh + ih] * a)
            sab = []
            for ih in range(8):
                row = _rsum(acc[ih])
                sa_ref[t, ih:ih + 1, :] = row
                sab.append(_bc(row))
            yacc = [None] * 8
            for jh in range(8):
                w, B, k, r = tile(1, jh), tile(2, jh), tile(3, jh), tile(4, jh)
                for ih in range(8):
                    s = st_ref[8 * jh + ih] * w + B * sab[ih] + k * vb[ih]
                    st_ref[8 * jh + ih] = s
                    _plus(yacc, ih, s * r)
            for ih in range(8):
                yi_ref[t, ih:ih + 1, :] = _rsum(yacc[ih])

        def pair(p, carry):
            t = 2 * p
            expand(t + 1, e1)
            step(t, e0)
            expand(jnp.minimum(t + 2, L - 1), e0)
            step(t + 1, e1)
            return carry

        lax.fori_loop(0, L // 2, pair, 0)
        _tiles_to_perm([yi_ref, sa_ref], L)

    tile = pl.BlockSpec((L, 8, 128), lambda c: (c, 0, 0))
    return pl.pallas_call(
        body, name="rwkv_scan_fwd", grid=(nch,), in_specs=[tile] * 6,
        out_specs=[tile, tile, pl.BlockSpec((1, HEAD, 8, 128), lambda c: (c, 0, 0, 0))],
        out_shape=[jax.ShapeDtypeStruct((T, 8, 128), F32)] * 2 + [jax.ShapeDtypeStruct((nch, HEAD, 8, 128), F32)],
        scratch_shapes=[pltpu.VMEM((HEAD, 8, 128), F32)] + [pltpu.VMEM((6, HEAD, 128), F32)] * 2, compiler_params=_cp(("arbitrary",)),
    )(*xes, vi)


def _scan_bwd_a(xes, dyi):
    T, L = dyi.shape[0], SCAN_L
    nch = T // L

    def body(*refs):
        xr, (dy_ref, dsa_ref, dv_ref, g_ref, e0, e1) = refs[:5], refs[5:]

        @pl.when(pl.program_id(0) == 0)
        def _():
            g_ref[...] = jnp.zeros_like(g_ref)

        expand = _expander(xr, [dy_ref])
        expand(L - 1, e0)

        def step(t, e):
            tile = lambda m, jh: e[m, 8 * jh:8 * jh + 8, :]
            dyb = [_bc(e[5, ih:ih + 1, :]) for ih in range(8)]
            dsa, dv = [None] * 8, [None] * 8
            for jh in range(8):
                B, k, r = tile(2, jh), tile(3, jh), tile(4, jh)
                for ih in range(8):
                    g = g_ref[8 * jh + ih] + r * dyb[ih]
                    g_ref[8 * jh + ih] = g
                    _plus(dsa, ih, g * B)
                    _plus(dv, ih, g * k)
            dsab = []
            for ih in range(8):
                row = _rsum(dsa[ih])
                dsa_ref[t, ih:ih + 1, :] = row
                dsab.append(_bc(row))
                dv_ref[t, ih:ih + 1, :] = _rsum(dv[ih])
            for jh in range(8):
                A, w = tile(0, jh), tile(1, jh)
                for ih in range(8):
                    g_ref[8 * jh + ih] = g_ref[8 * jh + ih] * w + A * dsab[ih]

        def pair(p, carry):
            t = L - 1 - 2 * p
            expand(t - 1, e1)
            step(t, e0)
            expand(jnp.maximum(t - 2, 0), e0)
            step(t - 1, e1)
            return carry

        lax.fori_loop(0, L // 2, pair, 0)
        _tiles_to_perm([dsa_ref, dv_ref], L)

    tile = pl.BlockSpec((L, 8, 128), lambda c: (nch - 1 - c, 0, 0))
    return pl.pallas_call(
        body, name="rwkv_scan_bwd_a", grid=(nch,), in_specs=[tile] * 6, out_specs=[tile, tile],
        out_shape=[jax.ShapeDtypeStruct((T, 8, 128), F32)] * 2,
        scratch_shapes=[pltpu.VMEM((HEAD, 8, 128), F32)] + [pltpu.VMEM((6, HEAD, 128), F32)] * 2, compiler_params=_cp(("arbitrary",)),
    )(*xes, dyi)


def _scan_bwd_b(xts, ies, ckb):
    T, L = xts[0].shape[0], SCAN_L
    nch = T // L

    def body(*refs):
        xr, er, ck_ref, dj, (hist, g_ref, e0, e1) = refs[:5], refs[5:9], refs[9], refs[10:15], refs[15:]

        @pl.when(pl.program_id(0) == 0)
        def _():
            g_ref[...] = jnp.zeros_like(g_ref)

        hist[0] = ck_ref[0]
        expand_vs = _expander(er[:2], [xr[1], xr[2], xr[3]])
        expand = _expander(er, [xr[0], xr[1], xr[4]])
        expand_vs(0, e0)

        def fstep(t, e_ref):
            w, B, k = e_ref[2, 0:8, :], e_ref[3, 0:8, :], e_ref[4, 0:8, :]
            row = lambda m, i: jnp.broadcast_to(e_ref[m, pl.ds(i, 1), :], (8, 128))
            for i in range(HEAD):
                hist[t + 1, i] = hist[t, i] * w + row(1, i) * B + row(0, i) * k

        def fpair(p, carry):
            t = 2 * p
            expand_vs(t + 1, e1)
            fstep(t, e0)
            expand_vs(jnp.minimum(t + 2, L - 1), e0)
            fstep(t + 1, e1)
            return carry

        lax.fori_loop(0, L // 2, fpair, 0)
        expand(L - 1, e0)

        def bstep(t, e_ref):
            A, w, r = e_ref[4, 0:8, :], e_ref[5, 0:8, :], e_ref[6, 0:8, :]
            row = lambda m, i: jnp.broadcast_to(e_ref[m, pl.ds(i, 1), :], (8, 128))
            acc = [_Acc() for _ in range(5)]
            for i in range(HEAD):
                dy_i, dsa_i = row(2, i), row(3, i)
                g = g_ref[i] + dy_i * r
                sp = hist[t, i]
                acc[4].add(i, hist[t + 1, i] * dy_i)
                acc[1].add(i, g * sp)
                acc[2].add(i, g * row(1, i))
                acc[3].add(i, g * row(0, i))
                acc[0].add(i, sp * dsa_i)
                g_ref[i] = g * w + dsa_i * A
            for m in range(5):
                dj[m][t] = acc[m].total()

        def bpair(p, carry):
            t = L - 1 - 2 * p
            expand(t - 1, e1)
            bstep(t, e0)
            expand(jnp.maximum(t - 2, 0), e0)
            bstep(t - 1, e1)
            return carry

        lax.fori_loop(0, L // 2, bpair, 0)
        _tiles_to_perm(dj, L)

    tile = pl.BlockSpec((L, 8, 128), lambda c: (nch - 1 - c, 0, 0))
    return pl.pallas_call(
        body, name="rwkv_scan_bwd_b", grid=(nch,),
        in_specs=[tile] * 9 + [pl.BlockSpec((1, HEAD, 8, 128), lambda c: (nch - 1 - c, 0, 0, 0))],
        out_specs=[tile] * 5, out_shape=[jax.ShapeDtypeStruct((T, 8, 128), F32)] * 5,
        scratch_shapes=[pltpu.VMEM((L + 1, HEAD, 8, 128), F32), pltpu.VMEM((HEAD, 8, 128), F32)] + [pltpu.VMEM((7, HEAD, 128), F32)] * 2,
        compiler_params=_cp(("arbitrary",)),
    )(*xts, *ies, ckb)


SWA_COLS = SWA_W + 2 * KV_W
BLK = 128


def _swa_core(n, k2a, k2b, vla, vra, vlb, vrb, sinks, *qps):
    iq = lax.broadcasted_iota(jnp.int32, (BLK, 2 * BLK), 0)
    ik = lax.broadcasted_iota(jnp.int32, (BLK, 2 * BLK), 1)
    diff = BLK + iq - ik
    valid = (diff >= 0) & (diff < WINDOW) & ((n > 0) | (ik >= BLK))
    lane = lax.broadcasted_iota(jnp.int32, (BLK, 128), 1)
    lane1 = lax.broadcasted_iota(jnp.int32, (1, 128), 1)
    nt = (((1,), (1,)), ((), ()))
    outs = []
    for pp in range(8):
        k2, vl, vr = (k2a, vla, vra) if pp < 4 else (k2b, vlb, vrb)
        qp = qps[pp]
        o = None
        for half, vv in ((0, vl), (1, vr)):
            qh = jnp.where((lane >= HEAD) == (half == 1), qp, 0.0).astype(BF16)
            s = lax.dot_general(qh, k2.astype(BF16), nt, preferred_element_type=F32) * (HEAD ** -0.5)
            s = jnp.where(valid, s, NEG_INF)
            sink = jnp.sum(jnp.where(lane1 == 2 * pp + half, sinks, 0.0), axis=1, keepdims=True)
            m = jnp.maximum(jnp.max(s, axis=1, keepdims=True), sink)
            p = jnp.exp(s - m)
            den = jnp.sum(p, axis=1, keepdims=True) + jnp.exp(sink - m)
            oh = jnp.dot((p / den).astype(BF16), vv.astype(BF16), preferred_element_type=F32)
            o = oh if o is None else o + oh
        outs.append(o)
    return jnp.concatenate(outs, axis=1)


def _swa_prep(pc, pp, b, cq, sq, ckc, skc, ckp, skp):
    zc, zp = pc + b, pp + b
    qr = zc[:, :SWA_W] * cq + _swap32(zc[:, :SWA_W]) * sq
    kc, kp = zc[:, SWA_W:SWA_W + KV_W], zp[:, SWA_W:SWA_W + KV_W]
    kb = jnp.concatenate([kp * ckp + _swap32(kp) * skp, kc * ckc + _swap32(kc) * skc], axis=0)
    vb = jnp.concatenate([zp[:, SWA_W + KV_W:], zc[:, SWA_W + KV_W:]], axis=0)
    lane = lax.broadcasted_iota(jnp.int32, kb.shape, 1)
    left = lane < HEAD
    kbr, vbr = pltpu.roll(kb, HEAD, 1), pltpu.roll(vb, HEAD, 1)
    return (jnp.where(left, kb, kbr), jnp.where(left, kbr, kb), jnp.where(left, vb, 0.0), jnp.where(left, 0.0, vbr),
            jnp.where(left, vbr, 0.0), jnp.where(left, 0.0, vb)), [qr[:, q * 128:(q + 1) * 128] for q in range(8)]


def _swa_specs(T, tabs_q, tabs_k):
    cur = lambda c: pl.BlockSpec((BLK, c), lambda n: (n, 0))
    prev = lambda c: pl.BlockSpec((BLK, c), lambda n: (jnp.maximum(n - 1, 0), 0))
    return cur, prev


def _swa_fwd(p_swa, b, sinks, cq, sq, ck, sk):
    T = p_swa.shape[0]
    cur, prev = _swa_specs(T, None, None)

    def body(pc, pp, b_ref, s_ref, cq_r, sq_r, ckc, skc, ckp, skp, o_ref):
        ops, qps = _swa_prep(pc[...], pp[...], b_ref[...], cq_r[...], sq_r[...], ckc[...], skc[...], ckp[...], skp[...])
        o_ref[...] = _swa_core(pl.program_id(0), *ops, s_ref[...], *qps).astype(o_ref.dtype)

    full = lambda a: pl.BlockSpec(a.shape, lambda n: (0, 0))
    return pl.pallas_call(
        body, name="swa_fwd", grid=(T // BLK,),
        in_specs=[cur(SWA_COLS), prev(SWA_COLS), full(b), full(sinks), cur(SWA_W), cur(SWA_W), cur(KV_W), cur(KV_W), prev(KV_W), prev(KV_W)],
        out_specs=cur(SWA_W), out_shape=jax.ShapeDtypeStruct((T, SWA_W), BF16), compiler_params=_cp(("arbitrary",)),
    )(p_swa, p_swa, b, sinks, cq, sq, ck, sk, ck, sk)


def _swa_bwd(p_swa, b, sinks, cq, sq, ck, sk, do):
    T = p_swa.shape[0]
    nb = T // BLK
    cur = lambda c: pl.BlockSpec((BLK, c), lambda s: (nb - 1 - s, 0))
    prev = lambda c: pl.BlockSpec((BLK, c), lambda s: (jnp.maximum(nb - 2 - s, 0), 0))

    def body(pc, pp, b_ref, s_ref, cq_r, sq_r, ckc, skc, ckp, skp, do_ref, dcur, db, dsk, carry):
        step = pl.program_id(0)
        n = nb - 1 - step

        @pl.when(step == 0)
        def _():
            carry[...] = jnp.zeros_like(carry)
            db[...] = jnp.zeros_like(db)
            dsk[...] = jnp.zeros_like(dsk)

        ops, qps = _swa_prep(pc[...], pp[...], b_ref[...], cq_r[...], sq_r[...], ckc[...], skc[...], ckp[...], skp[...])
        _, vjp = jax.vjp(functools.partial(_swa_core, n), *ops, s_ref[...], *qps)
        dk2a, dk2b, dvla, dvra, dvlb, dvrb, dsinks, *dqps = vjp(do_ref[...].astype(F32))
        dqr = jnp.concatenate(dqps, axis=1)
        lane = lax.broadcasted_iota(jnp.int32, dk2a.shape, 1)
        left = lane < HEAD
        dkb = jnp.where(left, dk2a + pltpu.roll(dk2a, HEAD, 1), dk2b + pltpu.roll(dk2b, HEAD, 1))
        dvb = jnp.where(left, dvla + pltpu.roll(dvra, HEAD, 1), pltpu.roll(dvlb, HEAD, 1) + dvrb)
        dq = dqr * cq_r[...] + _swap32(dqr * sq_r[...])
        dkp, dkc = dkb[:BLK], dkb[BLK:]
        dkp = dkp * ckp[...] + _swap32(dkp * skp[...])
        dkc = dkc * ckc[...] + _swap32(dkc * skc[...])
        dc = jnp.concatenate([dq, jnp.concatenate([dkc, dvb[BLK:]], axis=1) + carry[...]], axis=1)
        carry[...] = jnp.concatenate([dkp, dvb[:BLK]], axis=1)
        dcur[...] = dc.astype(dcur.dtype)
        db[...] += jnp.sum(dc, axis=0, keepdims=True)
        dsk[...] += dsinks

    full = lambda a: pl.BlockSpec(a.shape, lambda s: (0, 0))
    return pl.pallas_call(
        body, name="swa_bwd", grid=(nb,),
        in_specs=[cur(SWA_COLS), prev(SWA_COLS), full(b), full(sinks), cur(SWA_W), cur(SWA_W), cur(KV_W), cur(KV_W), prev(KV_W), prev(KV_W),
                  cur(SWA_W)],
        out_specs=[cur(SWA_COLS), full(b), full(sinks)],
        out_shape=[jax.ShapeDtypeStruct((T, SWA_COLS), BF16), jax.ShapeDtypeStruct(b.shape, F32), jax.ShapeDtypeStruct(sinks.shape, F32)],
        scratch_shapes=[pltpu.VMEM((BLK, 2 * KV_W), F32)], compiler_params=_cp(("arbitrary",)),
    )(p_swa, p_swa, b, sinks, cq, sq, ck, sk, ck, sk, do)


def _rope_tables(T):
    inv = 10000.0 ** (-jnp.arange(0, HEAD, 2, dtype=F32) / HEAD)
    ang = jnp.arange(T, dtype=F32)[:, None] * inv[None, :]
    c = jnp.concatenate([jnp.cos(ang), jnp.cos(ang)], axis=1)
    s = jnp.concatenate([-jnp.sin(ang), jnp.sin(ang)], axis=1)
    return jnp.tile(c, (1, 16)), jnp.tile(s, (1, 16)), jnp.tile(c, (1, 2)), jnp.tile(s, (1, 2))


def _xattn_core(*qkv):
    outs = []
    for h in range(XH):
        qh, kh, vh = qkv[h], qkv[XH + h], qkv[2 * XH + h]
        s = lax.dot_general(qh.astype(BF16), kh.astype(BF16), (((1,), (1,)), ((), ())), preferred_element_type=F32) * (XHD ** -0.5)
        p = jnp.exp(s - jnp.max(s, axis=1, keepdims=True))
        p = p / jnp.sum(p, axis=1, keepdims=True)
        outs.append(jnp.dot(p.astype(BF16), vh.astype(BF16), preferred_element_type=F32))
    return jnp.concatenate(outs, axis=1)


def _xattn_split(q, kv):
    return [q[:, h * XHD:(h + 1) * XHD] for h in range(XH)] + [kv[:, h * XHD:(h + 1) * XHD] for h in range(2 * XH)]


def _xattn_fwd(q, kv, tm=256):
    (o,) = _rows(lambda i, q, kv: (_xattn_core(*_xattn_split(q, kv)),), "xattn_fwd", q.shape[0], tm, [q], [kv], [(q.shape[1], BF16)], [])
    return o


def _xattn_bwd(q, kv, do, tm=256):
    def fn(i, q, do, kv):
        _, vjp = jax.vjp(_xattn_core, *_xattn_split(q, kv))
        d = vjp(do.astype(F32))
        return jnp.concatenate(d[:XH], axis=1), jnp.concatenate(d[XH:], axis=1)

    return _rows(fn, "xattn_bwd", q.shape[0], tm, [q, do], [kv], [(q.shape[1], BF16)], [(kv.shape, F32)])


def _loss_head(x, g, tgt, tm=256):
    D = x.shape[1]

    def fn(i, x, tgt, g):
        y, vjp = jax.vjp(_rms, x, g)
        err = y - tgt
        dx, dg = vjp(err * (1.0 / D))
        part = 0.5 / D * jnp.sum(jnp.sum(err * err, axis=1, keepdims=True), axis=0, keepdims=True)
        return dx, jnp.broadcast_to(part, (1, 128)), dg

    return _rows(fn, "loss_head", x.shape[0], tm, [x, tgt], [g], [(D, F32)], [((1, 128), F32), ((1, D), F32)])


def _local_step(x, mem, tgt, get_w, P, put_g):
    T = x.shape[0]
    W = dict(get_w("f1", None))

    def f1_down(after):
        W.update(get_w("f1d", after))
        return W["f1_down"]

    x1, s1 = _ffn_fwd(x, P["f1_norm"], W["f1_gate"], W["f1_up"], f1_down, "f1")

    W.update(get_w("mix", x1))
    h2 = _rms_fwd(x1, P["mix_norm"], "mix_norm")
    w_rkv, w_lora, w_swa = W["w_inT"][:3 * RW_W], W["w_inT"][3 * RW_W:SHIFT_COLS], W["w_inT"][SHIFT_COLS:]
    p_rkv = _mm(h2, w_rkv, "nt", "in_rkv", after=W.get("_after"))
    p_lora = _mm(h2, w_lora, "nt", "in_lora")
    p_swa = _mm(h2, w_swa, "nt", "in_swa")
    w_da = jnp.concatenate([W["rw_decay_up"], W["rw_aaa_up"]], axis=0)
    pre_params = (P["rw_mu"][:, :3 * RW_W], P["rw_mu"][:, 3 * RW_W:], P["rw_w0"], P["rw_a0"], P["rw_k_k"], P["rw_k_a"], w_da,
                  W["rw_gate_up"])
    r, decay, k2, v, an, bn, g = _rwkv_pre(p_rkv, p_lora, pre_params)
    scan_vecs = (an, decay, bn, k2, r)
    xes = [_to_perm(a) for a in scan_vecs]
    v_p = _to_perm(v)
    yi, sai, ck = _scan_fwd(xes, v_p)
    y_scan = _from_perm(yi)
    y_rw = _rwkv_post(y_scan, r, k2, v, g, P["rw_lnx_w"], P["rw_lnx_b"], P["rw_r_k"])
    cq, sq, ckt, skt = _rope_tables(T)
    y_swa = _swa_fwd(p_swa, P["b_in_attn"], P["attn_sinks"], cq, sq, ckt, skt)
    ycat = jnp.concatenate([y_rw, y_swa], axis=1)
    W.update(get_w("out", ycat))
    x2 = _mm(ycat, W["w_out"], "nn", "out_proj", res=x1, bias=P["b_out"])

    W.update(get_w("xattn", x2))
    hx = _rms_fwd(x2, P["xa_norm"], "xa_norm")
    mn = _rms_fwd(mem, P["mem_norm"], "mem_norm")
    q = _mm(hx, W["w_xq"], "nn", "xq", out_dtype=BF16)
    kv = _mm(mn, W["w_xkv"], "nn", "xkv", out_dtype=BF16)
    o = _xattn_fwd(q, kv)
    x3 = _mm(o, W["w_xo"], "nn", "xo", res=x2)

    W.update(get_w("f2", x3))
    x4, s2 = _ffn_fwd(x3, P["f2_norm"], W["f2_gate"], W["f2_up"], W["f2_down"], "f2")
    dx4, loss_part, d_final = _loss_head(x4, P["final_norm"], tgt)

    gs = {"final_norm": d_final}
    dx3, gs["f2_norm"] = _ffn_bwd(x3, P["f2_norm"], W["f2_gate"], W["f2_up"], W["f2_down"], s2, dx4, "f2", put_g)

    do = _mm(dx3, W["w_xo"], "nt", "xo_do", out_dtype=BF16)
    dw_xo = _mm(o, dx3, "tn", "xo_dw", out_dtype=BF16)
    dq, dkv = _xattn_bwd(q, kv, do)
    dw_xq = _mm(hx, dq, "tn", "xq_dw", out_dtype=BF16)
    dw_xkv = _mm(mn, dkv, "tn", "xkv_dw", out_dtype=BF16)
    sent = put_g("xattn", {"w_xq": dw_xq, "w_xkv": dw_xkv, "w_xo": dw_xo})
    dhx = _mm(dq, W["w_xq"], "nt", "xq_dh", after=sent)
    dmn = _mm(dkv, W["w_xkv"], "nt", "xkv_dmn")
    _, gs["mem_norm"], _ = _rms_bwd(mem, P["mem_norm"], dmn, jnp.zeros_like(mem), "mem_norm_bwd")
    dx2, gs["xa_norm"], gs["b_out"] = _rms_bwd(x2, P["xa_norm"], dhx, dx3, "xa_norm_bwd")

    dycat = _mm(dx2, W["w_out"], "nt", "out_dy")
    dw_out = _mm(ycat, dx2, "tn", "out_dw", out_dtype=BF16)
    dp_swa, gs["b_in_attn"], gs["attn_sinks"] = _swa_bwd(p_swa, P["b_in_attn"], P["attn_sinks"], cq, sq, ckt, skt, dycat[:, RW_W:])
    dy_scan, dr_b, dk2_b, dv_b, dg, gs["rw_lnx_w"], gs["rw_lnx_b"], gs["rw_r_k"] = _rwkv_post_bwd(
        y_scan, r, k2, v, g, P["rw_lnx_w"], P["rw_lnx_b"], P["rw_r_k"], dycat[:, :RW_W])
    dy_p = _to_perm(dy_scan)
    dsai, dvi = _scan_bwd_a(xes, dy_p)
    dj = _scan_bwd_b(xes, [v_p, sai, dy_p, dsai], _ck_a_to_b(ck))
    dan, ddecay, dbn, dk2_s, dr_s = (_from_perm(d) for d in dj)
    cts = (dr_s, ddecay, dk2_s, _from_perm(dvi), dan, dbn, dg, dr_b, dk2_b, dv_b)
    dp_rkv, dp_lora, dmu, dmul, gs["rw_w0"], gs["rw_a0"], gs["rw_k_k"], gs["rw_k_a"], dw_da, gs["rw_gate_up"] = _rwkv_pre_bwd(
        p_rkv, p_lora, pre_params, cts)
    gs["rw_mu"] = jnp.concatenate([dmu, dmul], axis=1)
    gs["rw_decay_up"], gs["rw_aaa_up"] = dw_da[:DECAY_LORA], dw_da[DECAY_LORA:]
    dw_inT = jnp.concatenate([_mm(dp_rkv, h2, "tn", "in_dw_rkv"), _mm(dp_lora, h2, "tn", "in_dw_lora"),
                              _mm(dp_swa, h2, "tn", "in_dw_swa")], axis=0)
    sent = put_g("mix", {"w_in": dw_inT, "w_out": dw_out})
    dh2 = _mm(dp_rkv, w_rkv, "nn", "in_dh_rkv", after=sent)
    dh2 = _mm(dp_lora, w_lora, "nn", "in_dh_lora", res=dh2)
    dh2 = _mm(dp_swa, w_swa, "nn", "in_dh_swa", res=dh2)
    dx1, gs["mix_norm"], _ = _rms_bwd(x1, P["mix_norm"], dh2, dx2, "mix_norm_bwd")

    dx0, gs["f1_norm"] = _ffn_bwd(x, P["f1_norm"], W["f1_gate"], W["f1_up"], W["f1_down"], s1, dx1, "f1", put_g)
    return loss_part, dx0, gs


_ANY = pl.BlockSpec(memory_space=pl.ANY)
_OTHER_CHIPS = ((1, 0), (0, 1), (1, 1))


def _mesh_pos():
    return lax.axis_index("x"), lax.axis_index("y"), lax.axis_index("c")


def _slot(ref, kind, s, rows, cols):
    if kind == "row":
        return ref.at[pl.ds(pl.multiple_of(s * rows, 8), rows), :]
    return ref.at[:, pl.ds(pl.multiple_of(s * cols, 128), cols)]


_HBM = pl.BlockSpec(memory_space=pltpu.HBM)
_SEMS = pl.BlockSpec(memory_space=pltpu.SEMAPHORE)
_SPLIT = dict(compiler_params=pltpu.CompilerParams(has_side_effects=pltpu.SideEffectType.DATAFLOW_SIDE_EFFECTING))


def _in_hbm(a):
    return pltpu.with_memory_space_constraint(a, pltpu.HBM)


def _full_shape(s, kind):
    return (4 * s.shape[0], s.shape[1]) if kind == "row" else (s.shape[0], 4 * s.shape[1])


def _half(ref, shape, h):
    rows, cols = shape
    if rows % 32 == 0:
        return ref.at[pl.ds(pl.multiple_of(h * (rows // 2), 16), rows // 2), :]
    assert cols % 256 == 0, shape
    return ref.at[:, pl.ds(pl.multiple_of(h * (cols // 2), 128), cols // 2)]


def _half_shape(shape):
    rows, cols = shape
    return (rows // 2, cols) if rows % 32 == 0 else (rows, cols // 2)


def _streams(src, dst, shape, c):
    hs = _half_shape(shape)
    s, d = _half(src, shape, c), _half(dst, shape, c)
    return [(_half(s, hs, q), _half(d, hs, q)) for q in range(2)]


def _swap_halves(name, fulls, shard_shapes, kinds):
    n = len(fulls)

    def body(*refs):
        out, send, recv = refs[n:2 * n], refs[2 * n], refs[2 * n + 1]
        x, y, c = _mesh_pos()
        sent = []
        for i in range(n):
            for r, (dx, dy) in enumerate(_OTHER_CHIPS):
                theirs = _slot(out[i], kinds[i], 2 * ((x + dx) % 2) + (y + dy) % 2, *shard_shapes[i])
                have = _half(theirs, shard_shapes[i], c)
                rc = pltpu.make_async_remote_copy(have, have, send.at[3 * i + r], recv.at[3 * i + r], device_id=(x, y, 1 - c),
                                                  device_id_type=MESH)
                rc.start()
                sent.append(rc)
        for i in range(n):
            for r, (dx, dy) in enumerate(_OTHER_CHIPS):
                theirs = _slot(out[i], kinds[i], 2 * ((x + dx) % 2) + (y + dy) % 2, *shard_shapes[i])
                need = _half(theirs, shard_shapes[i], 1 - c)
                pltpu.make_async_remote_copy(need, need, send.at[3 * i + r], recv.at[3 * i + r], device_id=(x, y, c),
                                             device_id_type=MESH).wait_recv()
        for rc in sent:
            rc.wait_send()

    return pl.pallas_call(
        body, name=name, in_specs=[_ANY] * n, out_specs=[_ANY] * n, out_shape=[jax.ShapeDtypeStruct(f.shape, f.dtype) for f in fulls],
        input_output_aliases={i: i for i in range(n)},
        scratch_shapes=[pltpu.SemaphoreType.DMA((3 * n,)), pltpu.SemaphoreType.DMA((3 * n,))],
    )(*fulls)


def _gather_start(name, shards, kinds, groups, after=None):
    n, ng = len(shards), len(groups)
    lands = [_in_hbm(lax.empty(_full_shape(s, k), s.dtype)) for s, k in zip(shards, kinds)]
    n_in = 2 * n + (after is not None)

    def body(*refs):
        src, land, sems, token = refs[:n], refs[n:2 * n], refs[n_in:n_in + 3 * ng], refs[-1]
        x, y, c = _mesh_pos()
        me = 2 * x + y
        for gi, idxs in enumerate(groups):
            send, recv, own = sems[3 * gi:3 * gi + 3]
            for k, i in enumerate(idxs):
                mine = _slot(land[i], kinds[i], me, *src[i].shape)
                for r, (dx, dy) in enumerate(_OTHER_CHIPS):
                    for q, (s, d) in enumerate(_streams(src[i], mine, src[i].shape, c)):
                        pltpu.make_async_remote_copy(s, d, send.at[6 * k + 2 * r + q], recv.at[6 * k + 2 * r + q],
                                                     device_id=((x + dx) % 2, (y + dy) % 2, c), device_id_type=MESH).start()
                pltpu.make_async_copy(src[i], mine, own.at[k]).start()
        token[...] = jnp.zeros_like(token)

    sem_shapes = [pltpu.SemaphoreType.DMA((w * len(g),)) for g in groups for w in (6, 6, 1)]
    thru = [pltpu.HBM(a.shape, a.dtype) for a in (*shards, *lands)]
    res = pl.pallas_call(
        body, name=name, in_specs=[_HBM] * (2 * n) + [_ANY] * (after is not None),
        out_specs=[_SEMS] * (3 * ng) + [_HBM] * (2 * n) + [pl.BlockSpec(memory_space=pltpu.VMEM)],
        out_shape=sem_shapes + thru + [jax.ShapeDtypeStruct((8, 128), F32)],
        input_output_aliases={i: 3 * ng + i for i in range(2 * n)}, **_SPLIT,
    )(*[_in_hbm(s) for s in shards], *lands, *([] if after is None else [after]))
    return res[:3 * ng], res[3 * ng:3 * ng + n], res[3 * ng + n:3 * ng + 2 * n], res[-1]


def _gather_wait(name, sems, shards, lands, kinds, after):
    m = len(shards)

    def body(*refs):
        src, land, (send, recv, own) = refs[:m], refs[m:2 * m], refs[2 * m:2 * m + 3]
        x, y, c = _mesh_pos()
        me = 2 * x + y
        for k in range(m):
            mine = _slot(land[k], kinds[k], me, *src[k].shape)
            for r in range(3):
                for q, (s, d) in enumerate(_streams(src[k], mine, src[k].shape, c)):
                    cp = pltpu.make_async_remote_copy(s, d, send.at[6 * k + 2 * r + q], recv.at[6 * k + 2 * r + q], device_id=(x, y, c),
                                                      device_id_type=MESH)
                    cp.wait_send()
                    cp.wait_recv()
            pltpu.make_async_copy(src[k], mine, own.at[k]).wait()

    thru = [pltpu.HBM(a.shape, a.dtype) for a in (*shards, *lands)]
    res = pl.pallas_call(
        body, name=name, in_specs=[_HBM] * (2 * m) + [_SEMS] * 3 + [pl.BlockSpec(memory_space=pl.ANY)],
        out_specs=[_HBM] * (2 * m), out_shape=thru, input_output_aliases={i: i for i in range(2 * m)}, **_SPLIT,
    )(*shards, *lands, *sems, after)
    return res[m:]


def _scatter_start(name, grads, kinds):
    m = len(grads)
    shard_shape = [(g.shape[0] // 4, g.shape[1]) if k == "row" else (g.shape[0], g.shape[1] // 4) for g, k in zip(grads, kinds)]
    lands = [_in_hbm(lax.empty((4, *s), g.dtype)) for s, g in zip(shard_shape, grads)]

    def body(*refs):
        src, land, (send, recv, own) = refs[:m], refs[m:2 * m], refs[2 * m:2 * m + 3]
        x, y, c = _mesh_pos()
        me = 2 * x + y
        for k in range(m):
            for r, (dx, dy) in enumerate(_OTHER_CHIPS):
                tx, ty = (x + dx) % 2, (y + dy) % 2
                pltpu.make_async_remote_copy(_slot(src[k], kinds[k], 2 * tx + ty, *shard_shape[k]), land[k].at[me],
                                             send.at[3 * k + r], recv.at[3 * k + r], device_id=(tx, ty, c), device_id_type=MESH).start()
            pltpu.make_async_copy(_slot(src[k], kinds[k], me, *shard_shape[k]), land[k].at[me], own.at[k]).start()
        refs[-1][...] = jnp.zeros_like(refs[-1])

    thru = [pltpu.HBM(a.shape, a.dtype) for a in (*grads, *lands)]
    res = pl.pallas_call(
        body, name=name, in_specs=[_HBM] * (2 * m),
        out_specs=[_SEMS] * 3 + [_HBM] * (2 * m) + [pl.BlockSpec(memory_space=pltpu.VMEM)],
        out_shape=[pltpu.SemaphoreType.DMA((3 * m,))] * 2 + [pltpu.SemaphoreType.DMA((m,))] + thru + [jax.ShapeDtypeStruct((8, 128), F32)],
        input_output_aliases={i: 3 + i for i in range(2 * m)}, **_SPLIT,
    )(*[_in_hbm(g) for g in grads], *lands)
    return res[:3], res[3:3 + m], res[3 + m:3 + 2 * m], res[-1]


def _scatter_wait(name, sems, grads, lands, kinds, after):
    m = len(grads)

    def body(*refs):
        src, land, (send, recv, own) = refs[:m], refs[m:2 * m], refs[2 * m:2 * m + 3]
        x, y, c = _mesh_pos()
        me = 2 * x + y
        for k in range(m):
            mine = _slot(src[k], kinds[k], me, *land[k].shape[1:])
            for r in range(3):
                cp = pltpu.make_async_remote_copy(mine, land[k].at[me], send.at[3 * k + r], recv.at[3 * k + r],
                                                  device_id=(x, y, c), device_id_type=MESH)
                cp.wait_send()
                cp.wait_recv()
            pltpu.make_async_copy(mine, land[k].at[me], own.at[k]).wait()

    thru = [pltpu.HBM(a.shape, a.dtype) for a in (*grads, *lands)]
    res = pl.pallas_call(
        body, name=name, in_specs=[_HBM] * (2 * m) + [_SEMS] * 3 + [pl.BlockSpec(memory_space=pl.ANY)],
        out_specs=[_HBM] * (2 * m), out_shape=thru, input_output_aliases={i: i for i in range(2 * m)}, **_SPLIT,
    )(*grads, *lands, *sems, after)
    return res[m:]


def _swap_with_sibling(arrs, name):
    n = len(arrs)

    def body(*refs):
        ins, outs = refs[:n], refs[n:2 * n]
        send, recv = refs[2 * n:]
        x, y, c = _mesh_pos()
        copies = []
        for i in range(n):
            rc = pltpu.make_async_remote_copy(ins[i], outs[i], send.at[i], recv.at[i], device_id=(x, y, 1 - c), device_id_type=MESH)
            rc.start()
            copies.append(rc)
        for rc in copies:
            rc.wait()

    return pl.pallas_call(
        body, name=name, in_specs=[_ANY] * n, out_specs=[_ANY] * n,
        out_shape=[jax.ShapeDtypeStruct(a.shape, a.dtype) for a in arrs],
        scratch_shapes=[pltpu.SemaphoreType.DMA((n,)), pltpu.SemaphoreType.DMA((n,))],
    )(*arrs)


def _small_start(pack, after):
    land = _in_hbm(lax.empty((8, *pack.shape), pack.dtype))

    def body(in_ref, land_ref, after_ref, send, recv, own, in_thru, land_thru, token):
        x, y, c = _mesh_pos()
        me = 4 * x + 2 * y + c
        for r in range(1, 8):
            dx, dy, dc = r // 4, (r // 2) % 2, r % 2
            pltpu.make_async_remote_copy(in_ref, land_ref.at[me], send.at[r - 1], recv.at[r - 1],
                                         device_id=((x + dx) % 2, (y + dy) % 2, (c + dc) % 2), device_id_type=MESH).start()
        pltpu.make_async_copy(in_ref, land_ref.at[me], own.at[0]).start()
        token[...] = jnp.zeros_like(token)

    res = pl.pallas_call(
        body, name="small_start", in_specs=[_HBM, _HBM, _ANY],
        out_specs=[_SEMS] * 3 + [_HBM, _HBM, pl.BlockSpec(memory_space=pltpu.VMEM)],
        out_shape=[pltpu.SemaphoreType.DMA((7,)), pltpu.SemaphoreType.DMA((7,)), pltpu.SemaphoreType.DMA((1,)),
                   pltpu.HBM(pack.shape, pack.dtype), pltpu.HBM(land.shape, land.dtype), jax.ShapeDtypeStruct((8, 128), F32)],
        input_output_aliases={0: 3, 1: 4}, **_SPLIT,
    )(_in_hbm(pack), land, after)
    return res[:3], res[3], res[4], res[5]


def _small_wait(sems, pack, land, after):
    def body(in_ref, land_ref, send, recv, own, after_ref, in_dead, got):
        x, y, c = _mesh_pos()
        me = 4 * x + 2 * y + c
        for r in range(1, 8):
            cp = pltpu.make_async_remote_copy(in_ref, land_ref.at[me], send.at[r - 1], recv.at[r - 1], device_id=(x, y, c),
                                              device_id_type=MESH)
            cp.wait_send()
            cp.wait_recv()
        pltpu.make_async_copy(in_ref, land_ref.at[me], own.at[0]).wait()

    res = pl.pallas_call(
        body, name="small_wait", in_specs=[_HBM, _HBM] + [_SEMS] * 3 + [_ANY], out_specs=[_HBM, _HBM],
        out_shape=[pltpu.HBM(pack.shape, pack.dtype), pltpu.HBM(land.shape, land.dtype)], input_output_aliases={0: 0, 1: 1}, **_SPLIT,
    )(pack, land, *sems, after)
    return res[1]


def _row_tile(R, dtype, target=256):
    mult = 8 * 4 // jnp.dtype(dtype).itemsize
    best = R
    for t in range(mult, min(R, target) + 1, mult):
        if R % t == 0:
            best = t
    return best


def _sum_slots(stack, name, out_dtype=F32):
    k, R, C = stack.shape
    tr = _row_tile(R, stack.dtype)

    def body(s_ref, o_ref):
        acc = s_ref[0].astype(F32)
        for j in range(1, k):
            acc = acc + s_ref[j].astype(F32)
        o_ref[...] = acc.astype(out_dtype)

    return pl.pallas_call(
        body, name=name, grid=(R // tr,), in_specs=[pl.BlockSpec((k, tr, C), lambda i: (0, i, 0))],
        out_specs=pl.BlockSpec((tr, C), lambda i: (i, 0)), out_shape=jax.ShapeDtypeStruct((R, C), out_dtype),
        compiler_params=_cp(("parallel",)),
    )(stack)


def _adamw(w, m, v, ga, gb, name, after=None):
    R, C = w.shape
    tr = _row_tile(R, F32, 128)
    gs = [ga] if gb is None else [ga, gb]
    extra = [] if after is None else [after]

    def body(*refs):
        w_ref, m_ref, v_ref = refs[:3]
        g = refs[3][...].astype(F32)
        if gb is not None:
            g = g + refs[4][...].astype(F32)
        g_ref, d_ref, nm_ref, nv_ref = refs[-4:]
        nm = ADAM_B1 * m_ref[...] + (1.0 - ADAM_B1) * g
        nv = ADAM_B2 * v_ref[...] + (1.0 - ADAM_B2) * (g * g)
        m_hat = nm / (1.0 - ADAM_B1 ** ADAM_STEP)
        v_hat = nv / (1.0 - ADAM_B2 ** ADAM_STEP)
        g_ref[...] = g
        d_ref[...] = -ADAM_LR * (m_hat / (jnp.sqrt(v_hat) + ADAM_EPS) + ADAM_WD * w_ref[...])
        nm_ref[...] = nm
        nv_ref[...] = nv

    spec = pl.BlockSpec((tr, C), lambda i: (i, 0))
    return pl.pallas_call(
        body, name=name, grid=(R // tr,), in_specs=[spec] * (3 + len(gs)) + [_ANY] * len(extra), out_specs=[spec] * 4,
        out_shape=[jax.ShapeDtypeStruct((R, C), F32)] * 4, compiler_params=_cp(("parallel",)),
    )(w, m, v, *gs, *extra)


def _pack(arrs):
    rows = []
    for a in arrs:
        flat = a.reshape(-1)
        rows.append(jnp.pad(flat, (0, -flat.shape[0] % 1024)).reshape(-1, 1024))
    p = jnp.concatenate(rows, axis=0)
    return jnp.pad(p, ((0, -p.shape[0] % 8), (0, 0)))


def _unpack(p, shapes):
    out, r = [], 0
    for s in shapes:
        n = 1
        for d in s:
            n *= d
        nr = -(-n // 1024)
        out.append(p[r:r + nr].reshape(-1)[:n].reshape(s))
        r += nr
    return out


BIG = ("f1_gate", "f1_up", "f1_down", "w_in", "w_out", "w_xq", "w_xkv", "w_xo", "f2_gate", "f2_up", "f2_down")
BIG_KIND = {"f1_gate": "col", "f1_up": "col", "f1_down": "row", "w_in": "row", "w_out": "row", "w_xq": "row", "w_xkv": "col",
            "w_xo": "row", "f2_gate": "col", "f2_up": "col", "f2_down": "row"}
LORA = ("rw_decay_up", "rw_aaa_up", "rw_gate_up")
WEIGHTS = ("f1_norm", "f1_gate", "f1_up", "f1_down", "mix_norm", "w_in", "b_in_attn", "rw_mu", "rw_w0", "rw_decay_up", "rw_a0",
           "rw_aaa_up", "rw_gate_up", "rw_k_k", "rw_k_a", "rw_r_k", "rw_lnx_w", "rw_lnx_b", "attn_sinks", "w_out", "b_out", "xa_norm",
           "mem_norm", "w_xq", "w_xkv", "w_xo", "f2_norm", "f2_gate", "f2_up", "f2_down", "final_norm")
SMALL = tuple(n for n in WEIGHTS if n not in BIG)
GROUP_ORDER = ("f1", "f1d", "mix", "out", "xattn", "f2")
GROUPS = {"f1": ("f1_gate", "f1_up"), "f1d": ("f1_down",), "mix": ("w_in",) + LORA, "out": ("w_out",), "xattn": ("w_xq", "w_xkv", "w_xo"),
          "f2": ("f2_gate", "f2_up", "f2_down")}


def kernel(x, mem, f1_norm, f1_gate, f1_up, f1_down, mix_norm, w_in, b_in_attn, rw_mu, rw_w0, rw_decay_up, rw_a0, rw_aaa_up, rw_gate_up, rw_k_k, rw_k_a, rw_r_k, rw_lnx_w, rw_lnx_b, attn_sinks, w_out, b_out, xa_norm, mem_norm, w_xq, w_xkv, w_xo, f2_norm, f2_gate, f2_up, f2_down, final_norm, loss_target, m_f1_norm, m_f1_gate, m_f1_up, m_f1_down, m_mix_norm, m_w_in, m_b_in_attn, m_rw_mu, m_rw_w0, m_rw_decay_up, m_rw_a0, m_rw_aaa_up, m_rw_gate_up, m_rw_k_k, m_rw_k_a, m_rw_r_k, m_rw_lnx_w, m_rw_lnx_b, m_attn_sinks, m_w_out, m_b_out, m_xa_norm, m_mem_norm, m_w_xq, m_w_xkv, m_w_xo, m_f2_norm, m_f2_gate, m_f2_up, m_f2_down, m_final_norm, v_f1_norm, v_f1_gate, v_f1_up, v_f1_down, v_mix_norm, v_w_in, v_b_in_attn, v_rw_mu, v_rw_w0, v_rw_decay_up, v_rw_a0, v_rw_aaa_up, v_rw_gate_up, v_rw_k_k, v_rw_k_a, v_rw_r_k, v_rw_lnx_w, v_rw_lnx_b, v_attn_sinks, v_w_out, v_b_out, v_xa_norm, v_mem_norm, v_w_xq, v_w_xkv, v_w_xo, v_f2_norm, v_f2_gate, v_f2_up, v_f2_down, v_final_norm):
    a = dict(locals())
    w = {n: a[n] for n in WEIGHTS}
    m = {n: a["m_" + n] for n in WEIGHTS}
    v = {n: a["v_" + n] for n in WEIGHTS}
    sq = lambda t: t.reshape(t.shape[-2:]) if t.ndim == 3 else t.reshape(1, -1)

    local_name = lambda n: "w_inT" if n == "w_in" else n
    kind_of = lambda n: BIG_KIND.get(n, "col")
    payload = lambda n: sq(w[n]).T if n == "w_in" else sq(w[n]) if n in LORA else sq(w[n]).astype(BF16)
    gathers = {}

    def start_gather(name, grps, after):
        shards = [payload(n) for g in grps for n in GROUPS[g]]
        kinds = [kind_of(n) for g in grps for n in GROUPS[g]]
        groups, at = [], 0
        for g in grps:
            groups.append(list(range(at, at + len(GROUPS[g]))))
            at += len(GROUPS[g])
        sems, src_thru, land_thru, token = _gather_start(name, shards, kinds, groups, after)
        for gi, g in enumerate(grps):
            gathers[g] = (sems[3 * gi:3 * gi + 3], [src_thru[i] for i in groups[gi]], [land_thru[i] for i in groups[gi]],
                          [kinds[i] for i in groups[gi]], token)

    early = GROUP_ORDER[:3]
    start_gather("gather_start", early, None)

    def get_w(grp, after):
        g_sems, g_src, g_land, g_kinds, token = gathers[grp]
        got = _gather_wait("gather_wait_" + grp, g_sems, g_src, g_land, g_kinds, token if after is None else after)
        got = _swap_halves("gather_swap_" + grp, got, [s.shape for s in g_src], g_kinds)
        out = {local_name(n): f for n, f in zip(GROUPS[grp], got)}
        if grp == early[-1]:
            start_gather("gather_start_late", GROUP_ORDER[3:], got[0])
            out["_after"] = gathers[GROUP_ORDER[3]][4]
        return out

    in_flight = []

    def put_g(label, gw):
        names = list(gw)
        *flight, sent = _scatter_start("scatter_start_" + label, [gw[n] for n in names], [kind_of(n) for n in names])
        in_flight.append((label, names, flight))
        return sent

    P = {n: sq(w[n]) for n in SMALL if n not in LORA}
    P["attn_sinks"] = jnp.pad(P["attn_sinks"], ((0, 0), (0, 128 - P["attn_sinks"].shape[1])))
    P["rw_r_k"] = w["rw_r_k"].reshape(1, RW_W)
    loss_part, grad_x, gs = _local_step(x[0], mem[0], loss_target[0], get_w, P, put_g)

    gs["attn_sinks"] = gs["attn_sinks"][:, :16]
    small_flight = _small_start(_pack([gs[n] for n in SMALL] + [loss_part]), grad_x)

    out, after = {}, small_flight[-1]
    for bi, batch in enumerate((in_flight[:-3], in_flight[-3:])):
        b_names, b_partial = [], []
        for label, names, (g_sems, g_thru, l_thru) in batch:
            stacks = _scatter_wait("scatter_wait_" + label, g_sems, g_thru, l_thru, [kind_of(n) for n in names], after)
            partial = [_sum_slots(s, "sum_chips_" + n, F32 if n == "w_in" else BF16) for s, n in zip(stacks, names)]
            b_names += names
            b_partial += partial
            after = partial[-1]
        sibling = _swap_with_sibling(b_partial, "swap_batch%d" % bi)
        chain = None
        for n, pa, sb in zip(b_names, b_partial, sibling):
            if n == "w_in":
                pa, sb = pa.T, sb.T
            out[n] = _adamw(sq(w[n]), sq(m[n]), sq(v[n]), pa, sb, "adamw_" + n, after=chain)
            chain = out[n][1]
        after = chain

    gsum = _sum_slots(_small_wait(*small_flight[:-1], after), "sum_small")
    *summed, loss_row = _unpack(gsum, [gs[n].shape for n in SMALL] + [loss_part.shape])
    g_small = dict(zip(SMALL, summed))
    loss = loss_row[0, 0]
    shard = 2 * lax.axis_index("x") + lax.axis_index("y")
    for n in LORA:
        cols = w[n].shape[-1]
        g_small[n] = lax.dynamic_slice_in_dim(g_small[n], shard * cols, cols, axis=1)

    flat = lambda d: _pack([d[n] for n in SMALL])
    res = _adamw(flat(w), flat(m), flat(v), _pack([g_small[n] for n in SMALL]), None, "adamw_small")
    shapes = [w[n].shape for n in SMALL]
    for k, p in enumerate(res):
        for n, t in zip(SMALL, _unpack(p, shapes)):
            out.setdefault(n, [None] * 4)[k] = t
    outs = [loss, grad_x.reshape(x.shape)]
    for k in range(4):
        outs += [out[n][k].reshape(w[n].shape) for n in WEIGHTS]
    return tuple(outs)
```

```python
import functools

import jax
import jax.numpy as jnp
from jax import lax
from jax.experimental import pallas as pl
from jax.experimental.pallas import tpu as pltpu

F32, BF16 = jnp.float32, jnp.bfloat16
MESH = pl.DeviceIdType.MESH

HEAD = 64
RW_HEADS = 16
RW_W = 1024
SWA_W = 1024
KV_W = 128
DECAY_LORA, AAA_LORA, GATE_LORA = 64, 64, 160
LORA_W = DECAY_LORA + AAA_LORA + GATE_LORA
SHIFT_COLS = 3 * RW_W + LORA_W
XH = 4
XHD = 512
MEM_LEN = 256
WINDOW = 128
GN_EPS = 64e-5
RMS_EPS = 1e-6
NEG_INF = -1e30
ADAM_LR, ADAM_B1, ADAM_B2, ADAM_EPS, ADAM_WD, ADAM_STEP = 0.001, 0.9, 0.999, 1e-08, 0.01, 10

VMEM_LIMIT = 56 * 1024 * 1024


def _cp(sem=None, **kw):
    return pltpu.CompilerParams(dimension_semantics=sem, vmem_limit_bytes=VMEM_LIMIT, **kw)


def _pick(dim, target):
    if dim <= target:
        return dim
    best = None
    for t in range(128, target + 1, 128):
        if dim % t == 0:
            best = t
    assert best is not None, (dim, target)
    return best


_DIMS = {"nn": (((1,), (0,)), ((), ())), "nt": (((1,), (1,)), ((), ())), "tn": (((0,), (0,)), ((), ()))}


def _mm(a, b, mode, name, out_dtype=F32, alpha=1.0, res=None, bias=None, tm=1024, tn=1024, tk=2048, after=None):
    if mode == "nn":
        (M, K), (K2, N) = a.shape, b.shape
    elif mode == "nt":
        (M, K), (N, K2) = a.shape, b.shape
    else:
        (K, M), (K2, N) = a.shape, b.shape
    assert K == K2, (name, a.shape, b.shape)
    tm, tn, tk = _pick(M, tm), _pick(N, tn), _pick(K, tk)
    nk = K // tk
    a_spec = pl.BlockSpec((tk, tm), lambda i, j, k: (k, i)) if mode == "tn" else pl.BlockSpec((tm, tk), lambda i, j, k: (i, k))
    b_spec = pl.BlockSpec((tn, tk), lambda i, j, k: (j, k)) if mode == "nt" else pl.BlockSpec((tk, tn), lambda i, j, k: (k, j))
    o_spec = pl.BlockSpec((tm, tn), lambda i, j, k: (i, j))
    ins, specs = [a, b], [a_spec, b_spec]
    if res is not None:
        ins.append(res)
        specs.append(o_spec)
    if bias is not None:
        ins.append(bias)
        specs.append(pl.BlockSpec((1, tn), lambda i, j, k: (0, j)))
    if after is not None:
        ins.append(after)
        specs.append(pl.BlockSpec(memory_space=pl.ANY))
    dims = _DIMS[mode]

    def body(*refs):
        a_ref, b_ref = refs[0], refs[1]
        part = lax.dot_general(a_ref[...].astype(BF16), b_ref[...].astype(BF16), dims, preferred_element_type=F32)

        def finish(o, o_ref):
            if alpha != 1.0:
                o = o * alpha
            p = 2
            if res is not None:
                o = o + refs[p][...].astype(F32)
                p += 1
            if bias is not None:
                o = o + refs[p][...]
            o_ref[...] = o.astype(out_dtype)

        if nk == 1:
            finish(part, refs[-1])
            return
        o_ref, acc_ref = refs[-2], refs[-1]
        k = pl.program_id(2)

        @pl.when(k == 0)
        def _():
            acc_ref[...] = part

        @pl.when(k > 0)
        def _():
            acc_ref[...] += part

        @pl.when(k == nk - 1)
        def _():
            finish(acc_ref[...], o_ref)

    return pl.pallas_call(
        body, name=name, grid=(M // tm, N // tn, nk), in_specs=specs, out_specs=o_spec,
        out_shape=jax.ShapeDtypeStruct((M, N), out_dtype), scratch_shapes=[pltpu.VMEM((tm, tn), F32)] * (nk > 1),
        compiler_params=_cp(("parallel", "parallel", "arbitrary")),
    )(*ins)


def _rows(fn, name, T, tm, tiled, full, out_tiled, out_acc, extra=(), reverse=False, scratch=()):
    n = T // tm
    idx = (lambda i: n - 1 - i) if reverse else (lambda i: i)
    in_specs = [pl.BlockSpec((tm, a.shape[1]), lambda i: (idx(i), 0)) for a in tiled]
    in_specs += [mk(idx) for _, mk in extra]
    in_specs += [pl.BlockSpec(a.shape, lambda i, nd=a.ndim: (0,) * nd) for a in full]
    out_specs = [pl.BlockSpec((tm, c), lambda i: (idx(i), 0)) for c, _ in out_tiled]
    out_specs += [pl.BlockSpec(s, lambda i, nd=len(s): (0,) * nd) for s, _ in out_acc]
    out_shape = [jax.ShapeDtypeStruct((T, c), d) for c, d in out_tiled] + [jax.ShapeDtypeStruct(s, d) for s, d in out_acc]
    n_in = len(tiled) + len(extra) + len(full)
    n_t, n_a = len(out_tiled), len(out_acc)

    def body(*refs):
        step = pl.program_id(0)
        vals = [r[...] for r in refs[:n_in]]
        outs = fn(idx(step), *vals, *refs[n_in + n_t + n_a:])
        for r, v in zip(refs[n_in:n_in + n_t], outs[:n_t]):
            r[...] = v.astype(r.dtype)
        for r, v in zip(refs[n_in + n_t:n_in + n_t + n_a], outs[n_t:]):
            @pl.when(step == 0)
            def _(r=r):
                r[...] = jnp.zeros_like(r)

            r[...] += v

    return pl.pallas_call(
        body, name=name, grid=(n,), in_specs=in_specs, out_specs=out_specs, out_shape=out_shape,
        scratch_shapes=list(scratch), compiler_params=_cp(("arbitrary",)),
    )(*tiled, *[a for a, _ in extra], *full)


def _rms(x, g):
    return x * lax.rsqrt(jnp.mean(x * x, axis=-1, keepdims=True) + RMS_EPS) * g


def _rms_fwd(x, g, name, tm=256):
    (h,) = _rows(lambda i, x, g: (_rms(x, g),), name, x.shape[0], min(tm, x.shape[0]), [x], [g], [(x.shape[1], BF16)], [])
    return h


def _rms_bwd(x, g, dh, dres, name, tm=256):
    D = x.shape[1]

    def fn(i, x, dh, dres, g):
        _, vjp = jax.vjp(_rms, x, g)
        dx, dg = vjp(dh.astype(F32))
        dx = dx + dres
        return dx, dg, jnp.sum(dx, axis=0, keepdims=True)

    return _rows(fn, name, x.shape[0], tm, [x, dh, dres], [g], [(D, F32)], [((1, D), F32), ((1, D), F32)])


def _ffn_up(h, wg, wu, name, tm=1024, tn=512, after=None):
    (M, K), N = h.shape, wg.shape[1]
    tm, tn = _pick(M, tm), _pick(N, tn)

    def body(*refs):
        h_ref, wg_ref, wu_ref = refs[:3]
        g_ref, u_ref, a_ref = refs[-3:]
        hb = h_ref[...].astype(BF16)
        g = jnp.dot(hb, wg_ref[...].astype(BF16), preferred_element_type=F32)
        u = jnp.dot(hb, wu_ref[...].astype(BF16), preferred_element_type=F32)
        g_ref[...] = g
        u_ref[...] = u
        a_ref[...] = (g * jax.nn.sigmoid(g) * u).astype(BF16)

    o_spec = pl.BlockSpec((tm, tn), lambda i, j: (i, j))
    w_spec = pl.BlockSpec((K, tn), lambda i, j: (0, j))
    extra = [] if after is None else [after]
    return pl.pallas_call(
        body, name=name, grid=(M // tm, N // tn),
        in_specs=[pl.BlockSpec((tm, K), lambda i, j: (i, 0)), w_spec, w_spec] + [pl.BlockSpec(memory_space=pl.ANY)] * len(extra),
        out_specs=[o_spec] * 3, out_shape=[jax.ShapeDtypeStruct((M, N), F32)] * 2 + [jax.ShapeDtypeStruct((M, N), BF16)],
        compiler_params=_cp(("parallel", "parallel")),
    )(h, wg, wu, *extra)


def _ffn_dact(dxo, wd, g, u, name, tm=1024, tn=512):
    (M, K), N = dxo.shape, wd.shape[0]
    tm, tn = _pick(M, tm), _pick(N, tn)

    def body(dx_ref, wd_ref, g_ref, u_ref, dg_ref, du_ref):
        da = 0.5 * lax.dot_general(dx_ref[...].astype(BF16), wd_ref[...].astype(BF16), _DIMS["nt"], preferred_element_type=F32)
        g = g_ref[...]
        s = jax.nn.sigmoid(g)
        dg_ref[...] = (da * u_ref[...] * (s * (1.0 + g * (1.0 - s)))).astype(BF16)
        du_ref[...] = (da * (g * s)).astype(BF16)

    t_spec = pl.BlockSpec((tm, tn), lambda i, j: (i, j))
    return pl.pallas_call(
        body, name=name, grid=(M // tm, N // tn),
        in_specs=[pl.BlockSpec((tm, K), lambda i, j: (i, 0)), pl.BlockSpec((tn, K), lambda i, j: (j, 0)), t_spec, t_spec],
        out_specs=[t_spec, t_spec], out_shape=[jax.ShapeDtypeStruct((M, N), BF16)] * 2, compiler_params=_cp(("parallel", "parallel")),
    )(dxo, wd, g, u)


def _ffn_fwd(x, gain, wg, wu, wd, tag, after=None):
    h = _rms_fwd(x, gain, tag + "_norm")
    G, U, A = _ffn_up(h, wg, wu, tag + "_up", after=after)
    xo = _mm(A, wd(A) if callable(wd) else wd, "nn", tag + "_down", alpha=0.5, res=x)
    return xo, (h, G, U, A)


def _ffn_bwd(x, gain, wg, wu, wd, saved, dxo, tag, send):
    h, G, U, A = saved
    dwd = _mm(A, dxo, "tn", tag + "_dwd", out_dtype=BF16, alpha=0.5, tm=1408)
    sent = send(tag + "_down", {tag + "_down": dwd})
    dG, dU = _ffn_dact(dxo, wd, G, U, tag + "_dact")
    dwu = _mm(h, dU, "tn", tag + "_dwu", out_dtype=BF16, after=sent)
    sent = send(tag + "_up", {tag + "_up": dwu})
    dwg = _mm(h, dG, "tn", tag + "_dwg", out_dtype=BF16, after=sent)
    sent = send(tag + "_gate", {tag + "_gate": dwg})
    dh = _mm(dG, wg, "nt", tag + "_dh_g", after=sent)
    dh = _mm(dU, wu, "nt", tag + "_dh_u", res=dh)
    dx, dgain, _ = _rms_bwd(x, gain, dh, dxo, tag + "_norm_bwd")
    return dx, dgain


def _segsum64_impl(x):
    r = lax.broadcasted_iota(jnp.int32, (128, 128), 0) // HEAD
    c = lax.broadcasted_iota(jnp.int32, (128, 128), 1) // HEAD
    ones = (r == c).astype(BF16)
    hi = x.astype(BF16)
    lo = (x - hi.astype(F32)).astype(BF16)
    outs = []
    for q in range(x.shape[1] // 128):
        sl = slice(q * 128, (q + 1) * 128)
        outs.append(jnp.dot(hi[:, sl], ones, preferred_element_type=F32) + jnp.dot(lo[:, sl], ones, preferred_element_type=F32))
    return outs[0] if len(outs) == 1 else jnp.concatenate(outs, axis=1)


@jax.custom_vjp
def _segsum64(x):
    return _segsum64_impl(x)


_segsum64.defvjp(lambda x: (_segsum64_impl(x), None), lambda _, ct: (_segsum64_impl(ct),))


def _swap32(x):
    lane = lax.broadcasted_iota(jnp.int32, (x.shape[0], 128), 1)
    outs = [jnp.take_along_axis(x[:, q * 128:(q + 1) * 128], lane ^ 32, axis=1) for q in range(x.shape[1] // 128)]
    return outs[0] if len(outs) == 1 else jnp.concatenate(outs, axis=1)


def _tree_sum(xs):
    xs = list(xs)
    while len(xs) > 1:
        nxt = [xs[i] + xs[i + 1] for i in range(0, len(xs) - 1, 2)]
        if len(xs) % 2:
            nxt.append(xs[-1])
        xs = nxt
    return xs[0]


class _Acc:
    def __init__(self, ways=4):
        self.parts = [None] * ways

    def add(self, i, term):
        k = i % len(self.parts)
        self.parts[k] = term if self.parts[k] is None else self.parts[k] + term

    def total(self):
        return _tree_sum([p for p in self.parts if p is not None])


def _softplus(x):
    return jnp.maximum(x, 0.0) + jnp.log(1.0 + jnp.exp(-jnp.abs(x)))


def _pre_core(k, da, gd, w0, a0, k_k, k_a, w_da, gate_up):
    lane = lax.broadcasted_iota(jnp.int32, da.shape, 1)
    w_da = w_da.astype(BF16)
    l1 = jnp.dot(jnp.where(lane < DECAY_LORA, jnp.tanh(da), 0.0).astype(BF16), w_da, preferred_element_type=F32)
    l2 = jnp.dot(jnp.where(lane >= DECAY_LORA, da, 0.0).astype(BF16), w_da, preferred_element_type=F32)
    wlog = -_softplus(-(w0 + l1)) - 0.5
    decay = jnp.exp(-jnp.exp(wlog))
    a = jax.nn.sigmoid(a0 + l2)
    g = jnp.dot(jax.nn.sigmoid(gd).astype(BF16), gate_up.astype(BF16), preferred_element_type=F32)
    kk = k * k_k
    kkn = kk / jnp.maximum(jnp.sqrt(_segsum64(kk * kk)), 1e-12)
    k2 = k * (1.0 + (a - 1.0) * k_a)
    return decay, k2, -kkn, kkn * a, g


def _pre_shift(i, zr, zl, zr8, zl8, mu, mul):
    live = (i > 0).astype(F32)
    dz = _shift_down(zr, zr8[7:8, :] * live) - zr
    dzl = _shift_down(zl, zl8[7:8, :] * live) - zl
    return zr + dz * mu, zl + dzl * mul, dz, dzl


def _shift_down(x, first_row):
    rolled = pltpu.roll(x, 1, 0)
    row = lax.broadcasted_iota(jnp.int32, x.shape, 0)
    return jnp.where(row == 0, first_row, rolled)


def _shift_up(x, last_row):
    rolled = pltpu.roll(x, x.shape[0] - 1, 0)
    row = lax.broadcasted_iota(jnp.int32, x.shape, 0)
    return jnp.where(row == x.shape[0] - 1, last_row, rolled)


def _prev_rows_spec(tm, cols):
    return lambda idx: pl.BlockSpec((8, cols), lambda i: (jnp.maximum(idx(i) * (tm // 8) - 1, 0), 0))


def _rwkv_pre(p_rkv, p_lora, params, tm=256):
    T = p_rkv.shape[0]

    def fn(i, zr, zl, zr8, zl8, mu, mul, *ps):
        z, z2, _, _ = _pre_shift(i, zr, zl, zr8, zl8, mu, mul)
        decay, k2, an, bn, g = _pre_core(z[:, RW_W:2 * RW_W], z2[:, :128], z2[:, 128:], *ps)
        return z[:, :RW_W], decay, k2, z[:, 2 * RW_W:], an, bn, g

    extra = [(p_rkv, _prev_rows_spec(tm, 3 * RW_W)), (p_lora, _prev_rows_spec(tm, LORA_W))]
    return _rows(fn, "rwkv_pre", T, tm, [p_rkv, p_lora], list(params), [(RW_W, F32)] * 7, [], extra=extra)


def _rwkv_pre_bwd(p_rkv, p_lora, params, cts, tm=256):
    T = p_rkv.shape[0]
    n = T // tm

    def fn(i, zr, zl, cr, cdec, ck2, cv, can, cbn, cg, cr_b, ck2_b, cv_b, zr8, zl8, mu, mul, *rest):
        ps, (car, carl) = rest[:-2], rest[-2:]
        cr, ck2, cv = cr + cr_b, ck2 + ck2_b, cv + cv_b
        z, z2, dif, difl = _pre_shift(i, zr, zl, zr8, zl8, mu, mul)
        _, vjp = jax.vjp(_pre_core, z[:, RW_W:2 * RW_W], z2[:, :128], z2[:, 128:], *ps)
        dk, dda, dgd, *dps = vjp((cdec, ck2, can, cbn, cg))
        dz = jnp.concatenate([cr, dk, cv], axis=1)
        dz2 = jnp.concatenate([dda, dgd], axis=1)
        dzp, dzlp = dz * mu, dz2 * mul

        @pl.when(i == n - 1)
        def _():
            car[...] = jnp.zeros_like(car)
            carl[...] = jnp.zeros_like(carl)

        d_rkv = dz - dzp + _shift_up(dzp, car[0:1, :])
        d_lora = dz2 - dzlp + _shift_up(dzlp, carl[0:1, :])
        car[0:1, :] = dzp[0:1, :]
        carl[0:1, :] = dzlp[0:1, :]
        return (d_rkv, d_lora, jnp.sum(dz * dif, axis=0, keepdims=True), jnp.sum(dz2 * difl, axis=0, keepdims=True), *dps)

    extra = [(p_rkv, _prev_rows_spec(tm, 3 * RW_W)), (p_lora, _prev_rows_spec(tm, LORA_W))]
    acc = [(p.shape, F32) for p in params]
    return _rows(fn, "rwkv_pre_bwd", T, tm, [p_rkv, p_lora, *cts], list(params), [(3 * RW_W, BF16), (LORA_W, BF16)], acc,
                 extra=extra, reverse=True, scratch=[pltpu.VMEM((8, 3 * RW_W), F32), pltpu.VMEM((8, LORA_W), F32)])


def _post_core(y, r, k2, v, g, lw, lb, rk):
    mu = _segsum64(y) * (1.0 / HEAD)
    yc = y - mu
    var = _segsum64(yc * yc) * (1.0 / HEAD)
    yn = yc * lax.rsqrt(var + GN_EPS) * lw + lb
    return (yn + _segsum64(r * k2 * rk) * v) * g


def _rwkv_post(y, r, k2, v, g, lw, lb, rk, tm=256):
    (o,) = _rows(lambda i, *a: (_post_core(*a),), "rwkv_post", y.shape[0], tm, [y, r, k2, v, g], [lw, lb, rk], [(RW_W, BF16)], [])
    return o


def _rwkv_post_bwd(y, r, k2, v, g, lw, lb, rk, do, tm=256):
    def fn(i, y, r, k2, v, g, do, lw, lb, rk):
        _, vjp = jax.vjp(_post_core, y, r, k2, v, g, lw, lb, rk)
        return vjp(do.astype(F32))

    return _rows(fn, "rwkv_post_bwd", y.shape[0], tm, [y, r, k2, v, g, do], [lw, lb, rk], [(RW_W, F32)] * 5, [((1, RW_W), F32)] * 3)


SCAN_L = 32


def _to_perm(x):
    T = x.shape[0]
    return x.reshape(T, RW_HEADS, HEAD).transpose(0, 2, 1).reshape(T, 8, 128)


def _from_perm(x):
    T = x.shape[0]
    return x.reshape(T, HEAD, RW_HEADS).transpose(0, 2, 1).reshape(T, RW_W)


def _as_tile(p):
    lane = lax.broadcasted_iota(jnp.int32, (8, 128), 1)
    return jnp.take_along_axis(p, (lane % 8) * 16 + lane // 8, axis=1)


def _as_perm(t):
    lane = lax.broadcasted_iota(jnp.int32, (8, 128), 1)
    return jnp.take_along_axis(t, (lane % 16) * 8 + lane // 16, axis=1)


def _tiles_to_perm(refs, L):
    for r in refs:
        for t in range(L):
            r[t] = _as_perm(r[t])


def _expander(srcs, tiles=()):
    s = lax.broadcasted_iota(jnp.int32, (8, 128), 0)
    lane = lax.broadcasted_iota(jnp.int32, (8, 128), 1)
    idx = 16 * s + lane // 8

    def expand(t, e_ref):
        for m, r in enumerate(srcs):
            for g in range(8):
                row = jnp.broadcast_to(r[t, pl.ds(g, 1), :], (8, 128))
                e_ref[m, g * 8:(g + 1) * 8, :] = jnp.take_along_axis(row, idx, axis=1)
        for k, r in enumerate(tiles):
            e_ref[len(srcs) + k, 0:8, :] = _as_tile(r[t])

    return expand


def _ck_a_to_b(ck):
    n = ck.shape[0]
    return ck.reshape(n, 8, 8, 8, RW_HEADS, 8).transpose(0, 2, 5, 1, 4, 3).reshape(n, HEAD, 8, 128)


def _bc(row):
    return jnp.broadcast_to(row, (8, 128))


def _rsum(x):
    return jnp.sum(x, axis=0, keepdims=True)


def _plus(acc, k, term):
    acc[k] = term if acc[k] is None else acc[k] + term


def _scan_fwd(xes, vi):
    T, L = vi.shape[0], SCAN_L
    nch = T // L

    def body(*refs):
        xr, (vi_ref, yi_ref, sa_ref, ck_ref, st_ref, e0, e1) = refs[:5], refs[5:]

        @pl.when(pl.program_id(0) == 0)
        def _():
            st_ref[...] = jnp.zeros_like(st_ref)

        ck_ref[0] = st_ref[...]
        expand = _expander(xr, [vi_ref])
        expand(0, e0)

        def step(t, e):
            tile = lambda m, jh: e[m, 8 * jh:8 * jh + 8, :]
            vb = [_bc(e[5, ih:ih + 1, :]) for ih in range(8)]
            acc = [None] * 8
            for jh in range(8):
                a = tile(0, jh)
                for ih in range(8):
                    _plus(acc, ih, st_ref[8 * jh + ih] * a)
            sab = []
            for ih in range(8):
                row = _rsum(acc[ih])
                sa_ref[t, ih:ih + 1, :] = row
                sab.append(_bc(row))
            yacc = [None] * 8
            for jh in range(8):
                w, B, k, r = tile(1, jh), tile(2, jh), tile(3, jh), tile(4, jh)
                for ih in range(8):
                    s = st_ref[8 * jh + ih] * w + B * sab[ih] + k * vb[ih]
                    st_ref[8 * jh + ih] = s
                    _plus(yacc, ih, s * r)
            for ih in range(8):
                yi_ref[t, ih:ih + 1, :] = _rsum(yacc[ih])

        def pair(p, carry):
            t = 2 * p
            expand(t + 1, e1)
            step(t, e0)
            expand(jnp.minimum(t + 2, L - 1), e0)
            step(t + 1, e1)
            return carry

        lax.fori_loop(0, L // 2, pair, 0)
        _tiles_to_perm([yi_ref, sa_ref], L)

    tile = pl.BlockSpec((L, 8, 128), lambda c: (c, 0, 0))
    return pl.pallas_call(
        body, name="rwkv_scan_fwd", grid=(nch,), in_specs=[tile] * 6,
        out_specs=[tile, tile, pl.BlockSpec((1, HEAD, 8, 128), lambda c: (c, 0, 0, 0))],
        out_shape=[jax.ShapeDtypeStruct((T, 8, 128), F32)] * 2 + [jax.ShapeDtypeStruct((nch, HEAD, 8, 128), F32)],
        scratch_shapes=[pltpu.VMEM((HEAD, 8, 128), F32)] + [pltpu.VMEM((6, HEAD, 128), F32)] * 2, compiler_params=_cp(("arbitrary",)),
    )(*xes, vi)


def _scan_bwd_a(xes, dyi):
    T, L = dyi.shape[0], SCAN_L
    nch = T // L

    def body(*refs):
        xr, (dy_ref, dsa_ref, dv_ref, g_ref, e0, e1) = refs[:5], refs[5:]

        @pl.when(pl.program_id(0) == 0)
        def _():
            g_ref[...] = jnp.zeros_like(g_ref)

        expand = _expander(xr, [dy_ref])
        expand(L - 1, e0)

        def step(t, e):
            tile = lambda m, jh: e[m, 8 * jh:8 * jh + 8, :]
            dyb = [_bc(e[5, ih:ih + 1, :]) for ih in range(8)]
            dsa, dv = [None] * 8, [None] * 8
            for jh in range(8):
                B, k, r = tile(2, jh), tile(3, jh), tile(4, jh)
                for ih in range(8):
                    g = g_ref[8 * jh + ih] + r * dyb[ih]
                    g_ref[8 * jh + ih] = g
                    _plus(dsa, ih, g * B)
                    _plus(dv, ih, g * k)
            dsab = []
            for ih in range(8):
                row = _rsum(dsa[ih])
                dsa_ref[t, ih:ih + 1, :] = row
                dsab.append(_bc(row))
                dv_ref[t, ih:ih + 1, :] = _rsum(dv[ih])
            for jh in range(8):
                A, w = tile(0, jh), tile(1, jh)
                for ih in range(8):
                    g_ref[8 * jh + ih] = g_ref[8 * jh + ih] * w + A * dsab[ih]

        def pair(p, carry):
            t = L - 1 - 2 * p
            expand(t - 1, e1)
            step(t, e0)
            expand(jnp.maximum(t - 2, 0), e0)
            step(t - 1, e1)
            return carry

        lax.fori_loop(0, L // 2, pair, 0)
        _tiles_to_perm([dsa_ref, dv_ref], L)

    tile = pl.BlockSpec((L, 8, 128), lambda c: (nch - 1 - c, 0, 0))
    return pl.pallas_call(
        body, name="rwkv_scan_bwd_a", grid=(nch,), in_specs=[tile] * 6, out_specs=[tile, tile],
        out_shape=[jax.ShapeDtypeStruct((T, 8, 128), F32)] * 2,
        scratch_shapes=[pltpu.VMEM((HEAD, 8, 128), F32)] + [pltpu.VMEM((6, HEAD, 128), F32)] * 2, compiler_params=_cp(("arbitrary",)),
    )(*xes, dyi)


def _scan_bwd_b(xts, ies, ckb):
    T, L = xts[0].shape[0], SCAN_L
    nch = T // L

    def body(*refs):
        xr, er, ck_ref, dj, (hist, g_ref, e0, e1) = refs[:5], refs[5:9], refs[9], refs[10:15], refs[15:]

        @pl.when(pl.program_id(0) == 0)
        def _():
            g_ref[...] = jnp.zeros_like(g_ref)

        hist[0] = ck_ref[0]
        expand_vs = _expander(er[:2], [xr[1], xr[2], xr[3]])
        expand = _expander(er, [xr[0], xr[1], xr[4]])
        expand_vs(0, e0)

        def fstep(t, e_ref):
            w, B, k = e_ref[2, 0:8, :], e_ref[3, 0:8, :], e_ref[4, 0:8, :]
            row = lambda m, i: jnp.broadcast_to(e_ref[m, pl.ds(i, 1), :], (8, 128))
            for i in range(HEAD):
                hist[t + 1, i] = hist[t, i] * w + row(1, i) * B + row(0, i) * k

        def fpair(p, carry):
            t = 2 * p
            expand_vs(t + 1, e1)
            fstep(t, e0)
            expand_vs(jnp.minimum(t + 2, L - 1), e0)
            fstep(t + 1, e1)
            return carry

        lax.fori_loop(0, L // 2, fpair, 0)
        expand(L - 1, e0)

        def bstep(t, e_ref):
            A, w, r = e_ref[4, 0:8, :], e_ref[5, 0:8, :], e_ref[6, 0:8, :]
            row = lambda m, i: jnp.broadcast_to(e_ref[m, pl.ds(i, 1), :], (8, 128))
            acc = [_Acc() for _ in range(5)]
            for i in range(HEAD):
                dy_i, dsa_i = row(2, i), row(3, i)
                g = g_ref[i] + dy_i * r
                sp = hist[t, i]
                acc[4].add(i, hist[t + 1, i] * dy_i)
                acc[1].add(i, g * sp)
                acc[2].add(i, g * row(1, i))
                acc[3].add(i, g * row(0, i))
                acc[0].add(i, sp * dsa_i)
                g_ref[i] = g * w + dsa_i * A
            for m in range(5):
                dj[m][t] = acc[m].total()

        def bpair(p, carry):
            t = L - 1 - 2 * p
            expand(t - 1, e1)
            bstep(t, e0)
            expand(jnp.maximum(t - 2, 0), e0)
            bstep(t - 1, e1)
            return carry

        lax.fori_loop(0, L // 2, bpair, 0)
        _tiles_to_perm(dj, L)

    tile = pl.BlockSpec((L, 8, 128), lambda c: (nch - 1 - c, 0, 0))
    return pl.pallas_call(
        body, name="rwkv_scan_bwd_b", grid=(nch,),
        in_specs=[tile] * 9 + [pl.BlockSpec((1, HEAD, 8, 128), lambda c: (nch - 1 - c, 0, 0, 0))],
        out_specs=[tile] * 5, out_shape=[jax.ShapeDtypeStruct((T, 8, 128), F32)] * 5,
        scratch_shapes=[pltpu.VMEM((L + 1, HEAD, 8, 128), F32), pltpu.VMEM((HEAD, 8, 128), F32)] + [pltpu.VMEM((7, HEAD, 128), F32)] * 2,
        compiler_params=_cp(("arbitrary",)),
    )(*xts, *ies, ckb)


SWA_COLS = SWA_W + 2 * KV_W
BLK = 128


def _swa_core(n, k2a, k2b, vla, vra, vlb, vrb, sinks, *qps):
    iq = lax.broadcasted_iota(jnp.int32, (BLK, 2 * BLK), 0)
    ik = lax.broadcasted_iota(jnp.int32, (BLK, 2 * BLK), 1)
    diff = BLK + iq - ik
    valid = (diff >= 0) & (diff < WINDOW) & ((n > 0) | (ik >= BLK))
    lane = lax.broadcasted_iota(jnp.int32, (BLK, 128), 1)
    lane1 = lax.broadcasted_iota(jnp.int32, (1, 128), 1)
    nt = (((1,), (1,)), ((), ()))
    outs = []
    for pp in range(8):
        k2, vl, vr = (k2a, vla, vra) if pp < 4 else (k2b, vlb, vrb)
        qp = qps[pp]
        o = None
        for half, vv in ((0, vl), (1, vr)):
            qh = jnp.where((lane >= HEAD) == (half == 1), qp, 0.0).astype(BF16)
            s = lax.dot_general(qh, k2.astype(BF16), nt, preferred_element_type=F32) * (HEAD ** -0.5)
            s = jnp.where(valid, s, NEG_INF)
            sink = jnp.sum(jnp.where(lane1 == 2 * pp + half, sinks, 0.0), axis=1, keepdims=True)
            m = jnp.maximum(jnp.max(s, axis=1, keepdims=True), sink)
            p = jnp.exp(s - m)
            den = jnp.sum(p, axis=1, keepdims=True) + jnp.exp(sink - m)
            oh = jnp.dot((p / den).astype(BF16), vv.astype(BF16), preferred_element_type=F32)
            o = oh if o is None else o + oh
        outs.append(o)
    return jnp.concatenate(outs, axis=1)


def _swa_prep(pc, pp, b, cq, sq, ckc, skc, ckp, skp):
    zc, zp = pc + b, pp + b
    qr = zc[:, :SWA_W] * cq + _swap32(zc[:, :SWA_W]) * sq
    kc, kp = zc[:, SWA_W:SWA_W + KV_W], zp[:, SWA_W:SWA_W + KV_W]
    kb = jnp.concatenate([kp * ckp + _swap32(kp) * skp, kc * ckc + _swap32(kc) * skc], axis=0)
    vb = jnp.concatenate([zp[:, SWA_W + KV_W:], zc[:, SWA_W + KV_W:]], axis=0)
    lane = lax.broadcasted_iota(jnp.int32, kb.shape, 1)
    left = lane < HEAD
    kbr, vbr = pltpu.roll(kb, HEAD, 1), pltpu.roll(vb, HEAD, 1)
    return (jnp.where(left, kb, kbr), jnp.where(left, kbr, kb), jnp.where(left, vb, 0.0), jnp.where(left, 0.0, vbr),
            jnp.where(left, vbr, 0.0), jnp.where(left, 0.0, vb)), [qr[:, q * 128:(q + 1) * 128] for q in range(8)]


def _swa_specs(T, tabs_q, tabs_k):
    cur = lambda c: pl.BlockSpec((BLK, c), lambda n: (n, 0))
    prev = lambda c: pl.BlockSpec((BLK, c), lambda n: (jnp.maximum(n - 1, 0), 0))
    return cur, prev


def _swa_fwd(p_swa, b, sinks, cq, sq, ck, sk):
    T = p_swa.shape[0]
    cur, prev = _swa_specs(T, None, None)

    def body(pc, pp, b_ref, s_ref, cq_r, sq_r, ckc, skc, ckp, skp, o_ref):
        ops, qps = _swa_prep(pc[...], pp[...], b_ref[...], cq_r[...], sq_r[...], ckc[...], skc[...], ckp[...], skp[...])
        o_ref[...] = _swa_core(pl.program_id(0), *ops, s_ref[...], *qps).astype(o_ref.dtype)

    full = lambda a: pl.BlockSpec(a.shape, lambda n: (0, 0))
    return pl.pallas_call(
        body, name="swa_fwd", grid=(T // BLK,),
        in_specs=[cur(SWA_COLS), prev(SWA_COLS), full(b), full(sinks), cur(SWA_W), cur(SWA_W), cur(KV_W), cur(KV_W), prev(KV_W), prev(KV_W)],
        out_specs=cur(SWA_W), out_shape=jax.ShapeDtypeStruct((T, SWA_W), BF16), compiler_params=_cp(("arbitrary",)),
    )(p_swa, p_swa, b, sinks, cq, sq, ck, sk, ck, sk)


def _swa_bwd(p_swa, b, sinks, cq, sq, ck, sk, do):
    T = p_swa.shape[0]
    nb = T // BLK
    cur = lambda c: pl.BlockSpec((BLK, c), lambda s: (nb - 1 - s, 0))
    prev = lambda c: pl.BlockSpec((BLK, c), lambda s: (jnp.maximum(nb - 2 - s, 0), 0))

    def body(pc, pp, b_ref, s_ref, cq_r, sq_r, ckc, skc, ckp, skp, do_ref, dcur, db, dsk, carry):
        step = pl.program_id(0)
        n = nb - 1 - step

        @pl.when(step == 0)
        def _():
            carry[...] = jnp.zeros_like(carry)
            db[...] = jnp.zeros_like(db)
            dsk[...] = jnp.zeros_like(dsk)

        ops, qps = _swa_prep(pc[...], pp[...], b_ref[...], cq_r[...], sq_r[...], ckc[...], skc[...], ckp[...], skp[...])
        _, vjp = jax.vjp(functools.partial(_swa_core, n), *ops, s_ref[...], *qps)
        dk2a, dk2b, dvla, dvra, dvlb, dvrb, dsinks, *dqps = vjp(do_ref[...].astype(F32))
        dqr = jnp.concatenate(dqps, axis=1)
        lane = lax.broadcasted_iota(jnp.int32, dk2a.shape, 1)
        left = lane < HEAD
        dkb = jnp.where(left, dk2a + pltpu.roll(dk2a, HEAD, 1), dk2b + pltpu.roll(dk2b, HEAD, 1))
        dvb = jnp.where(left, dvla + pltpu.roll(dvra, HEAD, 1), pltpu.roll(dvlb, HEAD, 1) + dvrb)
        dq = dqr * cq_r[...] + _swap32(dqr * sq_r[...])
        dkp, dkc = dkb[:BLK], dkb[BLK:]
        dkp = dkp * ckp[...] + _swap32(dkp * skp[...])
        dkc = dkc * ckc[...] + _swap32(dkc * skc[...])
        dc = jnp.concatenate([dq, jnp.concatenate([dkc, dvb[BLK:]], axis=1) + carry[...]], axis=1)
        carry[...] = jnp.concatenate([dkp, dvb[:BLK]], axis=1)
        dcur[...] = dc.astype(dcur.dtype)
        db[...] += jnp.sum(dc, axis=0, keepdims=True)
        dsk[...] += dsinks

    full = lambda a: pl.BlockSpec(a.shape, lambda s: (0, 0))
    return pl.pallas_call(
        body, name="swa_bwd", grid=(nb,),
        in_specs=[cur(SWA_COLS), prev(SWA_COLS), full(b), full(sinks), cur(SWA_W), cur(SWA_W), cur(KV_W), cur(KV_W), prev(KV_W), prev(KV_W),
                  cur(SWA_W)],
        out_specs=[cur(SWA_COLS), full(b), full(sinks)],
        out_shape=[jax.ShapeDtypeStruct((T, SWA_COLS), BF16), jax.ShapeDtypeStruct(b.shape, F32), jax.ShapeDtypeStruct(sinks.shape, F32)],
        scratch_shapes=[pltpu.VMEM((BLK, 2 * KV_W), F32)], compiler_params=_cp(("arbitrary",)),
    )(p_swa, p_swa, b, sinks, cq, sq, ck, sk, ck, sk, do)


def _rope_tables(T):
    inv = 10000.0 ** (-jnp.arange(0, HEAD, 2, dtype=F32) / HEAD)
    ang = jnp.arange(T, dtype=F32)[:, None] * inv[None, :]
    c = jnp.concatenate([jnp.cos(ang), jnp.cos(ang)], axis=1)
    s = jnp.concatenate([-jnp.sin(ang), jnp.sin(ang)], axis=1)
    return jnp.tile(c, (1, 16)), jnp.tile(s, (1, 16)), jnp.tile(c, (1, 2)), jnp.tile(s, (1, 2))


def _xattn_core(*qkv):
    outs = []
    for h in range(XH):
        qh, kh, vh = qkv[h], qkv[XH + h], qkv[2 * XH + h]
        s = lax.dot_general(qh.astype(BF16), kh.astype(BF16), (((1,), (1,)), ((), ())), preferred_element_type=F32) * (XHD ** -0.5)
        p = jnp.exp(s - jnp.max(s, axis=1, keepdims=True))
        p = p / jnp.sum(p, axis=1, keepdims=True)
        outs.append(jnp.dot(p.astype(BF16), vh.astype(BF16), preferred_element_type=F32))
    return jnp.concatenate(outs, axis=1)


def _xattn_split(q, kv):
    return [q[:, h * XHD:(h + 1) * XHD] for h in range(XH)] + [kv[:, h * XHD:(h + 1) * XHD] for h in range(2 * XH)]


def _xattn_fwd(q, kv, tm=256):
    (o,) = _rows(lambda i, q, kv: (_xattn_core(*_xattn_split(q, kv)),), "xattn_fwd", q.shape[0], tm, [q], [kv], [(q.shape[1], BF16)], [])
    return o


def _xattn_bwd(q, kv, do, tm=256):
    def fn(i, q, do, kv):
        _, vjp = jax.vjp(_xattn_core, *_xattn_split(q, kv))
        d = vjp(do.astype(F32))
        return jnp.concatenate(d[:XH], axis=1), jnp.concatenate(d[XH:], axis=1)

    return _rows(fn, "xattn_bwd", q.shape[0], tm, [q, do], [kv], [(q.shape[1], BF16)], [(kv.shape, F32)])


def _loss_head(x, g, tgt, tm=256):
    D = x.shape[1]

    def fn(i, x, tgt, g):
        y, vjp = jax.vjp(_rms, x, g)
        err = y - tgt
        dx, dg = vjp(err * (1.0 / D))
        part = 0.5 / D * jnp.sum(jnp.sum(err * err, axis=1, keepdims=True), axis=0, keepdims=True)
        return dx, jnp.broadcast_to(part, (1, 128)), dg

    return _rows(fn, "loss_head", x.shape[0], tm, [x, tgt], [g], [(D, F32)], [((1, 128), F32), ((1, D), F32)])


def _local_step(x, mem, tgt, get_w, P, put_g):
    T = x.shape[0]
    W = dict(get_w("f1", None))

    def f1_down(after):
        W.update(get_w("f1d", after))
        return W["f1_down"]

    x1, s1 = _ffn_fwd(x, P["f1_norm"], W["f1_gate"], W["f1_up"], f1_down, "f1")

    W.update(get_w("mix", x1))
    h2 = _rms_fwd(x1, P["mix_norm"], "mix_norm")
    w_rkv, w_lora, w_swa = W["w_inT"][:3 * RW_W], W["w_inT"][3 * RW_W:SHIFT_COLS], W["w_inT"][SHIFT_COLS:]
    p_rkv = _mm(h2, w_rkv, "nt", "in_rkv", after=W.get("_after"))
    p_lora = _mm(h2, w_lora, "nt", "in_lora")
    p_swa = _mm(h2, w_swa, "nt", "in_swa")
    w_da = jnp.concatenate([W["rw_decay_up"], W["rw_aaa_up"]], axis=0)
    pre_params = (P["rw_mu"][:, :3 * RW_W], P["rw_mu"][:, 3 * RW_W:], P["rw_w0"], P["rw_a0"], P["rw_k_k"], P["rw_k_a"], w_da,
                  W["rw_gate_up"])
    r, decay, k2, v, an, bn, g = _rwkv_pre(p_rkv, p_lora, pre_params)
    scan_vecs = (an, decay, bn, k2, r)
    xes = [_to_perm(a) for a in scan_vecs]
    v_p = _to_perm(v)
    yi, sai, ck = _scan_fwd(xes, v_p)
    y_scan = _from_perm(yi)
    y_rw = _rwkv_post(y_scan, r, k2, v, g, P["rw_lnx_w"], P["rw_lnx_b"], P["rw_r_k"])
    cq, sq, ckt, skt = _rope_tables(T)
    y_swa = _swa_fwd(p_swa, P["b_in_attn"], P["attn_sinks"], cq, sq, ckt, skt)
    ycat = jnp.concatenate([y_rw, y_swa], axis=1)
    W.update(get_w("out", ycat))
    x2 = _mm(ycat, W["w_out"], "nn", "out_proj", res=x1, bias=P["b_out"])

    W.update(get_w("xattn", x2))
    hx = _rms_fwd(x2, P["xa_norm"], "xa_norm")
    mn = _rms_fwd(mem, P["mem_norm"], "mem_norm")
    q = _mm(hx, W["w_xq"], "nn", "xq", out_dtype=BF16)
    kv = _mm(mn, W["w_xkv"], "nn", "xkv", out_dtype=BF16)
    o = _xattn_fwd(q, kv)
    x3 = _mm(o, W["w_xo"], "nn", "xo", res=x2)

    W.update(get_w("f2", x3))
    x4, s2 = _ffn_fwd(x3, P["f2_norm"], W["f2_gate"], W["f2_up"], W["f2_down"], "f2")
    dx4, loss_part, d_final = _loss_head(x4, P["final_norm"], tgt)

    gs = {"final_norm": d_final}
    dx3, gs["f2_norm"] = _ffn_bwd(x3, P["f2_norm"], W["f2_gate"], W["f2_up"], W["f2_down"], s2, dx4, "f2", put_g)

    do = _mm(dx3, W["w_xo"], "nt", "xo_do", out_dtype=BF16)
    dw_xo = _mm(o, dx3, "tn", "xo_dw", out_dtype=BF16)
    dq, dkv = _xattn_bwd(q, kv, do)
    dw_xq = _mm(hx, dq, "tn", "xq_dw", out_dtype=BF16)
    dw_xkv = _mm(mn, dkv, "tn", "xkv_dw", out_dtype=BF16)
    sent = put_g("xattn", {"w_xq": dw_xq, "w_xkv": dw_xkv, "w_xo": dw_xo})
    dhx = _mm(dq, W["w_xq"], "nt", "xq_dh", after=sent)
    dmn = _mm(dkv, W["w_xkv"], "nt", "xkv_dmn")
    _, gs["mem_norm"], _ = _rms_bwd(mem, P["mem_norm"], dmn, jnp.zeros_like(mem), "mem_norm_bwd")
    dx2, gs["xa_norm"], gs["b_out"] = _rms_bwd(x2, P["xa_norm"], dhx, dx3, "xa_norm_bwd")

    dycat = _mm(dx2, W["w_out"], "nt", "out_dy")
    dw_out = _mm(ycat, dx2, "tn", "out_dw", out_dtype=BF16)
    dp_swa, gs["b_in_attn"], gs["attn_sinks"] = _swa_bwd(p_swa, P["b_in_attn"], P["attn_sinks"], cq, sq, ckt, skt, dycat[:, RW_W:])
    dy_scan, dr_b, dk2_b, dv_b, dg, gs["rw_lnx_w"], gs["rw_lnx_b"], gs["rw_r_k"] = _rwkv_post_bwd(
        y_scan, r, k2, v, g, P["rw_lnx_w"], P["rw_lnx_b"], P["rw_r_k"], dycat[:, :RW_W])
    dy_p = _to_perm(dy_scan)
    dsai, dvi = _scan_bwd_a(xes, dy_p)
    dj = _scan_bwd_b(xes, [v_p, sai, dy_p, dsai], _ck_a_to_b(ck))
    dan, ddecay, dbn, dk2_s, dr_s = (_from_perm(d) for d in dj)
    cts = (dr_s, ddecay, dk2_s, _from_perm(dvi), dan, dbn, dg, dr_b, dk2_b, dv_b)
    dp_rkv, dp_lora, dmu, dmul, gs["rw_w0"], gs["rw_a0"], gs["rw_k_k"], gs["rw_k_a"], dw_da, gs["rw_gate_up"] = _rwkv_pre_bwd(
        p_rkv, p_lora, pre_params, cts)
    gs["rw_mu"] = jnp.concatenate([dmu, dmul], axis=1)
    gs["rw_decay_up"], gs["rw_aaa_up"] = dw_da[:DECAY_LORA], dw_da[DECAY_LORA:]
    dw_inT = jnp.concatenate([_mm(dp_rkv, h2, "tn", "in_dw_rkv"), _mm(dp_lora, h2, "tn", "in_dw_lora"),
                              _mm(dp_swa, h2, "tn", "in_dw_swa")], axis=0)
    sent = put_g("mix", {"w_in": dw_inT, "w_out": dw_out})
    dh2 = _mm(dp_rkv, w_rkv, "nn", "in_dh_rkv", after=sent)
    dh2 = _mm(dp_lora, w_lora, "nn", "in_dh_lora", res=dh2)
    dh2 = _mm(dp_swa, w_swa, "nn", "in_dh_swa", res=dh2)
    dx1, gs["mix_norm"], _ = _rms_bwd(x1, P["mix_norm"], dh2, dx2, "mix_norm_bwd")

    dx0, gs["f1_norm"] = _ffn_bwd(x, P["f1_norm"], W["f1_gate"], W["f1_up"], W["f1_down"], s1, dx1, "f1", put_g)
    return loss_part, dx0, gs


_ANY = pl.BlockSpec(memory_space=pl.ANY)
_OTHER_CHIPS = ((1, 0), (0, 1), (1, 1))


def _mesh_pos():
    return lax.axis_index("x"), lax.axis_index("y"), lax.axis_index("c")


def _slot(ref, kind, s, rows, cols):
    if kind == "row":
        return ref.at[pl.ds(pl.multiple_of(s * rows, 8), rows), :]
    return ref.at[:, pl.ds(pl.multiple_of(s * cols, 128), cols)]


_HBM = pl.BlockSpec(memory_space=pltpu.HBM)
_SEMS = pl.BlockSpec(memory_space=pltpu.SEMAPHORE)
_SPLIT = dict(compiler_params=pltpu.CompilerParams(has_side_effects=pltpu.SideEffectType.DATAFLOW_SIDE_EFFECTING))


def _in_hbm(a):
    return pltpu.with_memory_space_constraint(a, pltpu.HBM)


def _full_shape(s, kind):
    return (4 * s.shape[0], s.shape[1]) if kind == "row" else (s.shape[0], 4 * s.shape[1])


def _half(ref, shape, h):
    rows, cols = shape
    if rows % 32 == 0:
        return ref.at[pl.ds(pl.multiple_of(h * (rows // 2), 16), rows // 2), :]
    assert cols % 256 == 0, shape
    return ref.at[:, pl.ds(pl.multiple_of(h * (cols // 2), 128), cols // 2)]


def _half_shape(shape):
    rows, cols = shape
    return (rows // 2, cols) if rows % 32 == 0 else (rows, cols // 2)


def _streams(src, dst, shape, c):
    hs = _half_shape(shape)
    s, d = _half(src, shape, c), _half(dst, shape, c)
    return [(_half(s, hs, q), _half(d, hs, q)) for q in range(2)]


def _swap_halves(name, fulls, shard_shapes, kinds):
    n = len(fulls)

    def body(*refs):
        out, send, recv = refs[n:2 * n], refs[2 * n], refs[2 * n + 1]
        x, y, c = _mesh_pos()
        sent = []
        for i in range(n):
            for r, (dx, dy) in enumerate(_OTHER_CHIPS):
                theirs = _slot(out[i], kinds[i], 2 * ((x + dx) % 2) + (y + dy) % 2, *shard_shapes[i])
                have = _half(theirs, shard_shapes[i], c)
                rc = pltpu.make_async_remote_copy(have, have, send.at[3 * i + r], recv.at[3 * i + r], device_id=(x, y, 1 - c),
                                                  device_id_type=MESH)
                rc.start()
                sent.append(rc)
        for i in range(n):
            for r, (dx, dy) in enumerate(_OTHER_CHIPS):
                theirs = _slot(out[i], kinds[i], 2 * ((x + dx) % 2) + (y + dy) % 2, *shard_shapes[i])
                need = _half(theirs, shard_shapes[i], 1 - c)
                pltpu.make_async_remote_copy(need, need, send.at[3 * i + r], recv.at[3 * i + r], device_id=(x, y, c),
                                             device_id_type=MESH).wait_recv()
        for rc in sent:
            rc.wait_send()

    return pl.pallas_call(
        body, name=name, in_specs=[_ANY] * n, out_specs=[_ANY] * n, out_shape=[jax.ShapeDtypeStruct(f.shape, f.dtype) for f in fulls],
        input_output_aliases={i: i for i in range(n)},
        scratch_shapes=[pltpu.SemaphoreType.DMA((3 * n,)), pltpu.SemaphoreType.DMA((3 * n,))],
    )(*fulls)


def _gather_start(name, shards, kinds, groups, after=None):
    n, ng = len(shards), len(groups)
    lands = [_in_hbm(lax.empty(_full_shape(s, k), s.dtype)) for s, k in zip(shards, kinds)]
    n_in = 2 * n + (after is not None)

    def body(*refs):
        src, land, sems, token = refs[:n], refs[n:2 * n], refs[n_in:n_in + 3 * ng], refs[-1]
        x, y, c = _mesh_pos()
        me = 2 * x + y
        for gi, idxs in enumerate(groups):
            send, recv, own = sems[3 * gi:3 * gi + 3]
            for k, i in enumerate(idxs):
                mine = _slot(land[i], kinds[i], me, *src[i].shape)
                for r, (dx, dy) in enumerate(_OTHER_CHIPS):
                    for q, (s, d) in enumerate(_streams(src[i], mine, src[i].shape, c)):
                        pltpu.make_async_remote_copy(s, d, send.at[6 * k + 2 * r + q], recv.at[6 * k + 2 * r + q],
                                                     device_id=((x + dx) % 2, (y + dy) % 2, c), device_id_type=MESH).start()
                pltpu.make_async_copy(src[i], mine, own.at[k]).start()
        token[...] = jnp.zeros_like(token)

    sem_shapes = [pltpu.SemaphoreType.DMA((w * len(g),)) for g in groups for w in (6, 6, 1)]
    thru = [pltpu.HBM(a.shape, a.dtype) for a in (*shards, *lands)]
    res = pl.pallas_call(
        body, name=name, in_specs=[_HBM] * (2 * n) + [_ANY] * (after is not None),
        out_specs=[_SEMS] * (3 * ng) + [_HBM] * (2 * n) + [pl.BlockSpec(memory_space=pltpu.VMEM)],
        out_shape=sem_shapes + thru + [jax.ShapeDtypeStruct((8, 128), F32)],
        input_output_aliases={i: 3 * ng + i for i in range(2 * n)}, **_SPLIT,
    )(*[_in_hbm(s) for s in shards], *lands, *([] if after is None else [after]))
    return res[:3 * ng], res[3 * ng:3 * ng + n], res[3 * ng + n:3 * ng + 2 * n], res[-1]


def _gather_wait(name, sems, shards, lands, kinds, after):
    m = len(shards)

    def body(*refs):
        src, land, (send, recv, own) = refs[:m], refs[m:2 * m], refs[2 * m:2 * m + 3]
        x, y, c = _mesh_pos()
        me = 2 * x + y
        for k in range(m):
            mine = _slot(land[k], kinds[k], me, *src[k].shape)
            for r in range(3):
                for q, (s, d) in enumerate(_streams(src[k], mine, src[k].shape, c)):
                    cp = pltpu.make_async_remote_copy(s, d, send.at[6 * k + 2 * r + q], recv.at[6 * k + 2 * r + q], device_id=(x, y, c),
                                                      device_id_type=MESH)
                    cp.wait_send()
                    cp.wait_recv()
            pltpu.make_async_copy(src[k], mine, own.at[k]).wait()

    thru = [pltpu.HBM(a.shape, a.dtype) for a in (*shards, *lands)]
    res = pl.pallas_call(
        body, name=name, in_specs=[_HBM] * (2 * m) + [_SEMS] * 3 + [pl.BlockSpec(memory_space=pl.ANY)],
        out_specs=[_HBM] * (2 * m), out_shape=thru, input_output_aliases={i: i for i in range(2 * m)}, **_SPLIT,
    )(*shards, *lands, *sems, after)
    return res[m:]


def _scatter_start(name, grads, kinds):
    m = len(grads)
    shard_shape = [(g.shape[0] // 4, g.shape[1]) if k == "row" else (g.shape[0], g.shape[1] // 4) for g, k in zip(grads, kinds)]
    lands = [_in_hbm(lax.empty((4, *s), g.dtype)) for s, g in zip(shard_shape, grads)]

    def body(*refs):
        src, land, (send, recv, own) = refs[:m], refs[m:2 * m], refs[2 * m:2 * m + 3]
        x, y, c = _mesh_pos()
        me = 2 * x + y
        for k in range(m):
            for r, (dx, dy) in enumerate(_OTHER_CHIPS):
                tx, ty = (x + dx) % 2, (y + dy) % 2
                pltpu.make_async_remote_copy(_slot(src[k], kinds[k], 2 * tx + ty, *shard_shape[k]), land[k].at[me],
                                             send.at[3 * k + r], recv.at[3 * k + r], device_id=(tx, ty, c), device_id_type=MESH).start()
            pltpu.make_async_copy(_slot(src[k], kinds[k], me, *shard_shape[k]), land[k].at[me], own.at[k]).start()
        refs[-1][...] = jnp.zeros_like(refs[-1])

    thru = [pltpu.HBM(a.shape, a.dtype) for a in (*grads, *lands)]
    res = pl.pallas_call(
        body, name=name, in_specs=[_HBM] * (2 * m),
        out_specs=[_SEMS] * 3 + [_HBM] * (2 * m) + [pl.BlockSpec(memory_space=pltpu.VMEM)],
        out_shape=[pltpu.SemaphoreType.DMA((3 * m,))] * 2 + [pltpu.SemaphoreType.DMA((m,))] + thru + [jax.ShapeDtypeStruct((8, 128), F32)],
        input_output_aliases={i: 3 + i for i in range(2 * m)}, **_SPLIT,
    )(*[_in_hbm(g) for g in grads], *lands)
    return res[:3], res[3:3 + m], res[3 + m:3 + 2 * m], res[-1]


def _scatter_wait(name, sems, grads, lands, kinds, after):
    m = len(grads)

    def body(*refs):
        src, land, (send, recv, own) = refs[:m], refs[m:2 * m], refs[2 * m:2 * m + 3]
        x, y, c = _mesh_pos()
        me = 2 * x + y
        for k in range(m):
            mine = _slot(src[k], kinds[k], me, *land[k].shape[1:])
            for r in range(3):
                cp = pltpu.make_async_remote_copy(mine, land[k].at[me], send.at[3 * k + r], recv.at[3 * k + r],
                                                  device_id=(x, y, c), device_id_type=MESH)
                cp.wait_send()
                cp.wait_recv()
            pltpu.make_async_copy(mine, land[k].at[me], own.at[k]).wait()

    thru = [pltpu.HBM(a.shape, a.dtype) for a in (*grads, *lands)]
    res = pl.pallas_call(
        body, name=name, in_specs=[_HBM] * (2 * m) + [_SEMS] * 3 + [pl.BlockSpec(memory_space=pl.ANY)],
        out_specs=[_HBM] * (2 * m), out_shape=thru, input_output_aliases={i: i for i in range(2 * m)}, **_SPLIT,
    )(*grads, *lands, *sems, after)
    return res[m:]


def _swap_with_sibling(arrs, name):
    n = len(arrs)

    def body(*refs):
        ins, outs = refs[:n], refs[n:2 * n]
        send, recv = refs[2 * n:]
        x, y, c = _mesh_pos()
        copies = []
        for i in range(n):
            rc = pltpu.make_async_remote_copy(ins[i], outs[i], send.at[i], recv.at[i], device_id=(x, y, 1 - c), device_id_type=MESH)
            rc.start()
            copies.append(rc)
        for rc in copies:
            rc.wait()

    return pl.pallas_call(
        body, name=name, in_specs=[_ANY] * n, out_specs=[_ANY] * n,
        out_shape=[jax.ShapeDtypeStruct(a.shape, a.dtype) for a in arrs],
        scratch_shapes=[pltpu.SemaphoreType.DMA((n,)), pltpu.SemaphoreType.DMA((n,))],
    )(*arrs)


def _small_start(pack, after):
    land = _in_hbm(lax.empty((8, *pack.shape), pack.dtype))

    def body(in_ref, land_ref, after_ref, send, recv, own, in_thru, land_thru, token):
        x, y, c = _mesh_pos()
        me = 4 * x + 2 * y + c
        for r in range(1, 8):
            dx, dy, dc = r // 4, (r // 2) % 2, r % 2
            pltpu.make_async_remote_copy(in_ref, land_ref.at[me], send.at[r - 1], recv.at[r - 1],
                                         device_id=((x + dx) % 2, (y + dy) % 2, (c + dc) % 2), device_id_type=MESH).start()
        pltpu.make_async_copy(in_ref, land_ref.at[me], own.at[0]).start()
        token[...] = jnp.zeros_like(token)

    res = pl.pallas_call(
        body, name="small_start", in_specs=[_HBM, _HBM, _ANY],
        out_specs=[_SEMS] * 3 + [_HBM, _HBM, pl.BlockSpec(memory_space=pltpu.VMEM)],
        out_shape=[pltpu.SemaphoreType.DMA((7,)), pltpu.SemaphoreType.DMA((7,)), pltpu.SemaphoreType.DMA((1,)),
                   pltpu.HBM(pack.shape, pack.dtype), pltpu.HBM(land.shape, land.dtype), jax.ShapeDtypeStruct((8, 128), F32)],
        input_output_aliases={0: 3, 1: 4}, **_SPLIT,
    )(_in_hbm(pack), land, after)
    return res[:3], res[3], res[4], res[5]


def _small_wait(sems, pack, land, after):
    def body(in_ref, land_ref, send, recv, own, after_ref, in_dead, got):
        x, y, c = _mesh_pos()
        me = 4 * x + 2 * y + c
        for r in range(1, 8):
            cp = pltpu.make_async_remote_copy(in_ref, land_ref.at[me], send.at[r - 1], recv.at[r - 1], device_id=(x, y, c),
                                              device_id_type=MESH)
            cp.wait_send()
            cp.wait_recv()
        pltpu.make_async_copy(in_ref, land_ref.at[me], own.at[0]).wait()

    res = pl.pallas_call(
        body, name="small_wait", in_specs=[_HBM, _HBM] + [_SEMS] * 3 + [_ANY], out_specs=[_HBM, _HBM],
        out_shape=[pltpu.HBM(pack.shape, pack.dtype), pltpu.HBM(land.shape, land.dtype)], input_output_aliases={0: 0, 1: 1}, **_SPLIT,
    )(pack, land, *sems, after)
    return res[1]


def _row_tile(R, dtype, target=256):
    mult = 8 * 4 // jnp.dtype(dtype).itemsize
    best = R
    for t in range(mult, min(R, target) + 1, mult):
        if R % t == 0:
            best = t
    return best


def _sum_slots(stack, name, out_dtype=F32):
    k, R, C = stack.shape
    tr = _row_tile(R, stack.dtype)
    tc = C
    if tr < 64:
        tr, tc = R, _pick(C, 512)

    def body(s_ref, o_ref):
        acc = s_ref[0].astype(F32)
        for j in range(1, k):
            acc = acc + s_ref[j].astype(F32)
        o_ref[...] = acc.astype(out_dtype)

    return pl.pallas_call(
        body, name=name, grid=(R // tr, C // tc), in_specs=[pl.BlockSpec((k, tr, tc), lambda i, j: (0, i, j))],
        out_specs=pl.BlockSpec((tr, tc), lambda i, j: (i, j)), out_shape=jax.ShapeDtypeStruct((R, C), out_dtype),
        compiler_params=_cp(("parallel", "parallel")),
    )(stack)


W_IN_SHARD = 1160
W_IN_PAD = 1168


def _pad_shards(a):
    zeros = jnp.zeros((W_IN_PAD - W_IN_SHARD, a.shape[1]), a.dtype)
    parts = []
    for s in range(a.shape[0] // W_IN_SHARD):
        parts += [a[s * W_IN_SHARD:(s + 1) * W_IN_SHARD], zeros]
    return jnp.concatenate(parts, axis=0).astype(BF16)


def _unpad_shards(a):
    a = a.astype(F32)
    return jnp.concatenate([a[s * W_IN_PAD:s * W_IN_PAD + W_IN_SHARD] for s in range(a.shape[0] // W_IN_PAD)], axis=0)


def _adamw(w, m, v, ga, gb, name, after=None):
    R, C = w.shape
    tr = _row_tile(R, F32, 128)
    gs = [ga] if gb is None else [ga, gb]
    extra = [] if after is None else [after]

    def body(*refs):
        w_ref, m_ref, v_ref = refs[:3]
        g = refs[3][...].astype(F32)
        if gb is not None:
            g = g + refs[4][...].astype(F32)
        g_ref, d_ref, nm_ref, nv_ref = refs[-4:]
        nm = ADAM_B1 * m_ref[...] + (1.0 - ADAM_B1) * g
        nv = ADAM_B2 * v_ref[...] + (1.0 - ADAM_B2) * (g * g)
        m_hat = nm / (1.0 - ADAM_B1 ** ADAM_STEP)
        v_hat = nv / (1.0 - ADAM_B2 ** ADAM_STEP)
        g_ref[...] = g
        d_ref[...] = -ADAM_LR * (m_hat / (jnp.sqrt(v_hat) + ADAM_EPS) + ADAM_WD * w_ref[...])
        nm_ref[...] = nm
        nv_ref[...] = nv

    spec = pl.BlockSpec((tr, C), lambda i: (i, 0))
    return pl.pallas_call(
        body, name=name, grid=(R // tr,), in_specs=[spec] * (3 + len(gs)) + [_ANY] * len(extra), out_specs=[spec] * 4,
        out_shape=[jax.ShapeDtypeStruct((R, C), F32)] * 4, compiler_params=_cp(("parallel",)),
    )(w, m, v, *gs, *extra)


def _pack(arrs):
    rows = []
    for a in arrs:
        flat = a.reshape(-1)
        rows.append(jnp.pad(flat, (0, -flat.shape[0] % 1024)).reshape(-1, 1024))
    p = jnp.concatenate(rows, axis=0)
    return jnp.pad(p, ((0, -p.shape[0] % 8), (0, 0)))


def _unpack(p, shapes):
    out, r = [], 0
    for s in shapes:
        n = 1
        for d in s:
            n *= d
        nr = -(-n // 1024)
        out.append(p[r:r + nr].reshape(-1)[:n].reshape(s))
        r += nr
    return out


BIG = ("f1_gate", "f1_up", "f1_down", "w_in", "w_out", "w_xq", "w_xkv", "w_xo", "f2_gate", "f2_up", "f2_down")
BIG_KIND = {"f1_gate": "col", "f1_up": "col", "f1_down": "row", "w_in": "row", "w_out": "row", "w_xq": "row", "w_xkv": "col",
            "w_xo": "row", "f2_gate": "col", "f2_up": "col", "f2_down": "row"}
LORA = ("rw_decay_up", "rw_aaa_up", "rw_gate_up")
WEIGHTS = ("f1_norm", "f1_gate", "f1_up", "f1_down", "mix_norm", "w_in", "b_in_attn", "rw_mu", "rw_w0", "rw_decay_up", "rw_a0",
           "rw_aaa_up", "rw_gate_up", "rw_k_k", "rw_k_a", "rw_r_k", "rw_lnx_w", "rw_lnx_b", "attn_sinks", "w_out", "b_out", "xa_norm",
           "mem_norm", "w_xq", "w_xkv", "w_xo", "f2_norm", "f2_gate", "f2_up", "f2_down", "final_norm")
SMALL = tuple(n for n in WEIGHTS if n not in BIG)
GROUP_ORDER = ("f1", "f1d", "mix", "out", "xattn", "f2")
GROUPS = {"f1": ("f1_gate", "f1_up"), "f1d": ("f1_down",), "mix": ("w_in",) + LORA, "out": ("w_out",), "xattn": ("w_xq", "w_xkv", "w_xo"),
          "f2": ("f2_gate", "f2_up", "f2_down")}


def kernel(x, mem, f1_norm, f1_gate, f1_up, f1_down, mix_norm, w_in, b_in_attn, rw_mu, rw_w0, rw_decay_up, rw_a0, rw_aaa_up, rw_gate_up, rw_k_k, rw_k_a, rw_r_k, rw_lnx_w, rw_lnx_b, attn_sinks, w_out, b_out, xa_norm, mem_norm, w_xq, w_xkv, w_xo, f2_norm, f2_gate, f2_up, f2_down, final_norm, loss_target, m_f1_norm, m_f1_gate, m_f1_up, m_f1_down, m_mix_norm, m_w_in, m_b_in_attn, m_rw_mu, m_rw_w0, m_rw_decay_up, m_rw_a0, m_rw_aaa_up, m_rw_gate_up, m_rw_k_k, m_rw_k_a, m_rw_r_k, m_rw_lnx_w, m_rw_lnx_b, m_attn_sinks, m_w_out, m_b_out, m_xa_norm, m_mem_norm, m_w_xq, m_w_xkv, m_w_xo, m_f2_norm, m_f2_gate, m_f2_up, m_f2_down, m_final_norm, v_f1_norm, v_f1_gate, v_f1_up, v_f1_down, v_mix_norm, v_w_in, v_b_in_attn, v_rw_mu, v_rw_w0, v_rw_decay_up, v_rw_a0, v_rw_aaa_up, v_rw_gate_up, v_rw_k_k, v_rw_k_a, v_rw_r_k, v_rw_lnx_w, v_rw_lnx_b, v_attn_sinks, v_w_out, v_b_out, v_xa_norm, v_mem_norm, v_w_xq, v_w_xkv, v_w_xo, v_f2_norm, v_f2_gate, v_f2_up, v_f2_down, v_final_norm):
    a = dict(locals())
    w = {n: a[n] for n in WEIGHTS}
    m = {n: a["m_" + n] for n in WEIGHTS}
    v = {n: a["v_" + n] for n in WEIGHTS}
    sq = lambda t: t.reshape(t.shape[-2:]) if t.ndim == 3 else t.reshape(1, -1)

    local_name = lambda n: "w_inT" if n == "w_in" else n
    kind_of = lambda n: BIG_KIND.get(n, "col")
    payload = lambda n: _pad_shards(sq(w[n]).T) if n == "w_in" else sq(w[n]) if n in LORA else sq(w[n]).astype(BF16)
    gathers = {}

    def start_gather(name, grps, after):
        shards = [payload(n) for g in grps for n in GROUPS[g]]
        kinds = [kind_of(n) for g in grps for n in GROUPS[g]]
        groups, at = [], 0
        for g in grps:
            groups.append(list(range(at, at + len(GROUPS[g]))))
            at += len(GROUPS[g])
        sems, src_thru, land_thru, token = _gather_start(name, shards, kinds, groups, after)
        for gi, g in enumerate(grps):
            gathers[g] = (sems[3 * gi:3 * gi + 3], [src_thru[i] for i in groups[gi]], [land_thru[i] for i in groups[gi]],
                          [kinds[i] for i in groups[gi]], token)

    early = GROUP_ORDER[:3]
    start_gather("gather_start", early, None)

    def get_w(grp, after):
        g_sems, g_src, g_land, g_kinds, token = gathers[grp]
        got = _gather_wait("gather_wait_" + grp, g_sems, g_src, g_land, g_kinds, token if after is None else after)
        got = _swap_halves("gather_swap_" + grp, got, [s.shape for s in g_src], g_kinds)
        out = {local_name(n): (_unpad_shards(f) if n == "w_in" else f) for n, f in zip(GROUPS[grp], got)}
        if grp == early[-1]:
            start_gather("gather_start_late", GROUP_ORDER[3:], got[0])
            out["_after"] = gathers[GROUP_ORDER[3]][4]
        return out

    in_flight = []

    def put_g(label, gw):
        names = list(gw)
        grads = [_pad_shards(gw[n]) if n == "w_in" else gw[n] for n in names]
        *flight, sent = _scatter_start("scatter_start_" + label, grads, [kind_of(n) for n in names])
        in_flight.append((label, names, flight))
        return sent

    P = {n: sq(w[n]) for n in SMALL if n not in LORA}
    P["attn_sinks"] = jnp.pad(P["attn_sinks"], ((0, 0), (0, 128 - P["attn_sinks"].shape[1])))
    P["rw_r_k"] = w["rw_r_k"].reshape(1, RW_W)
    loss_part, grad_x, gs = _local_step(x[0], mem[0], loss_target[0], get_w, P, put_g)

    gs["attn_sinks"] = gs["attn_sinks"][:, :16]
    small_flight = _small_start(_pack([gs[n] for n in SMALL] + [loss_part]), grad_x)

    out, after = {}, small_flight[-1]
    for bi, batch in enumerate((in_flight[:-3], in_flight[-3:])):
        b_names, b_partial = [], []
        for label, names, (g_sems, g_thru, l_thru) in batch:
            stacks = _scatter_wait("scatter_wait_" + label, g_sems, g_thru, l_thru, [kind_of(n) for n in names], after)
            partial = [_sum_slots(s, "sum_chips_" + n, F32 if n == "w_in" else BF16) for s, n in zip(stacks, names)]
            b_names += names
            b_partial += partial
            after = partial[-1]
        sibling = _swap_with_sibling(b_partial, "swap_batch%d" % bi)
        chain = None
        for n, pa, sb in zip(b_names, b_partial, sibling):
            if n == "w_in":
                pa, sb = pa[:W_IN_SHARD].T, sb[:W_IN_SHARD].T
            out[n] = _adamw(sq(w[n]), sq(m[n]), sq(v[n]), pa, sb, "adamw_" + n, after=chain)
            chain = out[n][1]
        after = chain

    gsum = _sum_slots(_small_wait(*small_flight[:-1], after), "sum_small")
    *summed, loss_row = _unpack(gsum, [gs[n].shape for n in SMALL] + [loss_part.shape])
    g_small = dict(zip(SMALL, summed))
    loss = loss_row[0, 0]
    shard = 2 * lax.axis_index("x") + lax.axis_index("y")
    for n in LORA:
        cols = w[n].shape[-1]
        g_small[n] = lax.dynamic_slice_in_dim(g_small[n], shard * cols, cols, axis=1)

    flat = lambda d: _pack([d[n] for n in SMALL])
    res = _adamw(flat(w), flat(m), flat(v), _pack([g_small[n] for n in SMALL]), None, "adamw_small")
    shapes = [w[n].shape for n in SMALL]
    for k, p in enumerate(res):
        for n, t in zip(SMALL, _unpack(p, shapes)):
            out.setdefault(n, [None] * 4)[k] = t
    outs = [loss, grad_x.reshape(x.shape)]
    for k in range(4):
        outs += [out[n][k].reshape(w[n].shape) for n in WEIGHTS]
    return tuple(outs)
```

```python
import functools

import jax
import jax.numpy as jnp
from jax import lax
from jax.experimental import pallas as pl
from jax.experimental.pallas import tpu as pltpu

F32, BF16 = jnp.float32, jnp.bfloat16
MESH = pl.DeviceIdType.MESH

HEAD = 64
RW_HEADS = 16
RW_W = 1024
SWA_W = 1024
KV_W = 128
DECAY_LORA, AAA_LORA, GATE_LORA = 64, 64, 160
LORA_W = DECAY_LORA + AAA_LORA + GATE_LORA
SHIFT_COLS = 3 * RW_W + LORA_W
XH = 4
XHD = 512
MEM_LEN = 256
WINDOW = 128
GN_EPS = 64e-5
RMS_EPS = 1e-6
NEG_INF = -1e30
ADAM_LR, ADAM_B1, ADAM_B2, ADAM_EPS, ADAM_WD, ADAM_STEP = 0.001, 0.9, 0.999, 1e-08, 0.01, 10

VMEM_LIMIT = 56 * 1024 * 1024


def _cp(sem=None, **kw):
    return pltpu.CompilerParams(dimension_semantics=sem, vmem_limit_bytes=VMEM_LIMIT, **kw)


def _pick(dim, target):
    if dim <= target:
        return dim
    best = None
    for t in range(128, target + 1, 128):
        if dim % t == 0:
            best = t
    assert best is not None, (dim, target)
    return best


_DIMS = {"nn": (((1,), (0,)), ((), ())), "nt": (((1,), (1,)), ((), ())), "tn": (((0,), (0,)), ((), ()))}


def _mm(a, b, mode, name, out_dtype=F32, alpha=1.0, res=None, bias=None, tm=1024, tn=1024, tk=2048, after=None):
    if mode == "nn":
        (M, K), (K2, N) = a.shape, b.shape
    elif mode == "nt":
        (M, K), (N, K2) = a.shape, b.shape
    else:
        (K, M), (K2, N) = a.shape, b.shape
    assert K == K2, (name, a.shape, b.shape)
    tm, tn, tk = _pick(M, tm), _pick(N, tn), _pick(K, tk)
    nk = K // tk
    a_spec = pl.BlockSpec((tk, tm), lambda i, j, k: (k, i)) if mode == "tn" else pl.BlockSpec((tm, tk), lambda i, j, k: (i, k))
    b_spec = pl.BlockSpec((tn, tk), lambda i, j, k: (j, k)) if mode == "nt" else pl.BlockSpec((tk, tn), lambda i, j, k: (k, j))
    o_spec = pl.BlockSpec((tm, tn), lambda i, j, k: (i, j))
    ins, specs = [a, b], [a_spec, b_spec]
    if res is not None:
        ins.append(res)
        specs.append(o_spec)
    if bias is not None:
        ins.append(bias)
        specs.append(pl.BlockSpec((1, tn), lambda i, j, k: (0, j)))
    if after is not None:
        ins.append(after)
        specs.append(pl.BlockSpec(memory_space=pl.ANY))
    dims = _DIMS[mode]

    def body(*refs):
        a_ref, b_ref = refs[0], refs[1]
        part = lax.dot_general(a_ref[...].astype(BF16), b_ref[...].astype(BF16), dims, preferred_element_type=F32)

        def finish(o, o_ref):
            if alpha != 1.0:
                o = o * alpha
            p = 2
            if res is not None:
                o = o + refs[p][...].astype(F32)
                p += 1
            if bias is not None:
                o = o + refs[p][...]
            o_ref[...] = o.astype(out_dtype)

        if nk == 1:
            finish(part, refs[-1])
            return
        o_ref, acc_ref = refs[-2], refs[-1]
        k = pl.program_id(2)

        @pl.when(k == 0)
        def _():
            acc_ref[...] = part

        @pl.when(k > 0)
        def _():
            acc_ref[...] += part

        @pl.when(k == nk - 1)
        def _():
            finish(acc_ref[...], o_ref)

    return pl.pallas_call(
        body, name=name, grid=(M // tm, N // tn, nk), in_specs=specs, out_specs=o_spec,
        out_shape=jax.ShapeDtypeStruct((M, N), out_dtype), scratch_shapes=[pltpu.VMEM((tm, tn), F32)] * (nk > 1),
        compiler_params=_cp(("parallel", "parallel", "arbitrary")),
    )(*ins)


def _rows(fn, name, T, tm, tiled, full, out_tiled, out_acc, extra=(), reverse=False, scratch=()):
    n = T // tm
    idx = (lambda i: n - 1 - i) if reverse else (lambda i: i)
    in_specs = [pl.BlockSpec((tm, a.shape[1]), lambda i: (idx(i), 0)) for a in tiled]
    in_specs += [mk(idx) for _, mk in extra]
    in_specs += [pl.BlockSpec(a.shape, lambda i, nd=a.ndim: (0,) * nd) for a in full]
    out_specs = [pl.BlockSpec((tm, c), lambda i: (idx(i), 0)) for c, _ in out_tiled]
    out_specs += [pl.BlockSpec(s, lambda i, nd=len(s): (0,) * nd) for s, _ in out_acc]
    out_shape = [jax.ShapeDtypeStruct((T, c), d) for c, d in out_tiled] + [jax.ShapeDtypeStruct(s, d) for s, d in out_acc]
    n_in = len(tiled) + len(extra) + len(full)
    n_t, n_a = len(out_tiled), len(out_acc)

    def body(*refs):
        step = pl.program_id(0)
        vals = [r[...] for r in refs[:n_in]]
        outs = fn(idx(step), *vals, *refs[n_in + n_t + n_a:])
        for r, v in zip(refs[n_in:n_in + n_t], outs[:n_t]):
            r[...] = v.astype(r.dtype)
        for r, v in zip(refs[n_in + n_t:n_in + n_t + n_a], outs[n_t:]):
            @pl.when(step == 0)
            def _(r=r):
                r[...] = jnp.zeros_like(r)

            r[...] += v

    return pl.pallas_call(
        body, name=name, grid=(n,), in_specs=in_specs, out_specs=out_specs, out_shape=out_shape,
        scratch_shapes=list(scratch), compiler_params=_cp(("arbitrary",)),
    )(*tiled, *[a for a, _ in extra], *full)


def _rms(x, g):
    return x * lax.rsqrt(jnp.mean(x * x, axis=-1, keepdims=True) + RMS_EPS) * g


def _rms_fwd(x, g, name, tm=256):
    (h,) = _rows(lambda i, x, g: (_rms(x, g),), name, x.shape[0], min(tm, x.shape[0]), [x], [g], [(x.shape[1], BF16)], [])
    return h


def _rms_bwd(x, g, dh, dres, name, tm=256):
    D = x.shape[1]

    def fn(i, x, dh, dres, g):
        _, vjp = jax.vjp(_rms, x, g)
        dx, dg = vjp(dh.astype(F32))
        dx = dx + dres
        return dx, dg, jnp.sum(dx, axis=0, keepdims=True)

    return _rows(fn, name, x.shape[0], tm, [x, dh, dres], [g], [(D, F32)], [((1, D), F32), ((1, D), F32)])


def _ffn_up(h, wg, wu, name, tm=1024, tn=512, after=None):
    (M, K), N = h.shape, wg.shape[1]
    tm, tn = _pick(M, tm), _pick(N, tn)

    def body(*refs):
        h_ref, wg_ref, wu_ref = refs[:3]
        g_ref, u_ref, a_ref = refs[-3:]
        hb = h_ref[...].astype(BF16)
        g = jnp.dot(hb, wg_ref[...].astype(BF16), preferred_element_type=F32)
        u = jnp.dot(hb, wu_ref[...].astype(BF16), preferred_element_type=F32)
        g_ref[...] = g
        u_ref[...] = u
        a_ref[...] = (g * jax.nn.sigmoid(g) * u).astype(BF16)

    o_spec = pl.BlockSpec((tm, tn), lambda i, j: (i, j))
    w_spec = pl.BlockSpec((K, tn), lambda i, j: (0, j))
    extra = [] if after is None else [after]
    return pl.pallas_call(
        body, name=name, grid=(M // tm, N // tn),
        in_specs=[pl.BlockSpec((tm, K), lambda i, j: (i, 0)), w_spec, w_spec] + [pl.BlockSpec(memory_space=pl.ANY)] * len(extra),
        out_specs=[o_spec] * 3, out_shape=[jax.ShapeDtypeStruct((M, N), F32)] * 2 + [jax.ShapeDtypeStruct((M, N), BF16)],
        compiler_params=_cp(("parallel", "parallel")),
    )(h, wg, wu, *extra)


def _ffn_dact(dxo, wd, g, u, name, tm=1024, tn=512):
    (M, K), N = dxo.shape, wd.shape[0]
    tm, tn = _pick(M, tm), _pick(N, tn)

    def body(dx_ref, wd_ref, g_ref, u_ref, dg_ref, du_ref):
        da = 0.5 * lax.dot_general(dx_ref[...].astype(BF16), wd_ref[...].astype(BF16), _DIMS["nt"], preferred_element_type=F32)
        g = g_ref[...]
        s = jax.nn.sigmoid(g)
        dg_ref[...] = (da * u_ref[...] * (s * (1.0 + g * (1.0 - s)))).astype(BF16)
        du_ref[...] = (da * (g * s)).astype(BF16)

    t_spec = pl.BlockSpec((tm, tn), lambda i, j: (i, j))
    return pl.pallas_call(
        body, name=name, grid=(M // tm, N // tn),
        in_specs=[pl.BlockSpec((tm, K), lambda i, j: (i, 0)), pl.BlockSpec((tn, K), lambda i, j: (j, 0)), t_spec, t_spec],
        out_specs=[t_spec, t_spec], out_shape=[jax.ShapeDtypeStruct((M, N), BF16)] * 2, compiler_params=_cp(("parallel", "parallel")),
    )(dxo, wd, g, u)


def _ffn_fwd(x, gain, wg, wu, wd, tag, after=None):
    h = _rms_fwd(x, gain, tag + "_norm")
    G, U, A = _ffn_up(h, wg, wu, tag + "_up", after=after)
    xo = _mm(A, wd(A) if callable(wd) else wd, "nn", tag + "_down", alpha=0.5, res=x)
    return xo, (h, G, U, A)


def _ffn_bwd(x, gain, wg, wu, wd, saved, dxo, tag, send):
    h, G, U, A = saved
    dwd = _mm(A, dxo, "tn", tag + "_dwd", out_dtype=BF16, alpha=0.5, tm=1408)
    sent = send(tag + "_down", {tag + "_down": dwd})
    dG, dU = _ffn_dact(dxo, wd, G, U, tag + "_dact")
    dwu = _mm(h, dU, "tn", tag + "_dwu", out_dtype=BF16, after=sent)
    sent = send(tag + "_up", {tag + "_up": dwu})
    dwg = _mm(h, dG, "tn", tag + "_dwg", out_dtype=BF16, after=sent)
    sent = send(tag + "_gate", {tag + "_gate": dwg})
    dh = _mm(dG, wg, "nt", tag + "_dh_g", after=sent)
    dh = _mm(dU, wu, "nt", tag + "_dh_u", res=dh)
    dx, dgain, _ = _rms_bwd(x, gain, dh, dxo, tag + "_norm_bwd")
    return dx, dgain


def _segsum64_impl(x):
    r = lax.broadcasted_iota(jnp.int32, (128, 128), 0) // HEAD
    c = lax.broadcasted_iota(jnp.int32, (128, 128), 1) // HEAD
    ones = (r == c).astype(BF16)
    hi = x.astype(BF16)
    lo = (x - hi.astype(F32)).astype(BF16)
    outs = []
    for q in range(x.shape[1] // 128):
        sl = slice(q * 128, (q + 1) * 128)
        outs.append(jnp.dot(hi[:, sl], ones, preferred_element_type=F32) + jnp.dot(lo[:, sl], ones, preferred_element_type=F32))
    return outs[0] if len(outs) == 1 else jnp.concatenate(outs, axis=1)


@jax.custom_vjp
def _segsum64(x):
    return _segsum64_impl(x)


_segsum64.defvjp(lambda x: (_segsum64_impl(x), None), lambda _, ct: (_segsum64_impl(ct),))


def _swap32(x):
    lane = lax.broadcasted_iota(jnp.int32, (x.shape[0], 128), 1)
    outs = [jnp.take_along_axis(x[:, q * 128:(q + 1) * 128], lane ^ 32, axis=1) for q in range(x.shape[1] // 128)]
    return outs[0] if len(outs) == 1 else jnp.concatenate(outs, axis=1)


def _tree_sum(xs):
    xs = list(xs)
    while len(xs) > 1:
        nxt = [xs[i] + xs[i + 1] for i in range(0, len(xs) - 1, 2)]
        if len(xs) % 2:
            nxt.append(xs[-1])
        xs = nxt
    return xs[0]


class _Acc:
    def __init__(self, ways=4):
        self.parts = [None] * ways

    def add(self, i, term):
        k = i % len(self.parts)
        self.parts[k] = term if self.parts[k] is None else self.parts[k] + term

    def total(self):
        return _tree_sum([p for p in self.parts if p is not None])


def _softplus(x):
    return jnp.maximum(x, 0.0) + jnp.log(1.0 + jnp.exp(-jnp.abs(x)))


def _pre_core(k, da, gd, w0, a0, k_k, k_a, w_da, gate_up):
    lane = lax.broadcasted_iota(jnp.int32, da.shape, 1)
    w_da = w_da.astype(BF16)
    l1 = jnp.dot(jnp.where(lane < DECAY_LORA, jnp.tanh(da), 0.0).astype(BF16), w_da, preferred_element_type=F32)
    l2 = jnp.dot(jnp.where(lane >= DECAY_LORA, da, 0.0).astype(BF16), w_da, preferred_element_type=F32)
    wlog = -_softplus(-(w0 + l1)) - 0.5
    decay = jnp.exp(-jnp.exp(wlog))
    a = jax.nn.sigmoid(a0 + l2)
    g = jnp.dot(jax.nn.sigmoid(gd).astype(BF16), gate_up.astype(BF16), preferred_element_type=F32)
    kk = k * k_k
    kkn = kk / jnp.maximum(jnp.sqrt(_segsum64(kk * kk)), 1e-12)
    k2 = k * (1.0 + (a - 1.0) * k_a)
    return decay, k2, -kkn, kkn * a, g


def _pre_shift(i, zr, zl, zr8, zl8, mu, mul):
    live = (i > 0).astype(F32)
    dz = _shift_down(zr, zr8[7:8, :] * live) - zr
    dzl = _shift_down(zl, zl8[7:8, :] * live) - zl
    return zr + dz * mu, zl + dzl * mul, dz, dzl


def _shift_down(x, first_row):
    rolled = pltpu.roll(x, 1, 0)
    row = lax.broadcasted_iota(jnp.int32, x.shape, 0)
    return jnp.where(row == 0, first_row, rolled)


def _shift_up(x, last_row):
    rolled = pltpu.roll(x, x.shape[0] - 1, 0)
    row = lax.broadcasted_iota(jnp.int32, x.shape, 0)
    return jnp.where(row == x.shape[0] - 1, last_row, rolled)


def _prev_rows_spec(tm, cols):
    return lambda idx: pl.BlockSpec((8, cols), lambda i: (jnp.maximum(idx(i) * (tm // 8) - 1, 0), 0))


def _rwkv_pre(p_rkv, p_lora, params, tm=256):
    T = p_rkv.shape[0]

    def fn(i, zr, zl, zr8, zl8, mu, mul, *ps):
        z, z2, _, _ = _pre_shift(i, zr, zl, zr8, zl8, mu, mul)
        decay, k2, an, bn, g = _pre_core(z[:, RW_W:2 * RW_W], z2[:, :128], z2[:, 128:], *ps)
        return z[:, :RW_W], decay, k2, z[:, 2 * RW_W:], an, bn, g

    extra = [(p_rkv, _prev_rows_spec(tm, 3 * RW_W)), (p_lora, _prev_rows_spec(tm, LORA_W))]
    return _rows(fn, "rwkv_pre", T, tm, [p_rkv, p_lora], list(params), [(RW_W, F32)] * 7, [], extra=extra)


def _rwkv_pre_bwd(p_rkv, p_lora, params, cts, tm=256):
    T = p_rkv.shape[0]
    n = T // tm

    def fn(i, zr, zl, cr, cdec, ck2, cv, can, cbn, cg, cr_b, ck2_b, cv_b, zr8, zl8, mu, mul, *rest):
        ps, (car, carl) = rest[:-2], rest[-2:]
        cr, ck2, cv = cr + cr_b, ck2 + ck2_b, cv + cv_b
        z, z2, dif, difl = _pre_shift(i, zr, zl, zr8, zl8, mu, mul)
        _, vjp = jax.vjp(_pre_core, z[:, RW_W:2 * RW_W], z2[:, :128], z2[:, 128:], *ps)
        dk, dda, dgd, *dps = vjp((cdec, ck2, can, cbn, cg))
        dz = jnp.concatenate([cr, dk, cv], axis=1)
        dz2 = jnp.concatenate([dda, dgd], axis=1)
        dzp, dzlp = dz * mu, dz2 * mul

        @pl.when(i == n - 1)
        def _():
            car[...] = jnp.zeros_like(car)
            carl[...] = jnp.zeros_like(carl)

        d_rkv = dz - dzp + _shift_up(dzp, car[0:1, :])
        d_lora = dz2 - dzlp + _shift_up(dzlp, carl[0:1, :])
        car[0:1, :] = dzp[0:1, :]
        carl[0:1, :] = dzlp[0:1, :]
        return (d_rkv, d_lora, jnp.sum(dz * dif, axis=0, keepdims=True), jnp.sum(dz2 * difl, axis=0, keepdims=True), *dps)

    extra = [(p_rkv, _prev_rows_spec(tm, 3 * RW_W)), (p_lora, _prev_rows_spec(tm, LORA_W))]
    acc = [(p.shape, F32) for p in params]
    return _rows(fn, "rwkv_pre_bwd", T, tm, [p_rkv, p_lora, *cts], list(params), [(3 * RW_W, BF16), (LORA_W, BF16)], acc,
                 extra=extra, reverse=True, scratch=[pltpu.VMEM((8, 3 * RW_W), F32), pltpu.VMEM((8, LORA_W), F32)])


def _post_core(y, r, k2, v, g, lw, lb, rk):
    mu = _segsum64(y) * (1.0 / HEAD)
    yc = y - mu
    var = _segsum64(yc * yc) * (1.0 / HEAD)
    yn = yc * lax.rsqrt(var + GN_EPS) * lw + lb
    return (yn + _segsum64(r * k2 * rk) * v) * g


def _rwkv_post(y, r, k2, v, g, lw, lb, rk, tm=256):
    (o,) = _rows(lambda i, *a: (_post_core(*a),), "rwkv_post", y.shape[0], tm, [y, r, k2, v, g], [lw, lb, rk], [(RW_W, BF16)], [])
    return o


def _rwkv_post_bwd(y, r, k2, v, g, lw, lb, rk, do, tm=256):
    def fn(i, y, r, k2, v, g, do, lw, lb, rk):
        _, vjp = jax.vjp(_post_core, y, r, k2, v, g, lw, lb, rk)
        return vjp(do.astype(F32))

    return _rows(fn, "rwkv_post_bwd", y.shape[0], tm, [y, r, k2, v, g, do], [lw, lb, rk], [(RW_W, F32)] * 5, [((1, RW_W), F32)] * 3)


SCAN_L = 64


def _to_perm(x):
    T = x.shape[0]
    return x.reshape(T, RW_HEADS, HEAD).transpose(0, 2, 1).reshape(T, 8, 128)


def _from_perm(x):
    T = x.shape[0]
    return x.reshape(T, HEAD, RW_HEADS).transpose(0, 2, 1).reshape(T, RW_W)


def _as_tile(p):
    lane = lax.broadcasted_iota(jnp.int32, (8, 128), 1)
    return jnp.take_along_axis(p, (lane % 8) * 16 + lane // 8, axis=1)


def _as_perm(t):
    lane = lax.broadcasted_iota(jnp.int32, (8, 128), 1)
    return jnp.take_along_axis(t, (lane % 16) * 8 + lane // 16, axis=1)


def _tiles_to_perm(refs, L):
    for r in refs:
        for t in range(L):
            r[t] = _as_perm(r[t])


def _expander(srcs, tiles=()):
    s = lax.broadcasted_iota(jnp.int32, (8, 128), 0)
    lane = lax.broadcasted_iota(jnp.int32, (8, 128), 1)
    idx = 16 * s + lane // 8

    def expand(t, e_ref):
        for m, r in enumerate(srcs):
            for g in range(8):
                row = jnp.broadcast_to(r[t, pl.ds(g, 1), :], (8, 128))
                e_ref[m, g * 8:(g + 1) * 8, :] = jnp.take_along_axis(row, idx, axis=1)
        for k, r in enumerate(tiles):
            e_ref[len(srcs) + k, 0:8, :] = _as_tile(r[t])

    return expand


def _ck_a_to_b(ck):
    n = ck.shape[0]
    return ck.reshape(n, 8, 8, 8, RW_HEADS, 8).transpose(0, 2, 5, 1, 4, 3).reshape(n, HEAD, 8, 128)


def _bc(row):
    return jnp.broadcast_to(row, (8, 128))


def _rsum(x):
    return jnp.sum(x, axis=0, keepdims=True)


def _plus(acc, k, term):
    acc[k] = term if acc[k] is None else acc[k] + term


def _scan_fwd(xes, vi):
    T, L = vi.shape[0], SCAN_L
    nch = T // L

    def body(*refs):
        xr, (vi_ref, yi_ref, sa_ref, ck_ref, st_ref, e0, e1) = refs[:5], refs[5:]

        @pl.when(pl.program_id(0) == 0)
        def _():
            st_ref[...] = jnp.zeros_like(st_ref)

        ck_ref[0] = st_ref[...]
        expand = _expander(xr, [vi_ref])
        expand(0, e0)

        def step(t, e):
            tile = lambda m, jh: e[m, 8 * jh:8 * jh + 8, :]
            vb = [_bc(e[5, ih:ih + 1, :]) for ih in range(8)]
            acc = [None] * 8
            for jh in range(8):
                a = tile(0, jh)
                for ih in range(8):
                    _plus(acc, ih, st_ref[8 * jh + ih] * a)
            sab = []
            for ih in range(8):
                row = _rsum(acc[ih])
                sa_ref[t, ih:ih + 1, :] = row
                sab.append(_bc(row))
            yacc = [None] * 8
            for jh in range(8):
                w, B, k, r = tile(1, jh), tile(2, jh), tile(3, jh), tile(4, jh)
                for ih in range(8):
                    s = st_ref[8 * jh + ih] * w + B * sab[ih] + k * vb[ih]
                    st_ref[8 * jh + ih] = s
                    _plus(yacc, ih, s * r)
            for ih in range(8):
                yi_ref[t, ih:ih + 1, :] = _rsum(yacc[ih])

        def pair(p, carry):
            t = 2 * p
            expand(t + 1, e1)
            step(t, e0)
            expand(jnp.minimum(t + 2, L - 1), e0)
            step(t + 1, e1)
            return carry

        lax.fori_loop(0, L // 2, pair, 0)
        _tiles_to_perm([yi_ref, sa_ref], L)

    tile = pl.BlockSpec((L, 8, 128), lambda c: (c, 0, 0))
    return pl.pallas_call(
        body, name="rwkv_scan_fwd", grid=(nch,), in_specs=[tile] * 6,
        out_specs=[tile, tile, pl.BlockSpec((1, HEAD, 8, 128), lambda c: (c, 0, 0, 0))],
        out_shape=[jax.ShapeDtypeStruct((T, 8, 128), F32)] * 2 + [jax.ShapeDtypeStruct((nch, HEAD, 8, 128), F32)],
        scratch_shapes=[pltpu.VMEM((HEAD, 8, 128), F32)] + [pltpu.VMEM((6, HEAD, 128), F32)] * 2, compiler_params=_cp(("arbitrary",)),
    )(*xes, vi)


def _scan_bwd_a(xes, dyi):
    T, L = dyi.shape[0], SCAN_L
    nch = T // L

    def body(*refs):
        xr, (dy_ref, dsa_ref, dv_ref, g_ref, e0, e1) = refs[:5], refs[5:]

        @pl.when(pl.program_id(0) == 0)
        def _():
            g_ref[...] = jnp.zeros_like(g_ref)

        expand = _expander(xr, [dy_ref])
        expand(L - 1, e0)

        def step(t, e):
            tile = lambda m, jh: e[m, 8 * jh:8 * jh + 8, :]
            dyb = [_bc(e[5, ih:ih + 1, :]) for ih in range(8)]
            dsa, dv = [None] * 8, [None] * 8
            for jh in range(8):
                B, k, r = tile(2, jh), tile(3, jh), tile(4, jh)
                for ih in range(8):
                    g = g_ref[8 * jh + ih] + r * dyb[ih]
                    g_ref[8 * jh + ih] = g
                    _plus(dsa, ih, g * B)
                    _plus(dv, ih, g * k)
            dsab = []
            for ih in range(8):
                row = _rsum(dsa[ih])
                dsa_ref[t, ih:ih + 1, :] = row
                dsab.append(_bc(row))
                dv_ref[t, ih:ih + 1, :] = _rsum(dv[ih])
            for jh in range(8):
                A, w = tile(0, jh), tile(1, jh)
                for ih in range(8):
                    g_ref[8 * jh + ih] = g_ref[8 * jh + ih] * w + A * dsab[ih]

        def pair(p, carry):
            t = L - 1 - 2 * p
            expand(t - 1, e1)
            step(t, e0)
            expand(jnp.maximum(t - 2, 0), e0)
            step(t - 1, e1)
            return carry

        lax.fori_loop(0, L // 2, pair, 0)
        _tiles_to_perm([dsa_ref, dv_ref], L)

    tile = pl.BlockSpec((L, 8, 128), lambda c: (nch - 1 - c, 0, 0))
    return pl.pallas_call(
        body, name="rwkv_scan_bwd_a", grid=(nch,), in_specs=[tile] * 6, out_specs=[tile, tile],
        out_shape=[jax.ShapeDtypeStruct((T, 8, 128), F32)] * 2,
        scratch_shapes=[pltpu.VMEM((HEAD, 8, 128), F32)] + [pltpu.VMEM((6, HEAD, 128), F32)] * 2, compiler_params=_cp(("arbitrary",)),
    )(*xes, dyi)


def _scan_bwd_b(xts, ies, ckb):
    T, L = xts[0].shape[0], SCAN_L
    nch = T // L

    def body(*refs):
        xr, er, ck_ref, dj, (hist, g_ref, e0, e1) = refs[:5], refs[5:9], refs[9], refs[10:15], refs[15:]

        @pl.when(pl.program_id(0) == 0)
        def _():
            g_ref[...] = jnp.zeros_like(g_ref)

        hist[0] = ck_ref[0]
        expand_vs = _expander(er[:2], [xr[1], xr[2], xr[3]])
        expand = _expander(er, [xr[0], xr[1], xr[4]])
        expand_vs(0, e0)

        def fstep(t, e_ref):
            w, B, k = e_ref[2, 0:8, :], e_ref[3, 0:8, :], e_ref[4, 0:8, :]
            row = lambda m, i: jnp.broadcast_to(e_ref[m, pl.ds(i, 1), :], (8, 128))
            for i in range(HEAD):
                hist[t + 1, i] = hist[t, i] * w + row(1, i) * B + row(0, i) * k

        def fpair(p, carry):
            t = 2 * p
            expand_vs(t + 1, e1)
            fstep(t, e0)
            expand_vs(jnp.minimum(t + 2, L - 1), e0)
            fstep(t + 1, e1)
            return carry

        lax.fori_loop(0, L // 2, fpair, 0)
        expand(L - 1, e0)

        def bstep(t, e_ref):
            A, w, r = e_ref[4, 0:8, :], e_ref[5, 0:8, :], e_ref[6, 0:8, :]
            row = lambda m, i: jnp.broadcast_to(e_ref[m, pl.ds(i, 1), :], (8, 128))
            acc = [_Acc() for _ in range(5)]
            for i in range(HEAD):
                dy_i, dsa_i = row(2, i), row(3, i)
                g = g_ref[i] + dy_i * r
                sp = hist[t, i]
                acc[4].add(i, hist[t + 1, i] * dy_i)
                acc[1].add(i, g * sp)
                acc[2].add(i, g * row(1, i))
                acc[3].add(i, g * row(0, i))
                acc[0].add(i, sp * dsa_i)
                g_ref[i] = g * w + dsa_i * A
            for m in range(5):
                dj[m][t] = acc[m].total()

        def bpair(p, carry):
            t = L - 1 - 2 * p
            expand(t - 1, e1)
            bstep(t, e0)
            expand(jnp.maximum(t - 2, 0), e0)
            bstep(t - 1, e1)
            return carry

        lax.fori_loop(0, L // 2, bpair, 0)
        _tiles_to_perm(dj, L)

    tile = pl.BlockSpec((L, 8, 128), lambda c: (nch - 1 - c, 0, 0))
    return pl.pallas_call(
        body, name="rwkv_scan_bwd_b", grid=(nch,),
        in_specs=[tile] * 9 + [pl.BlockSpec((1, HEAD, 8, 128), lambda c: (nch - 1 - c, 0, 0, 0))],
        out_specs=[tile] * 5, out_shape=[jax.ShapeDtypeStruct((T, 8, 128), F32)] * 5,
        scratch_shapes=[pltpu.VMEM((L + 1, HEAD, 8, 128), F32), pltpu.VMEM((HEAD, 8, 128), F32)] + [pltpu.VMEM((7, HEAD, 128), F32)] * 2,
        compiler_params=_cp(("arbitrary",)),
    )(*xts, *ies, ckb)


SWA_COLS = SWA_W + 2 * KV_W
BLK = 128


def _swa_core(n, k2a, k2b, vla, vra, vlb, vrb, sinks, *qps):
    iq = lax.broadcasted_iota(jnp.int32, (BLK, 2 * BLK), 0)
    ik = lax.broadcasted_iota(jnp.int32, (BLK, 2 * BLK), 1)
    diff = BLK + iq - ik
    valid = (diff >= 0) & (diff < WINDOW) & ((n > 0) | (ik >= BLK))
    lane = lax.broadcasted_iota(jnp.int32, (BLK, 128), 1)
    lane1 = lax.broadcasted_iota(jnp.int32, (1, 128), 1)
    nt = (((1,), (1,)), ((), ()))
    outs = []
    for pp in range(8):
        k2, vl, vr = (k2a, vla, vra) if pp < 4 else (k2b, vlb, vrb)
        qp = qps[pp]
        o = None
        for half, vv in ((0, vl), (1, vr)):
            qh = jnp.where((lane >= HEAD) == (half == 1), qp, 0.0).astype(BF16)
            s = lax.dot_general(qh, k2.astype(BF16), nt, preferred_element_type=F32) * (HEAD ** -0.5)
            s = jnp.where(valid, s, NEG_INF)
            sink = jnp.sum(jnp.where(lane1 == 2 * pp + half, sinks, 0.0), axis=1, keepdims=True)
            m = jnp.maximum(jnp.max(s, axis=1, keepdims=True), sink)
            p = jnp.exp(s - m)
            den = jnp.sum(p, axis=1, keepdims=True) + jnp.exp(sink - m)
            oh = jnp.dot((p / den).astype(BF16), vv.astype(BF16), preferred_element_type=F32)
            o = oh if o is None else o + oh
        outs.append(o)
    return jnp.concatenate(outs, axis=1)


def _swa_prep(pc, pp, b, cq, sq, ckc, skc, ckp, skp):
    zc, zp = pc + b, pp + b
    qr = zc[:, :SWA_W] * cq + _swap32(zc[:, :SWA_W]) * sq
    kc, kp = zc[:, SWA_W:SWA_W + KV_W], zp[:, SWA_W:SWA_W + KV_W]
    kb = jnp.concatenate([kp * ckp + _swap32(kp) * skp, kc * ckc + _swap32(kc) * skc], axis=0)
    vb = jnp.concatenate([zp[:, SWA_W + KV_W:], zc[:, SWA_W + KV_W:]], axis=0)
    lane = lax.broadcasted_iota(jnp.int32, kb.shape, 1)
    left = lane < HEAD
    kbr, vbr = pltpu.roll(kb, HEAD, 1), pltpu.roll(vb, HEAD, 1)
    return (jnp.where(left, kb, kbr), jnp.where(left, kbr, kb), jnp.where(left, vb, 0.0), jnp.where(left, 0.0, vbr),
            jnp.where(left, vbr, 0.0), jnp.where(left, 0.0, vb)), [qr[:, q * 128:(q + 1) * 128] for q in range(8)]


def _swa_specs(T, tabs_q, tabs_k):
    cur = lambda c: pl.BlockSpec((BLK, c), lambda n: (n, 0))
    prev = lambda c: pl.BlockSpec((BLK, c), lambda n: (jnp.maximum(n - 1, 0), 0))
    return cur, prev


def _swa_fwd(p_swa, b, sinks, cq, sq, ck, sk):
    T = p_swa.shape[0]
    cur, prev = _swa_specs(T, None, None)

    def body(pc, pp, b_ref, s_ref, cq_r, sq_r, ckc, skc, ckp, skp, o_ref):
        ops, qps = _swa_prep(pc[...], pp[...], b_ref[...], cq_r[...], sq_r[...], ckc[...], skc[...], ckp[...], skp[...])
        o_ref[...] = _swa_core(pl.program_id(0), *ops, s_ref[...], *qps).astype(o_ref.dtype)

    full = lambda a: pl.BlockSpec(a.shape, lambda n: (0, 0))
    return pl.pallas_call(
        body, name="swa_fwd", grid=(T // BLK,),
        in_specs=[cur(SWA_COLS), prev(SWA_COLS), full(b), full(sinks), cur(SWA_W), cur(SWA_W), cur(KV_W), cur(KV_W), prev(KV_W), prev(KV_W)],
        out_specs=cur(SWA_W), out_shape=jax.ShapeDtypeStruct((T, SWA_W), BF16), compiler_params=_cp(("arbitrary",)),
    )(p_swa, p_swa, b, sinks, cq, sq, ck, sk, ck, sk)


def _swa_bwd(p_swa, b, sinks, cq, sq, ck, sk, do):
    T = p_swa.shape[0]
    nb = T // BLK
    cur = lambda c: pl.BlockSpec((BLK, c), lambda s: (nb - 1 - s, 0))
    prev = lambda c: pl.BlockSpec((BLK, c), lambda s: (jnp.maximum(nb - 2 - s, 0), 0))

    def body(pc, pp, b_ref, s_ref, cq_r, sq_r, ckc, skc, ckp, skp, do_ref, dcur, db, dsk, carry):
        step = pl.program_id(0)
        n = nb - 1 - step

        @pl.when(step == 0)
        def _():
            carry[...] = jnp.zeros_like(carry)
            db[...] = jnp.zeros_like(db)
            dsk[...] = jnp.zeros_like(dsk)

        ops, qps = _swa_prep(pc[...], pp[...], b_ref[...], cq_r[...], sq_r[...], ckc[...], skc[...], ckp[...], skp[...])
        _, vjp = jax.vjp(functools.partial(_swa_core, n), *ops, s_ref[...], *qps)
        dk2a, dk2b, dvla, dvra, dvlb, dvrb, dsinks, *dqps = vjp(do_ref[...].astype(F32))
        dqr = jnp.concatenate(dqps, axis=1)
        lane = lax.broadcasted_iota(jnp.int32, dk2a.shape, 1)
        left = lane < HEAD
        dkb = jnp.where(left, dk2a + pltpu.roll(dk2a, HEAD, 1), dk2b + pltpu.roll(dk2b, HEAD, 1))
        dvb = jnp.where(left, dvla + pltpu.roll(dvra, HEAD, 1), pltpu.roll(dvlb, HEAD, 1) + dvrb)
        dq = dqr * cq_r[...] + _swap32(dqr * sq_r[...])
        dkp, dkc = dkb[:BLK], dkb[BLK:]
        dkp = dkp * ckp[...] + _swap32(dkp * skp[...])
        dkc = dkc * ckc[...] + _swap32(dkc * skc[...])
        dc = jnp.concatenate([dq, jnp.concatenate([dkc, dvb[BLK:]], axis=1) + carry[...]], axis=1)
        carry[...] = jnp.concatenate([dkp, dvb[:BLK]], axis=1)
        dcur[...] = dc.astype(dcur.dtype)
        db[...] += jnp.sum(dc, axis=0, keepdims=True)
        dsk[...] += dsinks

    full = lambda a: pl.BlockSpec(a.shape, lambda s: (0, 0))
    return pl.pallas_call(
        body, name="swa_bwd", grid=(nb,),
        in_specs=[cur(SWA_COLS), prev(SWA_COLS), full(b), full(sinks), cur(SWA_W), cur(SWA_W), cur(KV_W), cur(KV_W), prev(KV_W), prev(KV_W),
                  cur(SWA_W)],
        out_specs=[cur(SWA_COLS), full(b), full(sinks)],
        out_shape=[jax.ShapeDtypeStruct((T, SWA_COLS), BF16), jax.ShapeDtypeStruct(b.shape, F32), jax.ShapeDtypeStruct(sinks.shape, F32)],
        scratch_shapes=[pltpu.VMEM((BLK, 2 * KV_W), F32)], compiler_params=_cp(("arbitrary",)),
    )(p_swa, p_swa, b, sinks, cq, sq, ck, sk, ck, sk, do)


def _rope_tables(T):
    inv = 10000.0 ** (-jnp.arange(0, HEAD, 2, dtype=F32) / HEAD)
    ang = jnp.arange(T, dtype=F32)[:, None] * inv[None, :]
    c = jnp.concatenate([jnp.cos(ang), jnp.cos(ang)], axis=1)
    s = jnp.concatenate([-jnp.sin(ang), jnp.sin(ang)], axis=1)
    return jnp.tile(c, (1, 16)), jnp.tile(s, (1, 16)), jnp.tile(c, (1, 2)), jnp.tile(s, (1, 2))


def _xattn_core(*qkv):
    outs = []
    for h in range(XH):
        qh, kh, vh = qkv[h], qkv[XH + h], qkv[2 * XH + h]
        s = lax.dot_general(qh.astype(BF16), kh.astype(BF16), (((1,), (1,)), ((), ())), preferred_element_type=F32) * (XHD ** -0.5)
        p = jnp.exp(s - jnp.max(s, axis=1, keepdims=True))
        p = p / jnp.sum(p, axis=1, keepdims=True)
        outs.append(jnp.dot(p.astype(BF16), vh.astype(BF16), preferred_element_type=F32))
    return jnp.concatenate(outs, axis=1)


def _xattn_split(q, kv):
    return [q[:, h * XHD:(h + 1) * XHD] for h in range(XH)] + [kv[:, h * XHD:(h + 1) * XHD] for h in range(2 * XH)]


def _xattn_fwd(q, kv, tm=256):
    (o,) = _rows(lambda i, q, kv: (_xattn_core(*_xattn_split(q, kv)),), "xattn_fwd", q.shape[0], tm, [q], [kv], [(q.shape[1], BF16)], [])
    return o


def _xattn_bwd(q, kv, do, tm=256):
    def fn(i, q, do, kv):
        _, vjp = jax.vjp(_xattn_core, *_xattn_split(q, kv))
        d = vjp(do.astype(F32))
        return jnp.concatenate(d[:XH], axis=1), jnp.concatenate(d[XH:], axis=1)

    return _rows(fn, "xattn_bwd", q.shape[0], tm, [q, do], [kv], [(q.shape[1], BF16)], [(kv.shape, F32)])


def _loss_head(x, g, tgt, tm=256):
    D = x.shape[1]

    def fn(i, x, tgt, g):
        y, vjp = jax.vjp(_rms, x, g)
        err = y - tgt
        dx, dg = vjp(err * (1.0 / D))
        part = 0.5 / D * jnp.sum(jnp.sum(err * err, axis=1, keepdims=True), axis=0, keepdims=True)
        return dx, jnp.broadcast_to(part, (1, 128)), dg

    return _rows(fn, "loss_head", x.shape[0], tm, [x, tgt], [g], [(D, F32)], [((1, 128), F32), ((1, D), F32)])


def _local_step(x, mem, tgt, get_w, P, put_g):
    T = x.shape[0]
    W = dict(get_w("f1", None))

    def f1_down(after):
        W.update(get_w("f1d", after))
        return W["f1_down"]

    x1, s1 = _ffn_fwd(x, P["f1_norm"], W["f1_gate"], W["f1_up"], f1_down, "f1")

    W.update(get_w("mix", x1))
    h2 = _rms_fwd(x1, P["mix_norm"], "mix_norm")
    w_rkv, w_lora, w_swa = W["w_inT"][:3 * RW_W], W["w_inT"][3 * RW_W:SHIFT_COLS], W["w_inT"][SHIFT_COLS:]
    p_rkv = _mm(h2, w_rkv, "nt", "in_rkv", after=W.get("_after"))
    p_lora = _mm(h2, w_lora, "nt", "in_lora")
    p_swa = _mm(h2, w_swa, "nt", "in_swa")
    w_da = jnp.concatenate([W["rw_decay_up"], W["rw_aaa_up"]], axis=0)
    pre_params = (P["rw_mu"][:, :3 * RW_W], P["rw_mu"][:, 3 * RW_W:], P["rw_w0"], P["rw_a0"], P["rw_k_k"], P["rw_k_a"], w_da,
                  W["rw_gate_up"])
    r, decay, k2, v, an, bn, g = _rwkv_pre(p_rkv, p_lora, pre_params)
    scan_vecs = (an, decay, bn, k2, r)
    xes = [_to_perm(a) for a in scan_vecs]
    v_p = _to_perm(v)
    yi, sai, ck = _scan_fwd(xes, v_p)
    y_scan = _from_perm(yi)
    y_rw = _rwkv_post(y_scan, r, k2, v, g, P["rw_lnx_w"], P["rw_lnx_b"], P["rw_r_k"])
    cq, sq, ckt, skt = _rope_tables(T)
    y_swa = _swa_fwd(p_swa, P["b_in_attn"], P["attn_sinks"], cq, sq, ckt, skt)
    ycat = jnp.concatenate([y_rw, y_swa], axis=1)
    W.update(get_w("out", ycat))
    x2 = _mm(ycat, W["w_out"], "nn", "out_proj", res=x1, bias=P["b_out"])

    W.update(get_w("xattn", x2))
    hx = _rms_fwd(x2, P["xa_norm"], "xa_norm")
    mn = _rms_fwd(mem, P["mem_norm"], "mem_norm")
    q = _mm(hx, W["w_xq"], "nn", "xq", out_dtype=BF16)
    kv = _mm(mn, W["w_xkv"], "nn", "xkv", out_dtype=BF16)
    o = _xattn_fwd(q, kv)
    x3 = _mm(o, W["w_xo"], "nn", "xo", res=x2)

    W.update(get_w("f2", x3))
    x4, s2 = _ffn_fwd(x3, P["f2_norm"], W["f2_gate"], W["f2_up"], W["f2_down"], "f2")
    dx4, loss_part, d_final = _loss_head(x4, P["final_norm"], tgt)

    gs = {"final_norm": d_final}
    dx3, gs["f2_norm"] = _ffn_bwd(x3, P["f2_norm"], W["f2_gate"], W["f2_up"], W["f2_down"], s2, dx4, "f2", put_g)

    do = _mm(dx3, W["w_xo"], "nt", "xo_do", out_dtype=BF16)
    dw_xo = _mm(o, dx3, "tn", "xo_dw", out_dtype=BF16)
    dq, dkv = _xattn_bwd(q, kv, do)
    dw_xq = _mm(hx, dq, "tn", "xq_dw", out_dtype=BF16)
    dw_xkv = _mm(mn, dkv, "tn", "xkv_dw", out_dtype=BF16)
    sent = put_g("xattn", {"w_xq": dw_xq, "w_xkv": dw_xkv, "w_xo": dw_xo})
    dhx = _mm(dq, W["w_xq"], "nt", "xq_dh", after=sent)
    dmn = _mm(dkv, W["w_xkv"], "nt", "xkv_dmn")
    _, gs["mem_norm"], _ = _rms_bwd(mem, P["mem_norm"], dmn, jnp.zeros_like(mem), "mem_norm_bwd")
    dx2, gs["xa_norm"], gs["b_out"] = _rms_bwd(x2, P["xa_norm"], dhx, dx3, "xa_norm_bwd")

    dycat = _mm(dx2, W["w_out"], "nt", "out_dy")
    dw_out = _mm(ycat, dx2, "tn", "out_dw", out_dtype=BF16)
    dp_swa, gs["b_in_attn"], gs["attn_sinks"] = _swa_bwd(p_swa, P["b_in_attn"], P["attn_sinks"], cq, sq, ckt, skt, dycat[:, RW_W:])
    dy_scan, dr_b, dk2_b, dv_b, dg, gs["rw_lnx_w"], gs["rw_lnx_b"], gs["rw_r_k"] = _rwkv_post_bwd(
        y_scan, r, k2, v, g, P["rw_lnx_w"], P["rw_lnx_b"], P["rw_r_k"], dycat[:, :RW_W])
    dy_p = _to_perm(dy_scan)
    dsai, dvi = _scan_bwd_a(xes, dy_p)
    dj = _scan_bwd_b(xes, [v_p, sai, dy_p, dsai], _ck_a_to_b(ck))
    dan, ddecay, dbn, dk2_s, dr_s = (_from_perm(d) for d in dj)
    cts = (dr_s, ddecay, dk2_s, _from_perm(dvi), dan, dbn, dg, dr_b, dk2_b, dv_b)
    dp_rkv, dp_lora, dmu, dmul, gs["rw_w0"], gs["rw_a0"], gs["rw_k_k"], gs["rw_k_a"], dw_da, gs["rw_gate_up"] = _rwkv_pre_bwd(
        p_rkv, p_lora, pre_params, cts)
    gs["rw_mu"] = jnp.concatenate([dmu, dmul], axis=1)
    gs["rw_decay_up"], gs["rw_aaa_up"] = dw_da[:DECAY_LORA], dw_da[DECAY_LORA:]
    dw_inT = jnp.concatenate([_mm(dp_rkv, h2, "tn", "in_dw_rkv"), _mm(dp_lora, h2, "tn", "in_dw_lora"),
                              _mm(dp_swa, h2, "tn", "in_dw_swa")], axis=0)
    sent = put_g("mix", {"w_in": dw_inT, "w_out": dw_out})
    dh2 = _mm(dp_rkv, w_rkv, "nn", "in_dh_rkv", after=sent)
    dh2 = _mm(dp_lora, w_lora, "nn", "in_dh_lora", res=dh2)
    dh2 = _mm(dp_swa, w_swa, "nn", "in_dh_swa", res=dh2)
    dx1, gs["mix_norm"], _ = _rms_bwd(x1, P["mix_norm"], dh2, dx2, "mix_norm_bwd")

    dx0, gs["f1_norm"] = _ffn_bwd(x, P["f1_norm"], W["f1_gate"], W["f1_up"], W["f1_down"], s1, dx1, "f1", put_g)
    return loss_part, dx0, gs


_ANY = pl.BlockSpec(memory_space=pl.ANY)
_OTHER_CHIPS = ((1, 0), (0, 1), (1, 1))


def _mesh_pos():
    return lax.axis_index("x"), lax.axis_index("y"), lax.axis_index("c")


def _slot(ref, kind, s, rows, cols):
    if kind == "row":
        return ref.at[pl.ds(pl.multiple_of(s * rows, 8), rows), :]
    return ref.at[:, pl.ds(pl.multiple_of(s * cols, 128), cols)]


_HBM = pl.BlockSpec(memory_space=pltpu.HBM)
_SEMS = pl.BlockSpec(memory_space=pltpu.SEMAPHORE)
_SPLIT = dict(compiler_params=pltpu.CompilerParams(has_side_effects=pltpu.SideEffectType.DATAFLOW_SIDE_EFFECTING))


def _in_hbm(a):
    return pltpu.with_memory_space_constraint(a, pltpu.HBM)


def _full_shape(s, kind):
    return (4 * s.shape[0], s.shape[1]) if kind == "row" else (s.shape[0], 4 * s.shape[1])


def _half(ref, shape, h):
    rows, cols = shape
    if rows % 32 == 0:
        return ref.at[pl.ds(pl.multiple_of(h * (rows // 2), 16), rows // 2), :]
    assert cols % 256 == 0, shape
    return ref.at[:, pl.ds(pl.multiple_of(h * (cols // 2), 128), cols // 2)]


def _half_shape(shape):
    rows, cols = shape
    return (rows // 2, cols) if rows % 32 == 0 else (rows, cols // 2)


def _streams(src, dst, shape, c):
    hs = _half_shape(shape)
    s, d = _half(src, shape, c), _half(dst, shape, c)
    return [(_half(s, hs, q), _half(d, hs, q)) for q in range(2)]


def _swap_halves(name, fulls, shard_shapes, kinds):
    n = len(fulls)

    def body(*refs):
        out, send, recv = refs[n:2 * n], refs[2 * n], refs[2 * n + 1]
        x, y, c = _mesh_pos()
        sent = []
        for i in range(n):
            for r, (dx, dy) in enumerate(_OTHER_CHIPS):
                theirs = _slot(out[i], kinds[i], 2 * ((x + dx) % 2) + (y + dy) % 2, *shard_shapes[i])
                have = _half(theirs, shard_shapes[i], c)
                rc = pltpu.make_async_remote_copy(have, have, send.at[3 * i + r], recv.at[3 * i + r], device_id=(x, y, 1 - c),
                                                  device_id_type=MESH)
                rc.start()
                sent.append(rc)
        for i in range(n):
            for r, (dx, dy) in enumerate(_OTHER_CHIPS):
                theirs = _slot(out[i], kinds[i], 2 * ((x + dx) % 2) + (y + dy) % 2, *shard_shapes[i])
                need = _half(theirs, shard_shapes[i], 1 - c)
                pltpu.make_async_remote_copy(need, need, send.at[3 * i + r], recv.at[3 * i + r], device_id=(x, y, c),
                                             device_id_type=MESH).wait_recv()
        for rc in sent:
            rc.wait_send()

    return pl.pallas_call(
        body, name=name, in_specs=[_ANY] * n, out_specs=[_ANY] * n, out_shape=[jax.ShapeDtypeStruct(f.shape, f.dtype) for f in fulls],
        input_output_aliases={i: i for i in range(n)},
        scratch_shapes=[pltpu.SemaphoreType.DMA((3 * n,)), pltpu.SemaphoreType.DMA((3 * n,))],
    )(*fulls)


def _gather_start(name, shards, kinds, groups, after=None):
    n, ng = len(shards), len(groups)
    lands = [_in_hbm(lax.empty(_full_shape(s, k), s.dtype)) for s, k in zip(shards, kinds)]
    n_in = 2 * n + (after is not None)

    def body(*refs):
        src, land, sems, token = refs[:n], refs[n:2 * n], refs[n_in:n_in + 3 * ng], refs[-1]
        x, y, c = _mesh_pos()
        me = 2 * x + y
        for gi, idxs in enumerate(groups):
            send, recv, own = sems[3 * gi:3 * gi + 3]
            for k, i in enumerate(idxs):
                mine = _slot(land[i], kinds[i], me, *src[i].shape)
                for r, (dx, dy) in enumerate(_OTHER_CHIPS):
                    for q, (s, d) in enumerate(_streams(src[i], mine, src[i].shape, c)):
                        pltpu.make_async_remote_copy(s, d, send.at[6 * k + 2 * r + q], recv.at[6 * k + 2 * r + q],
                                                     device_id=((x + dx) % 2, (y + dy) % 2, c), device_id_type=MESH).start()
                pltpu.make_async_copy(src[i], mine, own.at[k]).start()
        token[...] = jnp.zeros_like(token)

    sem_shapes = [pltpu.SemaphoreType.DMA((w * len(g),)) for g in groups for w in (6, 6, 1)]
    thru = [pltpu.HBM(a.shape, a.dtype) for a in (*shards, *lands)]
    res = pl.pallas_call(
        body, name=name, in_specs=[_HBM] * (2 * n) + [_ANY] * (after is not None),
        out_specs=[_SEMS] * (3 * ng) + [_HBM] * (2 * n) + [pl.BlockSpec(memory_space=pltpu.VMEM)],
        out_shape=sem_shapes + thru + [jax.ShapeDtypeStruct((8, 128), F32)],
        input_output_aliases={i: 3 * ng + i for i in range(2 * n)}, **_SPLIT,
    )(*[_in_hbm(s) for s in shards], *lands, *([] if after is None else [after]))
    return res[:3 * ng], res[3 * ng:3 * ng + n], res[3 * ng + n:3 * ng + 2 * n], res[-1]


def _gather_wait(name, sems, shards, lands, kinds, after):
    m = len(shards)

    def body(*refs):
        src, land, (send, recv, own) = refs[:m], refs[m:2 * m], refs[2 * m:2 * m + 3]
        x, y, c = _mesh_pos()
        me = 2 * x + y
        for k in range(m):
            mine = _slot(land[k], kinds[k], me, *src[k].shape)
            for r in range(3):
                for q, (s, d) in enumerate(_streams(src[k], mine, src[k].shape, c)):
                    cp = pltpu.make_async_remote_copy(s, d, send.at[6 * k + 2 * r + q], recv.at[6 * k + 2 * r + q], device_id=(x, y, c),
                                                      device_id_type=MESH)
                    cp.wait_send()
                    cp.wait_recv()
            pltpu.make_async_copy(src[k], mine, own.at[k]).wait()

    thru = [pltpu.HBM(a.shape, a.dtype) for a in (*shards, *lands)]
    res = pl.pallas_call(
        body, name=name, in_specs=[_HBM] * (2 * m) + [_SEMS] * 3 + [pl.BlockSpec(memory_space=pl.ANY)],
        out_specs=[_HBM] * (2 * m), out_shape=thru, input_output_aliases={i: i for i in range(2 * m)}, **_SPLIT,
    )(*shards, *lands, *sems, after)
    return res[m:]


def _scatter_start(name, grads, kinds):
    m = len(grads)
    shard_shape = [(g.shape[0] // 4, g.shape[1]) if k == "row" else (g.shape[0], g.shape[1] // 4) for g, k in zip(grads, kinds)]
    lands = [_in_hbm(lax.empty((4, *s), g.dtype)) for s, g in zip(shard_shape, grads)]

    def body(*refs):
        src, land, (send, recv, own) = refs[:m], refs[m:2 * m], refs[2 * m:2 * m + 3]
        x, y, c = _mesh_pos()
        me = 2 * x + y
        for k in range(m):
            for r, (dx, dy) in enumerate(_OTHER_CHIPS):
                tx, ty = (x + dx) % 2, (y + dy) % 2
                pltpu.make_async_remote_copy(_slot(src[k], kinds[k], 2 * tx + ty, *shard_shape[k]), land[k].at[me],
                                             send.at[3 * k + r], recv.at[3 * k + r], device_id=(tx, ty, c), device_id_type=MESH).start()
            pltpu.make_async_copy(_slot(src[k], kinds[k], me, *shard_shape[k]), land[k].at[me], own.at[k]).start()
        refs[-1][...] = jnp.zeros_like(refs[-1])

    thru = [pltpu.HBM(a.shape, a.dtype) for a in (*grads, *lands)]
    res = pl.pallas_call(
        body, name=name, in_specs=[_HBM] * (2 * m),
        out_specs=[_SEMS] * 3 + [_HBM] * (2 * m) + [pl.BlockSpec(memory_space=pltpu.VMEM)],
        out_shape=[pltpu.SemaphoreType.DMA((3 * m,))] * 2 + [pltpu.SemaphoreType.DMA((m,))] + thru + [jax.ShapeDtypeStruct((8, 128), F32)],
        input_output_aliases={i: 3 + i for i in range(2 * m)}, **_SPLIT,
    )(*[_in_hbm(g) for g in grads], *lands)
    return res[:3], res[3:3 + m], res[3 + m:3 + 2 * m], res[-1]


def _scatter_wait(name, sems, grads, lands, kinds, after):
    m = len(grads)

    def body(*refs):
        src, land, (send, recv, own) = refs[:m], refs[m:2 * m], refs[2 * m:2 * m + 3]
        x, y, c = _mesh_pos()
        me = 2 * x + y
        for k in range(m):
            mine = _slot(src[k], kinds[k], me, *land[k].shape[1:])
            for r in range(3):
                cp = pltpu.make_async_remote_copy(mine, land[k].at[me], send.at[3 * k + r], recv.at[3 * k + r],
                                                  device_id=(x, y, c), device_id_type=MESH)
                cp.wait_send()
                cp.wait_recv()
            pltpu.make_async_copy(mine, land[k].at[me], own.at[k]).wait()

    thru = [pltpu.HBM(a.shape, a.dtype) for a in (*grads, *lands)]
    res = pl.pallas_call(
        body, name=name, in_specs=[_HBM] * (2 * m) + [_SEMS] * 3 + [pl.BlockSpec(memory_space=pl.ANY)],
        out_specs=[_HBM] * (2 * m), out_shape=thru, input_output_aliases={i: i for i in range(2 * m)}, **_SPLIT,
    )(*grads, *lands, *sems, after)
    return res[m:]


def _swap_with_sibling(arrs, name):
    n = len(arrs)

    def body(*refs):
        ins, outs = refs[:n], refs[n:2 * n]
        send, recv = refs[2 * n:]
        x, y, c = _mesh_pos()
        copies = []
        for i in range(n):
            rc = pltpu.make_async_remote_copy(ins[i], outs[i], send.at[i], recv.at[i], device_id=(x, y, 1 - c), device_id_type=MESH)
            rc.start()
            copies.append(rc)
        for rc in copies:
            rc.wait()

    return pl.pallas_call(
        body, name=name, in_specs=[_ANY] * n, out_specs=[_ANY] * n,
        out_shape=[jax.ShapeDtypeStruct(a.shape, a.dtype) for a in arrs],
        scratch_shapes=[pltpu.SemaphoreType.DMA((n,)), pltpu.SemaphoreType.DMA((n,))],
    )(*arrs)


def _small_start(pack, after):
    land = _in_hbm(lax.empty((8, *pack.shape), pack.dtype))

    def body(in_ref, land_ref, after_ref, send, recv, own, in_thru, land_thru, token):
        x, y, c = _mesh_pos()
        me = 4 * x + 2 * y + c
        for r in range(1, 8):
            dx, dy, dc = r // 4, (r // 2) % 2, r % 2
            pltpu.make_async_remote_copy(in_ref, land_ref.at[me], send.at[r - 1], recv.at[r - 1],
                                         device_id=((x + dx) % 2, (y + dy) % 2, (c + dc) % 2), device_id_type=MESH).start()
        pltpu.make_async_copy(in_ref, land_ref.at[me], own.at[0]).start()
        token[...] = jnp.zeros_like(token)

    res = pl.pallas_call(
        body, name="small_start", in_specs=[_HBM, _HBM, _ANY],
        out_specs=[_SEMS] * 3 + [_HBM, _HBM, pl.BlockSpec(memory_space=pltpu.VMEM)],
        out_shape=[pltpu.SemaphoreType.DMA((7,)), pltpu.SemaphoreType.DMA((7,)), pltpu.SemaphoreType.DMA((1,)),
                   pltpu.HBM(pack.shape, pack.dtype), pltpu.HBM(land.shape, land.dtype), jax.ShapeDtypeStruct((8, 128), F32)],
        input_output_aliases={0: 3, 1: 4}, **_SPLIT,
    )(_in_hbm(pack), land, after)
    return res[:3], res[3], res[4], res[5]


def _small_wait(sems, pack, land, after):
    def body(in_ref, land_ref, send, recv, own, after_ref, in_dead, got):
        x, y, c = _mesh_pos()
        me = 4 * x + 2 * y + c
        for r in range(1, 8):
            cp = pltpu.make_async_remote_copy(in_ref, land_ref.at[me], send.at[r - 1], recv.at[r - 1], device_id=(x, y, c),
                                              device_id_type=MESH)
            cp.wait_send()
            cp.wait_recv()
        pltpu.make_async_copy(in_ref, land_ref.at[me], own.at[0]).wait()

    res = pl.pallas_call(
        body, name="small_wait", in_specs=[_HBM, _HBM] + [_SEMS] * 3 + [_ANY], out_specs=[_HBM, _HBM],
        out_shape=[pltpu.HBM(pack.shape, pack.dtype), pltpu.HBM(land.shape, land.dtype)], input_output_aliases={0: 0, 1: 1}, **_SPLIT,
    )(pack, land, *sems, after)
    return res[1]


def _row_tile(R, dtype, target=256):
    mult = 8 * 4 // jnp.dtype(dtype).itemsize
    best = R
    for t in range(mult, min(R, target) + 1, mult):
        if R % t == 0:
            best = t
    return best


def _sum_slots(stack, name, out_dtype=F32):
    k, R, C = stack.shape
    tr = _row_tile(R, stack.dtype)
    tc = C
    if tr < 64:
        tr, tc = R, _pick(C, 512)

    def body(s_ref, o_ref):
        acc = s_ref[0].astype(F32)
        for j in range(1, k):
            acc = acc + s_ref[j].astype(F32)
        o_ref[...] = acc.astype(out_dtype)

    return pl.pallas_call(
        body, name=name, grid=(R // tr, C // tc), in_specs=[pl.BlockSpec((k, tr, tc), lambda i, j: (0, i, j))],
        out_specs=pl.BlockSpec((tr, tc), lambda i, j: (i, j)), out_shape=jax.ShapeDtypeStruct((R, C), out_dtype),
        compiler_params=_cp(("parallel", "parallel")),
    )(stack)


W_IN_SHARD = 1160
W_IN_PAD = 1168


def _pad_shards(a):
    zeros = jnp.zeros((W_IN_PAD - W_IN_SHARD, a.shape[1]), a.dtype)
    parts = []
    for s in range(a.shape[0] // W_IN_SHARD):
        parts += [a[s * W_IN_SHARD:(s + 1) * W_IN_SHARD], zeros]
    return jnp.concatenate(parts, axis=0).astype(BF16)


def _unpad_shards(a):
    a = a.astype(F32)
    return jnp.concatenate([a[s * W_IN_PAD:s * W_IN_PAD + W_IN_SHARD] for s in range(a.shape[0] // W_IN_PAD)], axis=0)


def _adamw(w, m, v, ga, gb, name, after=None):
    R, C = w.shape
    tr = _row_tile(R, F32, 128)
    gs = [ga] if gb is None else [ga, gb]
    extra = [] if after is None else [after]

    def body(*refs):
        w_ref, m_ref, v_ref = refs[:3]
        g = refs[3][...].astype(F32)
        if gb is not None:
            g = g + refs[4][...].astype(F32)
        g_ref, d_ref, nm_ref, nv_ref = refs[-4:]
        nm = ADAM_B1 * m_ref[...] + (1.0 - ADAM_B1) * g
        nv = ADAM_B2 * v_ref[...] + (1.0 - ADAM_B2) * (g * g)
        m_hat = nm / (1.0 - ADAM_B1 ** ADAM_STEP)
        v_hat = nv / (1.0 - ADAM_B2 ** ADAM_STEP)
        g_ref[...] = g
        d_ref[...] = -ADAM_LR * (m_hat / (jnp.sqrt(v_hat) + ADAM_EPS) + ADAM_WD * w_ref[...])
        nm_ref[...] = nm
        nv_ref[...] = nv

    spec = pl.BlockSpec((tr, C), lambda i: (i, 0))
    return pl.pallas_call(
        body, name=name, grid=(R // tr,), in_specs=[spec] * (3 + len(gs)) + [_ANY] * len(extra), out_specs=[spec] * 4,
        out_shape=[jax.ShapeDtypeStruct((R, C), F32)] * 4, compiler_params=_cp(("parallel",)),
    )(w, m, v, *gs, *extra)


def _pack(arrs):
    rows = []
    for a in arrs:
        flat = a.reshape(-1)
        rows.append(jnp.pad(flat, (0, -flat.shape[0] % 1024)).reshape(-1, 1024))
    p = jnp.concatenate(rows, axis=0)
    return jnp.pad(p, ((0, -p.shape[0] % 8), (0, 0)))


def _unpack(p, shapes):
    out, r = [], 0
    for s in shapes:
        n = 1
        for d in s:
            n *= d
        nr = -(-n // 1024)
        out.append(p[r:r + nr].reshape(-1)[:n].reshape(s))
        r += nr
    return out


BIG = ("f1_gate", "f1_up", "f1_down", "w_in", "w_out", "w_xq", "w_xkv", "w_xo", "f2_gate", "f2_up", "f2_down")
BIG_KIND = {"f1_gate": "col", "f1_up": "col", "f1_down": "row", "w_in": "row", "w_out": "row", "w_xq": "row", "w_xkv": "col",
            "w_xo": "row", "f2_gate": "col", "f2_up": "col", "f2_down": "row"}
LORA = ("rw_decay_up", "rw_aaa_up", "rw_gate_up")
WEIGHTS = ("f1_norm", "f1_gate", "f1_up", "f1_down", "mix_norm", "w_in", "b_in_attn", "rw_mu", "rw_w0", "rw_decay_up", "rw_a0",
           "rw_aaa_up", "rw_gate_up", "rw_k_k", "rw_k_a", "rw_r_k", "rw_lnx_w", "rw_lnx_b", "attn_sinks", "w_out", "b_out", "xa_norm",
           "mem_norm", "w_xq", "w_xkv", "w_xo", "f2_norm", "f2_gate", "f2_up", "f2_down", "final_norm")
SMALL = tuple(n for n in WEIGHTS if n not in BIG)
GROUP_ORDER = ("f1", "f1d", "mix", "out", "xattn", "f2")
GROUPS = {"f1": ("f1_gate", "f1_up"), "f1d": ("f1_down",), "mix": ("w_in",) + LORA, "out": ("w_out",), "xattn": ("w_xq", "w_xkv", "w_xo"),
          "f2": ("f2_gate", "f2_up", "f2_down")}


def kernel(x, mem, f1_norm, f1_gate, f1_up, f1_down, mix_norm, w_in, b_in_attn, rw_mu, rw_w0, rw_decay_up, rw_a0, rw_aaa_up, rw_gate_up, rw_k_k, rw_k_a, rw_r_k, rw_lnx_w, rw_lnx_b, attn_sinks, w_out, b_out, xa_norm, mem_norm, w_xq, w_xkv, w_xo, f2_norm, f2_gate, f2_up, f2_down, final_norm, loss_target, m_f1_norm, m_f1_gate, m_f1_up, m_f1_down, m_mix_norm, m_w_in, m_b_in_attn, m_rw_mu, m_rw_w0, m_rw_decay_up, m_rw_a0, m_rw_aaa_up, m_rw_gate_up, m_rw_k_k, m_rw_k_a, m_rw_r_k, m_rw_lnx_w, m_rw_lnx_b, m_attn_sinks, m_w_out, m_b_out, m_xa_norm, m_mem_norm, m_w_xq, m_w_xkv, m_w_xo, m_f2_norm, m_f2_gate, m_f2_up, m_f2_down, m_final_norm, v_f1_norm, v_f1_gate, v_f1_up, v_f1_down, v_mix_norm, v_w_in, v_b_in_attn, v_rw_mu, v_rw_w0, v_rw_decay_up, v_rw_a0, v_rw_aaa_up, v_rw_gate_up, v_rw_k_k, v_rw_k_a, v_rw_r_k, v_rw_lnx_w, v_rw_lnx_b, v_attn_sinks, v_w_out, v_b_out, v_xa_norm, v_mem_norm, v_w_xq, v_w_xkv, v_w_xo, v_f2_norm, v_f2_gate, v_f2_up, v_f2_down, v_final_norm):
    a = dict(locals())
    w = {n: a[n] for n in WEIGHTS}
    m = {n: a["m_" + n] for n in WEIGHTS}
    v = {n: a["v_" + n] for n in WEIGHTS}
    sq = lambda t: t.reshape(t.shape[-2:]) if t.ndim == 3 else t.reshape(1, -1)

    local_name = lambda n: "w_inT" if n == "w_in" else n
    kind_of = lambda n: BIG_KIND.get(n, "col")
    payload = lambda n: _pad_shards(sq(w[n]).T) if n == "w_in" else sq(w[n]) if n in LORA else sq(w[n]).astype(BF16)
    gathers = {}

    def start_gather(name, grps, after):
        shards = [payload(n) for g in grps for n in GROUPS[g]]
        kinds = [kind_of(n) for g in grps for n in GROUPS[g]]
        groups, at = [], 0
        for g in grps:
            groups.append(list(range(at, at + len(GROUPS[g]))))
            at += len(GROUPS[g])
        sems, src_thru, land_thru, token = _gather_start(name, shards, kinds, groups, after)
        for gi, g in enumerate(grps):
            gathers[g] = (sems[3 * gi:3 * gi + 3], [src_thru[i] for i in groups[gi]], [land_thru[i] for i in groups[gi]],
                          [kinds[i] for i in groups[gi]], token)

    early = GROUP_ORDER[:3]
    start_gather("gather_start", early, None)

    def get_w(grp, after):
        g_sems, g_src, g_land, g_kinds, token = gathers[grp]
        got = _gather_wait("gather_wait_" + grp, g_sems, g_src, g_land, g_kinds, token if after is None else after)
        got = _swap_halves("gather_swap_" + grp, got, [s.shape for s in g_src], g_kinds)
        out = {local_name(n): (_unpad_shards(f) if n == "w_in" else f) for n, f in zip(GROUPS[grp], got)}
        if grp == early[-1]:
            start_gather("gather_start_late", GROUP_ORDER[3:], got[0])
            out["_after"] = gathers[GROUP_ORDER[3]][4]
        return out

    in_flight = []

    def put_g(label, gw):
        names = list(gw)
        grads = [_pad_shards(gw[n]) if n == "w_in" else gw[n] for n in names]
        *flight, sent = _scatter_start("scatter_start_" + label, grads, [kind_of(n) for n in names])
        in_flight.append((label, names, flight))
        return sent

    P = {n: sq(w[n]) for n in SMALL if n not in LORA}
    P["attn_sinks"] = jnp.pad(P["attn_sinks"], ((0, 0), (0, 128 - P["attn_sinks"].shape[1])))
    P["rw_r_k"] = w["rw_r_k"].reshape(1, RW_W)
    loss_part, grad_x, gs = _local_step(x[0], mem[0], loss_target[0], get_w, P, put_g)

    gs["attn_sinks"] = gs["attn_sinks"][:, :16]
    small_flight = _small_start(_pack([gs[n] for n in SMALL] + [loss_part]), grad_x)

    out, after = {}, small_flight[-1]
    for bi, batch in enumerate((in_flight[:-3], in_flight[-3:])):
        b_names, b_partial = [], []
        for label, names, (g_sems, g_thru, l_thru) in batch:
            stacks = _scatter_wait("scatter_wait_" + label, g_sems, g_thru, l_thru, [kind_of(n) for n in names], after)
            partial = [_sum_slots(s, "sum_chips_" + n, F32 if n == "w_in" else BF16) for s, n in zip(stacks, names)]
            b_names += names
            b_partial += partial
            after = partial[-1]
        sibling = _swap_with_sibling(b_partial, "swap_batch%d" % bi)
        chain = None
        for n, pa, sb in zip(b_names, b_partial, sibling):
            if n == "w_in":
                pa, sb = pa[:W_IN_SHARD].T, sb[:W_IN_SHARD].T
            out[n] = _adamw(sq(w[n]), sq(m[n]), sq(v[n]), pa, sb, "adamw_" + n, after=chain)
            chain = out[n][1]
        after = chain

    gsum = _sum_slots(_small_wait(*small_flight[:-1], after), "sum_small")
    *summed, loss_row = _unpack(gsum, [gs[n].shape for n in SMALL] + [loss_part.shape])
    g_small = dict(zip(SMALL, summed))
    loss = loss_row[0, 0]
    shard = 2 * lax.axis_index("x") + lax.axis_index("y")
    for n in LORA:
        cols = w[n].shape[-1]
        g_small[n] = lax.dynamic_slice_in_dim(g_small[n], shard * cols, cols, axis=1)

    flat = lambda d: _pack([d[n] for n in SMALL])
    res = _adamw(flat(w), flat(m), flat(v), _pack([g_small[n] for n in SMALL]), None, "adamw_small")
    shapes = [w[n].shape for n in SMALL]
    for k, p in enumerate(res):
        for n, t in zip(SMALL, _unpack(p, shapes)):
            out.setdefault(n, [None] * 4)[k] = t
    outs = [loss, grad_x.reshape(x.shape)]
    for k in range(4):
        outs += [out[n][k].reshape(w[n].shape) for n in WEIGHTS]
    return tuple(outs)
```

```python
import functools

import jax
import jax.numpy as jnp
from jax import lax
from jax.experimental import pallas as pl
from jax.experimental.pallas import tpu as pltpu

F32, BF16 = jnp.float32, jnp.bfloat16
MESH = pl.DeviceIdType.MESH

HEAD = 64
RW_HEADS = 16
RW_W = 1024
SWA_W = 1024
KV_W = 128
DECAY_LORA, AAA_LORA, GATE_LORA = 64, 64, 160
LORA_W = DECAY_LORA + AAA_LORA + GATE_LORA
SHIFT_COLS = 3 * RW_W + LORA_W
XH = 4
XHD = 512
MEM_LEN = 256
WINDOW = 128
GN_EPS = 64e-5
RMS_EPS = 1e-6
NEG_INF = -1e30
ADAM_LR, ADAM_B1, ADAM_B2, ADAM_EPS, ADAM_WD, ADAM_STEP = 0.001, 0.9, 0.999, 1e-08, 0.01, 10

VMEM_LIMIT = 56 * 1024 * 1024


def _cp(sem=None, **kw):
    return pltpu.CompilerParams(dimension_semantics=sem, vmem_limit_bytes=VMEM_LIMIT, **kw)


def _pick(dim, target):
    if dim <= target:
        return dim
    best = None
    for t in range(128, target + 1, 128):
        if dim % t == 0:
            best = t
    assert best is not None, (dim, target)
    return best


_DIMS = {"nn": (((1,), (0,)), ((), ())), "nt": (((1,), (1,)), ((), ())), "tn": (((0,), (0,)), ((), ()))}


def _mm(a, b, mode, name, out_dtype=F32, alpha=1.0, res=None, bias=None, tm=1024, tn=1024, tk=2048, after=None):
    if mode == "nn":
        (M, K), (K2, N) = a.shape, b.shape
    elif mode == "nt":
        (M, K), (N, K2) = a.shape, b.shape
    else:
        (K, M), (K2, N) = a.shape, b.shape
    assert K == K2, (name, a.shape, b.shape)
    tm, tn, tk = _pick(M, tm), _pick(N, tn), _pick(K, tk)
    nk = K // tk
    a_spec = pl.BlockSpec((tk, tm), lambda i, j, k: (k, i)) if mode == "tn" else pl.BlockSpec((tm, tk), lambda i, j, k: (i, k))
    b_spec = pl.BlockSpec((tn, tk), lambda i, j, k: (j, k)) if mode == "nt" else pl.BlockSpec((tk, tn), lambda i, j, k: (k, j))
    o_spec = pl.BlockSpec((tm, tn), lambda i, j, k: (i, j))
    ins, specs = [a, b], [a_spec, b_spec]
    if res is not None:
        ins.append(res)
        specs.append(o_spec)
    if bias is not None:
        ins.append(bias)
        specs.append(pl.BlockSpec((1, tn), lambda i, j, k: (0, j)))
    if after is not None:
        ins.append(after)
        specs.append(pl.BlockSpec(memory_space=pl.ANY))
    dims = _DIMS[mode]

    def body(*refs):
        a_ref, b_ref = refs[0], refs[1]
        part = lax.dot_general(a_ref[...].astype(BF16), b_ref[...].astype(BF16), dims, preferred_element_type=F32)

        def finish(o, o_ref):
            if alpha != 1.0:
                o = o * alpha
            p = 2
            if res is not None:
                o = o + refs[p][...].astype(F32)
                p += 1
            if bias is not None:
                o = o + refs[p][...]
            o_ref[...] = o.astype(out_dtype)

        if nk == 1:
            finish(part, refs[-1])
            return
        o_ref, acc_ref = refs[-2], refs[-1]
        k = pl.program_id(2)

        @pl.when(k == 0)
        def _():
            acc_ref[...] = part

        @pl.when(k > 0)
        def _():
            acc_ref[...] += part

        @pl.when(k == nk - 1)
        def _():
            finish(acc_ref[...], o_ref)

    return pl.pallas_call(
        body, name=name, grid=(M // tm, N // tn, nk), in_specs=specs, out_specs=o_spec,
        out_shape=jax.ShapeDtypeStruct((M, N), out_dtype), scratch_shapes=[pltpu.VMEM((tm, tn), F32)] * (nk > 1),
        compiler_params=_cp(("parallel", "parallel", "arbitrary")),
    )(*ins)


def _rows(fn, name, T, tm, tiled, full, out_tiled, out_acc, extra=(), reverse=False, scratch=()):
    n = T // tm
    idx = (lambda i: n - 1 - i) if reverse else (lambda i: i)
    in_specs = [pl.BlockSpec((tm, a.shape[1]), lambda i: (idx(i), 0)) for a in tiled]
    in_specs += [mk(idx) for _, mk in extra]
    in_specs += [pl.BlockSpec(a.shape, lambda i, nd=a.ndim: (0,) * nd) for a in full]
    out_specs = [pl.BlockSpec((tm, c), lambda i: (idx(i), 0)) for c, _ in out_tiled]
    out_specs += [pl.BlockSpec(s, lambda i, nd=len(s): (0,) * nd) for s, _ in out_acc]
    out_shape = [jax.ShapeDtypeStruct((T, c), d) for c, d in out_tiled] + [jax.ShapeDtypeStruct(s, d) for s, d in out_acc]
    n_in = len(tiled) + len(extra) + len(full)
    n_t, n_a = len(out_tiled), len(out_acc)

    def body(*refs):
        step = pl.program_id(0)
        vals = [r[...] for r in refs[:n_in]]
        outs = fn(idx(step), *vals, *refs[n_in + n_t + n_a:])
        for r, v in zip(refs[n_in:n_in + n_t], outs[:n_t]):
            r[...] = v.astype(r.dtype)
        for r, v in zip(refs[n_in + n_t:n_in + n_t + n_a], outs[n_t:]):
            @pl.when(step == 0)
            def _(r=r):
                r[...] = jnp.zeros_like(r)

            r[...] += v

    return pl.pallas_call(
        body, name=name, grid=(n,), in_specs=in_specs, out_specs=out_specs, out_shape=out_shape,
        scratch_shapes=list(scratch), compiler_params=_cp(("arbitrary",)),
    )(*tiled, *[a for a, _ in extra], *full)


def _rms(x, g):
    return x * lax.rsqrt(jnp.mean(x * x, axis=-1, keepdims=True) + RMS_EPS) * g


def _rms_fwd(x, g, name, tm=256):
    (h,) = _rows(lambda i, x, g: (_rms(x, g),), name, x.shape[0], min(tm, x.shape[0]), [x], [g], [(x.shape[1], BF16)], [])
    return h


def _rms_bwd(x, g, dh, dres, name, tm=256):
    D = x.shape[1]

    def fn(i, x, dh, dres, g):
        _, vjp = jax.vjp(_rms, x, g)
        dx, dg = vjp(dh.astype(F32))
        dx = dx + dres
        return dx, dg, jnp.sum(dx, axis=0, keepdims=True)

    return _rows(fn, name, x.shape[0], tm, [x, dh, dres], [g], [(D, F32)], [((1, D), F32), ((1, D), F32)])


def _ffn_up(h, wg, wu, name, tm=1024, tn=512, after=None):
    (M, K), N = h.shape, wg.shape[1]
    tm, tn = _pick(M, tm), _pick(N, tn)

    def body(*refs):
        h_ref, wg_ref, wu_ref = refs[:3]
        g_ref, u_ref, a_ref = refs[-3:]
        hb = h_ref[...].astype(BF16)
        g = jnp.dot(hb, wg_ref[...].astype(BF16), preferred_element_type=F32)
        u = jnp.dot(hb, wu_ref[...].astype(BF16), preferred_element_type=F32)
        g_ref[...] = g
        u_ref[...] = u
        a_ref[...] = (g * jax.nn.sigmoid(g) * u).astype(BF16)

    o_spec = pl.BlockSpec((tm, tn), lambda i, j: (i, j))
    w_spec = pl.BlockSpec((K, tn), lambda i, j: (0, j))
    extra = [] if after is None else [after]
    return pl.pallas_call(
        body, name=name, grid=(M // tm, N // tn),
        in_specs=[pl.BlockSpec((tm, K), lambda i, j: (i, 0)), w_spec, w_spec] + [pl.BlockSpec(memory_space=pl.ANY)] * len(extra),
        out_specs=[o_spec] * 3, out_shape=[jax.ShapeDtypeStruct((M, N), F32)] * 2 + [jax.ShapeDtypeStruct((M, N), BF16)],
        compiler_params=_cp(("parallel", "parallel")),
    )(h, wg, wu, *extra)


def _ffn_dact(dxo, wd, g, u, name, tm=1024, tn=512):
    (M, K), N = dxo.shape, wd.shape[0]
    tm, tn = _pick(M, tm), _pick(N, tn)

    def body(dx_ref, wd_ref, g_ref, u_ref, dg_ref, du_ref):
        da = 0.5 * lax.dot_general(dx_ref[...].astype(BF16), wd_ref[...].astype(BF16), _DIMS["nt"], preferred_element_type=F32)
        g = g_ref[...]
        s = jax.nn.sigmoid(g)
        dg_ref[...] = (da * u_ref[...] * (s * (1.0 + g * (1.0 - s)))).astype(BF16)
        du_ref[...] = (da * (g * s)).astype(BF16)

    t_spec = pl.BlockSpec((tm, tn), lambda i, j: (i, j))
    return pl.pallas_call(
        body, name=name, grid=(M // tm, N // tn),
        in_specs=[pl.BlockSpec((tm, K), lambda i, j: (i, 0)), pl.BlockSpec((tn, K), lambda i, j: (j, 0)), t_spec, t_spec],
        out_specs=[t_spec, t_spec], out_shape=[jax.ShapeDtypeStruct((M, N), BF16)] * 2, compiler_params=_cp(("parallel", "parallel")),
    )(dxo, wd, g, u)


def _ffn_fwd(x, h, wg, wu, wd, tag, after=None):
    G, U, A = _ffn_up(h, wg, wu, tag + "_up", after=after)
    xo = _mm(A, wd(A) if callable(wd) else wd, "nn", tag + "_down", alpha=0.5, res=x)
    return xo, (h, G, U, A)


def _ffn_bwd(x, gain, wg, wu, wd, saved, dxo, tag, send):
    h, G, U, A = saved
    dwd = _mm(A, dxo, "tn", tag + "_dwd", out_dtype=BF16, alpha=0.5, tm=1408)
    sent = send(tag + "_down", {tag + "_down": dwd})
    dG, dU = _ffn_dact(dxo, wd, G, U, tag + "_dact")
    dwu = _mm(h, dU, "tn", tag + "_dwu", out_dtype=BF16, after=sent)
    sent = send(tag + "_up", {tag + "_up": dwu})
    dwg = _mm(h, dG, "tn", tag + "_dwg", out_dtype=BF16, after=sent)
    sent = send(tag + "_gate", {tag + "_gate": dwg})
    dh = _mm(dG, wg, "nt", tag + "_dh_g", after=sent)
    dh = _mm(dU, wu, "nt", tag + "_dh_u", res=dh)
    dx, dgain, _ = _rms_bwd(x, gain, dh, dxo, tag + "_norm_bwd")
    return dx, dgain


def _segsum64_impl(x):
    r = lax.broadcasted_iota(jnp.int32, (128, 128), 0) // HEAD
    c = lax.broadcasted_iota(jnp.int32, (128, 128), 1) // HEAD
    ones = (r == c).astype(BF16)
    hi = x.astype(BF16)
    lo = (x - hi.astype(F32)).astype(BF16)
    outs = []
    for q in range(x.shape[1] // 128):
        sl = slice(q * 128, (q + 1) * 128)
        outs.append(jnp.dot(hi[:, sl], ones, preferred_element_type=F32) + jnp.dot(lo[:, sl], ones, preferred_element_type=F32))
    return outs[0] if len(outs) == 1 else jnp.concatenate(outs, axis=1)


@jax.custom_vjp
def _segsum64(x):
    return _segsum64_impl(x)


_segsum64.defvjp(lambda x: (_segsum64_impl(x), None), lambda _, ct: (_segsum64_impl(ct),))


def _swap32(x):
    lane = lax.broadcasted_iota(jnp.int32, (x.shape[0], 128), 1)
    outs = [jnp.take_along_axis(x[:, q * 128:(q + 1) * 128], lane ^ 32, axis=1) for q in range(x.shape[1] // 128)]
    return outs[0] if len(outs) == 1 else jnp.concatenate(outs, axis=1)


def _tree_sum(xs):
    xs = list(xs)
    while len(xs) > 1:
        nxt = [xs[i] + xs[i + 1] for i in range(0, len(xs) - 1, 2)]
        if len(xs) % 2:
            nxt.append(xs[-1])
        xs = nxt
    return xs[0]


class _Acc:
    def __init__(self, ways=4):
        self.parts = [None] * ways

    def add(self, i, term):
        k = i % len(self.parts)
        self.parts[k] = term if self.parts[k] is None else self.parts[k] + term

    def total(self):
        return _tree_sum([p for p in self.parts if p is not None])


def _softplus(x):
    return jnp.maximum(x, 0.0) + jnp.log(1.0 + jnp.exp(-jnp.abs(x)))


def _pre_core(k, da, gd, w0, a0, k_k, k_a, w_da, gate_up):
    lane = lax.broadcasted_iota(jnp.int32, da.shape, 1)
    w_da = w_da.astype(BF16)
    l1 = jnp.dot(jnp.where(lane < DECAY_LORA, jnp.tanh(da), 0.0).astype(BF16), w_da, preferred_element_type=F32)
    l2 = jnp.dot(jnp.where(lane >= DECAY_LORA, da, 0.0).astype(BF16), w_da, preferred_element_type=F32)
    wlog = -_softplus(-(w0 + l1)) - 0.5
    decay = jnp.exp(-jnp.exp(wlog))
    a = jax.nn.sigmoid(a0 + l2)
    g = jnp.dot(jax.nn.sigmoid(gd).astype(BF16), gate_up.astype(BF16), preferred_element_type=F32)
    kk = k * k_k
    kkn = kk / jnp.maximum(jnp.sqrt(_segsum64(kk * kk)), 1e-12)
    k2 = k * (1.0 + (a - 1.0) * k_a)
    return decay, k2, -kkn, kkn * a, g


def _pre_shift(i, zr, zl, zr8, zl8, mu, mul):
    live = (i > 0).astype(F32)
    dz = _shift_down(zr, zr8[7:8, :] * live) - zr
    dzl = _shift_down(zl, zl8[7:8, :] * live) - zl
    return zr + dz * mu, zl + dzl * mul, dz, dzl


def _shift_down(x, first_row):
    rolled = pltpu.roll(x, 1, 0)
    row = lax.broadcasted_iota(jnp.int32, x.shape, 0)
    return jnp.where(row == 0, first_row, rolled)


def _shift_up(x, last_row):
    rolled = pltpu.roll(x, x.shape[0] - 1, 0)
    row = lax.broadcasted_iota(jnp.int32, x.shape, 0)
    return jnp.where(row == x.shape[0] - 1, last_row, rolled)


def _prev_rows_spec(tm, cols):
    return lambda idx: pl.BlockSpec((8, cols), lambda i: (jnp.maximum(idx(i) * (tm // 8) - 1, 0), 0))


def _rwkv_pre(p_rkv, p_lora, params, tm=256):
    T = p_rkv.shape[0]

    def fn(i, zr, zl, zr8, zl8, mu, mul, *ps):
        z, z2, _, _ = _pre_shift(i, zr, zl, zr8, zl8, mu, mul)
        decay, k2, an, bn, g = _pre_core(z[:, RW_W:2 * RW_W], z2[:, :128], z2[:, 128:], *ps)
        return z[:, :RW_W], decay, k2, z[:, 2 * RW_W:], an, bn, g

    extra = [(p_rkv, _prev_rows_spec(tm, 3 * RW_W)), (p_lora, _prev_rows_spec(tm, LORA_W))]
    return _rows(fn, "rwkv_pre", T, tm, [p_rkv, p_lora], list(params), [(RW_W, F32)] * 7, [], extra=extra)


def _rwkv_pre_bwd(p_rkv, p_lora, params, cts, tm=256):
    T = p_rkv.shape[0]
    n = T // tm

    def fn(i, zr, zl, cr, cdec, ck2, cv, can, cbn, cg, cr_b, ck2_b, cv_b, zr8, zl8, mu, mul, *rest):
        ps, (car, carl) = rest[:-2], rest[-2:]
        cr, ck2, cv = cr + cr_b, ck2 + ck2_b, cv + cv_b
        z, z2, dif, difl = _pre_shift(i, zr, zl, zr8, zl8, mu, mul)
        _, vjp = jax.vjp(_pre_core, z[:, RW_W:2 * RW_W], z2[:, :128], z2[:, 128:], *ps)
        dk, dda, dgd, *dps = vjp((cdec, ck2, can, cbn, cg))
        dz = jnp.concatenate([cr, dk, cv], axis=1)
        dz2 = jnp.concatenate([dda, dgd], axis=1)
        dzp, dzlp = dz * mu, dz2 * mul

        @pl.when(i == n - 1)
        def _():
            car[...] = jnp.zeros_like(car)
            carl[...] = jnp.zeros_like(carl)

        d_rkv = dz - dzp + _shift_up(dzp, car[0:1, :])
        d_lora = dz2 - dzlp + _shift_up(dzlp, carl[0:1, :])
        car[0:1, :] = dzp[0:1, :]
        carl[0:1, :] = dzlp[0:1, :]
        return (d_rkv, d_lora, jnp.sum(dz * dif, axis=0, keepdims=True), jnp.sum(dz2 * difl, axis=0, keepdims=True), *dps)

    extra = [(p_rkv, _prev_rows_spec(tm, 3 * RW_W)), (p_lora, _prev_rows_spec(tm, LORA_W))]
    acc = [(p.shape, F32) for p in params]
    return _rows(fn, "rwkv_pre_bwd", T, tm, [p_rkv, p_lora, *cts], list(params), [(3 * RW_W, BF16), (LORA_W, BF16)], acc,
                 extra=extra, reverse=True, scratch=[pltpu.VMEM((8, 3 * RW_W), F32), pltpu.VMEM((8, LORA_W), F32)])


def _post_core(y, r, k2, v, g, lw, lb, rk):
    mu = _segsum64(y) * (1.0 / HEAD)
    yc = y - mu
    var = _segsum64(yc * yc) * (1.0 / HEAD)
    yn = yc * lax.rsqrt(var + GN_EPS) * lw + lb
    return (yn + _segsum64(r * k2 * rk) * v) * g


def _rwkv_post(y, r, k2, v, g, lw, lb, rk, tm=256):
    (o,) = _rows(lambda i, *a: (_post_core(*a),), "rwkv_post", y.shape[0], tm, [y, r, k2, v, g], [lw, lb, rk], [(RW_W, BF16)], [])
    return o


def _rwkv_post_bwd(y, r, k2, v, g, lw, lb, rk, do, tm=256):
    def fn(i, y, r, k2, v, g, do, lw, lb, rk):
        _, vjp = jax.vjp(_post_core, y, r, k2, v, g, lw, lb, rk)
        return vjp(do.astype(F32))

    return _rows(fn, "rwkv_post_bwd", y.shape[0], tm, [y, r, k2, v, g, do], [lw, lb, rk], [(RW_W, F32)] * 5, [((1, RW_W), F32)] * 3)


SCAN_L = 64


def _to_perm(x):
    T = x.shape[0]
    return x.reshape(T, RW_HEADS, HEAD).transpose(0, 2, 1).reshape(T, 8, 128)


def _from_perm(x):
    T = x.shape[0]
    return x.reshape(T, HEAD, RW_HEADS).transpose(0, 2, 1).reshape(T, RW_W)


def _as_tile(p):
    lane = lax.broadcasted_iota(jnp.int32, (8, 128), 1)
    return jnp.take_along_axis(p, (lane % 8) * 16 + lane // 8, axis=1)


def _as_perm(t):
    lane = lax.broadcasted_iota(jnp.int32, (8, 128), 1)
    return jnp.take_along_axis(t, (lane % 16) * 8 + lane // 16, axis=1)


def _tiles_to_perm(refs, L):
    for r in refs:
        for t in range(L):
            r[t] = _as_perm(r[t])


def _expander(srcs, tiles=()):
    s = lax.broadcasted_iota(jnp.int32, (8, 128), 0)
    lane = lax.broadcasted_iota(jnp.int32, (8, 128), 1)
    idx = 16 * s + lane // 8

    def expand(t, e_ref):
        for m, r in enumerate(srcs):
            for g in range(8):
                row = jnp.broadcast_to(r[t, pl.ds(g, 1), :], (8, 128))
                e_ref[m, g * 8:(g + 1) * 8, :] = jnp.take_along_axis(row, idx, axis=1)
        for k, r in enumerate(tiles):
            e_ref[len(srcs) + k, 0:8, :] = _as_tile(r[t])

    return expand


def _ck_a_to_b(ck):
    n = ck.shape[0]
    return ck.reshape(n, 8, 8, 8, RW_HEADS, 8).transpose(0, 2, 5, 1, 4, 3).reshape(n, HEAD, 8, 128)


def _bc(row):
    return jnp.broadcast_to(row, (8, 128))


def _rsum(x):
    return jnp.sum(x, axis=0, keepdims=True)


def _plus(acc, k, term):
    acc[k] = term if acc[k] is None else acc[k] + term


def _scan_fwd(xes, vi):
    T, L = vi.shape[0], SCAN_L
    nch = T // L

    def body(*refs):
        xr, (vi_ref, yi_ref, sa_ref, ck_ref, st_ref, e0, e1) = refs[:5], refs[5:]

        @pl.when(pl.program_id(0) == 0)
        def _():
            st_ref[...] = jnp.zeros_like(st_ref)

        ck_ref[0] = st_ref[...]
        expand = _expander(xr, [vi_ref])
        expand(0, e0)

        def step(t, e):
            tile = lambda m, jh: e[m, 8 * jh:8 * jh + 8, :]
            vb = [_bc(e[5, ih:ih + 1, :]) for ih in range(8)]
            acc = [None] * 8
            for jh in range(8):
                a = tile(0, jh)
                for ih in range(8):
                    _plus(acc, ih, st_ref[8 * jh + ih] * a)
            sab = []
            for ih in range(8):
                row = _rsum(acc[ih])
                sa_ref[t, ih:ih + 1, :] = row
                sab.append(_bc(row))
            yacc = [None] * 8
            for jh in range(8):
                w, B, k, r = tile(1, jh), tile(2, jh), tile(3, jh), tile(4, jh)
                for ih in range(8):
                    s = st_ref[8 * jh + ih] * w + B * sab[ih] + k * vb[ih]
                    st_ref[8 * jh + ih] = s
                    _plus(yacc, ih, s * r)
            for ih in range(8):
                yi_ref[t, ih:ih + 1, :] = _rsum(yacc[ih])

        def pair(p, carry):
            t = 2 * p
            expand(t + 1, e1)
            step(t, e0)
            expand(jnp.minimum(t + 2, L - 1), e0)
            step(t + 1, e1)
            return carry

        lax.fori_loop(0, L // 2, pair, 0)
        _tiles_to_perm([yi_ref, sa_ref], L)

    tile = pl.BlockSpec((L, 8, 128), lambda c: (c, 0, 0))
    return pl.pallas_call(
        body, name="rwkv_scan_fwd", grid=(nch,), in_specs=[tile] * 6,
        out_specs=[tile, tile, pl.BlockSpec((1, HEAD, 8, 128), lambda c: (c, 0, 0, 0))],
        out_shape=[jax.ShapeDtypeStruct((T, 8, 128), F32)] * 2 + [jax.ShapeDtypeStruct((nch, HEAD, 8, 128), F32)],
        scratch_shapes=[pltpu.VMEM((HEAD, 8, 128), F32)] + [pltpu.VMEM((6, HEAD, 128), F32)] * 2, compiler_params=_cp(("arbitrary",)),
    )(*xes, vi)


def _scan_bwd_a(xes, dyi):
    T, L = dyi.shape[0], SCAN_L
    nch = T // L

    def body(*refs):
        xr, (dy_ref, dsa_ref, dv_ref, g_ref, e0, e1) = refs[:5], refs[5:]

        @pl.when(pl.program_id(0) == 0)
        def _():
            g_ref[...] = jnp.zeros_like(g_ref)

        expand = _expander(xr, [dy_ref])
        expand(L - 1, e0)

        def step(t, e):
            tile = lambda m, jh: e[m, 8 * jh:8 * jh + 8, :]
            dyb = [_bc(e[5, ih:ih + 1, :]) for ih in range(8)]
            dsa, dv = [None] * 8, [None] * 8
            for jh in range(8):
                B, k, r = tile(2, jh), tile(3, jh), tile(4, jh)
                for ih in range(8):
                    g = g_ref[8 * jh + ih] + r * dyb[ih]
                    g_ref[8 * jh + ih] = g
                    _plus(dsa, ih, g * B)
                    _plus(dv, ih, g * k)
            dsab = []
            for ih in range(8):
                row = _rsum(dsa[ih])
                dsa_ref[t, ih:ih + 1, :] = row
                dsab.append(_bc(row))
                dv_ref[t, ih:ih + 1, :] = _rsum(dv[ih])
            for jh in range(8):
                A, w = tile(0, jh), tile(1, jh)
                for ih in range(8):
                    g_ref[8 * jh + ih] = g_ref[8 * jh + ih] * w + A * dsab[ih]

        def pair(p, carry):
            t = L - 1 - 2 * p
            expand(t - 1, e1)
            step(t, e0)
            expand(jnp.maximum(t - 2, 0), e0)
            step(t - 1, e1)
            return carry

        lax.fori_loop(0, L // 2, pair, 0)
        _tiles_to_perm([dsa_ref, dv_ref], L)

    tile = pl.BlockSpec((L, 8, 128), lambda c: (nch - 1 - c, 0, 0))
    return pl.pallas_call(
        body, name="rwkv_scan_bwd_a", grid=(nch,), in_specs=[tile] * 6, out_specs=[tile, tile],
        out_shape=[jax.ShapeDtypeStruct((T, 8, 128), F32)] * 2,
        scratch_shapes=[pltpu.VMEM((HEAD, 8, 128), F32)] + [pltpu.VMEM((6, HEAD, 128), F32)] * 2, compiler_params=_cp(("arbitrary",)),
    )(*xes, dyi)


def _scan_bwd_b(xts, ies, ckb):
    T, L = xts[0].shape[0], SCAN_L
    nch = T // L

    def body(*refs):
        xr, er, ck_ref, dj, (hist, g_ref, e0, e1) = refs[:5], refs[5:9], refs[9], refs[10:15], refs[15:]

        @pl.when(pl.program_id(0) == 0)
        def _():
            g_ref[...] = jnp.zeros_like(g_ref)

        hist[0] = ck_ref[0]
        expand_vs = _expander(er[:2], [xr[1], xr[2], xr[3]])
        expand = _expander(er, [xr[0], xr[1], xr[4]])
        expand_vs(0, e0)

        def fstep(t, e_ref):
            w, B, k = e_ref[2, 0:8, :], e_ref[3, 0:8, :], e_ref[4, 0:8, :]
            row = lambda m, i: jnp.broadcast_to(e_ref[m, pl.ds(i, 1), :], (8, 128))
            for i in range(HEAD):
                hist[t + 1, i] = hist[t, i] * w + row(1, i) * B + row(0, i) * k

        def fpair(p, carry):
            t = 2 * p
            expand_vs(t + 1, e1)
            fstep(t, e0)
            expand_vs(jnp.minimum(t + 2, L - 1), e0)
            fstep(t + 1, e1)
            return carry

        lax.fori_loop(0, L // 2, fpair, 0)
        expand(L - 1, e0)

        def bstep(t, e_ref):
            A, w, r = e_ref[4, 0:8, :], e_ref[5, 0:8, :], e_ref[6, 0:8, :]
            row = lambda m, i: jnp.broadcast_to(e_ref[m, pl.ds(i, 1), :], (8, 128))
            acc = [_Acc() for _ in range(5)]
            for i in range(HEAD):
                dy_i, dsa_i = row(2, i), row(3, i)
                g = g_ref[i] + dy_i * r
                sp = hist[t, i]
                acc[4].add(i, hist[t + 1, i] * dy_i)
                acc[1].add(i, g * sp)
                acc[2].add(i, g * row(1, i))
                acc[3].add(i, g * row(0, i))
                acc[0].add(i, sp * dsa_i)
                g_ref[i] = g * w + dsa_i * A
            for m in range(5):
                dj[m][t] = acc[m].total()

        def bpair(p, carry):
            t = L - 1 - 2 * p
            expand(t - 1, e1)
            bstep(t, e0)
            expand(jnp.maximum(t - 2, 0), e0)
            bstep(t - 1, e1)
            return carry

        lax.fori_loop(0, L // 2, bpair, 0)
        _tiles_to_perm(dj, L)

    tile = pl.BlockSpec((L, 8, 128), lambda c: (nch - 1 - c, 0, 0))
    return pl.pallas_call(
        body, name="rwkv_scan_bwd_b", grid=(nch,),
        in_specs=[tile] * 9 + [pl.BlockSpec((1, HEAD, 8, 128), lambda c: (nch - 1 - c, 0, 0, 0))],
        out_specs=[tile] * 5, out_shape=[jax.ShapeDtypeStruct((T, 8, 128), F32)] * 5,
        scratch_shapes=[pltpu.VMEM((L + 1, HEAD, 8, 128), F32), pltpu.VMEM((HEAD, 8, 128), F32)] + [pltpu.VMEM((7, HEAD, 128), F32)] * 2,
        compiler_params=_cp(("arbitrary",)),
    )(*xts, *ies, ckb)


SWA_COLS = SWA_W + 2 * KV_W
BLK = 128


def _swa_core(n, k2a, k2b, vla, vra, vlb, vrb, sinks, *qps):
    iq = lax.broadcasted_iota(jnp.int32, (BLK, 2 * BLK), 0)
    ik = lax.broadcasted_iota(jnp.int32, (BLK, 2 * BLK), 1)
    diff = BLK + iq - ik
    valid = (diff >= 0) & (diff < WINDOW) & ((n > 0) | (ik >= BLK))
    lane = lax.broadcasted_iota(jnp.int32, (BLK, 128), 1)
    lane1 = lax.broadcasted_iota(jnp.int32, (1, 128), 1)
    nt = (((1,), (1,)), ((), ()))
    outs = []
    for pp in range(8):
        k2, vl, vr = (k2a, vla, vra) if pp < 4 else (k2b, vlb, vrb)
        qp = qps[pp]
        o = None
        for half, vv in ((0, vl), (1, vr)):
            qh = jnp.where((lane >= HEAD) == (half == 1), qp, 0.0).astype(BF16)
            s = lax.dot_general(qh, k2.astype(BF16), nt, preferred_element_type=F32) * (HEAD ** -0.5)
            s = jnp.where(valid, s, NEG_INF)
            sink = jnp.sum(jnp.where(lane1 == 2 * pp + half, sinks, 0.0), axis=1, keepdims=True)
            m = jnp.maximum(jnp.max(s, axis=1, keepdims=True), sink)
            p = jnp.exp(s - m)
            den = jnp.sum(p, axis=1, keepdims=True) + jnp.exp(sink - m)
            oh = jnp.dot((p / den).astype(BF16), vv.astype(BF16), preferred_element_type=F32)
            o = oh if o is None else o + oh
        outs.append(o)
    return jnp.concatenate(outs, axis=1)


def _swa_prep(pc, pp, b, cq, sq, ckc, skc, ckp, skp):
    zc, zp = pc + b, pp + b
    qr = zc[:, :SWA_W] * cq + _swap32(zc[:, :SWA_W]) * sq
    kc, kp = zc[:, SWA_W:SWA_W + KV_W], zp[:, SWA_W:SWA_W + KV_W]
    kb = jnp.concatenate([kp * ckp + _swap32(kp) * skp, kc * ckc + _swap32(kc) * skc], axis=0)
    vb = jnp.concatenate([zp[:, SWA_W + KV_W:], zc[:, SWA_W + KV_W:]], axis=0)
    lane = lax.broadcasted_iota(jnp.int32, kb.shape, 1)
    left = lane < HEAD
    kbr, vbr = pltpu.roll(kb, HEAD, 1), pltpu.roll(vb, HEAD, 1)
    return (jnp.where(left, kb, kbr), jnp.where(left, kbr, kb), jnp.where(left, vb, 0.0), jnp.where(left, 0.0, vbr),
            jnp.where(left, vbr, 0.0), jnp.where(left, 0.0, vb)), [qr[:, q * 128:(q + 1) * 128] for q in range(8)]


def _swa_specs(T, tabs_q, tabs_k):
    cur = lambda c: pl.BlockSpec((BLK, c), lambda n: (n, 0))
    prev = lambda c: pl.BlockSpec((BLK, c), lambda n: (jnp.maximum(n - 1, 0), 0))
    return cur, prev


def _swa_fwd(p_swa, b, sinks, cq, sq, ck, sk):
    T = p_swa.shape[0]
    cur, prev = _swa_specs(T, None, None)

    def body(pc, pp, b_ref, s_ref, cq_r, sq_r, ckc, skc, ckp, skp, o_ref):
        ops, qps = _swa_prep(pc[...], pp[...], b_ref[...], cq_r[...], sq_r[...], ckc[...], skc[...], ckp[...], skp[...])
        o_ref[...] = _swa_core(pl.program_id(0), *ops, s_ref[...], *qps).astype(o_ref.dtype)

    full = lambda a: pl.BlockSpec(a.shape, lambda n: (0, 0))
    return pl.pallas_call(
        body, name="swa_fwd", grid=(T // BLK,),
        in_specs=[cur(SWA_COLS), prev(SWA_COLS), full(b), full(sinks), cur(SWA_W), cur(SWA_W), cur(KV_W), cur(KV_W), prev(KV_W), prev(KV_W)],
        out_specs=cur(SWA_W), out_shape=jax.ShapeDtypeStruct((T, SWA_W), BF16), compiler_params=_cp(("arbitrary",)),
    )(p_swa, p_swa, b, sinks, cq, sq, ck, sk, ck, sk)


def _swa_bwd(p_swa, b, sinks, cq, sq, ck, sk, do):
    T = p_swa.shape[0]
    nb = T // BLK
    cur = lambda c: pl.BlockSpec((BLK, c), lambda s: (nb - 1 - s, 0))
    prev = lambda c: pl.BlockSpec((BLK, c), lambda s: (jnp.maximum(nb - 2 - s, 0), 0))

    def body(pc, pp, b_ref, s_ref, cq_r, sq_r, ckc, skc, ckp, skp, do_ref, dcur, db, dsk, carry):
        step = pl.program_id(0)
        n = nb - 1 - step

        @pl.when(step == 0)
        def _():
            carry[...] = jnp.zeros_like(carry)
            db[...] = jnp.zeros_like(db)
            dsk[...] = jnp.zeros_like(dsk)

        ops, qps = _swa_prep(pc[...], pp[...], b_ref[...], cq_r[...], sq_r[...], ckc[...], skc[...], ckp[...], skp[...])
        _, vjp = jax.vjp(functools.partial(_swa_core, n), *ops, s_ref[...], *qps)
        dk2a, dk2b, dvla, dvra, dvlb, dvrb, dsinks, *dqps = vjp(do_ref[...].astype(F32))
        dqr = jnp.concatenate(dqps, axis=1)
        lane = lax.broadcasted_iota(jnp.int32, dk2a.shape, 1)
        left = lane < HEAD
        dkb = jnp.where(left, dk2a + pltpu.roll(dk2a, HEAD, 1), dk2b + pltpu.roll(dk2b, HEAD, 1))
        dvb = jnp.where(left, dvla + pltpu.roll(dvra, HEAD, 1), pltpu.roll(dvlb, HEAD, 1) + dvrb)
        dq = dqr * cq_r[...] + _swap32(dqr * sq_r[...])
        dkp, dkc = dkb[:BLK], dkb[BLK:]
        dkp = dkp * ckp[...] + _swap32(dkp * skp[...])
        dkc = dkc * ckc[...] + _swap32(dkc * skc[...])
        dc = jnp.concatenate([dq, jnp.concatenate([dkc, dvb[BLK:]], axis=1) + carry[...]], axis=1)
        carry[...] = jnp.concatenate([dkp, dvb[:BLK]], axis=1)
        dcur[...] = dc.astype(dcur.dtype)
        db[...] += jnp.sum(dc, axis=0, keepdims=True)
        dsk[...] += dsinks

    full = lambda a: pl.BlockSpec(a.shape, lambda s: (0, 0))
    return pl.pallas_call(
        body, name="swa_bwd", grid=(nb,),
        in_specs=[cur(SWA_COLS), prev(SWA_COLS), full(b), full(sinks), cur(SWA_W), cur(SWA_W), cur(KV_W), cur(KV_W), prev(KV_W), prev(KV_W),
                  cur(SWA_W)],
        out_specs=[cur(SWA_COLS), full(b), full(sinks)],
        out_shape=[jax.ShapeDtypeStruct((T, SWA_COLS), BF16), jax.ShapeDtypeStruct(b.shape, F32), jax.ShapeDtypeStruct(sinks.shape, F32)],
        scratch_shapes=[pltpu.VMEM((BLK, 2 * KV_W), F32)], compiler_params=_cp(("arbitrary",)),
    )(p_swa, p_swa, b, sinks, cq, sq, ck, sk, ck, sk, do)


def _rope_tables(T):
    inv = 10000.0 ** (-jnp.arange(0, HEAD, 2, dtype=F32) / HEAD)
    ang = jnp.arange(T, dtype=F32)[:, None] * inv[None, :]
    c = jnp.concatenate([jnp.cos(ang), jnp.cos(ang)], axis=1)
    s = jnp.concatenate([-jnp.sin(ang), jnp.sin(ang)], axis=1)
    return jnp.tile(c, (1, 16)), jnp.tile(s, (1, 16)), jnp.tile(c, (1, 2)), jnp.tile(s, (1, 2))


def _xattn_core(*qkv):
    outs = []
    for h in range(XH):
        qh, kh, vh = qkv[h], qkv[XH + h], qkv[2 * XH + h]
        s = lax.dot_general(qh.astype(BF16), kh.astype(BF16), (((1,), (1,)), ((), ())), preferred_element_type=F32) * (XHD ** -0.5)
        p = jnp.exp(s - jnp.max(s, axis=1, keepdims=True))
        p = p / jnp.sum(p, axis=1, keepdims=True)
        outs.append(jnp.dot(p.astype(BF16), vh.astype(BF16), preferred_element_type=F32))
    return jnp.concatenate(outs, axis=1)


def _xattn_split(q, kv):
    return [q[:, h * XHD:(h + 1) * XHD] for h in range(XH)] + [kv[:, h * XHD:(h + 1) * XHD] for h in range(2 * XH)]


def _xattn_fwd(q, kv, tm=256):
    (o,) = _rows(lambda i, q, kv: (_xattn_core(*_xattn_split(q, kv)),), "xattn_fwd", q.shape[0], tm, [q], [kv], [(q.shape[1], BF16)], [])
    return o


def _xattn_bwd(q, kv, do, tm=256):
    def fn(i, q, do, kv):
        _, vjp = jax.vjp(_xattn_core, *_xattn_split(q, kv))
        d = vjp(do.astype(F32))
        return jnp.concatenate(d[:XH], axis=1), jnp.concatenate(d[XH:], axis=1)

    return _rows(fn, "xattn_bwd", q.shape[0], tm, [q, do], [kv], [(q.shape[1], BF16)], [(kv.shape, F32)])


def _loss_head(x, g, tgt, tm=256):
    D = x.shape[1]

    def fn(i, x, tgt, g):
        y, vjp = jax.vjp(_rms, x, g)
        err = y - tgt
        dx, dg = vjp(err * (1.0 / D))
        part = 0.5 / D * jnp.sum(jnp.sum(err * err, axis=1, keepdims=True), axis=0, keepdims=True)
        return dx, jnp.broadcast_to(part, (1, 128)), dg

    return _rows(fn, "loss_head", x.shape[0], tm, [x, tgt], [g], [(D, F32)], [((1, 128), F32), ((1, D), F32)])


def _local_step(x, mem, tgt, get_w, P, put_g):
    T = x.shape[0]
    h1 = _rms_fwd(x, P["f1_norm"], "f1_norm")
    mn = _rms_fwd(mem, P["mem_norm"], "mem_norm")
    W = dict(get_w("f1", h1))

    def f1_down(after):
        W.update(get_w("f1d", after))
        return W["f1_down"]

    x1, s1 = _ffn_fwd(x, h1, W["f1_gate"], W["f1_up"], f1_down, "f1")

    h2 = _rms_fwd(x1, P["mix_norm"], "mix_norm")
    W.update(get_w("mix", h2))
    w_rkv, w_lora, w_swa = W["w_inT"][:3 * RW_W], W["w_inT"][3 * RW_W:SHIFT_COLS], W["w_inT"][SHIFT_COLS:]
    p_rkv = _mm(h2, w_rkv, "nt", "in_rkv", after=W.get("_after"))
    p_lora = _mm(h2, w_lora, "nt", "in_lora")
    p_swa = _mm(h2, w_swa, "nt", "in_swa")
    w_da = jnp.concatenate([W["rw_decay_up"], W["rw_aaa_up"]], axis=0)
    pre_params = (P["rw_mu"][:, :3 * RW_W], P["rw_mu"][:, 3 * RW_W:], P["rw_w0"], P["rw_a0"], P["rw_k_k"], P["rw_k_a"], w_da,
                  W["rw_gate_up"])
    r, decay, k2, v, an, bn, g = _rwkv_pre(p_rkv, p_lora, pre_params)
    scan_vecs = (an, decay, bn, k2, r)
    xes = [_to_perm(a) for a in scan_vecs]
    v_p = _to_perm(v)
    yi, sai, ck = _scan_fwd(xes, v_p)
    y_scan = _from_perm(yi)
    y_rw = _rwkv_post(y_scan, r, k2, v, g, P["rw_lnx_w"], P["rw_lnx_b"], P["rw_r_k"])
    cq, sq, ckt, skt = _rope_tables(T)
    y_swa = _swa_fwd(p_swa, P["b_in_attn"], P["attn_sinks"], cq, sq, ckt, skt)
    ycat = jnp.concatenate([y_rw, y_swa], axis=1)
    W.update(get_w("out", ycat))
    x2 = _mm(ycat, W["w_out"], "nn", "out_proj", res=x1, bias=P["b_out"])

    hx = _rms_fwd(x2, P["xa_norm"], "xa_norm")
    W.update(get_w("xattn", hx))
    q = _mm(hx, W["w_xq"], "nn", "xq", out_dtype=BF16)
    kv = _mm(mn, W["w_xkv"], "nn", "xkv", out_dtype=BF16)
    o = _xattn_fwd(q, kv)
    x3 = _mm(o, W["w_xo"], "nn", "xo", res=x2)

    h3 = _rms_fwd(x3, P["f2_norm"], "f2_norm")
    W.update(get_w("f2", h3))
    x4, s2 = _ffn_fwd(x3, h3, W["f2_gate"], W["f2_up"], W["f2_down"], "f2")
    dx4, loss_part, d_final = _loss_head(x4, P["final_norm"], tgt)

    gs = {"final_norm": d_final}
    dx3, gs["f2_norm"] = _ffn_bwd(x3, P["f2_norm"], W["f2_gate"], W["f2_up"], W["f2_down"], s2, dx4, "f2", put_g)

    do = _mm(dx3, W["w_xo"], "nt", "xo_do", out_dtype=BF16)
    dw_xo = _mm(o, dx3, "tn", "xo_dw", out_dtype=BF16)
    dq, dkv = _xattn_bwd(q, kv, do)
    dw_xq = _mm(hx, dq, "tn", "xq_dw", out_dtype=BF16)
    dw_xkv = _mm(mn, dkv, "tn", "xkv_dw", out_dtype=BF16)
    sent = put_g("xattn", {"w_xq": dw_xq, "w_xkv": dw_xkv, "w_xo": dw_xo})
    dhx = _mm(dq, W["w_xq"], "nt", "xq_dh", after=sent)
    dmn = _mm(dkv, W["w_xkv"], "nt", "xkv_dmn")
    _, gs["mem_norm"], _ = _rms_bwd(mem, P["mem_norm"], dmn, jnp.zeros_like(mem), "mem_norm_bwd")
    dx2, gs["xa_norm"], gs["b_out"] = _rms_bwd(x2, P["xa_norm"], dhx, dx3, "xa_norm_bwd")

    dycat = _mm(dx2, W["w_out"], "nt", "out_dy")
    dw_out = _mm(ycat, dx2, "tn", "out_dw", out_dtype=BF16)
    dp_swa, gs["b_in_attn"], gs["attn_sinks"] = _swa_bwd(p_swa, P["b_in_attn"], P["attn_sinks"], cq, sq, ckt, skt, dycat[:, RW_W:])
    dy_scan, dr_b, dk2_b, dv_b, dg, gs["rw_lnx_w"], gs["rw_lnx_b"], gs["rw_r_k"] = _rwkv_post_bwd(
        y_scan, r, k2, v, g, P["rw_lnx_w"], P["rw_lnx_b"], P["rw_r_k"], dycat[:, :RW_W])
    dy_p = _to_perm(dy_scan)
    dsai, dvi = _scan_bwd_a(xes, dy_p)
    dj = _scan_bwd_b(xes, [v_p, sai, dy_p, dsai], _ck_a_to_b(ck))
    dan, ddecay, dbn, dk2_s, dr_s = (_from_perm(d) for d in dj)
    cts = (dr_s, ddecay, dk2_s, _from_perm(dvi), dan, dbn, dg, dr_b, dk2_b, dv_b)
    dp_rkv, dp_lora, dmu, dmul, gs["rw_w0"], gs["rw_a0"], gs["rw_k_k"], gs["rw_k_a"], dw_da, gs["rw_gate_up"] = _rwkv_pre_bwd(
        p_rkv, p_lora, pre_params, cts)
    gs["rw_mu"] = jnp.concatenate([dmu, dmul], axis=1)
    gs["rw_decay_up"], gs["rw_aaa_up"] = dw_da[:DECAY_LORA], dw_da[DECAY_LORA:]
    dw_inT = jnp.concatenate([_mm(dp_rkv, h2, "tn", "in_dw_rkv"), _mm(dp_lora, h2, "tn", "in_dw_lora"),
                              _mm(dp_swa, h2, "tn", "in_dw_swa")], axis=0)
    sent = put_g("mix", {"w_in": dw_inT, "w_out": dw_out})
    dh2 = _mm(dp_rkv, w_rkv, "nn", "in_dh_rkv", after=sent)
    dh2 = _mm(dp_lora, w_lora, "nn", "in_dh_lora", res=dh2)
    dh2 = _mm(dp_swa, w_swa, "nn", "in_dh_swa", res=dh2)
    dx1, gs["mix_norm"], _ = _rms_bwd(x1, P["mix_norm"], dh2, dx2, "mix_norm_bwd")

    dx0, gs["f1_norm"] = _ffn_bwd(x, P["f1_norm"], W["f1_gate"], W["f1_up"], W["f1_down"], s1, dx1, "f1", put_g)
    return loss_part, dx0, gs


_ANY = pl.BlockSpec(memory_space=pl.ANY)
_OTHER_CHIPS = ((1, 0), (0, 1), (1, 1))


def _mesh_pos():
    return lax.axis_index("x"), lax.axis_index("y"), lax.axis_index("c")


def _slot(ref, kind, s, rows, cols):
    if kind == "row":
        return ref.at[pl.ds(pl.multiple_of(s * rows, 8), rows), :]
    return ref.at[:, pl.ds(pl.multiple_of(s * cols, 128), cols)]


_HBM = pl.BlockSpec(memory_space=pltpu.HBM)
_SEMS = pl.BlockSpec(memory_space=pltpu.SEMAPHORE)
_SPLIT = dict(compiler_params=pltpu.CompilerParams(has_side_effects=pltpu.SideEffectType.DATAFLOW_SIDE_EFFECTING))


def _in_hbm(a):
    return pltpu.with_memory_space_constraint(a, pltpu.HBM)


def _full_shape(s, kind):
    return (4 * s.shape[0], s.shape[1]) if kind == "row" else (s.shape[0], 4 * s.shape[1])


def _half(ref, shape, h):
    rows, cols = shape
    if rows % 32 == 0:
        return ref.at[pl.ds(pl.multiple_of(h * (rows // 2), 16), rows // 2), :]
    assert cols % 256 == 0, shape
    return ref.at[:, pl.ds(pl.multiple_of(h * (cols // 2), 128), cols // 2)]


def _half_shape(shape):
    rows, cols = shape
    return (rows // 2, cols) if rows % 32 == 0 else (rows, cols // 2)


def _streams(src, dst, shape, c):
    hs = _half_shape(shape)
    s, d = _half(src, shape, c), _half(dst, shape, c)
    return [(_half(s, hs, q), _half(d, hs, q)) for q in range(2)]


def _swap_halves(name, fulls, shard_shapes, kinds):
    n = len(fulls)

    def body(*refs):
        out, send, recv = refs[n:2 * n], refs[2 * n], refs[2 * n + 1]
        x, y, c = _mesh_pos()
        sent = []
        for i in range(n):
            for r, (dx, dy) in enumerate(_OTHER_CHIPS):
                theirs = _slot(out[i], kinds[i], 2 * ((x + dx) % 2) + (y + dy) % 2, *shard_shapes[i])
                have = _half(theirs, shard_shapes[i], c)
                rc = pltpu.make_async_remote_copy(have, have, send.at[3 * i + r], recv.at[3 * i + r], device_id=(x, y, 1 - c),
                                                  device_id_type=MESH)
                rc.start()
                sent.append(rc)
        for i in range(n):
            for r, (dx, dy) in enumerate(_OTHER_CHIPS):
                theirs = _slot(out[i], kinds[i], 2 * ((x + dx) % 2) + (y + dy) % 2, *shard_shapes[i])
                need = _half(theirs, shard_shapes[i], 1 - c)
                pltpu.make_async_remote_copy(need, need, send.at[3 * i + r], recv.at[3 * i + r], device_id=(x, y, c),
                                             device_id_type=MESH).wait_recv()
        for rc in sent:
            rc.wait_send()

    return pl.pallas_call(
        body, name=name, in_specs=[_ANY] * n, out_specs=[_ANY] * n, out_shape=[jax.ShapeDtypeStruct(f.shape, f.dtype) for f in fulls],
        input_output_aliases={i: i for i in range(n)},
        scratch_shapes=[pltpu.SemaphoreType.DMA((3 * n,)), pltpu.SemaphoreType.DMA((3 * n,))],
    )(*fulls)


def _gather_start(name, shards, kinds, groups, after=None):
    n, ng = len(shards), len(groups)
    lands = [_in_hbm(lax.empty(_full_shape(s, k), s.dtype)) for s, k in zip(shards, kinds)]
    n_in = 2 * n + (after is not None)

    def body(*refs):
        src, land, sems, token = refs[:n], refs[n:2 * n], refs[n_in:n_in + 3 * ng], refs[-1]
        x, y, c = _mesh_pos()
        me = 2 * x + y
        for gi, idxs in enumerate(groups):
            send, recv, own = sems[3 * gi:3 * gi + 3]
            for k, i in enumerate(idxs):
                mine = _slot(land[i], kinds[i], me, *src[i].shape)
                for r, (dx, dy) in enumerate(_OTHER_CHIPS):
                    for q, (s, d) in enumerate(_streams(src[i], mine, src[i].shape, c)):
                        pltpu.make_async_remote_copy(s, d, send.at[6 * k + 2 * r + q], recv.at[6 * k + 2 * r + q],
                                                     device_id=((x + dx) % 2, (y + dy) % 2, c), device_id_type=MESH).start()
                pltpu.make_async_copy(src[i], mine, own.at[k]).start()
        token[...] = jnp.zeros_like(token)

    sem_shapes = [pltpu.SemaphoreType.DMA((w * len(g),)) for g in groups for w in (6, 6, 1)]
    thru = [pltpu.HBM(a.shape, a.dtype) for a in (*shards, *lands)]
    res = pl.pallas_call(
        body, name=name, in_specs=[_HBM] * (2 * n) + [_ANY] * (after is not None),
        out_specs=[_SEMS] * (3 * ng) + [_HBM] * (2 * n) + [pl.BlockSpec(memory_space=pltpu.VMEM)],
        out_shape=sem_shapes + thru + [jax.ShapeDtypeStruct((8, 128), F32)],
        input_output_aliases={i: 3 * ng + i for i in range(2 * n)}, **_SPLIT,
    )(*[_in_hbm(s) for s in shards], *lands, *([] if after is None else [after]))
    return res[:3 * ng], res[3 * ng:3 * ng + n], res[3 * ng + n:3 * ng + 2 * n], res[-1]


def _gather_wait(name, sems, shards, lands, kinds, after):
    m = len(shards)

    def body(*refs):
        src, land, (send, recv, own) = refs[:m], refs[m:2 * m], refs[2 * m:2 * m + 3]
        x, y, c = _mesh_pos()
        me = 2 * x + y
        for k in range(m):
            mine = _slot(land[k], kinds[k], me, *src[k].shape)
            for r in range(3):
                for q, (s, d) in enumerate(_streams(src[k], mine, src[k].shape, c)):
                    cp = pltpu.make_async_remote_copy(s, d, send.at[6 * k + 2 * r + q], recv.at[6 * k + 2 * r + q], device_id=(x, y, c),
                                                      device_id_type=MESH)
                    cp.wait_send()
                    cp.wait_recv()
            pltpu.make_async_copy(src[k], mine, own.at[k]).wait()

    thru = [pltpu.HBM(a.shape, a.dtype) for a in (*shards, *lands)]
    res = pl.pallas_call(
        body, name=name, in_specs=[_HBM] * (2 * m) + [_SEMS] * 3 + [pl.BlockSpec(memory_space=pl.ANY)],
        out_specs=[_HBM] * (2 * m), out_shape=thru, input_output_aliases={i: i for i in range(2 * m)}, **_SPLIT,
    )(*shards, *lands, *sems, after)
    return res[m:]


def _scatter_start(name, grads, kinds):
    m = len(grads)
    shard_shape = [(g.shape[0] // 4, g.shape[1]) if k == "row" else (g.shape[0], g.shape[1] // 4) for g, k in zip(grads, kinds)]
    lands = [_in_hbm(lax.empty((4, *s), g.dtype)) for s, g in zip(shard_shape, grads)]

    def body(*refs):
        src, land, (send, recv, own) = refs[:m], refs[m:2 * m], refs[2 * m:2 * m + 3]
        x, y, c = _mesh_pos()
        me = 2 * x + y
        for k in range(m):
            for r, (dx, dy) in enumerate(_OTHER_CHIPS):
                tx, ty = (x + dx) % 2, (y + dy) % 2
                pltpu.make_async_remote_copy(_slot(src[k], kinds[k], 2 * tx + ty, *shard_shape[k]), land[k].at[me],
                                             send.at[3 * k + r], recv.at[3 * k + r], device_id=(tx, ty, c), device_id_type=MESH).start()
            pltpu.make_async_copy(_slot(src[k], kinds[k], me, *shard_shape[k]), land[k].at[me], own.at[k]).start()
        refs[-1][...] = jnp.zeros_like(refs[-1])

    thru = [pltpu.HBM(a.shape, a.dtype) for a in (*grads, *lands)]
    res = pl.pallas_call(
        body, name=name, in_specs=[_HBM] * (2 * m),
        out_specs=[_SEMS] * 3 + [_HBM] * (2 * m) + [pl.BlockSpec(memory_space=pltpu.VMEM)],
        out_shape=[pltpu.SemaphoreType.DMA((3 * m,))] * 2 + [pltpu.SemaphoreType.DMA((m,))] + thru + [jax.ShapeDtypeStruct((8, 128), F32)],
        input_output_aliases={i: 3 + i for i in range(2 * m)}, **_SPLIT,
    )(*[_in_hbm(g) for g in grads], *lands)
    return res[:3], res[3:3 + m], res[3 + m:3 + 2 * m], res[-1]


def _scatter_wait(name, sems, grads, lands, kinds, after):
    m = len(grads)

    def body(*refs):
        src, land, (send, recv, own) = refs[:m], refs[m:2 * m], refs[2 * m:2 * m + 3]
        x, y, c = _mesh_pos()
        me = 2 * x + y
        for k in range(m):
            mine = _slot(src[k], kinds[k], me, *land[k].shape[1:])
            for r in range(3):
                cp = pltpu.make_async_remote_copy(mine, land[k].at[me], send.at[3 * k + r], recv.at[3 * k + r],
                                                  device_id=(x, y, c), device_id_type=MESH)
                cp.wait_send()
                cp.wait_recv()
            pltpu.make_async_copy(mine, land[k].at[me], own.at[k]).wait()

    thru = [pltpu.HBM(a.shape, a.dtype) for a in (*grads, *lands)]
    res = pl.pallas_call(
        body, name=name, in_specs=[_HBM] * (2 * m) + [_SEMS] * 3 + [pl.BlockSpec(memory_space=pl.ANY)],
        out_specs=[_HBM] * (2 * m), out_shape=thru, input_output_aliases={i: i for i in range(2 * m)}, **_SPLIT,
    )(*grads, *lands, *sems, after)
    return res[m:]


def _swap_with_sibling(arrs, name):
    n = len(arrs)

    def body(*refs):
        ins, outs = refs[:n], refs[n:2 * n]
        send, recv = refs[2 * n:]
        x, y, c = _mesh_pos()
        copies = []
        for i in range(n):
            rc = pltpu.make_async_remote_copy(ins[i], outs[i], send.at[i], recv.at[i], device_id=(x, y, 1 - c), device_id_type=MESH)
            rc.start()
            copies.append(rc)
        for rc in copies:
            rc.wait()

    return pl.pallas_call(
        body, name=name, in_specs=[_ANY] * n, out_specs=[_ANY] * n,
        out_shape=[jax.ShapeDtypeStruct(a.shape, a.dtype) for a in arrs],
        scratch_shapes=[pltpu.SemaphoreType.DMA((n,)), pltpu.SemaphoreType.DMA((n,))],
    )(*arrs)


def _small_start(pack, after):
    land = _in_hbm(lax.empty((8, *pack.shape), pack.dtype))

    def body(in_ref, land_ref, after_ref, send, recv, own, in_thru, land_thru, token):
        x, y, c = _mesh_pos()
        me = 4 * x + 2 * y + c
        for r in range(1, 8):
            dx, dy, dc = r // 4, (r // 2) % 2, r % 2
            pltpu.make_async_remote_copy(in_ref, land_ref.at[me], send.at[r - 1], recv.at[r - 1],
                                         device_id=((x + dx) % 2, (y + dy) % 2, (c + dc) % 2), device_id_type=MESH).start()
        pltpu.make_async_copy(in_ref, land_ref.at[me], own.at[0]).start()
        token[...] = jnp.zeros_like(token)

    res = pl.pallas_call(
        body, name="small_start", in_specs=[_HBM, _HBM, _ANY],
        out_specs=[_SEMS] * 3 + [_HBM, _HBM, pl.BlockSpec(memory_space=pltpu.VMEM)],
        out_shape=[pltpu.SemaphoreType.DMA((7,)), pltpu.SemaphoreType.DMA((7,)), pltpu.SemaphoreType.DMA((1,)),
                   pltpu.HBM(pack.shape, pack.dtype), pltpu.HBM(land.shape, land.dtype), jax.ShapeDtypeStruct((8, 128), F32)],
        input_output_aliases={0: 3, 1: 4}, **_SPLIT,
    )(_in_hbm(pack), land, after)
    return res[:3], res[3], res[4], res[5]


def _small_wait(sems, pack, land, after):
    def body(in_ref, land_ref, send, recv, own, after_ref, in_dead, got):
        x, y, c = _mesh_pos()
        me = 4 * x + 2 * y + c
        for r in range(1, 8):
            cp = pltpu.make_async_remote_copy(in_ref, land_ref.at[me], send.at[r - 1], recv.at[r - 1], device_id=(x, y, c),
                                              device_id_type=MESH)
            cp.wait_send()
            cp.wait_recv()
        pltpu.make_async_copy(in_ref, land_ref.at[me], own.at[0]).wait()

    res = pl.pallas_call(
        body, name="small_wait", in_specs=[_HBM, _HBM] + [_SEMS] * 3 + [_ANY], out_specs=[_HBM, _HBM],
        out_shape=[pltpu.HBM(pack.shape, pack.dtype), pltpu.HBM(land.shape, land.dtype)], input_output_aliases={0: 0, 1: 1}, **_SPLIT,
    )(pack, land, *sems, after)
    return res[1]


def _row_tile(R, dtype, target=256):
    mult = 8 * 4 // jnp.dtype(dtype).itemsize
    best = R
    for t in range(mult, min(R, target) + 1, mult):
        if R % t == 0:
            best = t
    return best


def _sum_slots(stack, name, out_dtype=F32):
    k, R, C = stack.shape
    tr = _row_tile(R, stack.dtype)
    tc = C
    if tr < 64:
        tr, tc = R, _pick(C, 512)

    def body(s_ref, o_ref):
        acc = s_ref[0].astype(F32)
        for j in range(1, k):
            acc = acc + s_ref[j].astype(F32)
        o_ref[...] = acc.astype(out_dtype)

    return pl.pallas_call(
        body, name=name, grid=(R // tr, C // tc), in_specs=[pl.BlockSpec((k, tr, tc), lambda i, j: (0, i, j))],
        out_specs=pl.BlockSpec((tr, tc), lambda i, j: (i, j)), out_shape=jax.ShapeDtypeStruct((R, C), out_dtype),
        compiler_params=_cp(("parallel", "parallel")),
    )(stack)


W_IN_SHARD = 1160
W_IN_PAD = 1168


def _pad_shards(a):
    zeros = jnp.zeros((W_IN_PAD - W_IN_SHARD, a.shape[1]), a.dtype)
    parts = []
    for s in range(a.shape[0] // W_IN_SHARD):
        parts += [a[s * W_IN_SHARD:(s + 1) * W_IN_SHARD], zeros]
    return jnp.concatenate(parts, axis=0).astype(BF16)


def _unpad_shards(a):
    a = a.astype(F32)
    return jnp.concatenate([a[s * W_IN_PAD:s * W_IN_PAD + W_IN_SHARD] for s in range(a.shape[0] // W_IN_PAD)], axis=0)


def _adamw(w, m, v, ga, gb, name, after=None):
    R, C = w.shape
    tr = _row_tile(R, F32, 128)
    gs = [ga] if gb is None else [ga, gb]
    extra = [] if after is None else [after]

    def body(*refs):
        w_ref, m_ref, v_ref = refs[:3]
        g = refs[3][...].astype(F32)
        if gb is not None:
            g = g + refs[4][...].astype(F32)
        g_ref, d_ref, nm_ref, nv_ref = refs[-4:]
        nm = ADAM_B1 * m_ref[...] + (1.0 - ADAM_B1) * g
        nv = ADAM_B2 * v_ref[...] + (1.0 - ADAM_B2) * (g * g)
        m_hat = nm / (1.0 - ADAM_B1 ** ADAM_STEP)
        v_hat = nv / (1.0 - ADAM_B2 ** ADAM_STEP)
        g_ref[...] = g
        d_ref[...] = -ADAM_LR * (m_hat / (jnp.sqrt(v_hat) + ADAM_EPS) + ADAM_WD * w_ref[...])
        nm_ref[...] = nm
        nv_ref[...] = nv

    spec = pl.BlockSpec((tr, C), lambda i: (i, 0))
    return pl.pallas_call(
        body, name=name, grid=(R // tr,), in_specs=[spec] * (3 + len(gs)) + [_ANY] * len(extra), out_specs=[spec] * 4,
        out_shape=[jax.ShapeDtypeStruct((R, C), F32)] * 4, compiler_params=_cp(("parallel",)),
    )(w, m, v, *gs, *extra)


def _pack(arrs):
    rows = []
    for a in arrs:
        flat = a.reshape(-1)
        rows.append(jnp.pad(flat, (0, -flat.shape[0] % 1024)).reshape(-1, 1024))
    p = jnp.concatenate(rows, axis=0)
    return jnp.pad(p, ((0, -p.shape[0] % 8), (0, 0)))


def _unpack(p, shapes):
    out, r = [], 0
    for s in shapes:
        n = 1
        for d in s:
            n *= d
        nr = -(-n // 1024)
        out.append(p[r:r + nr].reshape(-1)[:n].reshape(s))
        r += nr
    return out


BIG = ("f1_gate", "f1_up", "f1_down", "w_in", "w_out", "w_xq", "w_xkv", "w_xo", "f2_gate", "f2_up", "f2_down")
BIG_KIND = {"f1_gate": "col", "f1_up": "col", "f1_down": "row", "w_in": "row", "w_out": "row", "w_xq": "row", "w_xkv": "col",
            "w_xo": "row", "f2_gate": "col", "f2_up": "col", "f2_down": "row"}
LORA = ("rw_decay_up", "rw_aaa_up", "rw_gate_up")
WEIGHTS = ("f1_norm", "f1_gate", "f1_up", "f1_down", "mix_norm", "w_in", "b_in_attn", "rw_mu", "rw_w0", "rw_decay_up", "rw_a0",
           "rw_aaa_up", "rw_gate_up", "rw_k_k", "rw_k_a", "rw_r_k", "rw_lnx_w", "rw_lnx_b", "attn_sinks", "w_out", "b_out", "xa_norm",
           "mem_norm", "w_xq", "w_xkv", "w_xo", "f2_norm", "f2_gate", "f2_up", "f2_down", "final_norm")
SMALL = tuple(n for n in WEIGHTS if n not in BIG)
GROUP_ORDER = ("f1", "f1d", "mix", "out", "xattn", "f2")
GROUPS = {"f1": ("f1_gate", "f1_up"), "f1d": ("f1_down",), "mix": ("w_in",) + LORA, "out": ("w_out",), "xattn": ("w_xq", "w_xkv", "w_xo"),
          "f2": ("f2_gate", "f2_up", "f2_down")}


def kernel(x, mem, f1_norm, f1_gate, f1_up, f1_down, mix_norm, w_in, b_in_attn, rw_mu, rw_w0, rw_decay_up, rw_a0, rw_aaa_up, rw_gate_up, rw_k_k, rw_k_a, rw_r_k, rw_lnx_w, rw_lnx_b, attn_sinks, w_out, b_out, xa_norm, mem_norm, w_xq, w_xkv, w_xo, f2_norm, f2_gate, f2_up, f2_down, final_norm, loss_target, m_f1_norm, m_f1_gate, m_f1_up, m_f1_down, m_mix_norm, m_w_in, m_b_in_attn, m_rw_mu, m_rw_w0, m_rw_decay_up, m_rw_a0, m_rw_aaa_up, m_rw_gate_up, m_rw_k_k, m_rw_k_a, m_rw_r_k, m_rw_lnx_w, m_rw_lnx_b, m_attn_sinks, m_w_out, m_b_out, m_xa_norm, m_mem_norm, m_w_xq, m_w_xkv, m_w_xo, m_f2_norm, m_f2_gate, m_f2_up, m_f2_down, m_final_norm, v_f1_norm, v_f1_gate, v_f1_up, v_f1_down, v_mix_norm, v_w_in, v_b_in_attn, v_rw_mu, v_rw_w0, v_rw_decay_up, v_rw_a0, v_rw_aaa_up, v_rw_gate_up, v_rw_k_k, v_rw_k_a, v_rw_r_k, v_rw_lnx_w, v_rw_lnx_b, v_attn_sinks, v_w_out, v_b_out, v_xa_norm, v_mem_norm, v_w_xq, v_w_xkv, v_w_xo, v_f2_norm, v_f2_gate, v_f2_up, v_f2_down, v_final_norm):
    a = dict(locals())
    w = {n: a[n] for n in WEIGHTS}
    m = {n: a["m_" + n] for n in WEIGHTS}
    v = {n: a["v_" + n] for n in WEIGHTS}
    sq = lambda t: t.reshape(t.shape[-2:]) if t.ndim == 3 else t.reshape(1, -1)

    local_name = lambda n: "w_inT" if n == "w_in" else n
    kind_of = lambda n: BIG_KIND.get(n, "col")
    payload = lambda n: _pad_shards(sq(w[n]).T) if n == "w_in" else sq(w[n]) if n in LORA else sq(w[n]).astype(BF16)
    gathers = {}

    def start_gather(name, grps, after):
        shards = [payload(n) for g in grps for n in GROUPS[g]]
        kinds = [kind_of(n) for g in grps for n in GROUPS[g]]
        groups, at = [], 0
        for g in grps:
            groups.append(list(range(at, at + len(GROUPS[g]))))
            at += len(GROUPS[g])
        sems, src_thru, land_thru, token = _gather_start(name, shards, kinds, groups, after)
        for gi, g in enumerate(grps):
            gathers[g] = (sems[3 * gi:3 * gi + 3], [src_thru[i] for i in groups[gi]], [land_thru[i] for i in groups[gi]],
                          [kinds[i] for i in groups[gi]], token)

    early = GROUP_ORDER[:3]
    start_gather("gather_start_f1", early[:1], None)
    start_gather("gather_start", early[1:], gathers[early[0]][4])

    def get_w(grp, after):
        g_sems, g_src, g_land, g_kinds, token = gathers[grp]
        got = _gather_wait("gather_wait_" + grp, g_sems, g_src, g_land, g_kinds, token if after is None else after)
        got = _swap_halves("gather_swap_" + grp, got, [s.shape for s in g_src], g_kinds)
        out = {local_name(n): (_unpad_shards(f) if n == "w_in" else f) for n, f in zip(GROUPS[grp], got)}
        if grp == early[-1]:
            start_gather("gather_start_late", GROUP_ORDER[3:], got[0])
            out["_after"] = gathers[GROUP_ORDER[3]][4]
        return out

    in_flight = []

    def put_g(label, gw):
        names = list(gw)
        grads = [_pad_shards(gw[n]) if n == "w_in" else gw[n] for n in names]
        *flight, sent = _scatter_start("scatter_start_" + label, grads, [kind_of(n) for n in names])
        in_flight.append((label, names, flight))
        return sent

    P = {n: sq(w[n]) for n in SMALL if n not in LORA}
    P["attn_sinks"] = jnp.pad(P["attn_sinks"], ((0, 0), (0, 128 - P["attn_sinks"].shape[1])))
    P["rw_r_k"] = w["rw_r_k"].reshape(1, RW_W)
    loss_part, grad_x, gs = _local_step(x[0], mem[0], loss_target[0], get_w, P, put_g)

    gs["attn_sinks"] = gs["attn_sinks"][:, :16]
    small_flight = _small_start(_pack([gs[n] for n in SMALL] + [loss_part]), grad_x)

    out, after = {}, small_flight[-1]
    for bi, batch in enumerate((in_flight[:-3], in_flight[-3:])):
        b_names, b_partial = [], []
        for label, names, (g_sems, g_thru, l_thru) in batch:
            stacks = _scatter_wait("scatter_wait_" + label, g_sems, g_thru, l_thru, [kind_of(n) for n in names], after)
            partial = [_sum_slots(s, "sum_chips_" + n, F32 if n == "w_in" else BF16) for s, n in zip(stacks, names)]
            b_names += names
            b_partial += partial
            after = partial[-1]
        sibling = _swap_with_sibling(b_partial, "swap_batch%d" % bi)
        chain = None
        for n, pa, sb in zip(b_names, b_partial, sibling):
            if n == "w_in":
                pa, sb = pa[:W_IN_SHARD].T, sb[:W_IN_SHARD].T
            out[n] = _adamw(sq(w[n]), sq(m[n]), sq(v[n]), pa, sb, "adamw_" + n, after=chain)
            chain = out[n][1]
        after = chain

    gsum = _sum_slots(_small_wait(*small_flight[:-1], after), "sum_small")
    *summed, loss_row = _unpack(gsum, [gs[n].shape for n in SMALL] + [loss_part.shape])
    g_small = dict(zip(SMALL, summed))
    loss = loss_row[0, 0]
    shard = 2 * lax.axis_index("x") + lax.axis_index("y")
    for n in LORA:
        cols = w[n].shape[-1]
        g_small[n] = lax.dynamic_slice_in_dim(g_small[n], shard * cols, cols, axis=1)

    flat = lambda d: _pack([d[n] for n in SMALL])
    res = _adamw(flat(w), flat(m), flat(v), _pack([g_small[n] for n in SMALL]), None, "adamw_small")
    shapes = [w[n].shape for n in SMALL]
    for k, p in enumerate(res):
        for n, t in zip(SMALL, _unpack(p, shapes)):
            out.setdefault(n, [None] * 4)[k] = t
    outs = [loss, grad_x.reshape(x.shape)]
    for k in range(4):
        outs += [out[n][k].reshape(w[n].shape) for n in WEIGHTS]
    return tuple(outs)
```

```python
import functools

import jax
import jax.numpy as jnp
from jax import lax
from jax.experimental import pallas as pl
from jax.experimental.pallas import tpu as pltpu

F32, BF16 = jnp.float32, jnp.bfloat16
MESH = pl.DeviceIdType.MESH

HEAD = 64
RW_HEADS = 16
RW_W = 1024
SWA_W = 1024
KV_W = 128
DECAY_LORA, AAA_LORA, GATE_LORA = 64, 64, 160
LORA_W = DECAY_LORA + AAA_LORA + GATE_LORA
SHIFT_COLS = 3 * RW_W + LORA_W
XH = 4
XHD = 512
MEM_LEN = 256
WINDOW = 128
GN_EPS = 64e-5
RMS_EPS = 1e-6
NEG_INF = -1e30
ADAM_LR, ADAM_B1, ADAM_B2, ADAM_EPS, ADAM_WD, ADAM_STEP = 0.001, 0.9, 0.999, 1e-08, 0.01, 10

VMEM_LIMIT = 56 * 1024 * 1024


def _cp(sem=None, **kw):
    return pltpu.CompilerParams(dimension_semantics=sem, vmem_limit_bytes=VMEM_LIMIT, **kw)


def _pick(dim, target):
    if dim <= target:
        return dim
    best = None
    for t in range(128, target + 1, 128):
        if dim % t == 0:
            best = t
    assert best is not None, (dim, target)
    return best


_DIMS = {"nn": (((1,), (0,)), ((), ())), "nt": (((1,), (1,)), ((), ())), "tn": (((0,), (0,)), ((), ()))}


def _mm(a, b, mode, name, out_dtype=F32, alpha=1.0, res=None, bias=None, tm=1024, tn=1024, tk=2048, after=None):
    if mode == "nn":
        (M, K), (K2, N) = a.shape, b.shape
    elif mode == "nt":
        (M, K), (N, K2) = a.shape, b.shape
    else:
        (K, M), (K2, N) = a.shape, b.shape
    assert K == K2, (name, a.shape, b.shape)
    tm, tn, tk = _pick(M, tm), _pick(N, tn), _pick(K, tk)
    nk = K // tk
    a_spec = pl.BlockSpec((tk, tm), lambda i, j, k: (k, i)) if mode == "tn" else pl.BlockSpec((tm, tk), lambda i, j, k: (i, k))
    b_spec = pl.BlockSpec((tn, tk), lambda i, j, k: (j, k)) if mode == "nt" else pl.BlockSpec((tk, tn), lambda i, j, k: (k, j))
    o_spec = pl.BlockSpec((tm, tn), lambda i, j, k: (i, j))
    ins, specs = [a, b], [a_spec, b_spec]
    if res is not None:
        ins.append(res)
        specs.append(o_spec)
    if bias is not None:
        ins.append(bias)
        specs.append(pl.BlockSpec((1, tn), lambda i, j, k: (0, j)))
    if after is not None:
        ins.append(after)
        specs.append(pl.BlockSpec(memory_space=pl.ANY))
    dims = _DIMS[mode]

    def body(*refs):
        a_ref, b_ref = refs[0], refs[1]
        part = lax.dot_general(a_ref[...].astype(BF16), b_ref[...].astype(BF16), dims, preferred_element_type=F32)

        def finish(o, o_ref):
            if alpha != 1.0:
                o = o * alpha
            p = 2
            if res is not None:
                o = o + refs[p][...].astype(F32)
                p += 1
            if bias is not None:
                o = o + refs[p][...]
            o_ref[...] = o.astype(out_dtype)

        if nk == 1:
            finish(part, refs[-1])
            return
        o_ref, acc_ref = refs[-2], refs[-1]
        k = pl.program_id(2)

        @pl.when(k == 0)
        def _():
            acc_ref[...] = part

        @pl.when(k > 0)
        def _():
            acc_ref[...] += part

        @pl.when(k == nk - 1)
        def _():
            finish(acc_ref[...], o_ref)

    return pl.pallas_call(
        body, name=name, grid=(M // tm, N // tn, nk), in_specs=specs, out_specs=o_spec,
        out_shape=jax.ShapeDtypeStruct((M, N), out_dtype), scratch_shapes=[pltpu.VMEM((tm, tn), F32)] * (nk > 1),
        compiler_params=_cp(("parallel", "parallel", "arbitrary")),
    )(*ins)


def _rows(fn, name, T, tm, tiled, full, out_tiled, out_acc, extra=(), reverse=False, scratch=()):
    n = T // tm
    idx = (lambda i: n - 1 - i) if reverse else (lambda i: i)
    in_specs = [pl.BlockSpec((tm, a.shape[1]), lambda i: (idx(i), 0)) for a in tiled]
    in_specs += [mk(idx) for _, mk in extra]
    in_specs += [pl.BlockSpec(a.shape, lambda i, nd=a.ndim: (0,) * nd) for a in full]
    out_specs = [pl.BlockSpec((tm, c), lambda i: (idx(i), 0)) for c, _ in out_tiled]
    out_specs += [pl.BlockSpec(s, lambda i, nd=len(s): (0,) * nd) for s, _ in out_acc]
    out_shape = [jax.ShapeDtypeStruct((T, c), d) for c, d in out_tiled] + [jax.ShapeDtypeStruct(s, d) for s, d in out_acc]
    n_in = len(tiled) + len(extra) + len(full)
    n_t, n_a = len(out_tiled), len(out_acc)

    def body(*refs):
        step = pl.program_id(0)
        vals = [r[...] for r in refs[:n_in]]
        outs = fn(idx(step), *vals, *refs[n_in + n_t + n_a:])
        for r, v in zip(refs[n_in:n_in + n_t], outs[:n_t]):
            r[...] = v.astype(r.dtype)
        for r, v in zip(refs[n_in + n_t:n_in + n_t + n_a], outs[n_t:]):
            @pl.when(step == 0)
            def _(r=r):
                r[...] = jnp.zeros_like(r)

            r[...] += v

    return pl.pallas_call(
        body, name=name, grid=(n,), in_specs=in_specs, out_specs=out_specs, out_shape=out_shape,
        scratch_shapes=list(scratch), compiler_params=_cp(("arbitrary",)),
    )(*tiled, *[a for a, _ in extra], *full)


def _rms(x, g):
    return x * lax.rsqrt(jnp.mean(x * x, axis=-1, keepdims=True) + RMS_EPS) * g


def _rms_fwd(x, g, name, tm=256):
    (h,) = _rows(lambda i, x, g: (_rms(x, g),), name, x.shape[0], min(tm, x.shape[0]), [x], [g], [(x.shape[1], BF16)], [])
    return h


def _rms_bwd(x, g, dh, dres, name, tm=256):
    D = x.shape[1]

    def fn(i, x, dh, dres, g):
        _, vjp = jax.vjp(_rms, x, g)
        dx, dg = vjp(dh.astype(F32))
        dx = dx + dres
        return dx, dg, jnp.sum(dx, axis=0, keepdims=True)

    return _rows(fn, name, x.shape[0], tm, [x, dh, dres], [g], [(D, F32)], [((1, D), F32), ((1, D), F32)])


def _ffn_up(h, wg, wu, name, tm=1024, tn=512, after=None):
    (M, K), N = h.shape, wg.shape[1]
    tm, tn = _pick(M, tm), _pick(N, tn)

    def body(*refs):
        h_ref, wg_ref, wu_ref = refs[:3]
        g_ref, u_ref, a_ref = refs[-3:]
        hb = h_ref[...].astype(BF16)
        g = jnp.dot(hb, wg_ref[...].astype(BF16), preferred_element_type=F32)
        u = jnp.dot(hb, wu_ref[...].astype(BF16), preferred_element_type=F32)
        g_ref[...] = g
        u_ref[...] = u
        a_ref[...] = (g * jax.nn.sigmoid(g) * u).astype(BF16)

    o_spec = pl.BlockSpec((tm, tn), lambda i, j: (i, j))
    w_spec = pl.BlockSpec((K, tn), lambda i, j: (0, j))
    extra = [] if after is None else [after]
    return pl.pallas_call(
        body, name=name, grid=(M // tm, N // tn),
        in_specs=[pl.BlockSpec((tm, K), lambda i, j: (i, 0)), w_spec, w_spec] + [pl.BlockSpec(memory_space=pl.ANY)] * len(extra),
        out_specs=[o_spec] * 3, out_shape=[jax.ShapeDtypeStruct((M, N), F32)] * 2 + [jax.ShapeDtypeStruct((M, N), BF16)],
        compiler_params=_cp(("parallel", "parallel")),
    )(h, wg, wu, *extra)


def _ffn_dact(dxo, wd, g, u, name, tm=1024, tn=512):
    (M, K), N = dxo.shape, wd.shape[0]
    tm, tn = _pick(M, tm), _pick(N, tn)

    def body(dx_ref, wd_ref, g_ref, u_ref, dg_ref, du_ref):
        da = 0.5 * lax.dot_general(dx_ref[...].astype(BF16), wd_ref[...].astype(BF16), _DIMS["nt"], preferred_element_type=F32)
        g = g_ref[...]
        s = jax.nn.sigmoid(g)
        dg_ref[...] = (da * u_ref[...] * (s * (1.0 + g * (1.0 - s)))).astype(BF16)
        du_ref[...] = (da * (g * s)).astype(BF16)

    t_spec = pl.BlockSpec((tm, tn), lambda i, j: (i, j))
    return pl.pallas_call(
        body, name=name, grid=(M // tm, N // tn),
        in_specs=[pl.BlockSpec((tm, K), lambda i, j: (i, 0)), pl.BlockSpec((tn, K), lambda i, j: (j, 0)), t_spec, t_spec],
        out_specs=[t_spec, t_spec], out_shape=[jax.ShapeDtypeStruct((M, N), BF16)] * 2, compiler_params=_cp(("parallel", "parallel")),
    )(dxo, wd, g, u)


def _ffn_fwd(x, h, wg, wu, wd, tag, after=None):
    G, U, A = _ffn_up(h, wg, wu, tag + "_up", after=after)
    xo = _mm(A, wd(A) if callable(wd) else wd, "nn", tag + "_down", alpha=0.5, res=x)
    return xo, (h, G, U, A)


def _ffn_bwd(x, gain, wg, wu, wd, saved, dxo, tag, send):
    h, G, U, A = saved
    dwd = _mm(A, dxo, "tn", tag + "_dwd", out_dtype=BF16, alpha=0.5, tm=1408)
    sent = send(tag + "_down", {tag + "_down": dwd})
    dG, dU = _ffn_dact(dxo, wd, G, U, tag + "_dact")
    dwu = _mm(h, dU, "tn", tag + "_dwu", out_dtype=BF16, after=sent)
    sent = send(tag + "_up", {tag + "_up": dwu})
    dwg = _mm(h, dG, "tn", tag + "_dwg", out_dtype=BF16, after=sent)
    sent = send(tag + "_gate", {tag + "_gate": dwg})
    dh = _mm(dG, wg, "nt", tag + "_dh_g", after=sent)
    dh = _mm(dU, wu, "nt", tag + "_dh_u", res=dh)
    dx, dgain, _ = _rms_bwd(x, gain, dh, dxo, tag + "_norm_bwd")
    return dx, dgain


def _segsum64_impl(x):
    r = lax.broadcasted_iota(jnp.int32, (128, 128), 0) // HEAD
    c = lax.broadcasted_iota(jnp.int32, (128, 128), 1) // HEAD
    ones = (r == c).astype(BF16)
    hi = x.astype(BF16)
    lo = (x - hi.astype(F32)).astype(BF16)
    outs = []
    for q in range(x.shape[1] // 128):
        sl = slice(q * 128, (q + 1) * 128)
        outs.append(jnp.dot(hi[:, sl], ones, preferred_element_type=F32) + jnp.dot(lo[:, sl], ones, preferred_element_type=F32))
    return outs[0] if len(outs) == 1 else jnp.concatenate(outs, axis=1)


@jax.custom_vjp
def _segsum64(x):
    return _segsum64_impl(x)


_segsum64.defvjp(lambda x: (_segsum64_impl(x), None), lambda _, ct: (_segsum64_impl(ct),))


def _swap32(x):
    lane = lax.broadcasted_iota(jnp.int32, (x.shape[0], 128), 1)
    outs = [jnp.take_along_axis(x[:, q * 128:(q + 1) * 128], lane ^ 32, axis=1) for q in range(x.shape[1] // 128)]
    return outs[0] if len(outs) == 1 else jnp.concatenate(outs, axis=1)


def _tree_sum(xs):
    xs = list(xs)
    while len(xs) > 1:
        nxt = [xs[i] + xs[i + 1] for i in range(0, len(xs) - 1, 2)]
        if len(xs) % 2:
            nxt.append(xs[-1])
        xs = nxt
    return xs[0]


class _Acc:
    def __init__(self, ways=4):
        self.parts = [None] * ways

    def add(self, i, term):
        k = i % len(self.parts)
        self.parts[k] = term if self.parts[k] is None else self.parts[k] + term

    def total(self):
        return _tree_sum([p for p in self.parts if p is not None])


def _softplus(x):
    return jnp.maximum(x, 0.0) + jnp.log(1.0 + jnp.exp(-jnp.abs(x)))


def _pre_core(k, da, gd, w0, a0, k_k, k_a, w_da, gate_up):
    lane = lax.broadcasted_iota(jnp.int32, da.shape, 1)
    w_da = w_da.astype(BF16)
    l1 = jnp.dot(jnp.where(lane < DECAY_LORA, jnp.tanh(da), 0.0).astype(BF16), w_da, preferred_element_type=F32)
    l2 = jnp.dot(jnp.where(lane >= DECAY_LORA, da, 0.0).astype(BF16), w_da, preferred_element_type=F32)
    wlog = -_softplus(-(w0 + l1)) - 0.5
    decay = jnp.exp(-jnp.exp(wlog))
    a = jax.nn.sigmoid(a0 + l2)
    g = jnp.dot(jax.nn.sigmoid(gd).astype(BF16), gate_up.astype(BF16), preferred_element_type=F32)
    kk = k * k_k
    kkn = kk / jnp.maximum(jnp.sqrt(_segsum64(kk * kk)), 1e-12)
    k2 = k * (1.0 + (a - 1.0) * k_a)
    return decay, k2, -kkn, kkn * a, g


def _pre_shift(i, zr, zl, zr8, zl8, mu, mul):
    live = (i > 0).astype(F32)
    dz = _shift_down(zr, zr8[7:8, :] * live) - zr
    dzl = _shift_down(zl, zl8[7:8, :] * live) - zl
    return zr + dz * mu, zl + dzl * mul, dz, dzl


def _shift_down(x, first_row):
    rolled = pltpu.roll(x, 1, 0)
    row = lax.broadcasted_iota(jnp.int32, x.shape, 0)
    return jnp.where(row == 0, first_row, rolled)


def _shift_up(x, last_row):
    rolled = pltpu.roll(x, x.shape[0] - 1, 0)
    row = lax.broadcasted_iota(jnp.int32, x.shape, 0)
    return jnp.where(row == x.shape[0] - 1, last_row, rolled)


def _prev_rows_spec(tm, cols):
    return lambda idx: pl.BlockSpec((8, cols), lambda i: (jnp.maximum(idx(i) * (tm // 8) - 1, 0), 0))


def _rwkv_pre(p_rkv, p_lora, params, tm=256):
    T = p_rkv.shape[0]

    def fn(i, zr, zl, zr8, zl8, mu, mul, *ps):
        z, z2, _, _ = _pre_shift(i, zr, zl, zr8, zl8, mu, mul)
        decay, k2, an, bn, g = _pre_core(z[:, RW_W:2 * RW_W], z2[:, :128], z2[:, 128:], *ps)
        return z[:, :RW_W], decay, k2, z[:, 2 * RW_W:], an, bn, g

    extra = [(p_rkv, _prev_rows_spec(tm, 3 * RW_W)), (p_lora, _prev_rows_spec(tm, LORA_W))]
    return _rows(fn, "rwkv_pre", T, tm, [p_rkv, p_lora], list(params), [(RW_W, F32)] * 7, [], extra=extra)


def _rwkv_pre_bwd(p_rkv, p_lora, params, cts, tm=256):
    T = p_rkv.shape[0]
    n = T // tm

    def fn(i, zr, zl, cr, cdec, ck2, cv, can, cbn, cg, cr_b, ck2_b, cv_b, zr8, zl8, mu, mul, *rest):
        ps, (car, carl) = rest[:-2], rest[-2:]
        cr, ck2, cv = cr + cr_b, ck2 + ck2_b, cv + cv_b
        z, z2, dif, difl = _pre_shift(i, zr, zl, zr8, zl8, mu, mul)
        _, vjp = jax.vjp(_pre_core, z[:, RW_W:2 * RW_W], z2[:, :128], z2[:, 128:], *ps)
        dk, dda, dgd, *dps = vjp((cdec, ck2, can, cbn, cg))
        dz = jnp.concatenate([cr, dk, cv], axis=1)
        dz2 = jnp.concatenate([dda, dgd], axis=1)
        dzp, dzlp = dz * mu, dz2 * mul

        @pl.when(i == n - 1)
        def _():
            car[...] = jnp.zeros_like(car)
            carl[...] = jnp.zeros_like(carl)

        d_rkv = dz - dzp + _shift_up(dzp, car[0:1, :])
        d_lora = dz2 - dzlp + _shift_up(dzlp, carl[0:1, :])
        car[0:1, :] = dzp[0:1, :]
        carl[0:1, :] = dzlp[0:1, :]
        return (d_rkv, d_lora, jnp.sum(dz * dif, axis=0, keepdims=True), jnp.sum(dz2 * difl, axis=0, keepdims=True), *dps)

    extra = [(p_rkv, _prev_rows_spec(tm, 3 * RW_W)), (p_lora, _prev_rows_spec(tm, LORA_W))]
    acc = [(p.shape, F32) for p in params]
    return _rows(fn, "rwkv_pre_bwd", T, tm, [p_rkv, p_lora, *cts], list(params), [(3 * RW_W, BF16), (LORA_W, BF16)], acc,
                 extra=extra, reverse=True, scratch=[pltpu.VMEM((8, 3 * RW_W), F32), pltpu.VMEM((8, LORA_W), F32)])


def _post_core(y, r, k2, v, g, lw, lb, rk):
    mu = _segsum64(y) * (1.0 / HEAD)
    yc = y - mu
    var = _segsum64(yc * yc) * (1.0 / HEAD)
    yn = yc * lax.rsqrt(var + GN_EPS) * lw + lb
    return (yn + _segsum64(r * k2 * rk) * v) * g


def _rwkv_post(y, r, k2, v, g, lw, lb, rk, tm=256):
    (o,) = _rows(lambda i, *a: (_post_core(*a),), "rwkv_post", y.shape[0], tm, [y, r, k2, v, g], [lw, lb, rk], [(RW_W, BF16)], [])
    return o


def _rwkv_post_bwd(y, r, k2, v, g, lw, lb, rk, do, tm=256):
    def fn(i, y, r, k2, v, g, do, lw, lb, rk):
        _, vjp = jax.vjp(_post_core, y, r, k2, v, g, lw, lb, rk)
        return vjp(do.astype(F32))

    return _rows(fn, "rwkv_post_bwd", y.shape[0], tm, [y, r, k2, v, g, do], [lw, lb, rk], [(RW_W, F32)] * 5, [((1, RW_W), F32)] * 3)


SCAN_L = 64


def _to_perm(x):
    T = x.shape[0]
    return x.reshape(T, RW_HEADS, HEAD).transpose(0, 2, 1).reshape(T, 8, 128)


def _from_perm(x):
    T = x.shape[0]
    return x.reshape(T, HEAD, RW_HEADS).transpose(0, 2, 1).reshape(T, RW_W)


def _as_tile(p):
    lane = lax.broadcasted_iota(jnp.int32, (8, 128), 1)
    return jnp.take_along_axis(p, (lane % 8) * 16 + lane // 8, axis=1)


def _as_perm(t):
    lane = lax.broadcasted_iota(jnp.int32, (8, 128), 1)
    return jnp.take_along_axis(t, (lane % 16) * 8 + lane // 16, axis=1)


def _tiles_to_perm(refs, L):
    for r in refs:
        for t in range(L):
            r[t] = _as_perm(r[t])


def _expander(srcs, tiles=()):
    s = lax.broadcasted_iota(jnp.int32, (8, 128), 0)
    lane = lax.broadcasted_iota(jnp.int32, (8, 128), 1)
    idx = 16 * s + lane // 8

    def expand(t, e_ref):
        for m, r in enumerate(srcs):
            for g in range(8):
                row = jnp.broadcast_to(r[t, pl.ds(g, 1), :], (8, 128))
                e_ref[m, g * 8:(g + 1) * 8, :] = jnp.take_along_axis(row, idx, axis=1)
        for k, r in enumerate(tiles):
            e_ref[len(srcs) + k, 0:8, :] = _as_tile(r[t])

    return expand


def _ck_a_to_b(ck):
    n = ck.shape[0]
    return ck.reshape(n, 8, 8, 8, RW_HEADS, 8).transpose(0, 2, 5, 1, 4, 3).reshape(n, HEAD, 8, 128)


def _bc(row):
    return jnp.broadcast_to(row, (8, 128))


def _rsum(x):
    return jnp.sum(x, axis=0, keepdims=True)


def _plus(acc, k, term):
    acc[k] = term if acc[k] is None else acc[k] + term


def _scan_fwd(xes, vi):
    T, L = vi.shape[0], SCAN_L
    nch = T // L

    def body(*refs):
        xr, (vi_ref, yi_ref, sa_ref, ck_ref, st_ref, e0, e1) = refs[:5], refs[5:]

        @pl.when(pl.program_id(0) == 0)
        def _():
            st_ref[...] = jnp.zeros_like(st_ref)

        ck_ref[0] = st_ref[...]
        expand = _expander(xr, [vi_ref])
        expand(0, e0)

        def step(t, e):
            tile = lambda m, jh: e[m, 8 * jh:8 * jh + 8, :]
            vb = [_bc(e[5, ih:ih + 1, :]) for ih in range(8)]
            acc = [None] * 8
            for jh in range(8):
                a = tile(0, jh)
                for ih in range(8):
                    _plus(acc, ih, st_ref[8 * jh + ih] * a)
            sab = []
            for ih in range(8):
                row = _rsum(acc[ih])
                sa_ref[t, ih:ih + 1, :] = row
                sab.append(_bc(row))
            yacc = [None] * 8
            for jh in range(8):
                w, B, k, r = tile(1, jh), tile(2, jh), tile(3, jh), tile(4, jh)
                for ih in range(8):
                    s = st_ref[8 * jh + ih] * w + B * sab[ih] + k * vb[ih]
                    st_ref[8 * jh + ih] = s
                    _plus(yacc, ih, s * r)
            for ih in range(8):
                yi_ref[t, ih:ih + 1, :] = _rsum(yacc[ih])

        def pair(p, carry):
            t = 2 * p
            expand(t + 1, e1)
            step(t, e0)
            expand(jnp.minimum(t + 2, L - 1), e0)
            step(t + 1, e1)
            return carry

        lax.fori_loop(0, L // 2, pair, 0)
        _tiles_to_perm([yi_ref, sa_ref], L)

    tile = pl.BlockSpec((L, 8, 128), lambda c: (c, 0, 0))
    return pl.pallas_call(
        body, name="rwkv_scan_fwd", grid=(nch,), in_specs=[tile] * 6,
        out_specs=[tile, tile, pl.BlockSpec((1, HEAD, 8, 128), lambda c: (c, 0, 0, 0))],
        out_shape=[jax.ShapeDtypeStruct((T, 8, 128), F32)] * 2 + [jax.ShapeDtypeStruct((nch, HEAD, 8, 128), F32)],
        scratch_shapes=[pltpu.VMEM((HEAD, 8, 128), F32)] + [pltpu.VMEM((6, HEAD, 128), F32)] * 2, compiler_params=_cp(("arbitrary",)),
    )(*xes, vi)


def _scan_bwd_a(xes, dyi):
    T, L = dyi.shape[0], SCAN_L
    nch = T // L

    def body(*refs):
        xr, (dy_ref, dsa_ref, dv_ref, g_ref, e0, e1) = refs[:5], refs[5:]

        @pl.when(pl.program_id(0) == 0)
        def _():
            g_ref[...] = jnp.zeros_like(g_ref)

        expand = _expander(xr, [dy_ref])
        expand(L - 1, e0)

        def step(t, e):
            tile = lambda m, jh: e[m, 8 * jh:8 * jh + 8, :]
            dyb = [_bc(e[5, ih:ih + 1, :]) for ih in range(8)]
            dsa, dv = [None] * 8, [None] * 8
            for jh in range(8):
                B, k, r = tile(2, jh), tile(3, jh), tile(4, jh)
                for ih in range(8):
                    g = g_ref[8 * jh + ih] + r * dyb[ih]
                    g_ref[8 * jh + ih] = g
                    _plus(dsa, ih, g * B)
                    _plus(dv, ih, g * k)
            dsab = []
            for ih in range(8):
                row = _rsum(dsa[ih])
                dsa_ref[t, ih:ih + 1, :] = row
                dsab.append(_bc(row))
                dv_ref[t, ih:ih + 1, :] = _rsum(dv[ih])
            for jh in range(8):
                A, w = tile(0, jh), tile(1, jh)
                for ih in range(8):
                    g_ref[8 * jh + ih] = g_ref[8 * jh + ih] * w + A * dsab[ih]

        def pair(p, carry):
            t = L - 1 - 2 * p
            expand(t - 1, e1)
            step(t, e0)
            expand(jnp.maximum(t - 2, 0), e0)
            step(t - 1, e1)
            return carry

        lax.fori_loop(0, L // 2, pair, 0)
        _tiles_to_perm([dsa_ref, dv_ref], L)

    tile = pl.BlockSpec((L, 8, 128), lambda c: (nch - 1 - c, 0, 0))
    return pl.pallas_call(
        body, name="rwkv_scan_bwd_a", grid=(nch,), in_specs=[tile] * 6, out_specs=[tile, tile],
        out_shape=[jax.ShapeDtypeStruct((T, 8, 128), F32)] * 2,
        scratch_shapes=[pltpu.VMEM((HEAD, 8, 128), F32)] + [pltpu.VMEM((6, HEAD, 128), F32)] * 2, compiler_params=_cp(("arbitrary",)),
    )(*xes, dyi)


def _scan_bwd_b(xts, ies, ckb):
    T, L = xts[0].shape[0], SCAN_L
    nch = T // L

    def body(*refs):
        xr, er, ck_ref, dj, (hist, g_ref, e0, e1) = refs[:5], refs[5:9], refs[9], refs[10:15], refs[15:]

        @pl.when(pl.program_id(0) == 0)
        def _():
            g_ref[...] = jnp.zeros_like(g_ref)

        hist[0] = ck_ref[0]
        expand_vs = _expander(er[:2], [xr[1], xr[2], xr[3]])
        expand = _expander(er, [xr[0], xr[1], xr[4]])
        expand_vs(0, e0)

        def fstep(t, e_ref):
            w, B, k = e_ref[2, 0:8, :], e_ref[3, 0:8, :], e_ref[4, 0:8, :]
            row = lambda m, i: jnp.broadcast_to(e_ref[m, pl.ds(i, 1), :], (8, 128))
            for i in range(HEAD):
                hist[t + 1, i] = hist[t, i] * w + row(1, i) * B + row(0, i) * k

        def fpair(p, carry):
            t = 2 * p
            expand_vs(t + 1, e1)
            fstep(t, e0)
            expand_vs(jnp.minimum(t + 2, L - 1), e0)
            fstep(t + 1, e1)
            return carry

        lax.fori_loop(0, L // 2, fpair, 0)
        expand(L - 1, e0)

        def bstep(t, e_ref):
            A, w, r = e_ref[4, 0:8, :], e_ref[5, 0:8, :], e_ref[6, 0:8, :]
            row = lambda m, i: jnp.broadcast_to(e_ref[m, pl.ds(i, 1), :], (8, 128))
            acc = [_Acc() for _ in range(5)]
            for i in range(HEAD):
                dy_i, dsa_i = row(2, i), row(3, i)
                g = g_ref[i] + dy_i * r
                sp = hist[t, i]
                acc[4].add(i, hist[t + 1, i] * dy_i)
                acc[1].add(i, g * sp)
                acc[2].add(i, g * row(1, i))
                acc[3].add(i, g * row(0, i))
                acc[0].add(i, sp * dsa_i)
                g_ref[i] = g * w + dsa_i * A
            for m in range(5):
                dj[m][t] = acc[m].total()

        def bpair(p, carry):
            t = L - 1 - 2 * p
            expand(t - 1, e1)
            bstep(t, e0)
            expand(jnp.maximum(t - 2, 0), e0)
            bstep(t - 1, e1)
            return carry

        lax.fori_loop(0, L // 2, bpair, 0)
        _tiles_to_perm(dj, L)

    tile = pl.BlockSpec((L, 8, 128), lambda c: (nch - 1 - c, 0, 0))
    return pl.pallas_call(
        body, name="rwkv_scan_bwd_b", grid=(nch,),
        in_specs=[tile] * 9 + [pl.BlockSpec((1, HEAD, 8, 128), lambda c: (nch - 1 - c, 0, 0, 0))],
        out_specs=[tile] * 5, out_shape=[jax.ShapeDtypeStruct((T, 8, 128), F32)] * 5,
        scratch_shapes=[pltpu.VMEM((L + 1, HEAD, 8, 128), F32), pltpu.VMEM((HEAD, 8, 128), F32)] + [pltpu.VMEM((7, HEAD, 128), F32)] * 2,
        compiler_params=_cp(("arbitrary",)),
    )(*xts, *ies, ckb)


SWA_COLS = SWA_W + 2 * KV_W
BLK = 128


def _swa_core(n, k2a, k2b, vla, vra, vlb, vrb, sinks, *qps):
    iq = lax.broadcasted_iota(jnp.int32, (BLK, 2 * BLK), 0)
    ik = lax.broadcasted_iota(jnp.int32, (BLK, 2 * BLK), 1)
    diff = BLK + iq - ik
    valid = (diff >= 0) & (diff < WINDOW) & ((n > 0) | (ik >= BLK))
    lane = lax.broadcasted_iota(jnp.int32, (BLK, 128), 1)
    lane1 = lax.broadcasted_iota(jnp.int32, (1, 128), 1)
    nt = (((1,), (1,)), ((), ()))
    outs = []
    for pp in range(8):
        k2, vl, vr = (k2a, vla, vra) if pp < 4 else (k2b, vlb, vrb)
        qp = qps[pp]
        o = None
        for half, vv in ((0, vl), (1, vr)):
            qh = jnp.where((lane >= HEAD) == (half == 1), qp, 0.0).astype(BF16)
            s = lax.dot_general(qh, k2.astype(BF16), nt, preferred_element_type=F32) * (HEAD ** -0.5)
            s = jnp.where(valid, s, NEG_INF)
            sink = jnp.sum(jnp.where(lane1 == 2 * pp + half, sinks, 0.0), axis=1, keepdims=True)
            m = jnp.maximum(jnp.max(s, axis=1, keepdims=True), sink)
            p = jnp.exp(s - m)
            den = jnp.sum(p, axis=1, keepdims=True) + jnp.exp(sink - m)
            oh = jnp.dot((p / den).astype(BF16), vv.astype(BF16), preferred_element_type=F32)
            o = oh if o is None else o + oh
        outs.append(o)
    return jnp.concatenate(outs, axis=1)


def _swa_prep(pc, pp, b, cq, sq, ckc, skc, ckp, skp):
    zc, zp = pc + b, pp + b
    qr = zc[:, :SWA_W] * cq + _swap32(zc[:, :SWA_W]) * sq
    kc, kp = zc[:, SWA_W:SWA_W + KV_W], zp[:, SWA_W:SWA_W + KV_W]
    kb = jnp.concatenate([kp * ckp + _swap32(kp) * skp, kc * ckc + _swap32(kc) * skc], axis=0)
    vb = jnp.concatenate([zp[:, SWA_W + KV_W:], zc[:, SWA_W + KV_W:]], axis=0)
    lane = lax.broadcasted_iota(jnp.int32, kb.shape, 1)
    left = lane < HEAD
    kbr, vbr = pltpu.roll(kb, HEAD, 1), pltpu.roll(vb, HEAD, 1)
    return (jnp.where(left, kb, kbr), jnp.where(left, kbr, kb), jnp.where(left, vb, 0.0), jnp.where(left, 0.0, vbr),
            jnp.where(left, vbr, 0.0), jnp.where(left, 0.0, vb)), [qr[:, q * 128:(q + 1) * 128] for q in range(8)]


def _swa_specs(T, tabs_q, tabs_k):
    cur = lambda c: pl.BlockSpec((BLK, c), lambda n: (n, 0))
    prev = lambda c: pl.BlockSpec((BLK, c), lambda n: (jnp.maximum(n - 1, 0), 0))
    return cur, prev


def _swa_fwd(p_swa, b, sinks, cq, sq, ck, sk):
    T = p_swa.shape[0]
    cur, prev = _swa_specs(T, None, None)

    def body(pc, pp, b_ref, s_ref, cq_r, sq_r, ckc, skc, ckp, skp, o_ref):
        ops, qps = _swa_prep(pc[...], pp[...], b_ref[...], cq_r[...], sq_r[...], ckc[...], skc[...], ckp[...], skp[...])
        o_ref[...] = _swa_core(pl.program_id(0), *ops, s_ref[...], *qps).astype(o_ref.dtype)

    full = lambda a: pl.BlockSpec(a.shape, lambda n: (0, 0))
    return pl.pallas_call(
        body, name="swa_fwd", grid=(T // BLK,),
        in_specs=[cur(SWA_COLS), prev(SWA_COLS), full(b), full(sinks), cur(SWA_W), cur(SWA_W), cur(KV_W), cur(KV_W), prev(KV_W), prev(KV_W)],
        out_specs=cur(SWA_W), out_shape=jax.ShapeDtypeStruct((T, SWA_W), BF16), compiler_params=_cp(("arbitrary",)),
    )(p_swa, p_swa, b, sinks, cq, sq, ck, sk, ck, sk)


def _swa_bwd(p_swa, b, sinks, cq, sq, ck, sk, do):
    T = p_swa.shape[0]
    nb = T // BLK
    cur = lambda c: pl.BlockSpec((BLK, c), lambda s: (nb - 1 - s, 0))
    prev = lambda c: pl.BlockSpec((BLK, c), lambda s: (jnp.maximum(nb - 2 - s, 0), 0))

    def body(pc, pp, b_ref, s_ref, cq_r, sq_r, ckc, skc, ckp, skp, do_ref, dcur, db, dsk, carry):
        step = pl.program_id(0)
        n = nb - 1 - step

        @pl.when(step == 0)
        def _():
            carry[...] = jnp.zeros_like(carry)
            db[...] = jnp.zeros_like(db)
            dsk[...] = jnp.zeros_like(dsk)

        ops, qps = _swa_prep(pc[...], pp[...], b_ref[...], cq_r[...], sq_r[...], ckc[...], skc[...], ckp[...], skp[...])
        _, vjp = jax.vjp(functools.partial(_swa_core, n), *ops, s_ref[...], *qps)
        dk2a, dk2b, dvla, dvra, dvlb, dvrb, dsinks, *dqps = vjp(do_ref[...].astype(F32))
        dqr = jnp.concatenate(dqps, axis=1)
        lane = lax.broadcasted_iota(jnp.int32, dk2a.shape, 1)
        left = lane < HEAD
        dkb = jnp.where(left, dk2a + pltpu.roll(dk2a, HEAD, 1), dk2b + pltpu.roll(dk2b, HEAD, 1))
        dvb = jnp.where(left, dvla + pltpu.roll(dvra, HEAD, 1), pltpu.roll(dvlb, HEAD, 1) + dvrb)
        dq = dqr * cq_r[...] + _swap32(dqr * sq_r[...])
        dkp, dkc = dkb[:BLK], dkb[BLK:]
        dkp = dkp * ckp[...] + _swap32(dkp * skp[...])
        dkc = dkc * ckc[...] + _swap32(dkc * skc[...])
        dc = jnp.concatenate([dq, jnp.concatenate([dkc, dvb[BLK:]], axis=1) + carry[...]], axis=1)
        carry[...] = jnp.concatenate([dkp, dvb[:BLK]], axis=1)
        dcur[...] = dc.astype(dcur.dtype)
        db[...] += jnp.sum(dc, axis=0, keepdims=True)
        dsk[...] += dsinks

    full = lambda a: pl.BlockSpec(a.shape, lambda s: (0, 0))
    return pl.pallas_call(
        body, name="swa_bwd", grid=(nb,),
        in_specs=[cur(SWA_COLS), prev(SWA_COLS), full(b), full(sinks), cur(SWA_W), cur(SWA_W), cur(KV_W), cur(KV_W), prev(KV_W), prev(KV_W),
                  cur(SWA_W)],
        out_specs=[cur(SWA_COLS), full(b), full(sinks)],
        out_shape=[jax.ShapeDtypeStruct((T, SWA_COLS), BF16), jax.ShapeDtypeStruct(b.shape, F32), jax.ShapeDtypeStruct(sinks.shape, F32)],
        scratch_shapes=[pltpu.VMEM((BLK, 2 * KV_W), F32)], compiler_params=_cp(("arbitrary",)),
    )(p_swa, p_swa, b, sinks, cq, sq, ck, sk, ck, sk, do)


def _rope_tables(T):
    inv = 10000.0 ** (-jnp.arange(0, HEAD, 2, dtype=F32) / HEAD)
    ang = jnp.arange(T, dtype=F32)[:, None] * inv[None, :]
    c = jnp.concatenate([jnp.cos(ang), jnp.cos(ang)], axis=1)
    s = jnp.concatenate([-jnp.sin(ang), jnp.sin(ang)], axis=1)
    return jnp.tile(c, (1, 16)), jnp.tile(s, (1, 16)), jnp.tile(c, (1, 2)), jnp.tile(s, (1, 2))


def _xattn_core(*qkv):
    outs = []
    for h in range(XH):
        qh, kh, vh = qkv[h], qkv[XH + h], qkv[2 * XH + h]
        s = lax.dot_general(qh.astype(BF16), kh.astype(BF16), (((1,), (1,)), ((), ())), preferred_element_type=F32) * (XHD ** -0.5)
        p = jnp.exp(s - jnp.max(s, axis=1, keepdims=True))
        p = p / jnp.sum(p, axis=1, keepdims=True)
        outs.append(jnp.dot(p.astype(BF16), vh.astype(BF16), preferred_element_type=F32))
    return jnp.concatenate(outs, axis=1)


def _xattn_split(q, kv):
    return [q[:, h * XHD:(h + 1) * XHD] for h in range(XH)] + [kv[:, h * XHD:(h + 1) * XHD] for h in range(2 * XH)]


def _xattn_fwd(q, kv, tm=256):
    (o,) = _rows(lambda i, q, kv: (_xattn_core(*_xattn_split(q, kv)),), "xattn_fwd", q.shape[0], tm, [q], [kv], [(q.shape[1], BF16)], [])
    return o


def _xattn_bwd(q, kv, do, tm=256):
    def fn(i, q, do, kv):
        _, vjp = jax.vjp(_xattn_core, *_xattn_split(q, kv))
        d = vjp(do.astype(F32))
        return jnp.concatenate(d[:XH], axis=1), jnp.concatenate(d[XH:], axis=1)

    return _rows(fn, "xattn_bwd", q.shape[0], tm, [q, do], [kv], [(q.shape[1], BF16)], [(kv.shape, F32)])


def _loss_head(x, g, tgt, tm=256):
    D = x.shape[1]

    def fn(i, x, tgt, g):
        y, vjp = jax.vjp(_rms, x, g)
        err = y - tgt
        dx, dg = vjp(err * (1.0 / D))
        part = 0.5 / D * jnp.sum(jnp.sum(err * err, axis=1, keepdims=True), axis=0, keepdims=True)
        return dx, jnp.broadcast_to(part, (1, 128)), dg

    return _rows(fn, "loss_head", x.shape[0], tm, [x, tgt], [g], [(D, F32)], [((1, 128), F32), ((1, D), F32)])


def _local_step(x, mem, tgt, get_w, P, put_g):
    T = x.shape[0]
    h1 = _rms_fwd(x, P["f1_norm"], "f1_norm")
    mn = _rms_fwd(mem, P["mem_norm"], "mem_norm")
    W = dict(get_w("f1", None))

    def f1_down(after):
        W.update(get_w("f1d", after))
        return W["f1_down"]

    x1, s1 = _ffn_fwd(x, h1, W["f1_gate"], W["f1_up"], f1_down, "f1")

    h2 = _rms_fwd(x1, P["mix_norm"], "mix_norm")
    W.update(get_w("mix", h2))
    w_rkv, w_lora, w_swa = W["w_inT"][:3 * RW_W], W["w_inT"][3 * RW_W:SHIFT_COLS], W["w_inT"][SHIFT_COLS:]
    p_rkv = _mm(h2, w_rkv, "nt", "in_rkv", after=W.get("_after"))
    p_lora = _mm(h2, w_lora, "nt", "in_lora")
    p_swa = _mm(h2, w_swa, "nt", "in_swa")
    w_da = jnp.concatenate([W["rw_decay_up"], W["rw_aaa_up"]], axis=0)
    pre_params = (P["rw_mu"][:, :3 * RW_W], P["rw_mu"][:, 3 * RW_W:], P["rw_w0"], P["rw_a0"], P["rw_k_k"], P["rw_k_a"], w_da,
                  W["rw_gate_up"])
    r, decay, k2, v, an, bn, g = _rwkv_pre(p_rkv, p_lora, pre_params)
    scan_vecs = (an, decay, bn, k2, r)
    xes = [_to_perm(a) for a in scan_vecs]
    v_p = _to_perm(v)
    yi, sai, ck = _scan_fwd(xes, v_p)
    y_scan = _from_perm(yi)
    y_rw = _rwkv_post(y_scan, r, k2, v, g, P["rw_lnx_w"], P["rw_lnx_b"], P["rw_r_k"])
    cq, sq, ckt, skt = _rope_tables(T)
    y_swa = _swa_fwd(p_swa, P["b_in_attn"], P["attn_sinks"], cq, sq, ckt, skt)
    ycat = jnp.concatenate([y_rw, y_swa], axis=1)
    W.update(get_w("out", ycat))
    x2 = _mm(ycat, W["w_out"], "nn", "out_proj", res=x1, bias=P["b_out"])

    hx = _rms_fwd(x2, P["xa_norm"], "xa_norm")
    W.update(get_w("xattn", hx))
    q = _mm(hx, W["w_xq"], "nn", "xq", out_dtype=BF16)
    kv = _mm(mn, W["w_xkv"], "nn", "xkv", out_dtype=BF16)
    o = _xattn_fwd(q, kv)
    x3 = _mm(o, W["w_xo"], "nn", "xo", res=x2)

    h3 = _rms_fwd(x3, P["f2_norm"], "f2_norm")
    W.update(get_w("f2", h3))
    x4, s2 = _ffn_fwd(x3, h3, W["f2_gate"], W["f2_up"], W["f2_down"], "f2")
    dx4, loss_part, d_final = _loss_head(x4, P["final_norm"], tgt)

    gs = {"final_norm": d_final}
    dx3, gs["f2_norm"] = _ffn_bwd(x3, P["f2_norm"], W["f2_gate"], W["f2_up"], W["f2_down"], s2, dx4, "f2", put_g)

    do = _mm(dx3, W["w_xo"], "nt", "xo_do", out_dtype=BF16)
    dw_xo = _mm(o, dx3, "tn", "xo_dw", out_dtype=BF16)
    dq, dkv = _xattn_bwd(q, kv, do)
    dw_xq = _mm(hx, dq, "tn", "xq_dw", out_dtype=BF16)
    dw_xkv = _mm(mn, dkv, "tn", "xkv_dw", out_dtype=BF16)
    sent = put_g("xattn", {"w_xq": dw_xq, "w_xkv": dw_xkv, "w_xo": dw_xo})
    dhx = _mm(dq, W["w_xq"], "nt", "xq_dh", after=sent)
    dmn = _mm(dkv, W["w_xkv"], "nt", "xkv_dmn")
    _, gs["mem_norm"], _ = _rms_bwd(mem, P["mem_norm"], dmn, jnp.zeros_like(mem), "mem_norm_bwd")
    dx2, gs["xa_norm"], gs["b_out"] = _rms_bwd(x2, P["xa_norm"], dhx, dx3, "xa_norm_bwd")

    dycat = _mm(dx2, W["w_out"], "nt", "out_dy")
    dw_out = _mm(ycat, dx2, "tn", "out_dw", out_dtype=BF16)
    dp_swa, gs["b_in_attn"], gs["attn_sinks"] = _swa_bwd(p_swa, P["b_in_attn"], P["attn_sinks"], cq, sq, ckt, skt, dycat[:, RW_W:])
    dy_scan, dr_b, dk2_b, dv_b, dg, gs["rw_lnx_w"], gs["rw_lnx_b"], gs["rw_r_k"] = _rwkv_post_bwd(
        y_scan, r, k2, v, g, P["rw_lnx_w"], P["rw_lnx_b"], P["rw_r_k"], dycat[:, :RW_W])
    dy_p = _to_perm(dy_scan)
    dsai, dvi = _scan_bwd_a(xes, dy_p)
    dj = _scan_bwd_b(xes, [v_p, sai, dy_p, dsai], _ck_a_to_b(ck))
    dan, ddecay, dbn, dk2_s, dr_s = (_from_perm(d) for d in dj)
    cts = (dr_s, ddecay, dk2_s, _from_perm(dvi), dan, dbn, dg, dr_b, dk2_b, dv_b)
    dp_rkv, dp_lora, dmu, dmul, gs["rw_w0"], gs["rw_a0"], gs["rw_k_k"], gs["rw_k_a"], dw_da, gs["rw_gate_up"] = _rwkv_pre_bwd(
        p_rkv, p_lora, pre_params, cts)
    gs["rw_mu"] = jnp.concatenate([dmu, dmul], axis=1)
    gs["rw_decay_up"], gs["rw_aaa_up"] = dw_da[:DECAY_LORA], dw_da[DECAY_LORA:]
    dw_inT = jnp.concatenate([_mm(dp_rkv, h2, "tn", "in_dw_rkv"), _mm(dp_lora, h2, "tn", "in_dw_lora"),
                              _mm(dp_swa, h2, "tn", "in_dw_swa")], axis=0)
    sent = put_g("mix", {"w_in": dw_inT, "w_out": dw_out})
    dh2 = _mm(dp_rkv, w_rkv, "nn", "in_dh_rkv", after=sent)
    dh2 = _mm(dp_lora, w_lora, "nn", "in_dh_lora", res=dh2)
    dh2 = _mm(dp_swa, w_swa, "nn", "in_dh_swa", res=dh2)
    dx1, gs["mix_norm"], _ = _rms_bwd(x1, P["mix_norm"], dh2, dx2, "mix_norm_bwd")

    dx0, gs["f1_norm"] = _ffn_bwd(x, P["f1_norm"], W["f1_gate"], W["f1_up"], W["f1_down"], s1, dx1, "f1", put_g)
    return loss_part, dx0, gs


_ANY = pl.BlockSpec(memory_space=pl.ANY)
_OTHER_CHIPS = ((1, 0), (0, 1), (1, 1))


def _mesh_pos():
    return lax.axis_index("x"), lax.axis_index("y"), lax.axis_index("c")


def _slot(ref, kind, s, rows, cols):
    if kind == "row":
        return ref.at[pl.ds(pl.multiple_of(s * rows, 8), rows), :]
    return ref.at[:, pl.ds(pl.multiple_of(s * cols, 128), cols)]


_HBM = pl.BlockSpec(memory_space=pltpu.HBM)
_SEMS = pl.BlockSpec(memory_space=pltpu.SEMAPHORE)
_SPLIT = dict(compiler_params=pltpu.CompilerParams(has_side_effects=pltpu.SideEffectType.DATAFLOW_SIDE_EFFECTING))


def _in_hbm(a):
    return pltpu.with_memory_space_constraint(a, pltpu.HBM)


def _full_shape(s, kind):
    return (4 * s.shape[0], s.shape[1]) if kind == "row" else (s.shape[0], 4 * s.shape[1])


def _half(ref, shape, h):
    rows, cols = shape
    if rows % 32 == 0:
        return ref.at[pl.ds(pl.multiple_of(h * (rows // 2), 16), rows // 2), :]
    assert cols % 256 == 0, shape
    return ref.at[:, pl.ds(pl.multiple_of(h * (cols // 2), 128), cols // 2)]


def _half_shape(shape):
    rows, cols = shape
    return (rows // 2, cols) if rows % 32 == 0 else (rows, cols // 2)


def _streams(src, dst, shape, c):
    hs = _half_shape(shape)
    s, d = _half(src, shape, c), _half(dst, shape, c)
    return [(_half(s, hs, q), _half(d, hs, q)) for q in range(2)]


def _swap_halves(name, fulls, shard_shapes, kinds):
    n = len(fulls)

    def body(*refs):
        out, send, recv = refs[n:2 * n], refs[2 * n], refs[2 * n + 1]
        x, y, c = _mesh_pos()
        sent = []
        for i in range(n):
            for r, (dx, dy) in enumerate(_OTHER_CHIPS):
                theirs = _slot(out[i], kinds[i], 2 * ((x + dx) % 2) + (y + dy) % 2, *shard_shapes[i])
                have = _half(theirs, shard_shapes[i], c)
                rc = pltpu.make_async_remote_copy(have, have, send.at[3 * i + r], recv.at[3 * i + r], device_id=(x, y, 1 - c),
                                                  device_id_type=MESH)
                rc.start()
                sent.append(rc)
        for i in range(n):
            for r, (dx, dy) in enumerate(_OTHER_CHIPS):
                theirs = _slot(out[i], kinds[i], 2 * ((x + dx) % 2) + (y + dy) % 2, *shard_shapes[i])
                need = _half(theirs, shard_shapes[i], 1 - c)
                pltpu.make_async_remote_copy(need, need, send.at[3 * i + r], recv.at[3 * i + r], device_id=(x, y, c),
                                             device_id_type=MESH).wait_recv()
        for rc in sent:
            rc.wait_send()

    return pl.pallas_call(
        body, name=name, in_specs=[_ANY] * n, out_specs=[_ANY] * n, out_shape=[jax.ShapeDtypeStruct(f.shape, f.dtype) for f in fulls],
        input_output_aliases={i: i for i in range(n)},
        scratch_shapes=[pltpu.SemaphoreType.DMA((3 * n,)), pltpu.SemaphoreType.DMA((3 * n,))],
    )(*fulls)


def _gather_start(name, shards, kinds, groups, after=None):
    n, ng = len(shards), len(groups)
    lands = [_in_hbm(lax.empty(_full_shape(s, k), s.dtype)) for s, k in zip(shards, kinds)]
    n_in = 2 * n + (after is not None)

    def body(*refs):
        src, land, sems, token = refs[:n], refs[n:2 * n], refs[n_in:n_in + 3 * ng], refs[-1]
        x, y, c = _mesh_pos()
        me = 2 * x + y
        for gi, idxs in enumerate(groups):
            send, recv, own = sems[3 * gi:3 * gi + 3]
            for k, i in enumerate(idxs):
                mine = _slot(land[i], kinds[i], me, *src[i].shape)
                for r, (dx, dy) in enumerate(_OTHER_CHIPS):
                    for q, (s, d) in enumerate(_streams(src[i], mine, src[i].shape, c)):
                        pltpu.make_async_remote_copy(s, d, send.at[6 * k + 2 * r + q], recv.at[6 * k + 2 * r + q],
                                                     device_id=((x + dx) % 2, (y + dy) % 2, c), device_id_type=MESH).start()
                pltpu.make_async_copy(src[i], mine, own.at[k]).start()
        token[...] = jnp.zeros_like(token)

    sem_shapes = [pltpu.SemaphoreType.DMA((w * len(g),)) for g in groups for w in (6, 6, 1)]
    thru = [pltpu.HBM(a.shape, a.dtype) for a in (*shards, *lands)]
    res = pl.pallas_call(
        body, name=name, in_specs=[_HBM] * (2 * n) + [_ANY] * (after is not None),
        out_specs=[_SEMS] * (3 * ng) + [_HBM] * (2 * n) + [pl.BlockSpec(memory_space=pltpu.VMEM)],
        out_shape=sem_shapes + thru + [jax.ShapeDtypeStruct((8, 128), F32)],
        input_output_aliases={i: 3 * ng + i for i in range(2 * n)}, **_SPLIT,
    )(*[_in_hbm(s) for s in shards], *lands, *([] if after is None else [after]))
    return res[:3 * ng], res[3 * ng:3 * ng + n], res[3 * ng + n:3 * ng + 2 * n], res[-1]


def _gather_wait(name, sems, shards, lands, kinds, after):
    m = len(shards)

    def body(*refs):
        src, land, (send, recv, own) = refs[:m], refs[m:2 * m], refs[2 * m:2 * m + 3]
        x, y, c = _mesh_pos()
        me = 2 * x + y
        for k in range(m):
            mine = _slot(land[k], kinds[k], me, *src[k].shape)
            for r in range(3):
                for q, (s, d) in enumerate(_streams(src[k], mine, src[k].shape, c)):
                    cp = pltpu.make_async_remote_copy(s, d, send.at[6 * k + 2 * r + q], recv.at[6 * k + 2 * r + q], device_id=(x, y, c),
                                                      device_id_type=MESH)
                    cp.wait_send()
                    cp.wait_recv()
            pltpu.make_async_copy(src[k], mine, own.at[k]).wait()

    thru = [pltpu.HBM(a.shape, a.dtype) for a in (*shards, *lands)]
    res = pl.pallas_call(
        body, name=name, in_specs=[_HBM] * (2 * m) + [_SEMS] * 3 + [pl.BlockSpec(memory_space=pl.ANY)],
        out_specs=[_HBM] * (2 * m), out_shape=thru, input_output_aliases={i: i for i in range(2 * m)}, **_SPLIT,
    )(*shards, *lands, *sems, after)
    return res[m:]


def _scatter_start(name, grads, kinds):
    m = len(grads)
    shard_shape = [(g.shape[0] // 4, g.shape[1]) if k == "row" else (g.shape[0], g.shape[1] // 4) for g, k in zip(grads, kinds)]
    lands = [_in_hbm(lax.empty((4, *s), g.dtype)) for s, g in zip(shard_shape, grads)]

    def body(*refs):
        src, land, (send, recv, own) = refs[:m], refs[m:2 * m], refs[2 * m:2 * m + 3]
        x, y, c = _mesh_pos()
        me = 2 * x + y
        for k in range(m):
            for r, (dx, dy) in enumerate(_OTHER_CHIPS):
                tx, ty = (x + dx) % 2, (y + dy) % 2
                pltpu.make_async_remote_copy(_slot(src[k], kinds[k], 2 * tx + ty, *shard_shape[k]), land[k].at[me],
                                             send.at[3 * k + r], recv.at[3 * k + r], device_id=(tx, ty, c), device_id_type=MESH).start()
            pltpu.make_async_copy(_slot(src[k], kinds[k], me, *shard_shape[k]), land[k].at[me], own.at[k]).start()
        refs[-1][...] = jnp.zeros_like(refs[-1])

    thru = [pltpu.HBM(a.shape, a.dtype) for a in (*grads, *lands)]
    res = pl.pallas_call(
        body, name=name, in_specs=[_HBM] * (2 * m),
        out_specs=[_SEMS] * 3 + [_HBM] * (2 * m) + [pl.BlockSpec(memory_space=pltpu.VMEM)],
        out_shape=[pltpu.SemaphoreType.DMA((3 * m,))] * 2 + [pltpu.SemaphoreType.DMA((m,))] + thru + [jax.ShapeDtypeStruct((8, 128), F32)],
        input_output_aliases={i: 3 + i for i in range(2 * m)}, **_SPLIT,
    )(*[_in_hbm(g) for g in grads], *lands)
    return res[:3], res[3:3 + m], res[3 + m:3 + 2 * m], res[-1]


def _scatter_wait(name, sems, grads, lands, kinds, after):
    m = len(grads)

    def body(*refs):
        src, land, (send, recv, own) = refs[:m], refs[m:2 * m], refs[2 * m:2 * m + 3]
        x, y, c = _mesh_pos()
        me = 2 * x + y
        for k in range(m):
            mine = _slot(src[k], kinds[k], me, *land[k].shape[1:])
            for r in range(3):
                cp = pltpu.make_async_remote_copy(mine, land[k].at[me], send.at[3 * k + r], recv.at[3 * k + r],
                                                  device_id=(x, y, c), device_id_type=MESH)
                cp.wait_send()
                cp.wait_recv()
            pltpu.make_async_copy(mine, land[k].at[me], own.at[k]).wait()

    thru = [pltpu.HBM(a.shape, a.dtype) for a in (*grads, *lands)]
    res = pl.pallas_call(
        body, name=name, in_specs=[_HBM] * (2 * m) + [_SEMS] * 3 + [pl.BlockSpec(memory_space=pl.ANY)],
        out_specs=[_HBM] * (2 * m), out_shape=thru, input_output_aliases={i: i for i in range(2 * m)}, **_SPLIT,
    )(*grads, *lands, *sems, after)
    return res[m:]


def _swap_with_sibling(arrs, name):
    n = len(arrs)

    def body(*refs):
        ins, outs = refs[:n], refs[n:2 * n]
        send, recv = refs[2 * n:]
        x, y, c = _mesh_pos()
        copies = []
        for i in range(n):
            rc = pltpu.make_async_remote_copy(ins[i], outs[i], send.at[i], recv.at[i], device_id=(x, y, 1 - c), device_id_type=MESH)
            rc.start()
            copies.append(rc)
        for rc in copies:
            rc.wait()

    return pl.pallas_call(
        body, name=name, in_specs=[_ANY] * n, out_specs=[_ANY] * n,
        out_shape=[jax.ShapeDtypeStruct(a.shape, a.dtype) for a in arrs],
        scratch_shapes=[pltpu.SemaphoreType.DMA((n,)), pltpu.SemaphoreType.DMA((n,))],
    )(*arrs)


def _small_start(pack, after):
    land = _in_hbm(lax.empty((8, *pack.shape), pack.dtype))

    def body(in_ref, land_ref, after_ref, send, recv, own, in_thru, land_thru, token):
        x, y, c = _mesh_pos()
        me = 4 * x + 2 * y + c
        for r in range(1, 8):
            dx, dy, dc = r // 4, (r // 2) % 2, r % 2
            pltpu.make_async_remote_copy(in_ref, land_ref.at[me], send.at[r - 1], recv.at[r - 1],
                                         device_id=((x + dx) % 2, (y + dy) % 2, (c + dc) % 2), device_id_type=MESH).start()
        pltpu.make_async_copy(in_ref, land_ref.at[me], own.at[0]).start()
        token[...] = jnp.zeros_like(token)

    res = pl.pallas_call(
        body, name="small_start", in_specs=[_HBM, _HBM, _ANY],
        out_specs=[_SEMS] * 3 + [_HBM, _HBM, pl.BlockSpec(memory_space=pltpu.VMEM)],
        out_shape=[pltpu.SemaphoreType.DMA((7,)), pltpu.SemaphoreType.DMA((7,)), pltpu.SemaphoreType.DMA((1,)),
                   pltpu.HBM(pack.shape, pack.dtype), pltpu.HBM(land.shape, land.dtype), jax.ShapeDtypeStruct((8, 128), F32)],
        input_output_aliases={0: 3, 1: 4}, **_SPLIT,
    )(_in_hbm(pack), land, after)
    return res[:3], res[3], res[4], res[5]


def _small_wait(sems, pack, land, after):
    def body(in_ref, land_ref, send, recv, own, after_ref, in_dead, got):
        x, y, c = _mesh_pos()
        me = 4 * x + 2 * y + c
        for r in range(1, 8):
            cp = pltpu.make_async_remote_copy(in_ref, land_ref.at[me], send.at[r - 1], recv.at[r - 1], device_id=(x, y, c),
                                              device_id_type=MESH)
            cp.wait_send()
            cp.wait_recv()
        pltpu.make_async_copy(in_ref, land_ref.at[me], own.at[0]).wait()

    res = pl.pallas_call(
        body, name="small_wait", in_specs=[_HBM, _HBM] + [_SEMS] * 3 + [_ANY], out_specs=[_HBM, _HBM],
        out_shape=[pltpu.HBM(pack.shape, pack.dtype), pltpu.HBM(land.shape, land.dtype)], input_output_aliases={0: 0, 1: 1}, **_SPLIT,
    )(pack, land, *sems, after)
    return res[1]


def _row_tile(R, dtype, target=256):
    mult = 8 * 4 // jnp.dtype(dtype).itemsize
    best = R
    for t in range(mult, min(R, target) + 1, mult):
        if R % t == 0:
            best = t
    return best


def _sum_slots(stack, name, out_dtype=F32):
    k, R, C = stack.shape
    tr = _row_tile(R, stack.dtype)
    tc = C
    if tr < 64:
        tr, tc = R, _pick(C, 512)

    def body(s_ref, o_ref):
        acc = s_ref[0].astype(F32)
        for j in range(1, k):
            acc = acc + s_ref[j].astype(F32)
        o_ref[...] = acc.astype(out_dtype)

    return pl.pallas_call(
        body, name=name, grid=(R // tr, C // tc), in_specs=[pl.BlockSpec((k, tr, tc), lambda i, j: (0, i, j))],
        out_specs=pl.BlockSpec((tr, tc), lambda i, j: (i, j)), out_shape=jax.ShapeDtypeStruct((R, C), out_dtype),
        compiler_params=_cp(("parallel", "parallel")),
    )(stack)


W_IN_SHARD = 1160
W_IN_PAD = 1168


def _pad_shards(a):
    zeros = jnp.zeros((W_IN_PAD - W_IN_SHARD, a.shape[1]), a.dtype)
    parts = []
    for s in range(a.shape[0] // W_IN_SHARD):
        parts += [a[s * W_IN_SHARD:(s + 1) * W_IN_SHARD], zeros]
    return jnp.concatenate(parts, axis=0).astype(BF16)


def _unpad_shards(a):
    a = a.astype(F32)
    return jnp.concatenate([a[s * W_IN_PAD:s * W_IN_PAD + W_IN_SHARD] for s in range(a.shape[0] // W_IN_PAD)], axis=0)


def _adamw(w, m, v, ga, gb, name, after=None):
    R, C = w.shape
    tr = _row_tile(R, F32, 128)
    gs = [ga] if gb is None else [ga, gb]
    extra = [] if after is None else [after]

    def body(*refs):
        w_ref, m_ref, v_ref = refs[:3]
        g = refs[3][...].astype(F32)
        if gb is not None:
            g = g + refs[4][...].astype(F32)
        g_ref, d_ref, nm_ref, nv_ref = refs[-4:]
        nm = ADAM_B1 * m_ref[...] + (1.0 - ADAM_B1) * g
        nv = ADAM_B2 * v_ref[...] + (1.0 - ADAM_B2) * (g * g)
        m_hat = nm / (1.0 - ADAM_B1 ** ADAM_STEP)
        v_hat = nv / (1.0 - ADAM_B2 ** ADAM_STEP)
        g_ref[...] = g
        d_ref[...] = -ADAM_LR * (m_hat / (jnp.sqrt(v_hat) + ADAM_EPS) + ADAM_WD * w_ref[...])
        nm_ref[...] = nm
        nv_ref[...] = nv

    spec = pl.BlockSpec((tr, C), lambda i: (i, 0))
    return pl.pallas_call(
        body, name=name, grid=(R // tr,), in_specs=[spec] * (3 + len(gs)) + [_ANY] * len(extra), out_specs=[spec] * 4,
        out_shape=[jax.ShapeDtypeStruct((R, C), F32)] * 4, compiler_params=_cp(("parallel",)),
    )(w, m, v, *gs, *extra)


def _pack(arrs):
    rows = []
    for a in arrs:
        flat = a.reshape(-1)
        rows.append(jnp.pad(flat, (0, -flat.shape[0] % 1024)).reshape(-1, 1024))
    p = jnp.concatenate(rows, axis=0)
    return jnp.pad(p, ((0, -p.shape[0] % 8), (0, 0)))


def _unpack(p, shapes):
    out, r = [], 0
    for s in shapes:
        n = 1
        for d in s:
            n *= d
        nr = -(-n // 1024)
        out.append(p[r:r + nr].reshape(-1)[:n].reshape(s))
        r += nr
    return out


BIG = ("f1_gate", "f1_up", "f1_down", "w_in", "w_out", "w_xq", "w_xkv", "w_xo", "f2_gate", "f2_up", "f2_down")
BIG_KIND = {"f1_gate": "col", "f1_up": "col", "f1_down": "row", "w_in": "row", "w_out": "row", "w_xq": "row", "w_xkv": "col",
            "w_xo": "row", "f2_gate": "col", "f2_up": "col", "f2_down": "row"}
LORA = ("rw_decay_up", "rw_aaa_up", "rw_gate_up")
WEIGHTS = ("f1_norm", "f1_gate", "f1_up", "f1_down", "mix_norm", "w_in", "b_in_attn", "rw_mu", "rw_w0", "rw_decay_up", "rw_a0",
           "rw_aaa_up", "rw_gate_up", "rw_k_k", "rw_k_a", "rw_r_k", "rw_lnx_w", "rw_lnx_b", "attn_sinks", "w_out", "b_out", "xa_norm",
           "mem_norm", "w_xq", "w_xkv", "w_xo", "f2_norm", "f2_gate", "f2_up", "f2_down", "final_norm")
SMALL = tuple(n for n in WEIGHTS if n not in BIG)
GROUP_ORDER = ("f1", "f1d", "mix", "out", "xattn", "f2")
GROUPS = {"f1": ("f1_gate", "f1_up"), "f1d": ("f1_down",), "mix": ("w_in",) + LORA, "out": ("w_out",), "xattn": ("w_xq", "w_xkv", "w_xo"),
          "f2": ("f2_gate", "f2_up", "f2_down")}


def kernel(x, mem, f1_norm, f1_gate, f1_up, f1_down, mix_norm, w_in, b_in_attn, rw_mu, rw_w0, rw_decay_up, rw_a0, rw_aaa_up, rw_gate_up, rw_k_k, rw_k_a, rw_r_k, rw_lnx_w, rw_lnx_b, attn_sinks, w_out, b_out, xa_norm, mem_norm, w_xq, w_xkv, w_xo, f2_norm, f2_gate, f2_up, f2_down, final_norm, loss_target, m_f1_norm, m_f1_gate, m_f1_up, m_f1_down, m_mix_norm, m_w_in, m_b_in_attn, m_rw_mu, m_rw_w0, m_rw_decay_up, m_rw_a0, m_rw_aaa_up, m_rw_gate_up, m_rw_k_k, m_rw_k_a, m_rw_r_k, m_rw_lnx_w, m_rw_lnx_b, m_attn_sinks, m_w_out, m_b_out, m_xa_norm, m_mem_norm, m_w_xq, m_w_xkv, m_w_xo, m_f2_norm, m_f2_gate, m_f2_up, m_f2_down, m_final_norm, v_f1_norm, v_f1_gate, v_f1_up, v_f1_down, v_mix_norm, v_w_in, v_b_in_attn, v_rw_mu, v_rw_w0, v_rw_decay_up, v_rw_a0, v_rw_aaa_up, v_rw_gate_up, v_rw_k_k, v_rw_k_a, v_rw_r_k, v_rw_lnx_w, v_rw_lnx_b, v_attn_sinks, v_w_out, v_b_out, v_xa_norm, v_mem_norm, v_w_xq, v_w_xkv, v_w_xo, v_f2_norm, v_f2_gate, v_f2_up, v_f2_down, v_final_norm):
    a = dict(locals())
    w = {n: a[n] for n in WEIGHTS}
    m = {n: a["m_" + n] for n in WEIGHTS}
    v = {n: a["v_" + n] for n in WEIGHTS}
    sq = lambda t: t.reshape(t.shape[-2:]) if t.ndim == 3 else t.reshape(1, -1)

    local_name = lambda n: "w_inT" if n == "w_in" else n
    kind_of = lambda n: BIG_KIND.get(n, "col")
    payload = lambda n: _pad_shards(sq(w[n]).T) if n == "w_in" else sq(w[n]) if n in LORA else sq(w[n]).astype(BF16)
    gathers = {}

    def start_gather(name, grps, after):
        shards = [payload(n) for g in grps for n in GROUPS[g]]
        kinds = [kind_of(n) for g in grps for n in GROUPS[g]]
        groups, at = [], 0
        for g in grps:
            groups.append(list(range(at, at + len(GROUPS[g]))))
            at += len(GROUPS[g])
        sems, src_thru, land_thru, token = _gather_start(name, shards, kinds, groups, after)
        for gi, g in enumerate(grps):
            gathers[g] = (sems[3 * gi:3 * gi + 3], [src_thru[i] for i in groups[gi]], [land_thru[i] for i in groups[gi]],
                          [kinds[i] for i in groups[gi]], token)

    early = GROUP_ORDER[:3]
    start_gather("gather_start_f1", early[:1], None)
    start_gather("gather_start", early[1:], gathers[early[0]][4])
    gathers[early[0]] = gathers[early[0]][:4] + (gathers[early[1]][4],)

    def get_w(grp, after):
        g_sems, g_src, g_land, g_kinds, token = gathers[grp]
        got = _gather_wait("gather_wait_" + grp, g_sems, g_src, g_land, g_kinds, token if after is None else after)
        got = _swap_halves("gather_swap_" + grp, got, [s.shape for s in g_src], g_kinds)
        out = {local_name(n): (_unpad_shards(f) if n == "w_in" else f) for n, f in zip(GROUPS[grp], got)}
        if grp == early[-1]:
            start_gather("gather_start_late", GROUP_ORDER[3:], got[0])
            out["_after"] = gathers[GROUP_ORDER[3]][4]
        return out

    in_flight = []

    def put_g(label, gw):
        names = list(gw)
        grads = [_pad_shards(gw[n]) if n == "w_in" else gw[n] for n in names]
        *flight, sent = _scatter_start("scatter_start_" + label, grads, [kind_of(n) for n in names])
        in_flight.append((label, names, flight))
        return sent

    P = {n: sq(w[n]) for n in SMALL if n not in LORA}
    P["attn_sinks"] = jnp.pad(P["attn_sinks"], ((0, 0), (0, 128 - P["attn_sinks"].shape[1])))
    P["rw_r_k"] = w["rw_r_k"].reshape(1, RW_W)
    loss_part, grad_x, gs = _local_step(x[0], mem[0], loss_target[0], get_w, P, put_g)

    gs["attn_sinks"] = gs["attn_sinks"][:, :16]
    small_flight = _small_start(_pack([gs[n] for n in SMALL] + [loss_part]), grad_x)

    out, after = {}, small_flight[-1]
    for bi, batch in enumerate((in_flight[:-3], in_flight[-3:])):
        b_names, b_partial = [], []
        for label, names, (g_sems, g_thru, l_thru) in batch:
            stacks = _scatter_wait("scatter_wait_" + label, g_sems, g_thru, l_thru, [kind_of(n) for n in names], after)
            partial = [_sum_slots(s, "sum_chips_" + n, F32 if n == "w_in" else BF16) for s, n in zip(stacks, names)]
            b_names += names
            b_partial += partial
            after = partial[-1]
        sibling = _swap_with_sibling(b_partial, "swap_batch%d" % bi)
        chain = None
        for n, pa, sb in zip(b_names, b_partial, sibling):
            if n == "w_in":
                pa, sb = pa[:W_IN_SHARD].T, sb[:W_IN_SHARD].T
            out[n] = _adamw(sq(w[n]), sq(m[n]), sq(v[n]), pa, sb, "adamw_" + n, after=chain)
            chain = out[n][1]
        after = chain

    gsum = _sum_slots(_small_wait(*small_flight[:-1], after), "sum_small")
    *summed, loss_row = _unpack(gsum, [gs[n].shape for n in SMALL] + [loss_part.shape])
    g_small = dict(zip(SMALL, summed))
    loss = loss_row[0, 0]
    shard = 2 * lax.axis_index("x") + lax.axis_index("y")
    for n in LORA:
        cols = w[n].shape[-1]
        g_small[n] = lax.dynamic_slice_in_dim(g_small[n], shard * cols, cols, axis=1)

    flat = lambda d: _pack([d[n] for n in SMALL])
    res = _adamw(flat(w), flat(m), flat(v), _pack([g_small[n] for n in SMALL]), None, "adamw_small")
    shapes = [w[n].shape for n in SMALL]
    for k, p in enumerate(res):
        for n, t in zip(SMALL, _unpack(p, shapes)):
            out.setdefault(n, [None] * 4)[k] = t
    outs = [loss, grad_x.reshape(x.shape)]
    for k in range(4):
        outs += [out[n][k].reshape(w[n].shape) for n in WEIGHTS]
    return tuple(outs)
```

```python
import functools

import jax
import jax.numpy as jnp
from jax import lax
from jax.experimental import pallas as pl
from jax.experimental.pallas import tpu as pltpu

F32, BF16 = jnp.float32, jnp.bfloat16
MESH = pl.DeviceIdType.MESH

HEAD = 64
RW_HEADS = 16
RW_W = 1024
SWA_W = 1024
KV_W = 128
DECAY_LORA, AAA_LORA, GATE_LORA = 64, 64, 160
LORA_W = DECAY_LORA + AAA_LORA + GATE_LORA
SHIFT_COLS = 3 * RW_W + LORA_W
XH = 4
XHD = 512
MEM_LEN = 256
WINDOW = 128
GN_EPS = 64e-5
RMS_EPS = 1e-6
NEG_INF = -1e30
ADAM_LR, ADAM_B1, ADAM_B2, ADAM_EPS, ADAM_WD, ADAM_STEP = 0.001, 0.9, 0.999, 1e-08, 0.01, 10

VMEM_LIMIT = 56 * 1024 * 1024


def _cp(sem=None, **kw):
    return pltpu.CompilerParams(dimension_semantics=sem, vmem_limit_bytes=VMEM_LIMIT, **kw)


def _pick(dim, target):
    if dim <= target:
        return dim
    best = None
    for t in range(128, target + 1, 128):
        if dim % t == 0:
            best = t
    assert best is not None, (dim, target)
    return best


_DIMS = {"nn": (((1,), (0,)), ((), ())), "nt": (((1,), (1,)), ((), ())), "tn": (((0,), (0,)), ((), ()))}


def _mm(a, b, mode, name, out_dtype=F32, alpha=1.0, res=None, bias=None, tm=1024, tn=1024, tk=2048, after=None):
    if mode == "nn":
        (M, K), (K2, N) = a.shape, b.shape
    elif mode == "nt":
        (M, K), (N, K2) = a.shape, b.shape
    else:
        (K, M), (K2, N) = a.shape, b.shape
    assert K == K2, (name, a.shape, b.shape)
    tm, tn, tk = _pick(M, tm), _pick(N, tn), _pick(K, tk)
    nk = K // tk
    a_spec = pl.BlockSpec((tk, tm), lambda i, j, k: (k, i)) if mode == "tn" else pl.BlockSpec((tm, tk), lambda i, j, k: (i, k))
    b_spec = pl.BlockSpec((tn, tk), lambda i, j, k: (j, k)) if mode == "nt" else pl.BlockSpec((tk, tn), lambda i, j, k: (k, j))
    o_spec = pl.BlockSpec((tm, tn), lambda i, j, k: (i, j))
    ins, specs = [a, b], [a_spec, b_spec]
    if res is not None:
        ins.append(res)
        specs.append(o_spec)
    if bias is not None:
        ins.append(bias)
        specs.append(pl.BlockSpec((1, tn), lambda i, j, k: (0, j)))
    if after is not None:
        ins.append(after)
        specs.append(pl.BlockSpec(memory_space=pl.ANY))
    dims = _DIMS[mode]

    def body(*refs):
        a_ref, b_ref = refs[0], refs[1]
        part = lax.dot_general(a_ref[...].astype(BF16), b_ref[...].astype(BF16), dims, preferred_element_type=F32)

        def finish(o, o_ref):
            if alpha != 1.0:
                o = o * alpha
            p = 2
            if res is not None:
                o = o + refs[p][...].astype(F32)
                p += 1
            if bias is not None:
                o = o + refs[p][...]
            o_ref[...] = o.astype(out_dtype)

        if nk == 1:
            finish(part, refs[-1])
            return
        o_ref, acc_ref = refs[-2], refs[-1]
        k = pl.program_id(2)

        @pl.when(k == 0)
        def _():
            acc_ref[...] = part

        @pl.when(k > 0)
        def _():
            acc_ref[...] += part

        @pl.when(k == nk - 1)
        def _():
            finish(acc_ref[...], o_ref)

    return pl.pallas_call(
        body, name=name, grid=(M // tm, N // tn, nk), in_specs=specs, out_specs=o_spec,
        out_shape=jax.ShapeDtypeStruct((M, N), out_dtype), scratch_shapes=[pltpu.VMEM((tm, tn), F32)] * (nk > 1),
        compiler_params=_cp(("parallel", "parallel", "arbitrary")),
    )(*ins)


def _rows(fn, name, T, tm, tiled, full, out_tiled, out_acc, extra=(), reverse=False, scratch=()):
    n = T // tm
    idx = (lambda i: n - 1 - i) if reverse else (lambda i: i)
    in_specs = [pl.BlockSpec((tm, a.shape[1]), lambda i: (idx(i), 0)) for a in tiled]
    in_specs += [mk(idx) for _, mk in extra]
    in_specs += [pl.BlockSpec(a.shape, lambda i, nd=a.ndim: (0,) * nd) for a in full]
    out_specs = [pl.BlockSpec((tm, c), lambda i: (idx(i), 0)) for c, _ in out_tiled]
    out_specs += [pl.BlockSpec(s, lambda i, nd=len(s): (0,) * nd) for s, _ in out_acc]
    out_shape = [jax.ShapeDtypeStruct((T, c), d) for c, d in out_tiled] + [jax.ShapeDtypeStruct(s, d) for s, d in out_acc]
    n_in = len(tiled) + len(extra) + len(full)
    n_t, n_a = len(out_tiled), len(out_acc)

    def body(*refs):
        step = pl.program_id(0)
        vals = [r[...] for r in refs[:n_in]]
        outs = fn(idx(step), *vals, *refs[n_in + n_t + n_a:])
        for r, v in zip(refs[n_in:n_in + n_t], outs[:n_t]):
            r[...] = v.astype(r.dtype)
        for r, v in zip(refs[n_in + n_t:n_in + n_t + n_a], outs[n_t:]):
            @pl.when(step == 0)
            def _(r=r):
                r[...] = jnp.zeros_like(r)

            r[...] += v

    return pl.pallas_call(
        body, name=name, grid=(n,), in_specs=in_specs, out_specs=out_specs, out_shape=out_shape,
        scratch_shapes=list(scratch), compiler_params=_cp(("arbitrary",)),
    )(*tiled, *[a for a, _ in extra], *full)


def _rms(x, g):
    return x * lax.rsqrt(jnp.mean(x * x, axis=-1, keepdims=True) + RMS_EPS) * g


def _rms_fwd(x, g, name, tm=256):
    (h,) = _rows(lambda i, x, g: (_rms(x, g),), name, x.shape[0], min(tm, x.shape[0]), [x], [g], [(x.shape[1], BF16)], [])
    return h


def _rms_bwd(x, g, dh, dres, name, tm=256):
    D = x.shape[1]

    def fn(i, x, dh, dres, g):
        _, vjp = jax.vjp(_rms, x, g)
        dx, dg = vjp(dh.astype(F32))
        dx = dx + dres
        return dx, dg, jnp.sum(dx, axis=0, keepdims=True)

    return _rows(fn, name, x.shape[0], tm, [x, dh, dres], [g], [(D, F32)], [((1, D), F32), ((1, D), F32)])


def _ffn_up(h, wg, wu, name, tm=1024, tn=512, after=None):
    (M, K), N = h.shape, wg.shape[1]
    tm, tn = _pick(M, tm), _pick(N, tn)

    def body(*refs):
        h_ref, wg_ref, wu_ref = refs[:3]
        g_ref, u_ref, a_ref = refs[-3:]
        hb = h_ref[...].astype(BF16)
        g = jnp.dot(hb, wg_ref[...].astype(BF16), preferred_element_type=F32)
        u = jnp.dot(hb, wu_ref[...].astype(BF16), preferred_element_type=F32)
        g_ref[...] = g
        u_ref[...] = u
        a_ref[...] = (g * jax.nn.sigmoid(g) * u).astype(BF16)

    o_spec = pl.BlockSpec((tm, tn), lambda i, j: (i, j))
    w_spec = pl.BlockSpec((K, tn), lambda i, j: (0, j))
    extra = [] if after is None else [after]
    return pl.pallas_call(
        body, name=name, grid=(M // tm, N // tn),
        in_specs=[pl.BlockSpec((tm, K), lambda i, j: (i, 0)), w_spec, w_spec] + [pl.BlockSpec(memory_space=pl.ANY)] * len(extra),
        out_specs=[o_spec] * 3, out_shape=[jax.ShapeDtypeStruct((M, N), F32)] * 2 + [jax.ShapeDtypeStruct((M, N), BF16)],
        compiler_params=_cp(("parallel", "parallel")),
    )(h, wg, wu, *extra)


def _ffn_dact(dxo, wd, g, u, name, tm=1024, tn=512):
    (M, K), N = dxo.shape, wd.shape[0]
    tm, tn = _pick(M, tm), _pick(N, tn)

    def body(dx_ref, wd_ref, g_ref, u_ref, dg_ref, du_ref):
        da = 0.5 * lax.dot_general(dx_ref[...].astype(BF16), wd_ref[...].astype(BF16), _DIMS["nt"], preferred_element_type=F32)
        g = g_ref[...]
        s = jax.nn.sigmoid(g)
        dg_ref[...] = (da * u_ref[...] * (s * (1.0 + g * (1.0 - s)))).astype(BF16)
        du_ref[...] = (da * (g * s)).astype(BF16)

    t_spec = pl.BlockSpec((tm, tn), lambda i, j: (i, j))
    return pl.pallas_call(
        body, name=name, grid=(M // tm, N // tn),
        in_specs=[pl.BlockSpec((tm, K), lambda i, j: (i, 0)), pl.BlockSpec((tn, K), lambda i, j: (j, 0)), t_spec, t_spec],
        out_specs=[t_spec, t_spec], out_shape=[jax.ShapeDtypeStruct((M, N), BF16)] * 2, compiler_params=_cp(("parallel", "parallel")),
    )(dxo, wd, g, u)


def _ffn_fwd(x, h, wg, wu, wd, tag, after=None):
    G, U, A = _ffn_up(h, wg, wu, tag + "_up", after=after)
    xo = _mm(A, wd(A) if callable(wd) else wd, "nn", tag + "_down", alpha=0.5, res=x)
    return xo, (h, G, U, A)


def _ffn_bwd(x, gain, wg, wu, wd, saved, dxo, tag, send):
    h, G, U, A = saved
    dwd = _mm(A, dxo, "tn", tag + "_dwd", out_dtype=BF16, alpha=0.5, tm=1408)
    sent = send(tag + "_down", {tag + "_down": dwd})
    dG, dU = _ffn_dact(dxo, wd, G, U, tag + "_dact")
    dwu = _mm(h, dU, "tn", tag + "_dwu", out_dtype=BF16, after=sent)
    sent = send(tag + "_up", {tag + "_up": dwu})
    dwg = _mm(h, dG, "tn", tag + "_dwg", out_dtype=BF16, after=sent)
    sent = send(tag + "_gate", {tag + "_gate": dwg})
    dh = _mm(dG, wg, "nt", tag + "_dh_g", after=sent)
    dh = _mm(dU, wu, "nt", tag + "_dh_u", res=dh)
    dx, dgain, _ = _rms_bwd(x, gain, dh, dxo, tag + "_norm_bwd")
    return dx, dgain


def _segsum64_impl(x):
    r = lax.broadcasted_iota(jnp.int32, (128, 128), 0) // HEAD
    c = lax.broadcasted_iota(jnp.int32, (128, 128), 1) // HEAD
    ones = (r == c).astype(BF16)
    hi = x.astype(BF16)
    lo = (x - hi.astype(F32)).astype(BF16)
    outs = []
    for q in range(x.shape[1] // 128):
        sl = slice(q * 128, (q + 1) * 128)
        outs.append(jnp.dot(hi[:, sl], ones, preferred_element_type=F32) + jnp.dot(lo[:, sl], ones, preferred_element_type=F32))
    return outs[0] if len(outs) == 1 else jnp.concatenate(outs, axis=1)


@jax.custom_vjp
def _segsum64(x):
    return _segsum64_impl(x)


_segsum64.defvjp(lambda x: (_segsum64_impl(x), None), lambda _, ct: (_segsum64_impl(ct),))


def _swap32(x):
    lane = lax.broadcasted_iota(jnp.int32, (x.shape[0], 128), 1)
    outs = [jnp.take_along_axis(x[:, q * 128:(q + 1) * 128], lane ^ 32, axis=1) for q in range(x.shape[1] // 128)]
    return outs[0] if len(outs) == 1 else jnp.concatenate(outs, axis=1)


def _tree_sum(xs):
    xs = list(xs)
    while len(xs) > 1:
        nxt = [xs[i] + xs[i + 1] for i in range(0, len(xs) - 1, 2)]
        if len(xs) % 2:
            nxt.append(xs[-1])
        xs = nxt
    return xs[0]


class _Acc:
    def __init__(self, ways=4):
        self.parts = [None] * ways

    def add(self, i, term):
        k = i % len(self.parts)
        self.parts[k] = term if self.parts[k] is None else self.parts[k] + term

    def total(self):
        return _tree_sum([p for p in self.parts if p is not None])


def _softplus(x):
    return jnp.maximum(x, 0.0) + jnp.log(1.0 + jnp.exp(-jnp.abs(x)))


def _pre_core(k, da, gd, w0, a0, k_k, k_a, w_da, gate_up):
    lane = lax.broadcasted_iota(jnp.int32, da.shape, 1)
    w_da = w_da.astype(BF16)
    l1 = jnp.dot(jnp.where(lane < DECAY_LORA, jnp.tanh(da), 0.0).astype(BF16), w_da, preferred_element_type=F32)
    l2 = jnp.dot(jnp.where(lane >= DECAY_LORA, da, 0.0).astype(BF16), w_da, preferred_element_type=F32)
    wlog = -_softplus(-(w0 + l1)) - 0.5
    decay = jnp.exp(-jnp.exp(wlog))
    a = jax.nn.sigmoid(a0 + l2)
    g = jnp.dot(jax.nn.sigmoid(gd).astype(BF16), gate_up.astype(BF16), preferred_element_type=F32)
    kk = k * k_k
    kkn = kk / jnp.maximum(jnp.sqrt(_segsum64(kk * kk)), 1e-12)
    k2 = k * (1.0 + (a - 1.0) * k_a)
    return decay, k2, -kkn, kkn * a, g


def _pre_shift(i, zr, zl, zr8, zl8, mu, mul):
    live = (i > 0).astype(F32)
    dz = _shift_down(zr, zr8[7:8, :] * live) - zr
    dzl = _shift_down(zl, zl8[7:8, :] * live) - zl
    return zr + dz * mu, zl + dzl * mul, dz, dzl


def _shift_down(x, first_row):
    rolled = pltpu.roll(x, 1, 0)
    row = lax.broadcasted_iota(jnp.int32, x.shape, 0)
    return jnp.where(row == 0, first_row, rolled)


def _shift_up(x, last_row):
    rolled = pltpu.roll(x, x.shape[0] - 1, 0)
    row = lax.broadcasted_iota(jnp.int32, x.shape, 0)
    return jnp.where(row == x.shape[0] - 1, last_row, rolled)


def _prev_rows_spec(tm, cols):
    return lambda idx: pl.BlockSpec((8, cols), lambda i: (jnp.maximum(idx(i) * (tm // 8) - 1, 0), 0))


def _rwkv_pre(p_rkv, p_lora, params, tm=256):
    T = p_rkv.shape[0]

    def fn(i, zr, zl, zr8, zl8, mu, mul, *ps):
        z, z2, _, _ = _pre_shift(i, zr, zl, zr8, zl8, mu, mul)
        decay, k2, an, bn, g = _pre_core(z[:, RW_W:2 * RW_W], z2[:, :128], z2[:, 128:], *ps)
        return z[:, :RW_W], decay, k2, z[:, 2 * RW_W:], an, bn, g

    extra = [(p_rkv, _prev_rows_spec(tm, 3 * RW_W)), (p_lora, _prev_rows_spec(tm, LORA_W))]
    return _rows(fn, "rwkv_pre", T, tm, [p_rkv, p_lora], list(params), [(RW_W, F32)] * 7, [], extra=extra)


def _rwkv_pre_bwd(p_rkv, p_lora, params, cts, tm=256):
    T = p_rkv.shape[0]
    n = T // tm

    def fn(i, zr, zl, cr, cdec, ck2, cv, can, cbn, cg, cr_b, ck2_b, cv_b, zr8, zl8, mu, mul, *rest):
        ps, (car, carl) = rest[:-2], rest[-2:]
        cr, ck2, cv = cr + cr_b, ck2 + ck2_b, cv + cv_b
        z, z2, dif, difl = _pre_shift(i, zr, zl, zr8, zl8, mu, mul)
        _, vjp = jax.vjp(_pre_core, z[:, RW_W:2 * RW_W], z2[:, :128], z2[:, 128:], *ps)
        dk, dda, dgd, *dps = vjp((cdec, ck2, can, cbn, cg))
        dz = jnp.concatenate([cr, dk, cv], axis=1)
        dz2 = jnp.concatenate([dda, dgd], axis=1)
        dzp, dzlp = dz * mu, dz2 * mul

        @pl.when(i == n - 1)
        def _():
            car[...] = jnp.zeros_like(car)
            carl[...] = jnp.zeros_like(carl)

        d_rkv = dz - dzp + _shift_up(dzp, car[0:1, :])
        d_lora = dz2 - dzlp + _shift_up(dzlp, carl[0:1, :])
        car[0:1, :] = dzp[0:1, :]
        carl[0:1, :] = dzlp[0:1, :]
        return (d_rkv, d_lora, jnp.sum(dz * dif, axis=0, keepdims=True), jnp.sum(dz2 * difl, axis=0, keepdims=True), *dps)

    extra = [(p_rkv, _prev_rows_spec(tm, 3 * RW_W)), (p_lora, _prev_rows_spec(tm, LORA_W))]
    acc = [(p.shape, F32) for p in params]
    return _rows(fn, "rwkv_pre_bwd", T, tm, [p_rkv, p_lora, *cts], list(params), [(3 * RW_W, BF16), (LORA_W, BF16)], acc,
                 extra=extra, reverse=True, scratch=[pltpu.VMEM((8, 3 * RW_W), F32), pltpu.VMEM((8, LORA_W), F32)])


def _post_core(y, r, k2, v, g, lw, lb, rk):
    mu = _segsum64(y) * (1.0 / HEAD)
    yc = y - mu
    var = _segsum64(yc * yc) * (1.0 / HEAD)
    yn = yc * lax.rsqrt(var + GN_EPS) * lw + lb
    return (yn + _segsum64(r * k2 * rk) * v) * g


def _rwkv_post(y, r, k2, v, g, lw, lb, rk, tm=256):
    (o,) = _rows(lambda i, *a: (_post_core(*a),), "rwkv_post", y.shape[0], tm, [y, r, k2, v, g], [lw, lb, rk], [(RW_W, BF16)], [])
    return o


def _rwkv_post_bwd(y, r, k2, v, g, lw, lb, rk, do, tm=256):
    def fn(i, y, r, k2, v, g, do, lw, lb, rk):
        _, vjp = jax.vjp(_post_core, y, r, k2, v, g, lw, lb, rk)
        return vjp(do.astype(F32))

    return _rows(fn, "rwkv_post_bwd", y.shape[0], tm, [y, r, k2, v, g, do], [lw, lb, rk], [(RW_W, F32)] * 5, [((1, RW_W), F32)] * 3)


SCAN_L = 64


def _to_perm(x):
    T = x.shape[0]
    return x.reshape(T, RW_HEADS, HEAD).transpose(0, 2, 1).reshape(T, 8, 128)


def _from_perm(x):
    T = x.shape[0]
    return x.reshape(T, HEAD, RW_HEADS).transpose(0, 2, 1).reshape(T, RW_W)


def _as_tile(p):
    lane = lax.broadcasted_iota(jnp.int32, (8, 128), 1)
    return jnp.take_along_axis(p, (lane % 8) * 16 + lane // 8, axis=1)


def _as_perm(t):
    lane = lax.broadcasted_iota(jnp.int32, (8, 128), 1)
    return jnp.take_along_axis(t, (lane % 16) * 8 + lane // 16, axis=1)


def _tiles_to_perm(refs, L):
    for r in refs:
        for t in range(L):
            r[t] = _as_perm(r[t])


def _expander(srcs, tiles=()):
    s = lax.broadcasted_iota(jnp.int32, (8, 128), 0)
    lane = lax.broadcasted_iota(jnp.int32, (8, 128), 1)
    idx = 16 * s + lane // 8

    def expand(t, e_ref):
        for m, r in enumerate(srcs):
            for g in range(8):
                row = jnp.broadcast_to(r[t, pl.ds(g, 1), :], (8, 128))
                e_ref[m, g * 8:(g + 1) * 8, :] = jnp.take_along_axis(row, idx, axis=1)
        for k, r in enumerate(tiles):
            e_ref[len(srcs) + k, 0:8, :] = _as_tile(r[t])

    return expand


def _ck_a_to_b(ck):
    n = ck.shape[0]
    return ck.reshape(n, 8, 8, 8, RW_HEADS, 8).transpose(0, 2, 5, 1, 4, 3).reshape(n, HEAD, 8, 128)


def _bc(row):
    return jnp.broadcast_to(row, (8, 128))


def _rsum(x):
    return jnp.sum(x, axis=0, keepdims=True)


def _plus(acc, k, term):
    acc[k] = term if acc[k] is None else acc[k] + term


def _scan_fwd(xes, vi):
    T, L = vi.shape[0], SCAN_L
    nch = T // L

    def body(*refs):
        xr, (vi_ref, yi_ref, sa_ref, ck_ref, st_ref, e0, e1) = refs[:5], refs[5:]

        @pl.when(pl.program_id(0) == 0)
        def _():
            st_ref[...] = jnp.zeros_like(st_ref)

        ck_ref[0] = st_ref[...]
        expand = _expander(xr, [vi_ref])
        expand(0, e0)

        def step(t, e):
            tile = lambda m, jh: e[m, 8 * jh:8 * jh + 8, :]
            vb = [_bc(e[5, ih:ih + 1, :]) for ih in range(8)]
            acc = [None] * 8
            for jh in range(8):
                a = tile(0, jh)
                for ih in range(8):
                    _plus(acc, ih, st_ref[8 * jh + ih] * a)
            sab = []
            for ih in range(8):
                row = _rsum(acc[ih])
                sa_ref[t, ih:ih + 1, :] = row
                sab.append(_bc(row))
            yacc = [None] * 8
            for jh in range(8):
                w, B, k, r = tile(1, jh), tile(2, jh), tile(3, jh), tile(4, jh)
                for ih in range(8):
                    s = st_ref[8 * jh + ih] * w + B * sab[ih] + k * vb[ih]
                    st_ref[8 * jh + ih] = s
                    _plus(yacc, ih, s * r)
            for ih in range(8):
                yi_ref[t, ih:ih + 1, :] = _rsum(yacc[ih])

        def pair(p, carry):
            t = 2 * p
            expand(t + 1, e1)
            step(t, e0)
            expand(jnp.minimum(t + 2, L - 1), e0)
            step(t + 1, e1)
            return carry

        lax.fori_loop(0, L // 2, pair, 0)
        _tiles_to_perm([yi_ref, sa_ref], L)

    tile = pl.BlockSpec((L, 8, 128), lambda c: (c, 0, 0))
    return pl.pallas_call(
        body, name="rwkv_scan_fwd", grid=(nch,), in_specs=[tile] * 6,
        out_specs=[tile, tile, pl.BlockSpec((1, HEAD, 8, 128), lambda c: (c, 0, 0, 0))],
        out_shape=[jax.ShapeDtypeStruct((T, 8, 128), F32)] * 2 + [jax.ShapeDtypeStruct((nch, HEAD, 8, 128), F32)],
        scratch_shapes=[pltpu.VMEM((HEAD, 8, 128), F32)] + [pltpu.VMEM((6, HEAD, 128), F32)] * 2, compiler_params=_cp(("arbitrary",)),
    )(*xes, vi)


def _scan_bwd_a(xes, dyi):
    T, L = dyi.shape[0], SCAN_L
    nch = T // L

    def body(*refs):
        xr, (dy_ref, dsa_ref, dv_ref, g_ref, e0, e1) = refs[:5], refs[5:]

        @pl.when(pl.program_id(0) == 0)
        def _():
            g_ref[...] = jnp.zeros_like(g_ref)

        expand = _expander(xr, [dy_ref])
        expand(L - 1, e0)

        def step(t, e):
            tile = lambda m, jh: e[m, 8 * jh:8 * jh + 8, :]
            dyb = [_bc(e[5, ih:ih + 1, :]) for ih in range(8)]
            dsa, dv = [None] * 8, [None] * 8
            for jh in range(8):
                B, k, r = tile(2, jh), tile(3, jh), tile(4, jh)
                for ih in range(8):
                    g = g_ref[8 * jh + ih] + r * dyb[ih]
                    g_ref[8 * jh + ih] = g
                    _plus(dsa, ih, g * B)
                    _plus(dv, ih, g * k)
            dsab = []
            for ih in range(8):
                row = _rsum(dsa[ih])
                dsa_ref[t, ih:ih + 1, :] = row
                dsab.append(_bc(row))
                dv_ref[t, ih:ih + 1, :] = _rsum(dv[ih])
            for jh in range(8):
                A, w = tile(0, jh), tile(1, jh)
                for ih in range(8):
                    g_ref[8 * jh + ih] = g_ref[8 * jh + ih] * w + A * dsab[ih]

        def pair(p, carry):
            t = L - 1 - 2 * p
            expand(t - 1, e1)
            step(t, e0)
            expand(jnp.maximum(t - 2, 0), e0)
            step(t - 1, e1)
            return carry

        lax.fori_loop(0, L // 2, pair, 0)
        _tiles_to_perm([dsa_ref, dv_ref], L)

    tile = pl.BlockSpec((L, 8, 128), lambda c: (nch - 1 - c, 0, 0))
    return pl.pallas_call(
        body, name="rwkv_scan_bwd_a", grid=(nch,), in_specs=[tile] * 6, out_specs=[tile, tile],
        out_shape=[jax.ShapeDtypeStruct((T, 8, 128), F32)] * 2,
        scratch_shapes=[pltpu.VMEM((HEAD, 8, 128), F32)] + [pltpu.VMEM((6, HEAD, 128), F32)] * 2, compiler_params=_cp(("arbitrary",)),
    )(*xes, dyi)


def _scan_bwd_b(xts, ies, ckb):
    T, L = xts[0].shape[0], SCAN_L
    nch = T // L

    def body(*refs):
        xr, er, ck_ref, dj, (hist, g_ref, e0, e1) = refs[:5], refs[5:9], refs[9], refs[10:15], refs[15:]

        @pl.when(pl.program_id(0) == 0)
        def _():
            g_ref[...] = jnp.zeros_like(g_ref)

        hist[0] = ck_ref[0]
        expand_vs = _expander(er[:2], [xr[1], xr[2], xr[3]])
        expand = _expander(er, [xr[0], xr[1], xr[4]])
        expand_vs(0, e0)

        def fstep(t, e_ref):
            w, B, k = e_ref[2, 0:8, :], e_ref[3, 0:8, :], e_ref[4, 0:8, :]
            row = lambda m, i: jnp.broadcast_to(e_ref[m, pl.ds(i, 1), :], (8, 128))
            for i in range(HEAD):
                hist[t + 1, i] = hist[t, i] * w + row(1, i) * B + row(0, i) * k

        def fpair(p, carry):
            t = 2 * p
            expand_vs(t + 1, e1)
            fstep(t, e0)
            expand_vs(jnp.minimum(t + 2, L - 1), e0)
            fstep(t + 1, e1)
            return carry

        lax.fori_loop(0, L // 2, fpair, 0)
        expand(L - 1, e0)

        def bstep(t, e_ref):
            A, w, r = e_ref[4, 0:8, :], e_ref[5, 0:8, :], e_ref[6, 0:8, :]
            row = lambda m, i: jnp.broadcast_to(e_ref[m, pl.ds(i, 1), :], (8, 128))
            acc = [_Acc() for _ in range(5)]
            for i in range(HEAD):
                dy_i, dsa_i = row(2, i), row(3, i)
                g = g_ref[i] + dy_i * r
                sp = hist[t, i]
                acc[4].add(i, hist[t + 1, i] * dy_i)
                acc[1].add(i, g * sp)
                acc[2].add(i, g * row(1, i))
                acc[3].add(i, g * row(0, i))
                acc[0].add(i, sp * dsa_i)
                g_ref[i] = g * w + dsa_i * A
            for m in range(5):
                dj[m][t] = acc[m].total()

        def bpair(p, carry):
            t = L - 1 - 2 * p
            expand(t - 1, e1)
            bstep(t, e0)
            expand(jnp.maximum(t - 2, 0), e0)
            bstep(t - 1, e1)
            return carry

        lax.fori_loop(0, L // 2, bpair, 0)
        _tiles_to_perm(dj, L)

    tile = pl.BlockSpec((L, 8, 128), lambda c: (nch - 1 - c, 0, 0))
    return pl.pallas_call(
        body, name="rwkv_scan_bwd_b", grid=(nch,),
        in_specs=[tile] * 9 + [pl.BlockSpec((1, HEAD, 8, 128), lambda c: (nch - 1 - c, 0, 0, 0))],
        out_specs=[tile] * 5, out_shape=[jax.ShapeDtypeStruct((T, 8, 128), F32)] * 5,
        scratch_shapes=[pltpu.VMEM((L + 1, HEAD, 8, 128), F32), pltpu.VMEM((HEAD, 8, 128), F32)] + [pltpu.VMEM((7, HEAD, 128), F32)] * 2,
        compiler_params=_cp(("arbitrary",)),
    )(*xts, *ies, ckb)


SWA_COLS = SWA_W + 2 * KV_W
BLK = 128


def _swa_core(n, k2a, k2b, vla, vra, vlb, vrb, sinks, *qps):
    iq = lax.broadcasted_iota(jnp.int32, (BLK, 2 * BLK), 0)
    ik = lax.broadcasted_iota(jnp.int32, (BLK, 2 * BLK), 1)
    diff = BLK + iq - ik
    valid = (diff >= 0) & (diff < WINDOW) & ((n > 0) | (ik >= BLK))
    lane = lax.broadcasted_iota(jnp.int32, (BLK, 128), 1)
    lane1 = lax.broadcasted_iota(jnp.int32, (1, 128), 1)
    nt = (((1,), (1,)), ((), ()))
    outs = []
    for pp in range(8):
        k2, vl, vr = (k2a, vla, vra) if pp < 4 else (k2b, vlb, vrb)
        qp = qps[pp]
        o = None
        for half, vv in ((0, vl), (1, vr)):
            qh = jnp.where((lane >= HEAD) == (half == 1), qp, 0.0).astype(BF16)
            s = lax.dot_general(qh, k2.astype(BF16), nt, preferred_element_type=F32) * (HEAD ** -0.5)
            s = jnp.where(valid, s, NEG_INF)
            sink = jnp.sum(jnp.where(lane1 == 2 * pp + half, sinks, 0.0), axis=1, keepdims=True)
            m = jnp.maximum(jnp.max(s, axis=1, keepdims=True), sink)
            p = jnp.exp(s - m)
            den = jnp.sum(p, axis=1, keepdims=True) + jnp.exp(sink - m)
            oh = jnp.dot((p / den).astype(BF16), vv.astype(BF16), preferred_element_type=F32)
            o = oh if o is None else o + oh
        outs.append(o)
    return jnp.concatenate(outs, axis=1)


def _swa_prep(pc, pp, b, cq, sq, ckc, skc, ckp, skp):
    zc, zp = pc + b, pp + b
    qr = zc[:, :SWA_W] * cq + _swap32(zc[:, :SWA_W]) * sq
    kc, kp = zc[:, SWA_W:SWA_W + KV_W], zp[:, SWA_W:SWA_W + KV_W]
    kb = jnp.concatenate([kp * ckp + _swap32(kp) * skp, kc * ckc + _swap32(kc) * skc], axis=0)
    vb = jnp.concatenate([zp[:, SWA_W + KV_W:], zc[:, SWA_W + KV_W:]], axis=0)
    lane = lax.broadcasted_iota(jnp.int32, kb.shape, 1)
    left = lane < HEAD
    kbr, vbr = pltpu.roll(kb, HEAD, 1), pltpu.roll(vb, HEAD, 1)
    return (jnp.where(left, kb, kbr), jnp.where(left, kbr, kb), jnp.where(left, vb, 0.0), jnp.where(left, 0.0, vbr),
            jnp.where(left, vbr, 0.0), jnp.where(left, 0.0, vb)), [qr[:, q * 128:(q + 1) * 128] for q in range(8)]


def _swa_specs(T, tabs_q, tabs_k):
    cur = lambda c: pl.BlockSpec((BLK, c), lambda n: (n, 0))
    prev = lambda c: pl.BlockSpec((BLK, c), lambda n: (jnp.maximum(n - 1, 0), 0))
    return cur, prev


def _swa_fwd(p_swa, b, sinks, cq, sq, ck, sk):
    T = p_swa.shape[0]
    cur, prev = _swa_specs(T, None, None)

    def body(pc, pp, b_ref, s_ref, cq_r, sq_r, ckc, skc, ckp, skp, o_ref):
        ops, qps = _swa_prep(pc[...], pp[...], b_ref[...], cq_r[...], sq_r[...], ckc[...], skc[...], ckp[...], skp[...])
        o_ref[...] = _swa_core(pl.program_id(0), *ops, s_ref[...], *qps).astype(o_ref.dtype)

    full = lambda a: pl.BlockSpec(a.shape, lambda n: (0, 0))
    return pl.pallas_call(
        body, name="swa_fwd", grid=(T // BLK,),
        in_specs=[cur(SWA_COLS), prev(SWA_COLS), full(b), full(sinks), cur(SWA_W), cur(SWA_W), cur(KV_W), cur(KV_W), prev(KV_W), prev(KV_W)],
        out_specs=cur(SWA_W), out_shape=jax.ShapeDtypeStruct((T, SWA_W), BF16), compiler_params=_cp(("arbitrary",)),
    )(p_swa, p_swa, b, sinks, cq, sq, ck, sk, ck, sk)


def _swa_bwd(p_swa, b, sinks, cq, sq, ck, sk, do):
    T = p_swa.shape[0]
    nb = T // BLK
    cur = lambda c: pl.BlockSpec((BLK, c), lambda s: (nb - 1 - s, 0))
    prev = lambda c: pl.BlockSpec((BLK, c), lambda s: (jnp.maximum(nb - 2 - s, 0), 0))

    def body(pc, pp, b_ref, s_ref, cq_r, sq_r, ckc, skc, ckp, skp, do_ref, dcur, db, dsk, carry):
        step = pl.program_id(0)
        n = nb - 1 - step

        @pl.when(step == 0)
        def _():
            carry[...] = jnp.zeros_like(carry)
            db[...] = jnp.zeros_like(db)
            dsk[...] = jnp.zeros_like(dsk)

        ops, qps = _swa_prep(pc[...], pp[...], b_ref[...], cq_r[...], sq_r[...], ckc[...], skc[...], ckp[...], skp[...])
        _, vjp = jax.vjp(functools.partial(_swa_core, n), *ops, s_ref[...], *qps)
        dk2a, dk2b, dvla, dvra, dvlb, dvrb, dsinks, *dqps = vjp(do_ref[...].astype(F32))
        dqr = jnp.concatenate(dqps, axis=1)
        lane = lax.broadcasted_iota(jnp.int32, dk2a.shape, 1)
        left = lane < HEAD
        dkb = jnp.where(left, dk2a + pltpu.roll(dk2a, HEAD, 1), dk2b + pltpu.roll(dk2b, HEAD, 1))
        dvb = jnp.where(left, dvla + pltpu.roll(dvra, HEAD, 1), pltpu.roll(dvlb, HEAD, 1) + dvrb)
        dq = dqr * cq_r[...] + _swap32(dqr * sq_r[...])
        dkp, dkc = dkb[:BLK], dkb[BLK:]
        dkp = dkp * ckp[...] + _swap32(dkp * skp[...])
        dkc = dkc * ckc[...] + _swap32(dkc * skc[...])
        dc = jnp.concatenate([dq, jnp.concatenate([dkc, dvb[BLK:]], axis=1) + carry[...]], axis=1)
        carry[...] = jnp.concatenate([dkp, dvb[:BLK]], axis=1)
        dcur[...] = dc.astype(dcur.dtype)
        db[...] += jnp.sum(dc, axis=0, keepdims=True)
        dsk[...] += dsinks

    full = lambda a: pl.BlockSpec(a.shape, lambda s: (0, 0))
    return pl.pallas_call(
        body, name="swa_bwd", grid=(nb,),
        in_specs=[cur(SWA_COLS), prev(SWA_COLS), full(b), full(sinks), cur(SWA_W), cur(SWA_W), cur(KV_W), cur(KV_W), prev(KV_W), prev(KV_W),
                  cur(SWA_W)],
        out_specs=[cur(SWA_COLS), full(b), full(sinks)],
        out_shape=[jax.ShapeDtypeStruct((T, SWA_COLS), BF16), jax.ShapeDtypeStruct(b.shape, F32), jax.ShapeDtypeStruct(sinks.shape, F32)],
        scratch_shapes=[pltpu.VMEM((BLK, 2 * KV_W), F32)], compiler_params=_cp(("arbitrary",)),
    )(p_swa, p_swa, b, sinks, cq, sq, ck, sk, ck, sk, do)


def _rope_tables(T):
    inv = 10000.0 ** (-jnp.arange(0, HEAD, 2, dtype=F32) / HEAD)
    ang = jnp.arange(T, dtype=F32)[:, None] * inv[None, :]
    c = jnp.concatenate([jnp.cos(ang), jnp.cos(ang)], axis=1)
    s = jnp.concatenate([-jnp.sin(ang), jnp.sin(ang)], axis=1)
    return jnp.tile(c, (1, 16)), jnp.tile(s, (1, 16)), jnp.tile(c, (1, 2)), jnp.tile(s, (1, 2))


def _xattn_core(*qkv):
    outs = []
    for h in range(XH):
        qh, kh, vh = qkv[h], qkv[XH + h], qkv[2 * XH + h]
        s = lax.dot_general(qh.astype(BF16), kh.astype(BF16), (((1,), (1,)), ((), ())), preferred_element_type=F32) * (XHD ** -0.5)
        p = jnp.exp(s - jnp.max(s, axis=1, keepdims=True))
        p = p / jnp.sum(p, axis=1, keepdims=True)
        outs.append(jnp.dot(p.astype(BF16), vh.astype(BF16), preferred_element_type=F32))
    return jnp.concatenate(outs, axis=1)


def _xattn_split(q, kv):
    return [q[:, h * XHD:(h + 1) * XHD] for h in range(XH)] + [kv[:, h * XHD:(h + 1) * XHD] for h in range(2 * XH)]


def _xattn_fwd(q, kv, tm=256):
    (o,) = _rows(lambda i, q, kv: (_xattn_core(*_xattn_split(q, kv)),), "xattn_fwd", q.shape[0], tm, [q], [kv], [(q.shape[1], BF16)], [])
    return o


def _xattn_bwd(q, kv, do, tm=256):
    def fn(i, q, do, kv):
        _, vjp = jax.vjp(_xattn_core, *_xattn_split(q, kv))
        d = vjp(do.astype(F32))
        return jnp.concatenate(d[:XH], axis=1), jnp.concatenate(d[XH:], axis=1)

    return _rows(fn, "xattn_bwd", q.shape[0], tm, [q, do], [kv], [(q.shape[1], BF16)], [(kv.shape, F32)])


def _loss_head(x, g, tgt, tm=256):
    D = x.shape[1]

    def fn(i, x, tgt, g):
        y, vjp = jax.vjp(_rms, x, g)
        err = y - tgt
        dx, dg = vjp(err * (1.0 / D))
        part = 0.5 / D * jnp.sum(jnp.sum(err * err, axis=1, keepdims=True), axis=0, keepdims=True)
        return dx, jnp.broadcast_to(part, (1, 128)), dg

    return _rows(fn, "loss_head", x.shape[0], tm, [x, tgt], [g], [(D, F32)], [((1, 128), F32), ((1, D), F32)])


def _local_step(x, mem, tgt, get_w, P, put_g):
    T = x.shape[0]
    h1 = _rms_fwd(x, P["f1_norm"], "f1_norm")
    mn = _rms_fwd(mem, P["mem_norm"], "mem_norm")
    W = dict(get_w("f1", None))

    def f1_down(after):
        W.update(get_w("f1d", after))
        return W["f1_down"]

    x1, s1 = _ffn_fwd(x, h1, W["f1_gate"], W["f1_up"], f1_down, "f1")

    h2 = _rms_fwd(x1, P["mix_norm"], "mix_norm")
    W.update(get_w("mix", h2))
    w_rkv, w_lora, w_swa = W["w_inT"][:3 * RW_W], W["w_inT"][3 * RW_W:SHIFT_COLS], W["w_inT"][SHIFT_COLS:]
    p_rkv = _mm(h2, w_rkv, "nt", "in_rkv", after=W.get("_after"))
    p_lora = _mm(h2, w_lora, "nt", "in_lora")
    p_swa = _mm(h2, w_swa, "nt", "in_swa")
    w_da = jnp.concatenate([W["rw_decay_up"], W["rw_aaa_up"]], axis=0)
    pre_params = (P["rw_mu"][:, :3 * RW_W], P["rw_mu"][:, 3 * RW_W:], P["rw_w0"], P["rw_a0"], P["rw_k_k"], P["rw_k_a"], w_da,
                  W["rw_gate_up"])
    r, decay, k2, v, an, bn, g = _rwkv_pre(p_rkv, p_lora, pre_params)
    scan_vecs = (an, decay, bn, k2, r)
    xes = [_to_perm(a) for a in scan_vecs]
    v_p = _to_perm(v)
    yi, sai, ck = _scan_fwd(xes, v_p)
    y_scan = _from_perm(yi)
    y_rw = _rwkv_post(y_scan, r, k2, v, g, P["rw_lnx_w"], P["rw_lnx_b"], P["rw_r_k"])
    cq, sq, ckt, skt = _rope_tables(T)
    y_swa = _swa_fwd(p_swa, P["b_in_attn"], P["attn_sinks"], cq, sq, ckt, skt)
    ycat = jnp.concatenate([y_rw, y_swa], axis=1)
    W.update(get_w("out", ycat))
    x2 = _mm(ycat, W["w_out"], "nn", "out_proj", res=x1, bias=P["b_out"])

    hx = _rms_fwd(x2, P["xa_norm"], "xa_norm")
    W.update(get_w("xattn", hx))
    q = _mm(hx, W["w_xq"], "nn", "xq", out_dtype=BF16)
    kv = _mm(mn, W["w_xkv"], "nn", "xkv", out_dtype=BF16)
    o = _xattn_fwd(q, kv)
    x3 = _mm(o, W["w_xo"], "nn", "xo", res=x2)

    h3 = _rms_fwd(x3, P["f2_norm"], "f2_norm")
    W.update(get_w("f2", h3))
    x4, s2 = _ffn_fwd(x3, h3, W["f2_gate"], W["f2_up"], W["f2_down"], "f2")
    dx4, loss_part, d_final = _loss_head(x4, P["final_norm"], tgt)

    gs = {"final_norm": d_final}
    dx3, gs["f2_norm"] = _ffn_bwd(x3, P["f2_norm"], W["f2_gate"], W["f2_up"], W["f2_down"], s2, dx4, "f2", put_g)

    do = _mm(dx3, W["w_xo"], "nt", "xo_do", out_dtype=BF16)
    dw_xo = _mm(o, dx3, "tn", "xo_dw", out_dtype=BF16)
    dq, dkv = _xattn_bwd(q, kv, do)
    dw_xq = _mm(hx, dq, "tn", "xq_dw", out_dtype=BF16)
    dw_xkv = _mm(mn, dkv, "tn", "xkv_dw", out_dtype=BF16)
    sent = put_g("xattn", {"w_xq": dw_xq, "w_xkv": dw_xkv, "w_xo": dw_xo})
    dhx = _mm(dq, W["w_xq"], "nt", "xq_dh", after=sent)
    dmn = _mm(dkv, W["w_xkv"], "nt", "xkv_dmn")
    _, gs["mem_norm"], _ = _rms_bwd(mem, P["mem_norm"], dmn, jnp.zeros_like(mem), "mem_norm_bwd")
    dx2, gs["xa_norm"], gs["b_out"] = _rms_bwd(x2, P["xa_norm"], dhx, dx3, "xa_norm_bwd")

    dycat = _mm(dx2, W["w_out"], "nt", "out_dy")
    dw_out = _mm(ycat, dx2, "tn", "out_dw", out_dtype=BF16)
    dp_swa, gs["b_in_attn"], gs["attn_sinks"] = _swa_bwd(p_swa, P["b_in_attn"], P["attn_sinks"], cq, sq, ckt, skt, dycat[:, RW_W:])
    dy_scan, dr_b, dk2_b, dv_b, dg, gs["rw_lnx_w"], gs["rw_lnx_b"], gs["rw_r_k"] = _rwkv_post_bwd(
        y_scan, r, k2, v, g, P["rw_lnx_w"], P["rw_lnx_b"], P["rw_r_k"], dycat[:, :RW_W])
    dy_p = _to_perm(dy_scan)
    dsai, dvi = _scan_bwd_a(xes, dy_p)
    dj = _scan_bwd_b(xes, [v_p, sai, dy_p, dsai], _ck_a_to_b(ck))
    dan, ddecay, dbn, dk2_s, dr_s = (_from_perm(d) for d in dj)
    cts = (dr_s, ddecay, dk2_s, _from_perm(dvi), dan, dbn, dg, dr_b, dk2_b, dv_b)
    dp_rkv, dp_lora, dmu, dmul, gs["rw_w0"], gs["rw_a0"], gs["rw_k_k"], gs["rw_k_a"], dw_da, gs["rw_gate_up"] = _rwkv_pre_bwd(
        p_rkv, p_lora, pre_params, cts)
    gs["rw_mu"] = jnp.concatenate([dmu, dmul], axis=1)
    gs["rw_decay_up"], gs["rw_aaa_up"] = dw_da[:DECAY_LORA], dw_da[DECAY_LORA:]
    dw_inT = jnp.concatenate([_mm(dp_rkv, h2, "tn", "in_dw_rkv"), _mm(dp_lora, h2, "tn", "in_dw_lora"),
                              _mm(dp_swa, h2, "tn", "in_dw_swa")], axis=0)
    sent = put_g("mix", {"w_in": dw_inT, "w_out": dw_out})
    dh2 = _mm(dp_rkv, w_rkv, "nn", "in_dh_rkv", after=sent)
    dh2 = _mm(dp_lora, w_lora, "nn", "in_dh_lora", res=dh2)
    dh2 = _mm(dp_swa, w_swa, "nn", "in_dh_swa", res=dh2)
    dx1, gs["mix_norm"], _ = _rms_bwd(x1, P["mix_norm"], dh2, dx2, "mix_norm_bwd")

    dx0, gs["f1_norm"] = _ffn_bwd(x, P["f1_norm"], W["f1_gate"], W["f1_up"], W["f1_down"], s1, dx1, "f1", put_g)
    return loss_part, dx0, gs


_ANY = pl.BlockSpec(memory_space=pl.ANY)
_OTHER_CHIPS = ((1, 0), (0, 1), (1, 1))


def _mesh_pos():
    return lax.axis_index("x"), lax.axis_index("y"), lax.axis_index("c")


def _slot(ref, kind, s, rows, cols):
    if kind == "row":
        return ref.at[pl.ds(pl.multiple_of(s * rows, 8), rows), :]
    return ref.at[:, pl.ds(pl.multiple_of(s * cols, 128), cols)]


_HBM = pl.BlockSpec(memory_space=pltpu.HBM)
_SEMS = pl.BlockSpec(memory_space=pltpu.SEMAPHORE)
_SPLIT = dict(compiler_params=pltpu.CompilerParams(has_side_effects=pltpu.SideEffectType.DATAFLOW_SIDE_EFFECTING))


def _in_hbm(a):
    return pltpu.with_memory_space_constraint(a, pltpu.HBM)


def _full_shape(s, kind):
    return (4 * s.shape[0], s.shape[1]) if kind == "row" else (s.shape[0], 4 * s.shape[1])


def _half(ref, shape, h):
    rows, cols = shape
    if rows % 32 == 0:
        return ref.at[pl.ds(pl.multiple_of(h * (rows // 2), 16), rows // 2), :]
    assert cols % 256 == 0, shape
    return ref.at[:, pl.ds(pl.multiple_of(h * (cols // 2), 128), cols // 2)]


def _half_shape(shape):
    rows, cols = shape
    return (rows // 2, cols) if rows % 32 == 0 else (rows, cols // 2)


def _streams(src, dst, shape, c):
    hs = _half_shape(shape)
    s, d = _half(src, shape, c), _half(dst, shape, c)
    return [(_half(s, hs, q), _half(d, hs, q)) for q in range(2)]


def _swap_halves(name, fulls, shard_shapes, kinds):
    n = len(fulls)

    def body(*refs):
        out, send, recv = refs[n:2 * n], refs[2 * n], refs[2 * n + 1]
        x, y, c = _mesh_pos()
        sent = []
        for i in range(n):
            for r, (dx, dy) in enumerate(_OTHER_CHIPS):
                theirs = _slot(out[i], kinds[i], 2 * ((x + dx) % 2) + (y + dy) % 2, *shard_shapes[i])
                have = _half(theirs, shard_shapes[i], c)
                rc = pltpu.make_async_remote_copy(have, have, send.at[3 * i + r], recv.at[3 * i + r], device_id=(x, y, 1 - c),
                                                  device_id_type=MESH)
                rc.start()
                sent.append(rc)
        for i in range(n):
            for r, (dx, dy) in enumerate(_OTHER_CHIPS):
                theirs = _slot(out[i], kinds[i], 2 * ((x + dx) % 2) + (y + dy) % 2, *shard_shapes[i])
                need = _half(theirs, shard_shapes[i], 1 - c)
                pltpu.make_async_remote_copy(need, need, send.at[3 * i + r], recv.at[3 * i + r], device_id=(x, y, c),
                                             device_id_type=MESH).wait_recv()
        for rc in sent:
            rc.wait_send()

    return pl.pallas_call(
        body, name=name, in_specs=[_ANY] * n, out_specs=[_ANY] * n, out_shape=[jax.ShapeDtypeStruct(f.shape, f.dtype) for f in fulls],
        input_output_aliases={i: i for i in range(n)},
        scratch_shapes=[pltpu.SemaphoreType.DMA((3 * n,)), pltpu.SemaphoreType.DMA((3 * n,))],
    )(*fulls)


def _gather_start(name, shards, kinds, groups, after=None):
    n, ng = len(shards), len(groups)
    lands = [_in_hbm(lax.empty(_full_shape(s, k), s.dtype)) for s, k in zip(shards, kinds)]
    n_in = 2 * n + (after is not None)

    def body(*refs):
        src, land, sems, token = refs[:n], refs[n:2 * n], refs[n_in:n_in + 3 * ng], refs[-1]
        x, y, c = _mesh_pos()
        me = 2 * x + y
        for gi, idxs in enumerate(groups):
            send, recv, own = sems[3 * gi:3 * gi + 3]
            for k, i in enumerate(idxs):
                mine = _slot(land[i], kinds[i], me, *src[i].shape)
                for r, (dx, dy) in enumerate(_OTHER_CHIPS):
                    for q, (s, d) in enumerate(_streams(src[i], mine, src[i].shape, c)):
                        pltpu.make_async_remote_copy(s, d, send.at[6 * k + 2 * r + q], recv.at[6 * k + 2 * r + q],
                                                     device_id=((x + dx) % 2, (y + dy) % 2, c), device_id_type=MESH).start()
                pltpu.make_async_copy(src[i], mine, own.at[k]).start()
        token[...] = jnp.zeros_like(token)

    sem_shapes = [pltpu.SemaphoreType.DMA((w * len(g),)) for g in groups for w in (6, 6, 1)]
    thru = [pltpu.HBM(a.shape, a.dtype) for a in (*shards, *lands)]
    res = pl.pallas_call(
        body, name=name, in_specs=[_HBM] * (2 * n) + [_ANY] * (after is not None),
        out_specs=[_SEMS] * (3 * ng) + [_HBM] * (2 * n) + [pl.BlockSpec(memory_space=pltpu.VMEM)],
        out_shape=sem_shapes + thru + [jax.ShapeDtypeStruct((8, 128), F32)],
        input_output_aliases={i: 3 * ng + i for i in range(2 * n)}, **_SPLIT,
    )(*[_in_hbm(s) for s in shards], *lands, *([] if after is None else [after]))
    return res[:3 * ng], res[3 * ng:3 * ng + n], res[3 * ng + n:3 * ng + 2 * n], res[-1]


def _gather_wait(name, sems, shards, lands, kinds, after):
    m = len(shards)

    def body(*refs):
        src, land, (send, recv, own) = refs[:m], refs[m:2 * m], refs[2 * m:2 * m + 3]
        x, y, c = _mesh_pos()
        me = 2 * x + y
        for k in range(m):
            mine = _slot(land[k], kinds[k], me, *src[k].shape)
            for r in range(3):
                for q, (s, d) in enumerate(_streams(src[k], mine, src[k].shape, c)):
                    cp = pltpu.make_async_remote_copy(s, d, send.at[6 * k + 2 * r + q], recv.at[6 * k + 2 * r + q], device_id=(x, y, c),
                                                      device_id_type=MESH)
                    cp.wait_send()
                    cp.wait_recv()
            pltpu.make_async_copy(src[k], mine, own.at[k]).wait()

    thru = [pltpu.HBM(a.shape, a.dtype) for a in (*shards, *lands)]
    res = pl.pallas_call(
        body, name=name, in_specs=[_HBM] * (2 * m) + [_SEMS] * 3 + [pl.BlockSpec(memory_space=pl.ANY)],
        out_specs=[_HBM] * (2 * m), out_shape=thru, input_output_aliases={i: i for i in range(2 * m)}, **_SPLIT,
    )(*shards, *lands, *sems, after)
    return res[m:]


def _scatter_start(name, grads, kinds):
    m = len(grads)
    shard_shape = [(g.shape[0] // 4, g.shape[1]) if k == "row" else (g.shape[0], g.shape[1] // 4) for g, k in zip(grads, kinds)]
    lands = [_in_hbm(lax.empty((4, *s), g.dtype)) for s, g in zip(shard_shape, grads)]

    def body(*refs):
        src, land, (send, recv, own) = refs[:m], refs[m:2 * m], refs[2 * m:2 * m + 3]
        x, y, c = _mesh_pos()
        me = 2 * x + y
        for k in range(m):
            for r, (dx, dy) in enumerate(_OTHER_CHIPS):
                tx, ty = (x + dx) % 2, (y + dy) % 2
                pltpu.make_async_remote_copy(_slot(src[k], kinds[k], 2 * tx + ty, *shard_shape[k]), land[k].at[me],
                                             send.at[3 * k + r], recv.at[3 * k + r], device_id=(tx, ty, c), device_id_type=MESH).start()
            pltpu.make_async_copy(_slot(src[k], kinds[k], me, *shard_shape[k]), land[k].at[me], own.at[k]).start()
        refs[-1][...] = jnp.zeros_like(refs[-1])

    thru = [pltpu.HBM(a.shape, a.dtype) for a in (*grads, *lands)]
    res = pl.pallas_call(
        body, name=name, in_specs=[_HBM] * (2 * m),
        out_specs=[_SEMS] * 3 + [_HBM] * (2 * m) + [pl.BlockSpec(memory_space=pltpu.VMEM)],
        out_shape=[pltpu.SemaphoreType.DMA((3 * m,))] * 2 + [pltpu.SemaphoreType.DMA((m,))] + thru + [jax.ShapeDtypeStruct((8, 128), F32)],
        input_output_aliases={i: 3 + i for i in range(2 * m)}, **_SPLIT,
    )(*[_in_hbm(g) for g in grads], *lands)
    return res[:3], res[3:3 + m], res[3 + m:3 + 2 * m], res[-1]


def _scatter_wait(name, sems, grads, lands, kinds, after):
    m = len(grads)

    def body(*refs):
        src, land, (send, recv, own) = refs[:m], refs[m:2 * m], refs[2 * m:2 * m + 3]
        x, y, c = _mesh_pos()
        me = 2 * x + y
        for k in range(m):
            mine = _slot(src[k], kinds[k], me, *land[k].shape[1:])
            for r in range(3):
                cp = pltpu.make_async_remote_copy(mine, land[k].at[me], send.at[3 * k + r], recv.at[3 * k + r],
                                                  device_id=(x, y, c), device_id_type=MESH)
                cp.wait_send()
                cp.wait_recv()
            pltpu.make_async_copy(mine, land[k].at[me], own.at[k]).wait()

    thru = [pltpu.HBM(a.shape, a.dtype) for a in (*grads, *lands)]
    res = pl.pallas_call(
        body, name=name, in_specs=[_HBM] * (2 * m) + [_SEMS] * 3 + [pl.BlockSpec(memory_space=pl.ANY)],
        out_specs=[_HBM] * (2 * m), out_shape=thru, input_output_aliases={i: i for i in range(2 * m)}, **_SPLIT,
    )(*grads, *lands, *sems, after)
    return res[m:]


def _swap_with_sibling(arrs, name):
    n = len(arrs)

    def body(*refs):
        ins, outs = refs[:n], refs[n:2 * n]
        send, recv = refs[2 * n:]
        x, y, c = _mesh_pos()
        copies = []
        for i in range(n):
            rc = pltpu.make_async_remote_copy(ins[i], outs[i], send.at[i], recv.at[i], device_id=(x, y, 1 - c), device_id_type=MESH)
            rc.start()
            copies.append(rc)
        for rc in copies:
            rc.wait()

    return pl.pallas_call(
        body, name=name, in_specs=[_ANY] * n, out_specs=[_ANY] * n,
        out_shape=[jax.ShapeDtypeStruct(a.shape, a.dtype) for a in arrs],
        scratch_shapes=[pltpu.SemaphoreType.DMA((n,)), pltpu.SemaphoreType.DMA((n,))],
    )(*arrs)


def _small_start(pack, after):
    land = _in_hbm(lax.empty((8, *pack.shape), pack.dtype))

    def body(in_ref, land_ref, after_ref, send, recv, own, in_thru, land_thru, token):
        x, y, c = _mesh_pos()
        me = 4 * x + 2 * y + c
        for r in range(1, 8):
            dx, dy, dc = r // 4, (r // 2) % 2, r % 2
            pltpu.make_async_remote_copy(in_ref, land_ref.at[me], send.at[r - 1], recv.at[r - 1],
                                         device_id=((x + dx) % 2, (y + dy) % 2, (c + dc) % 2), device_id_type=MESH).start()
        pltpu.make_async_copy(in_ref, land_ref.at[me], own.at[0]).start()
        token[...] = jnp.zeros_like(token)

    res = pl.pallas_call(
        body, name="small_start", in_specs=[_HBM, _HBM, _ANY],
        out_specs=[_SEMS] * 3 + [_HBM, _HBM, pl.BlockSpec(memory_space=pltpu.VMEM)],
        out_shape=[pltpu.SemaphoreType.DMA((7,)), pltpu.SemaphoreType.DMA((7,)), pltpu.SemaphoreType.DMA((1,)),
                   pltpu.HBM(pack.shape, pack.dtype), pltpu.HBM(land.shape, land.dtype), jax.ShapeDtypeStruct((8, 128), F32)],
        input_output_aliases={0: 3, 1: 4}, **_SPLIT,
    )(_in_hbm(pack), land, after)
    return res[:3], res[3], res[4], res[5]


def _small_wait(sems, pack, land, after):
    def body(in_ref, land_ref, send, recv, own, after_ref, in_dead, got):
        x, y, c = _mesh_pos()
        me = 4 * x + 2 * y + c
        for r in range(1, 8):
            cp = pltpu.make_async_remote_copy(in_ref, land_ref.at[me], send.at[r - 1], recv.at[r - 1], device_id=(x, y, c),
                                              device_id_type=MESH)
            cp.wait_send()
            cp.wait_recv()
        pltpu.make_async_copy(in_ref, land_ref.at[me], own.at[0]).wait()

    res = pl.pallas_call(
        body, name="small_wait", in_specs=[_HBM, _HBM] + [_SEMS] * 3 + [_ANY], out_specs=[_HBM, _HBM],
        out_shape=[pltpu.HBM(pack.shape, pack.dtype), pltpu.HBM(land.shape, land.dtype)], input_output_aliases={0: 0, 1: 1}, **_SPLIT,
    )(pack, land, *sems, after)
    return res[1]


def _row_tile(R, dtype, target=256):
    mult = 8 * 4 // jnp.dtype(dtype).itemsize
    best = R
    for t in range(mult, min(R, target) + 1, mult):
        if R % t == 0:
            best = t
    return best


def _sum_slots(stack, name, out_dtype=F32):
    k, R, C = stack.shape
    tr = _row_tile(R, stack.dtype)
    tc = C
    if tr < 64:
        tr, tc = R, _pick(C, 512)

    def body(s_ref, o_ref):
        acc = s_ref[0].astype(F32)
        for j in range(1, k):
            acc = acc + s_ref[j].astype(F32)
        o_ref[...] = acc.astype(out_dtype)

    return pl.pallas_call(
        body, name=name, grid=(R // tr, C // tc), in_specs=[pl.BlockSpec((k, tr, tc), lambda i, j: (0, i, j))],
        out_specs=pl.BlockSpec((tr, tc), lambda i, j: (i, j)), out_shape=jax.ShapeDtypeStruct((R, C), out_dtype),
        compiler_params=_cp(("parallel", "parallel")),
    )(stack)


W_IN_SHARD = 1160
W_IN_PAD = 1168


def _pad_shards(a):
    zeros = jnp.zeros((W_IN_PAD - W_IN_SHARD, a.shape[1]), a.dtype)
    parts = []
    for s in range(a.shape[0] // W_IN_SHARD):
        parts += [a[s * W_IN_SHARD:(s + 1) * W_IN_SHARD], zeros]
    return jnp.concatenate(parts, axis=0).astype(BF16)


def _unpad_shards(a):
    a = a.astype(F32)
    return jnp.concatenate([a[s * W_IN_PAD:s * W_IN_PAD + W_IN_SHARD] for s in range(a.shape[0] // W_IN_PAD)], axis=0)


def _adamw(w, m, v, ga, gb, name, after=None):
    R, C = w.shape
    tr = _row_tile(R, F32, 128)
    gs = [ga] if gb is None else [ga, gb]
    extra = [] if after is None else [after]

    def body(*refs):
        w_ref, m_ref, v_ref = refs[:3]
        g = refs[3][...].astype(F32)
        if gb is not None:
            g = g + refs[4][...].astype(F32)
        g_ref, d_ref, nm_ref, nv_ref = refs[-4:]
        nm = ADAM_B1 * m_ref[...] + (1.0 - ADAM_B1) * g
        nv = ADAM_B2 * v_ref[...] + (1.0 - ADAM_B2) * (g * g)
        m_hat = nm / (1.0 - ADAM_B1 ** ADAM_STEP)
        v_hat = nv / (1.0 - ADAM_B2 ** ADAM_STEP)
        g_ref[...] = g
        d_ref[...] = -ADAM_LR * (m_hat / (jnp.sqrt(v_hat) + ADAM_EPS) + ADAM_WD * w_ref[...])
        nm_ref[...] = nm
        nv_ref[...] = nv

    spec = pl.BlockSpec((tr, C), lambda i: (i, 0))
    return pl.pallas_call(
        body, name=name, grid=(R // tr,), in_specs=[spec] * (3 + len(gs)) + [_ANY] * len(extra), out_specs=[spec] * 4,
        out_shape=[jax.ShapeDtypeStruct((R, C), F32)] * 4, compiler_params=_cp(("parallel",)),
    )(w, m, v, *gs, *extra)


def _pack(arrs):
    rows = []
    for a in arrs:
        flat = a.reshape(-1)
        rows.append(jnp.pad(flat, (0, -flat.shape[0] % 1024)).reshape(-1, 1024))
    p = jnp.concatenate(rows, axis=0)
    return jnp.pad(p, ((0, -p.shape[0] % 8), (0, 0)))


def _unpack(p, shapes):
    out, r = [], 0
    for s in shapes:
        n = 1
        for d in s:
            n *= d
        nr = -(-n // 1024)
        out.append(p[r:r + nr].reshape(-1)[:n].reshape(s))
        r += nr
    return out


BIG = ("f1_gate", "f1_up", "f1_down", "w_in", "w_out", "w_xq", "w_xkv", "w_xo", "f2_gate", "f2_up", "f2_down")
BIG_KIND = {"f1_gate": "col", "f1_up": "col", "f1_down": "row", "w_in": "row", "w_out": "row", "w_xq": "row", "w_xkv": "col",
            "w_xo": "row", "f2_gate": "col", "f2_up": "col", "f2_down": "row"}
LORA = ("rw_decay_up", "rw_aaa_up", "rw_gate_up")
WEIGHTS = ("f1_norm", "f1_gate", "f1_up", "f1_down", "mix_norm", "w_in", "b_in_attn", "rw_mu", "rw_w0", "rw_decay_up", "rw_a0",
           "rw_aaa_up", "rw_gate_up", "rw_k_k", "rw_k_a", "rw_r_k", "rw_lnx_w", "rw_lnx_b", "attn_sinks", "w_out", "b_out", "xa_norm",
           "mem_norm", "w_xq", "w_xkv", "w_xo", "f2_norm", "f2_gate", "f2_up", "f2_down", "final_norm")
SMALL = tuple(n for n in WEIGHTS if n not in BIG)
GROUP_ORDER = ("f1", "f1d", "mix", "out", "xattn", "f2")
GROUPS = {"f1": ("f1_gate", "f1_up"), "f1d": ("f1_down",), "mix": ("w_in",) + LORA, "out": ("w_out",), "xattn": ("w_xq", "w_xkv", "w_xo"),
          "f2": ("f2_gate", "f2_up", "f2_down")}


def kernel(x, mem, f1_norm, f1_gate, f1_up, f1_down, mix_norm, w_in, b_in_attn, rw_mu, rw_w0, rw_decay_up, rw_a0, rw_aaa_up, rw_gate_up, rw_k_k, rw_k_a, rw_r_k, rw_lnx_w, rw_lnx_b, attn_sinks, w_out, b_out, xa_norm, mem_norm, w_xq, w_xkv, w_xo, f2_norm, f2_gate, f2_up, f2_down, final_norm, loss_target, m_f1_norm, m_f1_gate, m_f1_up, m_f1_down, m_mix_norm, m_w_in, m_b_in_attn, m_rw_mu, m_rw_w0, m_rw_decay_up, m_rw_a0, m_rw_aaa_up, m_rw_gate_up, m_rw_k_k, m_rw_k_a, m_rw_r_k, m_rw_lnx_w, m_rw_lnx_b, m_attn_sinks, m_w_out, m_b_out, m_xa_norm, m_mem_norm, m_w_xq, m_w_xkv, m_w_xo, m_f2_norm, m_f2_gate, m_f2_up, m_f2_down, m_final_norm, v_f1_norm, v_f1_gate, v_f1_up, v_f1_down, v_mix_norm, v_w_in, v_b_in_attn, v_rw_mu, v_rw_w0, v_rw_decay_up, v_rw_a0, v_rw_aaa_up, v_rw_gate_up, v_rw_k_k, v_rw_k_a, v_rw_r_k, v_rw_lnx_w, v_rw_lnx_b, v_attn_sinks, v_w_out, v_b_out, v_xa_norm, v_mem_norm, v_w_xq, v_w_xkv, v_w_xo, v_f2_norm, v_f2_gate, v_f2_up, v_f2_down, v_final_norm):
    a = dict(locals())
    w = {n: a[n] for n in WEIGHTS}
    m = {n: a["m_" + n] for n in WEIGHTS}
    v = {n: a["v_" + n] for n in WEIGHTS}
    sq = lambda t: t.reshape(t.shape[-2:]) if t.ndim == 3 else t.reshape(1, -1)

    local_name = lambda n: "w_inT" if n == "w_in" else n
    kind_of = lambda n: BIG_KIND.get(n, "col")
    payload = lambda n: _pad_shards(sq(w[n]).T) if n == "w_in" else sq(w[n]) if n in LORA else sq(w[n]).astype(BF16)
    gathers = {}

    def start_gather(name, grps, after):
        shards = [payload(n) for g in grps for n in GROUPS[g]]
        kinds = [kind_of(n) for g in grps for n in GROUPS[g]]
        groups, at = [], 0
        for g in grps:
            groups.append(list(range(at, at + len(GROUPS[g]))))
            at += len(GROUPS[g])
        sems, src_thru, land_thru, token = _gather_start(name, shards, kinds, groups, after)
        for gi, g in enumerate(grps):
            gathers[g] = (sems[3 * gi:3 * gi + 3], [src_thru[i] for i in groups[gi]], [land_thru[i] for i in groups[gi]],
                          [kinds[i] for i in groups[gi]], token)

    early = GROUP_ORDER[:3]
    start_gather("gather_start_f1", early[:1], None)
    start_gather("gather_start", early[1:], gathers[early[0]][4])
    gathers[early[0]] = gathers[early[0]][:4] + (gathers[early[1]][4],)

    def get_w(grp, after):
        g_sems, g_src, g_land, g_kinds, token = gathers[grp]
        got = _gather_wait("gather_wait_" + grp, g_sems, g_src, g_land, g_kinds, token if after is None else after)
        got = _swap_halves("gather_swap_" + grp, got, [s.shape for s in g_src], g_kinds)
        out = {local_name(n): (_unpad_shards(f) if n == "w_in" else f) for n, f in zip(GROUPS[grp], got)}
        if grp == early[-1]:
            start_gather("gather_start_late", GROUP_ORDER[3:], got[0])
            out["_after"] = gathers[GROUP_ORDER[3]][4]
        return out

    in_flight = []

    def put_g(label, gw):
        names = list(gw)
        grads = [_pad_shards(gw[n]) if n == "w_in" else gw[n] for n in names]
        *flight, sent = _scatter_start("scatter_start_" + label, grads, [kind_of(n) for n in names])
        in_flight.append((label, names, flight))
        return sent

    P = {n: sq(w[n]) for n in SMALL if n not in LORA}
    P["attn_sinks"] = jnp.pad(P["attn_sinks"], ((0, 0), (0, 128 - P["attn_sinks"].shape[1])))
    P["rw_r_k"] = w["rw_r_k"].reshape(1, RW_W)
    loss_part, grad_x, gs = _local_step(x[0], mem[0], loss_target[0], get_w, P, put_g)

    gs["attn_sinks"] = gs["attn_sinks"][:, :16]
    small_flight = _small_start(_pack([gs[n] for n in SMALL] + [loss_part]), grad_x)

    out, after = {}, small_flight[-1]
    for bi, batch in enumerate((in_flight[:-3], in_flight[-3:])):
        b_names, b_partial = [], []
        for label, names, (g_sems, g_thru, l_thru) in batch:
            stacks = _scatter_wait("scatter_wait_" + label, g_sems, g_thru, l_thru, [kind_of(n) for n in names], after)
            partial = [_sum_slots(s, "sum_chips_" + n, F32 if n == "w_in" else BF16) for s, n in zip(stacks, names)]
            b_names += names
            b_partial += partial
            after = partial[-1]
        sibling = _swap_with_sibling(b_partial, "swap_batch%d" % bi)
        chain = None
        for n, pa, sb in zip(b_names, b_partial, sibling):
            if n == "w_in":
                pa, sb = pa[:W_IN_SHARD].T, sb[:W_IN_SHARD].T
            out[n] = _adamw(sq(w[n]), sq(m[n]), sq(v[n]), pa, sb, "adamw_" + n, after=chain)
            chain = out[n][1]
        after = chain
        if bi == 0:
            gsum = _sum_slots(_small_wait(*small_flight[:-1], after), "sum_small")
            *summed, loss_row = _unpack(gsum, [gs[n].shape for n in SMALL] + [loss_part.shape])
            g_small = dict(zip(SMALL, summed))
            loss = loss_row[0, 0]
            shard = 2 * lax.axis_index("x") + lax.axis_index("y")
            for n in LORA:
                cols = w[n].shape[-1]
                g_small[n] = lax.dynamic_slice_in_dim(g_small[n], shard * cols, cols, axis=1)
            flat = lambda d: _pack([d[n] for n in SMALL])
            res = _adamw(flat(w), flat(m), flat(v), _pack([g_small[n] for n in SMALL]), None, "adamw_small")
            after = res[1]

    shapes = [w[n].shape for n in SMALL]
    for k, p in enumerate(res):
        for n, t in zip(SMALL, _unpack(p, shapes)):
            out.setdefault(n, [None] * 4)[k] = t
    outs = [loss, grad_x.reshape(x.shape)]
    for k in range(4):
        outs += [out[n][k].reshape(w[n].shape) for n in WEIGHTS]
    return tuple(outs)
```

```python
import functools

import jax
import jax.numpy as jnp
from jax import lax
from jax.experimental import pallas as pl
from jax.experimental.pallas import tpu as pltpu

F32, BF16 = jnp.float32, jnp.bfloat16
MESH = pl.DeviceIdType.MESH

HEAD = 64
RW_HEADS = 16
RW_W = 1024
SWA_W = 1024
KV_W = 128
DECAY_LORA, AAA_LORA, GATE_LORA = 64, 64, 160
LORA_W = DECAY_LORA + AAA_LORA + GATE_LORA
SHIFT_COLS = 3 * RW_W + LORA_W
XH = 4
XHD = 512
MEM_LEN = 256
WINDOW = 128
GN_EPS = 64e-5
RMS_EPS = 1e-6
NEG_INF = -1e30
ADAM_LR, ADAM_B1, ADAM_B2, ADAM_EPS, ADAM_WD, ADAM_STEP = 0.001, 0.9, 0.999, 1e-08, 0.01, 10

VMEM_LIMIT = 56 * 1024 * 1024


def _cp(sem=None, **kw):
    return pltpu.CompilerParams(dimension_semantics=sem, vmem_limit_bytes=VMEM_LIMIT, **kw)


def _pick(dim, target):
    if dim <= target:
        return dim
    best = None
    for t in range(128, target + 1, 128):
        if dim % t == 0:
            best = t
    assert best is not None, (dim, target)
    return best


_DIMS = {"nn": (((1,), (0,)), ((), ())), "nt": (((1,), (1,)), ((), ())), "tn": (((0,), (0,)), ((), ()))}


def _mm(a, b, mode, name, out_dtype=F32, alpha=1.0, res=None, bias=None, tm=1024, tn=1024, tk=2048, after=None):
    if mode == "nn":
        (M, K), (K2, N) = a.shape, b.shape
    elif mode == "nt":
        (M, K), (N, K2) = a.shape, b.shape
    else:
        (K, M), (K2, N) = a.shape, b.shape
    assert K == K2, (name, a.shape, b.shape)
    tm, tn, tk = _pick(M, tm), _pick(N, tn), _pick(K, tk)
    nk = K // tk
    a_spec = pl.BlockSpec((tk, tm), lambda i, j, k: (k, i)) if mode == "tn" else pl.BlockSpec((tm, tk), lambda i, j, k: (i, k))
    b_spec = pl.BlockSpec((tn, tk), lambda i, j, k: (j, k)) if mode == "nt" else pl.BlockSpec((tk, tn), lambda i, j, k: (k, j))
    o_spec = pl.BlockSpec((tm, tn), lambda i, j, k: (i, j))
    ins, specs = [a, b], [a_spec, b_spec]
    if res is not None:
        ins.append(res)
        specs.append(o_spec)
    if bias is not None:
        ins.append(bias)
        specs.append(pl.BlockSpec((1, tn), lambda i, j, k: (0, j)))
    if after is not None:
        ins.append(after)
        specs.append(pl.BlockSpec(memory_space=pl.ANY))
    dims = _DIMS[mode]

    def body(*refs):
        a_ref, b_ref = refs[0], refs[1]
        part = lax.dot_general(a_ref[...].astype(BF16), b_ref[...].astype(BF16), dims, preferred_element_type=F32)

        def finish(o, o_ref):
            if alpha != 1.0:
                o = o * alpha
            p = 2
            if res is not None:
                o = o + refs[p][...].astype(F32)
                p += 1
            if bias is not None:
                o = o + refs[p][...]
            o_ref[...] = o.astype(out_dtype)

        if nk == 1:
            finish(part, refs[-1])
            return
        o_ref, acc_ref = refs[-2], refs[-1]
        k = pl.program_id(2)

        @pl.when(k == 0)
        def _():
            acc_ref[...] = part

        @pl.when(k > 0)
        def _():
            acc_ref[...] += part

        @pl.when(k == nk - 1)
        def _():
            finish(acc_ref[...], o_ref)

    return pl.pallas_call(
        body, name=name, grid=(M // tm, N // tn, nk), in_specs=specs, out_specs=o_spec,
        out_shape=jax.ShapeDtypeStruct((M, N), out_dtype), scratch_shapes=[pltpu.VMEM((tm, tn), F32)] * (nk > 1),
        compiler_params=_cp(("parallel", "parallel", "arbitrary")),
    )(*ins)


def _mm2_nt(a1, b1, a2, b2, name, after, tm=1024, tn=1024, tk=2048):
    (M, K), N = a1.shape, b1.shape[0]
    tm, tn, tk = _pick(M, tm), _pick(N, tn), _pick(K, tk)
    nk = K // tk
    assert nk > 1, (name, K, tk)

    def body(a1_ref, b1_ref, a2_ref, b2_ref, after_ref, o_ref, acc_ref):
        dot = lambda a, b: lax.dot_general(a[...].astype(BF16), b[...].astype(BF16), _DIMS["nt"], preferred_element_type=F32)
        part = dot(a1_ref, b1_ref) + dot(a2_ref, b2_ref)
        k = pl.program_id(2)

        @pl.when(k == 0)
        def _():
            acc_ref[...] = part

        @pl.when(k > 0)
        def _():
            acc_ref[...] += part

        @pl.when(k == nk - 1)
        def _():
            o_ref[...] = acc_ref[...]

    a_spec = pl.BlockSpec((tm, tk), lambda i, j, k: (i, k))
    b_spec = pl.BlockSpec((tn, tk), lambda i, j, k: (j, k))
    return pl.pallas_call(
        body, name=name, grid=(M // tm, N // tn, nk), in_specs=[a_spec, b_spec, a_spec, b_spec, pl.BlockSpec(memory_space=pl.ANY)],
        out_specs=pl.BlockSpec((tm, tn), lambda i, j, k: (i, j)), out_shape=jax.ShapeDtypeStruct((M, N), F32),
        scratch_shapes=[pltpu.VMEM((tm, tn), F32)], compiler_params=_cp(("parallel", "parallel", "arbitrary")),
    )(a1, b1, a2, b2, after)


def _rows(fn, name, T, tm, tiled, full, out_tiled, out_acc, extra=(), reverse=False, scratch=()):
    n = T // tm
    idx = (lambda i: n - 1 - i) if reverse else (lambda i: i)
    in_specs = [pl.BlockSpec((tm, a.shape[1]), lambda i: (idx(i), 0)) for a in tiled]
    in_specs += [mk(idx) for _, mk in extra]
    in_specs += [pl.BlockSpec(a.shape, lambda i, nd=a.ndim: (0,) * nd) for a in full]
    out_specs = [pl.BlockSpec((tm, c), lambda i: (idx(i), 0)) for c, _ in out_tiled]
    out_specs += [pl.BlockSpec(s, lambda i, nd=len(s): (0,) * nd) for s, _ in out_acc]
    out_shape = [jax.ShapeDtypeStruct((T, c), d) for c, d in out_tiled] + [jax.ShapeDtypeStruct(s, d) for s, d in out_acc]
    n_in = len(tiled) + len(extra) + len(full)
    n_t, n_a = len(out_tiled), len(out_acc)

    def body(*refs):
        step = pl.program_id(0)
        vals = [r[...] for r in refs[:n_in]]
        outs = fn(idx(step), *vals, *refs[n_in + n_t + n_a:])
        for r, v in zip(refs[n_in:n_in + n_t], outs[:n_t]):
            r[...] = v.astype(r.dtype)
        for r, v in zip(refs[n_in + n_t:n_in + n_t + n_a], outs[n_t:]):
            @pl.when(step == 0)
            def _(r=r):
                r[...] = jnp.zeros_like(r)

            r[...] += v

    return pl.pallas_call(
        body, name=name, grid=(n,), in_specs=in_specs, out_specs=out_specs, out_shape=out_shape,
        scratch_shapes=list(scratch), compiler_params=_cp(("arbitrary",)),
    )(*tiled, *[a for a, _ in extra], *full)


def _rms(x, g):
    return x * lax.rsqrt(jnp.mean(x * x, axis=-1, keepdims=True) + RMS_EPS) * g


def _rms_fwd(x, g, name, tm=256):
    (h,) = _rows(lambda i, x, g: (_rms(x, g),), name, x.shape[0], min(tm, x.shape[0]), [x], [g], [(x.shape[1], BF16)], [])
    return h


def _rms_bwd(x, g, dh, dres, name, tm=256):
    D = x.shape[1]

    def fn(i, x, dh, dres, g):
        _, vjp = jax.vjp(_rms, x, g)
        dx, dg = vjp(dh.astype(F32))
        dx = dx + dres
        return dx, dg, jnp.sum(dx, axis=0, keepdims=True)

    return _rows(fn, name, x.shape[0], tm, [x, dh, dres], [g], [(D, F32)], [((1, D), F32), ((1, D), F32)])


def _ffn_up(h, wg, wu, name, tm=1024, tn=512, after=None):
    (M, K), N = h.shape, wg.shape[1]
    tm, tn = _pick(M, tm), _pick(N, tn)

    def body(*refs):
        h_ref, wg_ref, wu_ref = refs[:3]
        g_ref, u_ref, a_ref = refs[-3:]
        hb = h_ref[...].astype(BF16)
        g = jnp.dot(hb, wg_ref[...].astype(BF16), preferred_element_type=F32)
        u = jnp.dot(hb, wu_ref[...].astype(BF16), preferred_element_type=F32)
        g_ref[...] = g
        u_ref[...] = u
        a_ref[...] = (g * jax.nn.sigmoid(g) * u).astype(BF16)

    o_spec = pl.BlockSpec((tm, tn), lambda i, j: (i, j))
    w_spec = pl.BlockSpec((K, tn), lambda i, j: (0, j))
    extra = [] if after is None else [after]
    return pl.pallas_call(
        body, name=name, grid=(M // tm, N // tn),
        in_specs=[pl.BlockSpec((tm, K), lambda i, j: (i, 0)), w_spec, w_spec] + [pl.BlockSpec(memory_space=pl.ANY)] * len(extra),
        out_specs=[o_spec] * 3, out_shape=[jax.ShapeDtypeStruct((M, N), F32)] * 2 + [jax.ShapeDtypeStruct((M, N), BF16)],
        compiler_params=_cp(("parallel", "parallel")),
    )(h, wg, wu, *extra)


def _ffn_dact(dxo, wd, g, u, name, tm=1024, tn=512):
    (M, K), N = dxo.shape, wd.shape[0]
    tm, tn = _pick(M, tm), _pick(N, tn)

    def body(dx_ref, wd_ref, g_ref, u_ref, dg_ref, du_ref):
        da = 0.5 * lax.dot_general(dx_ref[...].astype(BF16), wd_ref[...].astype(BF16), _DIMS["nt"], preferred_element_type=F32)
        g = g_ref[...]
        s = jax.nn.sigmoid(g)
        dg_ref[...] = (da * u_ref[...] * (s * (1.0 + g * (1.0 - s)))).astype(BF16)
        du_ref[...] = (da * (g * s)).astype(BF16)

    t_spec = pl.BlockSpec((tm, tn), lambda i, j: (i, j))
    return pl.pallas_call(
        body, name=name, grid=(M // tm, N // tn),
        in_specs=[pl.BlockSpec((tm, K), lambda i, j: (i, 0)), pl.BlockSpec((tn, K), lambda i, j: (j, 0)), t_spec, t_spec],
        out_specs=[t_spec, t_spec], out_shape=[jax.ShapeDtypeStruct((M, N), BF16)] * 2, compiler_params=_cp(("parallel", "parallel")),
    )(dxo, wd, g, u)


def _ffn_fwd(x, h, wg, wu, wd, tag, after=None):
    G, U, A = _ffn_up(h, wg, wu, tag + "_up", after=after)
    xo = _mm(A, wd(A) if callable(wd) else wd, "nn", tag + "_down", alpha=0.5, res=x)
    return xo, (h, G, U, A)


def _ffn_bwd(x, gain, wg, wu, wd, saved, dxo, tag, send):
    h, G, U, A = saved
    dwd = _mm(A, dxo, "tn", tag + "_dwd", out_dtype=BF16, alpha=0.5, tm=1408)
    sent = send(tag + "_down", {tag + "_down": dwd})
    dG, dU = _ffn_dact(dxo, wd, G, U, tag + "_dact")
    dwu = _mm(h, dU, "tn", tag + "_dwu", out_dtype=BF16, after=sent)
    sent = send(tag + "_up", {tag + "_up": dwu})
    dwg = _mm(h, dG, "tn", tag + "_dwg", out_dtype=BF16, after=sent)
    sent = send(tag + "_gate", {tag + "_gate": dwg})
    dh = _mm2_nt(dG, wg, dU, wu, tag + "_dh", sent)
    dx, dgain, _ = _rms_bwd(x, gain, dh, dxo, tag + "_norm_bwd")
    return dx, dgain


def _segsum64_impl(x):
    r = lax.broadcasted_iota(jnp.int32, (128, 128), 0) // HEAD
    c = lax.broadcasted_iota(jnp.int32, (128, 128), 1) // HEAD
    ones = (r == c).astype(BF16)
    hi = x.astype(BF16)
    lo = (x - hi.astype(F32)).astype(BF16)
    outs = []
    for q in range(x.shape[1] // 128):
        sl = slice(q * 128, (q + 1) * 128)
        outs.append(jnp.dot(hi[:, sl], ones, preferred_element_type=F32) + jnp.dot(lo[:, sl], ones, preferred_element_type=F32))
    return outs[0] if len(outs) == 1 else jnp.concatenate(outs, axis=1)


@jax.custom_vjp
def _segsum64(x):
    return _segsum64_impl(x)


_segsum64.defvjp(lambda x: (_segsum64_impl(x), None), lambda _, ct: (_segsum64_impl(ct),))


def _swap32(x):
    lane = lax.broadcasted_iota(jnp.int32, (x.shape[0], 128), 1)
    outs = [jnp.take_along_axis(x[:, q * 128:(q + 1) * 128], lane ^ 32, axis=1) for q in range(x.shape[1] // 128)]
    return outs[0] if len(outs) == 1 else jnp.concatenate(outs, axis=1)


def _tree_sum(xs):
    xs = list(xs)
    while len(xs) > 1:
        nxt = [xs[i] + xs[i + 1] for i in range(0, len(xs) - 1, 2)]
        if len(xs) % 2:
            nxt.append(xs[-1])
        xs = nxt
    return xs[0]


class _Acc:
    def __init__(self, ways=4):
        self.parts = [None] * ways

    def add(self, i, term):
        k = i % len(self.parts)
        self.parts[k] = term if self.parts[k] is None else self.parts[k] + term

    def total(self):
        return _tree_sum([p for p in self.parts if p is not None])


def _softplus(x):
    return jnp.maximum(x, 0.0) + jnp.log(1.0 + jnp.exp(-jnp.abs(x)))


def _pre_core(k, da, gd, w0, a0, k_k, k_a, w_da, gate_up):
    lane = lax.broadcasted_iota(jnp.int32, da.shape, 1)
    w_da = w_da.astype(BF16)
    l1 = jnp.dot(jnp.where(lane < DECAY_LORA, jnp.tanh(da), 0.0).astype(BF16), w_da, preferred_element_type=F32)
    l2 = jnp.dot(jnp.where(lane >= DECAY_LORA, da, 0.0).astype(BF16), w_da, preferred_element_type=F32)
    wlog = -_softplus(-(w0 + l1)) - 0.5
    decay = jnp.exp(-jnp.exp(wlog))
    a = jax.nn.sigmoid(a0 + l2)
    g = jnp.dot(jax.nn.sigmoid(gd).astype(BF16), gate_up.astype(BF16), preferred_element_type=F32)
    kk = k * k_k
    kkn = kk / jnp.maximum(jnp.sqrt(_segsum64(kk * kk)), 1e-12)
    k2 = k * (1.0 + (a - 1.0) * k_a)
    return decay, k2, -kkn, kkn * a, g


def _pre_shift(i, zr, zl, zr8, zl8, mu, mul):
    live = (i > 0).astype(F32)
    dz = _shift_down(zr, zr8[7:8, :] * live) - zr
    dzl = _shift_down(zl, zl8[7:8, :] * live) - zl
    return zr + dz * mu, zl + dzl * mul, dz, dzl


def _shift_down(x, first_row):
    rolled = pltpu.roll(x, 1, 0)
    row = lax.broadcasted_iota(jnp.int32, x.shape, 0)
    return jnp.where(row == 0, first_row, rolled)


def _shift_up(x, last_row):
    rolled = pltpu.roll(x, x.shape[0] - 1, 0)
    row = lax.broadcasted_iota(jnp.int32, x.shape, 0)
    return jnp.where(row == x.shape[0] - 1, last_row, rolled)


def _prev_rows_spec(tm, cols):
    return lambda idx: pl.BlockSpec((8, cols), lambda i: (jnp.maximum(idx(i) * (tm // 8) - 1, 0), 0))


def _rwkv_pre(p_rkv, p_lora, params, tm=256):
    T = p_rkv.shape[0]

    def fn(i, zr, zl, zr8, zl8, mu, mul, *ps):
        z, z2, _, _ = _pre_shift(i, zr, zl, zr8, zl8, mu, mul)
        decay, k2, an, bn, g = _pre_core(z[:, RW_W:2 * RW_W], z2[:, :128], z2[:, 128:], *ps)
        return z[:, :RW_W], decay, k2, z[:, 2 * RW_W:], an, bn, g

    extra = [(p_rkv, _prev_rows_spec(tm, 3 * RW_W)), (p_lora, _prev_rows_spec(tm, LORA_W))]
    return _rows(fn, "rwkv_pre", T, tm, [p_rkv, p_lora], list(params), [(RW_W, F32)] * 7, [], extra=extra)


def _rwkv_pre_bwd(p_rkv, p_lora, params, cts, tm=256):
    T = p_rkv.shape[0]
    n = T // tm

    def fn(i, zr, zl, cr, cdec, ck2, cv, can, cbn, cg, cr_b, ck2_b, cv_b, zr8, zl8, mu, mul, *rest):
        ps, (car, carl) = rest[:-2], rest[-2:]
        cr, ck2, cv = cr + cr_b, ck2 + ck2_b, cv + cv_b
        z, z2, dif, difl = _pre_shift(i, zr, zl, zr8, zl8, mu, mul)
        _, vjp = jax.vjp(_pre_core, z[:, RW_W:2 * RW_W], z2[:, :128], z2[:, 128:], *ps)
        dk, dda, dgd, *dps = vjp((cdec, ck2, can, cbn, cg))
        dz = jnp.concatenate([cr, dk, cv], axis=1)
        dz2 = jnp.concatenate([dda, dgd], axis=1)
        dzp, dzlp = dz * mu, dz2 * mul

        @pl.when(i == n - 1)
        def _():
            car[...] = jnp.zeros_like(car)
            carl[...] = jnp.zeros_like(carl)

        d_rkv = dz - dzp + _shift_up(dzp, car[0:1, :])
        d_lora = dz2 - dzlp + _shift_up(dzlp, carl[0:1, :])
        car[0:1, :] = dzp[0:1, :]
        carl[0:1, :] = dzlp[0:1, :]
        return (d_rkv, d_lora, jnp.sum(dz * dif, axis=0, keepdims=True), jnp.sum(dz2 * difl, axis=0, keepdims=True), *dps)

    extra = [(p_rkv, _prev_rows_spec(tm, 3 * RW_W)), (p_lora, _prev_rows_spec(tm, LORA_W))]
    acc = [(p.shape, F32) for p in params]
    return _rows(fn, "rwkv_pre_bwd", T, tm, [p_rkv, p_lora, *cts], list(params), [(3 * RW_W, BF16), (LORA_W, BF16)], acc,
                 extra=extra, reverse=True, scratch=[pltpu.VMEM((8, 3 * RW_W), F32), pltpu.VMEM((8, LORA_W), F32)])


def _post_core(y, r, k2, v, g, lw, lb, rk):
    mu = _segsum64(y) * (1.0 / HEAD)
    yc = y - mu
    var = _segsum64(yc * yc) * (1.0 / HEAD)
    yn = yc * lax.rsqrt(var + GN_EPS) * lw + lb
    return (yn + _segsum64(r * k2 * rk) * v) * g


def _rwkv_post(y, r, k2, v, g, lw, lb, rk, tm=256):
    (o,) = _rows(lambda i, *a: (_post_core(*a),), "rwkv_post", y.shape[0], tm, [y, r, k2, v, g], [lw, lb, rk], [(RW_W, BF16)], [])
    return o


def _rwkv_post_bwd(y, r, k2, v, g, lw, lb, rk, do, tm=256):
    def fn(i, y, r, k2, v, g, do, lw, lb, rk):
        _, vjp = jax.vjp(_post_core, y, r, k2, v, g, lw, lb, rk)
        return vjp(do.astype(F32))

    return _rows(fn, "rwkv_post_bwd", y.shape[0], tm, [y, r, k2, v, g, do], [lw, lb, rk], [(RW_W, F32)] * 5, [((1, RW_W), F32)] * 3)


SCAN_L = 64


def _to_perm(x):
    T = x.shape[0]
    return x.reshape(T, RW_HEADS, HEAD).transpose(0, 2, 1).reshape(T, 8, 128)


def _from_perm(x):
    T = x.shape[0]
    return x.reshape(T, HEAD, RW_HEADS).transpose(0, 2, 1).reshape(T, RW_W)


def _as_tile(p):
    lane = lax.broadcasted_iota(jnp.int32, (8, 128), 1)
    return jnp.take_along_axis(p, (lane % 8) * 16 + lane // 8, axis=1)


def _as_perm(t):
    lane = lax.broadcasted_iota(jnp.int32, (8, 128), 1)
    return jnp.take_along_axis(t, (lane % 16) * 8 + lane // 16, axis=1)


def _tiles_to_perm(refs, L):
    for r in refs:
        for t in range(L):
            r[t] = _as_perm(r[t])


def _expander(srcs, tiles=()):
    s = lax.broadcasted_iota(jnp.int32, (8, 128), 0)
    lane = lax.broadcasted_iota(jnp.int32, (8, 128), 1)
    idx = 16 * s + lane // 8

    def expand(t, e_ref):
        for m, r in enumerate(srcs):
            for g in range(8):
                row = jnp.broadcast_to(r[t, pl.ds(g, 1), :], (8, 128))
                e_ref[m, g * 8:(g + 1) * 8, :] = jnp.take_along_axis(row, idx, axis=1)
        for k, r in enumerate(tiles):
            e_ref[len(srcs) + k, 0:8, :] = _as_tile(r[t])

    return expand


def _ck_a_to_b(ck):
    n = ck.shape[0]
    return ck.reshape(n, 8, 8, 8, RW_HEADS, 8).transpose(0, 2, 5, 1, 4, 3).reshape(n, HEAD, 8, 128)


def _bc(row):
    return jnp.broadcast_to(row, (8, 128))


def _rsum(x):
    return jnp.sum(x, axis=0, keepdims=True)


def _plus(acc, k, term):
    acc[k] = term if acc[k] is None else acc[k] + term


def _scan_fwd(xes, vi):
    T, L = vi.shape[0], SCAN_L
    nch = T // L

    def body(*refs):
        xr, (vi_ref, yi_ref, sa_ref, ck_ref, st_ref, e0, e1) = refs[:5], refs[5:]

        @pl.when(pl.program_id(0) == 0)
        def _():
            st_ref[...] = jnp.zeros_like(st_ref)

        ck_ref[0] = st_ref[...]
        expand = _expander(xr, [vi_ref])
        expand(0, e0)

        def step(t, e):
            tile = lambda m, jh: e[m, 8 * jh:8 * jh + 8, :]
            vb = [_bc(e[5, ih:ih + 1, :]) for ih in range(8)]
            acc = [None] * 8
            for jh in range(8):
                a = tile(0, jh)
                for ih in range(8):
                    _plus(acc, ih, st_ref[8 * jh + ih] * a)
            sab = []
            for ih in range(8):
                row = _rsum(acc[ih])
                sa_ref[t, ih:ih + 1, :] = row
                sab.append(_bc(row))
            yacc = [None] * 8
            for jh in range(8):
                w, B, k, r = tile(1, jh), tile(2, jh), tile(3, jh), tile(4, jh)
                for ih in range(8):
                    s = st_ref[8 * jh + ih] * w + B * sab[ih] + k * vb[ih]
                    st_ref[8 * jh + ih] = s
                    _plus(yacc, ih, s * r)
            for ih in range(8):
                yi_ref[t, ih:ih + 1, :] = _rsum(yacc[ih])

        def pair(p, carry):
            t = 2 * p
            expand(t + 1, e1)
            step(t, e0)
            expand(jnp.minimum(t + 2, L - 1), e0)
            step(t + 1, e1)
            return carry

        lax.fori_loop(0, L // 2, pair, 0)
        _tiles_to_perm([yi_ref, sa_ref], L)

    tile = pl.BlockSpec((L, 8, 128), lambda c: (c, 0, 0))
    return pl.pallas_call(
        body, name="rwkv_scan_fwd", grid=(nch,), in_specs=[tile] * 6,
        out_specs=[tile, tile, pl.BlockSpec((1, HEAD, 8, 128), lambda c: (c, 0, 0, 0))],
        out_shape=[jax.ShapeDtypeStruct((T, 8, 128), F32)] * 2 + [jax.ShapeDtypeStruct((nch, HEAD, 8, 128), F32)],
        scratch_shapes=[pltpu.VMEM((HEAD, 8, 128), F32)] + [pltpu.VMEM((6, HEAD, 128), F32)] * 2, compiler_params=_cp(("arbitrary",)),
    )(*xes, vi)


def _scan_bwd_a(xes, dyi):
    T, L = dyi.shape[0], SCAN_L
    nch = T // L

    def body(*refs):
        xr, (dy_ref, dsa_ref, dv_ref, g_ref, e0, e1) = refs[:5], refs[5:]

        @pl.when(pl.program_id(0) == 0)
        def _():
            g_ref[...] = jnp.zeros_like(g_ref)

        expand = _expander(xr, [dy_ref])
        expand(L - 1, e0)

        def step(t, e):
            tile = lambda m, jh: e[m, 8 * jh:8 * jh + 8, :]
            dyb = [_bc(e[5, ih:ih + 1, :]) for ih in range(8)]
            dsa, dv = [None] * 8, [None] * 8
            for jh in range(8):
                B, k, r = tile(2, jh), tile(3, jh), tile(4, jh)
                for ih in range(8):
                    g = g_ref[8 * jh + ih] + r * dyb[ih]
                    g_ref[8 * jh + ih] = g
                    _plus(dsa, ih, g * B)
                    _plus(dv, ih, g * k)
            dsab = []
            for ih in range(8):
                row = _rsum(dsa[ih])
                dsa_ref[t, ih:ih + 1, :] = row
                dsab.append(_bc(row))
                dv_ref[t, ih:ih + 1, :] = _rsum(dv[ih])
            for jh in range(8):
                A, w = tile(0, jh), tile(1, jh)
                for ih in range(8):
                    g_ref[8 * jh + ih] = g_ref[8 * jh + ih] * w + A * dsab[ih]

        def pair(p, carry):
            t = L - 1 - 2 * p
            expand(t - 1, e1)
            step(t, e0)
            expand(jnp.maximum(t - 2, 0), e0)
            step(t - 1, e1)
            return carry

        lax.fori_loop(0, L // 2, pair, 0)
        _tiles_to_perm([dsa_ref, dv_ref], L)

    tile = pl.BlockSpec((L, 8, 128), lambda c: (nch - 1 - c, 0, 0))
    return pl.pallas_call(
        body, name="rwkv_scan_bwd_a", grid=(nch,), in_specs=[tile] * 6, out_specs=[tile, tile],
        out_shape=[jax.ShapeDtypeStruct((T, 8, 128), F32)] * 2,
        scratch_shapes=[pltpu.VMEM((HEAD, 8, 128), F32)] + [pltpu.VMEM((6, HEAD, 128), F32)] * 2, compiler_params=_cp(("arbitrary",)),
    )(*xes, dyi)


def _scan_bwd_b(xts, ies, ckb):
    T, L = xts[0].shape[0], SCAN_L
    nch = T // L

    def body(*refs):
        xr, er, ck_ref, dj, (hist, g_ref, e0, e1) = refs[:5], refs[5:9], refs[9], refs[10:15], refs[15:]

        @pl.when(pl.program_id(0) == 0)
        def _():
            g_ref[...] = jnp.zeros_like(g_ref)

        hist[0] = ck_ref[0]
        expand_vs = _expander(er[:2], [xr[1], xr[2], xr[3]])
        expand = _expander(er, [xr[0], xr[1], xr[4]])
        expand_vs(0, e0)

        def fstep(t, e_ref):
            w, B, k = e_ref[2, 0:8, :], e_ref[3, 0:8, :], e_ref[4, 0:8, :]
            row = lambda m, i: jnp.broadcast_to(e_ref[m, pl.ds(i, 1), :], (8, 128))
            for i in range(HEAD):
                hist[t + 1, i] = hist[t, i] * w + row(1, i) * B + row(0, i) * k

        def fpair(p, carry):
            t = 2 * p
            expand_vs(t + 1, e1)
            fstep(t, e0)
            expand_vs(jnp.minimum(t + 2, L - 1), e0)
            fstep(t + 1, e1)
            return carry

        lax.fori_loop(0, L // 2, fpair, 0)
        expand(L - 1, e0)

        def bstep(t, e_ref):
            A, w, r = e_ref[4, 0:8, :], e_ref[5, 0:8, :], e_ref[6, 0:8, :]
            row = lambda m, i: jnp.broadcast_to(e_ref[m, pl.ds(i, 1), :], (8, 128))
            acc = [_Acc() for _ in range(5)]
            for i in range(HEAD):
                dy_i, dsa_i = row(2, i), row(3, i)
                g = g_ref[i] + dy_i * r
                sp = hist[t, i]
                acc[4].add(i, hist[t + 1, i] * dy_i)
                acc[1].add(i, g * sp)
                acc[2].add(i, g * row(1, i))
                acc[3].add(i, g * row(0, i))
                acc[0].add(i, sp * dsa_i)
                g_ref[i] = g * w + dsa_i * A
            for m in range(5):
                dj[m][t] = acc[m].total()

        def bpair(p, carry):
            t = L - 1 - 2 * p
            expand(t - 1, e1)
            bstep(t, e0)
            expand(jnp.maximum(t - 2, 0), e0)
            bstep(t - 1, e1)
            return carry

        lax.fori_loop(0, L // 2, bpair, 0)
        _tiles_to_perm(dj, L)

    tile = pl.BlockSpec((L, 8, 128), lambda c: (nch - 1 - c, 0, 0))
    return pl.pallas_call(
        body, name="rwkv_scan_bwd_b", grid=(nch,),
        in_specs=[tile] * 9 + [pl.BlockSpec((1, HEAD, 8, 128), lambda c: (nch - 1 - c, 0, 0, 0))],
        out_specs=[tile] * 5, out_shape=[jax.ShapeDtypeStruct((T, 8, 128), F32)] * 5,
        scratch_shapes=[pltpu.VMEM((L + 1, HEAD, 8, 128), F32), pltpu.VMEM((HEAD, 8, 128), F32)] + [pltpu.VMEM((7, HEAD, 128), F32)] * 2,
        compiler_params=_cp(("arbitrary",)),
    )(*xts, *ies, ckb)


SWA_COLS = SWA_W + 2 * KV_W
BLK = 128


def _swa_core(n, k2a, k2b, vla, vra, vlb, vrb, sinks, *qps):
    iq = lax.broadcasted_iota(jnp.int32, (BLK, 2 * BLK), 0)
    ik = lax.broadcasted_iota(jnp.int32, (BLK, 2 * BLK), 1)
    diff = BLK + iq - ik
    valid = (diff >= 0) & (diff < WINDOW) & ((n > 0) | (ik >= BLK))
    lane = lax.broadcasted_iota(jnp.int32, (BLK, 128), 1)
    lane1 = lax.broadcasted_iota(jnp.int32, (1, 128), 1)
    nt = (((1,), (1,)), ((), ()))
    outs = []
    for pp in range(8):
        k2, vl, vr = (k2a, vla, vra) if pp < 4 else (k2b, vlb, vrb)
        qp = qps[pp]
        o = None
        for half, vv in ((0, vl), (1, vr)):
            qh = jnp.where((lane >= HEAD) == (half == 1), qp, 0.0).astype(BF16)
            s = lax.dot_general(qh, k2.astype(BF16), nt, preferred_element_type=F32) * (HEAD ** -0.5)
            s = jnp.where(valid, s, NEG_INF)
            sink = jnp.sum(jnp.where(lane1 == 2 * pp + half, sinks, 0.0), axis=1, keepdims=True)
            m = jnp.maximum(jnp.max(s, axis=1, keepdims=True), sink)
            p = jnp.exp(s - m)
            den = jnp.sum(p, axis=1, keepdims=True) + jnp.exp(sink - m)
            oh = jnp.dot((p / den).astype(BF16), vv.astype(BF16), preferred_element_type=F32)
            o = oh if o is None else o + oh
        outs.append(o)
    return jnp.concatenate(outs, axis=1)


def _swa_prep(pc, pp, b, cq, sq, ckc, skc, ckp, skp):
    zc, zp = pc + b, pp + b
    qr = zc[:, :SWA_W] * cq + _swap32(zc[:, :SWA_W]) * sq
    kc, kp = zc[:, SWA_W:SWA_W + KV_W], zp[:, SWA_W:SWA_W + KV_W]
    kb = jnp.concatenate([kp * ckp + _swap32(kp) * skp, kc * ckc + _swap32(kc) * skc], axis=0)
    vb = jnp.concatenate([zp[:, SWA_W + KV_W:], zc[:, SWA_W + KV_W:]], axis=0)
    lane = lax.broadcasted_iota(jnp.int32, kb.shape, 1)
    left = lane < HEAD
    kbr, vbr = pltpu.roll(kb, HEAD, 1), pltpu.roll(vb, HEAD, 1)
    return (jnp.where(left, kb, kbr), jnp.where(left, kbr, kb), jnp.where(left, vb, 0.0), jnp.where(left, 0.0, vbr),
            jnp.where(left, vbr, 0.0), jnp.where(left, 0.0, vb)), [qr[:, q * 128:(q + 1) * 128] for q in range(8)]


def _swa_specs(T, tabs_q, tabs_k):
    cur = lambda c: pl.BlockSpec((BLK, c), lambda n: (n, 0))
    prev = lambda c: pl.BlockSpec((BLK, c), lambda n: (jnp.maximum(n - 1, 0), 0))
    return cur, prev


def _swa_fwd(p_swa, b, sinks, cq, sq, ck, sk):
    T = p_swa.shape[0]
    cur, prev = _swa_specs(T, None, None)

    def body(pc, pp, b_ref, s_ref, cq_r, sq_r, ckc, skc, ckp, skp, o_ref):
        ops, qps = _swa_prep(pc[...], pp[...], b_ref[...], cq_r[...], sq_r[...], ckc[...], skc[...], ckp[...], skp[...])
        o_ref[...] = _swa_core(pl.program_id(0), *ops, s_ref[...], *qps).astype(o_ref.dtype)

    full = lambda a: pl.BlockSpec(a.shape, lambda n: (0, 0))
    return pl.pallas_call(
        body, name="swa_fwd", grid=(T // BLK,),
        in_specs=[cur(SWA_COLS), prev(SWA_COLS), full(b), full(sinks), cur(SWA_W), cur(SWA_W), cur(KV_W), cur(KV_W), prev(KV_W), prev(KV_W)],
        out_specs=cur(SWA_W), out_shape=jax.ShapeDtypeStruct((T, SWA_W), BF16), compiler_params=_cp(("arbitrary",)),
    )(p_swa, p_swa, b, sinks, cq, sq, ck, sk, ck, sk)


def _swa_bwd(p_swa, b, sinks, cq, sq, ck, sk, do):
    T = p_swa.shape[0]
    nb = T // BLK
    cur = lambda c: pl.BlockSpec((BLK, c), lambda s: (nb - 1 - s, 0))
    prev = lambda c: pl.BlockSpec((BLK, c), lambda s: (jnp.maximum(nb - 2 - s, 0), 0))

    def body(pc, pp, b_ref, s_ref, cq_r, sq_r, ckc, skc, ckp, skp, do_ref, dcur, db, dsk, carry):
        step = pl.program_id(0)
        n = nb - 1 - step

        @pl.when(step == 0)
        def _():
            carry[...] = jnp.zeros_like(carry)
            db[...] = jnp.zeros_like(db)
            dsk[...] = jnp.zeros_like(dsk)

        ops, qps = _swa_prep(pc[...], pp[...], b_ref[...], cq_r[...], sq_r[...], ckc[...], skc[...], ckp[...], skp[...])
        _, vjp = jax.vjp(functools.partial(_swa_core, n), *ops, s_ref[...], *qps)
        dk2a, dk2b, dvla, dvra, dvlb, dvrb, dsinks, *dqps = vjp(do_ref[...].astype(F32))
        dqr = jnp.concatenate(dqps, axis=1)
        lane = lax.broadcasted_iota(jnp.int32, dk2a.shape, 1)
        left = lane < HEAD
        dkb = jnp.where(left, dk2a + pltpu.roll(dk2a, HEAD, 1), dk2b + pltpu.roll(dk2b, HEAD, 1))
        dvb = jnp.where(left, dvla + pltpu.roll(dvra, HEAD, 1), pltpu.roll(dvlb, HEAD, 1) + dvrb)
        dq = dqr * cq_r[...] + _swap32(dqr * sq_r[...])
        dkp, dkc = dkb[:BLK], dkb[BLK:]
        dkp = dkp * ckp[...] + _swap32(dkp * skp[...])
        dkc = dkc * ckc[...] + _swap32(dkc * skc[...])
        dc = jnp.concatenate([dq, jnp.concatenate([dkc, dvb[BLK:]], axis=1) + carry[...]], axis=1)
        carry[...] = jnp.concatenate([dkp, dvb[:BLK]], axis=1)
        dcur[...] = dc.astype(dcur.dtype)
        db[...] += jnp.sum(dc, axis=0, keepdims=True)
        dsk[...] += dsinks

    full = lambda a: pl.BlockSpec(a.shape, lambda s: (0, 0))
    return pl.pallas_call(
        body, name="swa_bwd", grid=(nb,),
        in_specs=[cur(SWA_COLS), prev(SWA_COLS), full(b), full(sinks), cur(SWA_W), cur(SWA_W), cur(KV_W), cur(KV_W), prev(KV_W), prev(KV_W),
                  cur(SWA_W)],
        out_specs=[cur(SWA_COLS), full(b), full(sinks)],
        out_shape=[jax.ShapeDtypeStruct((T, SWA_COLS), BF16), jax.ShapeDtypeStruct(b.shape, F32), jax.ShapeDtypeStruct(sinks.shape, F32)],
        scratch_shapes=[pltpu.VMEM((BLK, 2 * KV_W), F32)], compiler_params=_cp(("arbitrary",)),
    )(p_swa, p_swa, b, sinks, cq, sq, ck, sk, ck, sk, do)


def _rope_tables(T):
    inv = 10000.0 ** (-jnp.arange(0, HEAD, 2, dtype=F32) / HEAD)
    ang = jnp.arange(T, dtype=F32)[:, None] * inv[None, :]
    c = jnp.concatenate([jnp.cos(ang), jnp.cos(ang)], axis=1)
    s = jnp.concatenate([-jnp.sin(ang), jnp.sin(ang)], axis=1)
    return jnp.tile(c, (1, 16)), jnp.tile(s, (1, 16)), jnp.tile(c, (1, 2)), jnp.tile(s, (1, 2))


def _xattn_core(*qkv):
    outs = []
    for h in range(XH):
        qh, kh, vh = qkv[h], qkv[XH + h], qkv[2 * XH + h]
        s = lax.dot_general(qh.astype(BF16), kh.astype(BF16), (((1,), (1,)), ((), ())), preferred_element_type=F32) * (XHD ** -0.5)
        p = jnp.exp(s - jnp.max(s, axis=1, keepdims=True))
        p = p / jnp.sum(p, axis=1, keepdims=True)
        outs.append(jnp.dot(p.astype(BF16), vh.astype(BF16), preferred_element_type=F32))
    return jnp.concatenate(outs, axis=1)


def _xattn_split(q, kv):
    return [q[:, h * XHD:(h + 1) * XHD] for h in range(XH)] + [kv[:, h * XHD:(h + 1) * XHD] for h in range(2 * XH)]


def _xattn_fwd(q, kv, tm=256):
    (o,) = _rows(lambda i, q, kv: (_xattn_core(*_xattn_split(q, kv)),), "xattn_fwd", q.shape[0], tm, [q], [kv], [(q.shape[1], BF16)], [])
    return o


def _xattn_bwd(q, kv, do, tm=256):
    def fn(i, q, do, kv):
        _, vjp = jax.vjp(_xattn_core, *_xattn_split(q, kv))
        d = vjp(do.astype(F32))
        return jnp.concatenate(d[:XH], axis=1), jnp.concatenate(d[XH:], axis=1)

    return _rows(fn, "xattn_bwd", q.shape[0], tm, [q, do], [kv], [(q.shape[1], BF16)], [(kv.shape, F32)])


def _loss_head(x, g, tgt, tm=256):
    D = x.shape[1]

    def fn(i, x, tgt, g):
        y, vjp = jax.vjp(_rms, x, g)
        err = y - tgt
        dx, dg = vjp(err * (1.0 / D))
        part = 0.5 / D * jnp.sum(jnp.sum(err * err, axis=1, keepdims=True), axis=0, keepdims=True)
        return dx, jnp.broadcast_to(part, (1, 128)), dg

    return _rows(fn, "loss_head", x.shape[0], tm, [x, tgt], [g], [(D, F32)], [((1, 128), F32), ((1, D), F32)])


def _local_step(x, mem, tgt, get_w, P, put_g):
    T = x.shape[0]
    h1 = _rms_fwd(x, P["f1_norm"], "f1_norm")
    mn = _rms_fwd(mem, P["mem_norm"], "mem_norm")
    W = dict(get_w("f1", None))

    def f1_down(after):
        W.update(get_w("f1d", after))
        return W["f1_down"]

    x1, s1 = _ffn_fwd(x, h1, W["f1_gate"], W["f1_up"], f1_down, "f1")

    h2 = _rms_fwd(x1, P["mix_norm"], "mix_norm")
    W.update(get_w("mix", h2))
    w_rkv, w_lora, w_swa = W["w_inT"][:3 * RW_W], W["w_inT"][3 * RW_W:SHIFT_COLS], W["w_inT"][SHIFT_COLS:]
    p_rkv = _mm(h2, w_rkv, "nt", "in_rkv", after=W.get("_after"))
    p_lora = _mm(h2, w_lora, "nt", "in_lora")
    p_swa = _mm(h2, w_swa, "nt", "in_swa")
    w_da = jnp.concatenate([W["rw_decay_up"], W["rw_aaa_up"]], axis=0)
    pre_params = (P["rw_mu"][:, :3 * RW_W], P["rw_mu"][:, 3 * RW_W:], P["rw_w0"], P["rw_a0"], P["rw_k_k"], P["rw_k_a"], w_da,
                  W["rw_gate_up"])
    r, decay, k2, v, an, bn, g = _rwkv_pre(p_rkv, p_lora, pre_params)
    scan_vecs = (an, decay, bn, k2, r)
    xes = [_to_perm(a) for a in scan_vecs]
    v_p = _to_perm(v)
    yi, sai, ck = _scan_fwd(xes, v_p)
    y_scan = _from_perm(yi)
    y_rw = _rwkv_post(y_scan, r, k2, v, g, P["rw_lnx_w"], P["rw_lnx_b"], P["rw_r_k"])
    cq, sq, ckt, skt = _rope_tables(T)
    y_swa = _swa_fwd(p_swa, P["b_in_attn"], P["attn_sinks"], cq, sq, ckt, skt)
    ycat = jnp.concatenate([y_rw, y_swa], axis=1)
    W.update(get_w("out", ycat))
    x2 = _mm(ycat, W["w_out"], "nn", "out_proj", res=x1, bias=P["b_out"])

    hx = _rms_fwd(x2, P["xa_norm"], "xa_norm")
    W.update(get_w("xattn", hx))
    q = _mm(hx, W["w_xq"], "nn", "xq", out_dtype=BF16)
    kv = _mm(mn, W["w_xkv"], "nn", "xkv", out_dtype=BF16)
    o = _xattn_fwd(q, kv)
    x3 = _mm(o, W["w_xo"], "nn", "xo", res=x2)

    h3 = _rms_fwd(x3, P["f2_norm"], "f2_norm")
    W.update(get_w("f2", h3))
    x4, s2 = _ffn_fwd(x3, h3, W["f2_gate"], W["f2_up"], W["f2_down"], "f2")
    dx4, loss_part, d_final = _loss_head(x4, P["final_norm"], tgt)

    gs = {"final_norm": d_final}
    dx3, gs["f2_norm"] = _ffn_bwd(x3, P["f2_norm"], W["f2_gate"], W["f2_up"], W["f2_down"], s2, dx4, "f2", put_g)

    do = _mm(dx3, W["w_xo"], "nt", "xo_do", out_dtype=BF16)
    dw_xo = _mm(o, dx3, "tn", "xo_dw", out_dtype=BF16)
    dq, dkv = _xattn_bwd(q, kv, do)
    dw_xq = _mm(hx, dq, "tn", "xq_dw", out_dtype=BF16)
    dw_xkv = _mm(mn, dkv, "tn", "xkv_dw", out_dtype=BF16)
    sent = put_g("xattn", {"w_xq": dw_xq, "w_xkv": dw_xkv, "w_xo": dw_xo})
    dhx = _mm(dq, W["w_xq"], "nt", "xq_dh", after=sent)
    dmn = _mm(dkv, W["w_xkv"], "nt", "xkv_dmn")
    _, gs["mem_norm"], _ = _rms_bwd(mem, P["mem_norm"], dmn, jnp.zeros_like(mem), "mem_norm_bwd")
    dx2, gs["xa_norm"], gs["b_out"] = _rms_bwd(x2, P["xa_norm"], dhx, dx3, "xa_norm_bwd")

    dycat = _mm(dx2, W["w_out"], "nt", "out_dy")
    dw_out = _mm(ycat, dx2, "tn", "out_dw", out_dtype=BF16)
    dp_swa, gs["b_in_attn"], gs["attn_sinks"] = _swa_bwd(p_swa, P["b_in_attn"], P["attn_sinks"], cq, sq, ckt, skt, dycat[:, RW_W:])
    dy_scan, dr_b, dk2_b, dv_b, dg, gs["rw_lnx_w"], gs["rw_lnx_b"], gs["rw_r_k"] = _rwkv_post_bwd(
        y_scan, r, k2, v, g, P["rw_lnx_w"], P["rw_lnx_b"], P["rw_r_k"], dycat[:, :RW_W])
    dy_p = _to_perm(dy_scan)
    dsai, dvi = _scan_bwd_a(xes, dy_p)
    dj = _scan_bwd_b(xes, [v_p, sai, dy_p, dsai], _ck_a_to_b(ck))
    dan, ddecay, dbn, dk2_s, dr_s = (_from_perm(d) for d in dj)
    cts = (dr_s, ddecay, dk2_s, _from_perm(dvi), dan, dbn, dg, dr_b, dk2_b, dv_b)
    dp_rkv, dp_lora, dmu, dmul, gs["rw_w0"], gs["rw_a0"], gs["rw_k_k"], gs["rw_k_a"], dw_da, gs["rw_gate_up"] = _rwkv_pre_bwd(
        p_rkv, p_lora, pre_params, cts)
    gs["rw_mu"] = jnp.concatenate([dmu, dmul], axis=1)
    gs["rw_decay_up"], gs["rw_aaa_up"] = dw_da[:DECAY_LORA], dw_da[DECAY_LORA:]
    dw_inT = jnp.concatenate([_mm(dp_rkv, h2, "tn", "in_dw_rkv"), _mm(dp_lora, h2, "tn", "in_dw_lora"),
                              _mm(dp_swa, h2, "tn", "in_dw_swa")], axis=0)
    sent = put_g("mix", {"w_in": dw_inT, "w_out": dw_out})
    dh2 = _mm(dp_rkv, w_rkv, "nn", "in_dh_rkv", after=sent)
    dh2 = _mm(dp_lora, w_lora, "nn", "in_dh_lora", res=dh2)
    dh2 = _mm(dp_swa, w_swa, "nn", "in_dh_swa", res=dh2)
    dx1, gs["mix_norm"], _ = _rms_bwd(x1, P["mix_norm"], dh2, dx2, "mix_norm_bwd")

    dx0, gs["f1_norm"] = _ffn_bwd(x, P["f1_norm"], W["f1_gate"], W["f1_up"], W["f1_down"], s1, dx1, "f1", put_g)
    return loss_part, dx0, gs


_ANY = pl.BlockSpec(memory_space=pl.ANY)
_OTHER_CHIPS = ((1, 0), (0, 1), (1, 1))


def _mesh_pos():
    return lax.axis_index("x"), lax.axis_index("y"), lax.axis_index("c")


def _slot(ref, kind, s, rows, cols):
    if kind == "row":
        return ref.at[pl.ds(pl.multiple_of(s * rows, 8), rows), :]
    return ref.at[:, pl.ds(pl.multiple_of(s * cols, 128), cols)]


_HBM = pl.BlockSpec(memory_space=pltpu.HBM)
_SEMS = pl.BlockSpec(memory_space=pltpu.SEMAPHORE)
_SPLIT = dict(compiler_params=pltpu.CompilerParams(has_side_effects=pltpu.SideEffectType.DATAFLOW_SIDE_EFFECTING))


def _in_hbm(a):
    return pltpu.with_memory_space_constraint(a, pltpu.HBM)


def _full_shape(s, kind):
    return (4 * s.shape[0], s.shape[1]) if kind == "row" else (s.shape[0], 4 * s.shape[1])


def _half(ref, shape, h):
    rows, cols = shape
    if rows % 32 == 0:
        return ref.at[pl.ds(pl.multiple_of(h * (rows // 2), 16), rows // 2), :]
    assert cols % 256 == 0, shape
    return ref.at[:, pl.ds(pl.multiple_of(h * (cols // 2), 128), cols // 2)]


def _half_shape(shape):
    rows, cols = shape
    return (rows // 2, cols) if rows % 32 == 0 else (rows, cols // 2)


def _streams(src, dst, shape, c):
    hs = _half_shape(shape)
    s, d = _half(src, shape, c), _half(dst, shape, c)
    return [(_half(s, hs, q), _half(d, hs, q)) for q in range(2)]


def _swap_halves(name, fulls, shard_shapes, kinds):
    n = len(fulls)

    def body(*refs):
        out, send, recv = refs[n:2 * n], refs[2 * n], refs[2 * n + 1]
        x, y, c = _mesh_pos()
        sent = []
        for i in range(n):
            for r, (dx, dy) in enumerate(_OTHER_CHIPS):
                theirs = _slot(out[i], kinds[i], 2 * ((x + dx) % 2) + (y + dy) % 2, *shard_shapes[i])
                have = _half(theirs, shard_shapes[i], c)
                rc = pltpu.make_async_remote_copy(have, have, send.at[3 * i + r], recv.at[3 * i + r], device_id=(x, y, 1 - c),
                                                  device_id_type=MESH)
                rc.start()
                sent.append(rc)
        for i in range(n):
            for r, (dx, dy) in enumerate(_OTHER_CHIPS):
                theirs = _slot(out[i], kinds[i], 2 * ((x + dx) % 2) + (y + dy) % 2, *shard_shapes[i])
                need = _half(theirs, shard_shapes[i], 1 - c)
                pltpu.make_async_remote_copy(need, need, send.at[3 * i + r], recv.at[3 * i + r], device_id=(x, y, c),
                                             device_id_type=MESH).wait_recv()
        for rc in sent:
            rc.wait_send()

    return pl.pallas_call(
        body, name=name, in_specs=[_ANY] * n, out_specs=[_ANY] * n, out_shape=[jax.ShapeDtypeStruct(f.shape, f.dtype) for f in fulls],
        input_output_aliases={i: i for i in range(n)},
        scratch_shapes=[pltpu.SemaphoreType.DMA((3 * n,)), pltpu.SemaphoreType.DMA((3 * n,))],
    )(*fulls)


def _gather_start(name, shards, kinds, groups, after=None):
    n, ng = len(shards), len(groups)
    lands = [_in_hbm(lax.empty(_full_shape(s, k), s.dtype)) for s, k in zip(shards, kinds)]
    n_in = 2 * n + (after is not None)

    def body(*refs):
        src, land, sems, token = refs[:n], refs[n:2 * n], refs[n_in:n_in + 3 * ng], refs[-1]
        x, y, c = _mesh_pos()
        me = 2 * x + y
        for gi, idxs in enumerate(groups):
            send, recv, own = sems[3 * gi:3 * gi + 3]
            for k, i in enumerate(idxs):
                mine = _slot(land[i], kinds[i], me, *src[i].shape)
                for r, (dx, dy) in enumerate(_OTHER_CHIPS):
                    for q, (s, d) in enumerate(_streams(src[i], mine, src[i].shape, c)):
                        pltpu.make_async_remote_copy(s, d, send.at[6 * k + 2 * r + q], recv.at[6 * k + 2 * r + q],
                                                     device_id=((x + dx) % 2, (y + dy) % 2, c), device_id_type=MESH).start()
                pltpu.make_async_copy(src[i], mine, own.at[k]).start()
        token[...] = jnp.zeros_like(token)

    sem_shapes = [pltpu.SemaphoreType.DMA((w * len(g),)) for g in groups for w in (6, 6, 1)]
    thru = [pltpu.HBM(a.shape, a.dtype) for a in (*shards, *lands)]
    res = pl.pallas_call(
        body, name=name, in_specs=[_HBM] * (2 * n) + [_ANY] * (after is not None),
        out_specs=[_SEMS] * (3 * ng) + [_HBM] * (2 * n) + [pl.BlockSpec(memory_space=pltpu.VMEM)],
        out_shape=sem_shapes + thru + [jax.ShapeDtypeStruct((8, 128), F32)],
        input_output_aliases={i: 3 * ng + i for i in range(2 * n)}, **_SPLIT,
    )(*[_in_hbm(s) for s in shards], *lands, *([] if after is None else [after]))
    return res[:3 * ng], res[3 * ng:3 * ng + n], res[3 * ng + n:3 * ng + 2 * n], res[-1]


def _gather_wait(name, sems, shards, lands, kinds, after):
    m = len(shards)

    def body(*refs):
        src, land, (send, recv, own) = refs[:m], refs[m:2 * m], refs[2 * m:2 * m + 3]
        x, y, c = _mesh_pos()
        me = 2 * x + y
        for k in range(m):
            mine = _slot(land[k], kinds[k], me, *src[k].shape)
            for r in range(3):
                for q, (s, d) in enumerate(_streams(src[k], mine, src[k].shape, c)):
                    cp = pltpu.make_async_remote_copy(s, d, send.at[6 * k + 2 * r + q], recv.at[6 * k + 2 * r + q], device_id=(x, y, c),
                                                      device_id_type=MESH)
                    cp.wait_send()
                    cp.wait_recv()
            pltpu.make_async_copy(src[k], mine, own.at[k]).wait()

    thru = [pltpu.HBM(a.shape, a.dtype) for a in (*shards, *lands)]
    res = pl.pallas_call(
        body, name=name, in_specs=[_HBM] * (2 * m) + [_SEMS] * 3 + [pl.BlockSpec(memory_space=pl.ANY)],
        out_specs=[_HBM] * (2 * m), out_shape=thru, input_output_aliases={i: i for i in range(2 * m)}, **_SPLIT,
    )(*shards, *lands, *sems, after)
    return res[m:]


def _scatter_start(name, grads, kinds):
    m = len(grads)
    shard_shape = [(g.shape[0] // 4, g.shape[1]) if k == "row" else (g.shape[0], g.shape[1] // 4) for g, k in zip(grads, kinds)]
    lands = [_in_hbm(lax.empty((4, *s), g.dtype)) for s, g in zip(shard_shape, grads)]

    def body(*refs):
        src, land, (send, recv, own) = refs[:m], refs[m:2 * m], refs[2 * m:2 * m + 3]
        x, y, c = _mesh_pos()
        me = 2 * x + y
        for k in range(m):
            for r, (dx, dy) in enumerate(_OTHER_CHIPS):
                tx, ty = (x + dx) % 2, (y + dy) % 2
                pltpu.make_async_remote_copy(_slot(src[k], kinds[k], 2 * tx + ty, *shard_shape[k]), land[k].at[me],
                                             send.at[3 * k + r], recv.at[3 * k + r], device_id=(tx, ty, c), device_id_type=MESH).start()
            pltpu.make_async_copy(_slot(src[k], kinds[k], me, *shard_shape[k]), land[k].at[me], own.at[k]).start()
        refs[-1][...] = jnp.zeros_like(refs[-1])

    thru = [pltpu.HBM(a.shape, a.dtype) for a in (*grads, *lands)]
    res = pl.pallas_call(
        body, name=name, in_specs=[_HBM] * (2 * m),
        out_specs=[_SEMS] * 3 + [_HBM] * (2 * m) + [pl.BlockSpec(memory_space=pltpu.VMEM)],
        out_shape=[pltpu.SemaphoreType.DMA((3 * m,))] * 2 + [pltpu.SemaphoreType.DMA((m,))] + thru + [jax.ShapeDtypeStruct((8, 128), F32)],
        input_output_aliases={i: 3 + i for i in range(2 * m)}, **_SPLIT,
    )(*[_in_hbm(g) for g in grads], *lands)
    return res[:3], res[3:3 + m], res[3 + m:3 + 2 * m], res[-1]


def _scatter_wait(name, sems, grads, lands, kinds, after):
    m = len(grads)

    def body(*refs):
        src, land, (send, recv, own) = refs[:m], refs[m:2 * m], refs[2 * m:2 * m + 3]
        x, y, c = _mesh_pos()
        me = 2 * x + y
        for k in range(m):
            mine = _slot(src[k], kinds[k], me, *land[k].shape[1:])
            for r in range(3):
                cp = pltpu.make_async_remote_copy(mine, land[k].at[me], send.at[3 * k + r], recv.at[3 * k + r],
                                                  device_id=(x, y, c), device_id_type=MESH)
                cp.wait_send()
                cp.wait_recv()
            pltpu.make_async_copy(mine, land[k].at[me], own.at[k]).wait()

    thru = [pltpu.HBM(a.shape, a.dtype) for a in (*grads, *lands)]
    res = pl.pallas_call(
        body, name=name, in_specs=[_HBM] * (2 * m) + [_SEMS] * 3 + [pl.BlockSpec(memory_space=pl.ANY)],
        out_specs=[_HBM] * (2 * m), out_shape=thru, input_output_aliases={i: i for i in range(2 * m)}, **_SPLIT,
    )(*grads, *lands, *sems, after)
    return res[m:]


def _swap_with_sibling(arrs, name):
    n = len(arrs)

    def body(*refs):
        ins, outs = refs[:n], refs[n:2 * n]
        send, recv = refs[2 * n:]
        x, y, c = _mesh_pos()
        copies = []
        for i in range(n):
            rc = pltpu.make_async_remote_copy(ins[i], outs[i], send.at[i], recv.at[i], device_id=(x, y, 1 - c), device_id_type=MESH)
            rc.start()
            copies.append(rc)
        for rc in copies:
            rc.wait()

    return pl.pallas_call(
        body, name=name, in_specs=[_ANY] * n, out_specs=[_ANY] * n,
        out_shape=[jax.ShapeDtypeStruct(a.shape, a.dtype) for a in arrs],
        scratch_shapes=[pltpu.SemaphoreType.DMA((n,)), pltpu.SemaphoreType.DMA((n,))],
    )(*arrs)


def _small_start(pack, after):
    land = _in_hbm(lax.empty((8, *pack.shape), pack.dtype))

    def body(in_ref, land_ref, after_ref, send, recv, own, in_thru, land_thru, token):
        x, y, c = _mesh_pos()
        me = 4 * x + 2 * y + c
        for r in range(1, 8):
            dx, dy, dc = r // 4, (r // 2) % 2, r % 2
            pltpu.make_async_remote_copy(in_ref, land_ref.at[me], send.at[r - 1], recv.at[r - 1],
                                         device_id=((x + dx) % 2, (y + dy) % 2, (c + dc) % 2), device_id_type=MESH).start()
        pltpu.make_async_copy(in_ref, land_ref.at[me], own.at[0]).start()
        token[...] = jnp.zeros_like(token)

    res = pl.pallas_call(
        body, name="small_start", in_specs=[_HBM, _HBM, _ANY],
        out_specs=[_SEMS] * 3 + [_HBM, _HBM, pl.BlockSpec(memory_space=pltpu.VMEM)],
        out_shape=[pltpu.SemaphoreType.DMA((7,)), pltpu.SemaphoreType.DMA((7,)), pltpu.SemaphoreType.DMA((1,)),
                   pltpu.HBM(pack.shape, pack.dtype), pltpu.HBM(land.shape, land.dtype), jax.ShapeDtypeStruct((8, 128), F32)],
        input_output_aliases={0: 3, 1: 4}, **_SPLIT,
    )(_in_hbm(pack), land, after)
    return res[:3], res[3], res[4], res[5]


def _small_wait(sems, pack, land, after):
    def body(in_ref, land_ref, send, recv, own, after_ref, in_dead, got):
        x, y, c = _mesh_pos()
        me = 4 * x + 2 * y + c
        for r in range(1, 8):
            cp = pltpu.make_async_remote_copy(in_ref, land_ref.at[me], send.at[r - 1], recv.at[r - 1], device_id=(x, y, c),
                                              device_id_type=MESH)
            cp.wait_send()
            cp.wait_recv()
        pltpu.make_async_copy(in_ref, land_ref.at[me], own.at[0]).wait()

    res = pl.pallas_call(
        body, name="small_wait", in_specs=[_HBM, _HBM] + [_SEMS] * 3 + [_ANY], out_specs=[_HBM, _HBM],
        out_shape=[pltpu.HBM(pack.shape, pack.dtype), pltpu.HBM(land.shape, land.dtype)], input_output_aliases={0: 0, 1: 1}, **_SPLIT,
    )(pack, land, *sems, after)
    return res[1]


def _row_tile(R, dtype, target=256):
    mult = 8 * 4 // jnp.dtype(dtype).itemsize
    best = R
    for t in range(mult, min(R, target) + 1, mult):
        if R % t == 0:
            best = t
    return best


def _sum_slots(stack, name, out_dtype=F32):
    k, R, C = stack.shape
    tr = _row_tile(R, stack.dtype)
    tc = C
    if tr < 64:
        tr, tc = R, _pick(C, 512)

    def body(s_ref, o_ref):
        acc = s_ref[0].astype(F32)
        for j in range(1, k):
            acc = acc + s_ref[j].astype(F32)
        o_ref[...] = acc.astype(out_dtype)

    return pl.pallas_call(
        body, name=name, grid=(R // tr, C // tc), in_specs=[pl.BlockSpec((k, tr, tc), lambda i, j: (0, i, j))],
        out_specs=pl.BlockSpec((tr, tc), lambda i, j: (i, j)), out_shape=jax.ShapeDtypeStruct((R, C), out_dtype),
        compiler_params=_cp(("parallel", "parallel")),
    )(stack)


W_IN_SHARD = 1160
W_IN_PAD = 1168


def _pad_shards(a):
    zeros = jnp.zeros((W_IN_PAD - W_IN_SHARD, a.shape[1]), a.dtype)
    parts = []
    for s in range(a.shape[0] // W_IN_SHARD):
        parts += [a[s * W_IN_SHARD:(s + 1) * W_IN_SHARD], zeros]
    return jnp.concatenate(parts, axis=0).astype(BF16)


def _unpad_shards(a):
    a = a.astype(F32)
    return jnp.concatenate([a[s * W_IN_PAD:s * W_IN_PAD + W_IN_SHARD] for s in range(a.shape[0] // W_IN_PAD)], axis=0)


def _adamw(w, m, v, ga, gb, name, after=None):
    R, C = w.shape
    tr = _row_tile(R, F32, 128)
    gs = [ga] if gb is None else [ga, gb]
    extra = [] if after is None else [after]

    def body(*refs):
        w_ref, m_ref, v_ref = refs[:3]
        g = refs[3][...].astype(F32)
        if gb is not None:
            g = g + refs[4][...].astype(F32)
        g_ref, d_ref, nm_ref, nv_ref = refs[-4:]
        nm = ADAM_B1 * m_ref[...] + (1.0 - ADAM_B1) * g
        nv = ADAM_B2 * v_ref[...] + (1.0 - ADAM_B2) * (g * g)
        m_hat = nm / (1.0 - ADAM_B1 ** ADAM_STEP)
        v_hat = nv / (1.0 - ADAM_B2 ** ADAM_STEP)
        g_ref[...] = g
        d_ref[...] = -ADAM_LR * (m_hat / (jnp.sqrt(v_hat) + ADAM_EPS) + ADAM_WD * w_ref[...])
        nm_ref[...] = nm
        nv_ref[...] = nv

    spec = pl.BlockSpec((tr, C), lambda i: (i, 0))
    return pl.pallas_call(
        body, name=name, grid=(R // tr,), in_specs=[spec] * (3 + len(gs)) + [_ANY] * len(extra), out_specs=[spec] * 4,
        out_shape=[jax.ShapeDtypeStruct((R, C), F32)] * 4, compiler_params=_cp(("parallel",)),
    )(w, m, v, *gs, *extra)


def _pack(arrs):
    rows = []
    for a in arrs:
        flat = a.reshape(-1)
        rows.append(jnp.pad(flat, (0, -flat.shape[0] % 1024)).reshape(-1, 1024))
    p = jnp.concatenate(rows, axis=0)
    return jnp.pad(p, ((0, -p.shape[0] % 8), (0, 0)))


def _unpack(p, shapes):
    out, r = [], 0
    for s in shapes:
        n = 1
        for d in s:
            n *= d
        nr = -(-n // 1024)
        out.append(p[r:r + nr].reshape(-1)[:n].reshape(s))
        r += nr
    return out


BIG = ("f1_gate", "f1_up", "f1_down", "w_in", "w_out", "w_xq", "w_xkv", "w_xo", "f2_gate", "f2_up", "f2_down")
BIG_KIND = {"f1_gate": "col", "f1_up": "col", "f1_down": "row", "w_in": "row", "w_out": "row", "w_xq": "row", "w_xkv": "col",
            "w_xo": "row", "f2_gate": "col", "f2_up": "col", "f2_down": "row"}
LORA = ("rw_decay_up", "rw_aaa_up", "rw_gate_up")
WEIGHTS = ("f1_norm", "f1_gate", "f1_up", "f1_down", "mix_norm", "w_in", "b_in_attn", "rw_mu", "rw_w0", "rw_decay_up", "rw_a0",
           "rw_aaa_up", "rw_gate_up", "rw_k_k", "rw_k_a", "rw_r_k", "rw_lnx_w", "rw_lnx_b", "attn_sinks", "w_out", "b_out", "xa_norm",
           "mem_norm", "w_xq", "w_xkv", "w_xo", "f2_norm", "f2_gate", "f2_up", "f2_down", "final_norm")
SMALL = tuple(n for n in WEIGHTS if n not in BIG)
GROUP_ORDER = ("f1", "f1d", "mix", "out", "xattn", "f2")
GROUPS = {"f1": ("f1_gate", "f1_up"), "f1d": ("f1_down",), "mix": ("w_in",) + LORA, "out": ("w_out",), "xattn": ("w_xq", "w_xkv", "w_xo"),
          "f2": ("f2_gate", "f2_up", "f2_down")}


def kernel(x, mem, f1_norm, f1_gate, f1_up, f1_down, mix_norm, w_in, b_in_attn, rw_mu, rw_w0, rw_decay_up, rw_a0, rw_aaa_up, rw_gate_up, rw_k_k, rw_k_a, rw_r_k, rw_lnx_w, rw_lnx_b, attn_sinks, w_out, b_out, xa_norm, mem_norm, w_xq, w_xkv, w_xo, f2_norm, f2_gate, f2_up, f2_down, final_norm, loss_target, m_f1_norm, m_f1_gate, m_f1_up, m_f1_down, m_mix_norm, m_w_in, m_b_in_attn, m_rw_mu, m_rw_w0, m_rw_decay_up, m_rw_a0, m_rw_aaa_up, m_rw_gate_up, m_rw_k_k, m_rw_k_a, m_rw_r_k, m_rw_lnx_w, m_rw_lnx_b, m_attn_sinks, m_w_out, m_b_out, m_xa_norm, m_mem_norm, m_w_xq, m_w_xkv, m_w_xo, m_f2_norm, m_f2_gate, m_f2_up, m_f2_down, m_final_norm, v_f1_norm, v_f1_gate, v_f1_up, v_f1_down, v_mix_norm, v_w_in, v_b_in_attn, v_rw_mu, v_rw_w0, v_rw_decay_up, v_rw_a0, v_rw_aaa_up, v_rw_gate_up, v_rw_k_k, v_rw_k_a, v_rw_r_k, v_rw_lnx_w, v_rw_lnx_b, v_attn_sinks, v_w_out, v_b_out, v_xa_norm, v_mem_norm, v_w_xq, v_w_xkv, v_w_xo, v_f2_norm, v_f2_gate, v_f2_up, v_f2_down, v_final_norm):
    a = dict(locals())
    w = {n: a[n] for n in WEIGHTS}
    m = {n: a["m_" + n] for n in WEIGHTS}
    v = {n: a["v_" + n] for n in WEIGHTS}
    sq = lambda t: t.reshape(t.shape[-2:]) if t.ndim == 3 else t.reshape(1, -1)

    local_name = lambda n: "w_inT" if n == "w_in" else n
    kind_of = lambda n: BIG_KIND.get(n, "col")
    payload = lambda n: _pad_shards(sq(w[n]).T) if n == "w_in" else sq(w[n]) if n in LORA else sq(w[n]).astype(BF16)
    gathers = {}

    def start_gather(name, grps, after):
        shards = [payload(n) for g in grps for n in GROUPS[g]]
        kinds = [kind_of(n) for g in grps for n in GROUPS[g]]
        groups, at = [], 0
        for g in grps:
            groups.append(list(range(at, at + len(GROUPS[g]))))
            at += len(GROUPS[g])
        sems, src_thru, land_thru, token = _gather_start(name, shards, kinds, groups, after)
        for gi, g in enumerate(grps):
            gathers[g] = (sems[3 * gi:3 * gi + 3], [src_thru[i] for i in groups[gi]], [land_thru[i] for i in groups[gi]],
                          [kinds[i] for i in groups[gi]], token)

    early = GROUP_ORDER[:3]
    start_gather("gather_start_f1", early[:1], None)
    start_gather("gather_start", early[1:], gathers[early[0]][4])
    gathers[early[0]] = gathers[early[0]][:4] + (gathers[early[1]][4],)

    def get_w(grp, after):
        g_sems, g_src, g_land, g_kinds, token = gathers[grp]
        got = _gather_wait("gather_wait_" + grp, g_sems, g_src, g_land, g_kinds, token if after is None else after)
        got = _swap_halves("gather_swap_" + grp, got, [s.shape for s in g_src], g_kinds)
        out = {local_name(n): (_unpad_shards(f) if n == "w_in" else f) for n, f in zip(GROUPS[grp], got)}
        if grp == early[-1]:
            start_gather("gather_start_late", GROUP_ORDER[3:], got[0])
            out["_after"] = gathers[GROUP_ORDER[3]][4]
        return out

    in_flight = []

    def put_g(label, gw):
        names = list(gw)
        grads = [_pad_shards(gw[n]) if n == "w_in" else gw[n] for n in names]
        *flight, sent = _scatter_start("scatter_start_" + label, grads, [kind_of(n) for n in names])
        in_flight.append((label, names, flight))
        return sent

    P = {n: sq(w[n]) for n in SMALL if n not in LORA}
    P["attn_sinks"] = jnp.pad(P["attn_sinks"], ((0, 0), (0, 128 - P["attn_sinks"].shape[1])))
    P["rw_r_k"] = w["rw_r_k"].reshape(1, RW_W)
    loss_part, grad_x, gs = _local_step(x[0], mem[0], loss_target[0], get_w, P, put_g)

    gs["attn_sinks"] = gs["attn_sinks"][:, :16]
    small_flight = _small_start(_pack([gs[n] for n in SMALL] + [loss_part]), grad_x)

    out, after = {}, small_flight[-1]
    for bi, batch in enumerate((in_flight[:-3], in_flight[-3:])):
        b_names, b_partial = [], []
        for label, names, (g_sems, g_thru, l_thru) in batch:
            stacks = _scatter_wait("scatter_wait_" + label, g_sems, g_thru, l_thru, [kind_of(n) for n in names], after)
            partial = [_sum_slots(s, "sum_chips_" + n, F32 if n == "w_in" else BF16) for s, n in zip(stacks, names)]
            b_names += names
            b_partial += partial
            after = partial[-1]
        sibling = _swap_with_sibling(b_partial, "swap_batch%d" % bi)
        chain = None
        for n, pa, sb in zip(b_names, b_partial, sibling):
            if n == "w_in":
                pa, sb = pa[:W_IN_SHARD].T, sb[:W_IN_SHARD].T
            out[n] = _adamw(sq(w[n]), sq(m[n]), sq(v[n]), pa, sb, "adamw_" + n, after=chain)
            chain = out[n][1]
        after = chain

    gsum = _sum_slots(_small_wait(*small_flight[:-1], after), "sum_small")
    *summed, loss_row = _unpack(gsum, [gs[n].shape for n in SMALL] + [loss_part.shape])
    g_small = dict(zip(SMALL, summed))
    loss = loss_row[0, 0]
    shard = 2 * lax.axis_index("x") + lax.axis_index("y")
    for n in LORA:
        cols = w[n].shape[-1]
        g_small[n] = lax.dynamic_slice_in_dim(g_small[n], shard * cols, cols, axis=1)

    flat = lambda d: _pack([d[n] for n in SMALL])
    res = _adamw(flat(w), flat(m), flat(v), _pack([g_small[n] for n in SMALL]), None, "adamw_small")
    shapes = [w[n].shape for n in SMALL]
    for k, p in enumerate(res):
        for n, t in zip(SMALL, _unpack(p, shapes)):
            out.setdefault(n, [None] * 4)[k] = t
    outs = [loss, grad_x.reshape(x.shape)]
    for k in range(4):
        outs += [out[n][k].reshape(w[n].shape) for n in WEIGHTS]
    return tuple(outs)
```
